```python
import functools
import math
import jax
import jax.numpy as jnp
from jax import lax
import numpy as np

D_MODEL = 2048
BATCH = 2
SEQ = 4096
DEPTH = 2
DEC_BATCH = 32
DEC_SEQ = 4
PAST_LEN = 16384
PAGE_SIZE = 128

MIX_WIDTH = D_MODEL
ATTN_WIDTH = MIX_WIDTH // 2
SSM_WIDTH = MIX_WIDTH - ATTN_WIDTH
HEAD_DIM = 64
N_HEADS = ATTN_WIDTH // HEAD_DIM
N_KV_HEADS = 4
Q_PER_KV = N_HEADS // N_KV_HEADS
KV_WIDTH = N_KV_HEADS * HEAD_DIM
WINDOW = 128
ROPE_THETA = 10000.0
SSM_GROUP = 16
SSM_GROUPS = SSM_WIDTH // SSM_GROUP
SSM_STATE = 64
DT_MIN = 0.001
DT_MAX = 0.1
IN_WIDTH = ATTN_WIDTH + 2 * KV_WIDTH + SSM_WIDTH
N_EXPERT_GROUPS = 4
EXPERTS_PER_GROUP = 8
N_EXPERTS = N_EXPERT_GROUPS * EXPERTS_PER_GROUP
TOP_K_INNER = 2
EXPERT_FF = D_MODEL // 4
MOE_BLOCK = 128
DEEPNORM_ALPHA = (2.0 * DEPTH) ** 0.25
DEEPNORM_BETA = (8.0 * DEPTH) ** -0.25
LN_EPS = 1e-5
RMS_EPS = 1e-6

kernel_name = 'hymba_s5_swa_sink_hmoe_deepnorm_step'


def _layer_norm(x, g, b):
    xf = x.astype(jnp.float32)
    mu = jnp.mean(xf, -1, keepdims=True)
    var = jnp.mean(jnp.square(xf - mu), -1, keepdims=True)
    return ((xf - mu) * lax.rsqrt(var + LN_EPS) * g.astype(jnp.float32) + b.astype(jnp.float32)).astype(x.dtype)


def _rms_norm(x, g):
    xf = x.astype(jnp.float32)
    inv = lax.rsqrt(jnp.mean(jnp.square(xf), -1, keepdims=True) + RMS_EPS)
    return (xf * inv * g.astype(jnp.float32)).astype(x.dtype)


def _rope(x, pos):
    half = HEAD_DIM // 2
    inv_freq = ROPE_THETA ** (-jnp.arange(half, dtype=jnp.float32) / half)
    ang = pos.astype(jnp.float32)[:, None] * inv_freq[None, :]
    cos = jnp.cos(ang)[:, None, :]
    sin = jnp.sin(ang)[:, None, :]
    xf = x.astype(jnp.float32)
    x1 = xf[..., :half]
    x2 = xf[..., half:]
    return jnp.concatenate([x1 * cos - x2 * sin, x2 * cos + x1 * sin], -1).astype(x.dtype)


def _window_attention(q, k, v, q_pos, k_pos, sinks):
    lead = q.shape[:-3]
    tq = q.shape[-3]
    qg = q.reshape(lead + (tq, N_KV_HEADS, Q_PER_KV, HEAD_DIM)).astype(jnp.float32)
    s = jnp.einsum('...qkgd,...skd->...kgqs', qg, k.astype(jnp.float32)) * (HEAD_DIM ** -0.5)
    rel = q_pos[..., :, None] - k_pos[..., None, :]
    visible = (rel >= 0) & (rel < WINDOW) & (k_pos[..., None, :] >= 0)
    s = jnp.where(visible[..., None, None, :, :], s, -jnp.inf)
    sink = jnp.broadcast_to(sinks.astype(jnp.float32).reshape(N_KV_HEADS, Q_PER_KV, 1, 1), s.shape[:-1] + (1,))
    p = jax.nn.softmax(jnp.concatenate([s, sink], -1), axis=-1)[..., :-1]
    o = jnp.einsum('...kgqs,...skd->...qkgd', p, v.astype(jnp.float32))
    return o.reshape(lead + (tq, N_HEADS, HEAD_DIM)).astype(q.dtype)


def _prompt_attention(q, k, v, sinks):
    n, t = q.shape[0], q.shape[1]
    nb = t // WINDOW
    qb = q.reshape(n, nb, WINDOW, N_HEADS, HEAD_DIM)

    def band(a):
        pad = jnp.zeros((n, WINDOW) + a.shape[2:], a.dtype)
        ap = jnp.concatenate([pad, a], 1).reshape((n, nb + 1, WINDOW) + a.shape[2:])
        return jnp.concatenate([ap[:, :-1], ap[:, 1:]], 2)

    q_pos = jnp.arange(t, dtype=jnp.int32).reshape(nb, WINDOW)
    kp = jnp.arange(-WINDOW, t, dtype=jnp.int32).reshape(nb + 1, WINDOW)
    k_pos = jnp.concatenate([kp[:-1], kp[1:]], 1)
    o = _window_attention(qb, band(k), band(v), q_pos, k_pos, sinks).reshape(n, t, N_HEADS, HEAD_DIM)
    wbuf = min(WINDOW, t)
    return o, k[:, t - wbuf:], v[:, t - wbuf:]


def _sample_attention(cache_k, cache_v, q, k, v, sinks):
    w = cache_k.shape[1]
    t = q.shape[1]
    kc = jnp.concatenate([cache_k.astype(k.dtype), k], 1)
    vc = jnp.concatenate([cache_v.astype(v.dtype), v], 1)
    q_pos = PAST_LEN + jnp.arange(t, dtype=jnp.int32)
    k_pos = jnp.concatenate([PAST_LEN - w + jnp.arange(w, dtype=jnp.int32), q_pos])
    o = _window_attention(q, kc, vc, q_pos, k_pos, sinks)
    return o, kc[:, -w:], vc[:, -w:]


def _ssm_discretise(lambda_re, lambda_im, log_dt, b_re, b_im):
    lam = lax.complex(jnp.minimum(lambda_re.astype(jnp.float32), -1e-4), lambda_im.astype(jnp.float32))
    dt = jnp.exp(log_dt.astype(jnp.float32))[:, None]
    lam_dt = lam * dt
    lam_bar = jnp.exp(lam_dt)
    b = lax.complex(b_re.astype(jnp.float32), b_im.astype(jnp.float32))
    b_bar = ((lam_bar - 1.0) / lam)[..., None] * b
    return lam_dt, lam_bar, b_bar


def _ssm_combine(e1, e2):
    a1, b1 = e1
    a2, b2 = e2
    return a2 * a1, a2 * b1 + b2


def _ssm_mixer(u, h0, lam_dt, lam_bar, b_bar, c_re, c_im, d, w_glu):
    n, t, _ = u.shape
    ug = u.reshape(n, t, SSM_GROUPS, SSM_GROUP).astype(jnp.float32)
    bu = jnp.einsum('ntgc,gpc->ntgp', ug.astype(jnp.complex64), b_bar)
    a = jnp.broadcast_to(lam_bar, bu.shape)
    _, h = lax.associative_scan(_ssm_combine, (a, bu), axis=1)
    steps = jnp.arange(1, t + 1, dtype=jnp.float32)[:, None, None]
    h = h + jnp.exp(lam_dt[None] * steps)[None] * h0[:, None]
    c = lax.complex(c_re.astype(jnp.float32), c_im.astype(jnp.float32))
    y = jnp.real(jnp.einsum('ntgp,gcp->ntgc', h, c)) + d.astype(jnp.float32) * ug
    y = jax.nn.gelu(y.reshape(n, t, SSM_WIDTH))
    y = y * jax.nn.sigmoid(y @ w_glu.astype(jnp.float32))
    return y.astype(u.dtype), h[:, -1]


def _hier_route(x2, wg, bg, we, be):
    lg = (x2 @ wg).astype(jnp.float32) + bg.astype(jnp.float32)
    pg = jax.nn.softmax(lg, -1)
    grp = jnp.argmax(lg, -1)
    onehot = jax.nn.one_hot(grp, N_EXPERT_GROUPS, dtype=jnp.float32)
    p_grp = jnp.sum(pg * onehot, -1)
    le_all = jnp.einsum('td,gde->tge', x2, we).astype(jnp.float32) + be.astype(jnp.float32)
    le = jnp.einsum('tge,tg->te', le_all, onehot)
    top_v, top_i = lax.top_k(le, TOP_K_INNER)
    w = jax.nn.softmax(top_v, -1) * p_grp[:, None]
    idx = (grp[:, None] * EXPERTS_PER_GROUP + top_i).astype(jnp.int32)
    return idx, w


def _moe(x2, idx, w, w_gate, w_up, w_down):
    t, d = x2.shape
    m = t * TOP_K_INNER
    flat_e = idx.reshape(m)
    flat_tok = jnp.repeat(jnp.arange(t, dtype=jnp.int32), TOP_K_INNER)
    flat_w = w.reshape(m)
    order = jnp.argsort(flat_e)
    se = flat_e[order]
    counts = jnp.bincount(flat_e, length=N_EXPERTS).astype(jnp.int32)
    padded = (counts + MOE_BLOCK - 1) // MOE_BLOCK * MOE_BLOCK
    pad_end = jnp.cumsum(padded)
    pad_start = pad_end - padded
    start = jnp.cumsum(counts) - counts
    dest = pad_start[se] + jnp.arange(m, dtype=jnp.int32) - start[se]
    nb = (m + MOE_BLOCK - 1) // MOE_BLOCK + N_EXPERTS
    rows = nb * MOE_BLOCK
    row_tok = jnp.full((rows,), t, jnp.int32).at[dest].set(flat_tok[order])
    row_w = jnp.zeros((rows,), x2.dtype).at[dest].set(flat_w[order].astype(x2.dtype))
    blk_e = jnp.minimum(jnp.searchsorted(pad_end, jnp.arange(nb, dtype=jnp.int32) * MOE_BLOCK, side='right'), N_EXPERTS - 1)
    xb = jnp.concatenate([x2, jnp.zeros((1, d), x2.dtype)], 0)[row_tok].reshape(nb, MOE_BLOCK, d)

    def expert_block(args):
        xblk, e = args
        h = jax.nn.silu(xblk @ w_gate[e]) * (xblk @ w_up[e])
        return h @ w_down[e]

    yb = lax.map(expert_block, (xb, blk_e)).reshape(rows, d)
    return jax.ops.segment_sum(yb * row_w[:, None], row_tok, num_segments=t + 1)[:t]


def _decoder_layer(x, pos, attn_fn, h0, w_in, lam_dt, lam_bar, b_bar, c_re, c_im, d, w_glu,
                   sinks, attn_g, ssm_g, w_out, ln1_g, ln1_b, rgw, rgb, rew, reb,
                   e_gate, e_up, e_down, ln2_g, ln2_b):
    n, t, _ = x.shape
    proj = x @ w_in
    q = proj[..., :ATTN_WIDTH].reshape(n, t, N_HEADS, HEAD_DIM)
    k = proj[..., ATTN_WIDTH:ATTN_WIDTH + KV_WIDTH].reshape(n, t, N_KV_HEADS, HEAD_DIM)
    v = proj[..., ATTN_WIDTH + KV_WIDTH:ATTN_WIDTH + 2 * KV_WIDTH].reshape(n, t, N_KV_HEADS, HEAD_DIM)
    u = proj[..., ATTN_WIDTH + 2 * KV_WIDTH:]
    q = _rope(q, pos)
    k = _rope(k, pos)
    o, k_buf, v_buf = attn_fn(q, k, v, sinks)
    s, h_last = _ssm_mixer(u, h0, lam_dt, lam_bar, b_bar, c_re, c_im, d, w_glu)
    merged = jnp.concatenate([_rms_norm(o.reshape(n, t, ATTN_WIDTH), attn_g), _rms_norm(s, ssm_g)], -1)
    x = _layer_norm(DEEPNORM_ALPHA * x + merged @ w_out, ln1_g, ln1_b)
    x2 = x.reshape(n * t, D_MODEL)
    idx, w = _hier_route(x2, rgw, rgb, rew, reb)
    ff = _moe(x2, idx, w, e_gate, e_up, e_down).reshape(n, t, D_MODEL)
    x = _layer_norm(DEEPNORM_ALPHA * x + ff, ln2_g, ln2_b)
    return x, k_buf, v_buf, h_last


def setup_inputs(seed: int = 0) -> dict:
    key = jax.random.key(seed)
    ks = jax.random.split(key, 32)
    f32 = jnp.float32

    def nrm(k, shape, scale):
        return jax.random.normal(k, shape, f32) * scale

    wbuf = min(WINDOW, PAST_LEN)
    n_idx = jnp.arange(SSM_STATE, dtype=f32)
    lam_re = -0.5 + nrm(ks[6], (DEPTH, SSM_GROUPS, SSM_STATE), 0.01)
    lam_im = math.pi * jnp.broadcast_to(n_idx, (DEPTH, SSM_GROUPS, SSM_STATE)) + nrm(ks[7], (DEPTH, SSM_GROUPS, SSM_STATE), 0.01)
    return {
        'x_prompt': nrm(ks[0], (BATCH, SEQ, D_MODEL), 1.0),
        'x_sample': nrm(ks[1], (DEC_BATCH, DEC_SEQ, D_MODEL), 1.0),
        'cache_k': nrm(ks[2], (DEPTH, DEC_BATCH, wbuf, N_KV_HEADS, HEAD_DIM), 1.0),
        'cache_v': nrm(ks[3], (DEPTH, DEC_BATCH, wbuf, N_KV_HEADS, HEAD_DIM), 1.0),
        'state_ssm_re': nrm(ks[4], (DEPTH, DEC_BATCH, SSM_GROUPS, SSM_STATE), 0.1),
        'state_ssm_im': nrm(ks[5], (DEPTH, DEC_BATCH, SSM_GROUPS, SSM_STATE), 0.1),
        'w_in': nrm(ks[8], (DEPTH, D_MODEL, IN_WIDTH), D_MODEL ** -0.5),
        'ssm_lambda_re': lam_re,
        'ssm_lambda_im': lam_im,
        'ssm_log_dt': jax.random.uniform(ks[9], (DEPTH, SSM_GROUPS), f32, math.log(DT_MIN), math.log(DT_MAX)),
        'ssm_b_re': nrm(ks[10], (DEPTH, SSM_GROUPS, SSM_STATE, SSM_GROUP), (2.0 * SSM_GROUP) ** -0.5),
        'ssm_b_im': nrm(ks[11], (DEPTH, SSM_GROUPS, SSM_STATE, SSM_GROUP), (2.0 * SSM_GROUP) ** -0.5),
        'ssm_c_re': nrm(ks[12], (DEPTH, SSM_GROUPS, SSM_GROUP, SSM_STATE), (2.0 * SSM_STATE) ** -0.5),
        'ssm_c_im': nrm(ks[13], (DEPTH, SSM_GROUPS, SSM_GROUP, SSM_STATE), (2.0 * SSM_STATE) ** -0.5),
        'ssm_d': nrm(ks[14], (DEPTH, SSM_GROUPS, SSM_GROUP), 1.0),
        'ssm_w_glu': nrm(ks[15], (DEPTH, SSM_WIDTH, SSM_WIDTH), SSM_WIDTH ** -0.5),
        'attn_sinks': nrm(ks[16], (DEPTH, N_HEADS), 0.5),
        'attn_norm_g': 1.0 + nrm(ks[17], (DEPTH, ATTN_WIDTH), 0.01),
        'ssm_norm_g': 1.0 + nrm(ks[18], (DEPTH, SSM_WIDTH), 0.01),
        'w_out': nrm(ks[19], (DEPTH, MIX_WIDTH, D_MODEL), DEEPNORM_BETA * MIX_WIDTH ** -0.5),
        'ln1_g': 1.0 + nrm(ks[20], (DEPTH, D_MODEL), 0.01),
        'ln1_b': nrm(ks[21], (DEPTH, D_MODEL), 0.01),
        'router_group_w': nrm(ks[22], (DEPTH, D_MODEL, N_EXPERT_GROUPS), D_MODEL ** -0.5),
        'router_group_b': nrm(ks[23], (DEPTH, N_EXPERT_GROUPS), 0.01),
        'router_expert_w': nrm(ks[24], (DEPTH, N_EXPERT_GROUPS, D_MODEL, EXPERTS_PER_GROUP), D_MODEL ** -0.5),
        'router_expert_b': nrm(ks[25], (DEPTH, N_EXPERT_GROUPS, EXPERTS_PER_GROUP), 0.01),
        'expert_w_gate': nrm(ks[26], (DEPTH, N_EXPERTS, D_MODEL, EXPERT_FF), D_MODEL ** -0.5),
        'expert_w_up': nrm(ks[27], (DEPTH, N_EXPERTS, D_MODEL, EXPERT_FF), D_MODEL ** -0.5),
        'expert_w_down': nrm(ks[28], (DEPTH, N_EXPERTS, EXPERT_FF, D_MODEL), DEEPNORM_BETA * EXPERT_FF ** -0.5),
        'ln2_g': 1.0 + nrm(ks[29], (DEPTH, D_MODEL), 0.01),
        'ln2_b': nrm(ks[30], (DEPTH, D_MODEL), 0.01),
    }


def reference(x_prompt, x_sample, cache_k, cache_v, state_ssm_re, state_ssm_im, w_in,
              ssm_lambda_re, ssm_lambda_im, ssm_log_dt, ssm_b_re, ssm_b_im, ssm_c_re, ssm_c_im,
              ssm_d, ssm_w_glu, attn_sinks, attn_norm_g, ssm_norm_g, w_out, ln1_g, ln1_b,
              router_group_w, router_group_b, router_expert_w, router_expert_b,
              expert_w_gate, expert_w_up, expert_w_down, ln2_g, ln2_b):
    y_p = x_prompt
    y_s = x_sample
    pos_p = jnp.arange(x_prompt.shape[1], dtype=jnp.int32)
    pos_s = PAST_LEN + jnp.arange(x_sample.shape[1], dtype=jnp.int32)
    kp_l, vp_l, hre_p_l, him_p_l = [], [], [], []
    ks_l, vs_l, hre_s_l, him_s_l = [], [], [], []
    for l in range(DEPTH):
        lam_dt, lam_bar, b_bar = _ssm_discretise(ssm_lambda_re[l], ssm_lambda_im[l], ssm_log_dt[l], ssm_b_re[l], ssm_b_im[l])
        weights = (w_in[l], lam_dt, lam_bar, b_bar, ssm_c_re[l], ssm_c_im[l], ssm_d[l], ssm_w_glu[l],
                   attn_sinks[l], attn_norm_g[l], ssm_norm_g[l], w_out[l], ln1_g[l], ln1_b[l],
                   router_group_w[l], router_group_b[l], router_expert_w[l], router_expert_b[l],
                   expert_w_gate[l], expert_w_up[l], expert_w_down[l], ln2_g[l], ln2_b[l])
        h0_p = jnp.zeros((x_prompt.shape[0], SSM_GROUPS, SSM_STATE), jnp.complex64)
        y_p, k_p, v_p, h_p = _decoder_layer(y_p, pos_p, _prompt_attention, h0_p, *weights)
        h0_s = lax.complex(state_ssm_re[l].astype(jnp.float32), state_ssm_im[l].astype(jnp.float32))
        attn_s = functools.partial(_sample_attention, cache_k[l], cache_v[l])
        y_s, k_s, v_s, h_s = _decoder_layer(y_s, pos_s, attn_s, h0_s, *weights)
        kp_l.append(k_p)
        vp_l.append(v_p)
        hre_p_l.append(jnp.real(h_p))
        him_p_l.append(jnp.imag(h_p))
        ks_l.append(k_s)
        vs_l.append(v_s)
        hre_s_l.append(jnp.real(h_s))
        him_s_l.append(jnp.imag(h_s))
    new_cache_k_prompt = jnp.stack(kp_l)
    new_cache_v_prompt = jnp.stack(vp_l)
    new_state_ssm_re_prompt = jnp.stack(hre_p_l)
    new_state_ssm_im_prompt = jnp.stack(him_p_l)
    new_cache_k_sample = jnp.stack(ks_l)
    new_cache_v_sample = jnp.stack(vs_l)
    new_state_ssm_re_sample = jnp.stack(hre_s_l)
    new_state_ssm_im_sample = jnp.stack(him_s_l)
    return (y_p, y_s, new_cache_k_prompt, new_cache_v_prompt, new_state_ssm_re_prompt, new_state_ssm_im_prompt,
            new_cache_k_sample, new_cache_v_sample, new_state_ssm_re_sample, new_state_ssm_im_sample)
```

```python
import functools
import math

import numpy as np
import jax
import jax.numpy as jnp
from jax import lax
from jax.experimental import pallas as pl
from jax.experimental.pallas import tpu as pltpu

F32 = jnp.float32
BF16 = jnp.bfloat16

D_MODEL = 2048
DEPTH = 2
PAST_LEN = 16384
ATTN_WIDTH = 1024
SSM_WIDTH = 1024
HEAD_DIM = 64
N_HEADS = 16
N_KV_HEADS = 4
Q_PER_KV = 4
KV_WIDTH = 256
WINDOW = 128
ROPE_THETA = 10000.0
SSM_GROUP = 16
SSM_GROUPS = 64
SSM_STATE = 64
N_EXPERT_GROUPS = 4
EXPERTS_PER_GROUP = 8
N_EXPERTS = 32
EXPERT_FF = 512
MOE_BLOCK = 128
DEEPNORM_ALPHA = (2.0 * DEPTH) ** 0.25
LN_EPS = 1e-5
RMS_EPS = 1e-6

LANES = 128
SUBLANES = 8
VMEM_LIMIT = 56 * 1024 * 1024

SSM_CHUNKS = 4
CH_GROUPS = SSM_GROUPS // SSM_CHUNKS
CH_STATES = CH_GROUPS * SSM_STATE
CH_IN = CH_GROUPS * SSM_GROUP
SCAN_T = 512
SCAN_SEG = SCAN_T // SUBLANES


def _cparams(sem=None, vmem=VMEM_LIMIT):
    return pltpu.CompilerParams(dimension_semantics=sem, vmem_limit_bytes=vmem)


def _full(shape):
    n = len(shape)
    return pl.BlockSpec(shape, lambda *a: (0,) * n)


def _inproj_body(x_ref, w_ref, cos_ref, sa_ref, sb_ref, q_ref, k_ref, v_ref, u_ref):
    x = x_ref[...].astype(BF16)
    cos = cos_ref[...]
    sa = sa_ref[...]
    sb = sb_ref[...]

    def rope(c):
        return c * cos + pltpu.roll(c, LANES - 32, 1) * sa + pltpu.roll(c, 32, 1) * sb

    pq = jnp.dot(x, w_ref[:, 0:ATTN_WIDTH], preferred_element_type=F32)
    for j in range(ATTN_WIDTH // LANES):
        q_ref[:, j * LANES:(j + 1) * LANES] = rope(pq[:, j * LANES:(j + 1) * LANES]).astype(BF16)
    pk = jnp.dot(x, w_ref[:, ATTN_WIDTH:ATTN_WIDTH + KV_WIDTH], preferred_element_type=F32)
    for j in range(KV_WIDTH // LANES):
        k_ref[:, j * LANES:(j + 1) * LANES] = rope(pk[:, j * LANES:(j + 1) * LANES])
    v_ref[...] = jnp.dot(x, w_ref[:, ATTN_WIDTH + KV_WIDTH:ATTN_WIDTH + 2 * KV_WIDTH],
                         preferred_element_type=F32)
    u_ref[...] = jnp.dot(x, w_ref[:, ATTN_WIDTH + 2 * KV_WIDTH:], preferred_element_type=F32)


def _inproj(x, w_bf, cos_t, sa_t, sb_t, tm):
    rows = x.shape[0]
    tab_blocks = cos_t.shape[0] // tm
    in_w = w_bf.shape[1]
    row_spec = lambda w: pl.BlockSpec((tm, w), lambda i: (i, 0))
    tab_spec = pl.BlockSpec((tm, LANES), lambda i: (i % tab_blocks, 0))
    return pl.pallas_call(
        _inproj_body,
        grid=(rows // tm,),
        in_specs=[row_spec(D_MODEL), _full((D_MODEL, in_w)), tab_spec, tab_spec, tab_spec],
        out_specs=[row_spec(ATTN_WIDTH), row_spec(KV_WIDTH), row_spec(KV_WIDTH), row_spec(SSM_WIDTH)],
        out_shape=[jax.ShapeDtypeStruct((rows, ATTN_WIDTH), BF16),
                   jax.ShapeDtypeStruct((rows, KV_WIDTH), F32),
                   jax.ShapeDtypeStruct((rows, KV_WIDTH), F32),
                   jax.ShapeDtypeStruct((rows, SSM_WIDTH), F32)],
        compiler_params=_cparams(("arbitrary",)),
        name="inproj_rope",
    )(x, w_bf, cos_t, sa_t, sb_t)


def _rope_tables(positions):
    half = HEAD_DIM // 2
    inv_freq = ROPE_THETA ** (-np.arange(half, dtype=np.float64) / half)
    ang = positions.astype(np.float64)[:, None] * inv_freq[None, :]
    lane = np.arange(LANES)
    cos = np.cos(ang)[:, lane % half]
    sin = np.sin(ang)[:, lane % half]
    first = (lane % HEAD_DIM) < half
    sa = np.where(first[None, :], -sin, 0.0)
    sb = np.where(first[None, :], 0.0, sin)
    return (jnp.asarray(cos, F32), jnp.asarray(sa, F32), jnp.asarray(sb, F32))


def _rms(x, g):
    return x * lax.rsqrt(jnp.mean(jnp.square(x), axis=-1, keepdims=True) + RMS_EPS) * g


def _attn_prompt_body(sink_ref, q_ref, kp_ref, kc_ref, vp_ref, vc_ref, g_ref, o_ref, acc_ref, *, blocks_per_seq):
    i = pl.program_id(0)
    has_prev = (i % blocks_per_seq) > 0
    kk = jnp.concatenate([kp_ref[...], kc_ref[...]], axis=0).astype(BF16)
    vv = jnp.concatenate([vp_ref[...], vc_ref[...]], axis=0).astype(BF16)
    row = lax.broadcasted_iota(jnp.int32, (WINDOW, 2 * WINDOW), 0)
    col = lax.broadcasted_iota(jnp.int32, (WINDOW, 2 * WINDOW), 1)
    visible = (col > row) & (col <= row + WINDOW) & ((col >= WINDOW) | has_prev)
    q = q_ref[...]
    for h in range(N_HEADS):
        g = h // Q_PER_KV
        qh = q[:, h * HEAD_DIM:(h + 1) * HEAD_DIM]
        kh = kk[:, g * HEAD_DIM:(g + 1) * HEAD_DIM]
        vh = vv[:, g * HEAD_DIM:(g + 1) * HEAD_DIM]
        s = lax.dot_general(qh, kh, (((1,), (1,)), ((), ())), preferred_element_type=F32) * (HEAD_DIM ** -0.5)
        s = jnp.where(visible, s, -jnp.inf)
        sink = sink_ref[h]
        m = jnp.maximum(jnp.max(s, axis=-1, keepdims=True), sink)
        p = jnp.exp(s - m)
        denom = jnp.sum(p, axis=-1, keepdims=True) + jnp.exp(sink - m)
        oh = jnp.dot(p.astype(BF16), vh, preferred_element_type=F32) / denom
        acc_ref[:, h * HEAD_DIM:(h + 1) * HEAD_DIM] = oh
    o_ref[...] = _rms(acc_ref[...], g_ref[...]).astype(BF16)


def _attn_prompt(sinks, q, k, v, g, seq):
    rows = q.shape[0]
    nblk = rows // WINDOW
    bps = seq // WINDOW
    cur = lambda w: pl.BlockSpec((WINDOW, w), lambda i: (i, 0))
    prev = lambda w: pl.BlockSpec((WINDOW, w), lambda i: (jnp.maximum(i - 1, 0), 0))
    return pl.pallas_call(
        functools.partial(_attn_prompt_body, blocks_per_seq=bps),
        grid=(nblk,),
        in_specs=[pl.BlockSpec(memory_space=pltpu.SMEM), cur(ATTN_WIDTH), prev(KV_WIDTH), cur(KV_WIDTH),
                  prev(KV_WIDTH), cur(KV_WIDTH), _full((1, ATTN_WIDTH))],
        out_specs=cur(ATTN_WIDTH),
        out_shape=jax.ShapeDtypeStruct((rows, ATTN_WIDTH), BF16),
        scratch_shapes=[pltpu.VMEM((WINDOW, ATTN_WIDTH), F32)],
        compiler_params=_cparams(("arbitrary",)),
        name="attn_prompt",
    )(sinks, q, k, k, v, v, g)


SEQ_PER_STEP = 8


def _attn_sample_body(sink_ref, q_ref, kn_ref, vn_ref, ck_ref, cv_ref, g_ref, o_ref, nk_ref, nv_ref, acc_ref,
                      *, dec_seq):
    nq = SEQ_PER_STEP * dec_seq
    wb = ck_ref.shape[1]
    nkc = SEQ_PER_STEP * wb
    ck = ck_ref[...].reshape(nkc, KV_WIDTH)
    cv = cv_ref[...].reshape(nkc, KV_WIDTH)
    kn = kn_ref[...]
    vn = vn_ref[...]
    ckb = ck.astype(BF16)
    cvb = cv.astype(BF16)
    knb = kn.astype(BF16)
    vnb = vn.astype(BF16)
    q = q_ref[...]

    rows = Q_PER_KV * nq
    r = lax.broadcasted_iota(jnp.int32, (rows, nkc), 0) % nq
    c = lax.broadcasted_iota(jnp.int32, (rows, nkc), 1)
    vis_c = ((r // dec_seq) == (c // wb)) & ((c % wb) >= (r % dec_seq) + 1 + (wb - WINDOW))
    rn = lax.broadcasted_iota(jnp.int32, (rows, nq), 0) % nq
    cn = lax.broadcasted_iota(jnp.int32, (rows, nq), 1)
    vis_n = ((rn // dec_seq) == (cn // dec_seq)) & ((cn % dec_seq) <= (rn % dec_seq))
    hrow = lax.broadcasted_iota(jnp.int32, (rows, 1), 0) // nq

    dn = (((1,), (1,)), ((), ()))
    for g in range(N_KV_HEADS):
        qs = jnp.concatenate([q[:, (g * Q_PER_KV + j) * HEAD_DIM:(g * Q_PER_KV + j + 1) * HEAD_DIM]
                              for j in range(Q_PER_KV)], axis=0)
        sl = slice(g * HEAD_DIM, (g + 1) * HEAD_DIM)
        sc = lax.dot_general(qs, ckb[:, sl], dn, preferred_element_type=F32) * (HEAD_DIM ** -0.5)
        sn = lax.dot_general(qs, knb[:, sl], dn, preferred_element_type=F32) * (HEAD_DIM ** -0.5)
        sc = jnp.where(vis_c, sc, -jnp.inf)
        sn = jnp.where(vis_n, sn, -jnp.inf)
        sink = jnp.zeros((rows, 1), F32)
        for j in range(Q_PER_KV):
            sink = jnp.where(hrow == j, sink_ref[g * Q_PER_KV + j], sink)
        m = jnp.maximum(jnp.maximum(jnp.max(sc, axis=-1, keepdims=True), jnp.max(sn, axis=-1, keepdims=True)), sink)
        pc = jnp.exp(sc - m)
        pn = jnp.exp(sn - m)
        denom = jnp.sum(pc, axis=-1, keepdims=True) + jnp.sum(pn, axis=-1, keepdims=True) + jnp.exp(sink - m)
        o = (jnp.dot(pc.astype(BF16), cvb[:, sl], preferred_element_type=F32)
             + jnp.dot(pn.astype(BF16), vnb[:, sl], preferred_element_type=F32)) / denom
        for j in range(Q_PER_KV):
            h = g * Q_PER_KV + j
            acc_ref[:, h * HEAD_DIM:(h + 1) * HEAD_DIM] = o[j * nq:(j + 1) * nq, :]
    o_ref[...] = _rms(acc_ref[...], g_ref[...]).astype(BF16)

    keep = wb - dec_seq
    for s in range(SEQ_PER_STEP):
        nk_ref[s, 0:keep, :] = ck_ref[s, dec_seq:wb, :]
        nv_ref[s, 0:keep, :] = cv_ref[s, dec_seq:wb, :]
        nk_ref[s, keep:wb, :] = kn[s * dec_seq:(s + 1) * dec_seq, :]
        nv_ref[s, keep:wb, :] = vn[s * dec_seq:(s + 1) * dec_seq, :]


def _attn_sample(sinks, q, kn, vn, ck, cv, g, dec_seq):
    rows = q.shape[0]
    nseq, wb, _ = ck.shape
    nq = SEQ_PER_STEP * dec_seq
    rowb = lambda w: pl.BlockSpec((nq, w), lambda i: (i, 0))
    cb = pl.BlockSpec((SEQ_PER_STEP, wb, KV_WIDTH), lambda i: (i, 0, 0))
    return pl.pallas_call(
        functools.partial(_attn_sample_body, dec_seq=dec_seq),
        grid=(nseq // SEQ_PER_STEP,),
        in_specs=[pl.BlockSpec(memory_space=pltpu.SMEM), rowb(ATTN_WIDTH), rowb(KV_WIDTH), rowb(KV_WIDTH),
                  cb, cb, _full((1, ATTN_WIDTH))],
        out_specs=[rowb(ATTN_WIDTH), cb, cb],
        out_shape=[jax.ShapeDtypeStruct((rows, ATTN_WIDTH), BF16),
                   jax.ShapeDtypeStruct(ck.shape, F32), jax.ShapeDtypeStruct(cv.shape, F32)],
        scratch_shapes=[pltpu.VMEM((nq, ATTN_WIDTH), F32)],
        compiler_params=_cparams(("arbitrary",)),
        name="attn_sample",
    )(sinks, q, kn, vn, ck, cv, g)


def _ssm_prompt_body(u_ref, b_ref, c_ref, lr_ref, li_ref, pr_ref, pi_ref, d_ref, y_ref, hl_ref,
                     uperm_ref, bu_ref, hb_ref, yp_ref, carry_ref):
    tc = pl.program_id(1)
    cc = pl.program_id(2)

    @pl.when(cc == 0)
    def _():
        for k in range(SSM_WIDTH // LANES):
            for s in range(SUBLANES):
                uperm_ref[k, pl.ds(s, SCAN_SEG, stride=SUBLANES), :] = (
                    u_ref[s * SCAN_SEG:(s + 1) * SCAN_SEG, k * LANES:(k + 1) * LANES])

    @pl.when(tc == 0)
    def _():
        carry_ref[cc] = jnp.zeros((2, CH_STATES), F32)

    lanes_per_chunk = CH_IN // LANES
    up = jnp.concatenate([uperm_ref[cc * lanes_per_chunk + k] for k in range(lanes_per_chunk)], axis=1)
    bu_ref[...] = jnp.dot(up.astype(BF16), b_ref[cc], preferred_element_type=F32)

    ar = jnp.broadcast_to(lr_ref[cc], (SUBLANES, CH_STATES))
    ai = jnp.broadcast_to(li_ref[cc], (SUBLANES, CH_STATES))

    def step(j, h):
        hr, hi = h
        r0 = pl.multiple_of(j * SUBLANES, SUBLANES)
        br = bu_ref[pl.ds(r0, SUBLANES), 0:CH_STATES]
        bi = bu_ref[pl.ds(r0, SUBLANES), CH_STATES:2 * CH_STATES]
        nr = ar * hr + (br - ai * hi)
        ni = ar * hi + (bi + ai * hr)
        bu_ref[pl.ds(r0, SUBLANES), 0:CH_STATES] = nr
        bu_ref[pl.ds(r0, SUBLANES), CH_STATES:2 * CH_STATES] = ni
        return nr, ni

    zero = jnp.zeros((SUBLANES, CH_STATES), F32)
    fr, fi = lax.fori_loop(0, SCAN_SEG, step, (zero, zero))

    cst = carry_ref[cc]
    c_r = cst[0:1, :]
    c_i = cst[1:2, :]
    lsr = pr_ref[cc, SCAN_SEG - 1:SCAN_SEG, :]
    lsi = pi_ref[cc, SCAN_SEG - 1:SCAN_SEG, :]
    rows_r, rows_i = [], []
    for s in range(SUBLANES):
        rows_r.append(c_r)
        rows_i.append(c_i)
        n_r = fr[s:s + 1, :] + (lsr * c_r - lsi * c_i)
        n_i = fi[s:s + 1, :] + (lsr * c_i + lsi * c_r)
        c_r, c_i = n_r, n_i
    carry_ref[cc] = jnp.concatenate([c_r, c_i], axis=0)
    hl_ref[0, cc] = jnp.concatenate([c_r, c_i], axis=0)
    cm_r = jnp.concatenate(rows_r, axis=0)
    cm_i = jnp.concatenate(rows_i, axis=0)

    def fix(jj, _):
        r0 = pl.multiple_of(jj * 2 * SUBLANES, 2 * SUBLANES)
        outs_r, outs_i = [], []
        for half in range(2):
            j = jj * 2 + half
            pr = pr_ref[cc, pl.ds(j, 1), :]
            pi = pi_ref[cc, pl.ds(j, 1), :]
            rr = r0 + half * SUBLANES
            h_r = bu_ref[pl.ds(rr, SUBLANES), 0:CH_STATES]
            h_i = bu_ref[pl.ds(rr, SUBLANES), CH_STATES:2 * CH_STATES]
            outs_r.append(h_r + (pr * cm_r - pi * cm_i))
            outs_i.append(h_i + (pr * cm_i + pi * cm_r))
        hb_ref[pl.ds(r0, 2 * SUBLANES), 0:CH_STATES] = jnp.concatenate(outs_r, axis=0).astype(BF16)
        hb_ref[pl.ds(r0, 2 * SUBLANES), CH_STATES:2 * CH_STATES] = jnp.concatenate(outs_i, axis=0).astype(BF16)
        return 0

    lax.fori_loop(0, SCAN_SEG // 2, fix, 0)

    yp = jnp.dot(hb_ref[...], c_ref[cc], preferred_element_type=F32) + d_ref[cc] * up
    for k in range(lanes_per_chunk):
        yp_ref[k] = yp[:, k * LANES:(k + 1) * LANES]
    for s in range(SUBLANES):
        for k in range(lanes_per_chunk):
            y_ref[s * SCAN_SEG:(s + 1) * SCAN_SEG, k * LANES:(k + 1) * LANES] = (
                yp_ref[k, pl.ds(s, SCAN_SEG, stride=SUBLANES), :])


def _ssm_prompt(u, sp, nbatch, seq):
    rows = u.shape[0]
    ntc = seq // SCAN_T
    y, hl = pl.pallas_call(
        _ssm_prompt_body,
        grid=(nbatch, ntc, SSM_CHUNKS),
        in_specs=[pl.BlockSpec((SCAN_T, SSM_WIDTH), lambda b, t, c: (b * ntc + t, 0)),
                  _full(sp["B"].shape), _full(sp["C"].shape), _full(sp["lr"].shape), _full(sp["li"].shape),
                  _full(sp["pr"].shape), _full(sp["pi"].shape), _full(sp["d"].shape)],
        out_specs=[pl.BlockSpec((SCAN_T, CH_IN), lambda b, t, c: (b * ntc + t, c)),
                   pl.BlockSpec((1, SSM_CHUNKS, 2, CH_STATES), lambda b, t, c: (b, 0, 0, 0))],
        out_shape=[jax.ShapeDtypeStruct((rows, SSM_WIDTH), F32),
                   jax.ShapeDtypeStruct((nbatch, SSM_CHUNKS, 2, CH_STATES), F32)],
        scratch_shapes=[pltpu.VMEM((SSM_WIDTH // LANES, SCAN_T, LANES), F32),
                        pltpu.VMEM((SCAN_T, 2 * CH_STATES), F32),
                        pltpu.VMEM((SCAN_T, 2 * CH_STATES), BF16),
                        pltpu.VMEM((CH_IN // LANES, SCAN_T, LANES), F32),
                        pltpu.VMEM((SSM_CHUNKS, 2, CH_STATES), F32)],
        compiler_params=_cparams(("arbitrary", "arbitrary", "arbitrary")),
        name="ssm_prompt",
    )(u, sp["B"], sp["C"], sp["lr"], sp["li"], sp["pr"], sp["pi"], sp["d"])
    return y, hl


def _ssm_sample_body(u_ref, h0_ref, b_ref, c_ref, lr_ref, li_ref, d_ref, y_ref, ho_ref, bu_ref, *, nseq, dec_seq):
    nl = CH_STATES // LANES
    for k in range(SSM_CHUNKS):
        uk = u_ref[:, k * CH_IN:(k + 1) * CH_IN]
        bu = jnp.dot(uk.astype(BF16), b_ref[k], preferred_element_type=F32)
        for c in range(2 * nl):
            bu_ref[c] = bu[:, c * LANES:(c + 1) * LANES]
        base = k * 2 * CH_STATES
        hr = h0_ref[:, base:base + CH_STATES]
        hi = h0_ref[:, base + CH_STATES:base + 2 * CH_STATES]
        ar = lr_ref[k]
        ai = li_ref[k]
        for t in range(dec_seq):
            br = jnp.concatenate([bu_ref[c, pl.ds(t, nseq, stride=dec_seq), :] for c in range(nl)], axis=1)
            bi = jnp.concatenate([bu_ref[nl + c, pl.ds(t, nseq, stride=dec_seq), :] for c in range(nl)], axis=1)
            hr, hi = ar * hr + (br - ai * hi), ar * hi + (bi + ai * hr)
            for c in range(nl):
                bu_ref[c, pl.ds(t, nseq, stride=dec_seq), :] = hr[:, c * LANES:(c + 1) * LANES]
                bu_ref[nl + c, pl.ds(t, nseq, stride=dec_seq), :] = hi[:, c * LANES:(c + 1) * LANES]
        ho_ref[:, base:base + CH_STATES] = hr
        ho_ref[:, base + CH_STATES:base + 2 * CH_STATES] = hi
        hall = jnp.concatenate([bu_ref[c] for c in range(2 * nl)], axis=1)
        y_ref[:, k * CH_IN:(k + 1) * CH_IN] = (
            jnp.dot(hall.astype(BF16), c_ref[k], preferred_element_type=F32) + d_ref[k] * uk)


def _ssm_sample(u, h0, sp, nseq, dec_seq):
    rows = u.shape[0]
    return pl.pallas_call(
        functools.partial(_ssm_sample_body, nseq=nseq, dec_seq=dec_seq),
        grid=(1,),
        in_specs=[_full(u.shape), _full(h0.shape), _full(sp["B"].shape), _full(sp["C"].shape),
                  _full(sp["lr"].shape), _full(sp["li"].shape), _full(sp["d"].shape)],
        out_specs=[_full((rows, SSM_WIDTH)), _full(h0.shape)],
        out_shape=[jax.ShapeDtypeStruct((rows, SSM_WIDTH), F32), jax.ShapeDtypeStruct(h0.shape, F32)],
        scratch_shapes=[pltpu.VMEM((2 * CH_STATES // LANES, rows, LANES), F32)],
        compiler_params=_cparams(("arbitrary",)),
        name="ssm_sample",
    )(u, h0, sp["B"], sp["C"], sp["lr"], sp["li"], sp["d"])


def _ssm_params(lambda_re, lambda_im, log_dt, b_re, b_im, c_re, c_im, d):
    lam = lax.complex(jnp.minimum(lambda_re.astype(F32), -1e-4), lambda_im.astype(F32))
    dt = jnp.exp(log_dt.astype(F32))[:, None]
    lam_dt = lam * dt
    lam_bar = jnp.exp(lam_dt)
    b = lax.complex(b_re.astype(F32), b_im.astype(F32))
    b_bar = ((lam_bar - 1.0) / lam)[..., None] * b
    eye = jnp.eye(CH_GROUPS, dtype=F32)

    def pack_b(part):
        x = part.reshape(SSM_CHUNKS, CH_GROUPS, SSM_STATE, SSM_GROUP)
        return jnp.einsum("kgpc,gh->kgchp", x, eye).reshape(SSM_CHUNKS, CH_IN, CH_STATES)

    def pack_c(part):
        x = part.reshape(SSM_CHUNKS, CH_GROUPS, SSM_GROUP, SSM_STATE)
        return jnp.einsum("kgcp,gh->kgphc", x, eye).reshape(SSM_CHUNKS, CH_STATES, CH_IN)

    bmat = jnp.concatenate([pack_b(jnp.real(b_bar)), pack_b(jnp.imag(b_bar))], axis=2).astype(BF16)
    cmat = jnp.concatenate([pack_c(c_re.astype(F32)), pack_c(-c_im.astype(F32))], axis=1).astype(BF16)
    steps = jnp.arange(1, SCAN_SEG + 1, dtype=F32)[:, None, None]
    pw = jnp.exp(lam_dt[None] * steps)
    pw = pw.reshape(SCAN_SEG, SSM_CHUNKS, CH_STATES).transpose(1, 0, 2)
    lb = lam_bar.reshape(SSM_CHUNKS, 1, CH_STATES)
    return dict(B=bmat, C=cmat, lr=jnp.real(lb), li=jnp.imag(lb), pr=jnp.real(pw), pi=jnp.imag(pw),
                d=d.astype(F32).reshape(SSM_CHUNKS, 1, CH_IN))


def _pack_state(re, im):
    n = re.shape[0]
    st = jnp.stack([re.reshape(n, SSM_CHUNKS, CH_STATES), im.reshape(n, SSM_CHUNKS, CH_STATES)], axis=2)
    return st.reshape(n, SSM_CHUNKS * 2 * CH_STATES).astype(F32)


def _unpack_state(h, n):
    st = h.reshape(n, SSM_CHUNKS, 2, CH_GROUPS, SSM_STATE)
    return (st[:, :, 0].reshape(n, SSM_GROUPS, SSM_STATE), st[:, :, 1].reshape(n, SSM_GROUPS, SSM_STATE))


def _glu_body(y_ref, w_ref, g_ref, o_ref):
    a = jax.nn.gelu(y_ref[...])
    z = a * jax.nn.sigmoid(jnp.dot(a.astype(BF16), w_ref[...], preferred_element_type=F32))
    o_ref[...] = _rms(z, g_ref[...]).astype(BF16)


def _glu(y, w_bf, g, tm):
    rows = y.shape[0]
    spec = pl.BlockSpec((tm, SSM_WIDTH), lambda i: (i, 0))
    return pl.pallas_call(
        _glu_body,
        grid=(rows // tm,),
        in_specs=[spec, _full((SSM_WIDTH, SSM_WIDTH)), _full((1, SSM_WIDTH))],
        out_specs=spec,
        out_shape=jax.ShapeDtypeStruct((rows, SSM_WIDTH), BF16),
        compiler_params=_cparams(("arbitrary",)),
        name="glu_rms",
    )(y, w_bf, g)


def _layer_norm(y, g, b):
    mu = jnp.mean(y, axis=-1, keepdims=True)
    yc = y - mu
    var = jnp.mean(jnp.square(yc), axis=-1, keepdims=True)
    return yc * lax.rsqrt(var + LN_EPS) * g + b


def _outproj_body(*refs, aliased, nblk):
    if aliased:
        refs = refs[1:]
    a_ref, s_ref, x_ref, w_ref, g_ref, b_ref, wr_ref, br_ref, o_ref, lg_ref = refs
    i = pl.program_id(0)

    @pl.when(i < nblk)
    def _():
        acc = jnp.dot(a_ref[...], w_ref[0:ATTN_WIDTH, :], preferred_element_type=F32)
        acc = acc + jnp.dot(s_ref[...], w_ref[ATTN_WIDTH:, :], preferred_element_type=F32)
        x1 = _layer_norm(DEEPNORM_ALPHA * x_ref[...] + acc, g_ref[...], b_ref[...])
        o_ref[...] = x1
        lg_ref[...] = jnp.dot(x1, wr_ref[...], preferred_element_type=F32,
                              precision=lax.Precision.HIGHEST) + br_ref[...]

    @pl.when(i >= nblk)
    def _():
        o_ref[...] = jnp.zeros(o_ref.shape, F32)


def _outproj(an, sn, x, w_bf, g, b, wr, br, tm, total_rows, row_block0, x1_buf=None):
    rows = an.shape[0]
    aliased = x1_buf is not None
    nblk = rows // tm
    tail = 0 if aliased or rows == total_rows else 1
    blk = lambda i: jnp.minimum(i, nblk - 1)
    half = lambda: pl.BlockSpec((tm, ATTN_WIDTH), lambda i: (blk(i), 0))
    in_specs = [half(), half(), pl.BlockSpec((tm, D_MODEL), lambda i: (blk(i), 0)),
                _full((D_MODEL, D_MODEL)), _full((1, D_MODEL)), _full((1, D_MODEL)),
                _full((D_MODEL, LANES)), _full((1, LANES))]
    args = [an, sn, x, w_bf, g, b, wr, br]
    if aliased:
        in_specs = [pl.BlockSpec(memory_space=pl.ANY)] + in_specs
        args = [x1_buf] + args
    return pl.pallas_call(
        functools.partial(_outproj_body, aliased=aliased, nblk=nblk),
        grid=(nblk + tail,),
        in_specs=in_specs,
        out_specs=[pl.BlockSpec((tm, D_MODEL), lambda i: (i + row_block0, 0)),
                   pl.BlockSpec((tm, LANES), lambda i: (blk(i), 0))],
        out_shape=[jax.ShapeDtypeStruct((total_rows, D_MODEL), F32), jax.ShapeDtypeStruct((rows, LANES), F32)],
        input_output_aliases={0: 0} if aliased else {},
        compiler_params=_cparams(("arbitrary",)),
        name="outproj_ln1",
    )(*args)


def _route_body(lg_ref, o_ref):
    x = lg_ref[...]
    lane = lax.broadcasted_iota(jnp.int32, x.shape, 1)
    big = jnp.int32(1 << 20)
    neg = -jnp.inf
    gmask = lane < N_EXPERT_GROUPS
    gm = jnp.max(jnp.where(gmask, x, neg), axis=-1, keepdims=True)
    grp = jnp.min(jnp.where(gmask & (x == gm), lane, big), axis=-1, keepdims=True)
    p_grp = 1.0 / jnp.sum(jnp.where(gmask, jnp.exp(x - gm), 0.0), axis=-1, keepdims=True)
    lo = N_EXPERT_GROUPS + grp * EXPERTS_PER_GROUP
    emask = (lane >= lo) & (lane < lo + EXPERTS_PER_GROUP)
    v1 = jnp.max(jnp.where(emask, x, neg), axis=-1, keepdims=True)
    i1 = jnp.min(jnp.where(emask & (x == v1), lane, big), axis=-1, keepdims=True)
    emask2 = emask & (lane != i1)
    v2 = jnp.max(jnp.where(emask2, x, neg), axis=-1, keepdims=True)
    i2 = jnp.min(jnp.where(emask2 & (x == v2), lane, big), axis=-1, keepdims=True)
    e2 = jnp.exp(v2 - v1)
    w1 = p_grp / (1.0 + e2)
    w2 = p_grp * e2 / (1.0 + e2)
    out = jnp.where(lane == 0, (i1 - N_EXPERT_GROUPS).astype(F32), 0.0)
    out = jnp.where(lane == 1, (i2 - N_EXPERT_GROUPS).astype(F32), out)
    out = jnp.where(lane == 2, w1, out)
    out = jnp.where(lane == 3, w2, out)
    o_ref[...] = out


def _route(logits, tm):
    rows = logits.shape[0]
    spec = pl.BlockSpec((tm, LANES), lambda i: (i, 0))
    return pl.pallas_call(
        _route_body, grid=(rows // tm,), in_specs=[spec], out_specs=spec,
        out_shape=jax.ShapeDtypeStruct((rows, LANES), F32),
        compiler_params=_cparams(("arbitrary",)), name="route_select",
    )(logits)


def _moe_body(blk_e_ref, nused_ref, row_tok_ref, x_hbm, wg_ref, wu_ref, wd_ref, o_ref,
              xbuf, sem, wg_bf, wu_bf, wd_bf):
    i = pl.program_id(0)
    nused = nused_ref[0]
    slot = i % 2

    def row_copy(blk, r, s):
        tok = row_tok_ref[blk * MOE_BLOCK + r]
        return pltpu.make_async_copy(x_hbm.at[pl.ds(tok, 1), :], xbuf.at[s, pl.ds(r, 1), :], sem.at[s])

    def start_gather(blk, s):
        def body(r, _):
            row_copy(blk, r, s).start()
            return 0
        lax.fori_loop(0, MOE_BLOCK, body, 0)

    @pl.when(i == 0)
    def _():
        start_gather(0, 0)

    @pl.when(i + 1 < nused)
    def _():
        start_gather(i + 1, 1 - slot)

    @pl.when(i < nused)
    def _():
        changed = (i == 0) | (blk_e_ref[i] != blk_e_ref[jnp.maximum(i - 1, 0)])

        @pl.when(changed)
        def _():
            wg_bf[...] = wg_ref[0].astype(BF16)
            wu_bf[...] = wu_ref[0].astype(BF16)
            wd_bf[...] = wd_ref[0].astype(BF16)

        pltpu.make_async_copy(x_hbm.at[pl.ds(0, MOE_BLOCK), :], xbuf.at[slot], sem.at[slot]).wait()
        x = xbuf[slot].astype(BF16)
        hg = jnp.dot(x, wg_bf[...], preferred_element_type=F32)
        hu = jnp.dot(x, wu_bf[...], preferred_element_type=F32)
        h = (jax.nn.silu(hg) * hu).astype(BF16)
        o_ref[...] = jnp.dot(h, wd_bf[...], preferred_element_type=F32)

    @pl.when(i >= nused)
    def _():
        o_ref[...] = jnp.zeros(o_ref.shape, F32)


def _moe(blk_e, nused, row_tok, x1, w_gate, w_up, w_down):
    nb = blk_e.shape[0]
    grid_spec = pltpu.PrefetchScalarGridSpec(
        num_scalar_prefetch=3,
        grid=(nb,),
        in_specs=[pl.BlockSpec(memory_space=pl.ANY),
                  pl.BlockSpec((1, D_MODEL, EXPERT_FF), lambda i, be, nu, rt: (be[i], 0, 0)),
                  pl.BlockSpec((1, D_MODEL, EXPERT_FF), lambda i, be, nu, rt: (be[i], 0, 0)),
                  pl.BlockSpec((1, EXPERT_FF, D_MODEL), lambda i, be, nu, rt: (be[i], 0, 0))],
        out_specs=pl.BlockSpec((MOE_BLOCK, D_MODEL), lambda i, be, nu, rt: (i, 0)),
        scratch_shapes=[pltpu.VMEM((2, MOE_BLOCK, D_MODEL), F32),
                        pltpu.SemaphoreType.DMA((2,)),
                        pltpu.VMEM((D_MODEL, EXPERT_FF), BF16),
                        pltpu.VMEM((D_MODEL, EXPERT_FF), BF16),
                        pltpu.VMEM((EXPERT_FF, D_MODEL), BF16)],
    )
    return pl.pallas_call(
        _moe_body,
        grid_spec=grid_spec,
        out_shape=jax.ShapeDtypeStruct((nb * MOE_BLOCK, D_MODEL), F32),
        compiler_params=_cparams(("arbitrary",)),
        name="moe_experts",
    )(blk_e, nused, row_tok, x1, w_gate, w_up, w_down)


def _moe_plan(idx, ntok):
    m = ntok * 2
    flat_e = idx.reshape(m)
    order = jnp.argsort(flat_e)
    se = flat_e[order]
    counts = jnp.bincount(flat_e, length=N_EXPERTS).astype(jnp.int32)
    padded = (counts + MOE_BLOCK - 1) // MOE_BLOCK * MOE_BLOCK
    pad_end = jnp.cumsum(padded)
    pad_start = pad_end - padded
    start = jnp.cumsum(counts) - counts
    dest = (pad_start[se] + jnp.arange(m, dtype=jnp.int32) - start[se]).astype(jnp.int32)
    nb = m // MOE_BLOCK + N_EXPERTS
    row_tok = jnp.zeros((nb * MOE_BLOCK,), jnp.int32).at[dest].set((order // 2).astype(jnp.int32))
    pos = jnp.zeros((m,), jnp.int32).at[order].set(dest)
    blk = jnp.arange(nb, dtype=jnp.int32)
    blk_e = jnp.minimum(jnp.searchsorted(pad_end, blk * MOE_BLOCK, side="right"), N_EXPERTS - 1).astype(jnp.int32)
    nused = (pad_end[-1] // MOE_BLOCK).astype(jnp.int32)
    blk_e = jnp.where(blk < nused, blk_e, blk_e[jnp.maximum(nused - 1, 0)])
    return blk_e, nused.reshape(1), row_tok, pos


COMB_T = 128


def _combine_body(pos_ref, ys_hbm, x_ref, rt_ref, g_ref, b_ref, op_ref, os_ref, gbuf, sem, *, n_prompt_blocks):
    i = pl.program_id(0)
    nsteps = pl.num_programs(0)
    slot = i % 2

    def start_gather(step, s):
        def body(r, _):
            a = (step * COMB_T + r) * 2
            for k in range(2):
                p = pos_ref[a + k]
                pltpu.make_async_copy(ys_hbm.at[pl.ds(p, 1), :], gbuf.at[s, k, pl.ds(r, 1), :], sem.at[s]).start()
            return 0
        lax.fori_loop(0, COMB_T, body, 0)

    @pl.when(i == 0)
    def _():
        start_gather(0, 0)

    @pl.when(i + 1 < nsteps)
    def _():
        start_gather(i + 1, 1 - slot)

    for k in range(2):
        pltpu.make_async_copy(ys_hbm.at[pl.ds(0, COMB_T), :], gbuf.at[slot, k], sem.at[slot]).wait()
    rt = rt_ref[...]
    ff = rt[:, 2:3] * gbuf[slot, 0] + rt[:, 3:4] * gbuf[slot, 1]
    out = _layer_norm(DEEPNORM_ALPHA * x_ref[...] + ff, g_ref[...], b_ref[...])

    @pl.when(i < n_prompt_blocks)
    def _():
        op_ref[...] = out

    @pl.when(i >= n_prompt_blocks)
    def _():
        os_ref[...] = out


def _combine(pos, ys, x1, route, g, b, n_prompt_rows):
    rows = x1.shape[0]
    npb = n_prompt_rows // COMB_T
    grid_spec = pltpu.PrefetchScalarGridSpec(
        num_scalar_prefetch=1,
        grid=(rows // COMB_T,),
        in_specs=[pl.BlockSpec(memory_space=pl.ANY),
                  pl.BlockSpec((COMB_T, D_MODEL), lambda i, p: (i, 0)),
                  pl.BlockSpec((COMB_T, LANES), lambda i, p: (i, 0)),
                  pl.BlockSpec((1, D_MODEL), lambda i, p: (0, 0)),
                  pl.BlockSpec((1, D_MODEL), lambda i, p: (0, 0))],
        out_specs=[pl.BlockSpec((COMB_T, D_MODEL), lambda i, p: (jnp.minimum(i, npb - 1), 0)),
                   pl.BlockSpec((COMB_T, D_MODEL), lambda i, p: (jnp.maximum(i - npb, 0), 0))],
        scratch_shapes=[pltpu.VMEM((2, 2, COMB_T, D_MODEL), F32), pltpu.SemaphoreType.DMA((2,))],
    )
    return pl.pallas_call(
        functools.partial(_combine_body, n_prompt_blocks=npb),
        grid_spec=grid_spec,
        out_shape=[jax.ShapeDtypeStruct((n_prompt_rows, D_MODEL), F32),
                   jax.ShapeDtypeStruct((rows - n_prompt_rows, D_MODEL), F32)],
        compiler_params=_cparams(("arbitrary",)),
        name="combine_ln2",
    )(pos, ys, x1, route, g, b)


def kernel(x_prompt, x_sample, cache_k, cache_v, state_ssm_re, state_ssm_im, w_in, ssm_lambda_re, ssm_lambda_im, ssm_log_dt, ssm_b_re, ssm_b_im, ssm_c_re, ssm_c_im, ssm_d, ssm_w_glu, attn_sinks, attn_norm_g, ssm_norm_g, w_out, ln1_g, ln1_b, router_group_w, router_group_b, router_expert_w, router_expert_b, expert_w_gate, expert_w_up, expert_w_down, ln2_g, ln2_b):
    nbatch, seq, _ = x_prompt.shape
    nseq, dec_seq, _ = x_sample.shape
    wbuf = cache_k.shape[2]
    rows_p = nbatch * seq
    rows_s = nseq * dec_seq
    rows_all = rows_p + rows_s
    tm_p = 512

    xp = x_prompt.reshape(rows_p, D_MODEL)
    xs = x_sample.reshape(rows_s, D_MODEL)
    tab_p = _rope_tables(np.arange(seq))
    tab_s = _rope_tables(PAST_LEN + (np.arange(rows_s) % dec_seq))

    outs = {k: [] for k in ("kp", "vp", "hrp", "hip", "ks", "vs", "hrs", "his")}
    for l in range(DEPTH):
        w_in_bf = w_in[l].astype(BF16)
        w_glu_bf = ssm_w_glu[l].astype(BF16)
        w_out_bf = w_out[l].astype(BF16)
        sp = _ssm_params(ssm_lambda_re[l], ssm_lambda_im[l], ssm_log_dt[l], ssm_b_re[l], ssm_b_im[l],
                         ssm_c_re[l], ssm_c_im[l], ssm_d[l])
        sinks = attn_sinks[l].astype(F32)
        attn_g = attn_norm_g[l].astype(F32).reshape(1, ATTN_WIDTH)
        ssm_g = ssm_norm_g[l].astype(F32).reshape(1, SSM_WIDTH)
        g1 = ln1_g[l].astype(F32).reshape(1, D_MODEL)
        b1 = ln1_b[l].astype(F32).reshape(1, D_MODEL)
        g2 = ln2_g[l].astype(F32).reshape(1, D_MODEL)
        b2 = ln2_b[l].astype(F32).reshape(1, D_MODEL)
        n_rt = N_EXPERT_GROUPS + N_EXPERTS
        wr = jnp.concatenate([router_group_w[l].astype(F32)]
                             + [router_expert_w[l, g].astype(F32) for g in range(N_EXPERT_GROUPS)]
                             + [jnp.zeros((D_MODEL, LANES - n_rt), F32)], axis=1)
        br = jnp.concatenate([router_group_b[l].astype(F32), router_expert_b[l].astype(F32).reshape(-1),
                              jnp.zeros((LANES - n_rt,), F32)]).reshape(1, LANES)

        q_p, k_p, v_p, u_p = _inproj(xp, w_in_bf, *tab_p, tm=tm_p)
        q_s, k_s, v_s, u_s = _inproj(xs, w_in_bf, *tab_s, tm=rows_s)

        an_p = _attn_prompt(sinks, q_p, k_p, v_p, attn_g, seq)
        an_s, nk_s, nv_s = _attn_sample(sinks, q_s, k_s, v_s,
                                        cache_k[l].astype(F32).reshape(nseq, wbuf, KV_WIDTH),
                                        cache_v[l].astype(F32).reshape(nseq, wbuf, KV_WIDTH), attn_g, dec_seq)

        y_p, hl_p = _ssm_prompt(u_p, sp, nbatch, seq)
        y_s, hl_s = _ssm_sample(u_s, _pack_state(state_ssm_re[l], state_ssm_im[l]), sp, nseq, dec_seq)
        sn_p = _glu(y_p, w_glu_bf, ssm_g, tm_p)
        sn_s = _glu(y_s, w_glu_bf, ssm_g, rows_s)

        x1, lg_p = _outproj(an_p, sn_p, xp, w_out_bf, g1, b1, wr, br, tm_p, rows_all, 0)
        x1, lg_s = _outproj(an_s, sn_s, xs, w_out_bf, g1, b1, wr, br, rows_s, rows_all, rows_p // rows_s, x1_buf=x1)
        route = _route(jnp.concatenate([lg_p, lg_s], axis=0), tm=640)
        idx = route[:, 0:2].astype(jnp.int32)
        blk_e, nused, row_tok, pos = _moe_plan(idx, rows_all)

        ys = _moe(blk_e, nused, row_tok, x1, expert_w_gate[l], expert_w_up[l], expert_w_down[l])
        xp, xs = _combine(pos, ys, x1, route, g2, b2, rows_p)

        wp = min(WINDOW, seq)
        outs["kp"].append(k_p.reshape(nbatch, seq, N_KV_HEADS, HEAD_DIM)[:, seq - wp:])
        outs["vp"].append(v_p.reshape(nbatch, seq, N_KV_HEADS, HEAD_DIM)[:, seq - wp:])
        hp = hl_p.reshape(nbatch, SSM_CHUNKS * 2 * CH_STATES)
        hr, hi = _unpack_state(hp, nbatch)
        outs["hrp"].append(hr)
        outs["hip"].append(hi)
        outs["ks"].append(nk_s.reshape(nseq, wbuf, N_KV_HEADS, HEAD_DIM))
        outs["vs"].append(nv_s.reshape(nseq, wbuf, N_KV_HEADS, HEAD_DIM))
        hr, hi = _unpack_state(hl_s, nseq)
        outs["hrs"].append(hr)
        outs["his"].append(hi)

    st = {k: jnp.stack(v) for k, v in outs.items()}
    return (xp.reshape(nbatch, seq, D_MODEL), xs.reshape(nseq, dec_seq, D_MODEL),
            st["kp"], st["vp"], st["hrp"], st["hip"], st["ks"], st["vs"], st["hrs"], st["his"])
```

```python
import functools
import math

import numpy as np
import jax
import jax.numpy as jnp
from jax import lax
from jax.experimental import pallas as pl
from jax.experimental.pallas import tpu as pltpu

F32 = jnp.float32
BF16 = jnp.bfloat16

D_MODEL = 2048
DEPTH = 2
PAST_LEN = 16384
ATTN_WIDTH = 1024
SSM_WIDTH = 1024
HEAD_DIM = 64
N_HEADS = 16
N_KV_HEADS = 4
Q_PER_KV = 4
KV_WIDTH = 256
WINDOW = 128
ROPE_THETA = 10000.0
SSM_GROUP = 16
SSM_GROUPS = 64
SSM_STATE = 64
N_EXPERT_GROUPS = 4
EXPERTS_PER_GROUP = 8
N_EXPERTS = 32
EXPERT_FF = 512
MOE_BLOCK = 128
DEEPNORM_ALPHA = (2.0 * DEPTH) ** 0.25
LN_EPS = 1e-5
RMS_EPS = 1e-6

LANES = 128
SUBLANES = 8
VMEM_LIMIT = 56 * 1024 * 1024

SSM_CHUNKS = 4
CH_GROUPS = SSM_GROUPS // SSM_CHUNKS
CH_STATES = CH_GROUPS * SSM_STATE
CH_IN = CH_GROUPS * SSM_GROUP
SCAN_T = 512
SCAN_SEG = SCAN_T // SUBLANES


def _cparams(sem=None, vmem=VMEM_LIMIT, bounds_checks=True):
    return pltpu.CompilerParams(dimension_semantics=sem, vmem_limit_bytes=vmem,
                                disable_bounds_checks=not bounds_checks)


def _full(shape):
    n = len(shape)
    return pl.BlockSpec(shape, lambda *a: (0,) * n)


def _inproj_body(x_ref, w_ref, cos_ref, sa_ref, sb_ref, q_ref, k_ref, v_ref, u_ref):
    x = x_ref[...].astype(BF16)
    cos = cos_ref[...]
    sa = sa_ref[...]
    sb = sb_ref[...]

    def rope(c):
        return c * cos + pltpu.roll(c, LANES - 32, 1) * sa + pltpu.roll(c, 32, 1) * sb

    pq = jnp.dot(x, w_ref[:, 0:ATTN_WIDTH], preferred_element_type=F32)
    for j in range(ATTN_WIDTH // LANES):
        q_ref[:, j * LANES:(j + 1) * LANES] = rope(pq[:, j * LANES:(j + 1) * LANES]).astype(BF16)
    pk = jnp.dot(x, w_ref[:, ATTN_WIDTH:ATTN_WIDTH + KV_WIDTH], preferred_element_type=F32)
    for j in range(KV_WIDTH // LANES):
        k_ref[:, j * LANES:(j + 1) * LANES] = rope(pk[:, j * LANES:(j + 1) * LANES])
    v_ref[...] = jnp.dot(x, w_ref[:, ATTN_WIDTH + KV_WIDTH:ATTN_WIDTH + 2 * KV_WIDTH],
                         preferred_element_type=F32)
    u_ref[...] = jnp.dot(x, w_ref[:, ATTN_WIDTH + 2 * KV_WIDTH:], preferred_element_type=F32)


def _inproj(x, w_bf, cos_t, sa_t, sb_t, tm):
    rows = x.shape[0]
    tab_blocks = cos_t.shape[0] // tm
    in_w = w_bf.shape[1]
    row_spec = lambda w: pl.BlockSpec((tm, w), lambda i: (i, 0))
    tab_spec = pl.BlockSpec((tm, LANES), lambda i: (i % tab_blocks, 0))
    return pl.pallas_call(
        _inproj_body,
        grid=(rows // tm,),
        in_specs=[row_spec(D_MODEL), _full((D_MODEL, in_w)), tab_spec, tab_spec, tab_spec],
        out_specs=[row_spec(ATTN_WIDTH), row_spec(KV_WIDTH), row_spec(KV_WIDTH), row_spec(SSM_WIDTH)],
        out_shape=[jax.ShapeDtypeStruct((rows, ATTN_WIDTH), BF16),
                   jax.ShapeDtypeStruct((rows, KV_WIDTH), F32),
                   jax.ShapeDtypeStruct((rows, KV_WIDTH), F32),
                   jax.ShapeDtypeStruct((rows, SSM_WIDTH), F32)],
        compiler_params=_cparams(("arbitrary",)),
        name="inproj_rope",
    )(x, w_bf, cos_t, sa_t, sb_t)


def _rope_tables(positions):
    half = HEAD_DIM // 2
    inv_freq = ROPE_THETA ** (-np.arange(half, dtype=np.float64) / half)
    ang = positions.astype(np.float64)[:, None] * inv_freq[None, :]
    lane = np.arange(LANES)
    cos = np.cos(ang)[:, lane % half]
    sin = np.sin(ang)[:, lane % half]
    first = (lane % HEAD_DIM) < half
    sa = np.where(first[None, :], -sin, 0.0)
    sb = np.where(first[None, :], 0.0, sin)
    return (jnp.asarray(cos, F32), jnp.asarray(sa, F32), jnp.asarray(sb, F32))


def _rms(x, g):
    return x * lax.rsqrt(jnp.mean(jnp.square(x), axis=-1, keepdims=True) + RMS_EPS) * g


def _attn_prompt_body(sink_ref, q_ref, kp_ref, kc_ref, vp_ref, vc_ref, g_ref, o_ref, acc_ref, *, blocks_per_seq):
    i = pl.program_id(0)
    has_prev = (i % blocks_per_seq) > 0
    kk = jnp.concatenate([kp_ref[...], kc_ref[...]], axis=0).astype(BF16)
    vv = jnp.concatenate([vp_ref[...], vc_ref[...]], axis=0).astype(BF16)
    row = lax.broadcasted_iota(jnp.int32, (WINDOW, 2 * WINDOW), 0)
    col = lax.broadcasted_iota(jnp.int32, (WINDOW, 2 * WINDOW), 1)
    visible = (col > row) & (col <= row + WINDOW) & ((col >= WINDOW) | has_prev)
    q = q_ref[...]
    for h in range(N_HEADS):
        g = h // Q_PER_KV
        qh = q[:, h * HEAD_DIM:(h + 1) * HEAD_DIM]
        kh = kk[:, g * HEAD_DIM:(g + 1) * HEAD_DIM]
        vh = vv[:, g * HEAD_DIM:(g + 1) * HEAD_DIM]
        s = lax.dot_general(qh, kh, (((1,), (1,)), ((), ())), preferred_element_type=F32) * (HEAD_DIM ** -0.5)
        s = jnp.where(visible, s, -jnp.inf)
        sink = sink_ref[h]
        m = jnp.maximum(jnp.max(s, axis=-1, keepdims=True), sink)
        p = jnp.exp(s - m)
        denom = jnp.sum(p, axis=-1, keepdims=True) + jnp.exp(sink - m)
        oh = jnp.dot(p.astype(BF16), vh, preferred_element_type=F32) / denom
        acc_ref[:, h * HEAD_DIM:(h + 1) * HEAD_DIM] = oh
    o_ref[...] = _rms(acc_ref[...], g_ref[...]).astype(BF16)


def _attn_prompt(sinks, q, k, v, g, seq):
    rows = q.shape[0]
    nblk = rows // WINDOW
    bps = seq // WINDOW
    cur = lambda w: pl.BlockSpec((WINDOW, w), lambda i: (i, 0))
    prev = lambda w: pl.BlockSpec((WINDOW, w), lambda i: (jnp.maximum(i - 1, 0), 0))
    return pl.pallas_call(
        functools.partial(_attn_prompt_body, blocks_per_seq=bps),
        grid=(nblk,),
        in_specs=[pl.BlockSpec(memory_space=pltpu.SMEM), cur(ATTN_WIDTH), prev(KV_WIDTH), cur(KV_WIDTH),
                  prev(KV_WIDTH), cur(KV_WIDTH), _full((1, ATTN_WIDTH))],
        out_specs=cur(ATTN_WIDTH),
        out_shape=jax.ShapeDtypeStruct((rows, ATTN_WIDTH), BF16),
        scratch_shapes=[pltpu.VMEM((WINDOW, ATTN_WIDTH), F32)],
        compiler_params=_cparams(("arbitrary",)),
        name="attn_prompt",
    )(sinks, q, k, k, v, v, g)


SEQ_PER_STEP = 8


def _attn_sample_body(sink_ref, q_ref, kn_ref, vn_ref, ck_ref, cv_ref, g_ref, o_ref, nk_ref, nv_ref, acc_ref,
                      *, dec_seq):
    nq = SEQ_PER_STEP * dec_seq
    wb = ck_ref.shape[1]
    nkc = SEQ_PER_STEP * wb
    ck = ck_ref[...].reshape(nkc, KV_WIDTH)
    cv = cv_ref[...].reshape(nkc, KV_WIDTH)
    kn = kn_ref[...]
    vn = vn_ref[...]
    ckb = ck.astype(BF16)
    cvb = cv.astype(BF16)
    knb = kn.astype(BF16)
    vnb = vn.astype(BF16)
    q = q_ref[...]

    rows = Q_PER_KV * nq
    r = lax.broadcasted_iota(jnp.int32, (rows, nkc), 0) % nq
    c = lax.broadcasted_iota(jnp.int32, (rows, nkc), 1)
    vis_c = ((r // dec_seq) == (c // wb)) & ((c % wb) >= (r % dec_seq) + 1 + (wb - WINDOW))
    rn = lax.broadcasted_iota(jnp.int32, (rows, nq), 0) % nq
    cn = lax.broadcasted_iota(jnp.int32, (rows, nq), 1)
    vis_n = ((rn // dec_seq) == (cn // dec_seq)) & ((cn % dec_seq) <= (rn % dec_seq))
    hrow = lax.broadcasted_iota(jnp.int32, (rows, 1), 0) // nq

    dn = (((1,), (1,)), ((), ()))
    for g in range(N_KV_HEADS):
        qs = jnp.concatenate([q[:, (g * Q_PER_KV + j) * HEAD_DIM:(g * Q_PER_KV + j + 1) * HEAD_DIM]
                              for j in range(Q_PER_KV)], axis=0)
        sl = slice(g * HEAD_DIM, (g + 1) * HEAD_DIM)
        sc = lax.dot_general(qs, ckb[:, sl], dn, preferred_element_type=F32) * (HEAD_DIM ** -0.5)
        sn = lax.dot_general(qs, knb[:, sl], dn, preferred_element_type=F32) * (HEAD_DIM ** -0.5)
        sc = jnp.where(vis_c, sc, -jnp.inf)
        sn = jnp.where(vis_n, sn, -jnp.inf)
        sink = jnp.zeros((rows, 1), F32)
        for j in range(Q_PER_KV):
            sink = jnp.where(hrow == j, sink_ref[g * Q_PER_KV + j], sink)
        m = jnp.maximum(jnp.maximum(jnp.max(sc, axis=-1, keepdims=True), jnp.max(sn, axis=-1, keepdims=True)), sink)
        pc = jnp.exp(sc - m)
        pn = jnp.exp(sn - m)
        denom = jnp.sum(pc, axis=-1, keepdims=True) + jnp.sum(pn, axis=-1, keepdims=True) + jnp.exp(sink - m)
        o = (jnp.dot(pc.astype(BF16), cvb[:, sl], preferred_element_type=F32)
             + jnp.dot(pn.astype(BF16), vnb[:, sl], preferred_element_type=F32)) / denom
        for j in range(Q_PER_KV):
            h = g * Q_PER_KV + j
            acc_ref[:, h * HEAD_DIM:(h + 1) * HEAD_DIM] = o[j * nq:(j + 1) * nq, :]
    o_ref[...] = _rms(acc_ref[...], g_ref[...]).astype(BF16)

    keep = wb - dec_seq
    for s in range(SEQ_PER_STEP):
        nk_ref[s, 0:keep, :] = ck_ref[s, dec_seq:wb, :]
        nv_ref[s, 0:keep, :] = cv_ref[s, dec_seq:wb, :]
        nk_ref[s, keep:wb, :] = kn[s * dec_seq:(s + 1) * dec_seq, :]
        nv_ref[s, keep:wb, :] = vn[s * dec_seq:(s + 1) * dec_seq, :]


def _attn_sample(sinks, q, kn, vn, ck, cv, g, dec_seq, layer):
    rows = q.shape[0]
    _, nseq, wb, _ = ck.shape
    nq = SEQ_PER_STEP * dec_seq
    rowb = lambda w: pl.BlockSpec((nq, w), lambda i: (i, 0))
    cin = pl.BlockSpec((None, SEQ_PER_STEP, wb, KV_WIDTH), lambda i: (layer, i, 0, 0))
    cb = pl.BlockSpec((SEQ_PER_STEP, wb, KV_WIDTH), lambda i: (i, 0, 0))
    return pl.pallas_call(
        functools.partial(_attn_sample_body, dec_seq=dec_seq),
        grid=(nseq // SEQ_PER_STEP,),
        in_specs=[pl.BlockSpec(memory_space=pltpu.SMEM), rowb(ATTN_WIDTH), rowb(KV_WIDTH), rowb(KV_WIDTH),
                  cin, cin, _full((1, ATTN_WIDTH))],
        out_specs=[rowb(ATTN_WIDTH), cb, cb],
        out_shape=[jax.ShapeDtypeStruct((rows, ATTN_WIDTH), BF16),
                   jax.ShapeDtypeStruct(ck.shape[1:], F32), jax.ShapeDtypeStruct(cv.shape[1:], F32)],
        scratch_shapes=[pltpu.VMEM((nq, ATTN_WIDTH), F32)],
        compiler_params=_cparams(("arbitrary",)),
        name="attn_sample",
    )(sinks, q, kn, vn, ck, cv, g)


def _ssm_prompt_body(u_ref, b_ref, c_ref, lr_ref, li_ref, pr_ref, pi_ref, d_ref, y_ref, hl_ref,
                     uperm_ref, bu_ref, hb_ref, yp_ref, carry_ref):
    tc = pl.program_id(1)
    cc = pl.program_id(2)

    @pl.when(cc == 0)
    def _():
        for k in range(SSM_WIDTH // LANES):
            for s in range(SUBLANES):
                uperm_ref[k, pl.ds(s, SCAN_SEG, stride=SUBLANES), :] = (
                    u_ref[s * SCAN_SEG:(s + 1) * SCAN_SEG, k * LANES:(k + 1) * LANES])

    @pl.when(tc == 0)
    def _():
        carry_ref[cc] = jnp.zeros((2, CH_STATES), F32)

    lanes_per_chunk = CH_IN // LANES
    up = jnp.concatenate([uperm_ref[cc * lanes_per_chunk + k] for k in range(lanes_per_chunk)], axis=1)
    bu_ref[...] = jnp.dot(up.astype(BF16), b_ref[cc], preferred_element_type=F32)

    ar = jnp.broadcast_to(lr_ref[cc], (SUBLANES, CH_STATES))
    ai = jnp.broadcast_to(li_ref[cc], (SUBLANES, CH_STATES))

    def step(j, h):
        hr, hi = h
        r0 = pl.multiple_of(j * SUBLANES, SUBLANES)
        br = bu_ref[pl.ds(r0, SUBLANES), 0:CH_STATES]
        bi = bu_ref[pl.ds(r0, SUBLANES), CH_STATES:2 * CH_STATES]
        nr = ar * hr + (br - ai * hi)
        ni = ar * hi + (bi + ai * hr)
        bu_ref[pl.ds(r0, SUBLANES), 0:CH_STATES] = nr
        bu_ref[pl.ds(r0, SUBLANES), CH_STATES:2 * CH_STATES] = ni
        return nr, ni

    zero = jnp.zeros((SUBLANES, CH_STATES), F32)
    fr, fi = lax.fori_loop(0, SCAN_SEG, step, (zero, zero))

    cst = carry_ref[cc]
    c_r = cst[0:1, :]
    c_i = cst[1:2, :]
    lsr = pr_ref[cc, SCAN_SEG - 1:SCAN_SEG, :]
    lsi = pi_ref[cc, SCAN_SEG - 1:SCAN_SEG, :]
    rows_r, rows_i = [], []
    for s in range(SUBLANES):
        rows_r.append(c_r)
        rows_i.append(c_i)
        n_r = fr[s:s + 1, :] + (lsr * c_r - lsi * c_i)
        n_i = fi[s:s + 1, :] + (lsr * c_i + lsi * c_r)
        c_r, c_i = n_r, n_i
    carry_ref[cc] = jnp.concatenate([c_r, c_i], axis=0)
    hl_ref[0, cc] = jnp.concatenate([c_r, c_i], axis=0)
    cm_r = jnp.concatenate(rows_r, axis=0)
    cm_i = jnp.concatenate(rows_i, axis=0)

    def fix(jj, _):
        r0 = pl.multiple_of(jj * 2 * SUBLANES, 2 * SUBLANES)
        outs_r, outs_i = [], []
        for half in range(2):
            j = jj * 2 + half
            pr = pr_ref[cc, pl.ds(j, 1), :]
            pi = pi_ref[cc, pl.ds(j, 1), :]
            rr = r0 + half * SUBLANES
            h_r = bu_ref[pl.ds(rr, SUBLANES), 0:CH_STATES]
            h_i = bu_ref[pl.ds(rr, SUBLANES), CH_STATES:2 * CH_STATES]
            outs_r.append(h_r + (pr * cm_r - pi * cm_i))
            outs_i.append(h_i + (pr * cm_i + pi * cm_r))
        hb_ref[pl.ds(r0, 2 * SUBLANES), 0:CH_STATES] = jnp.concatenate(outs_r, axis=0).astype(BF16)
        hb_ref[pl.ds(r0, 2 * SUBLANES), CH_STATES:2 * CH_STATES] = jnp.concatenate(outs_i, axis=0).astype(BF16)
        return 0

    lax.fori_loop(0, SCAN_SEG // 2, fix, 0)

    yp = jnp.dot(hb_ref[...], c_ref[cc], preferred_element_type=F32) + d_ref[cc] * up
    for k in range(lanes_per_chunk):
        yp_ref[k] = yp[:, k * LANES:(k + 1) * LANES]
    for s in range(SUBLANES):
        for k in range(lanes_per_chunk):
            y_ref[s * SCAN_SEG:(s + 1) * SCAN_SEG, k * LANES:(k + 1) * LANES] = (
                yp_ref[k, pl.ds(s, SCAN_SEG, stride=SUBLANES), :])


def _ssm_prompt(u, sp, nbatch, seq):
    rows = u.shape[0]
    ntc = seq // SCAN_T
    y, hl = pl.pallas_call(
        _ssm_prompt_body,
        grid=(nbatch, ntc, SSM_CHUNKS),
        in_specs=[pl.BlockSpec((SCAN_T, SSM_WIDTH), lambda b, t, c: (b * ntc + t, 0)),
                  _full(sp["B"].shape), _full(sp["C"].shape), _full(sp["lr"].shape), _full(sp["li"].shape),
                  _full(sp["pr"].shape), _full(sp["pi"].shape), _full(sp["d"].shape)],
        out_specs=[pl.BlockSpec((SCAN_T, CH_IN), lambda b, t, c: (b * ntc + t, c)),
                   pl.BlockSpec((1, SSM_CHUNKS, 2, CH_STATES), lambda b, t, c: (b, 0, 0, 0))],
        out_shape=[jax.ShapeDtypeStruct((rows, SSM_WIDTH), F32),
                   jax.ShapeDtypeStruct((nbatch, SSM_CHUNKS, 2, CH_STATES), F32)],
        scratch_shapes=[pltpu.VMEM((SSM_WIDTH // LANES, SCAN_T, LANES), F32),
                        pltpu.VMEM((SCAN_T, 2 * CH_STATES), F32),
                        pltpu.VMEM((SCAN_T, 2 * CH_STATES), BF16),
                        pltpu.VMEM((CH_IN // LANES, SCAN_T, LANES), F32),
                        pltpu.VMEM((SSM_CHUNKS, 2, CH_STATES), F32)],
        compiler_params=_cparams(("arbitrary", "arbitrary", "arbitrary")),
        name="ssm_prompt",
    )(u, sp["B"], sp["C"], sp["lr"], sp["li"], sp["pr"], sp["pi"], sp["d"])
    return y, hl


def _ssm_sample_body(u_ref, h0_ref, b_ref, c_ref, lr_ref, li_ref, d_ref, y_ref, ho_ref, bu_ref, *, nseq, dec_seq):
    nl = CH_STATES // LANES
    for k in range(SSM_CHUNKS):
        uk = u_ref[:, k * CH_IN:(k + 1) * CH_IN]
        bu = jnp.dot(uk.astype(BF16), b_ref[k], preferred_element_type=F32)
        for c in range(2 * nl):
            bu_ref[c] = bu[:, c * LANES:(c + 1) * LANES]
        base = k * 2 * CH_STATES
        hr = h0_ref[:, base:base + CH_STATES]
        hi = h0_ref[:, base + CH_STATES:base + 2 * CH_STATES]
        ar = lr_ref[k]
        ai = li_ref[k]
        for t in range(dec_seq):
            br = jnp.concatenate([bu_ref[c, pl.ds(t, nseq, stride=dec_seq), :] for c in range(nl)], axis=1)
            bi = jnp.concatenate([bu_ref[nl + c, pl.ds(t, nseq, stride=dec_seq), :] for c in range(nl)], axis=1)
            hr, hi = ar * hr + (br - ai * hi), ar * hi + (bi + ai * hr)
            for c in range(nl):
                bu_ref[c, pl.ds(t, nseq, stride=dec_seq), :] = hr[:, c * LANES:(c + 1) * LANES]
                bu_ref[nl + c, pl.ds(t, nseq, stride=dec_seq), :] = hi[:, c * LANES:(c + 1) * LANES]
        ho_ref[:, base:base + CH_STATES] = hr
        ho_ref[:, base + CH_STATES:base + 2 * CH_STATES] = hi
        hall = jnp.concatenate([bu_ref[c] for c in range(2 * nl)], axis=1)
        y_ref[:, k * CH_IN:(k + 1) * CH_IN] = (
            jnp.dot(hall.astype(BF16), c_ref[k], preferred_element_type=F32) + d_ref[k] * uk)


def _ssm_sample(u, h0, sp, nseq, dec_seq):
    rows = u.shape[0]
    return pl.pallas_call(
        functools.partial(_ssm_sample_body, nseq=nseq, dec_seq=dec_seq),
        grid=(1,),
        in_specs=[_full(u.shape), _full(h0.shape), _full(sp["B"].shape), _full(sp["C"].shape),
                  _full(sp["lr"].shape), _full(sp["li"].shape), _full(sp["d"].shape)],
        out_specs=[_full((rows, SSM_WIDTH)), _full(h0.shape)],
        out_shape=[jax.ShapeDtypeStruct((rows, SSM_WIDTH), F32), jax.ShapeDtypeStruct(h0.shape, F32)],
        scratch_shapes=[pltpu.VMEM((2 * CH_STATES // LANES, rows, LANES), F32)],
        compiler_params=_cparams(("arbitrary",)),
        name="ssm_sample",
    )(u, h0, sp["B"], sp["C"], sp["lr"], sp["li"], sp["d"])


def _ssm_params(lambda_re, lambda_im, log_dt, b_re, b_im, c_re, c_im, d):
    lam = lax.complex(jnp.minimum(lambda_re.astype(F32), -1e-4), lambda_im.astype(F32))
    dt = jnp.exp(log_dt.astype(F32))[:, None]
    lam_dt = lam * dt
    lam_bar = jnp.exp(lam_dt)
    b = lax.complex(b_re.astype(F32), b_im.astype(F32))
    b_bar = ((lam_bar - 1.0) / lam)[..., None] * b
    eye = jnp.eye(CH_GROUPS, dtype=F32)

    def pack_b(part):
        x = part.reshape(SSM_CHUNKS, CH_GROUPS, SSM_STATE, SSM_GROUP)
        return jnp.einsum("kgpc,gh->kgchp", x, eye).reshape(SSM_CHUNKS, CH_IN, CH_STATES)

    def pack_c(part):
        x = part.reshape(SSM_CHUNKS, CH_GROUPS, SSM_GROUP, SSM_STATE)
        return jnp.einsum("kgcp,gh->kgphc", x, eye).reshape(SSM_CHUNKS, CH_STATES, CH_IN)

    bmat = jnp.concatenate([pack_b(jnp.real(b_bar)), pack_b(jnp.imag(b_bar))], axis=2).astype(BF16)
    cmat = jnp.concatenate([pack_c(c_re.astype(F32)), pack_c(-c_im.astype(F32))], axis=1).astype(BF16)
    steps = jnp.arange(1, SCAN_SEG + 1, dtype=F32)[:, None, None]
    pw = jnp.exp(lam_dt[None] * steps)
    pw = pw.reshape(SCAN_SEG, SSM_CHUNKS, CH_STATES).transpose(1, 0, 2)
    lb = lam_bar.reshape(SSM_CHUNKS, 1, CH_STATES)
    return dict(B=bmat, C=cmat, lr=jnp.real(lb), li=jnp.imag(lb), pr=jnp.real(pw), pi=jnp.imag(pw),
                d=d.astype(F32).reshape(SSM_CHUNKS, 1, CH_IN))


def _pack_state(re, im):
    n = re.shape[0]
    st = jnp.stack([re.reshape(n, SSM_CHUNKS, CH_STATES), im.reshape(n, SSM_CHUNKS, CH_STATES)], axis=2)
    return st.reshape(n, SSM_CHUNKS * 2 * CH_STATES).astype(F32)


def _unpack_state(h, n):
    st = h.reshape(n, SSM_CHUNKS, 2, CH_GROUPS, SSM_STATE)
    return (st[:, :, 0].reshape(n, SSM_GROUPS, SSM_STATE), st[:, :, 1].reshape(n, SSM_GROUPS, SSM_STATE))


def _glu_body(y_ref, w_ref, g_ref, o_ref):
    a = jax.nn.gelu(y_ref[...])
    z = a * jax.nn.sigmoid(jnp.dot(a.astype(BF16), w_ref[...], preferred_element_type=F32))
    o_ref[...] = _rms(z, g_ref[...]).astype(BF16)


def _glu(y, w_bf, g, tm):
    rows = y.shape[0]
    spec = pl.BlockSpec((tm, SSM_WIDTH), lambda i: (i, 0))
    return pl.pallas_call(
        _glu_body,
        grid=(rows // tm,),
        in_specs=[spec, _full((SSM_WIDTH, SSM_WIDTH)), _full((1, SSM_WIDTH))],
        out_specs=spec,
        out_shape=jax.ShapeDtypeStruct((rows, SSM_WIDTH), BF16),
        compiler_params=_cparams(("arbitrary",)),
        name="glu_rms",
    )(y, w_bf, g)


def _layer_norm(y, g, b):
    mu = jnp.mean(y, axis=-1, keepdims=True)
    yc = y - mu
    var = jnp.mean(jnp.square(yc), axis=-1, keepdims=True)
    return yc * lax.rsqrt(var + LN_EPS) * g + b


def _outproj_body(*refs, aliased, nblk):
    if aliased:
        refs = refs[1:]
    a_ref, s_ref, x_ref, w_ref, g_ref, b_ref, wrh_ref, wrl_ref, br_ref, o_ref, lg_ref = refs
    i = pl.program_id(0)

    @pl.when(i < nblk)
    def _():
        acc = jnp.dot(a_ref[...], w_ref[0:ATTN_WIDTH, :], preferred_element_type=F32)
        acc = acc + jnp.dot(s_ref[...], w_ref[ATTN_WIDTH:, :], preferred_element_type=F32)
        x1 = _layer_norm(DEEPNORM_ALPHA * x_ref[...] + acc, g_ref[...], b_ref[...])
        o_ref[...] = x1
        xh = x1.astype(BF16)
        xl = (x1 - xh.astype(F32)).astype(BF16)
        lg = jnp.dot(xh, wrh_ref[...], preferred_element_type=F32)
        lg = lg + (jnp.dot(xl, wrh_ref[...], preferred_element_type=F32)
                   + jnp.dot(xh, wrl_ref[...], preferred_element_type=F32))
        lg_ref[...] = lg + br_ref[...]

    @pl.when(i >= nblk)
    def _():
        o_ref[...] = jnp.zeros(o_ref.shape, F32)


def _outproj(an, sn, x, w_bf, g, b, wr, br, tm, total_rows, row_block0, x1_buf=None):
    rows = an.shape[0]
    aliased = x1_buf is not None
    nblk = rows // tm
    tail = 0 if aliased or rows == total_rows else 1
    blk = lambda i: jnp.minimum(i, nblk - 1)
    half = lambda: pl.BlockSpec((tm, ATTN_WIDTH), lambda i: (blk(i), 0))
    in_specs = [half(), half(), pl.BlockSpec((tm, D_MODEL), lambda i: (blk(i), 0)),
                _full((D_MODEL, D_MODEL)), _full((1, D_MODEL)), _full((1, D_MODEL)),
                _full((D_MODEL, LANES)), _full((D_MODEL, LANES)), _full((1, LANES))]
    wr_hi = wr.astype(BF16)
    wr_lo = (wr - wr_hi.astype(F32)).astype(BF16)
    args = [an, sn, x, w_bf, g, b, wr_hi, wr_lo, br]
    if aliased:
        in_specs = [pl.BlockSpec(memory_space=pl.ANY)] + in_specs
        args = [x1_buf] + args
    return pl.pallas_call(
        functools.partial(_outproj_body, aliased=aliased, nblk=nblk),
        grid=(nblk + tail,),
        in_specs=in_specs,
        out_specs=[pl.BlockSpec((tm, D_MODEL), lambda i: (i + row_block0, 0)),
                   pl.BlockSpec((tm, LANES), lambda i: (blk(i), 0))],
        out_shape=[jax.ShapeDtypeStruct((total_rows, D_MODEL), F32), jax.ShapeDtypeStruct((rows, LANES), F32)],
        input_output_aliases={0: 0} if aliased else {},
        compiler_params=_cparams(("arbitrary",)),
        name="outproj_ln1",
    )(*args)


ROUTE_T = 640
R_E0, R_E1, R_W0, R_W1, R_P0, R_P1 = range(6)


def _route_select(x):
    lane = lax.broadcasted_iota(jnp.int32, x.shape, 1)
    big = jnp.int32(1 << 20)
    neg = -jnp.inf
    gmask = lane < N_EXPERT_GROUPS
    gm = jnp.max(jnp.where(gmask, x, neg), axis=-1, keepdims=True)
    grp = jnp.min(jnp.where(gmask & (x == gm), lane, big), axis=-1, keepdims=True)
    p_grp = 1.0 / jnp.sum(jnp.where(gmask, jnp.exp(x - gm), 0.0), axis=-1, keepdims=True)
    lo = N_EXPERT_GROUPS + grp * EXPERTS_PER_GROUP
    emask = (lane >= lo) & (lane < lo + EXPERTS_PER_GROUP)
    v1 = jnp.max(jnp.where(emask, x, neg), axis=-1, keepdims=True)
    i1 = jnp.min(jnp.where(emask & (x == v1), lane, big), axis=-1, keepdims=True)
    emask2 = emask & (lane != i1)
    v2 = jnp.max(jnp.where(emask2, x, neg), axis=-1, keepdims=True)
    i2 = jnp.min(jnp.where(emask2 & (x == v2), lane, big), axis=-1, keepdims=True)
    e2 = jnp.exp(v2 - v1)
    w1 = p_grp / (1.0 + e2)
    w2 = p_grp * e2 / (1.0 + e2)
    return (i1 - N_EXPERT_GROUPS).astype(F32), (i2 - N_EXPERT_GROUPS).astype(F32), w1, w2


def _route_plan_body(lg_ref, ltri_ref, utri_ref, o_ref, meta_ref, rec_ref, run_ref):
    ph = pl.program_id(0)
    j = pl.program_id(1)
    tm = lg_ref.shape[0]
    lane = lax.broadcasted_iota(jnp.int32, (tm, LANES), 1)
    lanef = lane.astype(F32)
    r0 = pl.multiple_of(j * tm, tm)

    @pl.when((ph == 0) & (j == 0))
    def _():
        run_ref[...] = jnp.zeros(run_ref.shape, F32)

    @pl.when(ph == 0)
    def _():
        e0, e1, w0, w1 = _route_select(lg_ref[...])
        oh0 = (lanef == e0).astype(F32)
        oh1 = (lanef == e1).astype(F32)
        oh2 = oh0 + oh1
        before = jnp.dot(ltri_ref[...], oh2.astype(BF16), preferred_element_type=F32) + run_ref[0:1, :]
        rk0 = jnp.sum(before * oh0, axis=-1, keepdims=True)
        rk1 = jnp.sum(before * oh1, axis=-1, keepdims=True)
        run_ref[...] = run_ref[...] + jnp.sum(oh2, axis=0, keepdims=True)
        rec = jnp.where(lane == R_E0, e0, 0.0)
        rec = jnp.where(lane == R_E1, e1, rec)
        rec = jnp.where(lane == R_W0, w0, rec)
        rec = jnp.where(lane == R_W1, w1, rec)
        rec = jnp.where(lane == R_P0, rk0, rec)
        rec = jnp.where(lane == R_P1, rk1, rec)
        rec_ref[pl.ds(r0, tm), :] = rec

    @pl.when(ph == 1)
    def _():
        counts = run_ref[...]
        nblk = jnp.floor((counts + (MOE_BLOCK - 1)) * (1.0 / MOE_BLOCK))
        end_blk = jnp.dot(nblk.astype(BF16), utri_ref[...], preferred_element_type=F32)
        start_row = (end_blk - nblk) * MOE_BLOCK
        rec = rec_ref[pl.ds(r0, tm), :]
        oh0 = (lanef == rec[:, R_E0:R_E0 + 1]).astype(F32)
        oh1 = (lanef == rec[:, R_E1:R_E1 + 1]).astype(F32)
        p0 = jnp.sum(oh0 * start_row[0:1, :], axis=-1, keepdims=True) + rec[:, R_P0:R_P0 + 1]
        p1 = jnp.sum(oh1 * start_row[0:1, :], axis=-1, keepdims=True) + rec[:, R_P1:R_P1 + 1]
        out = jnp.where(lane == R_P0, p0, rec)
        out = jnp.where(lane == R_P1, p1, out)
        o_ref[...] = out

        @pl.when(j == 0)
        def _():
            mrows = meta_ref.shape[0]
            nused = end_blk[0:1, N_EXPERTS - 1:N_EXPERTS]
            blk = lax.broadcasted_iota(jnp.int32, (mrows, LANES), 0).astype(F32)
            blk = jnp.minimum(blk, nused - 1.0)
            mlane = lax.broadcasted_iota(jnp.int32, (mrows, LANES), 1)
            done = jnp.where((mlane < N_EXPERTS) & (end_blk[0:1, :] <= blk), 1.0, 0.0)
            blk_e = jnp.minimum(jnp.sum(done, axis=-1, keepdims=True), N_EXPERTS - 1.0)
            meta_ref[...] = jnp.where(mlane == 0, blk_e, jnp.where(mlane == 1, nused, 0.0))


def _route_plan(logits, nblocks):
    rows = logits.shape[0]
    tm = ROUTE_T
    ltri = jnp.asarray(np.tril(np.ones((tm, tm), np.float32), -1), BF16)
    utri = jnp.asarray(np.triu(np.ones((LANES, LANES), np.float32)), BF16)
    mrows = -(-nblocks // SUBLANES) * SUBLANES
    return pl.pallas_call(
        _route_plan_body,
        grid=(2, rows // tm),
        in_specs=[pl.BlockSpec((tm, LANES), lambda p, j: (j * (1 - p), 0)),
                  pl.BlockSpec((tm, tm), lambda p, j: (0, 0)),
                  pl.BlockSpec((LANES, LANES), lambda p, j: (0, 0))],
        out_specs=[pl.BlockSpec((tm, LANES), lambda p, j: (j * p, 0)),
                   pl.BlockSpec((mrows, LANES), lambda p, j: (0, 0))],
        out_shape=[jax.ShapeDtypeStruct((rows, LANES), F32), jax.ShapeDtypeStruct((mrows, LANES), F32)],
        scratch_shapes=[pltpu.VMEM((rows, LANES), F32), pltpu.VMEM((SUBLANES, LANES), F32)],
        compiler_params=_cparams(("arbitrary", "arbitrary")),
        name="route_plan",
    )(logits, ltri, utri)


def _rowmap_body(pos_ref, o_ref, *, ntok, nrows):
    unroll = 8

    def zero(c, _):
        for u in range(unroll):
            o_ref[c * unroll + u] = 0
        return 0
    lax.fori_loop(0, nrows // unroll, zero, 0)

    def fill(c, _):
        for u in range(unroll):
            t = c * unroll + u
            o_ref[pos_ref[2 * t]] = t
            o_ref[pos_ref[2 * t + 1]] = t
        return 0
    lax.fori_loop(0, ntok // unroll, fill, 0)


def _rowmap(pos, nrows):
    ntok = pos.shape[0] // 2
    return pl.pallas_call(
        functools.partial(_rowmap_body, ntok=ntok, nrows=nrows),
        in_specs=[pl.BlockSpec(memory_space=pltpu.SMEM)],
        out_specs=pl.BlockSpec(memory_space=pltpu.SMEM),
        out_shape=jax.ShapeDtypeStruct((nrows,), jnp.int32),
        name="moe_rowmap",
    )(pos)


GATHER_UNROLL = 8


def _moe_body(blk_e_ref, nused_ref, row_tok_ref, x_hbm, wg_ref, wu_ref, wd_ref, o_ref,
              xbuf, sem, wg_bf, wu_bf, wd_bf):
    i = pl.program_id(0)
    nused = nused_ref[0]
    slot = i % 2

    def row_copy(blk, r, s):
        tok = row_tok_ref[blk * MOE_BLOCK + r]
        return pltpu.make_async_copy(x_hbm.at[pl.ds(tok, 1), :], xbuf.at[s, pl.ds(r, 1), :], sem.at[s])

    def start_gather(blk, s):
        def body(c, _):
            for u in range(GATHER_UNROLL):
                row_copy(blk, c * GATHER_UNROLL + u, s).start()
            return 0
        lax.fori_loop(0, MOE_BLOCK // GATHER_UNROLL, body, 0)

    @pl.when(i == 0)
    def _():
        start_gather(0, 0)

    @pl.when(i + 1 < nused)
    def _():
        start_gather(i + 1, 1 - slot)

    @pl.when(i < nused)
    def _():
        changed = (i == 0) | (blk_e_ref[i] != blk_e_ref[jnp.maximum(i - 1, 0)])

        @pl.when(changed)
        def _():
            wg_bf[...] = wg_ref[...].astype(BF16)
            wu_bf[...] = wu_ref[...].astype(BF16)
            wd_bf[...] = wd_ref[...].astype(BF16)

        pltpu.make_async_copy(x_hbm.at[pl.ds(0, MOE_BLOCK), :], xbuf.at[slot], sem.at[slot]).wait()
        x = xbuf[slot].astype(BF16)
        hg = jnp.dot(x, wg_bf[...], preferred_element_type=F32)
        hu = jnp.dot(x, wu_bf[...], preferred_element_type=F32)
        h = (jax.nn.silu(hg) * hu).astype(BF16)
        o_ref[...] = jnp.dot(h, wd_bf[...], preferred_element_type=F32)

    @pl.when(i >= nused)
    def _():
        o_ref[...] = jnp.zeros(o_ref.shape, F32)


def _moe(blk_e, nused, row_tok, x1, w_gate, w_up, w_down, layer):
    nb = blk_e.shape[0]
    grid_spec = pltpu.PrefetchScalarGridSpec(
        num_scalar_prefetch=3,
        grid=(nb,),
        in_specs=[pl.BlockSpec(memory_space=pl.ANY),
                  pl.BlockSpec((None, None, D_MODEL, EXPERT_FF), lambda i, be, nu, rt: (layer, be[i], 0, 0)),
                  pl.BlockSpec((None, None, D_MODEL, EXPERT_FF), lambda i, be, nu, rt: (layer, be[i], 0, 0)),
                  pl.BlockSpec((None, None, EXPERT_FF, D_MODEL), lambda i, be, nu, rt: (layer, be[i], 0, 0))],
        out_specs=pl.BlockSpec((MOE_BLOCK, D_MODEL), lambda i, be, nu, rt: (i, 0)),
        scratch_shapes=[pltpu.VMEM((2, MOE_BLOCK, D_MODEL), F32),
                        pltpu.SemaphoreType.DMA((2,)),
                        pltpu.VMEM((D_MODEL, EXPERT_FF), BF16),
                        pltpu.VMEM((D_MODEL, EXPERT_FF), BF16),
                        pltpu.VMEM((EXPERT_FF, D_MODEL), BF16)],
    )
    return pl.pallas_call(
        _moe_body,
        grid_spec=grid_spec,
        out_shape=jax.ShapeDtypeStruct((nb * MOE_BLOCK, D_MODEL), F32),
        compiler_params=_cparams(("arbitrary",), bounds_checks=False),
        name="moe_experts",
    )(blk_e, nused, row_tok, x1, w_gate, w_up, w_down)


COMB_T = 128


def _combine_body(pos_ref, ys_hbm, x_ref, rt_ref, g_ref, b_ref, op_ref, os_ref, gbuf, sem, *, n_prompt_blocks):
    i = pl.program_id(0)
    nsteps = pl.num_programs(0)
    slot = i % 2

    def start_gather(step, s):
        def body(c, _):
            for u in range(GATHER_UNROLL // 2):
                r = c * (GATHER_UNROLL // 2) + u
                a = (step * COMB_T + r) * 2
                for k in range(2):
                    p = pos_ref[a + k]
                    pltpu.make_async_copy(ys_hbm.at[pl.ds(p, 1), :], gbuf.at[s, k, pl.ds(r, 1), :],
                                          sem.at[s]).start()
            return 0
        lax.fori_loop(0, COMB_T // (GATHER_UNROLL // 2), body, 0)

    @pl.when(i == 0)
    def _():
        start_gather(0, 0)

    @pl.when(i + 1 < nsteps)
    def _():
        start_gather(i + 1, 1 - slot)

    for k in range(2):
        pltpu.make_async_copy(ys_hbm.at[pl.ds(0, COMB_T), :], gbuf.at[slot, k], sem.at[slot]).wait()
    rt = rt_ref[...]
    ff = rt[:, R_W0:R_W0 + 1] * gbuf[slot, 0] + rt[:, R_W1:R_W1 + 1] * gbuf[slot, 1]
    out = _layer_norm(DEEPNORM_ALPHA * x_ref[...] + ff, g_ref[...], b_ref[...])

    @pl.when(i < n_prompt_blocks)
    def _():
        op_ref[...] = out

    @pl.when(i >= n_prompt_blocks)
    def _():
        os_ref[...] = out


def _combine(pos, ys, x1, route, g, b, n_prompt_rows):
    rows = x1.shape[0]
    npb = n_prompt_rows // COMB_T
    grid_spec = pltpu.PrefetchScalarGridSpec(
        num_scalar_prefetch=1,
        grid=(rows // COMB_T,),
        in_specs=[pl.BlockSpec(memory_space=pl.ANY),
                  pl.BlockSpec((COMB_T, D_MODEL), lambda i, p: (i, 0)),
                  pl.BlockSpec((COMB_T, LANES), lambda i, p: (i, 0)),
                  pl.BlockSpec((1, D_MODEL), lambda i, p: (0, 0)),
                  pl.BlockSpec((1, D_MODEL), lambda i, p: (0, 0))],
        out_specs=[pl.BlockSpec((COMB_T, D_MODEL), lambda i, p: (jnp.minimum(i, npb - 1), 0)),
                   pl.BlockSpec((COMB_T, D_MODEL), lambda i, p: (jnp.maximum(i - npb, 0), 0))],
        scratch_shapes=[pltpu.VMEM((2, 2, COMB_T, D_MODEL), F32), pltpu.SemaphoreType.DMA((2,))],
    )
    return pl.pallas_call(
        functools.partial(_combine_body, n_prompt_blocks=npb),
        grid_spec=grid_spec,
        out_shape=[jax.ShapeDtypeStruct((n_prompt_rows, D_MODEL), F32),
                   jax.ShapeDtypeStruct((rows - n_prompt_rows, D_MODEL), F32)],
        compiler_params=_cparams(("arbitrary",), bounds_checks=False),
        name="combine_ln2",
    )(pos, ys, x1, route, g, b)


def kernel(x_prompt, x_sample, cache_k, cache_v, state_ssm_re, state_ssm_im, w_in, ssm_lambda_re, ssm_lambda_im, ssm_log_dt, ssm_b_re, ssm_b_im, ssm_c_re, ssm_c_im, ssm_d, ssm_w_glu, attn_sinks, attn_norm_g, ssm_norm_g, w_out, ln1_g, ln1_b, router_group_w, router_group_b, router_expert_w, router_expert_b, expert_w_gate, expert_w_up, expert_w_down, ln2_g, ln2_b):
    nbatch, seq, _ = x_prompt.shape
    nseq, dec_seq, _ = x_sample.shape
    wbuf = cache_k.shape[2]
    rows_p = nbatch * seq
    rows_s = nseq * dec_seq
    rows_all = rows_p + rows_s
    tm_p = 512

    xp = x_prompt.reshape(rows_p, D_MODEL)
    xs = x_sample.reshape(rows_s, D_MODEL)
    tab_p = _rope_tables(np.arange(seq))
    tab_s = _rope_tables(PAST_LEN + (np.arange(rows_s) % dec_seq))
    ck_all = cache_k.astype(F32).reshape(DEPTH, nseq, wbuf, KV_WIDTH)
    cv_all = cache_v.astype(F32).reshape(DEPTH, nseq, wbuf, KV_WIDTH)

    outs = {k: [] for k in ("kp", "vp", "hrp", "hip", "ks", "vs", "hrs", "his")}
    for l in range(DEPTH):
        w_in_bf = w_in[l].astype(BF16)
        w_glu_bf = ssm_w_glu[l].astype(BF16)
        w_out_bf = w_out[l].astype(BF16)
        sp = _ssm_params(ssm_lambda_re[l], ssm_lambda_im[l], ssm_log_dt[l], ssm_b_re[l], ssm_b_im[l],
                         ssm_c_re[l], ssm_c_im[l], ssm_d[l])
        sinks = attn_sinks[l].astype(F32)
        attn_g = attn_norm_g[l].astype(F32).reshape(1, ATTN_WIDTH)
        ssm_g = ssm_norm_g[l].astype(F32).reshape(1, SSM_WIDTH)
        g1 = ln1_g[l].astype(F32).reshape(1, D_MODEL)
        b1 = ln1_b[l].astype(F32).reshape(1, D_MODEL)
        g2 = ln2_g[l].astype(F32).reshape(1, D_MODEL)
        b2 = ln2_b[l].astype(F32).reshape(1, D_MODEL)
        n_rt = N_EXPERT_GROUPS + N_EXPERTS
        wr = jnp.concatenate([router_group_w[l].astype(F32)]
                             + [router_expert_w[l, g].astype(F32) for g in range(N_EXPERT_GROUPS)]
                             + [jnp.zeros((D_MODEL, LANES - n_rt), F32)], axis=1)
        br = jnp.concatenate([router_group_b[l].astype(F32), router_expert_b[l].astype(F32).reshape(-1),
                              jnp.zeros((LANES - n_rt,), F32)]).reshape(1, LANES)

        q_p, k_p, v_p, u_p = _inproj(xp, w_in_bf, *tab_p, tm=tm_p)
        q_s, k_s, v_s, u_s = _inproj(xs, w_in_bf, *tab_s, tm=rows_s)

        an_p = _attn_prompt(sinks, q_p, k_p, v_p, attn_g, seq)
        an_s, nk_s, nv_s = _attn_sample(sinks, q_s, k_s, v_s, ck_all, cv_all, attn_g, dec_seq, l)

        y_p, hl_p = _ssm_prompt(u_p, sp, nbatch, seq)
        y_s, hl_s = _ssm_sample(u_s, _pack_state(state_ssm_re[l], state_ssm_im[l]), sp, nseq, dec_seq)
        sn_p = _glu(y_p, w_glu_bf, ssm_g, tm_p)
        sn_s = _glu(y_s, w_glu_bf, ssm_g, rows_s)

        x1, lg_p = _outproj(an_p, sn_p, xp, w_out_bf, g1, b1, wr, br, tm_p, rows_all, 0)
        x1, lg_s = _outproj(an_s, sn_s, xs, w_out_bf, g1, b1, wr, br, rows_s, rows_all, rows_p // rows_s, x1_buf=x1)
        nblocks = 2 * rows_all // MOE_BLOCK + N_EXPERTS
        route, meta = _route_plan(jnp.concatenate([lg_p, lg_s], axis=0), nblocks)
        pos = route[:, R_P0:R_P1 + 1].astype(jnp.int32).reshape(2 * rows_all)
        blk_e = meta[:nblocks, 0].astype(jnp.int32)
        nused = meta[0:1, 1].astype(jnp.int32)
        row_tok = _rowmap(pos, nblocks * MOE_BLOCK)

        ys = _moe(blk_e, nused, row_tok, x1, expert_w_gate, expert_w_up, expert_w_down, l)
        xp, xs = _combine(pos, ys, x1, route, g2, b2, rows_p)

        wp = min(WINDOW, seq)
        outs["kp"].append(k_p.reshape(nbatch, seq, N_KV_HEADS, HEAD_DIM)[:, seq - wp:])
        outs["vp"].append(v_p.reshape(nbatch, seq, N_KV_HEADS, HEAD_DIM)[:, seq - wp:])
        hp = hl_p.reshape(nbatch, SSM_CHUNKS * 2 * CH_STATES)
        hr, hi = _unpack_state(hp, nbatch)
        outs["hrp"].append(hr)
        outs["hip"].append(hi)
        outs["ks"].append(nk_s.reshape(nseq, wbuf, N_KV_HEADS, HEAD_DIM))
        outs["vs"].append(nv_s.reshape(nseq, wbuf, N_KV_HEADS, HEAD_DIM))
        hr, hi = _unpack_state(hl_s, nseq)
        outs["hrs"].append(hr)
        outs["his"].append(hi)

    st = {k: jnp.stack(v) for k, v in outs.items()}
    return (xp.reshape(nbatch, seq, D_MODEL), xs.reshape(nseq, dec_seq, D_MODEL),
            st["kp"], st["vp"], st["hrp"], st["hip"], st["ks"], st["vs"], st["hrs"], st["his"])
```

```python
import functools
import math

import numpy as np
import jax
import jax.numpy as jnp
from jax import lax
from jax.experimental import pallas as pl
from jax.experimental.pallas import tpu as pltpu

F32 = jnp.float32
BF16 = jnp.bfloat16

D_MODEL = 2048
DEPTH = 2
PAST_LEN = 16384
ATTN_WIDTH = 1024
SSM_WIDTH = 1024
HEAD_DIM = 64
N_HEADS = 16
N_KV_HEADS = 4
Q_PER_KV = 4
KV_WIDTH = 256
WINDOW = 128
ROPE_THETA = 10000.0
SSM_GROUP = 16
SSM_GROUPS = 64
SSM_STATE = 64
N_EXPERT_GROUPS = 4
EXPERTS_PER_GROUP = 8
N_EXPERTS = 32
EXPERT_FF = 512
MOE_BLOCK = 128
DEEPNORM_ALPHA = (2.0 * DEPTH) ** 0.25
LN_EPS = 1e-5
RMS_EPS = 1e-6

LANES = 128
SUBLANES = 8
VMEM_LIMIT = 56 * 1024 * 1024

SSM_CHUNKS = 4
CH_GROUPS = SSM_GROUPS // SSM_CHUNKS
CH_STATES = CH_GROUPS * SSM_STATE
CH_IN = CH_GROUPS * SSM_GROUP
SCAN_T = 512
SCAN_SEG = SCAN_T // SUBLANES


def _cparams(sem=None, vmem=VMEM_LIMIT, bounds_checks=True):
    return pltpu.CompilerParams(dimension_semantics=sem, vmem_limit_bytes=vmem,
                                disable_bounds_checks=not bounds_checks)


def _full(shape):
    n = len(shape)
    return pl.BlockSpec(shape, lambda *a: (0,) * n)


def _inproj_body(x_ref, w_ref, cos_ref, sa_ref, sb_ref, q_ref, k_ref, v_ref, u_ref):
    x = x_ref[...].astype(BF16)
    cos = cos_ref[...]
    sa = sa_ref[...]
    sb = sb_ref[...]

    def rope(c):
        return c * cos + pltpu.roll(c, LANES - 32, 1) * sa + pltpu.roll(c, 32, 1) * sb

    pq = jnp.dot(x, w_ref[:, 0:ATTN_WIDTH], preferred_element_type=F32)
    for j in range(ATTN_WIDTH // LANES):
        q_ref[:, j * LANES:(j + 1) * LANES] = rope(pq[:, j * LANES:(j + 1) * LANES]).astype(BF16)
    pk = jnp.dot(x, w_ref[:, ATTN_WIDTH:ATTN_WIDTH + KV_WIDTH], preferred_element_type=F32)
    for j in range(KV_WIDTH // LANES):
        k_ref[:, j * LANES:(j + 1) * LANES] = rope(pk[:, j * LANES:(j + 1) * LANES])
    v_ref[...] = jnp.dot(x, w_ref[:, ATTN_WIDTH + KV_WIDTH:ATTN_WIDTH + 2 * KV_WIDTH],
                         preferred_element_type=F32)
    u_ref[...] = jnp.dot(x, w_ref[:, ATTN_WIDTH + 2 * KV_WIDTH:], preferred_element_type=F32)


def _inproj(x, w_bf, cos_t, sa_t, sb_t, tm):
    rows = x.shape[0]
    tab_blocks = cos_t.shape[0] // tm
    in_w = w_bf.shape[1]
    row_spec = lambda w: pl.BlockSpec((tm, w), lambda i: (i, 0))
    tab_spec = pl.BlockSpec((tm, LANES), lambda i: (i % tab_blocks, 0))
    return pl.pallas_call(
        _inproj_body,
        grid=(rows // tm,),
        in_specs=[row_spec(D_MODEL), _full((D_MODEL, in_w)), tab_spec, tab_spec, tab_spec],
        out_specs=[row_spec(ATTN_WIDTH), row_spec(KV_WIDTH), row_spec(KV_WIDTH), row_spec(SSM_WIDTH)],
        out_shape=[jax.ShapeDtypeStruct((rows, ATTN_WIDTH), BF16),
                   jax.ShapeDtypeStruct((rows, KV_WIDTH), F32),
                   jax.ShapeDtypeStruct((rows, KV_WIDTH), F32),
                   jax.ShapeDtypeStruct((rows, SSM_WIDTH), F32)],
        compiler_params=_cparams(("arbitrary",)),
        name="inproj_rope",
    )(x, w_bf, cos_t, sa_t, sb_t)


def _rope_tables(positions):
    half = HEAD_DIM // 2
    inv_freq = ROPE_THETA ** (-np.arange(half, dtype=np.float64) / half)
    ang = positions.astype(np.float64)[:, None] * inv_freq[None, :]
    lane = np.arange(LANES)
    cos = np.cos(ang)[:, lane % half]
    sin = np.sin(ang)[:, lane % half]
    first = (lane % HEAD_DIM) < half
    sa = np.where(first[None, :], -sin, 0.0)
    sb = np.where(first[None, :], 0.0, sin)
    return (jnp.asarray(cos, F32), jnp.asarray(sa, F32), jnp.asarray(sb, F32))


def _rms(x, g):
    return x * lax.rsqrt(jnp.mean(jnp.square(x), axis=-1, keepdims=True) + RMS_EPS) * g


def _attn_prompt_body(sink_ref, q_ref, kp_ref, kc_ref, vp_ref, vc_ref, g_ref, o_ref, acc_ref, *, blocks_per_seq):
    i = pl.program_id(0)
    has_prev = (i % blocks_per_seq) > 0
    kk = jnp.concatenate([kp_ref[...], kc_ref[...]], axis=0).astype(BF16)
    vv = jnp.concatenate([vp_ref[...], vc_ref[...]], axis=0).astype(BF16)
    row = lax.broadcasted_iota(jnp.int32, (WINDOW, 2 * WINDOW), 0)
    col = lax.broadcasted_iota(jnp.int32, (WINDOW, 2 * WINDOW), 1)
    visible = (col > row) & (col <= row + WINDOW) & ((col >= WINDOW) | has_prev)
    q = q_ref[...]
    for h in range(N_HEADS):
        g = h // Q_PER_KV
        qh = q[:, h * HEAD_DIM:(h + 1) * HEAD_DIM]
        kh = kk[:, g * HEAD_DIM:(g + 1) * HEAD_DIM]
        vh = vv[:, g * HEAD_DIM:(g + 1) * HEAD_DIM]
        s = lax.dot_general(qh, kh, (((1,), (1,)), ((), ())), preferred_element_type=F32) * (HEAD_DIM ** -0.5)
        s = jnp.where(visible, s, -jnp.inf)
        sink = sink_ref[h]
        m = jnp.maximum(jnp.max(s, axis=-1, keepdims=True), sink)
        p = jnp.exp(s - m)
        denom = jnp.sum(p, axis=-1, keepdims=True) + jnp.exp(sink - m)
        oh = jnp.dot(p.astype(BF16), vh, preferred_element_type=F32) / denom
        acc_ref[:, h * HEAD_DIM:(h + 1) * HEAD_DIM] = oh
    o_ref[...] = _rms(acc_ref[...], g_ref[...]).astype(BF16)


def _attn_prompt(sinks, q, k, v, g, seq):
    rows = q.shape[0]
    nblk = rows // WINDOW
    bps = seq // WINDOW
    cur = lambda w: pl.BlockSpec((WINDOW, w), lambda i: (i, 0))
    prev = lambda w: pl.BlockSpec((WINDOW, w), lambda i: (jnp.maximum(i - 1, 0), 0))
    return pl.pallas_call(
        functools.partial(_attn_prompt_body, blocks_per_seq=bps),
        grid=(nblk,),
        in_specs=[pl.BlockSpec(memory_space=pltpu.SMEM), cur(ATTN_WIDTH), prev(KV_WIDTH), cur(KV_WIDTH),
                  prev(KV_WIDTH), cur(KV_WIDTH), _full((1, ATTN_WIDTH))],
        out_specs=cur(ATTN_WIDTH),
        out_shape=jax.ShapeDtypeStruct((rows, ATTN_WIDTH), BF16),
        scratch_shapes=[pltpu.VMEM((WINDOW, ATTN_WIDTH), F32)],
        compiler_params=_cparams(("arbitrary",)),
        name="attn_prompt",
    )(sinks, q, k, k, v, v, g)


SEQ_PER_STEP = 8


def _attn_sample_body(sink_ref, q_ref, kn_ref, vn_ref, ck_ref, cv_ref, g_ref, o_ref, nk_ref, nv_ref, acc_ref,
                      *, dec_seq):
    nq = SEQ_PER_STEP * dec_seq
    wb = ck_ref.shape[1]
    nkc = SEQ_PER_STEP * wb
    ck = ck_ref[...].reshape(nkc, KV_WIDTH)
    cv = cv_ref[...].reshape(nkc, KV_WIDTH)
    kn = kn_ref[...]
    vn = vn_ref[...]
    ckb = ck.astype(BF16)
    cvb = cv.astype(BF16)
    knb = kn.astype(BF16)
    vnb = vn.astype(BF16)
    q = q_ref[...]

    rows = Q_PER_KV * nq
    r = lax.broadcasted_iota(jnp.int32, (rows, nkc), 0) % nq
    c = lax.broadcasted_iota(jnp.int32, (rows, nkc), 1)
    vis_c = ((r // dec_seq) == (c // wb)) & ((c % wb) >= (r % dec_seq) + 1 + (wb - WINDOW))
    rn = lax.broadcasted_iota(jnp.int32, (rows, nq), 0) % nq
    cn = lax.broadcasted_iota(jnp.int32, (rows, nq), 1)
    vis_n = ((rn // dec_seq) == (cn // dec_seq)) & ((cn % dec_seq) <= (rn % dec_seq))
    hrow = lax.broadcasted_iota(jnp.int32, (rows, 1), 0) // nq

    dn = (((1,), (1,)), ((), ()))
    for g in range(N_KV_HEADS):
        qs = jnp.concatenate([q[:, (g * Q_PER_KV + j) * HEAD_DIM:(g * Q_PER_KV + j + 1) * HEAD_DIM]
                              for j in range(Q_PER_KV)], axis=0)
        sl = slice(g * HEAD_DIM, (g + 1) * HEAD_DIM)
        sc = lax.dot_general(qs, ckb[:, sl], dn, preferred_element_type=F32) * (HEAD_DIM ** -0.5)
        sn = lax.dot_general(qs, knb[:, sl], dn, preferred_element_type=F32) * (HEAD_DIM ** -0.5)
        sc = jnp.where(vis_c, sc, -jnp.inf)
        sn = jnp.where(vis_n, sn, -jnp.inf)
        sink = jnp.zeros((rows, 1), F32)
        for j in range(Q_PER_KV):
            sink = jnp.where(hrow == j, sink_ref[g * Q_PER_KV + j], sink)
        m = jnp.maximum(jnp.maximum(jnp.max(sc, axis=-1, keepdims=True), jnp.max(sn, axis=-1, keepdims=True)), sink)
        pc = jnp.exp(sc - m)
        pn = jnp.exp(sn - m)
        denom = jnp.sum(pc, axis=-1, keepdims=True) + jnp.sum(pn, axis=-1, keepdims=True) + jnp.exp(sink - m)
        o = (jnp.dot(pc.astype(BF16), cvb[:, sl], preferred_element_type=F32)
             + jnp.dot(pn.astype(BF16), vnb[:, sl], preferred_element_type=F32)) / denom
        for j in range(Q_PER_KV):
            h = g * Q_PER_KV + j
            acc_ref[:, h * HEAD_DIM:(h + 1) * HEAD_DIM] = o[j * nq:(j + 1) * nq, :]
    o_ref[...] = _rms(acc_ref[...], g_ref[...]).astype(BF16)

    keep = wb - dec_seq
    for s in range(SEQ_PER_STEP):
        nk_ref[s, 0:keep, :] = ck_ref[s, dec_seq:wb, :]
        nv_ref[s, 0:keep, :] = cv_ref[s, dec_seq:wb, :]
        nk_ref[s, keep:wb, :] = kn[s * dec_seq:(s + 1) * dec_seq, :]
        nv_ref[s, keep:wb, :] = vn[s * dec_seq:(s + 1) * dec_seq, :]


def _attn_sample(sinks, q, kn, vn, ck, cv, g, dec_seq, layer):
    rows = q.shape[0]
    _, nseq, wb, _ = ck.shape
    nq = SEQ_PER_STEP * dec_seq
    rowb = lambda w: pl.BlockSpec((nq, w), lambda i: (i, 0))
    cin = pl.BlockSpec((None, SEQ_PER_STEP, wb, KV_WIDTH), lambda i: (layer, i, 0, 0))
    cb = pl.BlockSpec((SEQ_PER_STEP, wb, KV_WIDTH), lambda i: (i, 0, 0))
    return pl.pallas_call(
        functools.partial(_attn_sample_body, dec_seq=dec_seq),
        grid=(nseq // SEQ_PER_STEP,),
        in_specs=[pl.BlockSpec(memory_space=pltpu.SMEM), rowb(ATTN_WIDTH), rowb(KV_WIDTH), rowb(KV_WIDTH),
                  cin, cin, _full((1, ATTN_WIDTH))],
        out_specs=[rowb(ATTN_WIDTH), cb, cb],
        out_shape=[jax.ShapeDtypeStruct((rows, ATTN_WIDTH), BF16),
                   jax.ShapeDtypeStruct(ck.shape[1:], F32), jax.ShapeDtypeStruct(cv.shape[1:], F32)],
        scratch_shapes=[pltpu.VMEM((nq, ATTN_WIDTH), F32)],
        compiler_params=_cparams(("arbitrary",)),
        name="attn_sample",
    )(sinks, q, kn, vn, ck, cv, g)


def _ssm_prompt_body(u_ref, b_ref, c_ref, lr_ref, li_ref, pr_ref, pi_ref, d_ref, y_ref, hl_ref,
                     uperm_ref, bu_ref, hb_ref, yp_ref, carry_ref):
    tc = pl.program_id(1)
    cc = pl.program_id(2)

    @pl.when(cc == 0)
    def _():
        for k in range(SSM_WIDTH // LANES):
            for s in range(SUBLANES):
                uperm_ref[k, pl.ds(s, SCAN_SEG, stride=SUBLANES), :] = (
                    u_ref[s * SCAN_SEG:(s + 1) * SCAN_SEG, k * LANES:(k + 1) * LANES])

    @pl.when(tc == 0)
    def _():
        carry_ref[cc] = jnp.zeros((2, CH_STATES), F32)

    lanes_per_chunk = CH_IN // LANES
    up = jnp.concatenate([uperm_ref[cc * lanes_per_chunk + k] for k in range(lanes_per_chunk)], axis=1)
    bu_ref[...] = jnp.dot(up.astype(BF16), b_ref[cc], preferred_element_type=F32)

    ar = jnp.broadcast_to(lr_ref[cc], (SUBLANES, CH_STATES))
    ai = jnp.broadcast_to(li_ref[cc], (SUBLANES, CH_STATES))

    def step(j, h):
        hr, hi = h
        r0 = pl.multiple_of(j * SUBLANES, SUBLANES)
        br = bu_ref[pl.ds(r0, SUBLANES), 0:CH_STATES]
        bi = bu_ref[pl.ds(r0, SUBLANES), CH_STATES:2 * CH_STATES]
        nr = ar * hr + (br - ai * hi)
        ni = ar * hi + (bi + ai * hr)
        bu_ref[pl.ds(r0, SUBLANES), 0:CH_STATES] = nr
        bu_ref[pl.ds(r0, SUBLANES), CH_STATES:2 * CH_STATES] = ni
        return nr, ni

    zero = jnp.zeros((SUBLANES, CH_STATES), F32)
    fr, fi = lax.fori_loop(0, SCAN_SEG, step, (zero, zero))

    cst = carry_ref[cc]
    c_r = cst[0:1, :]
    c_i = cst[1:2, :]
    lsr = pr_ref[cc, SCAN_SEG - 1:SCAN_SEG, :]
    lsi = pi_ref[cc, SCAN_SEG - 1:SCAN_SEG, :]
    rows_r, rows_i = [], []
    for s in range(SUBLANES):
        rows_r.append(c_r)
        rows_i.append(c_i)
        n_r = fr[s:s + 1, :] + (lsr * c_r - lsi * c_i)
        n_i = fi[s:s + 1, :] + (lsr * c_i + lsi * c_r)
        c_r, c_i = n_r, n_i
    carry_ref[cc] = jnp.concatenate([c_r, c_i], axis=0)
    hl_ref[0, cc] = jnp.concatenate([c_r, c_i], axis=0)
    cm_r = jnp.concatenate(rows_r, axis=0)
    cm_i = jnp.concatenate(rows_i, axis=0)

    def fix(jj, _):
        r0 = pl.multiple_of(jj * 2 * SUBLANES, 2 * SUBLANES)
        outs_r, outs_i = [], []
        for half in range(2):
            j = jj * 2 + half
            pr = pr_ref[cc, pl.ds(j, 1), :]
            pi = pi_ref[cc, pl.ds(j, 1), :]
            rr = r0 + half * SUBLANES
            h_r = bu_ref[pl.ds(rr, SUBLANES), 0:CH_STATES]
            h_i = bu_ref[pl.ds(rr, SUBLANES), CH_STATES:2 * CH_STATES]
            outs_r.append(h_r + (pr * cm_r - pi * cm_i))
            outs_i.append(h_i + (pr * cm_i + pi * cm_r))
        hb_ref[pl.ds(r0, 2 * SUBLANES), 0:CH_STATES] = jnp.concatenate(outs_r, axis=0).astype(BF16)
        hb_ref[pl.ds(r0, 2 * SUBLANES), CH_STATES:2 * CH_STATES] = jnp.concatenate(outs_i, axis=0).astype(BF16)
        return 0

    lax.fori_loop(0, SCAN_SEG // 2, fix, 0)

    yp = jnp.dot(hb_ref[...], c_ref[cc], preferred_element_type=F32) + d_ref[cc] * up
    for k in range(lanes_per_chunk):
        yp_ref[k] = yp[:, k * LANES:(k + 1) * LANES]
    for s in range(SUBLANES):
        for k in range(lanes_per_chunk):
            y_ref[s * SCAN_SEG:(s + 1) * SCAN_SEG, k * LANES:(k + 1) * LANES] = (
                yp_ref[k, pl.ds(s, SCAN_SEG, stride=SUBLANES), :])


def _ssm_prompt(u, sp, nbatch, seq):
    rows = u.shape[0]
    ntc = seq // SCAN_T
    y, hl = pl.pallas_call(
        _ssm_prompt_body,
        grid=(nbatch, ntc, SSM_CHUNKS),
        in_specs=[pl.BlockSpec((SCAN_T, SSM_WIDTH), lambda b, t, c: (b * ntc + t, 0)),
                  _full(sp["B"].shape), _full(sp["C"].shape), _full(sp["lr"].shape), _full(sp["li"].shape),
                  _full(sp["pr"].shape), _full(sp["pi"].shape), _full(sp["d"].shape)],
        out_specs=[pl.BlockSpec((SCAN_T, CH_IN), lambda b, t, c: (b * ntc + t, c)),
                   pl.BlockSpec((1, SSM_CHUNKS, 2, CH_STATES), lambda b, t, c: (b, 0, 0, 0))],
        out_shape=[jax.ShapeDtypeStruct((rows, SSM_WIDTH), F32),
                   jax.ShapeDtypeStruct((nbatch, SSM_CHUNKS, 2, CH_STATES), F32)],
        scratch_shapes=[pltpu.VMEM((SSM_WIDTH // LANES, SCAN_T, LANES), F32),
                        pltpu.VMEM((SCAN_T, 2 * CH_STATES), F32),
                        pltpu.VMEM((SCAN_T, 2 * CH_STATES), BF16),
                        pltpu.VMEM((CH_IN // LANES, SCAN_T, LANES), F32),
                        pltpu.VMEM((SSM_CHUNKS, 2, CH_STATES), F32)],
        compiler_params=_cparams(("arbitrary", "arbitrary", "arbitrary")),
        name="ssm_prompt",
    )(u, sp["B"], sp["C"], sp["lr"], sp["li"], sp["pr"], sp["pi"], sp["d"])
    return y, hl


def _ssm_sample_body(u_ref, h0_ref, b_ref, c_ref, lr_ref, li_ref, d_ref, y_ref, ho_ref, bu_ref, *, nseq, dec_seq):
    nl = CH_STATES // LANES
    for k in range(SSM_CHUNKS):
        uk = u_ref[:, k * CH_IN:(k + 1) * CH_IN]
        bu = jnp.dot(uk.astype(BF16), b_ref[k], preferred_element_type=F32)
        for c in range(2 * nl):
            bu_ref[c] = bu[:, c * LANES:(c + 1) * LANES]
        base = k * 2 * CH_STATES
        hr = h0_ref[:, base:base + CH_STATES]
        hi = h0_ref[:, base + CH_STATES:base + 2 * CH_STATES]
        ar = lr_ref[k]
        ai = li_ref[k]
        for t in range(dec_seq):
            br = jnp.concatenate([bu_ref[c, pl.ds(t, nseq, stride=dec_seq), :] for c in range(nl)], axis=1)
            bi = jnp.concatenate([bu_ref[nl + c, pl.ds(t, nseq, stride=dec_seq), :] for c in range(nl)], axis=1)
            hr, hi = ar * hr + (br - ai * hi), ar * hi + (bi + ai * hr)
            for c in range(nl):
                bu_ref[c, pl.ds(t, nseq, stride=dec_seq), :] = hr[:, c * LANES:(c + 1) * LANES]
                bu_ref[nl + c, pl.ds(t, nseq, stride=dec_seq), :] = hi[:, c * LANES:(c + 1) * LANES]
        ho_ref[:, base:base + CH_STATES] = hr
        ho_ref[:, base + CH_STATES:base + 2 * CH_STATES] = hi
        hall = jnp.concatenate([bu_ref[c] for c in range(2 * nl)], axis=1)
        y_ref[:, k * CH_IN:(k + 1) * CH_IN] = (
            jnp.dot(hall.astype(BF16), c_ref[k], preferred_element_type=F32) + d_ref[k] * uk)


def _ssm_sample(u, h0, sp, nseq, dec_seq):
    rows = u.shape[0]
    return pl.pallas_call(
        functools.partial(_ssm_sample_body, nseq=nseq, dec_seq=dec_seq),
        grid=(1,),
        in_specs=[_full(u.shape), _full(h0.shape), _full(sp["B"].shape), _full(sp["C"].shape),
                  _full(sp["lr"].shape), _full(sp["li"].shape), _full(sp["d"].shape)],
        out_specs=[_full((rows, SSM_WIDTH)), _full(h0.shape)],
        out_shape=[jax.ShapeDtypeStruct((rows, SSM_WIDTH), F32), jax.ShapeDtypeStruct(h0.shape, F32)],
        scratch_shapes=[pltpu.VMEM((2 * CH_STATES // LANES, rows, LANES), F32)],
        compiler_params=_cparams(("arbitrary",)),
        name="ssm_sample",
    )(u, h0, sp["B"], sp["C"], sp["lr"], sp["li"], sp["d"])


def _ssm_params(lambda_re, lambda_im, log_dt, b_re, b_im, c_re, c_im, d):
    lam = lax.complex(jnp.minimum(lambda_re.astype(F32), -1e-4), lambda_im.astype(F32))
    dt = jnp.exp(log_dt.astype(F32))[:, None]
    lam_dt = lam * dt
    lam_bar = jnp.exp(lam_dt)
    b = lax.complex(b_re.astype(F32), b_im.astype(F32))
    b_bar = ((lam_bar - 1.0) / lam)[..., None] * b
    eye = jnp.eye(CH_GROUPS, dtype=F32)

    def pack_b(part):
        x = part.reshape(SSM_CHUNKS, CH_GROUPS, SSM_STATE, SSM_GROUP)
        return jnp.einsum("kgpc,gh->kgchp", x, eye).reshape(SSM_CHUNKS, CH_IN, CH_STATES)

    def pack_c(part):
        x = part.reshape(SSM_CHUNKS, CH_GROUPS, SSM_GROUP, SSM_STATE)
        return jnp.einsum("kgcp,gh->kgphc", x, eye).reshape(SSM_CHUNKS, CH_STATES, CH_IN)

    bmat = jnp.concatenate([pack_b(jnp.real(b_bar)), pack_b(jnp.imag(b_bar))], axis=2).astype(BF16)
    cmat = jnp.concatenate([pack_c(c_re.astype(F32)), pack_c(-c_im.astype(F32))], axis=1).astype(BF16)
    steps = jnp.arange(1, SCAN_SEG + 1, dtype=F32)[:, None, None]
    pw = jnp.exp(lam_dt[None] * steps)
    pw = pw.reshape(SCAN_SEG, SSM_CHUNKS, CH_STATES).transpose(1, 0, 2)
    lb = lam_bar.reshape(SSM_CHUNKS, 1, CH_STATES)
    return dict(B=bmat, C=cmat, lr=jnp.real(lb), li=jnp.imag(lb), pr=jnp.real(pw), pi=jnp.imag(pw),
                d=d.astype(F32).reshape(SSM_CHUNKS, 1, CH_IN))


def _pack_state(re, im):
    n = re.shape[0]
    st = jnp.stack([re.reshape(n, SSM_CHUNKS, CH_STATES), im.reshape(n, SSM_CHUNKS, CH_STATES)], axis=2)
    return st.reshape(n, SSM_CHUNKS * 2 * CH_STATES).astype(F32)


def _unpack_state(h, n):
    st = h.reshape(n, SSM_CHUNKS, 2, CH_GROUPS, SSM_STATE)
    return (st[:, :, 0].reshape(n, SSM_GROUPS, SSM_STATE), st[:, :, 1].reshape(n, SSM_GROUPS, SSM_STATE))


def _glu_body(y_ref, w_ref, g_ref, o_ref):
    a = jax.nn.gelu(y_ref[...])
    z = a * jax.nn.sigmoid(jnp.dot(a.astype(BF16), w_ref[...], preferred_element_type=F32))
    o_ref[...] = _rms(z, g_ref[...]).astype(BF16)


def _glu(y, w_bf, g, tm):
    rows = y.shape[0]
    spec = pl.BlockSpec((tm, SSM_WIDTH), lambda i: (i, 0))
    return pl.pallas_call(
        _glu_body,
        grid=(rows // tm,),
        in_specs=[spec, _full((SSM_WIDTH, SSM_WIDTH)), _full((1, SSM_WIDTH))],
        out_specs=spec,
        out_shape=jax.ShapeDtypeStruct((rows, SSM_WIDTH), BF16),
        compiler_params=_cparams(("arbitrary",)),
        name="glu_rms",
    )(y, w_bf, g)


def _layer_norm(y, g, b):
    mu = jnp.mean(y, axis=-1, keepdims=True)
    yc = y - mu
    var = jnp.mean(jnp.square(yc), axis=-1, keepdims=True)
    return yc * lax.rsqrt(var + LN_EPS) * g + b


def _outproj_body(*refs, aliased, nblk):
    if aliased:
        refs = refs[1:]
    a_ref, s_ref, x_ref, w_ref, g_ref, b_ref, wrh_ref, wrl_ref, br_ref, o_ref, lg_ref = refs
    i = pl.program_id(0)

    @pl.when(i < nblk)
    def _():
        acc = jnp.dot(a_ref[...], w_ref[0:ATTN_WIDTH, :], preferred_element_type=F32)
        acc = acc + jnp.dot(s_ref[...], w_ref[ATTN_WIDTH:, :], preferred_element_type=F32)
        x1 = _layer_norm(DEEPNORM_ALPHA * x_ref[...] + acc, g_ref[...], b_ref[...])
        o_ref[...] = x1
        xh = x1.astype(BF16)
        xl = (x1 - xh.astype(F32)).astype(BF16)
        lg = jnp.dot(xh, wrh_ref[...], preferred_element_type=F32)
        lg = lg + (jnp.dot(xl, wrh_ref[...], preferred_element_type=F32)
                   + jnp.dot(xh, wrl_ref[...], preferred_element_type=F32))
        lg_ref[...] = lg + br_ref[...]

    @pl.when(i >= nblk)
    def _():
        o_ref[...] = jnp.zeros(o_ref.shape, F32)


def _outproj(an, sn, x, w_bf, g, b, wr, br, tm, total_rows, row_block0, x1_buf=None):
    rows = an.shape[0]
    aliased = x1_buf is not None
    nblk = rows // tm
    tail = 0 if aliased or rows == total_rows else 1
    blk = lambda i: jnp.minimum(i, nblk - 1)
    half = lambda: pl.BlockSpec((tm, ATTN_WIDTH), lambda i: (blk(i), 0))
    in_specs = [half(), half(), pl.BlockSpec((tm, D_MODEL), lambda i: (blk(i), 0)),
                _full((D_MODEL, D_MODEL)), _full((1, D_MODEL)), _full((1, D_MODEL)),
                _full((D_MODEL, LANES)), _full((D_MODEL, LANES)), _full((1, LANES))]
    wr_hi = wr.astype(BF16)
    wr_lo = (wr - wr_hi.astype(F32)).astype(BF16)
    args = [an, sn, x, w_bf, g, b, wr_hi, wr_lo, br]
    if aliased:
        in_specs = [pl.BlockSpec(memory_space=pl.ANY)] + in_specs
        args = [x1_buf] + args
    return pl.pallas_call(
        functools.partial(_outproj_body, aliased=aliased, nblk=nblk),
        grid=(nblk + tail,),
        in_specs=in_specs,
        out_specs=[pl.BlockSpec((tm, D_MODEL), lambda i: (i + row_block0, 0)),
                   pl.BlockSpec((tm, LANES), lambda i: (blk(i), 0))],
        out_shape=[jax.ShapeDtypeStruct((total_rows, D_MODEL), F32), jax.ShapeDtypeStruct((rows, LANES), F32)],
        input_output_aliases={0: 0} if aliased else {},
        compiler_params=_cparams(("arbitrary",)),
        name="outproj_ln1",
    )(*args)


ROUTE_T = 640
R_E0, R_E1, R_W0, R_W1, R_P0, R_P1 = range(6)


def _route_select(x):
    lane = lax.broadcasted_iota(jnp.int32, x.shape, 1)
    big = jnp.int32(1 << 20)
    neg = -jnp.inf
    gmask = lane < N_EXPERT_GROUPS
    gm = jnp.max(jnp.where(gmask, x, neg), axis=-1, keepdims=True)
    grp = jnp.min(jnp.where(gmask & (x == gm), lane, big), axis=-1, keepdims=True)
    p_grp = 1.0 / jnp.sum(jnp.where(gmask, jnp.exp(x - gm), 0.0), axis=-1, keepdims=True)
    lo = N_EXPERT_GROUPS + grp * EXPERTS_PER_GROUP
    emask = (lane >= lo) & (lane < lo + EXPERTS_PER_GROUP)
    v1 = jnp.max(jnp.where(emask, x, neg), axis=-1, keepdims=True)
    i1 = jnp.min(jnp.where(emask & (x == v1), lane, big), axis=-1, keepdims=True)
    emask2 = emask & (lane != i1)
    v2 = jnp.max(jnp.where(emask2, x, neg), axis=-1, keepdims=True)
    i2 = jnp.min(jnp.where(emask2 & (x == v2), lane, big), axis=-1, keepdims=True)
    e2 = jnp.exp(v2 - v1)
    w1 = p_grp / (1.0 + e2)
    w2 = p_grp * e2 / (1.0 + e2)
    return (i1 - N_EXPERT_GROUPS).astype(F32), (i2 - N_EXPERT_GROUPS).astype(F32), w1, w2


def _route_plan_body(lg_ref, ltri_ref, utri_ref, o_ref, meta_ref, rec_ref, run_ref):
    ph = pl.program_id(0)
    j = pl.program_id(1)
    tm = lg_ref.shape[0]
    lane = lax.broadcasted_iota(jnp.int32, (tm, LANES), 1)
    lanef = lane.astype(F32)
    r0 = pl.multiple_of(j * tm, tm)

    @pl.when((ph == 0) & (j == 0))
    def _():
        run_ref[...] = jnp.zeros(run_ref.shape, F32)

    @pl.when(ph == 0)
    def _():
        e0, e1, w0, w1 = _route_select(lg_ref[...])
        oh0 = (lanef == e0).astype(F32)
        oh1 = (lanef == e1).astype(F32)
        oh2 = oh0 + oh1
        before = jnp.dot(ltri_ref[...], oh2.astype(BF16), preferred_element_type=F32) + run_ref[0:1, :]
        rk0 = jnp.sum(before * oh0, axis=-1, keepdims=True)
        rk1 = jnp.sum(before * oh1, axis=-1, keepdims=True)
        run_ref[...] = run_ref[...] + jnp.sum(oh2, axis=0, keepdims=True)
        rec = jnp.where(lane == R_E0, e0, 0.0)
        rec = jnp.where(lane == R_E1, e1, rec)
        rec = jnp.where(lane == R_W0, w0, rec)
        rec = jnp.where(lane == R_W1, w1, rec)
        rec = jnp.where(lane == R_P0, rk0, rec)
        rec = jnp.where(lane == R_P1, rk1, rec)
        rec_ref[pl.ds(r0, tm), :] = rec

    @pl.when(ph == 1)
    def _():
        counts = run_ref[...]
        nblk = jnp.floor((counts + (MOE_BLOCK - 1)) * (1.0 / MOE_BLOCK))
        end_blk = jnp.dot(nblk.astype(BF16), utri_ref[...], preferred_element_type=F32)
        start_row = (end_blk - nblk) * MOE_BLOCK
        rec = rec_ref[pl.ds(r0, tm), :]
        oh0 = (lanef == rec[:, R_E0:R_E0 + 1]).astype(F32)
        oh1 = (lanef == rec[:, R_E1:R_E1 + 1]).astype(F32)
        p0 = jnp.sum(oh0 * start_row[0:1, :], axis=-1, keepdims=True) + rec[:, R_P0:R_P0 + 1]
        p1 = jnp.sum(oh1 * start_row[0:1, :], axis=-1, keepdims=True) + rec[:, R_P1:R_P1 + 1]
        out = jnp.where(lane == R_P0, p0, rec)
        out = jnp.where(lane == R_P1, p1, out)
        o_ref[...] = out

        @pl.when(j == 0)
        def _():
            meta_ref[...] = end_blk


def _route_plan(logits):
    rows = logits.shape[0]
    tm = ROUTE_T
    ltri = jnp.asarray(np.tril(np.ones((tm, tm), np.float32), -1), BF16)
    utri = jnp.asarray(np.triu(np.ones((LANES, LANES), np.float32)), BF16)
    mrows = SUBLANES
    return pl.pallas_call(
        _route_plan_body,
        grid=(2, rows // tm),
        in_specs=[pl.BlockSpec((tm, LANES), lambda p, j: (j * (1 - p), 0)),
                  pl.BlockSpec((tm, tm), lambda p, j: (0, 0)),
                  pl.BlockSpec((LANES, LANES), lambda p, j: (0, 0))],
        out_specs=[pl.BlockSpec((tm, LANES), lambda p, j: (j * p, 0)),
                   pl.BlockSpec((mrows, LANES), lambda p, j: (0, 0))],
        out_shape=[jax.ShapeDtypeStruct((rows, LANES), F32), jax.ShapeDtypeStruct((mrows, LANES), F32)],
        scratch_shapes=[pltpu.VMEM((rows, LANES), F32), pltpu.VMEM((SUBLANES, LANES), F32)],
        compiler_params=_cparams(("arbitrary", "arbitrary")),
        name="route_plan",
    )(logits, ltri, utri)


def _rowmap_body(pos_ref, o_ref, *, ntok, nrows):
    unroll = 8

    def zero(c, _):
        for u in range(unroll):
            o_ref[c * unroll + u] = 0
        return 0
    lax.fori_loop(0, nrows // unroll, zero, 0)

    def fill(c, _):
        for u in range(unroll):
            t = c * unroll + u
            o_ref[pos_ref[2 * t]] = t
            o_ref[pos_ref[2 * t + 1]] = t
        return 0
    lax.fori_loop(0, ntok // unroll, fill, 0)


def _rowmap(pos, nrows):
    ntok = pos.shape[0] // 2
    return pl.pallas_call(
        functools.partial(_rowmap_body, ntok=ntok, nrows=nrows),
        in_specs=[pl.BlockSpec(memory_space=pltpu.SMEM)],
        out_specs=pl.BlockSpec(memory_space=pltpu.SMEM),
        out_shape=jax.ShapeDtypeStruct((nrows,), jnp.int32),
        name="moe_rowmap",
    )(pos)


def _moe_body(bend_ref, row_tok_ref, x_hbm, wg_ref, wu_ref, wd_ref, ys_hbm,
              xb0, xb1, yb0, yb1, gsem, osem, wg_bf, wu_bf, wd_bf, *, nblocks):
    e = pl.program_id(0)
    nused = bend_ref[N_EXPERTS - 1]
    b1 = bend_ref[e]
    b0 = jnp.where(e == 0, 0, bend_ref[jnp.maximum(e - 1, 0)])
    xbufs = (xb0, xb1)
    ybufs = (yb0, yb1)

    def gather_start(blk, s):
        base = blk * MOE_BLOCK
        for r in range(MOE_BLOCK):
            tok = row_tok_ref[base + r]
            pltpu.make_async_copy(x_hbm.at[pl.ds(tok, 1), :], xbufs[s].at[pl.ds(r, 1), :], gsem.at[s]).start()

    def gather_wait(s):
        pltpu.make_async_copy(x_hbm.at[pl.ds(0, MOE_BLOCK), :], xbufs[s], gsem.at[s]).wait()

    def out_copy(blk, s):
        row0 = pl.multiple_of(blk * MOE_BLOCK, MOE_BLOCK)
        return pltpu.make_async_copy(ybufs[s], ys_hbm.at[pl.ds(row0, MOE_BLOCK), :], osem.at[s])

    @pl.when(e == 0)
    def _():
        gather_start(0, 0)

    @pl.when(b1 > b0)
    def _():
        wg_bf[...] = wg_ref[...].astype(BF16)
        wu_bf[...] = wu_ref[...].astype(BF16)
        wd_bf[...] = wd_ref[...].astype(BF16)

        def do_block(b, s):
            gather_wait(s)

            @pl.when(b >= 2)
            def _():
                out_copy(b - 2, s).wait()

            gather_start(jnp.minimum(b + 1, nused - 1), 1 - s)
            x = xbufs[s][...].astype(BF16)
            hg = jnp.dot(x, wg_bf[...], preferred_element_type=F32)
            hu = jnp.dot(x, wu_bf[...], preferred_element_type=F32)
            h = (jax.nn.silu(hg) * hu).astype(BF16)
            ybufs[s][...] = jnp.dot(h, wd_bf[...], preferred_element_type=F32)
            out_copy(b, s).start()

        def body(b, _):
            for s in range(2):
                @pl.when(b % 2 == s)
                def _():
                    do_block(b, s)
            return 0

        lax.fori_loop(b0, b1, body, 0)

    @pl.when(e == N_EXPERTS - 1)
    def _():
        last = nused - 1
        for s in range(2):
            @pl.when(last % 2 == s)
            def _():
                gather_wait(1 - s)
        out_copy(0, 0).wait()
        out_copy(0, 1).wait()
        yb0[...] = jnp.zeros(yb0.shape, F32)

        def zstart(b, _):
            out_copy(b, 0).start()
            return 0

        def zwait(b, _):
            out_copy(b, 0).wait()
            return 0

        lax.fori_loop(nused, nblocks, zstart, 0)
        lax.fori_loop(nused, nblocks, zwait, 0)


def _moe(bend, row_tok, x1, w_gate, w_up, w_down, layer):
    nblocks = row_tok.shape[0] // MOE_BLOCK
    assert 2 * x1.shape[0] // MOE_BLOCK >= 2
    wspec = lambda a, b: pl.BlockSpec((None, None, a, b), lambda e, be, rt: (layer, e, 0, 0))
    grid_spec = pltpu.PrefetchScalarGridSpec(
        num_scalar_prefetch=2,
        grid=(N_EXPERTS,),
        in_specs=[pl.BlockSpec(memory_space=pl.ANY), wspec(D_MODEL, EXPERT_FF), wspec(D_MODEL, EXPERT_FF),
                  wspec(EXPERT_FF, D_MODEL)],
        out_specs=pl.BlockSpec(memory_space=pl.ANY),
        scratch_shapes=[pltpu.VMEM((MOE_BLOCK, D_MODEL), F32), pltpu.VMEM((MOE_BLOCK, D_MODEL), F32),
                        pltpu.VMEM((MOE_BLOCK, D_MODEL), F32), pltpu.VMEM((MOE_BLOCK, D_MODEL), F32),
                        pltpu.SemaphoreType.DMA((2,)), pltpu.SemaphoreType.DMA((2,)),
                        pltpu.VMEM((D_MODEL, EXPERT_FF), BF16),
                        pltpu.VMEM((D_MODEL, EXPERT_FF), BF16),
                        pltpu.VMEM((EXPERT_FF, D_MODEL), BF16)],
    )
    return pl.pallas_call(
        functools.partial(_moe_body, nblocks=nblocks),
        grid_spec=grid_spec,
        out_shape=jax.ShapeDtypeStruct((nblocks * MOE_BLOCK, D_MODEL), F32),
        compiler_params=_cparams(("arbitrary",), bounds_checks=False),
        name="moe_experts",
    )(bend, row_tok, x1, w_gate, w_up, w_down)


COMB_T = 128


def _combine_body(pos_ref, ys_hbm, x_ref, rt_ref, g_ref, b_ref, op_ref, os_ref,
                  ga0, gb0, ga1, gb1, sem, *, n_prompt_blocks):
    i = pl.program_id(0)
    nsteps = pl.num_programs(0)
    bufs = ((ga0, gb0), (ga1, gb1))

    def gather_start(step, s):
        base = step * (2 * COMB_T)
        for r in range(COMB_T):
            for k in range(2):
                p = pos_ref[base + 2 * r + k]
                pltpu.make_async_copy(ys_hbm.at[pl.ds(p, 1), :], bufs[s][k].at[pl.ds(r, 1), :], sem.at[s]).start()

    def gather_wait(s):
        for k in range(2):
            pltpu.make_async_copy(ys_hbm.at[pl.ds(0, COMB_T), :], bufs[s][k], sem.at[s]).wait()

    @pl.when(i == 0)
    def _():
        gather_start(0, 0)

    def do_tile(s):
        gather_wait(s)
        gather_start(jnp.minimum(i + 1, nsteps - 1), 1 - s)
        rt = rt_ref[...]
        ff = rt[:, R_W0:R_W0 + 1] * bufs[s][0][...] + rt[:, R_W1:R_W1 + 1] * bufs[s][1][...]
        out = _layer_norm(DEEPNORM_ALPHA * x_ref[...] + ff, g_ref[...], b_ref[...])

        @pl.when(i < n_prompt_blocks)
        def _():
            op_ref[...] = out

        @pl.when(i >= n_prompt_blocks)
        def _():
            os_ref[...] = out

        @pl.when(i == nsteps - 1)
        def _():
            gather_wait(1 - s)

    for s in range(2):
        @pl.when(i % 2 == s)
        def _():
            do_tile(s)


def _combine(pos, ys, x1, route, g, b, n_prompt_rows):
    rows = x1.shape[0]
    npb = n_prompt_rows // COMB_T
    grid_spec = pltpu.PrefetchScalarGridSpec(
        num_scalar_prefetch=1,
        grid=(rows // COMB_T,),
        in_specs=[pl.BlockSpec(memory_space=pl.ANY),
                  pl.BlockSpec((COMB_T, D_MODEL), lambda i, p: (i, 0)),
                  pl.BlockSpec((COMB_T, LANES), lambda i, p: (i, 0)),
                  pl.BlockSpec((1, D_MODEL), lambda i, p: (0, 0)),
                  pl.BlockSpec((1, D_MODEL), lambda i, p: (0, 0))],
        out_specs=[pl.BlockSpec((COMB_T, D_MODEL), lambda i, p: (jnp.minimum(i, npb - 1), 0)),
                   pl.BlockSpec((COMB_T, D_MODEL), lambda i, p: (jnp.maximum(i - npb, 0), 0))],
        scratch_shapes=[pltpu.VMEM((COMB_T, D_MODEL), F32) for _ in range(4)] + [pltpu.SemaphoreType.DMA((2,))],
    )
    return pl.pallas_call(
        functools.partial(_combine_body, n_prompt_blocks=npb),
        grid_spec=grid_spec,
        out_shape=[jax.ShapeDtypeStruct((n_prompt_rows, D_MODEL), F32),
                   jax.ShapeDtypeStruct((rows - n_prompt_rows, D_MODEL), F32)],
        compiler_params=_cparams(("arbitrary",), bounds_checks=False),
        name="combine_ln2",
    )(pos, ys, x1, route, g, b)


def kernel(x_prompt, x_sample, cache_k, cache_v, state_ssm_re, state_ssm_im, w_in, ssm_lambda_re, ssm_lambda_im, ssm_log_dt, ssm_b_re, ssm_b_im, ssm_c_re, ssm_c_im, ssm_d, ssm_w_glu, attn_sinks, attn_norm_g, ssm_norm_g, w_out, ln1_g, ln1_b, router_group_w, router_group_b, router_expert_w, router_expert_b, expert_w_gate, expert_w_up, expert_w_down, ln2_g, ln2_b):
    nbatch, seq, _ = x_prompt.shape
    nseq, dec_seq, _ = x_sample.shape
    wbuf = cache_k.shape[2]
    rows_p = nbatch * seq
    rows_s = nseq * dec_seq
    rows_all = rows_p + rows_s
    tm_p = 512

    xp = x_prompt.reshape(rows_p, D_MODEL)
    xs = x_sample.reshape(rows_s, D_MODEL)
    tab_p = _rope_tables(np.arange(seq))
    tab_s = _rope_tables(PAST_LEN + (np.arange(rows_s) % dec_seq))
    ck_all = cache_k.astype(F32).reshape(DEPTH, nseq, wbuf, KV_WIDTH)
    cv_all = cache_v.astype(F32).reshape(DEPTH, nseq, wbuf, KV_WIDTH)

    outs = {k: [] for k in ("kp", "vp", "hrp", "hip", "ks", "vs", "hrs", "his")}
    for l in range(DEPTH):
        w_in_bf = w_in[l].astype(BF16)
        w_glu_bf = ssm_w_glu[l].astype(BF16)
        w_out_bf = w_out[l].astype(BF16)
        sp = _ssm_params(ssm_lambda_re[l], ssm_lambda_im[l], ssm_log_dt[l], ssm_b_re[l], ssm_b_im[l],
                         ssm_c_re[l], ssm_c_im[l], ssm_d[l])
        sinks = attn_sinks[l].astype(F32)
        attn_g = attn_norm_g[l].astype(F32).reshape(1, ATTN_WIDTH)
        ssm_g = ssm_norm_g[l].astype(F32).reshape(1, SSM_WIDTH)
        g1 = ln1_g[l].astype(F32).reshape(1, D_MODEL)
        b1 = ln1_b[l].astype(F32).reshape(1, D_MODEL)
        g2 = ln2_g[l].astype(F32).reshape(1, D_MODEL)
        b2 = ln2_b[l].astype(F32).reshape(1, D_MODEL)
        n_rt = N_EXPERT_GROUPS + N_EXPERTS
        wr = jnp.concatenate([router_group_w[l].astype(F32)]
                             + [router_expert_w[l, g].astype(F32) for g in range(N_EXPERT_GROUPS)]
                             + [jnp.zeros((D_MODEL, LANES - n_rt), F32)], axis=1)
        br = jnp.concatenate([router_group_b[l].astype(F32), router_expert_b[l].astype(F32).reshape(-1),
                              jnp.zeros((LANES - n_rt,), F32)]).reshape(1, LANES)

        q_p, k_p, v_p, u_p = _inproj(xp, w_in_bf, *tab_p, tm=tm_p)
        q_s, k_s, v_s, u_s = _inproj(xs, w_in_bf, *tab_s, tm=rows_s)

        an_p = _attn_prompt(sinks, q_p, k_p, v_p, attn_g, seq)
        an_s, nk_s, nv_s = _attn_sample(sinks, q_s, k_s, v_s, ck_all, cv_all, attn_g, dec_seq, l)

        y_p, hl_p = _ssm_prompt(u_p, sp, nbatch, seq)
        y_s, hl_s = _ssm_sample(u_s, _pack_state(state_ssm_re[l], state_ssm_im[l]), sp, nseq, dec_seq)
        sn_p = _glu(y_p, w_glu_bf, ssm_g, tm_p)
        sn_s = _glu(y_s, w_glu_bf, ssm_g, rows_s)

        x1, lg_p = _outproj(an_p, sn_p, xp, w_out_bf, g1, b1, wr, br, tm_p, rows_all, 0)
        x1, lg_s = _outproj(an_s, sn_s, xs, w_out_bf, g1, b1, wr, br, rows_s, rows_all, rows_p // rows_s, x1_buf=x1)
        nblocks = 2 * rows_all // MOE_BLOCK + N_EXPERTS
        route, meta = _route_plan(jnp.concatenate([lg_p, lg_s], axis=0))
        pos = route[:, R_P0:R_P1 + 1].astype(jnp.int32).reshape(2 * rows_all)
        bend = meta[0, :N_EXPERTS].astype(jnp.int32)
        row_tok = _rowmap(pos, nblocks * MOE_BLOCK)

        ys = _moe(bend, row_tok, x1, expert_w_gate, expert_w_up, expert_w_down, l)
        xp, xs = _combine(pos, ys, x1, route, g2, b2, rows_p)

        wp = min(WINDOW, seq)
        outs["kp"].append(k_p.reshape(nbatch, seq, N_KV_HEADS, HEAD_DIM)[:, seq - wp:])
        outs["vp"].append(v_p.reshape(nbatch, seq, N_KV_HEADS, HEAD_DIM)[:, seq - wp:])
        hp = hl_p.reshape(nbatch, SSM_CHUNKS * 2 * CH_STATES)
        hr, hi = _unpack_state(hp, nbatch)
        outs["hrp"].append(hr)
        outs["hip"].append(hi)
        outs["ks"].append(nk_s.reshape(nseq, wbuf, N_KV_HEADS, HEAD_DIM))
        outs["vs"].append(nv_s.reshape(nseq, wbuf, N_KV_HEADS, HEAD_DIM))
        hr, hi = _unpack_state(hl_s, nseq)
        outs["hrs"].append(hr)
        outs["his"].append(hi)

    st = {k: jnp.stack(v) for k, v in outs.items()}
    return (xp.reshape(nbatch, seq, D_MODEL), xs.reshape(nseq, dec_seq, D_MODEL),
            st["kp"], st["vp"], st["hrp"], st["hip"], st["ks"], st["vs"], st["hrs"], st["his"])
```

```python
import functools
import math

import numpy as np
import jax
import jax.numpy as jnp
from jax import lax
from jax.experimental import pallas as pl
from jax.experimental.pallas import tpu as pltpu

F32 = jnp.float32
BF16 = jnp.bfloat16

D_MODEL = 2048
DEPTH = 2
PAST_LEN = 16384
ATTN_WIDTH = 1024
SSM_WIDTH = 1024
HEAD_DIM = 64
N_HEADS = 16
N_KV_HEADS = 4
Q_PER_KV = 4
KV_WIDTH = 256
WINDOW = 128
ROPE_THETA = 10000.0
SSM_GROUP = 16
SSM_GROUPS = 64
SSM_STATE = 64
N_EXPERT_GROUPS = 4
EXPERTS_PER_GROUP = 8
N_EXPERTS = 32
EXPERT_FF = 512
MOE_BLOCK = 128
DEEPNORM_ALPHA = (2.0 * DEPTH) ** 0.25
LN_EPS = 1e-5
RMS_EPS = 1e-6

LANES = 128
SUBLANES = 8
VMEM_LIMIT = 56 * 1024 * 1024

SSM_CHUNKS = 4
CH_GROUPS = SSM_GROUPS // SSM_CHUNKS
CH_STATES = CH_GROUPS * SSM_STATE
CH_IN = CH_GROUPS * SSM_GROUP
SCAN_T = 512
SCAN_SEG = SCAN_T // SUBLANES


def _cparams(sem=None, vmem=VMEM_LIMIT, bounds_checks=True):
    return pltpu.CompilerParams(dimension_semantics=sem, vmem_limit_bytes=vmem,
                                disable_bounds_checks=not bounds_checks)


def _full(shape):
    n = len(shape)
    return pl.BlockSpec(shape, lambda *a: (0,) * n)


ROW_CHUNKS = D_MODEL // LANES
GATHER_PITCH = 24


def _store_row_linear(ref, val, pitch=ROW_CHUNKS):
    n = val.shape[0]
    for c in range(ROW_CHUNKS):
        ref[pl.ds(c, n, stride=pitch), :] = val[:, c * LANES:(c + 1) * LANES]


def _load_row_linear(ref, n, pitch=ROW_CHUNKS):
    return jnp.concatenate([ref[pl.ds(c, n, stride=pitch), :] for c in range(ROW_CHUNKS)], axis=1)


def _inproj_body(x_ref, w_ref, cos_ref, sa_ref, sb_ref, q_ref, k_ref, v_ref, u_ref):
    x = x_ref[...].astype(BF16)
    cos = cos_ref[...]
    sa = sa_ref[...]
    sb = sb_ref[...]

    def rope(c):
        return c * cos + pltpu.roll(c, LANES - 32, 1) * sa + pltpu.roll(c, 32, 1) * sb

    pq = jnp.dot(x, w_ref[:, 0:ATTN_WIDTH], preferred_element_type=F32)
    for j in range(ATTN_WIDTH // LANES):
        q_ref[:, j * LANES:(j + 1) * LANES] = rope(pq[:, j * LANES:(j + 1) * LANES]).astype(BF16)
    pk = jnp.dot(x, w_ref[:, ATTN_WIDTH:ATTN_WIDTH + KV_WIDTH], preferred_element_type=F32)
    for j in range(KV_WIDTH // LANES):
        k_ref[:, j * LANES:(j + 1) * LANES] = rope(pk[:, j * LANES:(j + 1) * LANES])
    v_ref[...] = jnp.dot(x, w_ref[:, ATTN_WIDTH + KV_WIDTH:ATTN_WIDTH + 2 * KV_WIDTH],
                         preferred_element_type=F32)
    u_ref[...] = jnp.dot(x, w_ref[:, ATTN_WIDTH + 2 * KV_WIDTH:], preferred_element_type=F32)


def _inproj(x, w_bf, cos_t, sa_t, sb_t, tm):
    rows = x.shape[0]
    tab_blocks = cos_t.shape[0] // tm
    in_w = w_bf.shape[1]
    row_spec = lambda w: pl.BlockSpec((tm, w), lambda i: (i, 0))
    tab_spec = pl.BlockSpec((tm, LANES), lambda i: (i % tab_blocks, 0))
    return pl.pallas_call(
        _inproj_body,
        grid=(rows // tm,),
        in_specs=[row_spec(D_MODEL), _full((D_MODEL, in_w)), tab_spec, tab_spec, tab_spec],
        out_specs=[row_spec(ATTN_WIDTH), row_spec(KV_WIDTH), row_spec(KV_WIDTH), row_spec(SSM_WIDTH)],
        out_shape=[jax.ShapeDtypeStruct((rows, ATTN_WIDTH), BF16),
                   jax.ShapeDtypeStruct((rows, KV_WIDTH), F32),
                   jax.ShapeDtypeStruct((rows, KV_WIDTH), F32),
                   jax.ShapeDtypeStruct((rows, SSM_WIDTH), F32)],
        compiler_params=_cparams(("arbitrary",)),
        name="inproj_rope",
    )(x, w_bf, cos_t, sa_t, sb_t)


def _rope_tables(positions):
    half = HEAD_DIM // 2
    inv_freq = ROPE_THETA ** (-np.arange(half, dtype=np.float64) / half)
    ang = positions.astype(np.float64)[:, None] * inv_freq[None, :]
    lane = np.arange(LANES)
    cos = np.cos(ang)[:, lane % half]
    sin = np.sin(ang)[:, lane % half]
    first = (lane % HEAD_DIM) < half
    sa = np.where(first[None, :], -sin, 0.0)
    sb = np.where(first[None, :], 0.0, sin)
    return (jnp.asarray(cos, F32), jnp.asarray(sa, F32), jnp.asarray(sb, F32))


def _rms(x, g):
    return x * lax.rsqrt(jnp.mean(jnp.square(x), axis=-1, keepdims=True) + RMS_EPS) * g


def _attn_prompt_body(sink_ref, q_ref, kp_ref, kc_ref, vp_ref, vc_ref, g_ref, o_ref, acc_ref, *, blocks_per_seq):
    i = pl.program_id(0)
    has_prev = (i % blocks_per_seq) > 0
    kk = jnp.concatenate([kp_ref[...], kc_ref[...]], axis=0).astype(BF16)
    vv = jnp.concatenate([vp_ref[...], vc_ref[...]], axis=0).astype(BF16)
    row = lax.broadcasted_iota(jnp.int32, (WINDOW, 2 * WINDOW), 0)
    col = lax.broadcasted_iota(jnp.int32, (WINDOW, 2 * WINDOW), 1)
    visible = (col > row) & (col <= row + WINDOW) & ((col >= WINDOW) | has_prev)
    q = q_ref[...]
    for h in range(N_HEADS):
        g = h // Q_PER_KV
        qh = q[:, h * HEAD_DIM:(h + 1) * HEAD_DIM]
        kh = kk[:, g * HEAD_DIM:(g + 1) * HEAD_DIM]
        vh = vv[:, g * HEAD_DIM:(g + 1) * HEAD_DIM]
        s = lax.dot_general(qh, kh, (((1,), (1,)), ((), ())), preferred_element_type=F32) * (HEAD_DIM ** -0.5)
        s = jnp.where(visible, s, -jnp.inf)
        sink = sink_ref[h]
        m = jnp.maximum(jnp.max(s, axis=-1, keepdims=True), sink)
        p = jnp.exp(s - m)
        denom = jnp.sum(p, axis=-1, keepdims=True) + jnp.exp(sink - m)
        oh = jnp.dot(p.astype(BF16), vh, preferred_element_type=F32) / denom
        acc_ref[:, h * HEAD_DIM:(h + 1) * HEAD_DIM] = oh
    o_ref[...] = _rms(acc_ref[...], g_ref[...]).astype(BF16)


def _attn_prompt(sinks, q, k, v, g, seq):
    rows = q.shape[0]
    nblk = rows // WINDOW
    bps = seq // WINDOW
    cur = lambda w: pl.BlockSpec((WINDOW, w), lambda i: (i, 0))
    prev = lambda w: pl.BlockSpec((WINDOW, w), lambda i: (jnp.maximum(i - 1, 0), 0))
    return pl.pallas_call(
        functools.partial(_attn_prompt_body, blocks_per_seq=bps),
        grid=(nblk,),
        in_specs=[pl.BlockSpec(memory_space=pltpu.SMEM), cur(ATTN_WIDTH), prev(KV_WIDTH), cur(KV_WIDTH),
                  prev(KV_WIDTH), cur(KV_WIDTH), _full((1, ATTN_WIDTH))],
        out_specs=cur(ATTN_WIDTH),
        out_shape=jax.ShapeDtypeStruct((rows, ATTN_WIDTH), BF16),
        scratch_shapes=[pltpu.VMEM((WINDOW, ATTN_WIDTH), F32)],
        compiler_params=_cparams(("arbitrary",)),
        name="attn_prompt",
    )(sinks, q, k, k, v, v, g)


SEQ_PER_STEP = 8


def _attn_sample_body(sink_ref, q_ref, kn_ref, vn_ref, ck_ref, cv_ref, g_ref, o_ref, nk_ref, nv_ref, acc_ref,
                      *, dec_seq):
    nq = SEQ_PER_STEP * dec_seq
    wb = ck_ref.shape[1]
    nkc = SEQ_PER_STEP * wb
    ck = ck_ref[...].reshape(nkc, KV_WIDTH)
    cv = cv_ref[...].reshape(nkc, KV_WIDTH)
    kn = kn_ref[...]
    vn = vn_ref[...]
    ckb = ck.astype(BF16)
    cvb = cv.astype(BF16)
    knb = kn.astype(BF16)
    vnb = vn.astype(BF16)
    q = q_ref[...]

    rows = Q_PER_KV * nq
    r = lax.broadcasted_iota(jnp.int32, (rows, nkc), 0) % nq
    c = lax.broadcasted_iota(jnp.int32, (rows, nkc), 1)
    vis_c = ((r // dec_seq) == (c // wb)) & ((c % wb) >= (r % dec_seq) + 1 + (wb - WINDOW))
    rn = lax.broadcasted_iota(jnp.int32, (rows, nq), 0) % nq
    cn = lax.broadcasted_iota(jnp.int32, (rows, nq), 1)
    vis_n = ((rn // dec_seq) == (cn // dec_seq)) & ((cn % dec_seq) <= (rn % dec_seq))
    hrow = lax.broadcasted_iota(jnp.int32, (rows, 1), 0) // nq

    dn = (((1,), (1,)), ((), ()))
    for g in range(N_KV_HEADS):
        qs = jnp.concatenate([q[:, (g * Q_PER_KV + j) * HEAD_DIM:(g * Q_PER_KV + j + 1) * HEAD_DIM]
                              for j in range(Q_PER_KV)], axis=0)
        sl = slice(g * HEAD_DIM, (g + 1) * HEAD_DIM)
        sc = lax.dot_general(qs, ckb[:, sl], dn, preferred_element_type=F32) * (HEAD_DIM ** -0.5)
        sn = lax.dot_general(qs, knb[:, sl], dn, preferred_element_type=F32) * (HEAD_DIM ** -0.5)
        sc = jnp.where(vis_c, sc, -jnp.inf)
        sn = jnp.where(vis_n, sn, -jnp.inf)
        sink = jnp.zeros((rows, 1), F32)
        for j in range(Q_PER_KV):
            sink = jnp.where(hrow == j, sink_ref[g * Q_PER_KV + j], sink)
        m = jnp.maximum(jnp.maximum(jnp.max(sc, axis=-1, keepdims=True), jnp.max(sn, axis=-1, keepdims=True)), sink)
        pc = jnp.exp(sc - m)
        pn = jnp.exp(sn - m)
        denom = jnp.sum(pc, axis=-1, keepdims=True) + jnp.sum(pn, axis=-1, keepdims=True) + jnp.exp(sink - m)
        o = (jnp.dot(pc.astype(BF16), cvb[:, sl], preferred_element_type=F32)
             + jnp.dot(pn.astype(BF16), vnb[:, sl], preferred_element_type=F32)) / denom
        for j in range(Q_PER_KV):
            h = g * Q_PER_KV + j
            acc_ref[:, h * HEAD_DIM:(h + 1) * HEAD_DIM] = o[j * nq:(j + 1) * nq, :]
    o_ref[...] = _rms(acc_ref[...], g_ref[...]).astype(BF16)

    keep = wb - dec_seq
    for s in range(SEQ_PER_STEP):
        nk_ref[s, 0:keep, :] = ck_ref[s, dec_seq:wb, :]
        nv_ref[s, 0:keep, :] = cv_ref[s, dec_seq:wb, :]
        nk_ref[s, keep:wb, :] = kn[s * dec_seq:(s + 1) * dec_seq, :]
        nv_ref[s, keep:wb, :] = vn[s * dec_seq:(s + 1) * dec_seq, :]


def _attn_sample(sinks, q, kn, vn, ck, cv, g, dec_seq, layer):
    rows = q.shape[0]
    _, nseq, wb, _ = ck.shape
    nq = SEQ_PER_STEP * dec_seq
    rowb = lambda w: pl.BlockSpec((nq, w), lambda i: (i, 0))
    cin = pl.BlockSpec((None, SEQ_PER_STEP, wb, KV_WIDTH), lambda i: (layer, i, 0, 0))
    cb = pl.BlockSpec((SEQ_PER_STEP, wb, KV_WIDTH), lambda i: (i, 0, 0))
    return pl.pallas_call(
        functools.partial(_attn_sample_body, dec_seq=dec_seq),
        grid=(nseq // SEQ_PER_STEP,),
        in_specs=[pl.BlockSpec(memory_space=pltpu.SMEM), rowb(ATTN_WIDTH), rowb(KV_WIDTH), rowb(KV_WIDTH),
                  cin, cin, _full((1, ATTN_WIDTH))],
        out_specs=[rowb(ATTN_WIDTH), cb, cb],
        out_shape=[jax.ShapeDtypeStruct((rows, ATTN_WIDTH), BF16),
                   jax.ShapeDtypeStruct(ck.shape[1:], F32), jax.ShapeDtypeStruct(cv.shape[1:], F32)],
        scratch_shapes=[pltpu.VMEM((nq, ATTN_WIDTH), F32)],
        compiler_params=_cparams(("arbitrary",)),
        name="attn_sample",
    )(sinks, q, kn, vn, ck, cv, g)


def _ssm_prompt_body(u_ref, b_ref, c_ref, lr_ref, li_ref, pr_ref, pi_ref, d_ref, y_ref, hl_ref,
                     uperm_ref, bu_ref, hb_ref, yp_ref, carry_ref):
    tc = pl.program_id(1)
    cc = pl.program_id(2)

    @pl.when(cc == 0)
    def _():
        for k in range(SSM_WIDTH // LANES):
            for s in range(SUBLANES):
                uperm_ref[k, pl.ds(s, SCAN_SEG, stride=SUBLANES), :] = (
                    u_ref[s * SCAN_SEG:(s + 1) * SCAN_SEG, k * LANES:(k + 1) * LANES])

    @pl.when(tc == 0)
    def _():
        carry_ref[cc] = jnp.zeros((2, CH_STATES), F32)

    lanes_per_chunk = CH_IN // LANES
    up = jnp.concatenate([uperm_ref[cc * lanes_per_chunk + k] for k in range(lanes_per_chunk)], axis=1)
    bu_ref[...] = jnp.dot(up.astype(BF16), b_ref[cc], preferred_element_type=F32)

    ar = jnp.broadcast_to(lr_ref[cc], (SUBLANES, CH_STATES))
    ai = jnp.broadcast_to(li_ref[cc], (SUBLANES, CH_STATES))

    def step(j, h):
        hr, hi = h
        r0 = pl.multiple_of(j * SUBLANES, SUBLANES)
        br = bu_ref[pl.ds(r0, SUBLANES), 0:CH_STATES]
        bi = bu_ref[pl.ds(r0, SUBLANES), CH_STATES:2 * CH_STATES]
        nr = ar * hr + (br - ai * hi)
        ni = ar * hi + (bi + ai * hr)
        bu_ref[pl.ds(r0, SUBLANES), 0:CH_STATES] = nr
        bu_ref[pl.ds(r0, SUBLANES), CH_STATES:2 * CH_STATES] = ni
        return nr, ni

    zero = jnp.zeros((SUBLANES, CH_STATES), F32)
    fr, fi = lax.fori_loop(0, SCAN_SEG, step, (zero, zero))

    cst = carry_ref[cc]
    c_r = cst[0:1, :]
    c_i = cst[1:2, :]
    lsr = pr_ref[cc, SCAN_SEG - 1:SCAN_SEG, :]
    lsi = pi_ref[cc, SCAN_SEG - 1:SCAN_SEG, :]
    rows_r, rows_i = [], []
    for s in range(SUBLANES):
        rows_r.append(c_r)
        rows_i.append(c_i)
        n_r = fr[s:s + 1, :] + (lsr * c_r - lsi * c_i)
        n_i = fi[s:s + 1, :] + (lsr * c_i + lsi * c_r)
        c_r, c_i = n_r, n_i
    carry_ref[cc] = jnp.concatenate([c_r, c_i], axis=0)
    hl_ref[0, cc] = jnp.concatenate([c_r, c_i], axis=0)
    cm_r = jnp.concatenate(rows_r, axis=0)
    cm_i = jnp.concatenate(rows_i, axis=0)

    def fix(jj, _):
        r0 = pl.multiple_of(jj * 2 * SUBLANES, 2 * SUBLANES)
        outs_r, outs_i = [], []
        for half in range(2):
            j = jj * 2 + half
            pr = pr_ref[cc, pl.ds(j, 1), :]
            pi = pi_ref[cc, pl.ds(j, 1), :]
            rr = r0 + half * SUBLANES
            h_r = bu_ref[pl.ds(rr, SUBLANES), 0:CH_STATES]
            h_i = bu_ref[pl.ds(rr, SUBLANES), CH_STATES:2 * CH_STATES]
            outs_r.append(h_r + (pr * cm_r - pi * cm_i))
            outs_i.append(h_i + (pr * cm_i + pi * cm_r))
        hb_ref[pl.ds(r0, 2 * SUBLANES), 0:CH_STATES] = jnp.concatenate(outs_r, axis=0).astype(BF16)
        hb_ref[pl.ds(r0, 2 * SUBLANES), CH_STATES:2 * CH_STATES] = jnp.concatenate(outs_i, axis=0).astype(BF16)
        return 0

    lax.fori_loop(0, SCAN_SEG // 2, fix, 0)

    yp = jnp.dot(hb_ref[...], c_ref[cc], preferred_element_type=F32) + d_ref[cc] * up
    for k in range(lanes_per_chunk):
        yp_ref[k] = yp[:, k * LANES:(k + 1) * LANES]
    for s in range(SUBLANES):
        for k in range(lanes_per_chunk):
            y_ref[s * SCAN_SEG:(s + 1) * SCAN_SEG, k * LANES:(k + 1) * LANES] = (
                yp_ref[k, pl.ds(s, SCAN_SEG, stride=SUBLANES), :])


def _ssm_prompt(u, sp, nbatch, seq):
    rows = u.shape[0]
    ntc = seq // SCAN_T
    y, hl = pl.pallas_call(
        _ssm_prompt_body,
        grid=(nbatch, ntc, SSM_CHUNKS),
        in_specs=[pl.BlockSpec((SCAN_T, SSM_WIDTH), lambda b, t, c: (b * ntc + t, 0)),
                  _full(sp["B"].shape), _full(sp["C"].shape), _full(sp["lr"].shape), _full(sp["li"].shape),
                  _full(sp["pr"].shape), _full(sp["pi"].shape), _full(sp["d"].shape)],
        out_specs=[pl.BlockSpec((SCAN_T, CH_IN), lambda b, t, c: (b * ntc + t, c)),
                   pl.BlockSpec((1, SSM_CHUNKS, 2, CH_STATES), lambda b, t, c: (b, 0, 0, 0))],
        out_shape=[jax.ShapeDtypeStruct((rows, SSM_WIDTH), F32),
                   jax.ShapeDtypeStruct((nbatch, SSM_CHUNKS, 2, CH_STATES), F32)],
        scratch_shapes=[pltpu.VMEM((SSM_WIDTH // LANES, SCAN_T, LANES), F32),
                        pltpu.VMEM((SCAN_T, 2 * CH_STATES), F32),
                        pltpu.VMEM((SCAN_T, 2 * CH_STATES), BF16),
                        pltpu.VMEM((CH_IN // LANES, SCAN_T, LANES), F32),
                        pltpu.VMEM((SSM_CHUNKS, 2, CH_STATES), F32)],
        compiler_params=_cparams(("arbitrary", "arbitrary", "arbitrary")),
        name="ssm_prompt",
    )(u, sp["B"], sp["C"], sp["lr"], sp["li"], sp["pr"], sp["pi"], sp["d"])
    return y, hl


def _ssm_sample_body(u_ref, h0_ref, b_ref, c_ref, lr_ref, li_ref, d_ref, y_ref, ho_ref, bu_ref, *, nseq, dec_seq):
    nl = CH_STATES // LANES
    for k in range(SSM_CHUNKS):
        uk = u_ref[:, k * CH_IN:(k + 1) * CH_IN]
        bu = jnp.dot(uk.astype(BF16), b_ref[k], preferred_element_type=F32)
        for c in range(2 * nl):
            bu_ref[c] = bu[:, c * LANES:(c + 1) * LANES]
        base = k * 2 * CH_STATES
        hr = h0_ref[:, base:base + CH_STATES]
        hi = h0_ref[:, base + CH_STATES:base + 2 * CH_STATES]
        ar = lr_ref[k]
        ai = li_ref[k]
        for t in range(dec_seq):
            br = jnp.concatenate([bu_ref[c, pl.ds(t, nseq, stride=dec_seq), :] for c in range(nl)], axis=1)
            bi = jnp.concatenate([bu_ref[nl + c, pl.ds(t, nseq, stride=dec_seq), :] for c in range(nl)], axis=1)
            hr, hi = ar * hr + (br - ai * hi), ar * hi + (bi + ai * hr)
            for c in range(nl):
                bu_ref[c, pl.ds(t, nseq, stride=dec_seq), :] = hr[:, c * LANES:(c + 1) * LANES]
                bu_ref[nl + c, pl.ds(t, nseq, stride=dec_seq), :] = hi[:, c * LANES:(c + 1) * LANES]
        ho_ref[:, base:base + CH_STATES] = hr
        ho_ref[:, base + CH_STATES:base + 2 * CH_STATES] = hi
        hall = jnp.concatenate([bu_ref[c] for c in range(2 * nl)], axis=1)
        y_ref[:, k * CH_IN:(k + 1) * CH_IN] = (
            jnp.dot(hall.astype(BF16), c_ref[k], preferred_element_type=F32) + d_ref[k] * uk)


def _ssm_sample(u, h0, sp, nseq, dec_seq):
    rows = u.shape[0]
    return pl.pallas_call(
        functools.partial(_ssm_sample_body, nseq=nseq, dec_seq=dec_seq),
        grid=(1,),
        in_specs=[_full(u.shape), _full(h0.shape), _full(sp["B"].shape), _full(sp["C"].shape),
                  _full(sp["lr"].shape), _full(sp["li"].shape), _full(sp["d"].shape)],
        out_specs=[_full((rows, SSM_WIDTH)), _full(h0.shape)],
        out_shape=[jax.ShapeDtypeStruct((rows, SSM_WIDTH), F32), jax.ShapeDtypeStruct(h0.shape, F32)],
        scratch_shapes=[pltpu.VMEM((2 * CH_STATES // LANES, rows, LANES), F32)],
        compiler_params=_cparams(("arbitrary",)),
        name="ssm_sample",
    )(u, h0, sp["B"], sp["C"], sp["lr"], sp["li"], sp["d"])


def _ssm_params(lambda_re, lambda_im, log_dt, b_re, b_im, c_re, c_im, d):
    lam = lax.complex(jnp.minimum(lambda_re.astype(F32), -1e-4), lambda_im.astype(F32))
    dt = jnp.exp(log_dt.astype(F32))[:, None]
    lam_dt = lam * dt
    lam_bar = jnp.exp(lam_dt)
    b = lax.complex(b_re.astype(F32), b_im.astype(F32))
    b_bar = ((lam_bar - 1.0) / lam)[..., None] * b
    eye = jnp.eye(CH_GROUPS, dtype=F32)

    def pack_b(part):
        x = part.reshape(SSM_CHUNKS, CH_GROUPS, SSM_STATE, SSM_GROUP)
        return jnp.einsum("kgpc,gh->kgchp", x, eye).reshape(SSM_CHUNKS, CH_IN, CH_STATES)

    def pack_c(part):
        x = part.reshape(SSM_CHUNKS, CH_GROUPS, SSM_GROUP, SSM_STATE)
        return jnp.einsum("kgcp,gh->kgphc", x, eye).reshape(SSM_CHUNKS, CH_STATES, CH_IN)

    bmat = jnp.concatenate([pack_b(jnp.real(b_bar)), pack_b(jnp.imag(b_bar))], axis=2).astype(BF16)
    cmat = jnp.concatenate([pack_c(c_re.astype(F32)), pack_c(-c_im.astype(F32))], axis=1).astype(BF16)
    steps = jnp.arange(1, SCAN_SEG + 1, dtype=F32)[:, None, None]
    pw = jnp.exp(lam_dt[None] * steps)
    pw = pw.reshape(SCAN_SEG, SSM_CHUNKS, CH_STATES).transpose(1, 0, 2)
    lb = lam_bar.reshape(SSM_CHUNKS, 1, CH_STATES)
    return dict(B=bmat, C=cmat, lr=jnp.real(lb), li=jnp.imag(lb), pr=jnp.real(pw), pi=jnp.imag(pw),
                d=d.astype(F32).reshape(SSM_CHUNKS, 1, CH_IN))


def _pack_state(re, im):
    n = re.shape[0]
    st = jnp.stack([re.reshape(n, SSM_CHUNKS, CH_STATES), im.reshape(n, SSM_CHUNKS, CH_STATES)], axis=2)
    return st.reshape(n, SSM_CHUNKS * 2 * CH_STATES).astype(F32)


def _unpack_state(h, n):
    st = h.reshape(n, SSM_CHUNKS, 2, CH_GROUPS, SSM_STATE)
    return (st[:, :, 0].reshape(n, SSM_GROUPS, SSM_STATE), st[:, :, 1].reshape(n, SSM_GROUPS, SSM_STATE))


def _glu_body(y_ref, w_ref, g_ref, o_ref):
    a = jax.nn.gelu(y_ref[...])
    z = a * jax.nn.sigmoid(jnp.dot(a.astype(BF16), w_ref[...], preferred_element_type=F32))
    o_ref[...] = _rms(z, g_ref[...]).astype(BF16)


def _glu(y, w_bf, g, tm):
    rows = y.shape[0]
    spec = pl.BlockSpec((tm, SSM_WIDTH), lambda i: (i, 0))
    return pl.pallas_call(
        _glu_body,
        grid=(rows // tm,),
        in_specs=[spec, _full((SSM_WIDTH, SSM_WIDTH)), _full((1, SSM_WIDTH))],
        out_specs=spec,
        out_shape=jax.ShapeDtypeStruct((rows, SSM_WIDTH), BF16),
        compiler_params=_cparams(("arbitrary",)),
        name="glu_rms",
    )(y, w_bf, g)


def _layer_norm(y, g, b):
    mu = jnp.mean(y, axis=-1, keepdims=True)
    yc = y - mu
    var = jnp.mean(jnp.square(yc), axis=-1, keepdims=True)
    return yc * lax.rsqrt(var + LN_EPS) * g + b


def _outproj_body(*refs, aliased, nblk):
    if aliased:
        refs = refs[1:]
    a_ref, s_ref, x_ref, w_ref, g_ref, b_ref, wrh_ref, wrl_ref, br_ref, o_ref, lg_ref = refs
    i = pl.program_id(0)

    @pl.when(i < nblk)
    def _():
        acc = jnp.dot(a_ref[...], w_ref[0:ATTN_WIDTH, :], preferred_element_type=F32)
        acc = acc + jnp.dot(s_ref[...], w_ref[ATTN_WIDTH:, :], preferred_element_type=F32)
        x1 = _layer_norm(DEEPNORM_ALPHA * x_ref[...] + acc, g_ref[...], b_ref[...])
        _store_row_linear(o_ref, x1)
        xh = x1.astype(BF16)
        xl = (x1 - xh.astype(F32)).astype(BF16)
        lg = jnp.dot(xh, wrh_ref[...], preferred_element_type=F32)
        lg = lg + (jnp.dot(xl, wrh_ref[...], preferred_element_type=F32)
                   + jnp.dot(xh, wrl_ref[...], preferred_element_type=F32))
        lg_ref[...] = lg + br_ref[...]

    @pl.when(i >= nblk)
    def _():
        o_ref[...] = jnp.zeros(o_ref.shape, F32)


def _outproj(an, sn, x, w_bf, g, b, wr, br, tm, total_rows, row_block0, x1_buf=None):
    rows = an.shape[0]
    aliased = x1_buf is not None
    nblk = rows // tm
    tail = 0 if aliased or rows == total_rows else 1
    blk = lambda i: jnp.minimum(i, nblk - 1)
    half = lambda: pl.BlockSpec((tm, ATTN_WIDTH), lambda i: (blk(i), 0))
    in_specs = [half(), half(), pl.BlockSpec((tm, D_MODEL), lambda i: (blk(i), 0)),
                _full((D_MODEL, D_MODEL)), _full((1, D_MODEL)), _full((1, D_MODEL)),
                _full((D_MODEL, LANES)), _full((D_MODEL, LANES)), _full((1, LANES))]
    wr_hi = wr.astype(BF16)
    wr_lo = (wr - wr_hi.astype(F32)).astype(BF16)
    args = [an, sn, x, w_bf, g, b, wr_hi, wr_lo, br]
    if aliased:
        in_specs = [pl.BlockSpec(memory_space=pl.ANY)] + in_specs
        args = [x1_buf] + args
    return pl.pallas_call(
        functools.partial(_outproj_body, aliased=aliased, nblk=nblk),
        grid=(nblk + tail,),
        in_specs=in_specs,
        out_specs=[pl.BlockSpec((tm * ROW_CHUNKS, LANES), lambda i: (i + row_block0, 0)),
                   pl.BlockSpec((tm, LANES), lambda i: (blk(i), 0))],
        out_shape=[jax.ShapeDtypeStruct((total_rows * ROW_CHUNKS, LANES), F32),
                   jax.ShapeDtypeStruct((rows, LANES), F32)],
        input_output_aliases={0: 0} if aliased else {},
        compiler_params=_cparams(("arbitrary",)),
        name="outproj_ln1",
    )(*args)


ROUTE_T = 640
R_E0, R_E1, R_W0, R_W1, R_P0, R_P1 = range(6)


def _route_select(x):
    lane = lax.broadcasted_iota(jnp.int32, x.shape, 1)
    big = jnp.int32(1 << 20)
    neg = -jnp.inf
    gmask = lane < N_EXPERT_GROUPS
    gm = jnp.max(jnp.where(gmask, x, neg), axis=-1, keepdims=True)
    grp = jnp.min(jnp.where(gmask & (x == gm), lane, big), axis=-1, keepdims=True)
    p_grp = 1.0 / jnp.sum(jnp.where(gmask, jnp.exp(x - gm), 0.0), axis=-1, keepdims=True)
    lo = N_EXPERT_GROUPS + grp * EXPERTS_PER_GROUP
    emask = (lane >= lo) & (lane < lo + EXPERTS_PER_GROUP)
    v1 = jnp.max(jnp.where(emask, x, neg), axis=-1, keepdims=True)
    i1 = jnp.min(jnp.where(emask & (x == v1), lane, big), axis=-1, keepdims=True)
    emask2 = emask & (lane != i1)
    v2 = jnp.max(jnp.where(emask2, x, neg), axis=-1, keepdims=True)
    i2 = jnp.min(jnp.where(emask2 & (x == v2), lane, big), axis=-1, keepdims=True)
    e2 = jnp.exp(v2 - v1)
    w1 = p_grp / (1.0 + e2)
    w2 = p_grp * e2 / (1.0 + e2)
    return (i1 - N_EXPERT_GROUPS).astype(F32), (i2 - N_EXPERT_GROUPS).astype(F32), w1, w2


def _route_plan_body(lg_ref, ltri_ref, utri_ref, o_ref, meta_ref, rec_ref, run_ref):
    ph = pl.program_id(0)
    j = pl.program_id(1)
    tm = lg_ref.shape[0]
    lane = lax.broadcasted_iota(jnp.int32, (tm, LANES), 1)
    lanef = lane.astype(F32)
    r0 = pl.multiple_of(j * tm, tm)

    @pl.when((ph == 0) & (j == 0))
    def _():
        run_ref[...] = jnp.zeros(run_ref.shape, F32)

    @pl.when(ph == 0)
    def _():
        e0, e1, w0, w1 = _route_select(lg_ref[...])
        oh0 = (lanef == e0).astype(F32)
        oh1 = (lanef == e1).astype(F32)
        oh2 = oh0 + oh1
        before = jnp.dot(ltri_ref[...], oh2.astype(BF16), preferred_element_type=F32) + run_ref[0:1, :]
        rk0 = jnp.sum(before * oh0, axis=-1, keepdims=True)
        rk1 = jnp.sum(before * oh1, axis=-1, keepdims=True)
        run_ref[...] = run_ref[...] + jnp.sum(oh2, axis=0, keepdims=True)
        rec = jnp.where(lane == R_E0, e0, 0.0)
        rec = jnp.where(lane == R_E1, e1, rec)
        rec = jnp.where(lane == R_W0, w0, rec)
        rec = jnp.where(lane == R_W1, w1, rec)
        rec = jnp.where(lane == R_P0, rk0, rec)
        rec = jnp.where(lane == R_P1, rk1, rec)
        rec_ref[pl.ds(r0, tm), :] = rec

    @pl.when(ph == 1)
    def _():
        counts = run_ref[...]
        nblk = jnp.floor((counts + (MOE_BLOCK - 1)) * (1.0 / MOE_BLOCK))
        end_blk = jnp.dot(nblk.astype(BF16), utri_ref[...], preferred_element_type=F32)
        start_row = (end_blk - nblk) * MOE_BLOCK
        rec = rec_ref[pl.ds(r0, tm), :]
        oh0 = (lanef == rec[:, R_E0:R_E0 + 1]).astype(F32)
        oh1 = (lanef == rec[:, R_E1:R_E1 + 1]).astype(F32)
        p0 = jnp.sum(oh0 * start_row[0:1, :], axis=-1, keepdims=True) + rec[:, R_P0:R_P0 + 1]
        p1 = jnp.sum(oh1 * start_row[0:1, :], axis=-1, keepdims=True) + rec[:, R_P1:R_P1 + 1]
        out = jnp.where(lane == R_P0, p0, rec)
        out = jnp.where(lane == R_P1, p1, out)
        o_ref[...] = out

        @pl.when(j == 0)
        def _():
            meta_ref[...] = end_blk


def _route_plan(logits):
    rows = logits.shape[0]
    tm = ROUTE_T
    ltri = jnp.asarray(np.tril(np.ones((tm, tm), np.float32), -1), BF16)
    utri = jnp.asarray(np.triu(np.ones((LANES, LANES), np.float32)), BF16)
    mrows = SUBLANES
    return pl.pallas_call(
        _route_plan_body,
        grid=(2, rows // tm),
        in_specs=[pl.BlockSpec((tm, LANES), lambda p, j: (j * (1 - p), 0)),
                  pl.BlockSpec((tm, tm), lambda p, j: (0, 0)),
                  pl.BlockSpec((LANES, LANES), lambda p, j: (0, 0))],
        out_specs=[pl.BlockSpec((tm, LANES), lambda p, j: (j * p, 0)),
                   pl.BlockSpec((mrows, LANES), lambda p, j: (0, 0))],
        out_shape=[jax.ShapeDtypeStruct((rows, LANES), F32), jax.ShapeDtypeStruct((mrows, LANES), F32)],
        scratch_shapes=[pltpu.VMEM((rows, LANES), F32), pltpu.VMEM((SUBLANES, LANES), F32)],
        compiler_params=_cparams(("arbitrary", "arbitrary")),
        name="route_plan",
    )(logits, ltri, utri)


def _rowmap_body(pos_ref, o_ref, *, ntok, nrows):
    unroll = 8

    def zero(c, _):
        for u in range(unroll):
            o_ref[c * unroll + u] = 0
        return 0
    lax.fori_loop(0, nrows // unroll, zero, 0)

    def fill(c, _):
        for u in range(unroll):
            t = c * unroll + u
            o_ref[pos_ref[2 * t]] = t
            o_ref[pos_ref[2 * t + 1]] = t
        return 0
    lax.fori_loop(0, ntok // unroll, fill, 0)


def _rowmap(pos, nrows):
    ntok = pos.shape[0] // 2
    return pl.pallas_call(
        functools.partial(_rowmap_body, ntok=ntok, nrows=nrows),
        in_specs=[pl.BlockSpec(memory_space=pltpu.SMEM)],
        out_specs=pl.BlockSpec(memory_space=pltpu.SMEM),
        out_shape=jax.ShapeDtypeStruct((nrows,), jnp.int32),
        name="moe_rowmap",
    )(pos)


def _moe_body(bend_ref, row_tok_ref, x_hbm, wg_ref, wu_ref, wd_ref, ys_hbm,
              xb0, xb1, yb0, yb1, gsem, osem, wg_bf, wu_bf, wd_bf, *, nblocks):
    e = pl.program_id(0)
    nused = bend_ref[N_EXPERTS - 1]
    b1 = bend_ref[e]
    b0 = jnp.where(e == 0, 0, bend_ref[jnp.maximum(e - 1, 0)])
    xbufs = (xb0, xb1)
    ybufs = (yb0, yb1)

    blk_rows = MOE_BLOCK * ROW_CHUNKS

    def gather_start(blk, s):
        base = blk * MOE_BLOCK
        for r in range(MOE_BLOCK):
            src = pl.multiple_of(row_tok_ref[base + r] * ROW_CHUNKS, ROW_CHUNKS)
            pltpu.make_async_copy(x_hbm.at[pl.ds(src, ROW_CHUNKS), :],
                                  xbufs[s].at[pl.ds(r * GATHER_PITCH, ROW_CHUNKS), :], gsem.at[s]).start()

    def gather_wait(s):
        pltpu.make_async_copy(x_hbm.at[pl.ds(0, blk_rows), :], xbufs[s].at[pl.ds(0, blk_rows), :], gsem.at[s]).wait()

    def out_copy(blk, s):
        row0 = pl.multiple_of(blk * blk_rows, blk_rows)
        return pltpu.make_async_copy(ybufs[s], ys_hbm.at[pl.ds(row0, blk_rows), :], osem.at[s])

    @pl.when(e == 0)
    def _():
        gather_start(0, 0)

    @pl.when(b1 > b0)
    def _():
        wg_bf[...] = wg_ref[...].astype(BF16)
        wu_bf[...] = wu_ref[...].astype(BF16)
        wd_bf[...] = wd_ref[...].astype(BF16)

        def do_block(b, s):
            gather_wait(s)

            @pl.when(b >= 2)
            def _():
                out_copy(b - 2, s).wait()

            gather_start(jnp.minimum(b + 1, nused - 1), 1 - s)
            x = _load_row_linear(xbufs[s], MOE_BLOCK, GATHER_PITCH).astype(BF16)
            hg = jnp.dot(x, wg_bf[...], preferred_element_type=F32)
            hu = jnp.dot(x, wu_bf[...], preferred_element_type=F32)
            h = (jax.nn.silu(hg) * hu).astype(BF16)
            _store_row_linear(ybufs[s], jnp.dot(h, wd_bf[...], preferred_element_type=F32))
            out_copy(b, s).start()

        def body(b, _):
            for s in range(2):
                @pl.when(b % 2 == s)
                def _():
                    do_block(b, s)
            return 0

        lax.fori_loop(b0, b1, body, 0)

    @pl.when(e == N_EXPERTS - 1)
    def _():
        last = nused - 1
        for s in range(2):
            @pl.when(last % 2 == s)
            def _():
                gather_wait(1 - s)
        out_copy(0, 0).wait()
        out_copy(0, 1).wait()
        yb0[...] = jnp.zeros(yb0.shape, F32)

        def zstart(b, _):
            out_copy(b, 0).start()
            return 0

        def zwait(b, _):
            out_copy(b, 0).wait()
            return 0

        lax.fori_loop(nused, nblocks, zstart, 0)
        lax.fori_loop(nused, nblocks, zwait, 0)


def _moe(bend, row_tok, x1, w_gate, w_up, w_down, layer):
    nblocks = row_tok.shape[0] // MOE_BLOCK
    assert 2 * (x1.shape[0] // ROW_CHUNKS) > MOE_BLOCK
    wspec = lambda a, b: pl.BlockSpec((None, None, a, b), lambda e, be, rt: (layer, e, 0, 0))
    grid_spec = pltpu.PrefetchScalarGridSpec(
        num_scalar_prefetch=2,
        grid=(N_EXPERTS,),
        in_specs=[pl.BlockSpec(memory_space=pl.ANY), wspec(D_MODEL, EXPERT_FF), wspec(D_MODEL, EXPERT_FF),
                  wspec(EXPERT_FF, D_MODEL)],
        out_specs=pl.BlockSpec(memory_space=pl.ANY),
        scratch_shapes=[pltpu.VMEM((MOE_BLOCK * GATHER_PITCH, LANES), F32),
                        pltpu.VMEM((MOE_BLOCK * GATHER_PITCH, LANES), F32),
                        pltpu.VMEM((MOE_BLOCK * ROW_CHUNKS, LANES), F32),
                        pltpu.VMEM((MOE_BLOCK * ROW_CHUNKS, LANES), F32),
                        pltpu.SemaphoreType.DMA((2,)), pltpu.SemaphoreType.DMA((2,)),
                        pltpu.VMEM((D_MODEL, EXPERT_FF), BF16),
                        pltpu.VMEM((D_MODEL, EXPERT_FF), BF16),
                        pltpu.VMEM((EXPERT_FF, D_MODEL), BF16)],
    )
    return pl.pallas_call(
        functools.partial(_moe_body, nblocks=nblocks),
        grid_spec=grid_spec,
        out_shape=jax.ShapeDtypeStruct((nblocks * MOE_BLOCK * ROW_CHUNKS, LANES), F32),
        compiler_params=_cparams(("arbitrary",), bounds_checks=False),
        name="moe_experts",
    )(bend, row_tok, x1, w_gate, w_up, w_down)


COMB_T = 128


def _combine_body(pos_ref, ys_hbm, x_ref, rt_ref, g_ref, b_ref, op_ref, os_ref,
                  ga0, gb0, ga1, gb1, sem, *, n_prompt_blocks):
    i = pl.program_id(0)
    nsteps = pl.num_programs(0)
    bufs = ((ga0, gb0), (ga1, gb1))

    tile_rows = COMB_T * ROW_CHUNKS

    def gather_start(step, s):
        base = step * (2 * COMB_T)
        for r in range(COMB_T):
            for k in range(2):
                src = pl.multiple_of(pos_ref[base + 2 * r + k] * ROW_CHUNKS, ROW_CHUNKS)
                pltpu.make_async_copy(ys_hbm.at[pl.ds(src, ROW_CHUNKS), :],
                                      bufs[s][k].at[pl.ds(r * GATHER_PITCH, ROW_CHUNKS), :], sem.at[s]).start()

    def gather_wait(s):
        for k in range(2):
            pltpu.make_async_copy(ys_hbm.at[pl.ds(0, tile_rows), :], bufs[s][k].at[pl.ds(0, tile_rows), :],
                                  sem.at[s]).wait()

    @pl.when(i == 0)
    def _():
        gather_start(0, 0)

    def do_tile(s):
        gather_wait(s)
        gather_start(jnp.minimum(i + 1, nsteps - 1), 1 - s)
        rt = rt_ref[...]
        ff = (rt[:, R_W0:R_W0 + 1] * _load_row_linear(bufs[s][0], COMB_T, GATHER_PITCH)
              + rt[:, R_W1:R_W1 + 1] * _load_row_linear(bufs[s][1], COMB_T, GATHER_PITCH))
        out = _layer_norm(DEEPNORM_ALPHA * _load_row_linear(x_ref, COMB_T) + ff, g_ref[...], b_ref[...])

        @pl.when(i < n_prompt_blocks)
        def _():
            op_ref[...] = out

        @pl.when(i >= n_prompt_blocks)
        def _():
            os_ref[...] = out

        @pl.when(i == nsteps - 1)
        def _():
            gather_wait(1 - s)

    for s in range(2):
        @pl.when(i % 2 == s)
        def _():
            do_tile(s)


def _combine(pos, ys, x1, route, g, b, n_prompt_rows):
    rows = x1.shape[0] // ROW_CHUNKS
    npb = n_prompt_rows // COMB_T
    grid_spec = pltpu.PrefetchScalarGridSpec(
        num_scalar_prefetch=1,
        grid=(rows // COMB_T,),
        in_specs=[pl.BlockSpec(memory_space=pl.ANY),
                  pl.BlockSpec((COMB_T * ROW_CHUNKS, LANES), lambda i, p: (i, 0)),
                  pl.BlockSpec((COMB_T, LANES), lambda i, p: (i, 0)),
                  pl.BlockSpec((1, D_MODEL), lambda i, p: (0, 0)),
                  pl.BlockSpec((1, D_MODEL), lambda i, p: (0, 0))],
        out_specs=[pl.BlockSpec((COMB_T, D_MODEL), lambda i, p: (jnp.minimum(i, npb - 1), 0)),
                   pl.BlockSpec((COMB_T, D_MODEL), lambda i, p: (jnp.maximum(i - npb, 0), 0))],
        scratch_shapes=([pltpu.VMEM((COMB_T * GATHER_PITCH, LANES), F32) for _ in range(4)]
                        + [pltpu.SemaphoreType.DMA((2,))]),
    )
    return pl.pallas_call(
        functools.partial(_combine_body, n_prompt_blocks=npb),
        grid_spec=grid_spec,
        out_shape=[jax.ShapeDtypeStruct((n_prompt_rows, D_MODEL), F32),
                   jax.ShapeDtypeStruct((rows - n_prompt_rows, D_MODEL), F32)],
        compiler_params=_cparams(("arbitrary",), bounds_checks=False),
        name="combine_ln2",
    )(pos, ys, x1, route, g, b)


def kernel(x_prompt, x_sample, cache_k, cache_v, state_ssm_re, state_ssm_im, w_in, ssm_lambda_re, ssm_lambda_im, ssm_log_dt, ssm_b_re, ssm_b_im, ssm_c_re, ssm_c_im, ssm_d, ssm_w_glu, attn_sinks, attn_norm_g, ssm_norm_g, w_out, ln1_g, ln1_b, router_group_w, router_group_b, router_expert_w, router_expert_b, expert_w_gate, expert_w_up, expert_w_down, ln2_g, ln2_b):
    nbatch, seq, _ = x_prompt.shape
    nseq, dec_seq, _ = x_sample.shape
    wbuf = cache_k.shape[2]
    rows_p = nbatch * seq
    rows_s = nseq * dec_seq
    rows_all = rows_p + rows_s
    tm_p = 512

    xp = x_prompt.reshape(rows_p, D_MODEL)
    xs = x_sample.reshape(rows_s, D_MODEL)
    tab_p = _rope_tables(np.arange(seq))
    tab_s = _rope_tables(PAST_LEN + (np.arange(rows_s) % dec_seq))
    ck_all = cache_k.astype(F32).reshape(DEPTH, nseq, wbuf, KV_WIDTH)
    cv_all = cache_v.astype(F32).reshape(DEPTH, nseq, wbuf, KV_WIDTH)

    outs = {k: [] for k in ("kp", "vp", "hrp", "hip", "ks", "vs", "hrs", "his")}
    for l in range(DEPTH):
        w_in_bf = w_in[l].astype(BF16)
        w_glu_bf = ssm_w_glu[l].astype(BF16)
        w_out_bf = w_out[l].astype(BF16)
        sp = _ssm_params(ssm_lambda_re[l], ssm_lambda_im[l], ssm_log_dt[l], ssm_b_re[l], ssm_b_im[l],
                         ssm_c_re[l], ssm_c_im[l], ssm_d[l])
        sinks = attn_sinks[l].astype(F32)
        attn_g = attn_norm_g[l].astype(F32).reshape(1, ATTN_WIDTH)
        ssm_g = ssm_norm_g[l].astype(F32).reshape(1, SSM_WIDTH)
        g1 = ln1_g[l].astype(F32).reshape(1, D_MODEL)
        b1 = ln1_b[l].astype(F32).reshape(1, D_MODEL)
        g2 = ln2_g[l].astype(F32).reshape(1, D_MODEL)
        b2 = ln2_b[l].astype(F32).reshape(1, D_MODEL)
        n_rt = N_EXPERT_GROUPS + N_EXPERTS
        wr = jnp.concatenate([router_group_w[l].astype(F32)]
                             + [router_expert_w[l, g].astype(F32) for g in range(N_EXPERT_GROUPS)]
                             + [jnp.zeros((D_MODEL, LANES - n_rt), F32)], axis=1)
        br = jnp.concatenate([router_group_b[l].astype(F32), router_expert_b[l].astype(F32).reshape(-1),
                              jnp.zeros((LANES - n_rt,), F32)]).reshape(1, LANES)

        q_p, k_p, v_p, u_p = _inproj(xp, w_in_bf, *tab_p, tm=tm_p)
        q_s, k_s, v_s, u_s = _inproj(xs, w_in_bf, *tab_s, tm=rows_s)

        an_p = _attn_prompt(sinks, q_p, k_p, v_p, attn_g, seq)
        an_s, nk_s, nv_s = _attn_sample(sinks, q_s, k_s, v_s, ck_all, cv_all, attn_g, dec_seq, l)

        y_p, hl_p = _ssm_prompt(u_p, sp, nbatch, seq)
        y_s, hl_s = _ssm_sample(u_s, _pack_state(state_ssm_re[l], state_ssm_im[l]), sp, nseq, dec_seq)
        sn_p = _glu(y_p, w_glu_bf, ssm_g, tm_p)
        sn_s = _glu(y_s, w_glu_bf, ssm_g, rows_s)

        x1, lg_p = _outproj(an_p, sn_p, xp, w_out_bf, g1, b1, wr, br, tm_p, rows_all, 0)
        x1, lg_s = _outproj(an_s, sn_s, xs, w_out_bf, g1, b1, wr, br, rows_s, rows_all, rows_p // rows_s, x1_buf=x1)
        nblocks = 2 * rows_all // MOE_BLOCK + N_EXPERTS
        route, meta = _route_plan(jnp.concatenate([lg_p, lg_s], axis=0))
        pos = route[:, R_P0:R_P1 + 1].astype(jnp.int32).reshape(2 * rows_all)
        bend = meta[0, :N_EXPERTS].astype(jnp.int32)
        row_tok = _rowmap(pos, nblocks * MOE_BLOCK)

        ys = _moe(bend, row_tok, x1, expert_w_gate, expert_w_up, expert_w_down, l)
        xp, xs = _combine(pos, ys, x1, route, g2, b2, rows_p)

        wp = min(WINDOW, seq)
        outs["kp"].append(k_p.reshape(nbatch, seq, N_KV_HEADS, HEAD_DIM)[:, seq - wp:])
        outs["vp"].append(v_p.reshape(nbatch, seq, N_KV_HEADS, HEAD_DIM)[:, seq - wp:])
        hp = hl_p.reshape(nbatch, SSM_CHUNKS * 2 * CH_STATES)
        hr, hi = _unpack_state(hp, nbatch)
        outs["hrp"].append(hr)
        outs["hip"].append(hi)
        outs["ks"].append(nk_s.reshape(nseq, wbuf, N_KV_HEADS, HEAD_DIM))
        outs["vs"].append(nv_s.reshape(nseq, wbuf, N_KV_HEADS, HEAD_DIM))
        hr, hi = _unpack_state(hl_s, nseq)
        outs["hrs"].append(hr)
        outs["his"].append(hi)

    st = {k: jnp.stack(v) for k, v in outs.items()}
    return (xp.reshape(nbatch, seq, D_MODEL), xs.reshape(nseq, dec_seq, D_MODEL),
            st["kp"], st["vp"], st["hrp"], st["hip"], st["ks"], st["vs"], st["hrs"], st["his"])
```

```python
import functools
import math

import numpy as np
import jax
import jax.numpy as jnp
from jax import lax
from jax.experimental import pallas as pl
from jax.experimental.pallas import tpu as pltpu

F32 = jnp.float32
BF16 = jnp.bfloat16

D_MODEL = 2048
DEPTH = 2
PAST_LEN = 16384
ATTN_WIDTH = 1024
SSM_WIDTH = 1024
HEAD_DIM = 64
N_HEADS = 16
N_KV_HEADS = 4
Q_PER_KV = 4
KV_WIDTH = 256
WINDOW = 128
ROPE_THETA = 10000.0
SSM_GROUP = 16
SSM_GROUPS = 64
SSM_STATE = 64
N_EXPERT_GROUPS = 4
EXPERTS_PER_GROUP = 8
N_EXPERTS = 32
EXPERT_FF = 512
MOE_BLOCK = 128
DEEPNORM_ALPHA = (2.0 * DEPTH) ** 0.25
LN_EPS = 1e-5
RMS_EPS = 1e-6

LANES = 128
SUBLANES = 8
VMEM_LIMIT = 56 * 1024 * 1024

SSM_CHUNKS = 4
CH_GROUPS = SSM_GROUPS // SSM_CHUNKS
CH_STATES = CH_GROUPS * SSM_STATE
CH_IN = CH_GROUPS * SSM_GROUP
SCAN_T = 512
SCAN_SEG = SCAN_T // SUBLANES
SCAN_UNROLL = 4


def _cparams(sem=None, vmem=VMEM_LIMIT, bounds_checks=True):
    return pltpu.CompilerParams(dimension_semantics=sem, vmem_limit_bytes=vmem,
                                disable_bounds_checks=not bounds_checks)


def _full(shape):
    n = len(shape)
    return pl.BlockSpec(shape, lambda *a: (0,) * n)


GATHER_PRIORITY = 1


def _inproj_body(x_ref, w_ref, cos_ref, sa_ref, sb_ref, q_ref, k_ref, v_ref, u_ref):
    x = x_ref[...].astype(BF16)
    cos = cos_ref[...]
    sa = sa_ref[...]
    sb = sb_ref[...]

    def rope(c):
        return c * cos + pltpu.roll(c, LANES - 32, 1) * sa + pltpu.roll(c, 32, 1) * sb

    pq = jnp.dot(x, w_ref[:, 0:ATTN_WIDTH], preferred_element_type=F32)
    for j in range(ATTN_WIDTH // LANES):
        q_ref[:, j * LANES:(j + 1) * LANES] = rope(pq[:, j * LANES:(j + 1) * LANES]).astype(BF16)
    pk = jnp.dot(x, w_ref[:, ATTN_WIDTH:ATTN_WIDTH + KV_WIDTH], preferred_element_type=F32)
    for j in range(KV_WIDTH // LANES):
        k_ref[:, j * LANES:(j + 1) * LANES] = rope(pk[:, j * LANES:(j + 1) * LANES])
    v_ref[...] = jnp.dot(x, w_ref[:, ATTN_WIDTH + KV_WIDTH:ATTN_WIDTH + 2 * KV_WIDTH],
                         preferred_element_type=F32)
    u_ref[...] = jnp.dot(x, w_ref[:, ATTN_WIDTH + 2 * KV_WIDTH:], preferred_element_type=F32)


def _inproj(x, w_bf, cos_t, sa_t, sb_t, tm):
    rows = x.shape[0]
    tab_blocks = cos_t.shape[0] // tm
    in_w = w_bf.shape[1]
    row_spec = lambda w: pl.BlockSpec((tm, w), lambda i: (i, 0))
    tab_spec = pl.BlockSpec((tm, LANES), lambda i: (i % tab_blocks, 0))
    return pl.pallas_call(
        _inproj_body,
        grid=(rows // tm,),
        in_specs=[row_spec(D_MODEL), _full((D_MODEL, in_w)), tab_spec, tab_spec, tab_spec],
        out_specs=[row_spec(ATTN_WIDTH), row_spec(KV_WIDTH), row_spec(KV_WIDTH), row_spec(SSM_WIDTH)],
        out_shape=[jax.ShapeDtypeStruct((rows, ATTN_WIDTH), BF16),
                   jax.ShapeDtypeStruct((rows, KV_WIDTH), F32),
                   jax.ShapeDtypeStruct((rows, KV_WIDTH), F32),
                   jax.ShapeDtypeStruct((rows, SSM_WIDTH), F32)],
        compiler_params=_cparams(("arbitrary",)),
        name="inproj_rope",
    )(x, w_bf, cos_t, sa_t, sb_t)


def _rope_tables(positions):
    half = HEAD_DIM // 2
    inv_freq = ROPE_THETA ** (-np.arange(half, dtype=np.float64) / half)
    ang = positions.astype(np.float64)[:, None] * inv_freq[None, :]
    lane = np.arange(LANES)
    cos = np.cos(ang)[:, lane % half]
    sin = np.sin(ang)[:, lane % half]
    first = (lane % HEAD_DIM) < half
    sa = np.where(first[None, :], -sin, 0.0)
    sb = np.where(first[None, :], 0.0, sin)
    return (jnp.asarray(cos, F32), jnp.asarray(sa, F32), jnp.asarray(sb, F32))


def _rms(x, g):
    return x * lax.rsqrt(jnp.mean(jnp.square(x), axis=-1, keepdims=True) + RMS_EPS) * g


def _attn_prompt_body(sink_ref, q_ref, kp_ref, kc_ref, vp_ref, vc_ref, g_ref, o_ref, acc_ref, *, blocks_per_seq):
    i = pl.program_id(0)
    has_prev = (i % blocks_per_seq) > 0
    kk = jnp.concatenate([kp_ref[...], kc_ref[...]], axis=0).astype(BF16)
    vv = jnp.concatenate([vp_ref[...], vc_ref[...]], axis=0).astype(BF16)
    row = lax.broadcasted_iota(jnp.int32, (WINDOW, 2 * WINDOW), 0)
    col = lax.broadcasted_iota(jnp.int32, (WINDOW, 2 * WINDOW), 1)
    visible = (col > row) & (col <= row + WINDOW) & ((col >= WINDOW) | has_prev)
    q = q_ref[...]
    for h in range(N_HEADS):
        g = h // Q_PER_KV
        qh = q[:, h * HEAD_DIM:(h + 1) * HEAD_DIM]
        kh = kk[:, g * HEAD_DIM:(g + 1) * HEAD_DIM]
        vh = vv[:, g * HEAD_DIM:(g + 1) * HEAD_DIM]
        s = lax.dot_general(qh, kh, (((1,), (1,)), ((), ())), preferred_element_type=F32) * (HEAD_DIM ** -0.5)
        s = jnp.where(visible, s, -jnp.inf)
        sink = sink_ref[h]
        m = jnp.maximum(jnp.max(s, axis=-1, keepdims=True), sink)
        p = jnp.exp(s - m)
        denom = jnp.sum(p, axis=-1, keepdims=True) + jnp.exp(sink - m)
        oh = jnp.dot(p.astype(BF16), vh, preferred_element_type=F32) / denom
        acc_ref[:, h * HEAD_DIM:(h + 1) * HEAD_DIM] = oh
    o_ref[...] = _rms(acc_ref[...], g_ref[...]).astype(BF16)


def _attn_prompt(sinks, q, k, v, g, seq):
    rows = q.shape[0]
    nblk = rows // WINDOW
    bps = seq // WINDOW
    cur = lambda w: pl.BlockSpec((WINDOW, w), lambda i: (i, 0))
    prev = lambda w: pl.BlockSpec((WINDOW, w), lambda i: (jnp.maximum(i - 1, 0), 0))
    return pl.pallas_call(
        functools.partial(_attn_prompt_body, blocks_per_seq=bps),
        grid=(nblk,),
        in_specs=[pl.BlockSpec(memory_space=pltpu.SMEM), cur(ATTN_WIDTH), prev(KV_WIDTH), cur(KV_WIDTH),
                  prev(KV_WIDTH), cur(KV_WIDTH), _full((1, ATTN_WIDTH))],
        out_specs=cur(ATTN_WIDTH),
        out_shape=jax.ShapeDtypeStruct((rows, ATTN_WIDTH), BF16),
        scratch_shapes=[pltpu.VMEM((WINDOW, ATTN_WIDTH), F32)],
        compiler_params=_cparams(("arbitrary",)),
        name="attn_prompt",
    )(sinks, q, k, k, v, v, g)


SEQ_PER_STEP = 8


def _attn_sample_body(sink_ref, q_ref, kn_ref, vn_ref, ck_ref, cv_ref, g_ref, o_ref, nk_ref, nv_ref, acc_ref,
                      *, dec_seq):
    nq = SEQ_PER_STEP * dec_seq
    wb = ck_ref.shape[1]
    nkc = SEQ_PER_STEP * wb
    ck = ck_ref[...].reshape(nkc, KV_WIDTH)
    cv = cv_ref[...].reshape(nkc, KV_WIDTH)
    kn = kn_ref[...]
    vn = vn_ref[...]
    ckb = ck.astype(BF16)
    cvb = cv.astype(BF16)
    knb = kn.astype(BF16)
    vnb = vn.astype(BF16)
    q = q_ref[...]

    rows = Q_PER_KV * nq
    r = lax.broadcasted_iota(jnp.int32, (rows, nkc), 0) % nq
    c = lax.broadcasted_iota(jnp.int32, (rows, nkc), 1)
    vis_c = ((r // dec_seq) == (c // wb)) & ((c % wb) >= (r % dec_seq) + 1 + (wb - WINDOW))
    rn = lax.broadcasted_iota(jnp.int32, (rows, nq), 0) % nq
    cn = lax.broadcasted_iota(jnp.int32, (rows, nq), 1)
    vis_n = ((rn // dec_seq) == (cn // dec_seq)) & ((cn % dec_seq) <= (rn % dec_seq))
    hrow = lax.broadcasted_iota(jnp.int32, (rows, 1), 0) // nq

    dn = (((1,), (1,)), ((), ()))
    for g in range(N_KV_HEADS):
        qs = jnp.concatenate([q[:, (g * Q_PER_KV + j) * HEAD_DIM:(g * Q_PER_KV + j + 1) * HEAD_DIM]
                              for j in range(Q_PER_KV)], axis=0)
        sl = slice(g * HEAD_DIM, (g + 1) * HEAD_DIM)
        sc = lax.dot_general(qs, ckb[:, sl], dn, preferred_element_type=F32) * (HEAD_DIM ** -0.5)
        sn = lax.dot_general(qs, knb[:, sl], dn, preferred_element_type=F32) * (HEAD_DIM ** -0.5)
        sc = jnp.where(vis_c, sc, -jnp.inf)
        sn = jnp.where(vis_n, sn, -jnp.inf)
        sink = jnp.zeros((rows, 1), F32)
        for j in range(Q_PER_KV):
            sink = jnp.where(hrow == j, sink_ref[g * Q_PER_KV + j], sink)
        m = jnp.maximum(jnp.maximum(jnp.max(sc, axis=-1, keepdims=True), jnp.max(sn, axis=-1, keepdims=True)), sink)
        pc = jnp.exp(sc - m)
        pn = jnp.exp(sn - m)
        denom = jnp.sum(pc, axis=-1, keepdims=True) + jnp.sum(pn, axis=-1, keepdims=True) + jnp.exp(sink - m)
        o = (jnp.dot(pc.astype(BF16), cvb[:, sl], preferred_element_type=F32)
             + jnp.dot(pn.astype(BF16), vnb[:, sl], preferred_element_type=F32)) / denom
        for j in range(Q_PER_KV):
            h = g * Q_PER_KV + j
            acc_ref[:, h * HEAD_DIM:(h + 1) * HEAD_DIM] = o[j * nq:(j + 1) * nq, :]
    o_ref[...] = _rms(acc_ref[...], g_ref[...]).astype(BF16)

    keep = wb - dec_seq
    for s in range(SEQ_PER_STEP):
        nk_ref[s, 0:keep, :] = ck_ref[s, dec_seq:wb, :]
        nv_ref[s, 0:keep, :] = cv_ref[s, dec_seq:wb, :]
        nk_ref[s, keep:wb, :] = kn[s * dec_seq:(s + 1) * dec_seq, :]
        nv_ref[s, keep:wb, :] = vn[s * dec_seq:(s + 1) * dec_seq, :]


def _attn_sample(sinks, q, kn, vn, ck, cv, g, dec_seq, layer):
    rows = q.shape[0]
    _, nseq, wb, _ = ck.shape
    nq = SEQ_PER_STEP * dec_seq
    rowb = lambda w: pl.BlockSpec((nq, w), lambda i: (i, 0))
    cin = pl.BlockSpec((None, SEQ_PER_STEP, wb, KV_WIDTH), lambda i: (layer, i, 0, 0))
    cb = pl.BlockSpec((SEQ_PER_STEP, wb, KV_WIDTH), lambda i: (i, 0, 0))
    return pl.pallas_call(
        functools.partial(_attn_sample_body, dec_seq=dec_seq),
        grid=(nseq // SEQ_PER_STEP,),
        in_specs=[pl.BlockSpec(memory_space=pltpu.SMEM), rowb(ATTN_WIDTH), rowb(KV_WIDTH), rowb(KV_WIDTH),
                  cin, cin, _full((1, ATTN_WIDTH))],
        out_specs=[rowb(ATTN_WIDTH), cb, cb],
        out_shape=[jax.ShapeDtypeStruct((rows, ATTN_WIDTH), BF16),
                   jax.ShapeDtypeStruct(ck.shape[1:], F32), jax.ShapeDtypeStruct(cv.shape[1:], F32)],
        scratch_shapes=[pltpu.VMEM((nq, ATTN_WIDTH), F32)],
        compiler_params=_cparams(("arbitrary",)),
        name="attn_sample",
    )(sinks, q, kn, vn, ck, cv, g)


def _ssm_prompt_body(u_ref, b_ref, c_ref, lr_ref, li_ref, pr_ref, pi_ref, d_ref, y_ref, hl_ref,
                     uperm_ref, bu_ref, hb_ref, yp_ref, carry_ref):
    tc = pl.program_id(1)
    cc = pl.program_id(2)

    @pl.when(cc == 0)
    def _():
        for k in range(SSM_WIDTH // LANES):
            for s in range(SUBLANES):
                uperm_ref[k, pl.ds(s, SCAN_SEG, stride=SUBLANES), :] = (
                    u_ref[s * SCAN_SEG:(s + 1) * SCAN_SEG, k * LANES:(k + 1) * LANES])

    @pl.when(tc == 0)
    def _():
        carry_ref[cc] = jnp.zeros((2, CH_STATES), F32)

    lanes_per_chunk = CH_IN // LANES
    up = jnp.concatenate([uperm_ref[cc * lanes_per_chunk + k] for k in range(lanes_per_chunk)], axis=1)
    bu_ref[...] = jnp.dot(up.astype(BF16), b_ref[cc], preferred_element_type=F32)

    ar = jnp.broadcast_to(lr_ref[cc], (SUBLANES, CH_STATES))
    ai = jnp.broadcast_to(li_ref[cc], (SUBLANES, CH_STATES))

    def advance(h, r0):
        hr, hi = h
        br = bu_ref[pl.ds(r0, SUBLANES), 0:CH_STATES]
        bi = bu_ref[pl.ds(r0, SUBLANES), CH_STATES:2 * CH_STATES]
        return ar * hr + (br - ai * hi), ar * hi + (bi + ai * hr)

    def step(jj, h):
        r0 = pl.multiple_of(jj * SCAN_UNROLL * SUBLANES, SCAN_UNROLL * SUBLANES)
        for k in range(SCAN_UNROLL):
            h = advance(h, r0 + k * SUBLANES)
        return h

    zero = jnp.zeros((SUBLANES, CH_STATES), F32)
    fr, fi = lax.fori_loop(0, SCAN_SEG // SCAN_UNROLL, step, (zero, zero))

    cst = carry_ref[cc]
    c_r = cst[0:1, :]
    c_i = cst[1:2, :]
    lsr = pr_ref[cc]
    lsi = pi_ref[cc]
    rows_r, rows_i = [], []
    for s in range(SUBLANES):
        rows_r.append(c_r)
        rows_i.append(c_i)
        n_r = fr[s:s + 1, :] + (lsr * c_r - lsi * c_i)
        n_i = fi[s:s + 1, :] + (lsr * c_i + lsi * c_r)
        c_r, c_i = n_r, n_i
    carry_ref[cc] = jnp.concatenate([c_r, c_i], axis=0)
    hl_ref[0, cc] = jnp.concatenate([c_r, c_i], axis=0)
    cm_r = jnp.concatenate(rows_r, axis=0)
    cm_i = jnp.concatenate(rows_i, axis=0)

    def emit(jj, h):
        r0 = pl.multiple_of(jj * 2 * SUBLANES, 2 * SUBLANES)
        h1 = advance(h, r0)
        h2 = advance(h1, r0 + SUBLANES)
        hb_ref[pl.ds(r0, 2 * SUBLANES), 0:CH_STATES] = jnp.concatenate([h1[0], h2[0]], axis=0).astype(BF16)
        hb_ref[pl.ds(r0, 2 * SUBLANES), CH_STATES:2 * CH_STATES] = (
            jnp.concatenate([h1[1], h2[1]], axis=0).astype(BF16))
        return h2

    lax.fori_loop(0, SCAN_SEG // 2, emit, (cm_r, cm_i))

    yp = jnp.dot(hb_ref[...], c_ref[cc], preferred_element_type=F32) + d_ref[cc] * up
    for k in range(lanes_per_chunk):
        yp_ref[k] = yp[:, k * LANES:(k + 1) * LANES]
    for s in range(SUBLANES):
        for k in range(lanes_per_chunk):
            y_ref[s * SCAN_SEG:(s + 1) * SCAN_SEG, k * LANES:(k + 1) * LANES] = (
                yp_ref[k, pl.ds(s, SCAN_SEG, stride=SUBLANES), :])


def _ssm_prompt(u, sp, nbatch, seq):
    rows = u.shape[0]
    ntc = seq // SCAN_T
    y, hl = pl.pallas_call(
        _ssm_prompt_body,
        grid=(nbatch, ntc, SSM_CHUNKS),
        in_specs=[pl.BlockSpec((SCAN_T, SSM_WIDTH), lambda b, t, c: (b * ntc + t, 0)),
                  _full(sp["B"].shape), _full(sp["C"].shape), _full(sp["lr"].shape), _full(sp["li"].shape),
                  _full(sp["pr"].shape), _full(sp["pi"].shape), _full(sp["d"].shape)],
        out_specs=[pl.BlockSpec((SCAN_T, CH_IN), lambda b, t, c: (b * ntc + t, c)),
                   pl.BlockSpec((1, SSM_CHUNKS, 2, CH_STATES), lambda b, t, c: (b, 0, 0, 0))],
        out_shape=[jax.ShapeDtypeStruct((rows, SSM_WIDTH), F32),
                   jax.ShapeDtypeStruct((nbatch, SSM_CHUNKS, 2, CH_STATES), F32)],
        scratch_shapes=[pltpu.VMEM((SSM_WIDTH // LANES, SCAN_T, LANES), F32),
                        pltpu.VMEM((SCAN_T, 2 * CH_STATES), F32),
                        pltpu.VMEM((SCAN_T, 2 * CH_STATES), BF16),
                        pltpu.VMEM((CH_IN // LANES, SCAN_T, LANES), F32),
                        pltpu.VMEM((SSM_CHUNKS, 2, CH_STATES), F32)],
        compiler_params=_cparams(("arbitrary", "arbitrary", "arbitrary")),
        name="ssm_prompt",
    )(u, sp["B"], sp["C"], sp["lr"], sp["li"], sp["pr"], sp["pi"], sp["d"])
    return y, hl


def _ssm_sample_body(u_ref, h0_ref, b_ref, c_ref, lr_ref, li_ref, d_ref, y_ref, ho_ref, bu_ref, *, nseq, dec_seq):
    nl = CH_STATES // LANES
    for k in range(SSM_CHUNKS):
        uk = u_ref[:, k * CH_IN:(k + 1) * CH_IN]
        bu = jnp.dot(uk.astype(BF16), b_ref[k], preferred_element_type=F32)
        for c in range(2 * nl):
            bu_ref[c] = bu[:, c * LANES:(c + 1) * LANES]
        base = k * 2 * CH_STATES
        hr = h0_ref[:, base:base + CH_STATES]
        hi = h0_ref[:, base + CH_STATES:base + 2 * CH_STATES]
        ar = lr_ref[k]
        ai = li_ref[k]
        for t in range(dec_seq):
            br = jnp.concatenate([bu_ref[c, pl.ds(t, nseq, stride=dec_seq), :] for c in range(nl)], axis=1)
            bi = jnp.concatenate([bu_ref[nl + c, pl.ds(t, nseq, stride=dec_seq), :] for c in range(nl)], axis=1)
            hr, hi = ar * hr + (br - ai * hi), ar * hi + (bi + ai * hr)
            for c in range(nl):
                bu_ref[c, pl.ds(t, nseq, stride=dec_seq), :] = hr[:, c * LANES:(c + 1) * LANES]
                bu_ref[nl + c, pl.ds(t, nseq, stride=dec_seq), :] = hi[:, c * LANES:(c + 1) * LANES]
        ho_ref[:, base:base + CH_STATES] = hr
        ho_ref[:, base + CH_STATES:base + 2 * CH_STATES] = hi
        hall = jnp.concatenate([bu_ref[c] for c in range(2 * nl)], axis=1)
        y_ref[:, k * CH_IN:(k + 1) * CH_IN] = (
            jnp.dot(hall.astype(BF16), c_ref[k], preferred_element_type=F32) + d_ref[k] * uk)


def _ssm_sample(u, h0, sp, nseq, dec_seq):
    rows = u.shape[0]
    return pl.pallas_call(
        functools.partial(_ssm_sample_body, nseq=nseq, dec_seq=dec_seq),
        grid=(1,),
        in_specs=[_full(u.shape), _full(h0.shape), _full(sp["B"].shape), _full(sp["C"].shape),
                  _full(sp["lr"].shape), _full(sp["li"].shape), _full(sp["d"].shape)],
        out_specs=[_full((rows, SSM_WIDTH)), _full(h0.shape)],
        out_shape=[jax.ShapeDtypeStruct((rows, SSM_WIDTH), F32), jax.ShapeDtypeStruct(h0.shape, F32)],
        scratch_shapes=[pltpu.VMEM((2 * CH_STATES // LANES, rows, LANES), F32)],
        compiler_params=_cparams(("arbitrary",)),
        name="ssm_sample",
    )(u, h0, sp["B"], sp["C"], sp["lr"], sp["li"], sp["d"])


def _ssm_params(lambda_re, lambda_im, log_dt, b_re, b_im, c_re, c_im, d):
    lam = lax.complex(jnp.minimum(lambda_re.astype(F32), -1e-4), lambda_im.astype(F32))
    dt = jnp.exp(log_dt.astype(F32))[:, None]
    lam_dt = lam * dt
    lam_bar = jnp.exp(lam_dt)
    b = lax.complex(b_re.astype(F32), b_im.astype(F32))
    b_bar = ((lam_bar - 1.0) / lam)[..., None] * b
    eye = jnp.eye(CH_GROUPS, dtype=F32)

    def pack_b(part):
        x = part.reshape(SSM_CHUNKS, CH_GROUPS, SSM_STATE, SSM_GROUP)
        return jnp.einsum("kgpc,gh->kgchp", x, eye).reshape(SSM_CHUNKS, CH_IN, CH_STATES)

    def pack_c(part):
        x = part.reshape(SSM_CHUNKS, CH_GROUPS, SSM_GROUP, SSM_STATE)
        return jnp.einsum("kgcp,gh->kgphc", x, eye).reshape(SSM_CHUNKS, CH_STATES, CH_IN)

    bmat = jnp.concatenate([pack_b(jnp.real(b_bar)), pack_b(jnp.imag(b_bar))], axis=2).astype(BF16)
    cmat = jnp.concatenate([pack_c(c_re.astype(F32)), pack_c(-c_im.astype(F32))], axis=1).astype(BF16)
    pw = jnp.exp(lam_dt * float(SCAN_SEG)).reshape(SSM_CHUNKS, 1, CH_STATES)
    lb = lam_bar.reshape(SSM_CHUNKS, 1, CH_STATES)
    return dict(B=bmat, C=cmat, lr=jnp.real(lb), li=jnp.imag(lb), pr=jnp.real(pw), pi=jnp.imag(pw),
                d=d.astype(F32).reshape(SSM_CHUNKS, 1, CH_IN))


def _pack_state(re, im):
    n = re.shape[0]
    st = jnp.stack([re.reshape(n, SSM_CHUNKS, CH_STATES), im.reshape(n, SSM_CHUNKS, CH_STATES)], axis=2)
    return st.reshape(n, SSM_CHUNKS * 2 * CH_STATES).astype(F32)


def _unpack_state(h, n):
    st = h.reshape(n, SSM_CHUNKS, 2, CH_GROUPS, SSM_STATE)
    return (st[:, :, 0].reshape(n, SSM_GROUPS, SSM_STATE), st[:, :, 1].reshape(n, SSM_GROUPS, SSM_STATE))


def _glu_body(y_ref, w_ref, g_ref, o_ref):
    a = jax.nn.gelu(y_ref[...])
    z = a * jax.nn.sigmoid(jnp.dot(a.astype(BF16), w_ref[...], preferred_element_type=F32))
    o_ref[...] = _rms(z, g_ref[...]).astype(BF16)


def _glu(y, w_bf, g, tm):
    rows = y.shape[0]
    spec = pl.BlockSpec((tm, SSM_WIDTH), lambda i: (i, 0))
    return pl.pallas_call(
        _glu_body,
        grid=(rows // tm,),
        in_specs=[spec, _full((SSM_WIDTH, SSM_WIDTH)), _full((1, SSM_WIDTH))],
        out_specs=spec,
        out_shape=jax.ShapeDtypeStruct((rows, SSM_WIDTH), BF16),
        compiler_params=_cparams(("arbitrary",)),
        name="glu_rms",
    )(y, w_bf, g)


def _layer_norm(y, g, b):
    mu = jnp.mean(y, axis=-1, keepdims=True)
    yc = y - mu
    var = jnp.mean(jnp.square(yc), axis=-1, keepdims=True)
    return yc * lax.rsqrt(var + LN_EPS) * g + b


OUTPROJ_SUB = 256


def _outproj_body(*refs, aliased, nblk):
    if aliased:
        refs = refs[1:]
    a_ref, s_ref, x_ref, w_ref, g_ref, b_ref, wrh_ref, wrl_ref, br_ref, o_ref, lg_ref = refs
    i = pl.program_id(0)

    @pl.when(i < nblk)
    def _():
        tm = a_ref.shape[0]
        sub = min(tm, OUTPROJ_SUB)
        for r0 in range(0, tm, sub):
            rows = slice(r0, r0 + sub)
            acc = jnp.dot(a_ref[rows, :], w_ref[0:ATTN_WIDTH, :], preferred_element_type=F32)
            acc = acc + jnp.dot(s_ref[rows, :], w_ref[ATTN_WIDTH:, :], preferred_element_type=F32)
            x1 = _layer_norm(DEEPNORM_ALPHA * x_ref[rows, :] + acc, g_ref[...], b_ref[...])
            o_ref[rows, :] = x1
            xh = x1.astype(BF16)
            xl = (x1 - xh.astype(F32)).astype(BF16)
            lg = jnp.dot(xh, wrh_ref[...], preferred_element_type=F32)
            lg = lg + (jnp.dot(xl, wrh_ref[...], preferred_element_type=F32)
                       + jnp.dot(xh, wrl_ref[...], preferred_element_type=F32))
            lg_ref[rows, :] = lg + br_ref[...]

    @pl.when(i >= nblk)
    def _():
        o_ref[...] = jnp.zeros(o_ref.shape, F32)


def _outproj(an, sn, x, w_bf, g, b, wr, br, tm, total_rows, row_block0, x1_buf=None):
    rows = an.shape[0]
    aliased = x1_buf is not None
    nblk = rows // tm
    tail = 0 if aliased or rows == total_rows else 1
    blk = lambda i: jnp.minimum(i, nblk - 1)
    half = lambda: pl.BlockSpec((tm, ATTN_WIDTH), lambda i: (blk(i), 0))
    in_specs = [half(), half(), pl.BlockSpec((tm, D_MODEL), lambda i: (blk(i), 0)),
                _full((D_MODEL, D_MODEL)), _full((1, D_MODEL)), _full((1, D_MODEL)),
                _full((D_MODEL, LANES)), _full((D_MODEL, LANES)), _full((1, LANES))]
    wr_hi = wr.astype(BF16)
    wr_lo = (wr - wr_hi.astype(F32)).astype(BF16)
    args = [an, sn, x, w_bf, g, b, wr_hi, wr_lo, br]
    if aliased:
        in_specs = [pl.BlockSpec(memory_space=pl.ANY)] + in_specs
        args = [x1_buf] + args
    return pl.pallas_call(
        functools.partial(_outproj_body, aliased=aliased, nblk=nblk),
        grid=(nblk + tail,),
        in_specs=in_specs,
        out_specs=[pl.BlockSpec((tm, D_MODEL), lambda i: (i + row_block0, 0)),
                   pl.BlockSpec((tm, LANES), lambda i: (blk(i), 0))],
        out_shape=[jax.ShapeDtypeStruct((total_rows, D_MODEL), F32), jax.ShapeDtypeStruct((rows, LANES), F32)],
        input_output_aliases={0: 0} if aliased else {},
        compiler_params=_cparams(("arbitrary",)),
        name="outproj_ln1",
    )(*args)


ROUTE_T = 640
R_E0, R_E1, R_W0, R_W1, R_P0, R_P1 = range(6)


def _route_select(x):
    lane = lax.broadcasted_iota(jnp.int32, x.shape, 1)
    big = jnp.int32(1 << 20)
    neg = -jnp.inf
    gmask = lane < N_EXPERT_GROUPS
    gm = jnp.max(jnp.where(gmask, x, neg), axis=-1, keepdims=True)
    grp = jnp.min(jnp.where(gmask & (x == gm), lane, big), axis=-1, keepdims=True)
    p_grp = 1.0 / jnp.sum(jnp.where(gmask, jnp.exp(x - gm), 0.0), axis=-1, keepdims=True)
    lo = N_EXPERT_GROUPS + grp * EXPERTS_PER_GROUP
    emask = (lane >= lo) & (lane < lo + EXPERTS_PER_GROUP)
    v1 = jnp.max(jnp.where(emask, x, neg), axis=-1, keepdims=True)
    i1 = jnp.min(jnp.where(emask & (x == v1), lane, big), axis=-1, keepdims=True)
    emask2 = emask & (lane != i1)
    v2 = jnp.max(jnp.where(emask2, x, neg), axis=-1, keepdims=True)
    i2 = jnp.min(jnp.where(emask2 & (x == v2), lane, big), axis=-1, keepdims=True)
    e2 = jnp.exp(v2 - v1)
    w1 = p_grp / (1.0 + e2)
    w2 = p_grp * e2 / (1.0 + e2)
    return (i1 - N_EXPERT_GROUPS).astype(F32), (i2 - N_EXPERT_GROUPS).astype(F32), w1, w2


def _route_plan_body(lg_ref, ltri_ref, utri_ref, o_ref, meta_ref, rec_ref, run_ref):
    ph = pl.program_id(0)
    j = pl.program_id(1)
    tm = lg_ref.shape[0]
    lane = lax.broadcasted_iota(jnp.int32, (tm, LANES), 1)
    lanef = lane.astype(F32)
    r0 = pl.multiple_of(j * tm, tm)

    @pl.when((ph == 0) & (j == 0))
    def _():
        run_ref[...] = jnp.zeros(run_ref.shape, F32)

    @pl.when(ph == 0)
    def _():
        e0, e1, w0, w1 = _route_select(lg_ref[...])
        oh0 = (lanef == e0).astype(F32)
        oh1 = (lanef == e1).astype(F32)
        oh2 = oh0 + oh1
        before = jnp.dot(ltri_ref[...], oh2.astype(BF16), preferred_element_type=F32) + run_ref[0:1, :]
        rk0 = jnp.sum(before * oh0, axis=-1, keepdims=True)
        rk1 = jnp.sum(before * oh1, axis=-1, keepdims=True)
        run_ref[...] = run_ref[...] + jnp.sum(oh2, axis=0, keepdims=True)
        rec = jnp.where(lane == R_E0, e0, 0.0)
        rec = jnp.where(lane == R_E1, e1, rec)
        rec = jnp.where(lane == R_W0, w0, rec)
        rec = jnp.where(lane == R_W1, w1, rec)
        rec = jnp.where(lane == R_P0, rk0, rec)
        rec = jnp.where(lane == R_P1, rk1, rec)
        rec_ref[pl.ds(r0, tm), :] = rec

    @pl.when(ph == 1)
    def _():
        counts = run_ref[...]
        nblk = jnp.floor((counts + (MOE_BLOCK - 1)) * (1.0 / MOE_BLOCK))
        end_blk = jnp.dot(nblk.astype(BF16), utri_ref[...], preferred_element_type=F32)
        start_row = (end_blk - nblk) * MOE_BLOCK
        rec = rec_ref[pl.ds(r0, tm), :]
        oh0 = (lanef == rec[:, R_E0:R_E0 + 1]).astype(F32)
        oh1 = (lanef == rec[:, R_E1:R_E1 + 1]).astype(F32)
        p0 = jnp.sum(oh0 * start_row[0:1, :], axis=-1, keepdims=True) + rec[:, R_P0:R_P0 + 1]
        p1 = jnp.sum(oh1 * start_row[0:1, :], axis=-1, keepdims=True) + rec[:, R_P1:R_P1 + 1]
        out = jnp.where(lane == R_P0, p0, rec)
        out = jnp.where(lane == R_P1, p1, out)
        o_ref[...] = out

        @pl.when(j == 0)
        def _():
            meta_ref[...] = end_blk


def _route_plan(logits):
    rows = logits.shape[0]
    tm = ROUTE_T
    ltri = jnp.asarray(np.tril(np.ones((tm, tm), np.float32), -1), BF16)
    utri = jnp.asarray(np.triu(np.ones((LANES, LANES), np.float32)), BF16)
    mrows = SUBLANES
    return pl.pallas_call(
        _route_plan_body,
        grid=(2, rows // tm),
        in_specs=[pl.BlockSpec((tm, LANES), lambda p, j: (j * (1 - p), 0)),
                  pl.BlockSpec((tm, tm), lambda p, j: (0, 0)),
                  pl.BlockSpec((LANES, LANES), lambda p, j: (0, 0))],
        out_specs=[pl.BlockSpec((tm, LANES), lambda p, j: (j * p, 0)),
                   pl.BlockSpec((mrows, LANES), lambda p, j: (0, 0))],
        out_shape=[jax.ShapeDtypeStruct((rows, LANES), F32), jax.ShapeDtypeStruct((mrows, LANES), F32)],
        scratch_shapes=[pltpu.VMEM((rows, LANES), F32), pltpu.VMEM((SUBLANES, LANES), F32)],
        compiler_params=_cparams(("arbitrary", "arbitrary")),
        name="route_plan",
    )(logits, ltri, utri)


def _rowmap_body(pos_ref, o_ref, *, ntok, nrows):
    unroll = 8

    def zero(c, _):
        for u in range(unroll):
            o_ref[c * unroll + u] = 0
        return 0
    lax.fori_loop(0, nrows // unroll, zero, 0)

    def fill(c, _):
        for u in range(unroll):
            t = c * unroll + u
            o_ref[pos_ref[2 * t]] = t
            o_ref[pos_ref[2 * t + 1]] = t
        return 0
    lax.fori_loop(0, ntok // unroll, fill, 0)


def _rowmap(pos, nrows):
    ntok = pos.shape[0] // 2
    return pl.pallas_call(
        functools.partial(_rowmap_body, ntok=ntok, nrows=nrows),
        in_specs=[pl.BlockSpec(memory_space=pltpu.SMEM)],
        out_specs=pl.BlockSpec(memory_space=pltpu.SMEM),
        out_shape=jax.ShapeDtypeStruct((nrows,), jnp.int32),
        name="moe_rowmap",
    )(pos)


def _moe_body(bend_ref, row_tok_ref, x_hbm, wg_ref, wu_ref, wd_ref, ys_hbm,
              xb0, xb1, yb0, yb1, gsem, osem, wg_bf, wu_bf, wd_bf, *, nblocks):
    e = pl.program_id(0)
    nused = bend_ref[N_EXPERTS - 1]
    b1 = bend_ref[e]
    b0 = jnp.where(e == 0, 0, bend_ref[jnp.maximum(e - 1, 0)])
    xbufs = (xb0, xb1)
    ybufs = (yb0, yb1)

    def gather_start(blk, s):
        base = blk * MOE_BLOCK
        for r in range(MOE_BLOCK):
            tok = row_tok_ref[base + r]
            pltpu.make_async_copy(x_hbm.at[pl.ds(tok, 1), :], xbufs[s].at[pl.ds(r, 1), :],
                                  gsem.at[s]).start(priority=GATHER_PRIORITY)

    def gather_wait(s):
        pltpu.make_async_copy(x_hbm.at[pl.ds(0, MOE_BLOCK), :], xbufs[s], gsem.at[s]).wait()

    def out_copy(blk, s):
        row0 = pl.multiple_of(blk * MOE_BLOCK, MOE_BLOCK)
        return pltpu.make_async_copy(ybufs[s], ys_hbm.at[pl.ds(row0, MOE_BLOCK), :], osem.at[s])

    @pl.when(e == 0)
    def _():
        gather_start(0, 0)

    @pl.when(b1 > b0)
    def _():
        wg_bf[...] = wg_ref[...].astype(BF16)
        wu_bf[...] = wu_ref[...].astype(BF16)
        wd_bf[...] = wd_ref[...].astype(BF16)

        def do_block(b, s):
            gather_wait(s)

            @pl.when(b >= 2)
            def _():
                out_copy(b - 2, s).wait()

            gather_start(jnp.minimum(b + 1, nused - 1), 1 - s)
            x = xbufs[s][...].astype(BF16)
            hg = jnp.dot(x, wg_bf[...], preferred_element_type=F32)
            hu = jnp.dot(x, wu_bf[...], preferred_element_type=F32)
            h = (jax.nn.silu(hg) * hu).astype(BF16)
            ybufs[s][...] = jnp.dot(h, wd_bf[...], preferred_element_type=F32)
            out_copy(b, s).start()

        def body(b, _):
            for s in range(2):
                @pl.when(b % 2 == s)
                def _():
                    do_block(b, s)
            return 0

        lax.fori_loop(b0, b1, body, 0)

    @pl.when(e == N_EXPERTS - 1)
    def _():
        last = nused - 1
        for s in range(2):
            @pl.when(last % 2 == s)
            def _():
                gather_wait(1 - s)
        out_copy(0, 0).wait()
        out_copy(0, 1).wait()
        yb0[...] = jnp.zeros(yb0.shape, F32)

        def zstart(b, _):
            out_copy(b, 0).start()
            return 0

        def zwait(b, _):
            out_copy(b, 0).wait()
            return 0

        lax.fori_loop(nused, nblocks, zstart, 0)
        lax.fori_loop(nused, nblocks, zwait, 0)


def _moe(bend, row_tok, x1, w_gate, w_up, w_down, layer):
    nblocks = row_tok.shape[0] // MOE_BLOCK
    assert 2 * x1.shape[0] > MOE_BLOCK
    wspec = lambda a, b: pl.BlockSpec((None, None, a, b), lambda e, be, rt: (layer, e, 0, 0))
    grid_spec = pltpu.PrefetchScalarGridSpec(
        num_scalar_prefetch=2,
        grid=(N_EXPERTS,),
        in_specs=[pl.BlockSpec(memory_space=pl.ANY), wspec(D_MODEL, EXPERT_FF), wspec(D_MODEL, EXPERT_FF),
                  wspec(EXPERT_FF, D_MODEL)],
        out_specs=pl.BlockSpec(memory_space=pl.ANY),
        scratch_shapes=[pltpu.VMEM((MOE_BLOCK, D_MODEL), F32), pltpu.VMEM((MOE_BLOCK, D_MODEL), F32),
                        pltpu.VMEM((MOE_BLOCK, D_MODEL), F32), pltpu.VMEM((MOE_BLOCK, D_MODEL), F32),
                        pltpu.SemaphoreType.DMA((2,)), pltpu.SemaphoreType.DMA((2,)),
                        pltpu.VMEM((D_MODEL, EXPERT_FF), BF16),
                        pltpu.VMEM((D_MODEL, EXPERT_FF), BF16),
                        pltpu.VMEM((EXPERT_FF, D_MODEL), BF16)],
    )
    return pl.pallas_call(
        functools.partial(_moe_body, nblocks=nblocks),
        grid_spec=grid_spec,
        out_shape=jax.ShapeDtypeStruct((nblocks * MOE_BLOCK, D_MODEL), F32),
        compiler_params=_cparams(("arbitrary",), bounds_checks=False),
        name="moe_experts",
    )(bend, row_tok, x1, w_gate, w_up, w_down)


COMB_T = 128


def _combine_body(pos_ref, ys_hbm, x_ref, rt_ref, g_ref, b_ref, op_ref, os_ref,
                  ga0, gb0, ga1, gb1, sem, *, n_prompt_blocks):
    i = pl.program_id(0)
    nsteps = pl.num_programs(0)
    bufs = ((ga0, gb0), (ga1, gb1))

    def gather_start(step, s):
        base = step * (2 * COMB_T)
        for r in range(COMB_T):
            for k in range(2):
                p = pos_ref[base + 2 * r + k]
                pltpu.make_async_copy(ys_hbm.at[pl.ds(p, 1), :], bufs[s][k].at[pl.ds(r, 1), :],
                                      sem.at[s]).start(priority=k)

    def gather_wait(s):
        for k in range(2):
            pltpu.make_async_copy(ys_hbm.at[pl.ds(0, COMB_T), :], bufs[s][k], sem.at[s]).wait()

    @pl.when(i == 0)
    def _():
        gather_start(0, 0)

    def do_tile(s):
        gather_wait(s)
        gather_start(jnp.minimum(i + 1, nsteps - 1), 1 - s)
        rt = rt_ref[...]
        ff = rt[:, R_W0:R_W0 + 1] * bufs[s][0][...] + rt[:, R_W1:R_W1 + 1] * bufs[s][1][...]
        out = _layer_norm(DEEPNORM_ALPHA * x_ref[...] + ff, g_ref[...], b_ref[...])

        @pl.when(i < n_prompt_blocks)
        def _():
            op_ref[...] = out

        @pl.when(i >= n_prompt_blocks)
        def _():
            os_ref[...] = out

        @pl.when(i == nsteps - 1)
        def _():
            gather_wait(1 - s)

    for s in range(2):
        @pl.when(i % 2 == s)
        def _():
            do_tile(s)


def _combine(pos, ys, x1, route, g, b, n_prompt_rows):
    rows = x1.shape[0]
    npb = n_prompt_rows // COMB_T
    grid_spec = pltpu.PrefetchScalarGridSpec(
        num_scalar_prefetch=1,
        grid=(rows // COMB_T,),
        in_specs=[pl.BlockSpec(memory_space=pl.ANY),
                  pl.BlockSpec((COMB_T, D_MODEL), lambda i, p: (i, 0)),
                  pl.BlockSpec((COMB_T, LANES), lambda i, p: (i, 0)),
                  pl.BlockSpec((1, D_MODEL), lambda i, p: (0, 0)),
                  pl.BlockSpec((1, D_MODEL), lambda i, p: (0, 0))],
        out_specs=[pl.BlockSpec((COMB_T, D_MODEL), lambda i, p: (jnp.minimum(i, npb - 1), 0)),
                   pl.BlockSpec((COMB_T, D_MODEL), lambda i, p: (jnp.maximum(i - npb, 0), 0))],
        scratch_shapes=[pltpu.VMEM((COMB_T, D_MODEL), F32) for _ in range(4)] + [pltpu.SemaphoreType.DMA((2,))],
    )
    return pl.pallas_call(
        functools.partial(_combine_body, n_prompt_blocks=npb),
        grid_spec=grid_spec,
        out_shape=[jax.ShapeDtypeStruct((n_prompt_rows, D_MODEL), F32),
                   jax.ShapeDtypeStruct((rows - n_prompt_rows, D_MODEL), F32)],
        compiler_params=_cparams(("arbitrary",), bounds_checks=False),
        name="combine_ln2",
    )(pos, ys, x1, route, g, b)


def kernel(x_prompt, x_sample, cache_k, cache_v, state_ssm_re, state_ssm_im, w_in, ssm_lambda_re, ssm_lambda_im, ssm_log_dt, ssm_b_re, ssm_b_im, ssm_c_re, ssm_c_im, ssm_d, ssm_w_glu, attn_sinks, attn_norm_g, ssm_norm_g, w_out, ln1_g, ln1_b, router_group_w, router_group_b, router_expert_w, router_expert_b, expert_w_gate, expert_w_up, expert_w_down, ln2_g, ln2_b):
    nbatch, seq, _ = x_prompt.shape
    nseq, dec_seq, _ = x_sample.shape
    wbuf = cache_k.shape[2]
    rows_p = nbatch * seq
    rows_s = nseq * dec_seq
    rows_all = rows_p + rows_s
    tm_p = 512

    xp = x_prompt.reshape(rows_p, D_MODEL)
    xs = x_sample.reshape(rows_s, D_MODEL)
    tab_p = _rope_tables(np.arange(seq))
    tab_s = _rope_tables(PAST_LEN + (np.arange(rows_s) % dec_seq))
    ck_all = cache_k.astype(F32).reshape(DEPTH, nseq, wbuf, KV_WIDTH)
    cv_all = cache_v.astype(F32).reshape(DEPTH, nseq, wbuf, KV_WIDTH)

    outs = {k: [] for k in ("kp", "vp", "hrp", "hip", "ks", "vs", "hrs", "his")}
    for l in range(DEPTH):
        w_in_bf = w_in[l].astype(BF16)
        w_glu_bf = ssm_w_glu[l].astype(BF16)
        w_out_bf = w_out[l].astype(BF16)
        sp = _ssm_params(ssm_lambda_re[l], ssm_lambda_im[l], ssm_log_dt[l], ssm_b_re[l], ssm_b_im[l],
                         ssm_c_re[l], ssm_c_im[l], ssm_d[l])
        sinks = attn_sinks[l].astype(F32)
        attn_g = attn_norm_g[l].astype(F32).reshape(1, ATTN_WIDTH)
        ssm_g = ssm_norm_g[l].astype(F32).reshape(1, SSM_WIDTH)
        g1 = ln1_g[l].astype(F32).reshape(1, D_MODEL)
        b1 = ln1_b[l].astype(F32).reshape(1, D_MODEL)
        g2 = ln2_g[l].astype(F32).reshape(1, D_MODEL)
        b2 = ln2_b[l].astype(F32).reshape(1, D_MODEL)
        n_rt = N_EXPERT_GROUPS + N_EXPERTS
        wr = jnp.concatenate([router_group_w[l].astype(F32)]
                             + [router_expert_w[l, g].astype(F32) for g in range(N_EXPERT_GROUPS)]
                             + [jnp.zeros((D_MODEL, LANES - n_rt), F32)], axis=1)
        br = jnp.concatenate([router_group_b[l].astype(F32), router_expert_b[l].astype(F32).reshape(-1),
                              jnp.zeros((LANES - n_rt,), F32)]).reshape(1, LANES)

        q_p, k_p, v_p, u_p = _inproj(xp, w_in_bf, *tab_p, tm=tm_p)
        q_s, k_s, v_s, u_s = _inproj(xs, w_in_bf, *tab_s, tm=rows_s)

        an_p = _attn_prompt(sinks, q_p, k_p, v_p, attn_g, seq)
        an_s, nk_s, nv_s = _attn_sample(sinks, q_s, k_s, v_s, ck_all, cv_all, attn_g, dec_seq, l)

        y_p, hl_p = _ssm_prompt(u_p, sp, nbatch, seq)
        y_s, hl_s = _ssm_sample(u_s, _pack_state(state_ssm_re[l], state_ssm_im[l]), sp, nseq, dec_seq)
        sn_p = _glu(y_p, w_glu_bf, ssm_g, tm_p)
        sn_s = _glu(y_s, w_glu_bf, ssm_g, rows_s)

        x1, lg_p = _outproj(an_p, sn_p, xp, w_out_bf, g1, b1, wr, br, tm_p, rows_all, 0)
        x1, lg_s = _outproj(an_s, sn_s, xs, w_out_bf, g1, b1, wr, br, rows_s, rows_all, rows_p // rows_s, x1_buf=x1)
        nblocks = 2 * rows_all // MOE_BLOCK + N_EXPERTS
        route, meta = _route_plan(jnp.concatenate([lg_p, lg_s], axis=0))
        pos = route[:, R_P0:R_P1 + 1].astype(jnp.int32).reshape(2 * rows_all)
        bend = meta[0, :N_EXPERTS].astype(jnp.int32)
        row_tok = _rowmap(pos, nblocks * MOE_BLOCK)

        ys = _moe(bend, row_tok, x1, expert_w_gate, expert_w_up, expert_w_down, l)
        xp, xs = _combine(pos, ys, x1, route, g2, b2, rows_p)

        wp = min(WINDOW, seq)
        outs["kp"].append(k_p.reshape(nbatch, seq, N_KV_HEADS, HEAD_DIM)[:, seq - wp:])
        outs["vp"].append(v_p.reshape(nbatch, seq, N_KV_HEADS, HEAD_DIM)[:, seq - wp:])
        hp = hl_p.reshape(nbatch, SSM_CHUNKS * 2 * CH_STATES)
        hr, hi = _unpack_state(hp, nbatch)
        outs["hrp"].append(hr)
        outs["hip"].append(hi)
        outs["ks"].append(nk_s.reshape(nseq, wbuf, N_KV_HEADS, HEAD_DIM))
        outs["vs"].append(nv_s.reshape(nseq, wbuf, N_KV_HEADS, HEAD_DIM))
        hr, hi = _unpack_state(hl_s, nseq)
        outs["hrs"].append(hr)
        outs["his"].append(hi)

    st = {k: jnp.stack(v) for k, v in outs.items()}
    return (xp.reshape(nbatch, seq, D_MODEL), xs.reshape(nseq, dec_seq, D_MODEL),
            st["kp"], st["vp"], st["hrp"], st["hip"], st["ks"], st["vs"], st["hrs"], st["his"])
```

```python
import functools
import math

import numpy as np
import jax
import jax.numpy as jnp
from jax import lax
from jax.experimental import pallas as pl
from jax.experimental.pallas import tpu as pltpu

F32 = jnp.float32
BF16 = jnp.bfloat16

D_MODEL = 2048
DEPTH = 2
PAST_LEN = 16384
ATTN_WIDTH = 1024
SSM_WIDTH = 1024
HEAD_DIM = 64
N_HEADS = 16
N_KV_HEADS = 4
Q_PER_KV = 4
KV_WIDTH = 256
WINDOW = 128
ROPE_THETA = 10000.0
SSM_GROUP = 16
SSM_GROUPS = 64
SSM_STATE = 64
N_EXPERT_GROUPS = 4
EXPERTS_PER_GROUP = 8
N_EXPERTS = 32
EXPERT_FF = 512
MOE_BLOCK = 128
DEEPNORM_ALPHA = (2.0 * DEPTH) ** 0.25
LN_EPS = 1e-5
RMS_EPS = 1e-6

LANES = 128
SUBLANES = 8
VMEM_LIMIT = 56 * 1024 * 1024

SSM_CHUNKS = 4
CH_GROUPS = SSM_GROUPS // SSM_CHUNKS
CH_STATES = CH_GROUPS * SSM_STATE
CH_IN = CH_GROUPS * SSM_GROUP
SCAN_T = 512
SCAN_SEG = SCAN_T // SUBLANES
SCAN_UNROLL = 4


def _cparams(sem=None, vmem=VMEM_LIMIT, bounds_checks=True):
    return pltpu.CompilerParams(dimension_semantics=sem, vmem_limit_bytes=vmem,
                                disable_bounds_checks=not bounds_checks)


def _full(shape):
    n = len(shape)
    return pl.BlockSpec(shape, lambda *a: (0,) * n)


GATHER_PRIORITY = 1


def _inproj_body(x_ref, w_ref, cos_ref, sa_ref, sb_ref, q_ref, k_ref, v_ref, u_ref):
    x = x_ref[...].astype(BF16)
    cos = cos_ref[...]
    sa = sa_ref[...]
    sb = sb_ref[...]

    def rope(c):
        return c * cos + pltpu.roll(c, LANES - 32, 1) * sa + pltpu.roll(c, 32, 1) * sb

    pq = jnp.dot(x, w_ref[:, 0:ATTN_WIDTH], preferred_element_type=F32)
    for j in range(ATTN_WIDTH // LANES):
        q_ref[:, j * LANES:(j + 1) * LANES] = rope(pq[:, j * LANES:(j + 1) * LANES]).astype(BF16)
    pk = jnp.dot(x, w_ref[:, ATTN_WIDTH:ATTN_WIDTH + KV_WIDTH], preferred_element_type=F32)
    for j in range(KV_WIDTH // LANES):
        k_ref[:, j * LANES:(j + 1) * LANES] = rope(pk[:, j * LANES:(j + 1) * LANES])
    v_ref[...] = jnp.dot(x, w_ref[:, ATTN_WIDTH + KV_WIDTH:ATTN_WIDTH + 2 * KV_WIDTH],
                         preferred_element_type=F32)
    u_ref[...] = jnp.dot(x, w_ref[:, ATTN_WIDTH + 2 * KV_WIDTH:], preferred_element_type=F32)


def _inproj(x, w_bf, cos_t, sa_t, sb_t, tm):
    rows = x.shape[0]
    tab_blocks = cos_t.shape[0] // tm
    in_w = w_bf.shape[1]
    row_spec = lambda w: pl.BlockSpec((tm, w), lambda i: (i, 0))
    tab_spec = pl.BlockSpec((tm, LANES), lambda i: (i % tab_blocks, 0))
    return pl.pallas_call(
        _inproj_body,
        grid=(rows // tm,),
        in_specs=[row_spec(D_MODEL), _full((D_MODEL, in_w)), tab_spec, tab_spec, tab_spec],
        out_specs=[row_spec(ATTN_WIDTH), row_spec(KV_WIDTH), row_spec(KV_WIDTH), row_spec(SSM_WIDTH)],
        out_shape=[jax.ShapeDtypeStruct((rows, ATTN_WIDTH), BF16),
                   jax.ShapeDtypeStruct((rows, KV_WIDTH), F32),
                   jax.ShapeDtypeStruct((rows, KV_WIDTH), F32),
                   jax.ShapeDtypeStruct((rows, SSM_WIDTH), F32)],
        compiler_params=_cparams(("arbitrary",)),
        name="inproj_rope",
    )(x, w_bf, cos_t, sa_t, sb_t)


def _rope_tables(positions):
    half = HEAD_DIM // 2
    inv_freq = ROPE_THETA ** (-np.arange(half, dtype=np.float64) / half)
    ang = positions.astype(np.float64)[:, None] * inv_freq[None, :]
    lane = np.arange(LANES)
    cos = np.cos(ang)[:, lane % half]
    sin = np.sin(ang)[:, lane % half]
    first = (lane % HEAD_DIM) < half
    sa = np.where(first[None, :], -sin, 0.0)
    sb = np.where(first[None, :], 0.0, sin)
    return (jnp.asarray(cos, F32), jnp.asarray(sa, F32), jnp.asarray(sb, F32))


def _rms(x, g):
    return x * lax.rsqrt(jnp.mean(jnp.square(x), axis=-1, keepdims=True) + RMS_EPS) * g


def _attn_prompt_body(sink_ref, q_ref, kp_ref, kc_ref, vp_ref, vc_ref, g_ref, o_ref, acc_ref, *, blocks_per_seq):
    i = pl.program_id(0)
    has_prev = (i % blocks_per_seq) > 0
    kk = jnp.concatenate([kp_ref[...], kc_ref[...]], axis=0).astype(BF16)
    vv = jnp.concatenate([vp_ref[...], vc_ref[...]], axis=0).astype(BF16)
    row = lax.broadcasted_iota(jnp.int32, (WINDOW, 2 * WINDOW), 0)
    col = lax.broadcasted_iota(jnp.int32, (WINDOW, 2 * WINDOW), 1)
    visible = (col > row) & (col <= row + WINDOW) & ((col >= WINDOW) | has_prev)
    q = q_ref[...]
    for h in range(N_HEADS):
        g = h // Q_PER_KV
        qh = q[:, h * HEAD_DIM:(h + 1) * HEAD_DIM]
        kh = kk[:, g * HEAD_DIM:(g + 1) * HEAD_DIM]
        vh = vv[:, g * HEAD_DIM:(g + 1) * HEAD_DIM]
        s = lax.dot_general(qh, kh, (((1,), (1,)), ((), ())), preferred_element_type=F32) * (HEAD_DIM ** -0.5)
        s = jnp.where(visible, s, -jnp.inf)
        sink = sink_ref[h]
        m = jnp.maximum(jnp.max(s, axis=-1, keepdims=True), sink)
        p = jnp.exp(s - m)
        denom = jnp.sum(p, axis=-1, keepdims=True) + jnp.exp(sink - m)
        oh = jnp.dot(p.astype(BF16), vh, preferred_element_type=F32) / denom
        acc_ref[:, h * HEAD_DIM:(h + 1) * HEAD_DIM] = oh
    o_ref[...] = _rms(acc_ref[...], g_ref[...]).astype(BF16)


def _attn_prompt(sinks, q, k, v, g, seq):
    rows = q.shape[0]
    nblk = rows // WINDOW
    bps = seq // WINDOW
    cur = lambda w: pl.BlockSpec((WINDOW, w), lambda i: (i, 0))
    prev = lambda w: pl.BlockSpec((WINDOW, w), lambda i: (jnp.maximum(i - 1, 0), 0))
    return pl.pallas_call(
        functools.partial(_attn_prompt_body, blocks_per_seq=bps),
        grid=(nblk,),
        in_specs=[pl.BlockSpec(memory_space=pltpu.SMEM), cur(ATTN_WIDTH), prev(KV_WIDTH), cur(KV_WIDTH),
                  prev(KV_WIDTH), cur(KV_WIDTH), _full((1, ATTN_WIDTH))],
        out_specs=cur(ATTN_WIDTH),
        out_shape=jax.ShapeDtypeStruct((rows, ATTN_WIDTH), BF16),
        scratch_shapes=[pltpu.VMEM((WINDOW, ATTN_WIDTH), F32)],
        compiler_params=_cparams(("arbitrary",)),
        name="attn_prompt",
    )(sinks, q, k, k, v, v, g)


SEQ_PER_STEP = 8


def _attn_sample_body(sink_ref, q_ref, kn_ref, vn_ref, ck_ref, cv_ref, g_ref, o_ref, nk_ref, nv_ref, acc_ref,
                      *, dec_seq):
    nq = SEQ_PER_STEP * dec_seq
    wb = ck_ref.shape[1]
    nkc = SEQ_PER_STEP * wb
    ck = ck_ref[...].reshape(nkc, KV_WIDTH)
    cv = cv_ref[...].reshape(nkc, KV_WIDTH)
    kn = kn_ref[...]
    vn = vn_ref[...]
    ckb = ck.astype(BF16)
    cvb = cv.astype(BF16)
    knb = kn.astype(BF16)
    vnb = vn.astype(BF16)
    q = q_ref[...]

    rows = Q_PER_KV * nq
    r = lax.broadcasted_iota(jnp.int32, (rows, nkc), 0) % nq
    c = lax.broadcasted_iota(jnp.int32, (rows, nkc), 1)
    vis_c = ((r // dec_seq) == (c // wb)) & ((c % wb) >= (r % dec_seq) + 1 + (wb - WINDOW))
    rn = lax.broadcasted_iota(jnp.int32, (rows, nq), 0) % nq
    cn = lax.broadcasted_iota(jnp.int32, (rows, nq), 1)
    vis_n = ((rn // dec_seq) == (cn // dec_seq)) & ((cn % dec_seq) <= (rn % dec_seq))
    hrow = lax.broadcasted_iota(jnp.int32, (rows, 1), 0) // nq

    dn = (((1,), (1,)), ((), ()))
    for g in range(N_KV_HEADS):
        qs = jnp.concatenate([q[:, (g * Q_PER_KV + j) * HEAD_DIM:(g * Q_PER_KV + j + 1) * HEAD_DIM]
                              for j in range(Q_PER_KV)], axis=0)
        sl = slice(g * HEAD_DIM, (g + 1) * HEAD_DIM)
        sc = lax.dot_general(qs, ckb[:, sl], dn, preferred_element_type=F32) * (HEAD_DIM ** -0.5)
        sn = lax.dot_general(qs, knb[:, sl], dn, preferred_element_type=F32) * (HEAD_DIM ** -0.5)
        sc = jnp.where(vis_c, sc, -jnp.inf)
        sn = jnp.where(vis_n, sn, -jnp.inf)
        sink = jnp.zeros((rows, 1), F32)
        for j in range(Q_PER_KV):
            sink = jnp.where(hrow == j, sink_ref[g * Q_PER_KV + j], sink)
        m = jnp.maximum(jnp.maximum(jnp.max(sc, axis=-1, keepdims=True), jnp.max(sn, axis=-1, keepdims=True)), sink)
        pc = jnp.exp(sc - m)
        pn = jnp.exp(sn - m)
        denom = jnp.sum(pc, axis=-1, keepdims=True) + jnp.sum(pn, axis=-1, keepdims=True) + jnp.exp(sink - m)
        o = (jnp.dot(pc.astype(BF16), cvb[:, sl], preferred_element_type=F32)
             + jnp.dot(pn.astype(BF16), vnb[:, sl], preferred_element_type=F32)) / denom
        for j in range(Q_PER_KV):
            h = g * Q_PER_KV + j
            acc_ref[:, h * HEAD_DIM:(h + 1) * HEAD_DIM] = o[j * nq:(j + 1) * nq, :]
    o_ref[...] = _rms(acc_ref[...], g_ref[...]).astype(BF16)

    keep = wb - dec_seq
    for s in range(SEQ_PER_STEP):
        nk_ref[s, 0:keep, :] = ck_ref[s, dec_seq:wb, :]
        nv_ref[s, 0:keep, :] = cv_ref[s, dec_seq:wb, :]
        nk_ref[s, keep:wb, :] = kn[s * dec_seq:(s + 1) * dec_seq, :]
        nv_ref[s, keep:wb, :] = vn[s * dec_seq:(s + 1) * dec_seq, :]


def _attn_sample(sinks, q, kn, vn, ck, cv, g, dec_seq, layer):
    rows = q.shape[0]
    _, nseq, wb, _ = ck.shape
    nq = SEQ_PER_STEP * dec_seq
    rowb = lambda w: pl.BlockSpec((nq, w), lambda i: (i, 0))
    cin = pl.BlockSpec((None, SEQ_PER_STEP, wb, KV_WIDTH), lambda i: (layer, i, 0, 0))
    cb = pl.BlockSpec((SEQ_PER_STEP, wb, KV_WIDTH), lambda i: (i, 0, 0))
    return pl.pallas_call(
        functools.partial(_attn_sample_body, dec_seq=dec_seq),
        grid=(nseq // SEQ_PER_STEP,),
        in_specs=[pl.BlockSpec(memory_space=pltpu.SMEM), rowb(ATTN_WIDTH), rowb(KV_WIDTH), rowb(KV_WIDTH),
                  cin, cin, _full((1, ATTN_WIDTH))],
        out_specs=[rowb(ATTN_WIDTH), cb, cb],
        out_shape=[jax.ShapeDtypeStruct((rows, ATTN_WIDTH), BF16),
                   jax.ShapeDtypeStruct(ck.shape[1:], F32), jax.ShapeDtypeStruct(cv.shape[1:], F32)],
        scratch_shapes=[pltpu.VMEM((nq, ATTN_WIDTH), F32)],
        compiler_params=_cparams(("arbitrary",)),
        name="attn_sample",
    )(sinks, q, kn, vn, ck, cv, g)


def _ssm_prompt_body(u_ref, b_ref, c_ref, lr_ref, li_ref, pr_ref, pi_ref, d_ref, y_ref, hl_ref,
                     uperm_ref, bu_ref, hb_ref, yp_ref, carry_ref):
    tc = pl.program_id(1)
    cc = pl.program_id(2)

    @pl.when(cc == 0)
    def _():
        for k in range(SSM_WIDTH // LANES):
            for s in range(SUBLANES):
                uperm_ref[k, pl.ds(s, SCAN_SEG, stride=SUBLANES), :] = (
                    u_ref[s * SCAN_SEG:(s + 1) * SCAN_SEG, k * LANES:(k + 1) * LANES])

    @pl.when(tc == 0)
    def _():
        carry_ref[cc] = jnp.zeros((2, CH_STATES), F32)

    lanes_per_chunk = CH_IN // LANES
    up = jnp.concatenate([uperm_ref[cc * lanes_per_chunk + k] for k in range(lanes_per_chunk)], axis=1)
    bu_ref[...] = jnp.dot(up.astype(BF16), b_ref[cc], preferred_element_type=F32)

    ar = jnp.broadcast_to(lr_ref[cc], (SUBLANES, CH_STATES))
    ai = jnp.broadcast_to(li_ref[cc], (SUBLANES, CH_STATES))

    def advance(h, r0):
        hr, hi = h
        br = bu_ref[pl.ds(r0, SUBLANES), 0:CH_STATES]
        bi = bu_ref[pl.ds(r0, SUBLANES), CH_STATES:2 * CH_STATES]
        return ar * hr + (br - ai * hi), ar * hi + (bi + ai * hr)

    def step(jj, h):
        r0 = pl.multiple_of(jj * SCAN_UNROLL * SUBLANES, SCAN_UNROLL * SUBLANES)
        for k in range(SCAN_UNROLL):
            h = advance(h, r0 + k * SUBLANES)
        return h

    zero = jnp.zeros((SUBLANES, CH_STATES), F32)
    fr, fi = lax.fori_loop(0, SCAN_SEG // SCAN_UNROLL, step, (zero, zero))

    cst = carry_ref[cc]
    c_r = cst[0:1, :]
    c_i = cst[1:2, :]
    lsr = pr_ref[cc]
    lsi = pi_ref[cc]
    rows_r, rows_i = [], []
    for s in range(SUBLANES):
        rows_r.append(c_r)
        rows_i.append(c_i)
        n_r = fr[s:s + 1, :] + (lsr * c_r - lsi * c_i)
        n_i = fi[s:s + 1, :] + (lsr * c_i + lsi * c_r)
        c_r, c_i = n_r, n_i
    carry_ref[cc] = jnp.concatenate([c_r, c_i], axis=0)
    hl_ref[0, cc] = jnp.concatenate([c_r, c_i], axis=0)
    cm_r = jnp.concatenate(rows_r, axis=0)
    cm_i = jnp.concatenate(rows_i, axis=0)

    def emit(jj, h):
        r0 = pl.multiple_of(jj * 2 * SUBLANES, 2 * SUBLANES)
        h1 = advance(h, r0)
        h2 = advance(h1, r0 + SUBLANES)
        hb_ref[pl.ds(r0, 2 * SUBLANES), 0:CH_STATES] = jnp.concatenate([h1[0], h2[0]], axis=0).astype(BF16)
        hb_ref[pl.ds(r0, 2 * SUBLANES), CH_STATES:2 * CH_STATES] = (
            jnp.concatenate([h1[1], h2[1]], axis=0).astype(BF16))
        return h2

    lax.fori_loop(0, SCAN_SEG // 2, emit, (cm_r, cm_i))

    yp = jnp.dot(hb_ref[...], c_ref[cc], preferred_element_type=F32) + d_ref[cc] * up
    for k in range(lanes_per_chunk):
        yp_ref[k] = yp[:, k * LANES:(k + 1) * LANES]
    for s in range(SUBLANES):
        for k in range(lanes_per_chunk):
            y_ref[s * SCAN_SEG:(s + 1) * SCAN_SEG, k * LANES:(k + 1) * LANES] = (
                yp_ref[k, pl.ds(s, SCAN_SEG, stride=SUBLANES), :])


def _ssm_prompt(u, sp, nbatch, seq):
    rows = u.shape[0]
    ntc = seq // SCAN_T
    y, hl = pl.pallas_call(
        _ssm_prompt_body,
        grid=(nbatch, ntc, SSM_CHUNKS),
        in_specs=[pl.BlockSpec((SCAN_T, SSM_WIDTH), lambda b, t, c: (b * ntc + t, 0)),
                  _full(sp["B"].shape), _full(sp["C"].shape), _full(sp["lr"].shape), _full(sp["li"].shape),
                  _full(sp["pr"].shape), _full(sp["pi"].shape), _full(sp["d"].shape)],
        out_specs=[pl.BlockSpec((SCAN_T, CH_IN), lambda b, t, c: (b * ntc + t, c)),
                   pl.BlockSpec((1, SSM_CHUNKS, 2, CH_STATES), lambda b, t, c: (b, 0, 0, 0))],
        out_shape=[jax.ShapeDtypeStruct((rows, SSM_WIDTH), F32),
                   jax.ShapeDtypeStruct((nbatch, SSM_CHUNKS, 2, CH_STATES), F32)],
        scratch_shapes=[pltpu.VMEM((SSM_WIDTH // LANES, SCAN_T, LANES), F32),
                        pltpu.VMEM((SCAN_T, 2 * CH_STATES), F32),
                        pltpu.VMEM((SCAN_T, 2 * CH_STATES), BF16),
                        pltpu.VMEM((CH_IN // LANES, SCAN_T, LANES), F32),
                        pltpu.VMEM((SSM_CHUNKS, 2, CH_STATES), F32)],
        compiler_params=_cparams(("arbitrary", "arbitrary", "arbitrary")),
        name="ssm_prompt",
    )(u, sp["B"], sp["C"], sp["lr"], sp["li"], sp["pr"], sp["pi"], sp["d"])
    return y, hl


def _ssm_sample_body(u_ref, h0_ref, b_ref, c_ref, lr_ref, li_ref, d_ref, y_ref, ho_ref, bu_ref, *, nseq, dec_seq):
    nl = CH_STATES // LANES
    for k in range(SSM_CHUNKS):
        uk = u_ref[:, k * CH_IN:(k + 1) * CH_IN]
        bu = jnp.dot(uk.astype(BF16), b_ref[k], preferred_element_type=F32)
        for c in range(2 * nl):
            bu_ref[c] = bu[:, c * LANES:(c + 1) * LANES]
        base = k * 2 * CH_STATES
        hr = h0_ref[:, base:base + CH_STATES]
        hi = h0_ref[:, base + CH_STATES:base + 2 * CH_STATES]
        ar = lr_ref[k]
        ai = li_ref[k]
        for t in range(dec_seq):
            br = jnp.concatenate([bu_ref[c, pl.ds(t, nseq, stride=dec_seq), :] for c in range(nl)], axis=1)
            bi = jnp.concatenate([bu_ref[nl + c, pl.ds(t, nseq, stride=dec_seq), :] for c in range(nl)], axis=1)
            hr, hi = ar * hr + (br - ai * hi), ar * hi + (bi + ai * hr)
            for c in range(nl):
                bu_ref[c, pl.ds(t, nseq, stride=dec_seq), :] = hr[:, c * LANES:(c + 1) * LANES]
                bu_ref[nl + c, pl.ds(t, nseq, stride=dec_seq), :] = hi[:, c * LANES:(c + 1) * LANES]
        ho_ref[:, base:base + CH_STATES] = hr
        ho_ref[:, base + CH_STATES:base + 2 * CH_STATES] = hi
        hall = jnp.concatenate([bu_ref[c] for c in range(2 * nl)], axis=1)
        y_ref[:, k * CH_IN:(k + 1) * CH_IN] = (
            jnp.dot(hall.astype(BF16), c_ref[k], preferred_element_type=F32) + d_ref[k] * uk)


def _ssm_sample(u, h0, sp, nseq, dec_seq):
    rows = u.shape[0]
    return pl.pallas_call(
        functools.partial(_ssm_sample_body, nseq=nseq, dec_seq=dec_seq),
        grid=(1,),
        in_specs=[_full(u.shape), _full(h0.shape), _full(sp["B"].shape), _full(sp["C"].shape),
                  _full(sp["lr"].shape), _full(sp["li"].shape), _full(sp["d"].shape)],
        out_specs=[_full((rows, SSM_WIDTH)), _full(h0.shape)],
        out_shape=[jax.ShapeDtypeStruct((rows, SSM_WIDTH), F32), jax.ShapeDtypeStruct(h0.shape, F32)],
        scratch_shapes=[pltpu.VMEM((2 * CH_STATES // LANES, rows, LANES), F32)],
        compiler_params=_cparams(("arbitrary",)),
        name="ssm_sample",
    )(u, h0, sp["B"], sp["C"], sp["lr"], sp["li"], sp["d"])


def _ssm_params(lambda_re, lambda_im, log_dt, b_re, b_im, c_re, c_im, d):
    lam = lax.complex(jnp.minimum(lambda_re.astype(F32), -1e-4), lambda_im.astype(F32))
    dt = jnp.exp(log_dt.astype(F32))[:, None]
    lam_dt = lam * dt
    lam_bar = jnp.exp(lam_dt)
    b = lax.complex(b_re.astype(F32), b_im.astype(F32))
    b_bar = ((lam_bar - 1.0) / lam)[..., None] * b
    eye = jnp.eye(CH_GROUPS, dtype=F32)

    def pack_b(part):
        x = part.reshape(SSM_CHUNKS, CH_GROUPS, SSM_STATE, SSM_GROUP)
        return jnp.einsum("kgpc,gh->kgchp", x, eye).reshape(SSM_CHUNKS, CH_IN, CH_STATES)

    def pack_c(part):
        x = part.reshape(SSM_CHUNKS, CH_GROUPS, SSM_GROUP, SSM_STATE)
        return jnp.einsum("kgcp,gh->kgphc", x, eye).reshape(SSM_CHUNKS, CH_STATES, CH_IN)

    bmat = jnp.concatenate([pack_b(jnp.real(b_bar)), pack_b(jnp.imag(b_bar))], axis=2).astype(BF16)
    cmat = jnp.concatenate([pack_c(c_re.astype(F32)), pack_c(-c_im.astype(F32))], axis=1).astype(BF16)
    pw = jnp.exp(lam_dt * float(SCAN_SEG)).reshape(SSM_CHUNKS, 1, CH_STATES)
    lb = lam_bar.reshape(SSM_CHUNKS, 1, CH_STATES)
    return dict(B=bmat, C=cmat, lr=jnp.real(lb), li=jnp.imag(lb), pr=jnp.real(pw), pi=jnp.imag(pw),
                d=d.astype(F32).reshape(SSM_CHUNKS, 1, CH_IN))


def _pack_state(re, im):
    n = re.shape[0]
    st = jnp.stack([re.reshape(n, SSM_CHUNKS, CH_STATES), im.reshape(n, SSM_CHUNKS, CH_STATES)], axis=2)
    return st.reshape(n, SSM_CHUNKS * 2 * CH_STATES).astype(F32)


def _unpack_state(h, n):
    st = h.reshape(n, SSM_CHUNKS, 2, CH_GROUPS, SSM_STATE)
    return (st[:, :, 0].reshape(n, SSM_GROUPS, SSM_STATE), st[:, :, 1].reshape(n, SSM_GROUPS, SSM_STATE))


def _glu_body(y_ref, w_ref, g_ref, o_ref):
    a = jax.nn.gelu(y_ref[...])
    z = a * jax.nn.sigmoid(jnp.dot(a.astype(BF16), w_ref[...], preferred_element_type=F32))
    o_ref[...] = _rms(z, g_ref[...]).astype(BF16)


def _glu(y, w_bf, g, tm):
    rows = y.shape[0]
    spec = pl.BlockSpec((tm, SSM_WIDTH), lambda i: (i, 0))
    return pl.pallas_call(
        _glu_body,
        grid=(rows // tm,),
        in_specs=[spec, _full((SSM_WIDTH, SSM_WIDTH)), _full((1, SSM_WIDTH))],
        out_specs=spec,
        out_shape=jax.ShapeDtypeStruct((rows, SSM_WIDTH), BF16),
        compiler_params=_cparams(("arbitrary",)),
        name="glu_rms",
    )(y, w_bf, g)


def _layer_norm(y, g, b):
    mu = jnp.mean(y, axis=-1, keepdims=True)
    yc = y - mu
    var = jnp.mean(jnp.square(yc), axis=-1, keepdims=True)
    return yc * lax.rsqrt(var + LN_EPS) * g + b


OUTPROJ_SUB = 256


def _outproj_body(*refs, aliased, nblk):
    if aliased:
        refs = refs[1:]
    a_ref, s_ref, x_ref, w_ref, g_ref, b_ref, wrc_ref, br_ref, o_ref, lg_ref = refs
    i = pl.program_id(0)

    @pl.when(i < nblk)
    def _():
        tm = a_ref.shape[0]
        sub = min(tm, OUTPROJ_SUB)
        for r0 in range(0, tm, sub):
            rows = slice(r0, r0 + sub)
            acc = jnp.dot(a_ref[rows, :], w_ref[0:ATTN_WIDTH, :], preferred_element_type=F32)
            acc = acc + jnp.dot(s_ref[rows, :], w_ref[ATTN_WIDTH:, :], preferred_element_type=F32)
            x1 = _layer_norm(DEEPNORM_ALPHA * x_ref[rows, :] + acc, g_ref[...], b_ref[...])
            o_ref[rows, :] = x1
            xh = x1.astype(BF16)
            xl = (x1 - xh.astype(F32)).astype(BF16)
            hh = jnp.dot(xh, wrc_ref[...], preferred_element_type=F32)
            lh = jnp.dot(xl, wrc_ref[:, 0:LANES], preferred_element_type=F32)
            lg_ref[rows, :] = hh[:, 0:LANES] + (hh[:, LANES:] + lh) + br_ref[...]

    @pl.when(i >= nblk)
    def _():
        o_ref[...] = jnp.zeros(o_ref.shape, F32)


def _outproj(an, sn, x, w_bf, g, b, wr, br, tm, total_rows, row_block0, x1_buf=None):
    rows = an.shape[0]
    aliased = x1_buf is not None
    nblk = rows // tm
    tail = 0 if aliased or rows == total_rows else 1
    blk = lambda i: jnp.minimum(i, nblk - 1)
    half = lambda: pl.BlockSpec((tm, ATTN_WIDTH), lambda i: (blk(i), 0))
    in_specs = [half(), half(), pl.BlockSpec((tm, D_MODEL), lambda i: (blk(i), 0)),
                _full((D_MODEL, D_MODEL)), _full((1, D_MODEL)), _full((1, D_MODEL)),
                _full((D_MODEL, 2 * LANES)), _full((1, LANES))]
    wr_hi = wr.astype(BF16)
    wr_lo = (wr - wr_hi.astype(F32)).astype(BF16)
    args = [an, sn, x, w_bf, g, b, jnp.concatenate([wr_hi, wr_lo], axis=1), br]
    if aliased:
        in_specs = [pl.BlockSpec(memory_space=pl.ANY)] + in_specs
        args = [x1_buf] + args
    return pl.pallas_call(
        functools.partial(_outproj_body, aliased=aliased, nblk=nblk),
        grid=(nblk + tail,),
        in_specs=in_specs,
        out_specs=[pl.BlockSpec((tm, D_MODEL), lambda i: (i + row_block0, 0)),
                   pl.BlockSpec((tm, LANES), lambda i: (blk(i), 0))],
        out_shape=[jax.ShapeDtypeStruct((total_rows, D_MODEL), F32), jax.ShapeDtypeStruct((rows, LANES), F32)],
        input_output_aliases={0: 0} if aliased else {},
        compiler_params=_cparams(("arbitrary",)),
        name="outproj_ln1",
    )(*args)


ROUTE_T = 640
R_E0, R_E1, R_W0, R_W1, R_P0, R_P1 = range(6)


def _route_select(x):
    lane = lax.broadcasted_iota(jnp.int32, x.shape, 1)
    big = jnp.int32(1 << 20)
    neg = -jnp.inf
    gmask = lane < N_EXPERT_GROUPS
    gm = jnp.max(jnp.where(gmask, x, neg), axis=-1, keepdims=True)
    grp = jnp.min(jnp.where(gmask & (x == gm), lane, big), axis=-1, keepdims=True)
    p_grp = 1.0 / jnp.sum(jnp.where(gmask, jnp.exp(x - gm), 0.0), axis=-1, keepdims=True)
    lo = N_EXPERT_GROUPS + grp * EXPERTS_PER_GROUP
    emask = (lane >= lo) & (lane < lo + EXPERTS_PER_GROUP)
    v1 = jnp.max(jnp.where(emask, x, neg), axis=-1, keepdims=True)
    i1 = jnp.min(jnp.where(emask & (x == v1), lane, big), axis=-1, keepdims=True)
    emask2 = emask & (lane != i1)
    v2 = jnp.max(jnp.where(emask2, x, neg), axis=-1, keepdims=True)
    i2 = jnp.min(jnp.where(emask2 & (x == v2), lane, big), axis=-1, keepdims=True)
    e2 = jnp.exp(v2 - v1)
    w1 = p_grp / (1.0 + e2)
    w2 = p_grp * e2 / (1.0 + e2)
    return (i1 - N_EXPERT_GROUPS).astype(F32), (i2 - N_EXPERT_GROUPS).astype(F32), w1, w2


def _route_plan_body(lg_ref, ltri_ref, utri_ref, o_ref, meta_ref, rec_ref, run_ref):
    ph = pl.program_id(0)
    j = pl.program_id(1)
    tm = lg_ref.shape[0]
    lane = lax.broadcasted_iota(jnp.int32, (tm, LANES), 1)
    lanef = lane.astype(F32)
    r0 = pl.multiple_of(j * tm, tm)

    @pl.when((ph == 0) & (j == 0))
    def _():
        run_ref[...] = jnp.zeros(run_ref.shape, F32)

    @pl.when(ph == 0)
    def _():
        e0, e1, w0, w1 = _route_select(lg_ref[...])
        oh0 = (lanef == e0).astype(F32)
        oh1 = (lanef == e1).astype(F32)
        oh2 = oh0 + oh1
        before = jnp.dot(ltri_ref[...], oh2.astype(BF16), preferred_element_type=F32) + run_ref[0:1, :]
        rk0 = jnp.sum(before * oh0, axis=-1, keepdims=True)
        rk1 = jnp.sum(before * oh1, axis=-1, keepdims=True)
        run_ref[...] = run_ref[...] + jnp.sum(oh2, axis=0, keepdims=True)
        rec = jnp.where(lane == R_E0, e0, 0.0)
        rec = jnp.where(lane == R_E1, e1, rec)
        rec = jnp.where(lane == R_W0, w0, rec)
        rec = jnp.where(lane == R_W1, w1, rec)
        rec = jnp.where(lane == R_P0, rk0, rec)
        rec = jnp.where(lane == R_P1, rk1, rec)
        rec_ref[pl.ds(r0, tm), :] = rec

    @pl.when(ph == 1)
    def _():
        counts = run_ref[...]
        nblk = jnp.floor((counts + (MOE_BLOCK - 1)) * (1.0 / MOE_BLOCK))
        end_blk = jnp.dot(nblk.astype(BF16), utri_ref[...], preferred_element_type=F32)
        start_row = (end_blk - nblk) * MOE_BLOCK
        rec = rec_ref[pl.ds(r0, tm), :]
        oh0 = (lanef == rec[:, R_E0:R_E0 + 1]).astype(F32)
        oh1 = (lanef == rec[:, R_E1:R_E1 + 1]).astype(F32)
        p0 = jnp.sum(oh0 * start_row[0:1, :], axis=-1, keepdims=True) + rec[:, R_P0:R_P0 + 1]
        p1 = jnp.sum(oh1 * start_row[0:1, :], axis=-1, keepdims=True) + rec[:, R_P1:R_P1 + 1]
        out = jnp.where(lane == R_P0, p0, rec)
        out = jnp.where(lane == R_P1, p1, out)
        o_ref[...] = out

        @pl.when(j == 0)
        def _():
            meta_ref[...] = end_blk


def _route_plan(logits):
    rows = logits.shape[0]
    tm = ROUTE_T
    ltri = jnp.asarray(np.tril(np.ones((tm, tm), np.float32), -1), BF16)
    utri = jnp.asarray(np.triu(np.ones((LANES, LANES), np.float32)), BF16)
    mrows = SUBLANES
    return pl.pallas_call(
        _route_plan_body,
        grid=(2, rows // tm),
        in_specs=[pl.BlockSpec((tm, LANES), lambda p, j: (j * (1 - p), 0)),
                  pl.BlockSpec((tm, tm), lambda p, j: (0, 0)),
                  pl.BlockSpec((LANES, LANES), lambda p, j: (0, 0))],
        out_specs=[pl.BlockSpec((tm, LANES), lambda p, j: (j * p, 0)),
                   pl.BlockSpec((mrows, LANES), lambda p, j: (0, 0))],
        out_shape=[jax.ShapeDtypeStruct((rows, LANES), F32), jax.ShapeDtypeStruct((mrows, LANES), F32)],
        scratch_shapes=[pltpu.VMEM((rows, LANES), F32), pltpu.VMEM((SUBLANES, LANES), F32)],
        compiler_params=_cparams(("arbitrary", "arbitrary")),
        name="route_plan",
    )(logits, ltri, utri)


def _rowmap_body(pos_ref, o_ref, *, ntok, nrows):
    unroll = 8

    def zero(c, _):
        for u in range(unroll):
            o_ref[c * unroll + u] = 0
        return 0
    lax.fori_loop(0, nrows // unroll, zero, 0)

    def fill(c, _):
        for u in range(unroll):
            t = c * unroll + u
            o_ref[pos_ref[2 * t]] = t
            o_ref[pos_ref[2 * t + 1]] = t
        return 0
    lax.fori_loop(0, ntok // unroll, fill, 0)


def _rowmap(pos, nrows):
    ntok = pos.shape[0] // 2
    return pl.pallas_call(
        functools.partial(_rowmap_body, ntok=ntok, nrows=nrows),
        in_specs=[pl.BlockSpec(memory_space=pltpu.SMEM)],
        out_specs=pl.BlockSpec(memory_space=pltpu.SMEM),
        out_shape=jax.ShapeDtypeStruct((nrows,), jnp.int32),
        name="moe_rowmap",
    )(pos)


GATHER_SLOTS = 4


def _moe_body(bend_ref, row_tok_ref, x_hbm, wg_ref, wu_ref, wd_ref, ys_hbm, *scratch, nblocks):
    xbufs = scratch[:GATHER_SLOTS]
    ybufs = scratch[GATHER_SLOTS:GATHER_SLOTS + 2]
    gsem, osem, wg_bf, wu_bf, wd_bf = scratch[GATHER_SLOTS + 2:]
    ahead = GATHER_SLOTS - 1
    e = pl.program_id(0)
    nused = bend_ref[N_EXPERTS - 1]
    b1 = bend_ref[e]
    b0 = jnp.where(e == 0, 0, bend_ref[jnp.maximum(e - 1, 0)])

    def gather_start(blk, s):
        base = blk * MOE_BLOCK
        for r in range(MOE_BLOCK):
            tok = row_tok_ref[base + r]
            pltpu.make_async_copy(x_hbm.at[pl.ds(tok, 1), :], xbufs[s].at[pl.ds(r, 1), :],
                                  gsem.at[s]).start(priority=GATHER_PRIORITY)

    def gather_wait(s):
        pltpu.make_async_copy(x_hbm.at[pl.ds(0, MOE_BLOCK), :], xbufs[s], gsem.at[s]).wait()

    def out_copy(blk, s):
        row0 = pl.multiple_of(blk * MOE_BLOCK, MOE_BLOCK)
        return pltpu.make_async_copy(ybufs[s], ys_hbm.at[pl.ds(row0, MOE_BLOCK), :], osem.at[s])

    @pl.when(e == 0)
    def _():
        for k in range(ahead):
            gather_start(k, k)

    @pl.when(b1 > b0)
    def _():
        wg_bf[...] = wg_ref[...].astype(BF16)
        wu_bf[...] = wu_ref[...].astype(BF16)
        wd_bf[...] = wd_ref[...].astype(BF16)

        def do_block(b, s):
            so = s % 2
            gather_wait(s)

            @pl.when(b >= 2)
            def _():
                out_copy(b - 2, so).wait()

            gather_start(jnp.minimum(b + ahead, nused - 1), (s + ahead) % GATHER_SLOTS)
            x = xbufs[s][...].astype(BF16)
            hg = jnp.dot(x, wg_bf[...], preferred_element_type=F32)
            hu = jnp.dot(x, wu_bf[...], preferred_element_type=F32)
            h = (jax.nn.silu(hg) * hu).astype(BF16)
            ybufs[so][...] = jnp.dot(h, wd_bf[...], preferred_element_type=F32)
            out_copy(b, so).start()

        def body(b, _):
            for s in range(GATHER_SLOTS):
                @pl.when(b % GATHER_SLOTS == s)
                def _():
                    do_block(b, s)
            return 0

        lax.fori_loop(b0, b1, body, 0)

    @pl.when(e == N_EXPERTS - 1)
    def _():
        last = nused - 1
        for s in range(GATHER_SLOTS):
            @pl.when(last % GATHER_SLOTS == s)
            def _():
                for k in range(1, GATHER_SLOTS):
                    gather_wait((s + k) % GATHER_SLOTS)
        out_copy(0, 0).wait()
        out_copy(0, 1).wait()
        yb0 = ybufs[0]
        yb0[...] = jnp.zeros(yb0.shape, F32)

        def zstart(b, _):
            out_copy(b, 0).start()
            return 0

        def zwait(b, _):
            out_copy(b, 0).wait()
            return 0

        lax.fori_loop(nused, nblocks, zstart, 0)
        lax.fori_loop(nused, nblocks, zwait, 0)


def _moe(bend, row_tok, x1, w_gate, w_up, w_down, layer):
    nblocks = row_tok.shape[0] // MOE_BLOCK
    assert 2 * x1.shape[0] > GATHER_SLOTS * MOE_BLOCK
    wspec = lambda a, b: pl.BlockSpec((None, None, a, b), lambda e, be, rt: (layer, e, 0, 0))
    grid_spec = pltpu.PrefetchScalarGridSpec(
        num_scalar_prefetch=2,
        grid=(N_EXPERTS,),
        in_specs=[pl.BlockSpec(memory_space=pl.ANY), wspec(D_MODEL, EXPERT_FF), wspec(D_MODEL, EXPERT_FF),
                  wspec(EXPERT_FF, D_MODEL)],
        out_specs=pl.BlockSpec(memory_space=pl.ANY),
        scratch_shapes=([pltpu.VMEM((MOE_BLOCK, D_MODEL), F32) for _ in range(GATHER_SLOTS + 2)]
                        + [pltpu.SemaphoreType.DMA((GATHER_SLOTS,)), pltpu.SemaphoreType.DMA((2,)),
                           pltpu.VMEM((D_MODEL, EXPERT_FF), BF16),
                           pltpu.VMEM((D_MODEL, EXPERT_FF), BF16),
                           pltpu.VMEM((EXPERT_FF, D_MODEL), BF16)]),
    )
    return pl.pallas_call(
        functools.partial(_moe_body, nblocks=nblocks),
        grid_spec=grid_spec,
        out_shape=jax.ShapeDtypeStruct((nblocks * MOE_BLOCK, D_MODEL), F32),
        compiler_params=_cparams(("arbitrary",), bounds_checks=False),
        name="moe_experts",
    )(bend, row_tok, x1, w_gate, w_up, w_down)


COMB_T = 128


def _combine_body(pos_ref, ys_hbm, x_ref, rt_ref, g_ref, b_ref, op_ref, os_ref, *scratch, n_prompt_blocks):
    i = pl.program_id(0)
    nsteps = pl.num_programs(0)
    bufs = tuple((scratch[2 * s], scratch[2 * s + 1]) for s in range(GATHER_SLOTS))
    sem = scratch[2 * GATHER_SLOTS]
    ahead = GATHER_SLOTS - 1

    def gather_start(step, s):
        base = step * (2 * COMB_T)
        for r in range(COMB_T):
            for k in range(2):
                p = pos_ref[base + 2 * r + k]
                pltpu.make_async_copy(ys_hbm.at[pl.ds(p, 1), :], bufs[s][k].at[pl.ds(r, 1), :],
                                      sem.at[s]).start(priority=k)

    def gather_wait(s):
        for k in range(2):
            pltpu.make_async_copy(ys_hbm.at[pl.ds(0, COMB_T), :], bufs[s][k], sem.at[s]).wait()

    @pl.when(i == 0)
    def _():
        for k in range(ahead):
            gather_start(k, k)

    def do_tile(s):
        gather_wait(s)
        gather_start(jnp.minimum(i + ahead, nsteps - 1), (s + ahead) % GATHER_SLOTS)
        rt = rt_ref[...]
        ff = rt[:, R_W0:R_W0 + 1] * bufs[s][0][...] + rt[:, R_W1:R_W1 + 1] * bufs[s][1][...]
        out = _layer_norm(DEEPNORM_ALPHA * x_ref[...] + ff, g_ref[...], b_ref[...])

        @pl.when(i < n_prompt_blocks)
        def _():
            op_ref[...] = out

        @pl.when(i >= n_prompt_blocks)
        def _():
            os_ref[...] = out

        @pl.when(i == nsteps - 1)
        def _():
            for k in range(1, GATHER_SLOTS):
                gather_wait((s + k) % GATHER_SLOTS)

    for s in range(GATHER_SLOTS):
        @pl.when(i % GATHER_SLOTS == s)
        def _():
            do_tile(s)


def _combine(pos, ys, x1, route, g, b, n_prompt_rows):
    rows = x1.shape[0]
    npb = n_prompt_rows // COMB_T
    assert rows // COMB_T >= GATHER_SLOTS
    grid_spec = pltpu.PrefetchScalarGridSpec(
        num_scalar_prefetch=1,
        grid=(rows // COMB_T,),
        in_specs=[pl.BlockSpec(memory_space=pl.ANY),
                  pl.BlockSpec((COMB_T, D_MODEL), lambda i, p: (i, 0)),
                  pl.BlockSpec((COMB_T, LANES), lambda i, p: (i, 0)),
                  pl.BlockSpec((1, D_MODEL), lambda i, p: (0, 0)),
                  pl.BlockSpec((1, D_MODEL), lambda i, p: (0, 0))],
        out_specs=[pl.BlockSpec((COMB_T, D_MODEL), lambda i, p: (jnp.minimum(i, npb - 1), 0)),
                   pl.BlockSpec((COMB_T, D_MODEL), lambda i, p: (jnp.maximum(i - npb, 0), 0))],
        scratch_shapes=([pltpu.VMEM((COMB_T, D_MODEL), F32) for _ in range(2 * GATHER_SLOTS)]
                        + [pltpu.SemaphoreType.DMA((GATHER_SLOTS,))]),
    )
    return pl.pallas_call(
        functools.partial(_combine_body, n_prompt_blocks=npb),
        grid_spec=grid_spec,
        out_shape=[jax.ShapeDtypeStruct((n_prompt_rows, D_MODEL), F32),
                   jax.ShapeDtypeStruct((rows - n_prompt_rows, D_MODEL), F32)],
        compiler_params=_cparams(("arbitrary",), bounds_checks=False),
        name="combine_ln2",
    )(pos, ys, x1, route, g, b)


def kernel(x_prompt, x_sample, cache_k, cache_v, state_ssm_re, state_ssm_im, w_in, ssm_lambda_re, ssm_lambda_im, ssm_log_dt, ssm_b_re, ssm_b_im, ssm_c_re, ssm_c_im, ssm_d, ssm_w_glu, attn_sinks, attn_norm_g, ssm_norm_g, w_out, ln1_g, ln1_b, router_group_w, router_group_b, router_expert_w, router_expert_b, expert_w_gate, expert_w_up, expert_w_down, ln2_g, ln2_b):
    nbatch, seq, _ = x_prompt.shape
    nseq, dec_seq, _ = x_sample.shape
    wbuf = cache_k.shape[2]
    rows_p = nbatch * seq
    rows_s = nseq * dec_seq
    rows_all = rows_p + rows_s
    tm_p = 512

    xp = x_prompt.reshape(rows_p, D_MODEL)
    xs = x_sample.reshape(rows_s, D_MODEL)
    tab_p = _rope_tables(np.arange(seq))
    tab_s = _rope_tables(PAST_LEN + (np.arange(rows_s) % dec_seq))
    ck_all = cache_k.astype(F32).reshape(DEPTH, nseq, wbuf, KV_WIDTH)
    cv_all = cache_v.astype(F32).reshape(DEPTH, nseq, wbuf, KV_WIDTH)

    outs = {k: [] for k in ("kp", "vp", "hrp", "hip", "ks", "vs", "hrs", "his")}
    for l in range(DEPTH):
        w_in_bf = w_in[l].astype(BF16)
        w_glu_bf = ssm_w_glu[l].astype(BF16)
        w_out_bf = w_out[l].astype(BF16)
        sp = _ssm_params(ssm_lambda_re[l], ssm_lambda_im[l], ssm_log_dt[l], ssm_b_re[l], ssm_b_im[l],
                         ssm_c_re[l], ssm_c_im[l], ssm_d[l])
        sinks = attn_sinks[l].astype(F32)
        attn_g = attn_norm_g[l].astype(F32).reshape(1, ATTN_WIDTH)
        ssm_g = ssm_norm_g[l].astype(F32).reshape(1, SSM_WIDTH)
        g1 = ln1_g[l].astype(F32).reshape(1, D_MODEL)
        b1 = ln1_b[l].astype(F32).reshape(1, D_MODEL)
        g2 = ln2_g[l].astype(F32).reshape(1, D_MODEL)
        b2 = ln2_b[l].astype(F32).reshape(1, D_MODEL)
        n_rt = N_EXPERT_GROUPS + N_EXPERTS
        wr = jnp.concatenate([router_group_w[l].astype(F32)]
                             + [router_expert_w[l, g].astype(F32) for g in range(N_EXPERT_GROUPS)]
                             + [jnp.zeros((D_MODEL, LANES - n_rt), F32)], axis=1)
        br = jnp.concatenate([router_group_b[l].astype(F32), router_expert_b[l].astype(F32).reshape(-1),
                              jnp.zeros((LANES - n_rt,), F32)]).reshape(1, LANES)

        q_p, k_p, v_p, u_p = _inproj(xp, w_in_bf, *tab_p, tm=tm_p)
        q_s, k_s, v_s, u_s = _inproj(xs, w_in_bf, *tab_s, tm=rows_s)

        an_p = _attn_prompt(sinks, q_p, k_p, v_p, attn_g, seq)
        an_s, nk_s, nv_s = _attn_sample(sinks, q_s, k_s, v_s, ck_all, cv_all, attn_g, dec_seq, l)

        y_p, hl_p = _ssm_prompt(u_p, sp, nbatch, seq)
        y_s, hl_s = _ssm_sample(u_s, _pack_state(state_ssm_re[l], state_ssm_im[l]), sp, nseq, dec_seq)
        sn_p = _glu(y_p, w_glu_bf, ssm_g, tm_p)
        sn_s = _glu(y_s, w_glu_bf, ssm_g, rows_s)

        x1, lg_p = _outproj(an_p, sn_p, xp, w_out_bf, g1, b1, wr, br, tm_p, rows_all, 0)
        x1, lg_s = _outproj(an_s, sn_s, xs, w_out_bf, g1, b1, wr, br, rows_s, rows_all, rows_p // rows_s, x1_buf=x1)
        nblocks = 2 * rows_all // MOE_BLOCK + N_EXPERTS
        route, meta = _route_plan(jnp.concatenate([lg_p, lg_s], axis=0))
        pos = route[:, R_P0:R_P1 + 1].astype(jnp.int32).reshape(2 * rows_all)
        bend = meta[0, :N_EXPERTS].astype(jnp.int32)
        row_tok = _rowmap(pos, nblocks * MOE_BLOCK)

        ys = _moe(bend, row_tok, x1, expert_w_gate, expert_w_up, expert_w_down, l)
        xp, xs = _combine(pos, ys, x1, route, g2, b2, rows_p)

        wp = min(WINDOW, seq)
        outs["kp"].append(k_p.reshape(nbatch, seq, N_KV_HEADS, HEAD_DIM)[:, seq - wp:])
        outs["vp"].append(v_p.reshape(nbatch, seq, N_KV_HEADS, HEAD_DIM)[:, seq - wp:])
        hp = hl_p.reshape(nbatch, SSM_CHUNKS * 2 * CH_STATES)
        hr, hi = _unpack_state(hp, nbatch)
        outs["hrp"].append(hr)
        outs["hip"].append(hi)
        outs["ks"].append(nk_s.reshape(nseq, wbuf, N_KV_HEADS, HEAD_DIM))
        outs["vs"].append(nv_s.reshape(nseq, wbuf, N_KV_HEADS, HEAD_DIM))
        hr, hi = _unpack_state(hl_s, nseq)
        outs["hrs"].append(hr)
        outs["his"].append(hi)

    st = {k: jnp.stack(v) for k, v in outs.items()}
    return (xp.reshape(nbatch, seq, D_MODEL), xs.reshape(nseq, dec_seq, D_MODEL),
            st["kp"], st["vp"], st["hrp"], st["hip"], st["ks"], st["vs"], st["hrs"], st["his"])
```

```python
import functools
import math

import numpy as np
import jax
import jax.numpy as jnp
from jax import lax
from jax.experimental import pallas as pl
from jax.experimental.pallas import tpu as pltpu

F32 = jnp.float32
BF16 = jnp.bfloat16

D_MODEL = 2048
DEPTH = 2
PAST_LEN = 16384
ATTN_WIDTH = 1024
SSM_WIDTH = 1024
HEAD_DIM = 64
N_HEADS = 16
N_KV_HEADS = 4
Q_PER_KV = 4
KV_WIDTH = 256
WINDOW = 128
ROPE_THETA = 10000.0
SSM_GROUP = 16
SSM_GROUPS = 64
SSM_STATE = 64
N_EXPERT_GROUPS = 4
EXPERTS_PER_GROUP = 8
N_EXPERTS = 32
EXPERT_FF = 512
MOE_BLOCK = 128
DEEPNORM_ALPHA = (2.0 * DEPTH) ** 0.25
LN_EPS = 1e-5
RMS_EPS = 1e-6

LANES = 128
SUBLANES = 8
VMEM_LIMIT = 56 * 1024 * 1024

SSM_CHUNKS = 4
CH_GROUPS = SSM_GROUPS // SSM_CHUNKS
CH_STATES = CH_GROUPS * SSM_STATE
CH_IN = CH_GROUPS * SSM_GROUP
SCAN_T = 512
SCAN_SEG = SCAN_T // SUBLANES


def _cparams(sem=None, vmem=VMEM_LIMIT, bounds_checks=True):
    return pltpu.CompilerParams(dimension_semantics=sem, vmem_limit_bytes=vmem,
                                disable_bounds_checks=not bounds_checks)


def _full(shape):
    n = len(shape)
    return pl.BlockSpec(shape, lambda *a: (0,) * n)


GATHER_PRIORITY = 1


def _inproj_body(x_ref, w_ref, cos_ref, sa_ref, sb_ref, q_ref, k_ref, v_ref, u_ref):
    x = x_ref[...].astype(BF16)
    cos = cos_ref[...]
    sa = sa_ref[...]
    sb = sb_ref[...]

    def rope(c):
        return c * cos + pltpu.roll(c, LANES - 32, 1) * sa + pltpu.roll(c, 32, 1) * sb

    pq = jnp.dot(x, w_ref[:, 0:ATTN_WIDTH], preferred_element_type=F32)
    for j in range(ATTN_WIDTH // LANES):
        q_ref[:, j * LANES:(j + 1) * LANES] = rope(pq[:, j * LANES:(j + 1) * LANES]).astype(BF16)
    pk = jnp.dot(x, w_ref[:, ATTN_WIDTH:ATTN_WIDTH + KV_WIDTH], preferred_element_type=F32)
    for j in range(KV_WIDTH // LANES):
        k_ref[:, j * LANES:(j + 1) * LANES] = rope(pk[:, j * LANES:(j + 1) * LANES])
    v_ref[...] = jnp.dot(x, w_ref[:, ATTN_WIDTH + KV_WIDTH:ATTN_WIDTH + 2 * KV_WIDTH],
                         preferred_element_type=F32)
    u_ref[...] = jnp.dot(x, w_ref[:, ATTN_WIDTH + 2 * KV_WIDTH:], preferred_element_type=F32)


def _layer_spec(shape, layer):
    return pl.BlockSpec((None,) + tuple(shape), lambda *a: (layer, 0, 0))


def _inproj(x, w_bf, layer, cos_t, sa_t, sb_t, tm):
    rows = x.shape[0]
    tab_blocks = cos_t.shape[0] // tm
    in_w = w_bf.shape[2]
    row_spec = lambda w: pl.BlockSpec((tm, w), lambda i: (i, 0))
    tab_spec = pl.BlockSpec((tm, LANES), lambda i: (i % tab_blocks, 0))
    return pl.pallas_call(
        _inproj_body,
        grid=(rows // tm,),
        in_specs=[row_spec(D_MODEL), _layer_spec((D_MODEL, in_w), layer), tab_spec, tab_spec, tab_spec],
        out_specs=[row_spec(ATTN_WIDTH), row_spec(KV_WIDTH), row_spec(KV_WIDTH), row_spec(SSM_WIDTH)],
        out_shape=[jax.ShapeDtypeStruct((rows, ATTN_WIDTH), BF16),
                   jax.ShapeDtypeStruct((rows, KV_WIDTH), F32),
                   jax.ShapeDtypeStruct((rows, KV_WIDTH), F32),
                   jax.ShapeDtypeStruct((rows, SSM_WIDTH), F32)],
        compiler_params=_cparams(("arbitrary",)),
        name="inproj_rope",
    )(x, w_bf, cos_t, sa_t, sb_t)


def _rope_tables(positions):
    half = HEAD_DIM // 2
    inv_freq = ROPE_THETA ** (-np.arange(half, dtype=np.float64) / half)
    ang = positions.astype(np.float64)[:, None] * inv_freq[None, :]
    lane = np.arange(LANES)
    cos = np.cos(ang)[:, lane % half]
    sin = np.sin(ang)[:, lane % half]
    first = (lane % HEAD_DIM) < half
    sa = np.where(first[None, :], -sin, 0.0)
    sb = np.where(first[None, :], 0.0, sin)
    return (jnp.asarray(cos, F32), jnp.asarray(sa, F32), jnp.asarray(sb, F32))


def _rms(x, g):
    return x * lax.rsqrt(jnp.mean(jnp.square(x), axis=-1, keepdims=True) + RMS_EPS) * g


def _attn_prompt_body(sink_ref, q_ref, kp_ref, kc_ref, vp_ref, vc_ref, g_ref, o_ref, acc_ref, *, blocks_per_seq):
    i = pl.program_id(0)
    has_prev = (i % blocks_per_seq) > 0
    kk = jnp.concatenate([kp_ref[...], kc_ref[...]], axis=0).astype(BF16)
    vv = jnp.concatenate([vp_ref[...], vc_ref[...]], axis=0).astype(BF16)
    row = lax.broadcasted_iota(jnp.int32, (WINDOW, 2 * WINDOW), 0)
    col = lax.broadcasted_iota(jnp.int32, (WINDOW, 2 * WINDOW), 1)
    visible = (col > row) & (col <= row + WINDOW) & ((col >= WINDOW) | has_prev)
    q = q_ref[...]
    for h in range(N_HEADS):
        g = h // Q_PER_KV
        qh = q[:, h * HEAD_DIM:(h + 1) * HEAD_DIM]
        kh = kk[:, g * HEAD_DIM:(g + 1) * HEAD_DIM]
        vh = vv[:, g * HEAD_DIM:(g + 1) * HEAD_DIM]
        s = lax.dot_general(qh, kh, (((1,), (1,)), ((), ())), preferred_element_type=F32) * (HEAD_DIM ** -0.5)
        s = jnp.where(visible, s, -jnp.inf)
        sink = sink_ref[h]
        m = jnp.maximum(jnp.max(s, axis=-1, keepdims=True), sink)
        p = jnp.exp(s - m)
        denom = jnp.sum(p, axis=-1, keepdims=True) + jnp.exp(sink - m)
        oh = jnp.dot(p.astype(BF16), vh, preferred_element_type=F32) / denom
        acc_ref[:, h * HEAD_DIM:(h + 1) * HEAD_DIM] = oh
    o_ref[...] = _rms(acc_ref[...], g_ref[...]).astype(BF16)


def _attn_prompt(sinks, q, k, v, g, seq):
    rows = q.shape[0]
    nblk = rows // WINDOW
    bps = seq // WINDOW
    cur = lambda w: pl.BlockSpec((WINDOW, w), lambda i: (i, 0))
    prev = lambda w: pl.BlockSpec((WINDOW, w), lambda i: (jnp.maximum(i - 1, 0), 0))
    return pl.pallas_call(
        functools.partial(_attn_prompt_body, blocks_per_seq=bps),
        grid=(nblk,),
        in_specs=[pl.BlockSpec(memory_space=pltpu.SMEM), cur(ATTN_WIDTH), prev(KV_WIDTH), cur(KV_WIDTH),
                  prev(KV_WIDTH), cur(KV_WIDTH), _full((1, ATTN_WIDTH))],
        out_specs=cur(ATTN_WIDTH),
        out_shape=jax.ShapeDtypeStruct((rows, ATTN_WIDTH), BF16),
        scratch_shapes=[pltpu.VMEM((WINDOW, ATTN_WIDTH), F32)],
        compiler_params=_cparams(("arbitrary",)),
        name="attn_prompt",
    )(sinks, q, k, k, v, v, g)


SEQ_PER_STEP = 8


def _attn_sample_body(sink_ref, q_ref, kn_ref, vn_ref, ck_ref, cv_ref, g_ref, o_ref, nk_ref, nv_ref, acc_ref,
                      *, dec_seq):
    nq = SEQ_PER_STEP * dec_seq
    wb = ck_ref.shape[1]
    nkc = SEQ_PER_STEP * wb
    ck = ck_ref[...].reshape(nkc, KV_WIDTH)
    cv = cv_ref[...].reshape(nkc, KV_WIDTH)
    kn = kn_ref[...]
    vn = vn_ref[...]
    ckb = ck.astype(BF16)
    cvb = cv.astype(BF16)
    knb = kn.astype(BF16)
    vnb = vn.astype(BF16)
    q = q_ref[...]

    rows = Q_PER_KV * nq
    r = lax.broadcasted_iota(jnp.int32, (rows, nkc), 0) % nq
    c = lax.broadcasted_iota(jnp.int32, (rows, nkc), 1)
    vis_c = ((r // dec_seq) == (c // wb)) & ((c % wb) >= (r % dec_seq) + 1 + (wb - WINDOW))
    rn = lax.broadcasted_iota(jnp.int32, (rows, nq), 0) % nq
    cn = lax.broadcasted_iota(jnp.int32, (rows, nq), 1)
    vis_n = ((rn // dec_seq) == (cn // dec_seq)) & ((cn % dec_seq) <= (rn % dec_seq))
    hrow = lax.broadcasted_iota(jnp.int32, (rows, 1), 0) // nq

    dn = (((1,), (1,)), ((), ()))
    for g in range(N_KV_HEADS):
        qs = jnp.concatenate([q[:, (g * Q_PER_KV + j) * HEAD_DIM:(g * Q_PER_KV + j + 1) * HEAD_DIM]
                              for j in range(Q_PER_KV)], axis=0)
        sl = slice(g * HEAD_DIM, (g + 1) * HEAD_DIM)
        sc = lax.dot_general(qs, ckb[:, sl], dn, preferred_element_type=F32) * (HEAD_DIM ** -0.5)
        sn = lax.dot_general(qs, knb[:, sl], dn, preferred_element_type=F32) * (HEAD_DIM ** -0.5)
        sc = jnp.where(vis_c, sc, -jnp.inf)
        sn = jnp.where(vis_n, sn, -jnp.inf)
        sink = jnp.zeros((rows, 1), F32)
        for j in range(Q_PER_KV):
            sink = jnp.where(hrow == j, sink_ref[g * Q_PER_KV + j], sink)
        m = jnp.maximum(jnp.maximum(jnp.max(sc, axis=-1, keepdims=True), jnp.max(sn, axis=-1, keepdims=True)), sink)
        pc = jnp.exp(sc - m)
        pn = jnp.exp(sn - m)
        denom = jnp.sum(pc, axis=-1, keepdims=True) + jnp.sum(pn, axis=-1, keepdims=True) + jnp.exp(sink - m)
        o = (jnp.dot(pc.astype(BF16), cvb[:, sl], preferred_element_type=F32)
             + jnp.dot(pn.astype(BF16), vnb[:, sl], preferred_element_type=F32)) / denom
        for j in range(Q_PER_KV):
            h = g * Q_PER_KV + j
            acc_ref[:, h * HEAD_DIM:(h + 1) * HEAD_DIM] = o[j * nq:(j + 1) * nq, :]
    o_ref[...] = _rms(acc_ref[...], g_ref[...]).astype(BF16)

    keep = wb - dec_seq
    for s in range(SEQ_PER_STEP):
        nk_ref[s, 0:keep, :] = ck_ref[s, dec_seq:wb, :]
        nv_ref[s, 0:keep, :] = cv_ref[s, dec_seq:wb, :]
        nk_ref[s, keep:wb, :] = kn[s * dec_seq:(s + 1) * dec_seq, :]
        nv_ref[s, keep:wb, :] = vn[s * dec_seq:(s + 1) * dec_seq, :]


def _attn_sample(sinks, q, kn, vn, ck, cv, g, dec_seq, layer):
    rows = q.shape[0]
    _, nseq, wb, _ = ck.shape
    nq = SEQ_PER_STEP * dec_seq
    rowb = lambda w: pl.BlockSpec((nq, w), lambda i: (i, 0))
    cin = pl.BlockSpec((None, SEQ_PER_STEP, wb, KV_WIDTH), lambda i: (layer, i, 0, 0))
    cb = pl.BlockSpec((SEQ_PER_STEP, wb, KV_WIDTH), lambda i: (i, 0, 0))
    return pl.pallas_call(
        functools.partial(_attn_sample_body, dec_seq=dec_seq),
        grid=(nseq // SEQ_PER_STEP,),
        in_specs=[pl.BlockSpec(memory_space=pltpu.SMEM), rowb(ATTN_WIDTH), rowb(KV_WIDTH), rowb(KV_WIDTH),
                  cin, cin, _full((1, ATTN_WIDTH))],
        out_specs=[rowb(ATTN_WIDTH), cb, cb],
        out_shape=[jax.ShapeDtypeStruct((rows, ATTN_WIDTH), BF16),
                   jax.ShapeDtypeStruct(ck.shape[1:], F32), jax.ShapeDtypeStruct(cv.shape[1:], F32)],
        scratch_shapes=[pltpu.VMEM((nq, ATTN_WIDTH), F32)],
        compiler_params=_cparams(("arbitrary",)),
        name="attn_sample",
    )(sinks, q, kn, vn, ck, cv, g)


def _ssm_prompt_body(ua_ref, uc_ref, b_ref, c_ref, lr_ref, li_ref, pr_ref, pi_ref, d_ref, y_ref, hl_ref,
                     uperm_ref, bu0, bu1, hb0, hb1, yp_ref, carry_ref, *, nchunks, ntc):
    k = pl.program_id(0)
    bus = (bu0, bu1)
    hbs = (hb0, hb1)
    lanes_per_chunk = CH_IN // LANES
    seg = lambda s: slice(s * SCAN_SEG, (s + 1) * SCAN_SEG)
    lane = lambda c: slice(c * LANES, (c + 1) * LANES)

    def stage_a(item, slot):
        cc = item % SSM_CHUNKS
        for c in range(lanes_per_chunk):
            for s in range(SUBLANES):
                uperm_ref[c, pl.ds(s, SCAN_SEG, stride=SUBLANES), :] = ua_ref[seg(s), lane(c)]
        up = jnp.concatenate([uperm_ref[c] for c in range(lanes_per_chunk)], axis=1)
        bus[slot][...] = jnp.dot(up.astype(BF16), b_ref[cc], preferred_element_type=F32)

    def stage_b(item, slot):
        cc = item % SSM_CHUNKS
        tc = (item // SSM_CHUNKS) % ntc
        bu_ref = bus[slot]
        hb_ref = hbs[slot]
        ar = jnp.broadcast_to(lr_ref[cc], (SUBLANES, CH_STATES))
        ai = jnp.broadcast_to(li_ref[cc], (SUBLANES, CH_STATES))

        def advance(h, j):
            hr, hi = h
            br = bu_ref[j * SUBLANES:(j + 1) * SUBLANES, 0:CH_STATES]
            bi = bu_ref[j * SUBLANES:(j + 1) * SUBLANES, CH_STATES:2 * CH_STATES]
            return ar * hr + (br - ai * hi), ar * hi + (bi + ai * hr)

        zero = jnp.zeros((SUBLANES, CH_STATES), F32)
        h = (zero, zero)
        for j in range(SCAN_SEG):
            h = advance(h, j)
        fr, fi = h

        cst = jnp.where(tc == 0, 0.0, carry_ref[cc])
        c_r = cst[0:1, :]
        c_i = cst[1:2, :]
        lsr = pr_ref[cc]
        lsi = pi_ref[cc]
        rows_r, rows_i = [], []
        for s in range(SUBLANES):
            rows_r.append(c_r)
            rows_i.append(c_i)
            n_r = fr[s:s + 1, :] + (lsr * c_r - lsi * c_i)
            n_i = fi[s:s + 1, :] + (lsr * c_i + lsi * c_r)
            c_r, c_i = n_r, n_i
        carry_ref[cc] = jnp.concatenate([c_r, c_i], axis=0)
        hl_ref[0, cc] = jnp.concatenate([c_r, c_i], axis=0)

        h = (jnp.concatenate(rows_r, axis=0), jnp.concatenate(rows_i, axis=0))
        for jj in range(SCAN_SEG // 2):
            h1 = advance(h, 2 * jj)
            h = advance(h1, 2 * jj + 1)
            rows = slice(2 * jj * SUBLANES, (2 * jj + 2) * SUBLANES)
            hb_ref[rows, 0:CH_STATES] = jnp.concatenate([h1[0], h[0]], axis=0).astype(BF16)
            hb_ref[rows, CH_STATES:2 * CH_STATES] = jnp.concatenate([h1[1], h[1]], axis=0).astype(BF16)

    def stage_c(item, slot):
        cc = item % SSM_CHUNKS
        yp = jnp.dot(hbs[slot][...], c_ref[cc], preferred_element_type=F32)
        d = d_ref[cc]
        for c in range(lanes_per_chunk):
            yp_ref[c] = yp[:, lane(c)]
        for s in range(SUBLANES):
            for c in range(lanes_per_chunk):
                y_ref[seg(s), lane(c)] = (yp_ref[c, pl.ds(s, SCAN_SEG, stride=SUBLANES), :]
                                          + d[:, lane(c)] * uc_ref[seg(s), lane(c)])

    def run(do_a, do_b, do_c, parity):
        if do_a:
            stage_a(k, parity)
        if do_b:
            stage_b(k - 1, 1 - parity)
        if do_c:
            stage_c(k - 2, parity)

    @pl.when(k == 0)
    def _():
        carry_ref[...] = jnp.zeros(carry_ref.shape, F32)
        run(True, False, False, 0)

    @pl.when(k == 1)
    def _():
        run(True, True, False, 1)

    for parity in range(2):
        @pl.when((k >= 2) & (k < nchunks) & (k % 2 == parity))
        def _():
            run(True, True, True, parity)

    @pl.when(k == nchunks)
    def _():
        run(False, True, True, nchunks % 2)

    @pl.when(k == nchunks + 1)
    def _():
        run(False, False, True, (nchunks + 1) % 2)


def _ssm_prompt(u, sp, nbatch, seq):
    rows = u.shape[0]
    ntc = seq // SCAN_T
    nchunks = nbatch * ntc * SSM_CHUNKS
    assert nchunks >= 2
    item_a = lambda k: jnp.minimum(k, nchunks - 1)
    item_b = lambda k: jnp.clip(k - 1, 0, nchunks - 1)
    item_c = lambda k: jnp.clip(k - 2, 0, nchunks - 1)
    chunk_block = lambda item: (item // SSM_CHUNKS, item % SSM_CHUNKS)
    y, hl = pl.pallas_call(
        functools.partial(_ssm_prompt_body, nchunks=nchunks, ntc=ntc),
        grid=(nchunks + 2,),
        in_specs=[pl.BlockSpec((SCAN_T, CH_IN), lambda k: chunk_block(item_a(k))),
                  pl.BlockSpec((SCAN_T, CH_IN), lambda k: chunk_block(item_c(k))),
                  _full(sp["B"].shape), _full(sp["C"].shape), _full(sp["lr"].shape), _full(sp["li"].shape),
                  _full(sp["pr"].shape), _full(sp["pi"].shape), _full(sp["d"].shape)],
        out_specs=[pl.BlockSpec((SCAN_T, CH_IN), lambda k: chunk_block(item_c(k))),
                   pl.BlockSpec((1, SSM_CHUNKS, 2, CH_STATES),
                                lambda k: (item_b(k) // (ntc * SSM_CHUNKS), 0, 0, 0))],
        out_shape=[jax.ShapeDtypeStruct((rows, SSM_WIDTH), F32),
                   jax.ShapeDtypeStruct((nbatch, SSM_CHUNKS, 2, CH_STATES), F32)],
        scratch_shapes=[pltpu.VMEM((CH_IN // LANES, SCAN_T, LANES), F32),
                        pltpu.VMEM((SCAN_T, 2 * CH_STATES), F32), pltpu.VMEM((SCAN_T, 2 * CH_STATES), F32),
                        pltpu.VMEM((SCAN_T, 2 * CH_STATES), BF16), pltpu.VMEM((SCAN_T, 2 * CH_STATES), BF16),
                        pltpu.VMEM((CH_IN // LANES, SCAN_T, LANES), F32),
                        pltpu.VMEM((SSM_CHUNKS, 2, CH_STATES), F32)],
        compiler_params=_cparams(("arbitrary",)),
        name="ssm_prompt",
    )(u, u, sp["B"], sp["C"], sp["lr"], sp["li"], sp["pr"], sp["pi"], sp["d"])
    return y, hl


def _ssm_sample_body(u_ref, h0_ref, b_ref, c_ref, lr_ref, li_ref, d_ref, y_ref, ho_ref, bu_ref, *, nseq, dec_seq):
    nl = CH_STATES // LANES
    for k in range(SSM_CHUNKS):
        uk = u_ref[:, k * CH_IN:(k + 1) * CH_IN]
        bu = jnp.dot(uk.astype(BF16), b_ref[k], preferred_element_type=F32)
        for c in range(2 * nl):
            bu_ref[c] = bu[:, c * LANES:(c + 1) * LANES]
        base = k * 2 * CH_STATES
        hr = h0_ref[:, base:base + CH_STATES]
        hi = h0_ref[:, base + CH_STATES:base + 2 * CH_STATES]
        ar = lr_ref[k]
        ai = li_ref[k]
        for t in range(dec_seq):
            br = jnp.concatenate([bu_ref[c, pl.ds(t, nseq, stride=dec_seq), :] for c in range(nl)], axis=1)
            bi = jnp.concatenate([bu_ref[nl + c, pl.ds(t, nseq, stride=dec_seq), :] for c in range(nl)], axis=1)
            hr, hi = ar * hr + (br - ai * hi), ar * hi + (bi + ai * hr)
            for c in range(nl):
                bu_ref[c, pl.ds(t, nseq, stride=dec_seq), :] = hr[:, c * LANES:(c + 1) * LANES]
                bu_ref[nl + c, pl.ds(t, nseq, stride=dec_seq), :] = hi[:, c * LANES:(c + 1) * LANES]
        ho_ref[:, base:base + CH_STATES] = hr
        ho_ref[:, base + CH_STATES:base + 2 * CH_STATES] = hi
        hall = jnp.concatenate([bu_ref[c] for c in range(2 * nl)], axis=1)
        y_ref[:, k * CH_IN:(k + 1) * CH_IN] = (
            jnp.dot(hall.astype(BF16), c_ref[k], preferred_element_type=F32) + d_ref[k] * uk)


def _ssm_sample(u, h0, sp, nseq, dec_seq):
    rows = u.shape[0]
    return pl.pallas_call(
        functools.partial(_ssm_sample_body, nseq=nseq, dec_seq=dec_seq),
        grid=(1,),
        in_specs=[_full(u.shape), _full(h0.shape), _full(sp["B"].shape), _full(sp["C"].shape),
                  _full(sp["lr"].shape), _full(sp["li"].shape), _full(sp["d"].shape)],
        out_specs=[_full((rows, SSM_WIDTH)), _full(h0.shape)],
        out_shape=[jax.ShapeDtypeStruct((rows, SSM_WIDTH), F32), jax.ShapeDtypeStruct(h0.shape, F32)],
        scratch_shapes=[pltpu.VMEM((2 * CH_STATES // LANES, rows, LANES), F32)],
        compiler_params=_cparams(("arbitrary",)),
        name="ssm_sample",
    )(u, h0, sp["B"], sp["C"], sp["lr"], sp["li"], sp["d"])


def _ssm_params(lambda_re, lambda_im, log_dt, b_re, b_im, c_re, c_im, d):
    lam = lax.complex(jnp.minimum(lambda_re.astype(F32), -1e-4), lambda_im.astype(F32))
    dt = jnp.exp(log_dt.astype(F32))[:, None]
    lam_dt = lam * dt
    lam_bar = jnp.exp(lam_dt)
    b = lax.complex(b_re.astype(F32), b_im.astype(F32))
    b_bar = ((lam_bar - 1.0) / lam)[..., None] * b
    eye = jnp.eye(CH_GROUPS, dtype=F32)

    def pack_b(part):
        x = part.reshape(SSM_CHUNKS, CH_GROUPS, SSM_STATE, SSM_GROUP)
        return jnp.einsum("kgpc,gh->kgchp", x, eye).reshape(SSM_CHUNKS, CH_IN, CH_STATES)

    def pack_c(part):
        x = part.reshape(SSM_CHUNKS, CH_GROUPS, SSM_GROUP, SSM_STATE)
        return jnp.einsum("kgcp,gh->kgphc", x, eye).reshape(SSM_CHUNKS, CH_STATES, CH_IN)

    bmat = jnp.concatenate([pack_b(jnp.real(b_bar)), pack_b(jnp.imag(b_bar))], axis=2).astype(BF16)
    cmat = jnp.concatenate([pack_c(c_re.astype(F32)), pack_c(-c_im.astype(F32))], axis=1).astype(BF16)
    pw = jnp.exp(lam_dt * float(SCAN_SEG)).reshape(SSM_CHUNKS, 1, CH_STATES)
    lb = lam_bar.reshape(SSM_CHUNKS, 1, CH_STATES)
    return dict(B=bmat, C=cmat, lr=jnp.real(lb), li=jnp.imag(lb), pr=jnp.real(pw), pi=jnp.imag(pw),
                d=d.astype(F32).reshape(SSM_CHUNKS, 1, CH_IN))


def _pack_state(re, im):
    n = re.shape[0]
    st = jnp.stack([re.reshape(n, SSM_CHUNKS, CH_STATES), im.reshape(n, SSM_CHUNKS, CH_STATES)], axis=2)
    return st.reshape(n, SSM_CHUNKS * 2 * CH_STATES).astype(F32)


def _unpack_state(h, n):
    st = h.reshape(n, SSM_CHUNKS, 2, CH_GROUPS, SSM_STATE)
    return (st[:, :, 0].reshape(n, SSM_GROUPS, SSM_STATE), st[:, :, 1].reshape(n, SSM_GROUPS, SSM_STATE))


def _glu_body(y_ref, w_ref, g_ref, o_ref):
    a = jax.nn.gelu(y_ref[...])
    z = a * jax.nn.sigmoid(jnp.dot(a.astype(BF16), w_ref[...], preferred_element_type=F32))
    o_ref[...] = _rms(z, g_ref[...]).astype(BF16)


def _glu(y, w_bf, layer, g, tm):
    rows = y.shape[0]
    spec = pl.BlockSpec((tm, SSM_WIDTH), lambda i: (i, 0))
    return pl.pallas_call(
        _glu_body,
        grid=(rows // tm,),
        in_specs=[spec, _layer_spec((SSM_WIDTH, SSM_WIDTH), layer), _full((1, SSM_WIDTH))],
        out_specs=spec,
        out_shape=jax.ShapeDtypeStruct((rows, SSM_WIDTH), BF16),
        compiler_params=_cparams(("arbitrary",)),
        name="glu_rms",
    )(y, w_bf, g)


def _layer_norm(y, g, b):
    mu = jnp.mean(y, axis=-1, keepdims=True)
    yc = y - mu
    var = jnp.mean(jnp.square(yc), axis=-1, keepdims=True)
    return yc * lax.rsqrt(var + LN_EPS) * g + b


OUTPROJ_SUB = 256


def _outproj_body(*refs, aliased, nblk):
    if aliased:
        refs = refs[1:]
    a_ref, s_ref, x_ref, w_ref, g_ref, b_ref, wrc_ref, br_ref, o_ref, lg_ref = refs
    i = pl.program_id(0)

    @pl.when(i < nblk)
    def _():
        tm = a_ref.shape[0]
        sub = min(tm, OUTPROJ_SUB)
        for r0 in range(0, tm, sub):
            rows = slice(r0, r0 + sub)
            acc = jnp.dot(a_ref[rows, :], w_ref[0:ATTN_WIDTH, :], preferred_element_type=F32)
            acc = acc + jnp.dot(s_ref[rows, :], w_ref[ATTN_WIDTH:, :], preferred_element_type=F32)
            x1 = _layer_norm(DEEPNORM_ALPHA * x_ref[rows, :] + acc, g_ref[...], b_ref[...])
            o_ref[rows, :] = x1
            xh = x1.astype(BF16)
            xl = (x1 - xh.astype(F32)).astype(BF16)
            hh = jnp.dot(xh, wrc_ref[...], preferred_element_type=F32)
            lh = jnp.dot(xl, wrc_ref[:, 0:LANES], preferred_element_type=F32)
            lg_ref[rows, :] = hh[:, 0:LANES] + (hh[:, LANES:] + lh) + br_ref[...]

    @pl.when(i >= nblk)
    def _():
        o_ref[...] = jnp.zeros(o_ref.shape, F32)


def _outproj(an, sn, x, w_bf, layer, g, b, wr, br, tm, total_rows, row_block0, x1_buf=None):
    rows = an.shape[0]
    aliased = x1_buf is not None
    nblk = rows // tm
    tail = 0 if aliased or rows == total_rows else 1
    blk = lambda i: jnp.minimum(i, nblk - 1)
    half = lambda: pl.BlockSpec((tm, ATTN_WIDTH), lambda i: (blk(i), 0))
    in_specs = [half(), half(), pl.BlockSpec((tm, D_MODEL), lambda i: (blk(i), 0)),
                _layer_spec((D_MODEL, D_MODEL), layer), _full((1, D_MODEL)), _full((1, D_MODEL)),
                _full((D_MODEL, 2 * LANES)), _full((1, LANES))]
    wr_hi = wr.astype(BF16)
    wr_lo = (wr - wr_hi.astype(F32)).astype(BF16)
    args = [an, sn, x, w_bf, g, b, jnp.concatenate([wr_hi, wr_lo], axis=1), br]
    if aliased:
        in_specs = [pl.BlockSpec(memory_space=pl.ANY)] + in_specs
        args = [x1_buf] + args
    return pl.pallas_call(
        functools.partial(_outproj_body, aliased=aliased, nblk=nblk),
        grid=(nblk + tail,),
        in_specs=in_specs,
        out_specs=[pl.BlockSpec((tm, D_MODEL), lambda i: (i + row_block0, 0)),
                   pl.BlockSpec((tm, LANES), lambda i: (blk(i), 0))],
        out_shape=[jax.ShapeDtypeStruct((total_rows, D_MODEL), F32), jax.ShapeDtypeStruct((rows, LANES), F32)],
        input_output_aliases={0: 0} if aliased else {},
        compiler_params=_cparams(("arbitrary",)),
        name="outproj_ln1",
    )(*args)


ROUTE_T = 640
R_E0, R_E1, R_W0, R_W1, R_P0, R_P1 = range(6)


def _route_select(x):
    lane = lax.broadcasted_iota(jnp.int32, x.shape, 1)
    big = jnp.int32(1 << 20)
    neg = -jnp.inf
    gmask = lane < N_EXPERT_GROUPS
    gm = jnp.max(jnp.where(gmask, x, neg), axis=-1, keepdims=True)
    grp = jnp.min(jnp.where(gmask & (x == gm), lane, big), axis=-1, keepdims=True)
    p_grp = 1.0 / jnp.sum(jnp.where(gmask, jnp.exp(x - gm), 0.0), axis=-1, keepdims=True)
    lo = N_EXPERT_GROUPS + grp * EXPERTS_PER_GROUP
    emask = (lane >= lo) & (lane < lo + EXPERTS_PER_GROUP)
    v1 = jnp.max(jnp.where(emask, x, neg), axis=-1, keepdims=True)
    i1 = jnp.min(jnp.where(emask & (x == v1), lane, big), axis=-1, keepdims=True)
    emask2 = emask & (lane != i1)
    v2 = jnp.max(jnp.where(emask2, x, neg), axis=-1, keepdims=True)
    i2 = jnp.min(jnp.where(emask2 & (x == v2), lane, big), axis=-1, keepdims=True)
    e2 = jnp.exp(v2 - v1)
    w1 = p_grp / (1.0 + e2)
    w2 = p_grp * e2 / (1.0 + e2)
    return (i1 - N_EXPERT_GROUPS).astype(F32), (i2 - N_EXPERT_GROUPS).astype(F32), w1, w2


def _route_plan_body(lg_ref, ltri_ref, utri_ref, o_ref, meta_ref, rec_ref, run_ref):
    ph = pl.program_id(0)
    j = pl.program_id(1)
    tm = lg_ref.shape[0]
    lane = lax.broadcasted_iota(jnp.int32, (tm, LANES), 1)
    lanef = lane.astype(F32)
    r0 = pl.multiple_of(j * tm, tm)

    @pl.when((ph == 0) & (j == 0))
    def _():
        run_ref[...] = jnp.zeros(run_ref.shape, F32)

    @pl.when(ph == 0)
    def _():
        e0, e1, w0, w1 = _route_select(lg_ref[...])
        oh0 = (lanef == e0).astype(F32)
        oh1 = (lanef == e1).astype(F32)
        oh2 = oh0 + oh1
        before = jnp.dot(ltri_ref[...], oh2.astype(BF16), preferred_element_type=F32) + run_ref[0:1, :]
        rk0 = jnp.sum(before * oh0, axis=-1, keepdims=True)
        rk1 = jnp.sum(before * oh1, axis=-1, keepdims=True)
        run_ref[...] = run_ref[...] + jnp.sum(oh2, axis=0, keepdims=True)
        rec = jnp.where(lane == R_E0, e0, 0.0)
        rec = jnp.where(lane == R_E1, e1, rec)
        rec = jnp.where(lane == R_W0, w0, rec)
        rec = jnp.where(lane == R_W1, w1, rec)
        rec = jnp.where(lane == R_P0, rk0, rec)
        rec = jnp.where(lane == R_P1, rk1, rec)
        rec_ref[pl.ds(r0, tm), :] = rec

    @pl.when(ph == 1)
    def _():
        counts = run_ref[...]
        nblk = jnp.floor((counts + (MOE_BLOCK - 1)) * (1.0 / MOE_BLOCK))
        end_blk = jnp.dot(nblk.astype(BF16), utri_ref[...], preferred_element_type=F32)
        start_row = (end_blk - nblk) * MOE_BLOCK
        rec = rec_ref[pl.ds(r0, tm), :]
        oh0 = (lanef == rec[:, R_E0:R_E0 + 1]).astype(F32)
        oh1 = (lanef == rec[:, R_E1:R_E1 + 1]).astype(F32)
        p0 = jnp.sum(oh0 * start_row[0:1, :], axis=-1, keepdims=True) + rec[:, R_P0:R_P0 + 1]
        p1 = jnp.sum(oh1 * start_row[0:1, :], axis=-1, keepdims=True) + rec[:, R_P1:R_P1 + 1]
        out = jnp.where(lane == R_P0, p0, rec)
        out = jnp.where(lane == R_P1, p1, out)
        o_ref[...] = out

        @pl.when(j == 0)
        def _():
            meta_ref[...] = end_blk


def _route_plan(logits):
    rows = logits.shape[0]
    tm = ROUTE_T
    ltri = jnp.asarray(np.tril(np.ones((tm, tm), np.float32), -1), BF16)
    utri = jnp.asarray(np.triu(np.ones((LANES, LANES), np.float32)), BF16)
    mrows = SUBLANES
    return pl.pallas_call(
        _route_plan_body,
        grid=(2, rows // tm),
        in_specs=[pl.BlockSpec((tm, LANES), lambda p, j: (j * (1 - p), 0)),
                  pl.BlockSpec((tm, tm), lambda p, j: (0, 0)),
                  pl.BlockSpec((LANES, LANES), lambda p, j: (0, 0))],
        out_specs=[pl.BlockSpec((tm, LANES), lambda p, j: (j * p, 0)),
                   pl.BlockSpec((mrows, LANES), lambda p, j: (0, 0))],
        out_shape=[jax.ShapeDtypeStruct((rows, LANES), F32), jax.ShapeDtypeStruct((mrows, LANES), F32)],
        scratch_shapes=[pltpu.VMEM((rows, LANES), F32), pltpu.VMEM((SUBLANES, LANES), F32)],
        compiler_params=_cparams(("arbitrary", "arbitrary")),
        name="route_plan",
    )(logits, ltri, utri)


def _rowmap_body(pos_ref, o_ref, *, ntok, nrows):
    unroll = 8

    def zero(c, _):
        for u in range(unroll):
            o_ref[c * unroll + u] = 0
        return 0
    lax.fori_loop(0, nrows // unroll, zero, 0)

    def fill(c, _):
        for u in range(unroll):
            t = c * unroll + u
            o_ref[pos_ref[2 * t]] = t
            o_ref[pos_ref[2 * t + 1]] = t
        return 0
    lax.fori_loop(0, ntok // unroll, fill, 0)


def _rowmap(pos, nrows):
    ntok = pos.shape[0] // 2
    return pl.pallas_call(
        functools.partial(_rowmap_body, ntok=ntok, nrows=nrows),
        in_specs=[pl.BlockSpec(memory_space=pltpu.SMEM)],
        out_specs=pl.BlockSpec(memory_space=pltpu.SMEM),
        out_shape=jax.ShapeDtypeStruct((nrows,), jnp.int32),
        name="moe_rowmap",
    )(pos)


GATHER_SLOTS = 4


def _moe_body(bend_ref, row_tok_ref, x_hbm, wg_ref, wu_ref, wd_ref, ys_hbm, *scratch, nblocks):
    xbufs = scratch[:GATHER_SLOTS]
    ybufs = scratch[GATHER_SLOTS:GATHER_SLOTS + 2]
    gsem, osem, wg_bf, wu_bf, wd_bf = scratch[GATHER_SLOTS + 2:]
    ahead = GATHER_SLOTS - 1
    e = pl.program_id(0)
    nused = bend_ref[N_EXPERTS - 1]
    b1 = bend_ref[e]
    b0 = jnp.where(e == 0, 0, bend_ref[jnp.maximum(e - 1, 0)])

    def gather_start(blk, s):
        base = blk * MOE_BLOCK
        for r in range(MOE_BLOCK):
            tok = row_tok_ref[base + r]
            pltpu.make_async_copy(x_hbm.at[pl.ds(tok, 1), :], xbufs[s].at[pl.ds(r, 1), :],
                                  gsem.at[s]).start(priority=GATHER_PRIORITY)

    def gather_wait(s):
        pltpu.make_async_copy(x_hbm.at[pl.ds(0, MOE_BLOCK), :], xbufs[s], gsem.at[s]).wait()

    def out_copy(blk, s):
        row0 = pl.multiple_of(blk * MOE_BLOCK, MOE_BLOCK)
        return pltpu.make_async_copy(ybufs[s], ys_hbm.at[pl.ds(row0, MOE_BLOCK), :], osem.at[s])

    @pl.when(e == 0)
    def _():
        for k in range(ahead):
            gather_start(k, k)

    @pl.when(b1 > b0)
    def _():
        wg_bf[...] = wg_ref[...].astype(BF16)
        wu_bf[...] = wu_ref[...].astype(BF16)
        wd_bf[...] = wd_ref[...].astype(BF16)

        def do_block(b, s):
            so = s % 2
            gather_wait(s)

            @pl.when(b >= 2)
            def _():
                out_copy(b - 2, so).wait()

            gather_start(jnp.minimum(b + ahead, nused - 1), (s + ahead) % GATHER_SLOTS)
            x = xbufs[s][...].astype(BF16)
            hg = jnp.dot(x, wg_bf[...], preferred_element_type=F32)
            hu = jnp.dot(x, wu_bf[...], preferred_element_type=F32)
            h = (jax.nn.silu(hg) * hu).astype(BF16)
            ybufs[so][...] = jnp.dot(h, wd_bf[...], preferred_element_type=F32)
            out_copy(b, so).start()

        def body(b, _):
            for s in range(GATHER_SLOTS):
                @pl.when(b % GATHER_SLOTS == s)
                def _():
                    do_block(b, s)
            return 0

        lax.fori_loop(b0, b1, body, 0)

    @pl.when(e == N_EXPERTS - 1)
    def _():
        last = nused - 1
        for s in range(GATHER_SLOTS):
            @pl.when(last % GATHER_SLOTS == s)
            def _():
                for k in range(1, GATHER_SLOTS):
                    gather_wait((s + k) % GATHER_SLOTS)
        out_copy(0, 0).wait()
        out_copy(0, 1).wait()
        yb0 = ybufs[0]
        yb0[...] = jnp.zeros(yb0.shape, F32)

        def zstart(b, _):
            out_copy(b, 0).start()
            return 0

        def zwait(b, _):
            out_copy(b, 0).wait()
            return 0

        lax.fori_loop(nused, nblocks, zstart, 0)
        lax.fori_loop(nused, nblocks, zwait, 0)


def _moe(bend, row_tok, x1, w_gate, w_up, w_down, layer):
    nblocks = row_tok.shape[0] // MOE_BLOCK
    assert 2 * x1.shape[0] > GATHER_SLOTS * MOE_BLOCK
    wspec = lambda a, b: pl.BlockSpec((None, None, a, b), lambda e, be, rt: (layer, e, 0, 0))
    grid_spec = pltpu.PrefetchScalarGridSpec(
        num_scalar_prefetch=2,
        grid=(N_EXPERTS,),
        in_specs=[pl.BlockSpec(memory_space=pl.ANY), wspec(D_MODEL, EXPERT_FF), wspec(D_MODEL, EXPERT_FF),
                  wspec(EXPERT_FF, D_MODEL)],
        out_specs=pl.BlockSpec(memory_space=pl.ANY),
        scratch_shapes=([pltpu.VMEM((MOE_BLOCK, D_MODEL), F32) for _ in range(GATHER_SLOTS + 2)]
                        + [pltpu.SemaphoreType.DMA((GATHER_SLOTS,)), pltpu.SemaphoreType.DMA((2,)),
                           pltpu.VMEM((D_MODEL, EXPERT_FF), BF16),
                           pltpu.VMEM((D_MODEL, EXPERT_FF), BF16),
                           pltpu.VMEM((EXPERT_FF, D_MODEL), BF16)]),
    )
    return pl.pallas_call(
        functools.partial(_moe_body, nblocks=nblocks),
        grid_spec=grid_spec,
        out_shape=jax.ShapeDtypeStruct((nblocks * MOE_BLOCK, D_MODEL), F32),
        compiler_params=_cparams(("arbitrary",), bounds_checks=False),
        name="moe_experts",
    )(bend, row_tok, x1, w_gate, w_up, w_down)


COMB_T = 128


def _combine_body(pos_ref, ys_hbm, x_ref, rt_ref, g_ref, b_ref, op_ref, os_ref, *scratch, n_prompt_blocks):
    i = pl.program_id(0)
    nsteps = pl.num_programs(0)
    bufs = tuple((scratch[2 * s], scratch[2 * s + 1]) for s in range(GATHER_SLOTS))
    sem = scratch[2 * GATHER_SLOTS]
    ahead = GATHER_SLOTS - 1

    def gather_start(step, s):
        base = step * (2 * COMB_T)
        for r in range(COMB_T):
            for k in range(2):
                p = pos_ref[base + 2 * r + k]
                pltpu.make_async_copy(ys_hbm.at[pl.ds(p, 1), :], bufs[s][k].at[pl.ds(r, 1), :],
                                      sem.at[s]).start(priority=k)

    def gather_wait(s):
        for k in range(2):
            pltpu.make_async_copy(ys_hbm.at[pl.ds(0, COMB_T), :], bufs[s][k], sem.at[s]).wait()

    @pl.when(i == 0)
    def _():
        for k in range(ahead):
            gather_start(k, k)

    def do_tile(s):
        gather_wait(s)
        gather_start(jnp.minimum(i + ahead, nsteps - 1), (s + ahead) % GATHER_SLOTS)
        rt = rt_ref[...]
        ff = rt[:, R_W0:R_W0 + 1] * bufs[s][0][...] + rt[:, R_W1:R_W1 + 1] * bufs[s][1][...]
        out = _layer_norm(DEEPNORM_ALPHA * x_ref[...] + ff, g_ref[...], b_ref[...])

        @pl.when(i < n_prompt_blocks)
        def _():
            op_ref[...] = out

        @pl.when(i >= n_prompt_blocks)
        def _():
            os_ref[...] = out

        @pl.when(i == nsteps - 1)
        def _():
            for k in range(1, GATHER_SLOTS):
                gather_wait((s + k) % GATHER_SLOTS)

    for s in range(GATHER_SLOTS):
        @pl.when(i % GATHER_SLOTS == s)
        def _():
            do_tile(s)


def _combine(pos, ys, x1, route, g, b, n_prompt_rows):
    rows = x1.shape[0]
    npb = n_prompt_rows // COMB_T
    assert rows // COMB_T >= GATHER_SLOTS
    grid_spec = pltpu.PrefetchScalarGridSpec(
        num_scalar_prefetch=1,
        grid=(rows // COMB_T,),
        in_specs=[pl.BlockSpec(memory_space=pl.ANY),
                  pl.BlockSpec((COMB_T, D_MODEL), lambda i, p: (i, 0)),
                  pl.BlockSpec((COMB_T, LANES), lambda i, p: (i, 0)),
                  pl.BlockSpec((1, D_MODEL), lambda i, p: (0, 0)),
                  pl.BlockSpec((1, D_MODEL), lambda i, p: (0, 0))],
        out_specs=[pl.BlockSpec((COMB_T, D_MODEL), lambda i, p: (jnp.minimum(i, npb - 1), 0)),
                   pl.BlockSpec((COMB_T, D_MODEL), lambda i, p: (jnp.maximum(i - npb, 0), 0))],
        scratch_shapes=([pltpu.VMEM((COMB_T, D_MODEL), F32) for _ in range(2 * GATHER_SLOTS)]
                        + [pltpu.SemaphoreType.DMA((GATHER_SLOTS,))]),
    )
    return pl.pallas_call(
        functools.partial(_combine_body, n_prompt_blocks=npb),
        grid_spec=grid_spec,
        out_shape=[jax.ShapeDtypeStruct((n_prompt_rows, D_MODEL), F32),
                   jax.ShapeDtypeStruct((rows - n_prompt_rows, D_MODEL), F32)],
        compiler_params=_cparams(("arbitrary",), bounds_checks=False),
        name="combine_ln2",
    )(pos, ys, x1, route, g, b)


def kernel(x_prompt, x_sample, cache_k, cache_v, state_ssm_re, state_ssm_im, w_in, ssm_lambda_re, ssm_lambda_im, ssm_log_dt, ssm_b_re, ssm_b_im, ssm_c_re, ssm_c_im, ssm_d, ssm_w_glu, attn_sinks, attn_norm_g, ssm_norm_g, w_out, ln1_g, ln1_b, router_group_w, router_group_b, router_expert_w, router_expert_b, expert_w_gate, expert_w_up, expert_w_down, ln2_g, ln2_b):
    nbatch, seq, _ = x_prompt.shape
    nseq, dec_seq, _ = x_sample.shape
    wbuf = cache_k.shape[2]
    rows_p = nbatch * seq
    rows_s = nseq * dec_seq
    rows_all = rows_p + rows_s
    tm_p = 512

    xp = x_prompt.reshape(rows_p, D_MODEL)
    xs = x_sample.reshape(rows_s, D_MODEL)
    tab_p = _rope_tables(np.arange(seq))
    tab_s = _rope_tables(PAST_LEN + (np.arange(rows_s) % dec_seq))
    ck_all = cache_k.astype(F32).reshape(DEPTH, nseq, wbuf, KV_WIDTH)
    cv_all = cache_v.astype(F32).reshape(DEPTH, nseq, wbuf, KV_WIDTH)

    w_in_bf = w_in.astype(BF16)
    w_glu_bf = ssm_w_glu.astype(BF16)
    w_out_bf = w_out.astype(BF16)

    outs = {k: [] for k in ("kp", "vp", "hrp", "hip", "ks", "vs", "hrs", "his")}
    for l in range(DEPTH):
        sp = _ssm_params(ssm_lambda_re[l], ssm_lambda_im[l], ssm_log_dt[l], ssm_b_re[l], ssm_b_im[l],
                         ssm_c_re[l], ssm_c_im[l], ssm_d[l])
        sinks = attn_sinks[l].astype(F32)
        attn_g = attn_norm_g[l].astype(F32).reshape(1, ATTN_WIDTH)
        ssm_g = ssm_norm_g[l].astype(F32).reshape(1, SSM_WIDTH)
        g1 = ln1_g[l].astype(F32).reshape(1, D_MODEL)
        b1 = ln1_b[l].astype(F32).reshape(1, D_MODEL)
        g2 = ln2_g[l].astype(F32).reshape(1, D_MODEL)
        b2 = ln2_b[l].astype(F32).reshape(1, D_MODEL)
        n_rt = N_EXPERT_GROUPS + N_EXPERTS
        wr = jnp.concatenate([router_group_w[l].astype(F32)]
                             + [router_expert_w[l, g].astype(F32) for g in range(N_EXPERT_GROUPS)]
                             + [jnp.zeros((D_MODEL, LANES - n_rt), F32)], axis=1)
        br = jnp.concatenate([router_group_b[l].astype(F32), router_expert_b[l].astype(F32).reshape(-1),
                              jnp.zeros((LANES - n_rt,), F32)]).reshape(1, LANES)

        q_p, k_p, v_p, u_p = _inproj(xp, w_in_bf, l, *tab_p, tm=tm_p)
        q_s, k_s, v_s, u_s = _inproj(xs, w_in_bf, l, *tab_s, tm=rows_s)

        an_p = _attn_prompt(sinks, q_p, k_p, v_p, attn_g, seq)
        an_s, nk_s, nv_s = _attn_sample(sinks, q_s, k_s, v_s, ck_all, cv_all, attn_g, dec_seq, l)

        y_p, hl_p = _ssm_prompt(u_p, sp, nbatch, seq)
        y_s, hl_s = _ssm_sample(u_s, _pack_state(state_ssm_re[l], state_ssm_im[l]), sp, nseq, dec_seq)
        sn_p = _glu(y_p, w_glu_bf, l, ssm_g, tm_p)
        sn_s = _glu(y_s, w_glu_bf, l, ssm_g, rows_s)

        x1, lg_p = _outproj(an_p, sn_p, xp, w_out_bf, l, g1, b1, wr, br, tm_p, rows_all, 0)
        x1, lg_s = _outproj(an_s, sn_s, xs, w_out_bf, l, g1, b1, wr, br, rows_s, rows_all, rows_p // rows_s,
                            x1_buf=x1)
        nblocks = 2 * rows_all // MOE_BLOCK + N_EXPERTS
        route, meta = _route_plan(jnp.concatenate([lg_p, lg_s], axis=0))
        pos = route[:, R_P0:R_P1 + 1].astype(jnp.int32).reshape(2 * rows_all)
        bend = meta[0, :N_EXPERTS].astype(jnp.int32)
        row_tok = _rowmap(pos, nblocks * MOE_BLOCK)

        ys = _moe(bend, row_tok, x1, expert_w_gate, expert_w_up, expert_w_down, l)
        xp, xs = _combine(pos, ys, x1, route, g2, b2, rows_p)

        wp = min(WINDOW, seq)
        tail = lambda a: a.reshape(nbatch, seq, KV_WIDTH)[:, seq - wp:].reshape(nbatch, wp, N_KV_HEADS, HEAD_DIM)
        outs["kp"].append(tail(k_p))
        outs["vp"].append(tail(v_p))
        hp = hl_p.reshape(nbatch, SSM_CHUNKS * 2 * CH_STATES)
        hr, hi = _unpack_state(hp, nbatch)
        outs["hrp"].append(hr)
        outs["hip"].append(hi)
        outs["ks"].append(nk_s.reshape(nseq, wbuf, N_KV_HEADS, HEAD_DIM))
        outs["vs"].append(nv_s.reshape(nseq, wbuf, N_KV_HEADS, HEAD_DIM))
        hr, hi = _unpack_state(hl_s, nseq)
        outs["hrs"].append(hr)
        outs["his"].append(hi)

    st = {k: jnp.stack(v) for k, v in outs.items()}
    return (xp.reshape(nbatch, seq, D_MODEL), xs.reshape(nseq, dec_seq, D_MODEL),
            st["kp"], st["vp"], st["hrp"], st["hip"], st["ks"], st["vs"], st["hrs"], st["his"])
```

```python
import functools
import math

import numpy as np
import jax
import jax.numpy as jnp
from jax import lax
from jax.experimental import pallas as pl
from jax.experimental.pallas import tpu as pltpu

F32 = jnp.float32
BF16 = jnp.bfloat16

D_MODEL = 2048
DEPTH = 2
PAST_LEN = 16384
ATTN_WIDTH = 1024
SSM_WIDTH = 1024
HEAD_DIM = 64
N_HEADS = 16
N_KV_HEADS = 4
Q_PER_KV = 4
KV_WIDTH = 256
WINDOW = 128
ROPE_THETA = 10000.0
SSM_GROUP = 16
SSM_GROUPS = 64
SSM_STATE = 64
N_EXPERT_GROUPS = 4
EXPERTS_PER_GROUP = 8
N_EXPERTS = 32
EXPERT_FF = 512
MOE_BLOCK = 128
DEEPNORM_ALPHA = (2.0 * DEPTH) ** 0.25
LN_EPS = 1e-5
RMS_EPS = 1e-6

LANES = 128
SUBLANES = 8
VMEM_LIMIT = 56 * 1024 * 1024

SSM_CHUNKS = 4
CH_GROUPS = SSM_GROUPS // SSM_CHUNKS
CH_STATES = CH_GROUPS * SSM_STATE
CH_IN = CH_GROUPS * SSM_GROUP
SCAN_T = 512
SCAN_SEG = SCAN_T // SUBLANES


def _cparams(sem=None, vmem=VMEM_LIMIT, bounds_checks=True):
    return pltpu.CompilerParams(dimension_semantics=sem, vmem_limit_bytes=vmem,
                                disable_bounds_checks=not bounds_checks)


def _full(shape):
    n = len(shape)
    return pl.BlockSpec(shape, lambda *a: (0,) * n)


GATHER_PRIORITY = 1


def _inproj_body(x_ref, w_ref, cos_ref, sa_ref, sb_ref, q_ref, k_ref, v_ref, u_ref):
    x = x_ref[...].astype(BF16)
    cos = cos_ref[...]
    sa = sa_ref[...]
    sb = sb_ref[...]

    def rope(c):
        return c * cos + pltpu.roll(c, LANES - 32, 1) * sa + pltpu.roll(c, 32, 1) * sb

    pq = jnp.dot(x, w_ref[:, 0:ATTN_WIDTH], preferred_element_type=F32)
    for j in range(ATTN_WIDTH // LANES):
        q_ref[:, j * LANES:(j + 1) * LANES] = rope(pq[:, j * LANES:(j + 1) * LANES]).astype(BF16)
    pk = jnp.dot(x, w_ref[:, ATTN_WIDTH:ATTN_WIDTH + KV_WIDTH], preferred_element_type=F32)
    for j in range(KV_WIDTH // LANES):
        k_ref[:, j * LANES:(j + 1) * LANES] = rope(pk[:, j * LANES:(j + 1) * LANES])
    v_ref[...] = jnp.dot(x, w_ref[:, ATTN_WIDTH + KV_WIDTH:ATTN_WIDTH + 2 * KV_WIDTH],
                         preferred_element_type=F32)
    u_ref[...] = jnp.dot(x, w_ref[:, ATTN_WIDTH + 2 * KV_WIDTH:], preferred_element_type=F32)


def _layer_spec(shape, layer):
    return pl.BlockSpec((None,) + tuple(shape), lambda *a: (layer, 0, 0))


def _inproj(x, w_bf, layer, cos_t, sa_t, sb_t, tm):
    rows = x.shape[0]
    tab_blocks = cos_t.shape[0] // tm
    in_w = w_bf.shape[2]
    row_spec = lambda w: pl.BlockSpec((tm, w), lambda i: (i, 0))
    tab_spec = pl.BlockSpec((tm, LANES), lambda i: (i % tab_blocks, 0))
    return pl.pallas_call(
        _inproj_body,
        grid=(rows // tm,),
        in_specs=[row_spec(D_MODEL), _layer_spec((D_MODEL, in_w), layer), tab_spec, tab_spec, tab_spec],
        out_specs=[row_spec(ATTN_WIDTH), row_spec(KV_WIDTH), row_spec(KV_WIDTH), row_spec(SSM_WIDTH)],
        out_shape=[jax.ShapeDtypeStruct((rows, ATTN_WIDTH), BF16),
                   jax.ShapeDtypeStruct((rows, KV_WIDTH), F32),
                   jax.ShapeDtypeStruct((rows, KV_WIDTH), F32),
                   jax.ShapeDtypeStruct((rows, SSM_WIDTH), F32)],
        compiler_params=_cparams(("arbitrary",)),
        name="inproj_rope",
    )(x, w_bf, cos_t, sa_t, sb_t)


def _rope_tables(positions):
    half = HEAD_DIM // 2
    inv_freq = ROPE_THETA ** (-np.arange(half, dtype=np.float64) / half)
    ang = positions.astype(np.float64)[:, None] * inv_freq[None, :]
    lane = np.arange(LANES)
    cos = np.cos(ang)[:, lane % half]
    sin = np.sin(ang)[:, lane % half]
    first = (lane % HEAD_DIM) < half
    sa = np.where(first[None, :], -sin, 0.0)
    sb = np.where(first[None, :], 0.0, sin)
    return (jnp.asarray(cos, F32), jnp.asarray(sa, F32), jnp.asarray(sb, F32))


def _rms(x, g):
    return x * lax.rsqrt(jnp.mean(jnp.square(x), axis=-1, keepdims=True) + RMS_EPS) * g


def _attn_prompt_body(sink_ref, q_ref, kp_ref, kc_ref, vp_ref, vc_ref, g_ref, o_ref, acc_ref, *, blocks_per_seq):
    i = pl.program_id(0)
    has_prev = (i % blocks_per_seq) > 0
    kk = jnp.concatenate([kp_ref[...], kc_ref[...]], axis=0).astype(BF16)
    vt = jnp.concatenate([vp_ref[...], vc_ref[...]], axis=0).T.astype(BF16)
    ncol = Q_PER_KV * WINDOW
    key = lax.broadcasted_iota(jnp.int32, (2 * WINDOW, ncol), 0)
    col = lax.broadcasted_iota(jnp.int32, (2 * WINDOW, ncol), 1)
    qry = col % WINDOW
    visible = (key > qry) & (key <= qry + WINDOW) & ((key >= WINDOW) | has_prev)
    head = lax.broadcasted_iota(jnp.int32, (1, ncol), 1) // WINDOW
    q = q_ref[...] * (HEAD_DIM ** -0.5)
    scores = []
    for g in range(N_KV_HEADS):
        qg = jnp.concatenate([q[:, (g * Q_PER_KV + j) * HEAD_DIM:(g * Q_PER_KV + j + 1) * HEAD_DIM]
                              for j in range(Q_PER_KV)], axis=0)
        kh = kk[:, g * HEAD_DIM:(g + 1) * HEAD_DIM]
        scores.append(lax.dot_general(kh, qg, (((1,), (1,)), ((), ())), preferred_element_type=F32))
    for g in range(N_KV_HEADS):
        st = jnp.where(visible, scores[g], -jnp.inf)
        sink = jnp.zeros((1, ncol), F32)
        for j in range(Q_PER_KV):
            sink = jnp.where(head == j, sink_ref[g * Q_PER_KV + j], sink)
        m = jnp.maximum(jnp.max(st, axis=0, keepdims=True), sink)
        p = jnp.exp(st - m)
        denom = jnp.sum(p, axis=0, keepdims=True) + jnp.exp(sink - m)
        ot = jnp.dot(vt[g * HEAD_DIM:(g + 1) * HEAD_DIM, :], p.astype(BF16), preferred_element_type=F32)
        ot = ot / denom
        for j in range(Q_PER_KV):
            h = g * Q_PER_KV + j
            acc_ref[h * HEAD_DIM:(h + 1) * HEAD_DIM, :] = ot[:, j * WINDOW:(j + 1) * WINDOW]
    o_ref[...] = _rms(acc_ref[...].T, g_ref[...]).astype(BF16)


def _attn_prompt(sinks, q, k, v, g, seq):
    rows = q.shape[0]
    nblk = rows // WINDOW
    bps = seq // WINDOW
    cur = lambda w: pl.BlockSpec((WINDOW, w), lambda i: (i, 0))
    prev = lambda w: pl.BlockSpec((WINDOW, w), lambda i: (jnp.maximum(i - 1, 0), 0))
    return pl.pallas_call(
        functools.partial(_attn_prompt_body, blocks_per_seq=bps),
        grid=(nblk,),
        in_specs=[pl.BlockSpec(memory_space=pltpu.SMEM), cur(ATTN_WIDTH), prev(KV_WIDTH), cur(KV_WIDTH),
                  prev(KV_WIDTH), cur(KV_WIDTH), _full((1, ATTN_WIDTH))],
        out_specs=cur(ATTN_WIDTH),
        out_shape=jax.ShapeDtypeStruct((rows, ATTN_WIDTH), BF16),
        scratch_shapes=[pltpu.VMEM((ATTN_WIDTH, WINDOW), F32)],
        compiler_params=_cparams(("arbitrary",)),
        name="attn_prompt",
    )(sinks, q, k, k, v, v, g)


SEQ_PER_STEP = 8


def _attn_sample_body(sink_ref, q_ref, kn_ref, vn_ref, ck_ref, cv_ref, g_ref, o_ref, nk_ref, nv_ref, acc_ref,
                      *, dec_seq):
    nq = SEQ_PER_STEP * dec_seq
    wb = ck_ref.shape[1]
    nkc = SEQ_PER_STEP * wb
    ck = ck_ref[...].reshape(nkc, KV_WIDTH)
    cv = cv_ref[...].reshape(nkc, KV_WIDTH)
    kn = kn_ref[...]
    vn = vn_ref[...]
    ckb = ck.astype(BF16)
    cvb = cv.astype(BF16)
    knb = kn.astype(BF16)
    vnb = vn.astype(BF16)
    q = q_ref[...]

    rows = Q_PER_KV * nq
    r = lax.broadcasted_iota(jnp.int32, (rows, nkc), 0) % nq
    c = lax.broadcasted_iota(jnp.int32, (rows, nkc), 1)
    vis_c = ((r // dec_seq) == (c // wb)) & ((c % wb) >= (r % dec_seq) + 1 + (wb - WINDOW))
    rn = lax.broadcasted_iota(jnp.int32, (rows, nq), 0) % nq
    cn = lax.broadcasted_iota(jnp.int32, (rows, nq), 1)
    vis_n = ((rn // dec_seq) == (cn // dec_seq)) & ((cn % dec_seq) <= (rn % dec_seq))
    hrow = lax.broadcasted_iota(jnp.int32, (rows, 1), 0) // nq

    dn = (((1,), (1,)), ((), ()))
    for g in range(N_KV_HEADS):
        qs = jnp.concatenate([q[:, (g * Q_PER_KV + j) * HEAD_DIM:(g * Q_PER_KV + j + 1) * HEAD_DIM]
                              for j in range(Q_PER_KV)], axis=0)
        sl = slice(g * HEAD_DIM, (g + 1) * HEAD_DIM)
        sc = lax.dot_general(qs, ckb[:, sl], dn, preferred_element_type=F32) * (HEAD_DIM ** -0.5)
        sn = lax.dot_general(qs, knb[:, sl], dn, preferred_element_type=F32) * (HEAD_DIM ** -0.5)
        sc = jnp.where(vis_c, sc, -jnp.inf)
        sn = jnp.where(vis_n, sn, -jnp.inf)
        sink = jnp.zeros((rows, 1), F32)
        for j in range(Q_PER_KV):
            sink = jnp.where(hrow == j, sink_ref[g * Q_PER_KV + j], sink)
        m = jnp.maximum(jnp.maximum(jnp.max(sc, axis=-1, keepdims=True), jnp.max(sn, axis=-1, keepdims=True)), sink)
        pc = jnp.exp(sc - m)
        pn = jnp.exp(sn - m)
        denom = jnp.sum(pc, axis=-1, keepdims=True) + jnp.sum(pn, axis=-1, keepdims=True) + jnp.exp(sink - m)
        o = (jnp.dot(pc.astype(BF16), cvb[:, sl], preferred_element_type=F32)
             + jnp.dot(pn.astype(BF16), vnb[:, sl], preferred_element_type=F32)) / denom
        for j in range(Q_PER_KV):
            h = g * Q_PER_KV + j
            acc_ref[:, h * HEAD_DIM:(h + 1) * HEAD_DIM] = o[j * nq:(j + 1) * nq, :]
    o_ref[...] = _rms(acc_ref[...], g_ref[...]).astype(BF16)

    keep = wb - dec_seq
    for s in range(SEQ_PER_STEP):
        nk_ref[s, 0:keep, :] = ck_ref[s, dec_seq:wb, :]
        nv_ref[s, 0:keep, :] = cv_ref[s, dec_seq:wb, :]
        nk_ref[s, keep:wb, :] = kn[s * dec_seq:(s + 1) * dec_seq, :]
        nv_ref[s, keep:wb, :] = vn[s * dec_seq:(s + 1) * dec_seq, :]


def _attn_sample(sinks, q, kn, vn, ck, cv, g, dec_seq, layer):
    rows = q.shape[0]
    _, nseq, wb, _ = ck.shape
    nq = SEQ_PER_STEP * dec_seq
    rowb = lambda w: pl.BlockSpec((nq, w), lambda i: (i, 0))
    cin = pl.BlockSpec((None, SEQ_PER_STEP, wb, KV_WIDTH), lambda i: (layer, i, 0, 0))
    cb = pl.BlockSpec((SEQ_PER_STEP, wb, KV_WIDTH), lambda i: (i, 0, 0))
    return pl.pallas_call(
        functools.partial(_attn_sample_body, dec_seq=dec_seq),
        grid=(nseq // SEQ_PER_STEP,),
        in_specs=[pl.BlockSpec(memory_space=pltpu.SMEM), rowb(ATTN_WIDTH), rowb(KV_WIDTH), rowb(KV_WIDTH),
                  cin, cin, _full((1, ATTN_WIDTH))],
        out_specs=[rowb(ATTN_WIDTH), cb, cb],
        out_shape=[jax.ShapeDtypeStruct((rows, ATTN_WIDTH), BF16),
                   jax.ShapeDtypeStruct(ck.shape[1:], F32), jax.ShapeDtypeStruct(cv.shape[1:], F32)],
        scratch_shapes=[pltpu.VMEM((nq, ATTN_WIDTH), F32)],
        compiler_params=_cparams(("arbitrary",)),
        name="attn_sample",
    )(sinks, q, kn, vn, ck, cv, g)


def _ssm_prompt_body(ua_ref, uc_ref, b_ref, c_ref, lr_ref, li_ref, pr_ref, pi_ref, d_ref, y_ref, hl_ref,
                     uperm_ref, bu0, bu1, hb0, hb1, yp_ref, carry_ref, *, nchunks, ntc):
    k = pl.program_id(0)
    bus = (bu0, bu1)
    hbs = (hb0, hb1)
    lanes_per_chunk = CH_IN // LANES
    seg = lambda s: slice(s * SCAN_SEG, (s + 1) * SCAN_SEG)
    lane = lambda c: slice(c * LANES, (c + 1) * LANES)

    def stage_a(item, slot):
        cc = item % SSM_CHUNKS
        for c in range(lanes_per_chunk):
            for s in range(SUBLANES):
                uperm_ref[c, pl.ds(s, SCAN_SEG, stride=SUBLANES), :] = ua_ref[seg(s), lane(c)]
        up = jnp.concatenate([uperm_ref[c] for c in range(lanes_per_chunk)], axis=1)
        bus[slot][...] = jnp.dot(up.astype(BF16), b_ref[cc], preferred_element_type=F32)

    def stage_b(item, slot):
        cc = item % SSM_CHUNKS
        tc = (item // SSM_CHUNKS) % ntc
        bu_ref = bus[slot]
        hb_ref = hbs[slot]
        ar = jnp.broadcast_to(lr_ref[cc], (SUBLANES, CH_STATES))
        ai = jnp.broadcast_to(li_ref[cc], (SUBLANES, CH_STATES))

        def advance(h, j):
            hr, hi = h
            br = bu_ref[j * SUBLANES:(j + 1) * SUBLANES, 0:CH_STATES]
            bi = bu_ref[j * SUBLANES:(j + 1) * SUBLANES, CH_STATES:2 * CH_STATES]
            return ar * hr + (br - ai * hi), ar * hi + (bi + ai * hr)

        zero = jnp.zeros((SUBLANES, CH_STATES), F32)
        h = (zero, zero)
        for j in range(SCAN_SEG):
            h = advance(h, j)
        fr, fi = h

        cst = jnp.where(tc == 0, 0.0, carry_ref[cc])
        c_r = cst[0:1, :]
        c_i = cst[1:2, :]
        lsr = pr_ref[cc]
        lsi = pi_ref[cc]
        rows_r, rows_i = [], []
        for s in range(SUBLANES):
            rows_r.append(c_r)
            rows_i.append(c_i)
            n_r = fr[s:s + 1, :] + (lsr * c_r - lsi * c_i)
            n_i = fi[s:s + 1, :] + (lsr * c_i + lsi * c_r)
            c_r, c_i = n_r, n_i
        carry_ref[cc] = jnp.concatenate([c_r, c_i], axis=0)
        hl_ref[0, cc] = jnp.concatenate([c_r, c_i], axis=0)

        h = (jnp.concatenate(rows_r, axis=0), jnp.concatenate(rows_i, axis=0))
        for jj in range(SCAN_SEG // 2):
            h1 = advance(h, 2 * jj)
            h = advance(h1, 2 * jj + 1)
            rows = slice(2 * jj * SUBLANES, (2 * jj + 2) * SUBLANES)
            hb_ref[rows, 0:CH_STATES] = jnp.concatenate([h1[0], h[0]], axis=0).astype(BF16)
            hb_ref[rows, CH_STATES:2 * CH_STATES] = jnp.concatenate([h1[1], h[1]], axis=0).astype(BF16)

    def stage_c(item, slot):
        cc = item % SSM_CHUNKS
        yp = jnp.dot(hbs[slot][...], c_ref[cc], preferred_element_type=F32)
        d = d_ref[cc]
        for c in range(lanes_per_chunk):
            yp_ref[c] = yp[:, lane(c)]
        for s in range(SUBLANES):
            for c in range(lanes_per_chunk):
                y_ref[seg(s), lane(c)] = (yp_ref[c, pl.ds(s, SCAN_SEG, stride=SUBLANES), :]
                                          + d[:, lane(c)] * uc_ref[seg(s), lane(c)])

    def run(do_a, do_b, do_c, parity):
        if do_a:
            stage_a(k, parity)
        if do_b:
            stage_b(k - 1, 1 - parity)
        if do_c:
            stage_c(k - 2, parity)

    @pl.when(k == 0)
    def _():
        carry_ref[...] = jnp.zeros(carry_ref.shape, F32)
        run(True, False, False, 0)

    @pl.when(k == 1)
    def _():
        run(True, True, False, 1)

    for parity in range(2):
        @pl.when((k >= 2) & (k < nchunks) & (k % 2 == parity))
        def _():
            run(True, True, True, parity)

    @pl.when(k == nchunks)
    def _():
        run(False, True, True, nchunks % 2)

    @pl.when(k == nchunks + 1)
    def _():
        run(False, False, True, (nchunks + 1) % 2)


def _ssm_prompt(u, sp, nbatch, seq):
    rows = u.shape[0]
    ntc = seq // SCAN_T
    nchunks = nbatch * ntc * SSM_CHUNKS
    assert nchunks >= 2
    item_a = lambda k: jnp.minimum(k, nchunks - 1)
    item_b = lambda k: jnp.clip(k - 1, 0, nchunks - 1)
    item_c = lambda k: jnp.clip(k - 2, 0, nchunks - 1)
    chunk_block = lambda item: (item // SSM_CHUNKS, item % SSM_CHUNKS)
    y, hl = pl.pallas_call(
        functools.partial(_ssm_prompt_body, nchunks=nchunks, ntc=ntc),
        grid=(nchunks + 2,),
        in_specs=[pl.BlockSpec((SCAN_T, CH_IN), lambda k: chunk_block(item_a(k))),
                  pl.BlockSpec((SCAN_T, CH_IN), lambda k: chunk_block(item_c(k))),
                  _full(sp["B"].shape), _full(sp["C"].shape), _full(sp["lr"].shape), _full(sp["li"].shape),
                  _full(sp["pr"].shape), _full(sp["pi"].shape), _full(sp["d"].shape)],
        out_specs=[pl.BlockSpec((SCAN_T, CH_IN), lambda k: chunk_block(item_c(k))),
                   pl.BlockSpec((1, SSM_CHUNKS, 2, CH_STATES),
                                lambda k: (item_b(k) // (ntc * SSM_CHUNKS), 0, 0, 0))],
        out_shape=[jax.ShapeDtypeStruct((rows, SSM_WIDTH), F32),
                   jax.ShapeDtypeStruct((nbatch, SSM_CHUNKS, 2, CH_STATES), F32)],
        scratch_shapes=[pltpu.VMEM((CH_IN // LANES, SCAN_T, LANES), F32),
                        pltpu.VMEM((SCAN_T, 2 * CH_STATES), F32), pltpu.VMEM((SCAN_T, 2 * CH_STATES), F32),
                        pltpu.VMEM((SCAN_T, 2 * CH_STATES), BF16), pltpu.VMEM((SCAN_T, 2 * CH_STATES), BF16),
                        pltpu.VMEM((CH_IN // LANES, SCAN_T, LANES), F32),
                        pltpu.VMEM((SSM_CHUNKS, 2, CH_STATES), F32)],
        compiler_params=_cparams(("arbitrary",)),
        name="ssm_prompt",
    )(u, u, sp["B"], sp["C"], sp["lr"], sp["li"], sp["pr"], sp["pi"], sp["d"])
    return y, hl


def _ssm_sample_body(u_ref, h0_ref, b_ref, c_ref, lr_ref, li_ref, d_ref, y_ref, ho_ref, bu_ref, *, nseq, dec_seq):
    nl = CH_STATES // LANES
    for k in range(SSM_CHUNKS):
        uk = u_ref[:, k * CH_IN:(k + 1) * CH_IN]
        bu = jnp.dot(uk.astype(BF16), b_ref[k], preferred_element_type=F32)
        for c in range(2 * nl):
            bu_ref[c] = bu[:, c * LANES:(c + 1) * LANES]
        base = k * 2 * CH_STATES
        hr = h0_ref[:, base:base + CH_STATES]
        hi = h0_ref[:, base + CH_STATES:base + 2 * CH_STATES]
        ar = lr_ref[k]
        ai = li_ref[k]
        for t in range(dec_seq):
            br = jnp.concatenate([bu_ref[c, pl.ds(t, nseq, stride=dec_seq), :] for c in range(nl)], axis=1)
            bi = jnp.concatenate([bu_ref[nl + c, pl.ds(t, nseq, stride=dec_seq), :] for c in range(nl)], axis=1)
            hr, hi = ar * hr + (br - ai * hi), ar * hi + (bi + ai * hr)
            for c in range(nl):
                bu_ref[c, pl.ds(t, nseq, stride=dec_seq), :] = hr[:, c * LANES:(c + 1) * LANES]
                bu_ref[nl + c, pl.ds(t, nseq, stride=dec_seq), :] = hi[:, c * LANES:(c + 1) * LANES]
        ho_ref[:, base:base + CH_STATES] = hr
        ho_ref[:, base + CH_STATES:base + 2 * CH_STATES] = hi
        hall = jnp.concatenate([bu_ref[c] for c in range(2 * nl)], axis=1)
        y_ref[:, k * CH_IN:(k + 1) * CH_IN] = (
            jnp.dot(hall.astype(BF16), c_ref[k], preferred_element_type=F32) + d_ref[k] * uk)


def _ssm_sample(u, h0, sp, nseq, dec_seq):
    rows = u.shape[0]
    return pl.pallas_call(
        functools.partial(_ssm_sample_body, nseq=nseq, dec_seq=dec_seq),
        grid=(1,),
        in_specs=[_full(u.shape), _full(h0.shape), _full(sp["B"].shape), _full(sp["C"].shape),
                  _full(sp["lr"].shape), _full(sp["li"].shape), _full(sp["d"].shape)],
        out_specs=[_full((rows, SSM_WIDTH)), _full(h0.shape)],
        out_shape=[jax.ShapeDtypeStruct((rows, SSM_WIDTH), F32), jax.ShapeDtypeStruct(h0.shape, F32)],
        scratch_shapes=[pltpu.VMEM((2 * CH_STATES // LANES, rows, LANES), F32)],
        compiler_params=_cparams(("arbitrary",)),
        name="ssm_sample",
    )(u, h0, sp["B"], sp["C"], sp["lr"], sp["li"], sp["d"])


def _ssm_params(lambda_re, lambda_im, log_dt, b_re, b_im, c_re, c_im, d):
    lam = lax.complex(jnp.minimum(lambda_re.astype(F32), -1e-4), lambda_im.astype(F32))
    dt = jnp.exp(log_dt.astype(F32))[:, None]
    lam_dt = lam * dt
    lam_bar = jnp.exp(lam_dt)
    b = lax.complex(b_re.astype(F32), b_im.astype(F32))
    b_bar = ((lam_bar - 1.0) / lam)[..., None] * b
    def block_diag(rows, row_group, col_group):
        n, w = rows.shape
        tiled = jnp.tile(rows, (1, CH_GROUPS))
        own = (lax.broadcasted_iota(jnp.int32, tiled.shape, 0) // row_group) % CH_GROUPS
        col = lax.broadcasted_iota(jnp.int32, tiled.shape, 1) // col_group
        return jnp.where(own == col, tiled, 0.0)

    def pack_b(part):
        rows = part.transpose(0, 2, 1).reshape(SSM_GROUPS * SSM_GROUP, SSM_STATE)
        return block_diag(rows, SSM_GROUP, SSM_STATE).reshape(SSM_CHUNKS, CH_IN, CH_STATES)

    def pack_c(part):
        rows = part.transpose(0, 2, 1).reshape(SSM_GROUPS * SSM_STATE, SSM_GROUP)
        return block_diag(rows, SSM_STATE, SSM_GROUP).reshape(SSM_CHUNKS, CH_STATES, CH_IN)

    bmat = jnp.concatenate([pack_b(jnp.real(b_bar)), pack_b(jnp.imag(b_bar))], axis=2).astype(BF16)
    cmat = jnp.concatenate([pack_c(c_re.astype(F32)), pack_c(-c_im.astype(F32))], axis=1).astype(BF16)
    pw = jnp.exp(lam_dt * float(SCAN_SEG)).reshape(SSM_CHUNKS, 1, CH_STATES)
    lb = lam_bar.reshape(SSM_CHUNKS, 1, CH_STATES)
    return dict(B=bmat, C=cmat, lr=jnp.real(lb), li=jnp.imag(lb), pr=jnp.real(pw), pi=jnp.imag(pw),
                d=d.astype(F32).reshape(SSM_CHUNKS, 1, CH_IN))


def _pack_state(re, im):
    n = re.shape[0]
    st = jnp.stack([re.reshape(n, SSM_CHUNKS, CH_STATES), im.reshape(n, SSM_CHUNKS, CH_STATES)], axis=2)
    return st.reshape(n, SSM_CHUNKS * 2 * CH_STATES).astype(F32)


def _unpack_state(h, n):
    st = h.reshape(n, SSM_CHUNKS, 2, CH_GROUPS, SSM_STATE)
    return (st[:, :, 0].reshape(n, SSM_GROUPS, SSM_STATE), st[:, :, 1].reshape(n, SSM_GROUPS, SSM_STATE))


def _glu_body(y_ref, w_ref, g_ref, o_ref):
    a = jax.nn.gelu(y_ref[...])
    z = a * jax.nn.sigmoid(jnp.dot(a.astype(BF16), w_ref[...], preferred_element_type=F32))
    o_ref[...] = _rms(z, g_ref[...]).astype(BF16)


def _glu(y, w_bf, layer, g, tm):
    rows = y.shape[0]
    spec = pl.BlockSpec((tm, SSM_WIDTH), lambda i: (i, 0))
    return pl.pallas_call(
        _glu_body,
        grid=(rows // tm,),
        in_specs=[spec, _layer_spec((SSM_WIDTH, SSM_WIDTH), layer), _full((1, SSM_WIDTH))],
        out_specs=spec,
        out_shape=jax.ShapeDtypeStruct((rows, SSM_WIDTH), BF16),
        compiler_params=_cparams(("arbitrary",)),
        name="glu_rms",
    )(y, w_bf, g)


def _layer_norm(y, g, b):
    mu = jnp.mean(y, axis=-1, keepdims=True)
    yc = y - mu
    var = jnp.mean(jnp.square(yc), axis=-1, keepdims=True)
    return yc * lax.rsqrt(var + LN_EPS) * g + b


OUTPROJ_SUB = 256


def _outproj_body(*refs, aliased, nblk):
    if aliased:
        refs = refs[1:]
    a_ref, s_ref, x_ref, w_ref, g_ref, b_ref, wrc_ref, br_ref, o_ref, lg_ref = refs
    i = pl.program_id(0)

    @pl.when(i < nblk)
    def _():
        tm = a_ref.shape[0]
        sub = min(tm, OUTPROJ_SUB)
        for r0 in range(0, tm, sub):
            rows = slice(r0, r0 + sub)
            acc = jnp.dot(a_ref[rows, :], w_ref[0:ATTN_WIDTH, :], preferred_element_type=F32)
            acc = acc + jnp.dot(s_ref[rows, :], w_ref[ATTN_WIDTH:, :], preferred_element_type=F32)
            x1 = _layer_norm(DEEPNORM_ALPHA * x_ref[rows, :] + acc, g_ref[...], b_ref[...])
            o_ref[rows, :] = x1
            xh = x1.astype(BF16)
            xl = (x1 - xh.astype(F32)).astype(BF16)
            hh = jnp.dot(xh, wrc_ref[...], preferred_element_type=F32)
            lh = jnp.dot(xl, wrc_ref[:, 0:LANES], preferred_element_type=F32)
            lg_ref[rows, :] = hh[:, 0:LANES] + (hh[:, LANES:] + lh) + br_ref[...]

    @pl.when(i >= nblk)
    def _():
        o_ref[...] = jnp.zeros(o_ref.shape, F32)


def _outproj(an, sn, x, w_bf, layer, g, b, wr, br, tm, total_rows, row_block0, x1_buf=None):
    rows = an.shape[0]
    aliased = x1_buf is not None
    nblk = rows // tm
    tail = 0 if aliased or rows == total_rows else 1
    blk = lambda i: jnp.minimum(i, nblk - 1)
    half = lambda: pl.BlockSpec((tm, ATTN_WIDTH), lambda i: (blk(i), 0))
    in_specs = [half(), half(), pl.BlockSpec((tm, D_MODEL), lambda i: (blk(i), 0)),
                _layer_spec((D_MODEL, D_MODEL), layer), _full((1, D_MODEL)), _full((1, D_MODEL)),
                _full((D_MODEL, 2 * LANES)), _full((1, LANES))]
    wr_hi = wr.astype(BF16)
    wr_lo = (wr - wr_hi.astype(F32)).astype(BF16)
    args = [an, sn, x, w_bf, g, b, jnp.concatenate([wr_hi, wr_lo], axis=1), br]
    if aliased:
        in_specs = [pl.BlockSpec(memory_space=pl.ANY)] + in_specs
        args = [x1_buf] + args
    return pl.pallas_call(
        functools.partial(_outproj_body, aliased=aliased, nblk=nblk),
        grid=(nblk + tail,),
        in_specs=in_specs,
        out_specs=[pl.BlockSpec((tm, D_MODEL), lambda i: (i + row_block0, 0)),
                   pl.BlockSpec((tm, LANES), lambda i: (blk(i), 0))],
        out_shape=[jax.ShapeDtypeStruct((total_rows, D_MODEL), F32), jax.ShapeDtypeStruct((rows, LANES), F32)],
        input_output_aliases={0: 0} if aliased else {},
        compiler_params=_cparams(("arbitrary",)),
        name="outproj_ln1",
    )(*args)


ROUTE_T = 640
R_E0, R_E1, R_W0, R_W1, R_P0, R_P1 = range(6)


def _route_select(x):
    lane = lax.broadcasted_iota(jnp.int32, x.shape, 1)
    big = jnp.int32(1 << 20)
    neg = -jnp.inf
    gmask = lane < N_EXPERT_GROUPS
    gm = jnp.max(jnp.where(gmask, x, neg), axis=-1, keepdims=True)
    grp = jnp.min(jnp.where(gmask & (x == gm), lane, big), axis=-1, keepdims=True)
    p_grp = 1.0 / jnp.sum(jnp.where(gmask, jnp.exp(x - gm), 0.0), axis=-1, keepdims=True)
    lo = N_EXPERT_GROUPS + grp * EXPERTS_PER_GROUP
    emask = (lane >= lo) & (lane < lo + EXPERTS_PER_GROUP)
    v1 = jnp.max(jnp.where(emask, x, neg), axis=-1, keepdims=True)
    i1 = jnp.min(jnp.where(emask & (x == v1), lane, big), axis=-1, keepdims=True)
    emask2 = emask & (lane != i1)
    v2 = jnp.max(jnp.where(emask2, x, neg), axis=-1, keepdims=True)
    i2 = jnp.min(jnp.where(emask2 & (x == v2), lane, big), axis=-1, keepdims=True)
    e2 = jnp.exp(v2 - v1)
    w1 = p_grp / (1.0 + e2)
    w2 = p_grp * e2 / (1.0 + e2)
    return (i1 - N_EXPERT_GROUPS).astype(F32), (i2 - N_EXPERT_GROUPS).astype(F32), w1, w2


def _route_plan_body(lg_ref, ltri_ref, utri_ref, o_ref, meta_ref, rec_ref, run_ref):
    ph = pl.program_id(0)
    j = pl.program_id(1)
    tm = lg_ref.shape[0]
    lane = lax.broadcasted_iota(jnp.int32, (tm, LANES), 1)
    lanef = lane.astype(F32)
    r0 = pl.multiple_of(j * tm, tm)

    @pl.when((ph == 0) & (j == 0))
    def _():
        run_ref[...] = jnp.zeros(run_ref.shape, F32)

    @pl.when(ph == 0)
    def _():
        e0, e1, w0, w1 = _route_select(lg_ref[...])
        oh0 = (lanef == e0).astype(F32)
        oh1 = (lanef == e1).astype(F32)
        oh2 = oh0 + oh1
        before = jnp.dot(ltri_ref[...], oh2.astype(BF16), preferred_element_type=F32) + run_ref[0:1, :]
        rk0 = jnp.sum(before * oh0, axis=-1, keepdims=True)
        rk1 = jnp.sum(before * oh1, axis=-1, keepdims=True)
        run_ref[...] = run_ref[...] + jnp.sum(oh2, axis=0, keepdims=True)
        rec = jnp.where(lane == R_E0, e0, 0.0)
        rec = jnp.where(lane == R_E1, e1, rec)
        rec = jnp.where(lane == R_W0, w0, rec)
        rec = jnp.where(lane == R_W1, w1, rec)
        rec = jnp.where(lane == R_P0, rk0, rec)
        rec = jnp.where(lane == R_P1, rk1, rec)
        rec_ref[pl.ds(r0, tm), :] = rec

    @pl.when(ph == 1)
    def _():
        counts = run_ref[...]
        nblk = jnp.floor((counts + (MOE_BLOCK - 1)) * (1.0 / MOE_BLOCK))
        end_blk = jnp.dot(nblk.astype(BF16), utri_ref[...], preferred_element_type=F32)
        start_row = (end_blk - nblk) * MOE_BLOCK
        rec = rec_ref[pl.ds(r0, tm), :]
        oh0 = (lanef == rec[:, R_E0:R_E0 + 1]).astype(F32)
        oh1 = (lanef == rec[:, R_E1:R_E1 + 1]).astype(F32)
        p0 = jnp.sum(oh0 * start_row[0:1, :], axis=-1, keepdims=True) + rec[:, R_P0:R_P0 + 1]
        p1 = jnp.sum(oh1 * start_row[0:1, :], axis=-1, keepdims=True) + rec[:, R_P1:R_P1 + 1]
        out = jnp.where(lane == R_P0, p0, rec)
        out = jnp.where(lane == R_P1, p1, out)
        o_ref[...] = out

        @pl.when(j == 0)
        def _():
            meta_ref[...] = end_blk


def _route_plan(logits):
    rows = logits.shape[0]
    tm = ROUTE_T
    ltri = jnp.asarray(np.tril(np.ones((tm, tm), np.float32), -1), BF16)
    utri = jnp.asarray(np.triu(np.ones((LANES, LANES), np.float32)), BF16)
    mrows = SUBLANES
    return pl.pallas_call(
        _route_plan_body,
        grid=(2, rows // tm),
        in_specs=[pl.BlockSpec((tm, LANES), lambda p, j: (j * (1 - p), 0)),
                  pl.BlockSpec((tm, tm), lambda p, j: (0, 0)),
                  pl.BlockSpec((LANES, LANES), lambda p, j: (0, 0))],
        out_specs=[pl.BlockSpec((tm, LANES), lambda p, j: (j * p, 0)),
                   pl.BlockSpec((mrows, LANES), lambda p, j: (0, 0))],
        out_shape=[jax.ShapeDtypeStruct((rows, LANES), F32), jax.ShapeDtypeStruct((mrows, LANES), F32)],
        scratch_shapes=[pltpu.VMEM((rows, LANES), F32), pltpu.VMEM((SUBLANES, LANES), F32)],
        compiler_params=_cparams(("arbitrary", "arbitrary")),
        name="route_plan",
    )(logits, ltri, utri)


def _rowmap_body(pos_ref, o_ref, *, ntok, nrows):
    unroll = 8

    def zero(c, _):
        for u in range(unroll):
            o_ref[c * unroll + u] = 0
        return 0
    lax.fori_loop(0, nrows // unroll, zero, 0)

    def fill(c, _):
        for u in range(unroll):
            t = c * unroll + u
            o_ref[pos_ref[2 * t]] = t
            o_ref[pos_ref[2 * t + 1]] = t
        return 0
    lax.fori_loop(0, ntok // unroll, fill, 0)


def _rowmap(pos, nrows):
    ntok = pos.shape[0] // 2
    return pl.pallas_call(
        functools.partial(_rowmap_body, ntok=ntok, nrows=nrows),
        in_specs=[pl.BlockSpec(memory_space=pltpu.SMEM)],
        out_specs=pl.BlockSpec(memory_space=pltpu.SMEM),
        out_shape=jax.ShapeDtypeStruct((nrows,), jnp.int32),
        name="moe_rowmap",
    )(pos)


GATHER_SLOTS = 4


def _moe_body(bend_ref, row_tok_ref, x_hbm, wg_ref, wu_ref, wd_ref, ys_hbm, *scratch, nblocks):
    xbufs = scratch[:GATHER_SLOTS]
    ybufs = scratch[GATHER_SLOTS:GATHER_SLOTS + 2]
    gsem, osem, wg_bf, wu_bf, wd_bf = scratch[GATHER_SLOTS + 2:]
    ahead = GATHER_SLOTS - 1
    e = pl.program_id(0)
    nused = bend_ref[N_EXPERTS - 1]
    b1 = bend_ref[e]
    b0 = jnp.where(e == 0, 0, bend_ref[jnp.maximum(e - 1, 0)])

    def gather_start(blk, s):
        base = blk * MOE_BLOCK
        for r in range(MOE_BLOCK):
            tok = row_tok_ref[base + r]
            pltpu.make_async_copy(x_hbm.at[pl.ds(tok, 1), :], xbufs[s].at[pl.ds(r, 1), :],
                                  gsem.at[s]).start(priority=GATHER_PRIORITY)

    def gather_wait(s):
        pltpu.make_async_copy(x_hbm.at[pl.ds(0, MOE_BLOCK), :], xbufs[s], gsem.at[s]).wait()

    def out_copy(blk, s):
        row0 = pl.multiple_of(blk * MOE_BLOCK, MOE_BLOCK)
        return pltpu.make_async_copy(ybufs[s], ys_hbm.at[pl.ds(row0, MOE_BLOCK), :], osem.at[s])

    @pl.when(e == 0)
    def _():
        for k in range(ahead):
            gather_start(k, k)

    @pl.when(b1 > b0)
    def _():
        wg_bf[...] = wg_ref[...].astype(BF16)
        wu_bf[...] = wu_ref[...].astype(BF16)
        wd_bf[...] = wd_ref[...].astype(BF16)

        def do_block(b, s):
            so = s % 2
            gather_wait(s)

            @pl.when(b >= 2)
            def _():
                out_copy(b - 2, so).wait()

            gather_start(jnp.minimum(b + ahead, nused - 1), (s + ahead) % GATHER_SLOTS)
            x = xbufs[s][...].astype(BF16)
            hg = jnp.dot(x, wg_bf[...], preferred_element_type=F32)
            hu = jnp.dot(x, wu_bf[...], preferred_element_type=F32)
            h = (jax.nn.silu(hg) * hu).astype(BF16)
            ybufs[so][...] = jnp.dot(h, wd_bf[...], preferred_element_type=F32)
            out_copy(b, so).start()

        def body(b, _):
            for s in range(GATHER_SLOTS):
                @pl.when(b % GATHER_SLOTS == s)
                def _():
                    do_block(b, s)
            return 0

        lax.fori_loop(b0, b1, body, 0)

    @pl.when(e == N_EXPERTS - 1)
    def _():
        last = nused - 1
        for s in range(GATHER_SLOTS):
            @pl.when(last % GATHER_SLOTS == s)
            def _():
                for k in range(1, GATHER_SLOTS):
                    gather_wait((s + k) % GATHER_SLOTS)
        out_copy(0, 0).wait()
        out_copy(0, 1).wait()
        yb0 = ybufs[0]
        yb0[...] = jnp.zeros(yb0.shape, F32)

        def zstart(b, _):
            out_copy(b, 0).start()
            return 0

        def zwait(b, _):
            out_copy(b, 0).wait()
            return 0

        lax.fori_loop(nused, nblocks, zstart, 0)
        lax.fori_loop(nused, nblocks, zwait, 0)


def _moe(bend, row_tok, x1, w_gate, w_up, w_down, layer):
    nblocks = row_tok.shape[0] // MOE_BLOCK
    assert 2 * x1.shape[0] > GATHER_SLOTS * MOE_BLOCK
    wspec = lambda a, b: pl.BlockSpec((None, None, a, b), lambda e, be, rt: (layer, e, 0, 0))
    grid_spec = pltpu.PrefetchScalarGridSpec(
        num_scalar_prefetch=2,
        grid=(N_EXPERTS,),
        in_specs=[pl.BlockSpec(memory_space=pl.ANY), wspec(D_MODEL, EXPERT_FF), wspec(D_MODEL, EXPERT_FF),
                  wspec(EXPERT_FF, D_MODEL)],
        out_specs=pl.BlockSpec(memory_space=pl.ANY),
        scratch_shapes=([pltpu.VMEM((MOE_BLOCK, D_MODEL), F32) for _ in range(GATHER_SLOTS + 2)]
                        + [pltpu.SemaphoreType.DMA((GATHER_SLOTS,)), pltpu.SemaphoreType.DMA((2,)),
                           pltpu.VMEM((D_MODEL, EXPERT_FF), BF16),
                           pltpu.VMEM((D_MODEL, EXPERT_FF), BF16),
                           pltpu.VMEM((EXPERT_FF, D_MODEL), BF16)]),
    )
    return pl.pallas_call(
        functools.partial(_moe_body, nblocks=nblocks),
        grid_spec=grid_spec,
        out_shape=jax.ShapeDtypeStruct((nblocks * MOE_BLOCK, D_MODEL), F32),
        compiler_params=_cparams(("arbitrary",), bounds_checks=False),
        name="moe_experts",
    )(bend, row_tok, x1, w_gate, w_up, w_down)


COMB_T = 128


def _combine_body(pos_ref, ys_hbm, x_ref, rt_ref, g_ref, b_ref, op_ref, os_ref, *scratch, n_prompt_blocks):
    i = pl.program_id(0)
    nsteps = pl.num_programs(0)
    bufs = tuple((scratch[2 * s], scratch[2 * s + 1]) for s in range(GATHER_SLOTS))
    sem = scratch[2 * GATHER_SLOTS]
    ahead = GATHER_SLOTS - 1

    def gather_start(step, s):
        base = step * (2 * COMB_T)
        for r in range(COMB_T):
            for k in range(2):
                p = pos_ref[base + 2 * r + k]
                pltpu.make_async_copy(ys_hbm.at[pl.ds(p, 1), :], bufs[s][k].at[pl.ds(r, 1), :],
                                      sem.at[s]).start(priority=k)

    def gather_wait(s):
        for k in range(2):
            pltpu.make_async_copy(ys_hbm.at[pl.ds(0, COMB_T), :], bufs[s][k], sem.at[s]).wait()

    @pl.when(i == 0)
    def _():
        for k in range(ahead):
            gather_start(k, k)

    def do_tile(s):
        gather_wait(s)
        gather_start(jnp.minimum(i + ahead, nsteps - 1), (s + ahead) % GATHER_SLOTS)
        rt = rt_ref[...]
        ff = rt[:, R_W0:R_W0 + 1] * bufs[s][0][...] + rt[:, R_W1:R_W1 + 1] * bufs[s][1][...]
        out = _layer_norm(DEEPNORM_ALPHA * x_ref[...] + ff, g_ref[...], b_ref[...])

        @pl.when(i < n_prompt_blocks)
        def _():
            op_ref[...] = out

        @pl.when(i >= n_prompt_blocks)
        def _():
            os_ref[...] = out

        @pl.when(i == nsteps - 1)
        def _():
            for k in range(1, GATHER_SLOTS):
                gather_wait((s + k) % GATHER_SLOTS)

    for s in range(GATHER_SLOTS):
        @pl.when(i % GATHER_SLOTS == s)
        def _():
            do_tile(s)


def _combine(pos, ys, x1, route, g, b, n_prompt_rows):
    rows = x1.shape[0]
    npb = n_prompt_rows // COMB_T
    assert rows // COMB_T >= GATHER_SLOTS
    grid_spec = pltpu.PrefetchScalarGridSpec(
        num_scalar_prefetch=1,
        grid=(rows // COMB_T,),
        in_specs=[pl.BlockSpec(memory_space=pl.ANY),
                  pl.BlockSpec((COMB_T, D_MODEL), lambda i, p: (i, 0)),
                  pl.BlockSpec((COMB_T, LANES), lambda i, p: (i, 0)),
                  pl.BlockSpec((1, D_MODEL), lambda i, p: (0, 0)),
                  pl.BlockSpec((1, D_MODEL), lambda i, p: (0, 0))],
        out_specs=[pl.BlockSpec((COMB_T, D_MODEL), lambda i, p: (jnp.minimum(i, npb - 1), 0)),
                   pl.BlockSpec((COMB_T, D_MODEL), lambda i, p: (jnp.maximum(i - npb, 0), 0))],
        scratch_shapes=([pltpu.VMEM((COMB_T, D_MODEL), F32) for _ in range(2 * GATHER_SLOTS)]
                        + [pltpu.SemaphoreType.DMA((GATHER_SLOTS,))]),
    )
    return pl.pallas_call(
        functools.partial(_combine_body, n_prompt_blocks=npb),
        grid_spec=grid_spec,
        out_shape=[jax.ShapeDtypeStruct((n_prompt_rows, D_MODEL), F32),
                   jax.ShapeDtypeStruct((rows - n_prompt_rows, D_MODEL), F32)],
        compiler_params=_cparams(("arbitrary",), bounds_checks=False),
        name="combine_ln2",
    )(pos, ys, x1, route, g, b)


def kernel(x_prompt, x_sample, cache_k, cache_v, state_ssm_re, state_ssm_im, w_in, ssm_lambda_re, ssm_lambda_im, ssm_log_dt, ssm_b_re, ssm_b_im, ssm_c_re, ssm_c_im, ssm_d, ssm_w_glu, attn_sinks, attn_norm_g, ssm_norm_g, w_out, ln1_g, ln1_b, router_group_w, router_group_b, router_expert_w, router_expert_b, expert_w_gate, expert_w_up, expert_w_down, ln2_g, ln2_b):
    nbatch, seq, _ = x_prompt.shape
    nseq, dec_seq, _ = x_sample.shape
    wbuf = cache_k.shape[2]
    rows_p = nbatch * seq
    rows_s = nseq * dec_seq
    rows_all = rows_p + rows_s
    tm_p = 512

    xp = x_prompt.reshape(rows_p, D_MODEL)
    xs = x_sample.reshape(rows_s, D_MODEL)
    tab_p = _rope_tables(np.arange(seq))
    tab_s = _rope_tables(PAST_LEN + (np.arange(rows_s) % dec_seq))
    ck_all = cache_k.astype(F32).reshape(DEPTH, nseq, wbuf, KV_WIDTH)
    cv_all = cache_v.astype(F32).reshape(DEPTH, nseq, wbuf, KV_WIDTH)

    w_in_bf = w_in.astype(BF16)
    w_glu_bf = ssm_w_glu.astype(BF16)
    w_out_bf = w_out.astype(BF16)

    outs = {k: [] for k in ("kp", "vp", "hrp", "hip", "ks", "vs", "hrs", "his")}
    for l in range(DEPTH):
        sp = _ssm_params(ssm_lambda_re[l], ssm_lambda_im[l], ssm_log_dt[l], ssm_b_re[l], ssm_b_im[l],
                         ssm_c_re[l], ssm_c_im[l], ssm_d[l])
        sinks = attn_sinks[l].astype(F32)
        attn_g = attn_norm_g[l].astype(F32).reshape(1, ATTN_WIDTH)
        ssm_g = ssm_norm_g[l].astype(F32).reshape(1, SSM_WIDTH)
        g1 = ln1_g[l].astype(F32).reshape(1, D_MODEL)
        b1 = ln1_b[l].astype(F32).reshape(1, D_MODEL)
        g2 = ln2_g[l].astype(F32).reshape(1, D_MODEL)
        b2 = ln2_b[l].astype(F32).reshape(1, D_MODEL)
        n_rt = N_EXPERT_GROUPS + N_EXPERTS
        wr = jnp.concatenate([router_group_w[l].astype(F32)]
                             + [router_expert_w[l, g].astype(F32) for g in range(N_EXPERT_GROUPS)]
                             + [jnp.zeros((D_MODEL, LANES - n_rt), F32)], axis=1)
        br = jnp.concatenate([router_group_b[l].astype(F32), router_expert_b[l].astype(F32).reshape(-1),
                              jnp.zeros((LANES - n_rt,), F32)]).reshape(1, LANES)

        q_p, k_p, v_p, u_p = _inproj(xp, w_in_bf, l, *tab_p, tm=tm_p)
        q_s, k_s, v_s, u_s = _inproj(xs, w_in_bf, l, *tab_s, tm=rows_s)

        an_p = _attn_prompt(sinks, q_p, k_p, v_p, attn_g, seq)
        an_s, nk_s, nv_s = _attn_sample(sinks, q_s, k_s, v_s, ck_all, cv_all, attn_g, dec_seq, l)

        y_p, hl_p = _ssm_prompt(u_p, sp, nbatch, seq)
        y_s, hl_s = _ssm_sample(u_s, _pack_state(state_ssm_re[l], state_ssm_im[l]), sp, nseq, dec_seq)
        sn_p = _glu(y_p, w_glu_bf, l, ssm_g, tm_p)
        sn_s = _glu(y_s, w_glu_bf, l, ssm_g, rows_s)

        x1, lg_p = _outproj(an_p, sn_p, xp, w_out_bf, l, g1, b1, wr, br, tm_p, rows_all, 0)
        x1, lg_s = _outproj(an_s, sn_s, xs, w_out_bf, l, g1, b1, wr, br, rows_s, rows_all, rows_p // rows_s,
                            x1_buf=x1)
        nblocks = 2 * rows_all // MOE_BLOCK + N_EXPERTS
        route, meta = _route_plan(jnp.concatenate([lg_p, lg_s], axis=0))
        pos = route[:, R_P0:R_P1 + 1].astype(jnp.int32).reshape(2 * rows_all)
        bend = meta[0, :N_EXPERTS].astype(jnp.int32)
        row_tok = _rowmap(pos, nblocks * MOE_BLOCK)

        ys = _moe(bend, row_tok, x1, expert_w_gate, expert_w_up, expert_w_down, l)
        xp, xs = _combine(pos, ys, x1, route, g2, b2, rows_p)

        wp = min(WINDOW, seq)
        tail = lambda a: a.reshape(nbatch, seq, KV_WIDTH)[:, seq - wp:].reshape(nbatch, wp, N_KV_HEADS, HEAD_DIM)
        outs["kp"].append(tail(k_p))
        outs["vp"].append(tail(v_p))
        hp = hl_p.reshape(nbatch, SSM_CHUNKS * 2 * CH_STATES)
        hr, hi = _unpack_state(hp, nbatch)
        outs["hrp"].append(hr)
        outs["hip"].append(hi)
        outs["ks"].append(nk_s.reshape(nseq, wbuf, N_KV_HEADS, HEAD_DIM))
        outs["vs"].append(nv_s.reshape(nseq, wbuf, N_KV_HEADS, HEAD_DIM))
        hr, hi = _unpack_state(hl_s, nseq)
        outs["hrs"].append(hr)
        outs["his"].append(hi)

    st = {k: jnp.stack(v) for k, v in outs.items()}
    return (xp.reshape(nbatch, seq, D_MODEL), xs.reshape(nseq, dec_seq, D_MODEL),
            st["kp"], st["vp"], st["hrp"], st["hip"], st["ks"], st["vs"], st["hrs"], st["his"])
```

```python
import functools
import math

import numpy as np
import jax
import jax.numpy as jnp
from jax import lax
from jax.experimental import pallas as pl
from jax.experimental.pallas import tpu as pltpu

F32 = jnp.float32
BF16 = jnp.bfloat16

D_MODEL = 2048
DEPTH = 2
PAST_LEN = 16384
ATTN_WIDTH = 1024
SSM_WIDTH = 1024
HEAD_DIM = 64
N_HEADS = 16
N_KV_HEADS = 4
Q_PER_KV = 4
KV_WIDTH = 256
WINDOW = 128
ROPE_THETA = 10000.0
SSM_GROUP = 16
SSM_GROUPS = 64
SSM_STATE = 64
N_EXPERT_GROUPS = 4
EXPERTS_PER_GROUP = 8
N_EXPERTS = 32
EXPERT_FF = 512
MOE_BLOCK = 128
DEEPNORM_ALPHA = (2.0 * DEPTH) ** 0.25
LN_EPS = 1e-5
RMS_EPS = 1e-6

LANES = 128
SUBLANES = 8
VMEM_LIMIT = 56 * 1024 * 1024

SSM_CHUNKS = 4
CH_GROUPS = SSM_GROUPS // SSM_CHUNKS
CH_STATES = CH_GROUPS * SSM_STATE
CH_IN = CH_GROUPS * SSM_GROUP
SCAN_T = 512
SCAN_SEG = SCAN_T // SUBLANES


def _cparams(sem=None, vmem=VMEM_LIMIT, bounds_checks=True):
    return pltpu.CompilerParams(dimension_semantics=sem, vmem_limit_bytes=vmem,
                                disable_bounds_checks=not bounds_checks)


def _full(shape):
    n = len(shape)
    return pl.BlockSpec(shape, lambda *a: (0,) * n)


def _inproj_body(x_ref, w_ref, cos_ref, sa_ref, sb_ref, q_ref, k_ref, v_ref, u_ref):
    x = x_ref[...].astype(BF16)
    cos = cos_ref[...]
    sa = sa_ref[...]
    sb = sb_ref[...]

    def rope(c):
        return c * cos + pltpu.roll(c, LANES - 32, 1) * sa + pltpu.roll(c, 32, 1) * sb

    pq = jnp.dot(x, w_ref[:, 0:ATTN_WIDTH], preferred_element_type=F32)
    for j in range(ATTN_WIDTH // LANES):
        q_ref[:, j * LANES:(j + 1) * LANES] = rope(pq[:, j * LANES:(j + 1) * LANES]).astype(BF16)
    pk = jnp.dot(x, w_ref[:, ATTN_WIDTH:ATTN_WIDTH + KV_WIDTH], preferred_element_type=F32)
    for j in range(KV_WIDTH // LANES):
        k_ref[:, j * LANES:(j + 1) * LANES] = rope(pk[:, j * LANES:(j + 1) * LANES])
    v_ref[...] = jnp.dot(x, w_ref[:, ATTN_WIDTH + KV_WIDTH:ATTN_WIDTH + 2 * KV_WIDTH],
                         preferred_element_type=F32)
    u_ref[...] = jnp.dot(x, w_ref[:, ATTN_WIDTH + 2 * KV_WIDTH:], preferred_element_type=F32)


def _layer_spec(shape, layer):
    return pl.BlockSpec((None,) + tuple(shape), lambda *a: (layer, 0, 0))


def _inproj(x, w_bf, layer, cos_t, sa_t, sb_t, tm):
    rows = x.shape[0]
    tab_blocks = cos_t.shape[0] // tm
    in_w = w_bf.shape[2]
    row_spec = lambda w: pl.BlockSpec((tm, w), lambda i: (i, 0))
    tab_spec = pl.BlockSpec((tm, LANES), lambda i: (i % tab_blocks, 0))
    return pl.pallas_call(
        _inproj_body,
        grid=(rows // tm,),
        in_specs=[row_spec(D_MODEL), _layer_spec((D_MODEL, in_w), layer), tab_spec, tab_spec, tab_spec],
        out_specs=[row_spec(ATTN_WIDTH), row_spec(KV_WIDTH), row_spec(KV_WIDTH), row_spec(SSM_WIDTH)],
        out_shape=[jax.ShapeDtypeStruct((rows, ATTN_WIDTH), BF16),
                   jax.ShapeDtypeStruct((rows, KV_WIDTH), F32),
                   jax.ShapeDtypeStruct((rows, KV_WIDTH), F32),
                   jax.ShapeDtypeStruct((rows, SSM_WIDTH), F32)],
        compiler_params=_cparams(("arbitrary",)),
        name="inproj_rope",
    )(x, w_bf, cos_t, sa_t, sb_t)


def _rope_tables(positions):
    half = HEAD_DIM // 2
    inv_freq = ROPE_THETA ** (-np.arange(half, dtype=np.float64) / half)
    ang = positions.astype(np.float64)[:, None] * inv_freq[None, :]
    lane = np.arange(LANES)
    cos = np.cos(ang)[:, lane % half]
    sin = np.sin(ang)[:, lane % half]
    first = (lane % HEAD_DIM) < half
    sa = np.where(first[None, :], -sin, 0.0)
    sb = np.where(first[None, :], 0.0, sin)
    return (jnp.asarray(cos, F32), jnp.asarray(sa, F32), jnp.asarray(sb, F32))


def _rms(x, g):
    return x * lax.rsqrt(jnp.mean(jnp.square(x), axis=-1, keepdims=True) + RMS_EPS) * g


def _attn_prompt_body(sink_ref, q_ref, kp_ref, kc_ref, vp_ref, vc_ref, g_ref, o_ref, acc_ref, *, blocks_per_seq):
    i = pl.program_id(0)
    has_prev = (i % blocks_per_seq) > 0
    kk = jnp.concatenate([kp_ref[...], kc_ref[...]], axis=0).astype(BF16)
    vt = jnp.concatenate([vp_ref[...], vc_ref[...]], axis=0).T.astype(BF16)
    ncol = Q_PER_KV * WINDOW
    key = lax.broadcasted_iota(jnp.int32, (2 * WINDOW, ncol), 0)
    col = lax.broadcasted_iota(jnp.int32, (2 * WINDOW, ncol), 1)
    qry = col % WINDOW
    visible = (key > qry) & (key <= qry + WINDOW) & ((key >= WINDOW) | has_prev)
    head = lax.broadcasted_iota(jnp.int32, (1, ncol), 1) // WINDOW
    q = q_ref[...] * (HEAD_DIM ** -0.5)
    scores = []
    for g in range(N_KV_HEADS):
        qg = jnp.concatenate([q[:, (g * Q_PER_KV + j) * HEAD_DIM:(g * Q_PER_KV + j + 1) * HEAD_DIM]
                              for j in range(Q_PER_KV)], axis=0)
        kh = kk[:, g * HEAD_DIM:(g + 1) * HEAD_DIM]
        scores.append(lax.dot_general(kh, qg, (((1,), (1,)), ((), ())), preferred_element_type=F32))
    for g in range(N_KV_HEADS):
        st = jnp.where(visible, scores[g], -jnp.inf)
        sink = jnp.zeros((1, ncol), F32)
        for j in range(Q_PER_KV):
            sink = jnp.where(head == j, sink_ref[g * Q_PER_KV + j], sink)
        m = jnp.maximum(jnp.max(st, axis=0, keepdims=True), sink)
        p = jnp.exp(st - m)
        denom = jnp.sum(p, axis=0, keepdims=True) + jnp.exp(sink - m)
        ot = jnp.dot(vt[g * HEAD_DIM:(g + 1) * HEAD_DIM, :], p.astype(BF16), preferred_element_type=F32)
        ot = ot / denom
        for j in range(Q_PER_KV):
            h = g * Q_PER_KV + j
            acc_ref[h * HEAD_DIM:(h + 1) * HEAD_DIM, :] = ot[:, j * WINDOW:(j + 1) * WINDOW]
    o_ref[...] = _rms(acc_ref[...].T, g_ref[...]).astype(BF16)


def _attn_prompt(sinks, q, k, v, g, seq):
    rows = q.shape[0]
    nblk = rows // WINDOW
    bps = seq // WINDOW
    cur = lambda w: pl.BlockSpec((WINDOW, w), lambda i: (i, 0))
    prev = lambda w: pl.BlockSpec((WINDOW, w), lambda i: (jnp.maximum(i - 1, 0), 0))
    return pl.pallas_call(
        functools.partial(_attn_prompt_body, blocks_per_seq=bps),
        grid=(nblk,),
        in_specs=[pl.BlockSpec(memory_space=pltpu.SMEM), cur(ATTN_WIDTH), prev(KV_WIDTH), cur(KV_WIDTH),
                  prev(KV_WIDTH), cur(KV_WIDTH), _full((1, ATTN_WIDTH))],
        out_specs=cur(ATTN_WIDTH),
        out_shape=jax.ShapeDtypeStruct((rows, ATTN_WIDTH), BF16),
        scratch_shapes=[pltpu.VMEM((ATTN_WIDTH, WINDOW), F32)],
        compiler_params=_cparams(("arbitrary",)),
        name="attn_prompt",
    )(sinks, q, k, k, v, v, g)


SEQ_PER_STEP = 8


def _attn_sample_body(sink_ref, q_ref, kn_ref, vn_ref, ck_ref, cv_ref, g_ref, o_ref, nk_ref, nv_ref, acc_ref,
                      *, dec_seq):
    nq = SEQ_PER_STEP * dec_seq
    wb = ck_ref.shape[1]
    nkc = SEQ_PER_STEP * wb
    ck = ck_ref[...].reshape(nkc, KV_WIDTH)
    cv = cv_ref[...].reshape(nkc, KV_WIDTH)
    kn = kn_ref[...]
    vn = vn_ref[...]
    ckb = ck.astype(BF16)
    cvb = cv.astype(BF16)
    knb = kn.astype(BF16)
    vnb = vn.astype(BF16)
    q = q_ref[...]

    rows = Q_PER_KV * nq
    r = lax.broadcasted_iota(jnp.int32, (rows, nkc), 0) % nq
    c = lax.broadcasted_iota(jnp.int32, (rows, nkc), 1)
    vis_c = ((r // dec_seq) == (c // wb)) & ((c % wb) >= (r % dec_seq) + 1 + (wb - WINDOW))
    rn = lax.broadcasted_iota(jnp.int32, (rows, nq), 0) % nq
    cn = lax.broadcasted_iota(jnp.int32, (rows, nq), 1)
    vis_n = ((rn // dec_seq) == (cn // dec_seq)) & ((cn % dec_seq) <= (rn % dec_seq))
    hrow = lax.broadcasted_iota(jnp.int32, (rows, 1), 0) // nq

    dn = (((1,), (1,)), ((), ()))
    for g in range(N_KV_HEADS):
        qs = jnp.concatenate([q[:, (g * Q_PER_KV + j) * HEAD_DIM:(g * Q_PER_KV + j + 1) * HEAD_DIM]
                              for j in range(Q_PER_KV)], axis=0)
        sl = slice(g * HEAD_DIM, (g + 1) * HEAD_DIM)
        sc = lax.dot_general(qs, ckb[:, sl], dn, preferred_element_type=F32) * (HEAD_DIM ** -0.5)
        sn = lax.dot_general(qs, knb[:, sl], dn, preferred_element_type=F32) * (HEAD_DIM ** -0.5)
        sc = jnp.where(vis_c, sc, -jnp.inf)
        sn = jnp.where(vis_n, sn, -jnp.inf)
        sink = jnp.zeros((rows, 1), F32)
        for j in range(Q_PER_KV):
            sink = jnp.where(hrow == j, sink_ref[g * Q_PER_KV + j], sink)
        m = jnp.maximum(jnp.maximum(jnp.max(sc, axis=-1, keepdims=True), jnp.max(sn, axis=-1, keepdims=True)), sink)
        pc = jnp.exp(sc - m)
        pn = jnp.exp(sn - m)
        denom = jnp.sum(pc, axis=-1, keepdims=True) + jnp.sum(pn, axis=-1, keepdims=True) + jnp.exp(sink - m)
        o = (jnp.dot(pc.astype(BF16), cvb[:, sl], preferred_element_type=F32)
             + jnp.dot(pn.astype(BF16), vnb[:, sl], preferred_element_type=F32)) / denom
        for j in range(Q_PER_KV):
            h = g * Q_PER_KV + j
            acc_ref[:, h * HEAD_DIM:(h + 1) * HEAD_DIM] = o[j * nq:(j + 1) * nq, :]
    o_ref[...] = _rms(acc_ref[...], g_ref[...]).astype(BF16)

    keep = wb - dec_seq
    for s in range(SEQ_PER_STEP):
        nk_ref[s, 0:keep, :] = ck_ref[s, dec_seq:wb, :]
        nv_ref[s, 0:keep, :] = cv_ref[s, dec_seq:wb, :]
        nk_ref[s, keep:wb, :] = kn[s * dec_seq:(s + 1) * dec_seq, :]
        nv_ref[s, keep:wb, :] = vn[s * dec_seq:(s + 1) * dec_seq, :]


def _attn_sample(sinks, q, kn, vn, ck, cv, g, dec_seq, layer):
    rows = q.shape[0]
    _, nseq, wb, _ = ck.shape
    nq = SEQ_PER_STEP * dec_seq
    rowb = lambda w: pl.BlockSpec((nq, w), lambda i: (i, 0))
    cin = pl.BlockSpec((None, SEQ_PER_STEP, wb, KV_WIDTH), lambda i: (layer, i, 0, 0))
    cb = pl.BlockSpec((SEQ_PER_STEP, wb, KV_WIDTH), lambda i: (i, 0, 0))
    return pl.pallas_call(
        functools.partial(_attn_sample_body, dec_seq=dec_seq),
        grid=(nseq // SEQ_PER_STEP,),
        in_specs=[pl.BlockSpec(memory_space=pltpu.SMEM), rowb(ATTN_WIDTH), rowb(KV_WIDTH), rowb(KV_WIDTH),
                  cin, cin, _full((1, ATTN_WIDTH))],
        out_specs=[rowb(ATTN_WIDTH), cb, cb],
        out_shape=[jax.ShapeDtypeStruct((rows, ATTN_WIDTH), BF16),
                   jax.ShapeDtypeStruct(ck.shape[1:], F32), jax.ShapeDtypeStruct(cv.shape[1:], F32)],
        scratch_shapes=[pltpu.VMEM((nq, ATTN_WIDTH), F32)],
        compiler_params=_cparams(("arbitrary",)),
        name="attn_sample",
    )(sinks, q, kn, vn, ck, cv, g)


def _ssm_prompt_body(ua_ref, uc_ref, b_ref, c_ref, lr_ref, li_ref, pr_ref, pi_ref, d_ref, y_ref, hl_ref,
                     uperm_ref, bu0, bu1, hb0, hb1, yp_ref, carry_ref, *, nchunks, ntc):
    k = pl.program_id(0)
    bus = (bu0, bu1)
    hbs = (hb0, hb1)
    lanes_per_chunk = CH_IN // LANES
    seg = lambda s: slice(s * SCAN_SEG, (s + 1) * SCAN_SEG)
    lane = lambda c: slice(c * LANES, (c + 1) * LANES)

    def stage_a(item, slot):
        cc = item % SSM_CHUNKS
        for c in range(lanes_per_chunk):
            for s in range(SUBLANES):
                uperm_ref[c, pl.ds(s, SCAN_SEG, stride=SUBLANES), :] = ua_ref[seg(s), lane(c)]
        up = jnp.concatenate([uperm_ref[c] for c in range(lanes_per_chunk)], axis=1)
        bus[slot][...] = jnp.dot(up.astype(BF16), b_ref[cc], preferred_element_type=F32)

    def stage_b(item, slot):
        cc = item % SSM_CHUNKS
        tc = (item // SSM_CHUNKS) % ntc
        bu_ref = bus[slot]
        hb_ref = hbs[slot]
        ar = jnp.broadcast_to(lr_ref[cc], (SUBLANES, CH_STATES))
        ai = jnp.broadcast_to(li_ref[cc], (SUBLANES, CH_STATES))

        def advance(h, j):
            hr, hi = h
            br = bu_ref[j * SUBLANES:(j + 1) * SUBLANES, 0:CH_STATES]
            bi = bu_ref[j * SUBLANES:(j + 1) * SUBLANES, CH_STATES:2 * CH_STATES]
            return ar * hr + (br - ai * hi), ar * hi + (bi + ai * hr)

        zero = jnp.zeros((SUBLANES, CH_STATES), F32)
        h = (zero, zero)
        for j in range(SCAN_SEG):
            h = advance(h, j)
        fr, fi = h

        cst = jnp.where(tc == 0, 0.0, carry_ref[cc])
        c_r = cst[0:1, :]
        c_i = cst[1:2, :]
        lsr = pr_ref[cc]
        lsi = pi_ref[cc]
        rows_r, rows_i = [], []
        for s in range(SUBLANES):
            rows_r.append(c_r)
            rows_i.append(c_i)
            n_r = fr[s:s + 1, :] + (lsr * c_r - lsi * c_i)
            n_i = fi[s:s + 1, :] + (lsr * c_i + lsi * c_r)
            c_r, c_i = n_r, n_i
        carry_ref[cc] = jnp.concatenate([c_r, c_i], axis=0)
        hl_ref[0, cc] = jnp.concatenate([c_r, c_i], axis=0)

        h = (jnp.concatenate(rows_r, axis=0), jnp.concatenate(rows_i, axis=0))
        for jj in range(SCAN_SEG // 2):
            h1 = advance(h, 2 * jj)
            h = advance(h1, 2 * jj + 1)
            rows = slice(2 * jj * SUBLANES, (2 * jj + 2) * SUBLANES)
            hb_ref[rows, 0:CH_STATES] = jnp.concatenate([h1[0], h[0]], axis=0).astype(BF16)
            hb_ref[rows, CH_STATES:2 * CH_STATES] = jnp.concatenate([h1[1], h[1]], axis=0).astype(BF16)

    def stage_c(item, slot):
        cc = item % SSM_CHUNKS
        yp = jnp.dot(hbs[slot][...], c_ref[cc], preferred_element_type=F32)
        d = d_ref[cc]
        for c in range(lanes_per_chunk):
            yp_ref[c] = yp[:, lane(c)]
        for s in range(SUBLANES):
            for c in range(lanes_per_chunk):
                y_ref[seg(s), lane(c)] = (yp_ref[c, pl.ds(s, SCAN_SEG, stride=SUBLANES), :]
                                          + d[:, lane(c)] * uc_ref[seg(s), lane(c)])

    def run(do_a, do_b, do_c, parity):
        if do_a:
            stage_a(k, parity)
        if do_b:
            stage_b(k - 1, 1 - parity)
        if do_c:
            stage_c(k - 2, parity)

    @pl.when(k == 0)
    def _():
        carry_ref[...] = jnp.zeros(carry_ref.shape, F32)
        run(True, False, False, 0)

    @pl.when(k == 1)
    def _():
        run(True, True, False, 1)

    for parity in range(2):
        @pl.when((k >= 2) & (k < nchunks) & (k % 2 == parity))
        def _():
            run(True, True, True, parity)

    @pl.when(k == nchunks)
    def _():
        run(False, True, True, nchunks % 2)

    @pl.when(k == nchunks + 1)
    def _():
        run(False, False, True, (nchunks + 1) % 2)


def _ssm_prompt(u, sp, nbatch, seq):
    rows = u.shape[0]
    ntc = seq // SCAN_T
    nchunks = nbatch * ntc * SSM_CHUNKS
    assert nchunks >= 2
    item_a = lambda k: jnp.minimum(k, nchunks - 1)
    item_b = lambda k: jnp.clip(k - 1, 0, nchunks - 1)
    item_c = lambda k: jnp.clip(k - 2, 0, nchunks - 1)
    chunk_block = lambda item: (item // SSM_CHUNKS, item % SSM_CHUNKS)
    y, hl = pl.pallas_call(
        functools.partial(_ssm_prompt_body, nchunks=nchunks, ntc=ntc),
        grid=(nchunks + 2,),
        in_specs=[pl.BlockSpec((SCAN_T, CH_IN), lambda k: chunk_block(item_a(k))),
                  pl.BlockSpec((SCAN_T, CH_IN), lambda k: chunk_block(item_c(k))),
                  _full(sp["B"].shape), _full(sp["C"].shape), _full(sp["lr"].shape), _full(sp["li"].shape),
                  _full(sp["pr"].shape), _full(sp["pi"].shape), _full(sp["d"].shape)],
        out_specs=[pl.BlockSpec((SCAN_T, CH_IN), lambda k: chunk_block(item_c(k))),
                   pl.BlockSpec((1, SSM_CHUNKS, 2, CH_STATES),
                                lambda k: (item_b(k) // (ntc * SSM_CHUNKS), 0, 0, 0))],
        out_shape=[jax.ShapeDtypeStruct((rows, SSM_WIDTH), F32),
                   jax.ShapeDtypeStruct((nbatch, SSM_CHUNKS, 2, CH_STATES), F32)],
        scratch_shapes=[pltpu.VMEM((CH_IN // LANES, SCAN_T, LANES), F32),
                        pltpu.VMEM((SCAN_T, 2 * CH_STATES), F32), pltpu.VMEM((SCAN_T, 2 * CH_STATES), F32),
                        pltpu.VMEM((SCAN_T, 2 * CH_STATES), BF16), pltpu.VMEM((SCAN_T, 2 * CH_STATES), BF16),
                        pltpu.VMEM((CH_IN // LANES, SCAN_T, LANES), F32),
                        pltpu.VMEM((SSM_CHUNKS, 2, CH_STATES), F32)],
        compiler_params=_cparams(("arbitrary",)),
        name="ssm_prompt",
    )(u, u, sp["B"], sp["C"], sp["lr"], sp["li"], sp["pr"], sp["pi"], sp["d"])
    return y, hl


def _ssm_sample_body(u_ref, h0_ref, b_ref, c_ref, lr_ref, li_ref, d_ref, y_ref, ho_ref, bu_ref, *, nseq, dec_seq):
    nl = CH_STATES // LANES
    for k in range(SSM_CHUNKS):
        uk = u_ref[:, k * CH_IN:(k + 1) * CH_IN]
        bu = jnp.dot(uk.astype(BF16), b_ref[k], preferred_element_type=F32)
        for c in range(2 * nl):
            bu_ref[c] = bu[:, c * LANES:(c + 1) * LANES]
        base = k * 2 * CH_STATES
        hr = h0_ref[:, base:base + CH_STATES]
        hi = h0_ref[:, base + CH_STATES:base + 2 * CH_STATES]
        ar = lr_ref[k]
        ai = li_ref[k]
        for t in range(dec_seq):
            br = jnp.concatenate([bu_ref[c, pl.ds(t, nseq, stride=dec_seq), :] for c in range(nl)], axis=1)
            bi = jnp.concatenate([bu_ref[nl + c, pl.ds(t, nseq, stride=dec_seq), :] for c in range(nl)], axis=1)
            hr, hi = ar * hr + (br - ai * hi), ar * hi + (bi + ai * hr)
            for c in range(nl):
                bu_ref[c, pl.ds(t, nseq, stride=dec_seq), :] = hr[:, c * LANES:(c + 1) * LANES]
                bu_ref[nl + c, pl.ds(t, nseq, stride=dec_seq), :] = hi[:, c * LANES:(c + 1) * LANES]
        ho_ref[:, base:base + CH_STATES] = hr
        ho_ref[:, base + CH_STATES:base + 2 * CH_STATES] = hi
        hall = jnp.concatenate([bu_ref[c] for c in range(2 * nl)], axis=1)
        y_ref[:, k * CH_IN:(k + 1) * CH_IN] = (
            jnp.dot(hall.astype(BF16), c_ref[k], preferred_element_type=F32) + d_ref[k] * uk)


def _ssm_sample(u, h0, sp, nseq, dec_seq):
    rows = u.shape[0]
    return pl.pallas_call(
        functools.partial(_ssm_sample_body, nseq=nseq, dec_seq=dec_seq),
        grid=(1,),
        in_specs=[_full(u.shape), _full(h0.shape), _full(sp["B"].shape), _full(sp["C"].shape),
                  _full(sp["lr"].shape), _full(sp["li"].shape), _full(sp["d"].shape)],
        out_specs=[_full((rows, SSM_WIDTH)), _full(h0.shape)],
        out_shape=[jax.ShapeDtypeStruct((rows, SSM_WIDTH), F32), jax.ShapeDtypeStruct(h0.shape, F32)],
        scratch_shapes=[pltpu.VMEM((2 * CH_STATES // LANES, rows, LANES), F32)],
        compiler_params=_cparams(("arbitrary",)),
        name="ssm_sample",
    )(u, h0, sp["B"], sp["C"], sp["lr"], sp["li"], sp["d"])


def _ssm_params(lambda_re, lambda_im, log_dt, b_re, b_im, c_re, c_im, d):
    lam = lax.complex(jnp.minimum(lambda_re.astype(F32), -1e-4), lambda_im.astype(F32))
    dt = jnp.exp(log_dt.astype(F32))[:, None]
    lam_dt = lam * dt
    lam_bar = jnp.exp(lam_dt)
    b = lax.complex(b_re.astype(F32), b_im.astype(F32))
    b_bar = ((lam_bar - 1.0) / lam)[..., None] * b
    def block_diag(rows, row_group, col_group):
        n, w = rows.shape
        tiled = jnp.tile(rows, (1, CH_GROUPS))
        own = (lax.broadcasted_iota(jnp.int32, tiled.shape, 0) // row_group) % CH_GROUPS
        col = lax.broadcasted_iota(jnp.int32, tiled.shape, 1) // col_group
        return jnp.where(own == col, tiled, 0.0)

    def pack_b(part):
        rows = part.transpose(0, 2, 1).reshape(SSM_GROUPS * SSM_GROUP, SSM_STATE)
        return block_diag(rows, SSM_GROUP, SSM_STATE).reshape(SSM_CHUNKS, CH_IN, CH_STATES)

    def pack_c(part):
        rows = part.transpose(0, 2, 1).reshape(SSM_GROUPS * SSM_STATE, SSM_GROUP)
        return block_diag(rows, SSM_STATE, SSM_GROUP).reshape(SSM_CHUNKS, CH_STATES, CH_IN)

    bmat = jnp.concatenate([pack_b(jnp.real(b_bar)), pack_b(jnp.imag(b_bar))], axis=2).astype(BF16)
    cmat = jnp.concatenate([pack_c(c_re.astype(F32)), pack_c(-c_im.astype(F32))], axis=1).astype(BF16)
    pw = jnp.exp(lam_dt * float(SCAN_SEG)).reshape(SSM_CHUNKS, 1, CH_STATES)
    lb = lam_bar.reshape(SSM_CHUNKS, 1, CH_STATES)
    return dict(B=bmat, C=cmat, lr=jnp.real(lb), li=jnp.imag(lb), pr=jnp.real(pw), pi=jnp.imag(pw),
                d=d.astype(F32).reshape(SSM_CHUNKS, 1, CH_IN))


def _pack_state(re, im):
    n = re.shape[0]
    st = jnp.stack([re.reshape(n, SSM_CHUNKS, CH_STATES), im.reshape(n, SSM_CHUNKS, CH_STATES)], axis=2)
    return st.reshape(n, SSM_CHUNKS * 2 * CH_STATES).astype(F32)


def _unpack_state(h, n):
    st = h.reshape(n, SSM_CHUNKS, 2, CH_GROUPS, SSM_STATE)
    return (st[:, :, 0].reshape(n, SSM_GROUPS, SSM_STATE), st[:, :, 1].reshape(n, SSM_GROUPS, SSM_STATE))


def _glu_body(y_ref, w_ref, g_ref, o_ref):
    a = jax.nn.gelu(y_ref[...])
    z = a * jax.nn.sigmoid(jnp.dot(a.astype(BF16), w_ref[...], preferred_element_type=F32))
    o_ref[...] = _rms(z, g_ref[...]).astype(BF16)


def _glu(y, w_bf, layer, g, tm):
    rows = y.shape[0]
    spec = pl.BlockSpec((tm, SSM_WIDTH), lambda i: (i, 0))
    return pl.pallas_call(
        _glu_body,
        grid=(rows // tm,),
        in_specs=[spec, _layer_spec((SSM_WIDTH, SSM_WIDTH), layer), _full((1, SSM_WIDTH))],
        out_specs=spec,
        out_shape=jax.ShapeDtypeStruct((rows, SSM_WIDTH), BF16),
        compiler_params=_cparams(("arbitrary",)),
        name="glu_rms",
    )(y, w_bf, g)


def _layer_norm(y, g, b):
    mu = jnp.mean(y, axis=-1, keepdims=True)
    yc = y - mu
    var = jnp.mean(jnp.square(yc), axis=-1, keepdims=True)
    return yc * lax.rsqrt(var + LN_EPS) * g + b


OUTPROJ_SUB = 256


def _outproj_body(*refs, aliased, nblk):
    if aliased:
        refs = refs[1:]
    a_ref, s_ref, x_ref, w_ref, g_ref, b_ref, wrc_ref, br_ref, o_ref, lg_ref = refs
    i = pl.program_id(0)

    @pl.when(i < nblk)
    def _():
        tm = a_ref.shape[0]
        sub = min(tm, OUTPROJ_SUB)
        for r0 in range(0, tm, sub):
            rows = slice(r0, r0 + sub)
            acc = jnp.dot(a_ref[rows, :], w_ref[0:ATTN_WIDTH, :], preferred_element_type=F32)
            acc = acc + jnp.dot(s_ref[rows, :], w_ref[ATTN_WIDTH:, :], preferred_element_type=F32)
            x1 = _layer_norm(DEEPNORM_ALPHA * x_ref[rows, :] + acc, g_ref[...], b_ref[...])
            o_ref[rows, :] = x1
            xh = x1.astype(BF16)
            xl = (x1 - xh.astype(F32)).astype(BF16)
            hh = jnp.dot(xh, wrc_ref[...], preferred_element_type=F32)
            lh = jnp.dot(xl, wrc_ref[:, 0:LANES], preferred_element_type=F32)
            lg_ref[rows, :] = hh[:, 0:LANES] + (hh[:, LANES:] + lh) + br_ref[...]

    @pl.when(i >= nblk)
    def _():
        o_ref[...] = jnp.zeros(o_ref.shape, F32)


def _outproj(an, sn, x, w_bf, layer, g, b, wr, br, tm, total_rows, row_block0, x1_buf=None):
    rows = an.shape[0]
    aliased = x1_buf is not None
    nblk = rows // tm
    tail = 0 if aliased or rows == total_rows else 1
    blk = lambda i: jnp.minimum(i, nblk - 1)
    half = lambda: pl.BlockSpec((tm, ATTN_WIDTH), lambda i: (blk(i), 0))
    in_specs = [half(), half(), pl.BlockSpec((tm, D_MODEL), lambda i: (blk(i), 0)),
                _layer_spec((D_MODEL, D_MODEL), layer), _full((1, D_MODEL)), _full((1, D_MODEL)),
                _full((D_MODEL, 2 * LANES)), _full((1, LANES))]
    wr_hi = wr.astype(BF16)
    wr_lo = (wr - wr_hi.astype(F32)).astype(BF16)
    args = [an, sn, x, w_bf, g, b, jnp.concatenate([wr_hi, wr_lo], axis=1), br]
    if aliased:
        in_specs = [pl.BlockSpec(memory_space=pl.ANY)] + in_specs
        args = [x1_buf] + args
    return pl.pallas_call(
        functools.partial(_outproj_body, aliased=aliased, nblk=nblk),
        grid=(nblk + tail,),
        in_specs=in_specs,
        out_specs=[pl.BlockSpec((tm, D_MODEL), lambda i: (i + row_block0, 0)),
                   pl.BlockSpec((tm, LANES), lambda i: (blk(i), 0))],
        out_shape=[jax.ShapeDtypeStruct((total_rows, D_MODEL), F32), jax.ShapeDtypeStruct((rows, LANES), F32)],
        input_output_aliases={0: 0} if aliased else {},
        compiler_params=_cparams(("arbitrary",)),
        name="outproj_ln1",
    )(*args)


ROUTE_T = 640
R_E0, R_E1, R_W0, R_W1, R_P0, R_P1 = range(6)


def _route_select(x):
    lane = lax.broadcasted_iota(jnp.int32, x.shape, 1)
    big = jnp.int32(1 << 20)
    neg = -jnp.inf
    gmask = lane < N_EXPERT_GROUPS
    gm = jnp.max(jnp.where(gmask, x, neg), axis=-1, keepdims=True)
    grp = jnp.min(jnp.where(gmask & (x == gm), lane, big), axis=-1, keepdims=True)
    p_grp = 1.0 / jnp.sum(jnp.where(gmask, jnp.exp(x - gm), 0.0), axis=-1, keepdims=True)
    lo = N_EXPERT_GROUPS + grp * EXPERTS_PER_GROUP
    emask = (lane >= lo) & (lane < lo + EXPERTS_PER_GROUP)
    v1 = jnp.max(jnp.where(emask, x, neg), axis=-1, keepdims=True)
    i1 = jnp.min(jnp.where(emask & (x == v1), lane, big), axis=-1, keepdims=True)
    emask2 = emask & (lane != i1)
    v2 = jnp.max(jnp.where(emask2, x, neg), axis=-1, keepdims=True)
    i2 = jnp.min(jnp.where(emask2 & (x == v2), lane, big), axis=-1, keepdims=True)
    e2 = jnp.exp(v2 - v1)
    w1 = p_grp / (1.0 + e2)
    w2 = p_grp * e2 / (1.0 + e2)
    return (i1 - N_EXPERT_GROUPS).astype(F32), (i2 - N_EXPERT_GROUPS).astype(F32), w1, w2


def _route_plan_body(lg_ref, ltri_ref, utri_ref, o_ref, meta_ref, rec_ref, run_ref):
    ph = pl.program_id(0)
    j = pl.program_id(1)
    tm = lg_ref.shape[0]
    lane = lax.broadcasted_iota(jnp.int32, (tm, LANES), 1)
    lanef = lane.astype(F32)
    r0 = pl.multiple_of(j * tm, tm)

    @pl.when((ph == 0) & (j == 0))
    def _():
        run_ref[...] = jnp.zeros(run_ref.shape, F32)

    @pl.when(ph == 0)
    def _():
        e0, e1, w0, w1 = _route_select(lg_ref[...])
        oh0 = (lanef == e0).astype(F32)
        oh1 = (lanef == e1).astype(F32)
        oh2 = oh0 + oh1
        before = jnp.dot(ltri_ref[...], oh2.astype(BF16), preferred_element_type=F32) + run_ref[0:1, :]
        rk0 = jnp.sum(before * oh0, axis=-1, keepdims=True)
        rk1 = jnp.sum(before * oh1, axis=-1, keepdims=True)
        run_ref[...] = run_ref[...] + jnp.sum(oh2, axis=0, keepdims=True)
        rec = jnp.where(lane == R_E0, e0, 0.0)
        rec = jnp.where(lane == R_E1, e1, rec)
        rec = jnp.where(lane == R_W0, w0, rec)
        rec = jnp.where(lane == R_W1, w1, rec)
        rec = jnp.where(lane == R_P0, rk0, rec)
        rec = jnp.where(lane == R_P1, rk1, rec)
        rec_ref[pl.ds(r0, tm), :] = rec

    @pl.when(ph == 1)
    def _():
        counts = run_ref[...]
        nblk = jnp.floor((counts + (MOE_BLOCK - 1)) * (1.0 / MOE_BLOCK))
        end_blk = jnp.dot(nblk.astype(BF16), utri_ref[...], preferred_element_type=F32)
        start_row = (end_blk - nblk) * MOE_BLOCK
        rec = rec_ref[pl.ds(r0, tm), :]
        oh0 = (lanef == rec[:, R_E0:R_E0 + 1]).astype(F32)
        oh1 = (lanef == rec[:, R_E1:R_E1 + 1]).astype(F32)
        p0 = jnp.sum(oh0 * start_row[0:1, :], axis=-1, keepdims=True) + rec[:, R_P0:R_P0 + 1]
        p1 = jnp.sum(oh1 * start_row[0:1, :], axis=-1, keepdims=True) + rec[:, R_P1:R_P1 + 1]
        out = jnp.where(lane == R_P0, p0, rec)
        out = jnp.where(lane == R_P1, p1, out)
        o_ref[...] = out

        @pl.when(j == 0)
        def _():
            meta_ref[...] = end_blk


def _route_plan(logits):
    rows = logits.shape[0]
    tm = ROUTE_T
    ltri = jnp.asarray(np.tril(np.ones((tm, tm), np.float32), -1), BF16)
    utri = jnp.asarray(np.triu(np.ones((LANES, LANES), np.float32)), BF16)
    mrows = SUBLANES
    return pl.pallas_call(
        _route_plan_body,
        grid=(2, rows // tm),
        in_specs=[pl.BlockSpec((tm, LANES), lambda p, j: (j * (1 - p), 0)),
                  pl.BlockSpec((tm, tm), lambda p, j: (0, 0)),
                  pl.BlockSpec((LANES, LANES), lambda p, j: (0, 0))],
        out_specs=[pl.BlockSpec((tm, LANES), lambda p, j: (j * p, 0)),
                   pl.BlockSpec((mrows, LANES), lambda p, j: (0, 0))],
        out_shape=[jax.ShapeDtypeStruct((rows, LANES), F32), jax.ShapeDtypeStruct((mrows, LANES), F32)],
        scratch_shapes=[pltpu.VMEM((rows, LANES), F32), pltpu.VMEM((SUBLANES, LANES), F32)],
        compiler_params=_cparams(("arbitrary", "arbitrary")),
        name="route_plan",
    )(logits, ltri, utri)


def _build_rowmap(pos_ref, o_ref):
    unroll = 8
    nrows = o_ref.shape[0]
    ntok = pos_ref.shape[0] // 2

    def zero(c, _):
        for u in range(unroll):
            o_ref[c * unroll + u] = 0
        return 0
    lax.fori_loop(0, nrows // unroll, zero, 0)

    def fill(c, _):
        for u in range(unroll):
            t = c * unroll + u
            o_ref[pos_ref[2 * t]] = t
            o_ref[pos_ref[2 * t + 1]] = t
        return 0
    lax.fori_loop(0, ntok // unroll, fill, 0)


GATHER_SLOTS = 4


def _moe_body(bend_ref, pos_ref, x_hbm, wg_ref, wu_ref, wd_ref, ys_hbm, *scratch, nblocks):
    xbufs = scratch[:GATHER_SLOTS]
    ybufs = scratch[GATHER_SLOTS:GATHER_SLOTS + 2]
    gsem, osem, wg_bf, wu_bf, wd_bf, row_tok_ref = scratch[GATHER_SLOTS + 2:]
    ahead = GATHER_SLOTS - 1
    e = pl.program_id(0)
    nused = bend_ref[N_EXPERTS - 1]
    b1 = bend_ref[e]
    b0 = jnp.where(e == 0, 0, bend_ref[jnp.maximum(e - 1, 0)])

    def gather_start(blk, s):
        base = blk * MOE_BLOCK
        for r in range(MOE_BLOCK):
            tok = row_tok_ref[base + r]
            pltpu.make_async_copy(x_hbm.at[pl.ds(tok, 1), :], xbufs[s].at[pl.ds(r, 1), :],
                                  gsem.at[s]).start(priority=r % 2)

    def gather_wait(s):
        pltpu.make_async_copy(x_hbm.at[pl.ds(0, MOE_BLOCK), :], xbufs[s], gsem.at[s]).wait()

    def out_copy(blk, s):
        row0 = pl.multiple_of(blk * MOE_BLOCK, MOE_BLOCK)
        return pltpu.make_async_copy(ybufs[s], ys_hbm.at[pl.ds(row0, MOE_BLOCK), :], osem.at[s])

    @pl.when(e == 0)
    def _():
        _build_rowmap(pos_ref, row_tok_ref)
        for k in range(ahead):
            gather_start(k, k)

    @pl.when(b1 > b0)
    def _():
        wg_bf[...] = wg_ref[...].astype(BF16)
        wu_bf[...] = wu_ref[...].astype(BF16)
        wd_bf[...] = wd_ref[...].astype(BF16)

        def do_block(b, s):
            so = s % 2
            gather_wait(s)

            @pl.when(b >= 2)
            def _():
                out_copy(b - 2, so).wait()

            gather_start(jnp.minimum(b + ahead, nused - 1), (s + ahead) % GATHER_SLOTS)
            x = xbufs[s][...].astype(BF16)
            hg = jnp.dot(x, wg_bf[...], preferred_element_type=F32)
            hu = jnp.dot(x, wu_bf[...], preferred_element_type=F32)
            h = (jax.nn.silu(hg) * hu).astype(BF16)
            ybufs[so][...] = jnp.dot(h, wd_bf[...], preferred_element_type=F32)
            out_copy(b, so).start()

        def body(b, _):
            for s in range(GATHER_SLOTS):
                @pl.when(b % GATHER_SLOTS == s)
                def _():
                    do_block(b, s)
            return 0

        lax.fori_loop(b0, b1, body, 0)

    @pl.when(e == N_EXPERTS - 1)
    def _():
        last = nused - 1
        for s in range(GATHER_SLOTS):
            @pl.when(last % GATHER_SLOTS == s)
            def _():
                for k in range(1, GATHER_SLOTS):
                    gather_wait((s + k) % GATHER_SLOTS)
        out_copy(0, 0).wait()
        out_copy(0, 1).wait()
        yb0 = ybufs[0]
        yb0[...] = jnp.zeros(yb0.shape, F32)

        def zstart(b, _):
            out_copy(b, 0).start()
            return 0

        def zwait(b, _):
            out_copy(b, 0).wait()
            return 0

        lax.fori_loop(nused, nblocks, zstart, 0)
        lax.fori_loop(nused, nblocks, zwait, 0)


def _moe(bend, pos, x1, w_gate, w_up, w_down, layer, nblocks):
    assert 2 * x1.shape[0] > GATHER_SLOTS * MOE_BLOCK
    wspec = lambda a, b: pl.BlockSpec((None, None, a, b), lambda e, be, rt: (layer, e, 0, 0))
    grid_spec = pltpu.PrefetchScalarGridSpec(
        num_scalar_prefetch=2,
        grid=(N_EXPERTS,),
        in_specs=[pl.BlockSpec(memory_space=pl.ANY), wspec(D_MODEL, EXPERT_FF), wspec(D_MODEL, EXPERT_FF),
                  wspec(EXPERT_FF, D_MODEL)],
        out_specs=pl.BlockSpec(memory_space=pl.ANY),
        scratch_shapes=([pltpu.VMEM((MOE_BLOCK, D_MODEL), F32) for _ in range(GATHER_SLOTS + 2)]
                        + [pltpu.SemaphoreType.DMA((GATHER_SLOTS,)), pltpu.SemaphoreType.DMA((2,)),
                           pltpu.VMEM((D_MODEL, EXPERT_FF), BF16),
                           pltpu.VMEM((D_MODEL, EXPERT_FF), BF16),
                           pltpu.VMEM((EXPERT_FF, D_MODEL), BF16),
                           pltpu.SMEM((nblocks * MOE_BLOCK,), jnp.int32)]),
    )
    return pl.pallas_call(
        functools.partial(_moe_body, nblocks=nblocks),
        grid_spec=grid_spec,
        out_shape=jax.ShapeDtypeStruct((nblocks * MOE_BLOCK, D_MODEL), F32),
        compiler_params=_cparams(("arbitrary",), bounds_checks=False),
        name="moe_experts",
    )(bend, pos, x1, w_gate, w_up, w_down)


COMB_T = 128


def _combine_body(pos_ref, ys_hbm, x_ref, rt_ref, g_ref, b_ref, op_ref, os_ref, *scratch, n_prompt_blocks):
    i = pl.program_id(0)
    nsteps = pl.num_programs(0)
    bufs = tuple((scratch[2 * s], scratch[2 * s + 1]) for s in range(GATHER_SLOTS))
    sem = scratch[2 * GATHER_SLOTS]
    ahead = GATHER_SLOTS - 1

    def gather_start(step, s):
        base = step * (2 * COMB_T)
        for r in range(COMB_T):
            for k in range(2):
                p = pos_ref[base + 2 * r + k]
                pltpu.make_async_copy(ys_hbm.at[pl.ds(p, 1), :], bufs[s][k].at[pl.ds(r, 1), :],
                                      sem.at[s]).start(priority=k)

    def gather_wait(s):
        for k in range(2):
            pltpu.make_async_copy(ys_hbm.at[pl.ds(0, COMB_T), :], bufs[s][k], sem.at[s]).wait()

    @pl.when(i == 0)
    def _():
        for k in range(ahead):
            gather_start(k, k)

    def do_tile(s):
        gather_wait(s)
        gather_start(jnp.minimum(i + ahead, nsteps - 1), (s + ahead) % GATHER_SLOTS)
        rt = rt_ref[...]
        ff = rt[:, R_W0:R_W0 + 1] * bufs[s][0][...] + rt[:, R_W1:R_W1 + 1] * bufs[s][1][...]
        out = _layer_norm(DEEPNORM_ALPHA * x_ref[...] + ff, g_ref[...], b_ref[...])

        @pl.when(i < n_prompt_blocks)
        def _():
            op_ref[...] = out

        @pl.when(i >= n_prompt_blocks)
        def _():
            os_ref[...] = out

        @pl.when(i == nsteps - 1)
        def _():
            for k in range(1, GATHER_SLOTS):
                gather_wait((s + k) % GATHER_SLOTS)

    for s in range(GATHER_SLOTS):
        @pl.when(i % GATHER_SLOTS == s)
        def _():
            do_tile(s)


def _combine(pos, ys, x1, route, g, b, n_prompt_rows):
    rows = x1.shape[0]
    npb = n_prompt_rows // COMB_T
    assert rows // COMB_T >= GATHER_SLOTS
    grid_spec = pltpu.PrefetchScalarGridSpec(
        num_scalar_prefetch=1,
        grid=(rows // COMB_T,),
        in_specs=[pl.BlockSpec(memory_space=pl.ANY),
                  pl.BlockSpec((COMB_T, D_MODEL), lambda i, p: (i, 0)),
                  pl.BlockSpec((COMB_T, LANES), lambda i, p: (i, 0)),
                  pl.BlockSpec((1, D_MODEL), lambda i, p: (0, 0)),
                  pl.BlockSpec((1, D_MODEL), lambda i, p: (0, 0))],
        out_specs=[pl.BlockSpec((COMB_T, D_MODEL), lambda i, p: (jnp.minimum(i, npb - 1), 0)),
                   pl.BlockSpec((COMB_T, D_MODEL), lambda i, p: (jnp.maximum(i - npb, 0), 0))],
        scratch_shapes=([pltpu.VMEM((COMB_T, D_MODEL), F32) for _ in range(2 * GATHER_SLOTS)]
                        + [pltpu.SemaphoreType.DMA((GATHER_SLOTS,))]),
    )
    return pl.pallas_call(
        functools.partial(_combine_body, n_prompt_blocks=npb),
        grid_spec=grid_spec,
        out_shape=[jax.ShapeDtypeStruct((n_prompt_rows, D_MODEL), F32),
                   jax.ShapeDtypeStruct((rows - n_prompt_rows, D_MODEL), F32)],
        compiler_params=_cparams(("arbitrary",), bounds_checks=False),
        name="combine_ln2",
    )(pos, ys, x1, route, g, b)


def kernel(x_prompt, x_sample, cache_k, cache_v, state_ssm_re, state_ssm_im, w_in, ssm_lambda_re, ssm_lambda_im, ssm_log_dt, ssm_b_re, ssm_b_im, ssm_c_re, ssm_c_im, ssm_d, ssm_w_glu, attn_sinks, attn_norm_g, ssm_norm_g, w_out, ln1_g, ln1_b, router_group_w, router_group_b, router_expert_w, router_expert_b, expert_w_gate, expert_w_up, expert_w_down, ln2_g, ln2_b):
    nbatch, seq, _ = x_prompt.shape
    nseq, dec_seq, _ = x_sample.shape
    wbuf = cache_k.shape[2]
    rows_p = nbatch * seq
    rows_s = nseq * dec_seq
    rows_all = rows_p + rows_s
    tm_p = 512

    xp = x_prompt.reshape(rows_p, D_MODEL)
    xs = x_sample.reshape(rows_s, D_MODEL)
    tab_p = _rope_tables(np.arange(seq))
    tab_s = _rope_tables(PAST_LEN + (np.arange(rows_s) % dec_seq))
    ck_all = cache_k.astype(F32).reshape(DEPTH, nseq, wbuf, KV_WIDTH)
    cv_all = cache_v.astype(F32).reshape(DEPTH, nseq, wbuf, KV_WIDTH)

    w_in_bf = w_in.astype(BF16)
    w_glu_bf = ssm_w_glu.astype(BF16)
    w_out_bf = w_out.astype(BF16)

    outs = {k: [] for k in ("kp", "vp", "hrp", "hip", "ks", "vs", "hrs", "his")}
    for l in range(DEPTH):
        sp = _ssm_params(ssm_lambda_re[l], ssm_lambda_im[l], ssm_log_dt[l], ssm_b_re[l], ssm_b_im[l],
                         ssm_c_re[l], ssm_c_im[l], ssm_d[l])
        sinks = attn_sinks[l].astype(F32)
        attn_g = attn_norm_g[l].astype(F32).reshape(1, ATTN_WIDTH)
        ssm_g = ssm_norm_g[l].astype(F32).reshape(1, SSM_WIDTH)
        g1 = ln1_g[l].astype(F32).reshape(1, D_MODEL)
        b1 = ln1_b[l].astype(F32).reshape(1, D_MODEL)
        g2 = ln2_g[l].astype(F32).reshape(1, D_MODEL)
        b2 = ln2_b[l].astype(F32).reshape(1, D_MODEL)
        n_rt = N_EXPERT_GROUPS + N_EXPERTS
        wr = jnp.concatenate([router_group_w[l].astype(F32)]
                             + [router_expert_w[l, g].astype(F32) for g in range(N_EXPERT_GROUPS)]
                             + [jnp.zeros((D_MODEL, LANES - n_rt), F32)], axis=1)
        br = jnp.concatenate([router_group_b[l].astype(F32), router_expert_b[l].astype(F32).reshape(-1),
                              jnp.zeros((LANES - n_rt,), F32)]).reshape(1, LANES)

        q_p, k_p, v_p, u_p = _inproj(xp, w_in_bf, l, *tab_p, tm=tm_p)
        q_s, k_s, v_s, u_s = _inproj(xs, w_in_bf, l, *tab_s, tm=rows_s)

        an_p = _attn_prompt(sinks, q_p, k_p, v_p, attn_g, seq)
        an_s, nk_s, nv_s = _attn_sample(sinks, q_s, k_s, v_s, ck_all, cv_all, attn_g, dec_seq, l)

        y_p, hl_p = _ssm_prompt(u_p, sp, nbatch, seq)
        y_s, hl_s = _ssm_sample(u_s, _pack_state(state_ssm_re[l], state_ssm_im[l]), sp, nseq, dec_seq)
        sn_p = _glu(y_p, w_glu_bf, l, ssm_g, tm_p)
        sn_s = _glu(y_s, w_glu_bf, l, ssm_g, rows_s)

        x1, lg_p = _outproj(an_p, sn_p, xp, w_out_bf, l, g1, b1, wr, br, tm_p, rows_all, 0)
        x1, lg_s = _outproj(an_s, sn_s, xs, w_out_bf, l, g1, b1, wr, br, rows_s, rows_all, rows_p // rows_s,
                            x1_buf=x1)
        nblocks = 2 * rows_all // MOE_BLOCK + N_EXPERTS
        route, meta = _route_plan(jnp.concatenate([lg_p, lg_s], axis=0))
        pos = route[:, R_P0:R_P1 + 1].astype(jnp.int32).reshape(2 * rows_all)
        bend = meta[0, :N_EXPERTS].astype(jnp.int32)

        ys = _moe(bend, pos, x1, expert_w_gate, expert_w_up, expert_w_down, l, nblocks)
        xp, xs = _combine(pos, ys, x1, route, g2, b2, rows_p)

        wp = min(WINDOW, seq)
        tail = lambda a: a.reshape(nbatch, seq, KV_WIDTH)[:, seq - wp:].reshape(nbatch, wp, N_KV_HEADS, HEAD_DIM)
        outs["kp"].append(tail(k_p))
        outs["vp"].append(tail(v_p))
        hp = hl_p.reshape(nbatch, SSM_CHUNKS * 2 * CH_STATES)
        hr, hi = _unpack_state(hp, nbatch)
        outs["hrp"].append(hr)
        outs["hip"].append(hi)
        outs["ks"].append(nk_s.reshape(nseq, wbuf, N_KV_HEADS, HEAD_DIM))
        outs["vs"].append(nv_s.reshape(nseq, wbuf, N_KV_HEADS, HEAD_DIM))
        hr, hi = _unpack_state(hl_s, nseq)
        outs["hrs"].append(hr)
        outs["his"].append(hi)

    st = {k: jnp.stack(v) for k, v in outs.items()}
    return (xp.reshape(nbatch, seq, D_MODEL), xs.reshape(nseq, dec_seq, D_MODEL),
            st["kp"], st["vp"], st["hrp"], st["hip"], st["ks"], st["vs"], st["hrs"], st["his"])
```

```python
import functools
import math

import numpy as np
import jax
import jax.numpy as jnp
from jax import lax
from jax.experimental import pallas as pl
from jax.experimental.pallas import tpu as pltpu

F32 = jnp.float32
BF16 = jnp.bfloat16

D_MODEL = 2048
DEPTH = 2
PAST_LEN = 16384
ATTN_WIDTH = 1024
SSM_WIDTH = 1024
HEAD_DIM = 64
N_HEADS = 16
N_KV_HEADS = 4
Q_PER_KV = 4
KV_WIDTH = 256
WINDOW = 128
ROPE_THETA = 10000.0
SSM_GROUP = 16
SSM_GROUPS = 64
SSM_STATE = 64
N_EXPERT_GROUPS = 4
EXPERTS_PER_GROUP = 8
N_EXPERTS = 32
EXPERT_FF = 512
MOE_BLOCK = 128
DEEPNORM_ALPHA = (2.0 * DEPTH) ** 0.25
LN_EPS = 1e-5
RMS_EPS = 1e-6

LANES = 128
SUBLANES = 8
VMEM_LIMIT = 56 * 1024 * 1024

SSM_CHUNKS = 4
CH_GROUPS = SSM_GROUPS // SSM_CHUNKS
CH_STATES = CH_GROUPS * SSM_STATE
CH_IN = CH_GROUPS * SSM_GROUP
SCAN_T = 512
SCAN_SEG = SCAN_T // SUBLANES


def _cparams(sem=None, vmem=VMEM_LIMIT, bounds_checks=True):
    return pltpu.CompilerParams(dimension_semantics=sem, vmem_limit_bytes=vmem,
                                disable_bounds_checks=not bounds_checks)


def _full(shape):
    n = len(shape)
    return pl.BlockSpec(shape, lambda *a: (0,) * n)


def _inproj_body(x_ref, w_ref, cos_ref, sa_ref, sb_ref, q_ref, k_ref, v_ref, u_ref):
    x = x_ref[...].astype(BF16)
    cos = cos_ref[...]
    sa = sa_ref[...]
    sb = sb_ref[...]

    def rope(c):
        return c * cos + pltpu.roll(c, LANES - 32, 1) * sa + pltpu.roll(c, 32, 1) * sb

    pq = jnp.dot(x, w_ref[:, 0:ATTN_WIDTH], preferred_element_type=F32)
    for j in range(ATTN_WIDTH // LANES):
        q_ref[:, j * LANES:(j + 1) * LANES] = rope(pq[:, j * LANES:(j + 1) * LANES]).astype(BF16)
    pk = jnp.dot(x, w_ref[:, ATTN_WIDTH:ATTN_WIDTH + KV_WIDTH], preferred_element_type=F32)
    for j in range(KV_WIDTH // LANES):
        k_ref[:, j * LANES:(j + 1) * LANES] = rope(pk[:, j * LANES:(j + 1) * LANES])
    v_ref[...] = jnp.dot(x, w_ref[:, ATTN_WIDTH + KV_WIDTH:ATTN_WIDTH + 2 * KV_WIDTH],
                         preferred_element_type=F32)
    u_ref[...] = jnp.dot(x, w_ref[:, ATTN_WIDTH + 2 * KV_WIDTH:], preferred_element_type=F32)


def _layer_spec(shape, layer):
    return pl.BlockSpec((None,) + tuple(shape), lambda *a: (layer,) + (0,) * len(shape))


def _inproj(x, w_bf, layer, cos_t, sa_t, sb_t, tm):
    rows = x.shape[0]
    tab_blocks = cos_t.shape[0] // tm
    in_w = w_bf.shape[2]
    row_spec = lambda w: pl.BlockSpec((tm, w), lambda i: (i, 0))
    tab_spec = pl.BlockSpec((tm, LANES), lambda i: (i % tab_blocks, 0))
    return pl.pallas_call(
        _inproj_body,
        grid=(rows // tm,),
        in_specs=[row_spec(D_MODEL), _layer_spec((D_MODEL, in_w), layer), tab_spec, tab_spec, tab_spec],
        out_specs=[row_spec(ATTN_WIDTH), row_spec(KV_WIDTH), row_spec(KV_WIDTH), row_spec(SSM_WIDTH)],
        out_shape=[jax.ShapeDtypeStruct((rows, ATTN_WIDTH), BF16),
                   jax.ShapeDtypeStruct((rows, KV_WIDTH), F32),
                   jax.ShapeDtypeStruct((rows, KV_WIDTH), F32),
                   jax.ShapeDtypeStruct((rows, SSM_WIDTH), F32)],
        compiler_params=_cparams(("arbitrary",)),
        name="inproj_rope",
    )(x, w_bf, cos_t, sa_t, sb_t)


def _rope_tables(positions):
    half = HEAD_DIM // 2
    inv_freq = ROPE_THETA ** (-np.arange(half, dtype=np.float64) / half)
    ang = positions.astype(np.float64)[:, None] * inv_freq[None, :]
    lane = np.arange(LANES)
    cos = np.cos(ang)[:, lane % half]
    sin = np.sin(ang)[:, lane % half]
    first = (lane % HEAD_DIM) < half
    sa = np.where(first[None, :], -sin, 0.0)
    sb = np.where(first[None, :], 0.0, sin)
    return (jnp.asarray(cos, F32), jnp.asarray(sa, F32), jnp.asarray(sb, F32))


def _rms(x, g):
    return x * lax.rsqrt(jnp.mean(jnp.square(x), axis=-1, keepdims=True) + RMS_EPS) * g


def _attn_prompt_body(sink_ref, q_ref, kp_ref, kc_ref, vp_ref, vc_ref, g_ref, o_ref, acc_ref, *, blocks_per_seq):
    i = pl.program_id(0)
    has_prev = (i % blocks_per_seq) > 0
    kk = jnp.concatenate([kp_ref[...], kc_ref[...]], axis=0).astype(BF16)
    vt = jnp.concatenate([vp_ref[...], vc_ref[...]], axis=0).T.astype(BF16)
    ncol = Q_PER_KV * WINDOW
    key = lax.broadcasted_iota(jnp.int32, (2 * WINDOW, ncol), 0)
    col = lax.broadcasted_iota(jnp.int32, (2 * WINDOW, ncol), 1)
    qry = col % WINDOW
    visible = (key > qry) & (key <= qry + WINDOW) & ((key >= WINDOW) | has_prev)
    head = lax.broadcasted_iota(jnp.int32, (1, ncol), 1) // WINDOW
    q = q_ref[...] * (HEAD_DIM ** -0.5)
    scores = []
    for g in range(N_KV_HEADS):
        qg = jnp.concatenate([q[:, (g * Q_PER_KV + j) * HEAD_DIM:(g * Q_PER_KV + j + 1) * HEAD_DIM]
                              for j in range(Q_PER_KV)], axis=0)
        kh = kk[:, g * HEAD_DIM:(g + 1) * HEAD_DIM]
        scores.append(lax.dot_general(kh, qg, (((1,), (1,)), ((), ())), preferred_element_type=F32))
    for g in range(N_KV_HEADS):
        st = jnp.where(visible, scores[g], -jnp.inf)
        sink = jnp.zeros((1, ncol), F32)
        for j in range(Q_PER_KV):
            sink = jnp.where(head == j, sink_ref[g * Q_PER_KV + j], sink)
        m = jnp.maximum(jnp.max(st, axis=0, keepdims=True), sink)
        p = jnp.exp(st - m)
        denom = jnp.sum(p, axis=0, keepdims=True) + jnp.exp(sink - m)
        ot = jnp.dot(vt[g * HEAD_DIM:(g + 1) * HEAD_DIM, :], p.astype(BF16), preferred_element_type=F32)
        ot = ot / denom
        for j in range(Q_PER_KV):
            h = g * Q_PER_KV + j
            acc_ref[h * HEAD_DIM:(h + 1) * HEAD_DIM, :] = ot[:, j * WINDOW:(j + 1) * WINDOW]
    o_ref[...] = _rms(acc_ref[...].T, g_ref[...]).astype(BF16)


def _attn_prompt(sinks, q, k, v, g, seq):
    rows = q.shape[0]
    nblk = rows // WINDOW
    bps = seq // WINDOW
    cur = lambda w: pl.BlockSpec((WINDOW, w), lambda i: (i, 0))
    prev = lambda w: pl.BlockSpec((WINDOW, w), lambda i: (jnp.maximum(i - 1, 0), 0))
    return pl.pallas_call(
        functools.partial(_attn_prompt_body, blocks_per_seq=bps),
        grid=(nblk,),
        in_specs=[pl.BlockSpec(memory_space=pltpu.SMEM), cur(ATTN_WIDTH), prev(KV_WIDTH), cur(KV_WIDTH),
                  prev(KV_WIDTH), cur(KV_WIDTH), _full((1, ATTN_WIDTH))],
        out_specs=cur(ATTN_WIDTH),
        out_shape=jax.ShapeDtypeStruct((rows, ATTN_WIDTH), BF16),
        scratch_shapes=[pltpu.VMEM((ATTN_WIDTH, WINDOW), F32)],
        compiler_params=_cparams(("arbitrary",)),
        name="attn_prompt",
    )(sinks, q, k, k, v, v, g)


SEQ_PER_STEP = 8


def _attn_sample_body(sink_ref, q_ref, kn_ref, vn_ref, ck_ref, cv_ref, g_ref, o_ref, nk_ref, nv_ref, acc_ref,
                      *, dec_seq):
    nq = SEQ_PER_STEP * dec_seq
    wb = ck_ref.shape[1]
    nkc = SEQ_PER_STEP * wb
    ck = ck_ref[...].reshape(nkc, KV_WIDTH)
    cv = cv_ref[...].reshape(nkc, KV_WIDTH)
    kn = kn_ref[...]
    vn = vn_ref[...]
    ckb = ck.astype(BF16)
    cvb = cv.astype(BF16)
    knb = kn.astype(BF16)
    vnb = vn.astype(BF16)
    q = q_ref[...]

    rows = Q_PER_KV * nq
    r = lax.broadcasted_iota(jnp.int32, (rows, nkc), 0) % nq
    c = lax.broadcasted_iota(jnp.int32, (rows, nkc), 1)
    vis_c = ((r // dec_seq) == (c // wb)) & ((c % wb) >= (r % dec_seq) + 1 + (wb - WINDOW))
    rn = lax.broadcasted_iota(jnp.int32, (rows, nq), 0) % nq
    cn = lax.broadcasted_iota(jnp.int32, (rows, nq), 1)
    vis_n = ((rn // dec_seq) == (cn // dec_seq)) & ((cn % dec_seq) <= (rn % dec_seq))
    hrow = lax.broadcasted_iota(jnp.int32, (rows, 1), 0) // nq

    dn = (((1,), (1,)), ((), ()))
    for g in range(N_KV_HEADS):
        qs = jnp.concatenate([q[:, (g * Q_PER_KV + j) * HEAD_DIM:(g * Q_PER_KV + j + 1) * HEAD_DIM]
                              for j in range(Q_PER_KV)], axis=0)
        sl = slice(g * HEAD_DIM, (g + 1) * HEAD_DIM)
        sc = lax.dot_general(qs, ckb[:, sl], dn, preferred_element_type=F32) * (HEAD_DIM ** -0.5)
        sn = lax.dot_general(qs, knb[:, sl], dn, preferred_element_type=F32) * (HEAD_DIM ** -0.5)
        sc = jnp.where(vis_c, sc, -jnp.inf)
        sn = jnp.where(vis_n, sn, -jnp.inf)
        sink = jnp.zeros((rows, 1), F32)
        for j in range(Q_PER_KV):
            sink = jnp.where(hrow == j, sink_ref[g * Q_PER_KV + j], sink)
        m = jnp.maximum(jnp.maximum(jnp.max(sc, axis=-1, keepdims=True), jnp.max(sn, axis=-1, keepdims=True)), sink)
        pc = jnp.exp(sc - m)
        pn = jnp.exp(sn - m)
        denom = jnp.sum(pc, axis=-1, keepdims=True) + jnp.sum(pn, axis=-1, keepdims=True) + jnp.exp(sink - m)
        o = (jnp.dot(pc.astype(BF16), cvb[:, sl], preferred_element_type=F32)
             + jnp.dot(pn.astype(BF16), vnb[:, sl], preferred_element_type=F32)) / denom
        for j in range(Q_PER_KV):
            h = g * Q_PER_KV + j
            acc_ref[:, h * HEAD_DIM:(h + 1) * HEAD_DIM] = o[j * nq:(j + 1) * nq, :]
    o_ref[...] = _rms(acc_ref[...], g_ref[...]).astype(BF16)

    keep = wb - dec_seq
    for s in range(SEQ_PER_STEP):
        nk_ref[s, 0:keep, :] = ck_ref[s, dec_seq:wb, :]
        nv_ref[s, 0:keep, :] = cv_ref[s, dec_seq:wb, :]
        nk_ref[s, keep:wb, :] = kn[s * dec_seq:(s + 1) * dec_seq, :]
        nv_ref[s, keep:wb, :] = vn[s * dec_seq:(s + 1) * dec_seq, :]


def _attn_sample(sinks, q, kn, vn, ck, cv, g, dec_seq, layer):
    rows = q.shape[0]
    _, nseq, wb, _ = ck.shape
    nq = SEQ_PER_STEP * dec_seq
    rowb = lambda w: pl.BlockSpec((nq, w), lambda i: (i, 0))
    cin = pl.BlockSpec((None, SEQ_PER_STEP, wb, KV_WIDTH), lambda i: (layer, i, 0, 0))
    cb = pl.BlockSpec((SEQ_PER_STEP, wb, KV_WIDTH), lambda i: (i, 0, 0))
    return pl.pallas_call(
        functools.partial(_attn_sample_body, dec_seq=dec_seq),
        grid=(nseq // SEQ_PER_STEP,),
        in_specs=[pl.BlockSpec(memory_space=pltpu.SMEM), rowb(ATTN_WIDTH), rowb(KV_WIDTH), rowb(KV_WIDTH),
                  cin, cin, _full((1, ATTN_WIDTH))],
        out_specs=[rowb(ATTN_WIDTH), cb, cb],
        out_shape=[jax.ShapeDtypeStruct((rows, ATTN_WIDTH), BF16),
                   jax.ShapeDtypeStruct(ck.shape[1:], F32), jax.ShapeDtypeStruct(cv.shape[1:], F32)],
        scratch_shapes=[pltpu.VMEM((nq, ATTN_WIDTH), F32)],
        compiler_params=_cparams(("arbitrary",)),
        name="attn_sample",
    )(sinks, q, kn, vn, ck, cv, g)


def _ssm_prompt_body(ua_ref, uc_ref, b_ref, c_ref, lr_ref, li_ref, pr_ref, pi_ref, d_ref, y_ref, hl_ref,
                     uperm_ref, bu0, bu1, hb0, hb1, yp_ref, carry_ref, *, nchunks, ntc):
    k = pl.program_id(0)
    bus = (bu0, bu1)
    hbs = (hb0, hb1)
    lanes_per_chunk = CH_IN // LANES
    seg = lambda s: slice(s * SCAN_SEG, (s + 1) * SCAN_SEG)
    lane = lambda c: slice(c * LANES, (c + 1) * LANES)

    def stage_a(item, slot):
        cc = item % SSM_CHUNKS
        for c in range(lanes_per_chunk):
            for s in range(SUBLANES):
                uperm_ref[c, pl.ds(s, SCAN_SEG, stride=SUBLANES), :] = ua_ref[seg(s), lane(c)]
        up = jnp.concatenate([uperm_ref[c] for c in range(lanes_per_chunk)], axis=1)
        bus[slot][...] = jnp.dot(up.astype(BF16), b_ref[cc], preferred_element_type=F32)

    def stage_b(item, slot):
        cc = item % SSM_CHUNKS
        tc = (item // SSM_CHUNKS) % ntc
        bu_ref = bus[slot]
        hb_ref = hbs[slot]
        ar = jnp.broadcast_to(lr_ref[cc], (SUBLANES, CH_STATES))
        ai = jnp.broadcast_to(li_ref[cc], (SUBLANES, CH_STATES))

        def advance(h, j):
            hr, hi = h
            br = bu_ref[j * SUBLANES:(j + 1) * SUBLANES, 0:CH_STATES]
            bi = bu_ref[j * SUBLANES:(j + 1) * SUBLANES, CH_STATES:2 * CH_STATES]
            return ar * hr + (br - ai * hi), ar * hi + (bi + ai * hr)

        zero = jnp.zeros((SUBLANES, CH_STATES), F32)
        h = (zero, zero)
        for j in range(SCAN_SEG):
            h = advance(h, j)
        fr, fi = h

        cst = jnp.where(tc == 0, 0.0, carry_ref[cc])
        c_r = cst[0:1, :]
        c_i = cst[1:2, :]
        lsr = pr_ref[cc]
        lsi = pi_ref[cc]
        rows_r, rows_i = [], []
        for s in range(SUBLANES):
            rows_r.append(c_r)
            rows_i.append(c_i)
            n_r = fr[s:s + 1, :] + (lsr * c_r - lsi * c_i)
            n_i = fi[s:s + 1, :] + (lsr * c_i + lsi * c_r)
            c_r, c_i = n_r, n_i
        carry_ref[cc] = jnp.concatenate([c_r, c_i], axis=0)
        hl_ref[0, cc] = jnp.concatenate([c_r, c_i], axis=0)

        h = (jnp.concatenate(rows_r, axis=0), jnp.concatenate(rows_i, axis=0))
        for jj in range(SCAN_SEG // 2):
            h1 = advance(h, 2 * jj)
            h = advance(h1, 2 * jj + 1)
            rows = slice(2 * jj * SUBLANES, (2 * jj + 2) * SUBLANES)
            hb_ref[rows, 0:CH_STATES] = jnp.concatenate([h1[0], h[0]], axis=0).astype(BF16)
            hb_ref[rows, CH_STATES:2 * CH_STATES] = jnp.concatenate([h1[1], h[1]], axis=0).astype(BF16)

    def stage_c(item, slot):
        cc = item % SSM_CHUNKS
        yp = jnp.dot(hbs[slot][...], c_ref[cc], preferred_element_type=F32)
        d = d_ref[cc]
        for c in range(lanes_per_chunk):
            yp_ref[c] = yp[:, lane(c)]
        for s in range(SUBLANES):
            for c in range(lanes_per_chunk):
                y_ref[seg(s), lane(c)] = (yp_ref[c, pl.ds(s, SCAN_SEG, stride=SUBLANES), :]
                                          + d[:, lane(c)] * uc_ref[seg(s), lane(c)])

    def run(do_a, do_b, do_c, parity):
        if do_a:
            stage_a(k, parity)
        if do_b:
            stage_b(k - 1, 1 - parity)
        if do_c:
            stage_c(k - 2, parity)

    @pl.when(k == 0)
    def _():
        carry_ref[...] = jnp.zeros(carry_ref.shape, F32)
        run(True, False, False, 0)

    @pl.when(k == 1)
    def _():
        run(True, True, False, 1)

    for parity in range(2):
        @pl.when((k >= 2) & (k < nchunks) & (k % 2 == parity))
        def _():
            run(True, True, True, parity)

    @pl.when(k == nchunks)
    def _():
        run(False, True, True, nchunks % 2)

    @pl.when(k == nchunks + 1)
    def _():
        run(False, False, True, (nchunks + 1) % 2)


def _ssm_prompt(u, sp, layer, nbatch, seq):
    rows = u.shape[0]
    ntc = seq // SCAN_T
    nchunks = nbatch * ntc * SSM_CHUNKS
    assert nchunks >= 2
    item_a = lambda k: jnp.minimum(k, nchunks - 1)
    item_b = lambda k: jnp.clip(k - 1, 0, nchunks - 1)
    item_c = lambda k: jnp.clip(k - 2, 0, nchunks - 1)
    chunk_block = lambda item: (item // SSM_CHUNKS, item % SSM_CHUNKS)
    y, hl = pl.pallas_call(
        functools.partial(_ssm_prompt_body, nchunks=nchunks, ntc=ntc),
        grid=(nchunks + 2,),
        in_specs=[pl.BlockSpec((SCAN_T, CH_IN), lambda k: chunk_block(item_a(k))),
                  pl.BlockSpec((SCAN_T, CH_IN), lambda k: chunk_block(item_c(k))),
                  ] + [_layer_spec(sp[n].shape[1:], layer) for n in ("B", "C", "lr", "li", "pr", "pi", "d")],
        out_specs=[pl.BlockSpec((SCAN_T, CH_IN), lambda k: chunk_block(item_c(k))),
                   pl.BlockSpec((1, SSM_CHUNKS, 2, CH_STATES),
                                lambda k: (item_b(k) // (ntc * SSM_CHUNKS), 0, 0, 0))],
        out_shape=[jax.ShapeDtypeStruct((rows, SSM_WIDTH), F32),
                   jax.ShapeDtypeStruct((nbatch, SSM_CHUNKS, 2, CH_STATES), F32)],
        scratch_shapes=[pltpu.VMEM((CH_IN // LANES, SCAN_T, LANES), F32),
                        pltpu.VMEM((SCAN_T, 2 * CH_STATES), F32), pltpu.VMEM((SCAN_T, 2 * CH_STATES), F32),
                        pltpu.VMEM((SCAN_T, 2 * CH_STATES), BF16), pltpu.VMEM((SCAN_T, 2 * CH_STATES), BF16),
                        pltpu.VMEM((CH_IN // LANES, SCAN_T, LANES), F32),
                        pltpu.VMEM((SSM_CHUNKS, 2, CH_STATES), F32)],
        compiler_params=_cparams(("arbitrary",)),
        name="ssm_prompt",
    )(u, u, sp["B"], sp["C"], sp["lr"], sp["li"], sp["pr"], sp["pi"], sp["d"])
    return y, hl


def _ssm_sample_body(u_ref, h0_ref, b_ref, c_ref, lr_ref, li_ref, d_ref, y_ref, ho_ref, bu_ref, *, nseq, dec_seq):
    nl = CH_STATES // LANES
    for k in range(SSM_CHUNKS):
        uk = u_ref[:, k * CH_IN:(k + 1) * CH_IN]
        bu = jnp.dot(uk.astype(BF16), b_ref[k], preferred_element_type=F32)
        for c in range(2 * nl):
            bu_ref[c] = bu[:, c * LANES:(c + 1) * LANES]
        base = k * 2 * CH_STATES
        hr = h0_ref[:, base:base + CH_STATES]
        hi = h0_ref[:, base + CH_STATES:base + 2 * CH_STATES]
        ar = lr_ref[k]
        ai = li_ref[k]
        for t in range(dec_seq):
            br = jnp.concatenate([bu_ref[c, pl.ds(t, nseq, stride=dec_seq), :] for c in range(nl)], axis=1)
            bi = jnp.concatenate([bu_ref[nl + c, pl.ds(t, nseq, stride=dec_seq), :] for c in range(nl)], axis=1)
            hr, hi = ar * hr + (br - ai * hi), ar * hi + (bi + ai * hr)
            for c in range(nl):
                bu_ref[c, pl.ds(t, nseq, stride=dec_seq), :] = hr[:, c * LANES:(c + 1) * LANES]
                bu_ref[nl + c, pl.ds(t, nseq, stride=dec_seq), :] = hi[:, c * LANES:(c + 1) * LANES]
        ho_ref[:, base:base + CH_STATES] = hr
        ho_ref[:, base + CH_STATES:base + 2 * CH_STATES] = hi
        hall = jnp.concatenate([bu_ref[c] for c in range(2 * nl)], axis=1)
        y_ref[:, k * CH_IN:(k + 1) * CH_IN] = (
            jnp.dot(hall.astype(BF16), c_ref[k], preferred_element_type=F32) + d_ref[k] * uk)


def _ssm_sample(u, h0, sp, layer, nseq, dec_seq):
    rows = u.shape[0]
    return pl.pallas_call(
        functools.partial(_ssm_sample_body, nseq=nseq, dec_seq=dec_seq),
        grid=(1,),
        in_specs=[_full(u.shape), _full(h0.shape)] + [_layer_spec(sp[n].shape[1:], layer)
                                                      for n in ("B", "C", "lr", "li", "d")],
        out_specs=[_full((rows, SSM_WIDTH)), _full(h0.shape)],
        out_shape=[jax.ShapeDtypeStruct((rows, SSM_WIDTH), F32), jax.ShapeDtypeStruct(h0.shape, F32)],
        scratch_shapes=[pltpu.VMEM((2 * CH_STATES // LANES, rows, LANES), F32)],
        compiler_params=_cparams(("arbitrary",)),
        name="ssm_sample",
    )(u, h0, sp["B"], sp["C"], sp["lr"], sp["li"], sp["d"])


def _ssm_params(lambda_re, lambda_im, log_dt, b_re, b_im, c_re, c_im, d):
    lam = lax.complex(jnp.minimum(lambda_re.astype(F32), -1e-4), lambda_im.astype(F32))
    dt = jnp.exp(log_dt.astype(F32))[:, None]
    lam_dt = lam * dt
    lam_bar = jnp.exp(lam_dt)
    b = lax.complex(b_re.astype(F32), b_im.astype(F32))
    b_bar = ((lam_bar - 1.0) / lam)[..., None] * b
    def block_diag(rows, row_group, col_group):
        n, w = rows.shape
        tiled = jnp.tile(rows, (1, CH_GROUPS))
        own = (lax.broadcasted_iota(jnp.int32, tiled.shape, 0) // row_group) % CH_GROUPS
        col = lax.broadcasted_iota(jnp.int32, tiled.shape, 1) // col_group
        return jnp.where(own == col, tiled, 0.0)

    def pack_b(part):
        rows = part.transpose(0, 2, 1).reshape(SSM_GROUPS * SSM_GROUP, SSM_STATE)
        return block_diag(rows, SSM_GROUP, SSM_STATE).reshape(SSM_CHUNKS, CH_IN, CH_STATES)

    def pack_c(part):
        rows = part.transpose(0, 2, 1).reshape(SSM_GROUPS * SSM_STATE, SSM_GROUP)
        return block_diag(rows, SSM_STATE, SSM_GROUP).reshape(SSM_CHUNKS, CH_STATES, CH_IN)

    bmat = jnp.concatenate([pack_b(jnp.real(b_bar)), pack_b(jnp.imag(b_bar))], axis=2).astype(BF16)
    cmat = jnp.concatenate([pack_c(c_re.astype(F32)), pack_c(-c_im.astype(F32))], axis=1).astype(BF16)
    pw = jnp.exp(lam_dt * float(SCAN_SEG)).reshape(SSM_CHUNKS, 1, CH_STATES)
    lb = lam_bar.reshape(SSM_CHUNKS, 1, CH_STATES)
    return dict(B=bmat, C=cmat, lr=jnp.real(lb), li=jnp.imag(lb), pr=jnp.real(pw), pi=jnp.imag(pw),
                d=d.astype(F32).reshape(SSM_CHUNKS, 1, CH_IN))


def _pack_state(re, im):
    n = re.shape[0]
    st = jnp.stack([re.reshape(n, SSM_CHUNKS, CH_STATES), im.reshape(n, SSM_CHUNKS, CH_STATES)], axis=2)
    return st.reshape(n, SSM_CHUNKS * 2 * CH_STATES).astype(F32)


def _unpack_state(h, n):
    st = h.reshape(n, SSM_CHUNKS, 2, CH_GROUPS, SSM_STATE)
    return (st[:, :, 0].reshape(n, SSM_GROUPS, SSM_STATE), st[:, :, 1].reshape(n, SSM_GROUPS, SSM_STATE))


def _glu_body(y_ref, w_ref, g_ref, o_ref):
    a = jax.nn.gelu(y_ref[...])
    z = a * jax.nn.sigmoid(jnp.dot(a.astype(BF16), w_ref[...], preferred_element_type=F32))
    o_ref[...] = _rms(z, g_ref[...]).astype(BF16)


def _glu(y, w_bf, layer, g, tm):
    rows = y.shape[0]
    spec = pl.BlockSpec((tm, SSM_WIDTH), lambda i: (i, 0))
    return pl.pallas_call(
        _glu_body,
        grid=(rows // tm,),
        in_specs=[spec, _layer_spec((SSM_WIDTH, SSM_WIDTH), layer), _full((1, SSM_WIDTH))],
        out_specs=spec,
        out_shape=jax.ShapeDtypeStruct((rows, SSM_WIDTH), BF16),
        compiler_params=_cparams(("arbitrary",)),
        name="glu_rms",
    )(y, w_bf, g)


def _layer_norm(y, g, b):
    mu = jnp.mean(y, axis=-1, keepdims=True)
    yc = y - mu
    var = jnp.mean(jnp.square(yc), axis=-1, keepdims=True)
    return yc * lax.rsqrt(var + LN_EPS) * g + b


OUTPROJ_SUB = 256


def _outproj_body(*refs, aliased, nblk):
    if aliased:
        refs = refs[1:]
    a_ref, s_ref, x_ref, w_ref, g_ref, b_ref, wrc_ref, br_ref, o_ref, lg_ref = refs
    i = pl.program_id(0)

    @pl.when(i < nblk)
    def _():
        tm = a_ref.shape[0]
        sub = min(tm, OUTPROJ_SUB)
        for r0 in range(0, tm, sub):
            rows = slice(r0, r0 + sub)
            acc = jnp.dot(a_ref[rows, :], w_ref[0:ATTN_WIDTH, :], preferred_element_type=F32)
            acc = acc + jnp.dot(s_ref[rows, :], w_ref[ATTN_WIDTH:, :], preferred_element_type=F32)
            x1 = _layer_norm(DEEPNORM_ALPHA * x_ref[rows, :] + acc, g_ref[...], b_ref[...])
            o_ref[rows, :] = x1
            hh = jnp.dot(x1.astype(BF16), wrc_ref[...], preferred_element_type=F32)
            lg_ref[rows, :] = hh[:, 0:LANES] + hh[:, LANES:] + br_ref[...]

    @pl.when(i >= nblk)
    def _():
        o_ref[...] = jnp.zeros(o_ref.shape, F32)


def _outproj(an, sn, x, w_bf, layer, g, b, wr, br, tm, total_rows, row_block0, x1_buf=None):
    rows = an.shape[0]
    aliased = x1_buf is not None
    nblk = rows // tm
    tail = 0 if aliased or rows == total_rows else 1
    blk = lambda i: jnp.minimum(i, nblk - 1)
    half = lambda: pl.BlockSpec((tm, ATTN_WIDTH), lambda i: (blk(i), 0))
    in_specs = [half(), half(), pl.BlockSpec((tm, D_MODEL), lambda i: (blk(i), 0)),
                _layer_spec((D_MODEL, D_MODEL), layer), _full((1, D_MODEL)), _full((1, D_MODEL)),
                _full((D_MODEL, 2 * LANES)), _full((1, LANES))]
    wr_hi = wr.astype(BF16)
    wr_lo = (wr - wr_hi.astype(F32)).astype(BF16)
    args = [an, sn, x, w_bf, g, b, jnp.concatenate([wr_hi, wr_lo], axis=1), br]
    if aliased:
        in_specs = [pl.BlockSpec(memory_space=pl.ANY)] + in_specs
        args = [x1_buf] + args
    return pl.pallas_call(
        functools.partial(_outproj_body, aliased=aliased, nblk=nblk),
        grid=(nblk + tail,),
        in_specs=in_specs,
        out_specs=[pl.BlockSpec((tm, D_MODEL), lambda i: (i + row_block0, 0)),
                   pl.BlockSpec((tm, LANES), lambda i: (blk(i), 0))],
        out_shape=[jax.ShapeDtypeStruct((total_rows, D_MODEL), F32), jax.ShapeDtypeStruct((rows, LANES), F32)],
        input_output_aliases={0: 0} if aliased else {},
        compiler_params=_cparams(("arbitrary",)),
        name="outproj_ln1",
    )(*args)


ROUTE_T = 640
R_E0, R_E1, R_W0, R_W1, R_P0, R_P1 = range(6)


def _route_select(x):
    lane = lax.broadcasted_iota(jnp.int32, x.shape, 1)
    big = jnp.int32(1 << 20)
    neg = -jnp.inf
    gmask = lane < N_EXPERT_GROUPS
    gm = jnp.max(jnp.where(gmask, x, neg), axis=-1, keepdims=True)
    grp = jnp.min(jnp.where(gmask & (x == gm), lane, big), axis=-1, keepdims=True)
    p_grp = 1.0 / jnp.sum(jnp.where(gmask, jnp.exp(x - gm), 0.0), axis=-1, keepdims=True)
    lo = N_EXPERT_GROUPS + grp * EXPERTS_PER_GROUP
    emask = (lane >= lo) & (lane < lo + EXPERTS_PER_GROUP)
    v1 = jnp.max(jnp.where(emask, x, neg), axis=-1, keepdims=True)
    i1 = jnp.min(jnp.where(emask & (x == v1), lane, big), axis=-1, keepdims=True)
    emask2 = emask & (lane != i1)
    v2 = jnp.max(jnp.where(emask2, x, neg), axis=-1, keepdims=True)
    i2 = jnp.min(jnp.where(emask2 & (x == v2), lane, big), axis=-1, keepdims=True)
    e2 = jnp.exp(v2 - v1)
    w1 = p_grp / (1.0 + e2)
    w2 = p_grp * e2 / (1.0 + e2)
    return (i1 - N_EXPERT_GROUPS).astype(F32), (i2 - N_EXPERT_GROUPS).astype(F32), w1, w2


def _route_plan_body(lg_ref, ltri_ref, utri_ref, o_ref, meta_ref, rec_ref, run_ref):
    ph = pl.program_id(0)
    j = pl.program_id(1)
    tm = lg_ref.shape[0]
    lane = lax.broadcasted_iota(jnp.int32, (tm, LANES), 1)
    lanef = lane.astype(F32)
    r0 = pl.multiple_of(j * tm, tm)

    @pl.when((ph == 0) & (j == 0))
    def _():
        run_ref[...] = jnp.zeros(run_ref.shape, F32)

    @pl.when(ph == 0)
    def _():
        e0, e1, w0, w1 = _route_select(lg_ref[...])
        oh0 = (lanef == e0).astype(F32)
        oh1 = (lanef == e1).astype(F32)
        oh2 = oh0 + oh1
        before = jnp.dot(ltri_ref[...], oh2.astype(BF16), preferred_element_type=F32) + run_ref[0:1, :]
        rk0 = jnp.sum(before * oh0, axis=-1, keepdims=True)
        rk1 = jnp.sum(before * oh1, axis=-1, keepdims=True)
        run_ref[...] = run_ref[...] + jnp.sum(oh2, axis=0, keepdims=True)
        rec = jnp.where(lane == R_E0, e0, 0.0)
        rec = jnp.where(lane == R_E1, e1, rec)
        rec = jnp.where(lane == R_W0, w0, rec)
        rec = jnp.where(lane == R_W1, w1, rec)
        rec = jnp.where(lane == R_P0, rk0, rec)
        rec = jnp.where(lane == R_P1, rk1, rec)
        rec_ref[pl.ds(r0, tm), :] = rec

    @pl.when(ph == 1)
    def _():
        counts = run_ref[...]
        nblk = jnp.floor((counts + (MOE_BLOCK - 1)) * (1.0 / MOE_BLOCK))
        end_blk = jnp.dot(nblk.astype(BF16), utri_ref[...], preferred_element_type=F32)
        start_row = (end_blk - nblk) * MOE_BLOCK
        rec = rec_ref[pl.ds(r0, tm), :]
        oh0 = (lanef == rec[:, R_E0:R_E0 + 1]).astype(F32)
        oh1 = (lanef == rec[:, R_E1:R_E1 + 1]).astype(F32)
        p0 = jnp.sum(oh0 * start_row[0:1, :], axis=-1, keepdims=True) + rec[:, R_P0:R_P0 + 1]
        p1 = jnp.sum(oh1 * start_row[0:1, :], axis=-1, keepdims=True) + rec[:, R_P1:R_P1 + 1]
        out = jnp.where(lane == R_P0, p0, rec)
        out = jnp.where(lane == R_P1, p1, out)
        o_ref[...] = out

        @pl.when(j == 0)
        def _():
            mrow = lax.broadcasted_iota(jnp.int32, meta_ref.shape, 0)
            meta_ref[...] = jnp.where(mrow == 1, counts, end_blk)


def _route_plan(logits):
    rows = logits.shape[0]
    tm = ROUTE_T
    ltri = jnp.asarray(np.tril(np.ones((tm, tm), np.float32), -1), BF16)
    utri = jnp.asarray(np.triu(np.ones((LANES, LANES), np.float32)), BF16)
    mrows = SUBLANES
    return pl.pallas_call(
        _route_plan_body,
        grid=(2, rows // tm),
        in_specs=[pl.BlockSpec((tm, LANES), lambda p, j: (j * (1 - p), 0)),
                  pl.BlockSpec((tm, tm), lambda p, j: (0, 0)),
                  pl.BlockSpec((LANES, LANES), lambda p, j: (0, 0))],
        out_specs=[pl.BlockSpec((tm, LANES), lambda p, j: (j * p, 0)),
                   pl.BlockSpec((mrows, LANES), lambda p, j: (0, 0))],
        out_shape=[jax.ShapeDtypeStruct((rows, LANES), F32), jax.ShapeDtypeStruct((mrows, LANES), F32)],
        scratch_shapes=[pltpu.VMEM((rows, LANES), F32), pltpu.VMEM((SUBLANES, LANES), F32)],
        compiler_params=_cparams(("arbitrary", "arbitrary")),
        name="route_plan",
    )(logits, ltri, utri)


def _build_rowmap(pos_ref, bend_ref, cnt_ref, o_ref):
    unroll = 8
    ntok = pos_ref.shape[0] // 2

    def zero_padding(e, _):
        first_blk = jnp.where(e == 0, 0, bend_ref[jnp.maximum(e - 1, 0)])

        def zero(r, _):
            o_ref[r] = 0
            return 0
        lax.fori_loop(first_blk * MOE_BLOCK + cnt_ref[e], bend_ref[e] * MOE_BLOCK, zero, 0)
        return 0
    lax.fori_loop(0, N_EXPERTS, zero_padding, 0)

    def fill(c, _):
        for u in range(unroll):
            t = c * unroll + u
            o_ref[pos_ref[2 * t]] = t
            o_ref[pos_ref[2 * t + 1]] = t
        return 0
    lax.fori_loop(0, ntok // unroll, fill, 0)


GATHER_SLOTS = 4


def _moe_body(bend_ref, cnt_ref, pos_ref, x_hbm, wg_ref, wu_ref, wd_ref, ys_hbm, *scratch, nblocks):
    xbufs = scratch[:GATHER_SLOTS]
    ybufs = scratch[GATHER_SLOTS:GATHER_SLOTS + 2]
    gsem, osem, wg_bf, wu_bf, wd_bf, row_tok_ref = scratch[GATHER_SLOTS + 2:]
    ahead = GATHER_SLOTS - 1
    e = pl.program_id(0)
    nused = bend_ref[N_EXPERTS - 1]
    b1 = bend_ref[e]
    b0 = jnp.where(e == 0, 0, bend_ref[jnp.maximum(e - 1, 0)])

    def gather_start(blk, s):
        base = blk * MOE_BLOCK
        for r in range(MOE_BLOCK):
            tok = row_tok_ref[base + r]
            pltpu.make_async_copy(x_hbm.at[pl.ds(tok, 1), :], xbufs[s].at[pl.ds(r, 1), :],
                                  gsem.at[s]).start(priority=r % 2)

    def gather_wait(s):
        pltpu.make_async_copy(x_hbm.at[pl.ds(0, MOE_BLOCK), :], xbufs[s], gsem.at[s]).wait()

    def out_copy(blk, s):
        row0 = pl.multiple_of(blk * MOE_BLOCK, MOE_BLOCK)
        return pltpu.make_async_copy(ybufs[s], ys_hbm.at[pl.ds(row0, MOE_BLOCK), :], osem.at[s])

    @pl.when(e == 0)
    def _():
        _build_rowmap(pos_ref, bend_ref, cnt_ref, row_tok_ref)
        for k in range(ahead):
            gather_start(k, k)

    @pl.when(b1 > b0)
    def _():
        wg_bf[...] = wg_ref[...].astype(BF16)
        wu_bf[...] = wu_ref[...].astype(BF16)
        wd_bf[...] = wd_ref[...].astype(BF16)

        def do_block(b, s):
            so = s % 2
            gather_wait(s)

            @pl.when(b >= 2)
            def _():
                out_copy(b - 2, so).wait()

            gather_start(jnp.minimum(b + ahead, nused - 1), (s + ahead) % GATHER_SLOTS)
            x = xbufs[s][...].astype(BF16)
            hg = jnp.dot(x, wg_bf[...], preferred_element_type=F32)
            hu = jnp.dot(x, wu_bf[...], preferred_element_type=F32)
            h = (jax.nn.silu(hg) * hu).astype(BF16)
            ybufs[so][...] = jnp.dot(h, wd_bf[...], preferred_element_type=F32)
            out_copy(b, so).start()

        def body(b, _):
            for s in range(GATHER_SLOTS):
                @pl.when(b % GATHER_SLOTS == s)
                def _():
                    do_block(b, s)
            return 0

        lax.fori_loop(b0, b1, body, 0)

    @pl.when(e == N_EXPERTS - 1)
    def _():
        last = nused - 1
        for s in range(GATHER_SLOTS):
            @pl.when(last % GATHER_SLOTS == s)
            def _():
                for k in range(1, GATHER_SLOTS):
                    gather_wait((s + k) % GATHER_SLOTS)
        out_copy(0, 0).wait()
        out_copy(0, 1).wait()
        yb0 = ybufs[0]
        yb0[...] = jnp.zeros(yb0.shape, F32)

        def zstart(b, _):
            out_copy(b, 0).start()
            return 0

        def zwait(b, _):
            out_copy(b, 0).wait()
            return 0

        lax.fori_loop(nused, nblocks, zstart, 0)
        lax.fori_loop(nused, nblocks, zwait, 0)


def _moe(bend, cnt, pos, x1, w_gate, w_up, w_down, layer, nblocks):
    assert 2 * x1.shape[0] > GATHER_SLOTS * MOE_BLOCK
    wspec = lambda a, b: pl.BlockSpec((None, None, a, b), lambda e, be, cn, ps: (layer, e, 0, 0))
    grid_spec = pltpu.PrefetchScalarGridSpec(
        num_scalar_prefetch=3,
        grid=(N_EXPERTS,),
        in_specs=[pl.BlockSpec(memory_space=pl.ANY), wspec(D_MODEL, EXPERT_FF), wspec(D_MODEL, EXPERT_FF),
                  wspec(EXPERT_FF, D_MODEL)],
        out_specs=pl.BlockSpec(memory_space=pl.ANY),
        scratch_shapes=([pltpu.VMEM((MOE_BLOCK, D_MODEL), F32) for _ in range(GATHER_SLOTS + 2)]
                        + [pltpu.SemaphoreType.DMA((GATHER_SLOTS,)), pltpu.SemaphoreType.DMA((2,)),
                           pltpu.VMEM((D_MODEL, EXPERT_FF), BF16),
                           pltpu.VMEM((D_MODEL, EXPERT_FF), BF16),
                           pltpu.VMEM((EXPERT_FF, D_MODEL), BF16),
                           pltpu.SMEM((nblocks * MOE_BLOCK,), jnp.int32)]),
    )
    return pl.pallas_call(
        functools.partial(_moe_body, nblocks=nblocks),
        grid_spec=grid_spec,
        out_shape=jax.ShapeDtypeStruct((nblocks * MOE_BLOCK, D_MODEL), F32),
        compiler_params=_cparams(("arbitrary",), bounds_checks=False),
        name="moe_experts",
    )(bend, cnt, pos, x1, w_gate, w_up, w_down)


COMB_T = 128


def _combine_body(pos_ref, ys_hbm, x_ref, rt_ref, g_ref, b_ref, op_ref, os_ref, *scratch, n_prompt_blocks):
    i = pl.program_id(0)
    nsteps = pl.num_programs(0)
    bufs = tuple((scratch[2 * s], scratch[2 * s + 1]) for s in range(GATHER_SLOTS))
    sem = scratch[2 * GATHER_SLOTS]
    ahead = GATHER_SLOTS - 1

    def gather_start(step, s):
        base = step * (2 * COMB_T)
        for r in range(COMB_T):
            for k in range(2):
                p = pos_ref[base + 2 * r + k]
                pltpu.make_async_copy(ys_hbm.at[pl.ds(p, 1), :], bufs[s][k].at[pl.ds(r, 1), :],
                                      sem.at[s]).start(priority=k)

    def gather_wait(s):
        for k in range(2):
            pltpu.make_async_copy(ys_hbm.at[pl.ds(0, COMB_T), :], bufs[s][k], sem.at[s]).wait()

    @pl.when(i == 0)
    def _():
        for k in range(ahead):
            gather_start(k, k)

    def do_tile(s):
        gather_wait(s)
        gather_start(jnp.minimum(i + ahead, nsteps - 1), (s + ahead) % GATHER_SLOTS)
        rt = rt_ref[...]
        ff = rt[:, R_W0:R_W0 + 1] * bufs[s][0][...] + rt[:, R_W1:R_W1 + 1] * bufs[s][1][...]
        out = _layer_norm(DEEPNORM_ALPHA * x_ref[...] + ff, g_ref[...], b_ref[...])

        @pl.when(i < n_prompt_blocks)
        def _():
            op_ref[...] = out

        @pl.when(i >= n_prompt_blocks)
        def _():
            os_ref[...] = out

        @pl.when(i == nsteps - 1)
        def _():
            for k in range(1, GATHER_SLOTS):
                gather_wait((s + k) % GATHER_SLOTS)

    for s in range(GATHER_SLOTS):
        @pl.when(i % GATHER_SLOTS == s)
        def _():
            do_tile(s)


def _combine(pos, ys, x1, route, g, b, n_prompt_rows):
    rows = x1.shape[0]
    npb = n_prompt_rows // COMB_T
    assert rows // COMB_T >= GATHER_SLOTS
    grid_spec = pltpu.PrefetchScalarGridSpec(
        num_scalar_prefetch=1,
        grid=(rows // COMB_T,),
        in_specs=[pl.BlockSpec(memory_space=pl.ANY),
                  pl.BlockSpec((COMB_T, D_MODEL), lambda i, p: (i, 0)),
                  pl.BlockSpec((COMB_T, LANES), lambda i, p: (i, 0)),
                  pl.BlockSpec((1, D_MODEL), lambda i, p: (0, 0)),
                  pl.BlockSpec((1, D_MODEL), lambda i, p: (0, 0))],
        out_specs=[pl.BlockSpec((COMB_T, D_MODEL), lambda i, p: (jnp.minimum(i, npb - 1), 0)),
                   pl.BlockSpec((COMB_T, D_MODEL), lambda i, p: (jnp.maximum(i - npb, 0), 0))],
        scratch_shapes=([pltpu.VMEM((COMB_T, D_MODEL), F32) for _ in range(2 * GATHER_SLOTS)]
                        + [pltpu.SemaphoreType.DMA((GATHER_SLOTS,))]),
    )
    return pl.pallas_call(
        functools.partial(_combine_body, n_prompt_blocks=npb),
        grid_spec=grid_spec,
        out_shape=[jax.ShapeDtypeStruct((n_prompt_rows, D_MODEL), F32),
                   jax.ShapeDtypeStruct((rows - n_prompt_rows, D_MODEL), F32)],
        compiler_params=_cparams(("arbitrary",), bounds_checks=False),
        name="combine_ln2",
    )(pos, ys, x1, route, g, b)


def kernel(x_prompt, x_sample, cache_k, cache_v, state_ssm_re, state_ssm_im, w_in, ssm_lambda_re, ssm_lambda_im, ssm_log_dt, ssm_b_re, ssm_b_im, ssm_c_re, ssm_c_im, ssm_d, ssm_w_glu, attn_sinks, attn_norm_g, ssm_norm_g, w_out, ln1_g, ln1_b, router_group_w, router_group_b, router_expert_w, router_expert_b, expert_w_gate, expert_w_up, expert_w_down, ln2_g, ln2_b):
    nbatch, seq, _ = x_prompt.shape
    nseq, dec_seq, _ = x_sample.shape
    wbuf = cache_k.shape[2]
    rows_p = nbatch * seq
    rows_s = nseq * dec_seq
    rows_all = rows_p + rows_s
    tm_p = 512

    xp = x_prompt.reshape(rows_p, D_MODEL)
    xs = x_sample.reshape(rows_s, D_MODEL)
    tab_p = _rope_tables(np.arange(seq))
    tab_s = _rope_tables(PAST_LEN + (np.arange(rows_s) % dec_seq))
    ck_all = cache_k.astype(F32).reshape(DEPTH, nseq, wbuf, KV_WIDTH)
    cv_all = cache_v.astype(F32).reshape(DEPTH, nseq, wbuf, KV_WIDTH)

    sp = jax.vmap(_ssm_params)(ssm_lambda_re, ssm_lambda_im, ssm_log_dt, ssm_b_re, ssm_b_im,
                               ssm_c_re, ssm_c_im, ssm_d)
    w_in_bf = w_in.astype(BF16)
    w_glu_bf = ssm_w_glu.astype(BF16)
    w_out_bf = w_out.astype(BF16)

    outs = {k: [] for k in ("kp", "vp", "hrp", "hip", "ks", "vs", "hrs", "his")}
    for l in range(DEPTH):
        sinks = attn_sinks[l].astype(F32)
        attn_g = attn_norm_g[l].astype(F32).reshape(1, ATTN_WIDTH)
        ssm_g = ssm_norm_g[l].astype(F32).reshape(1, SSM_WIDTH)
        g1 = ln1_g[l].astype(F32).reshape(1, D_MODEL)
        b1 = ln1_b[l].astype(F32).reshape(1, D_MODEL)
        g2 = ln2_g[l].astype(F32).reshape(1, D_MODEL)
        b2 = ln2_b[l].astype(F32).reshape(1, D_MODEL)
        n_rt = N_EXPERT_GROUPS + N_EXPERTS
        wr = jnp.concatenate([router_group_w[l].astype(F32)]
                             + [router_expert_w[l, g].astype(F32) for g in range(N_EXPERT_GROUPS)]
                             + [jnp.zeros((D_MODEL, LANES - n_rt), F32)], axis=1)
        br = jnp.concatenate([router_group_b[l].astype(F32), router_expert_b[l].astype(F32).reshape(-1),
                              jnp.zeros((LANES - n_rt,), F32)]).reshape(1, LANES)

        q_p, k_p, v_p, u_p = _inproj(xp, w_in_bf, l, *tab_p, tm=tm_p)
        q_s, k_s, v_s, u_s = _inproj(xs, w_in_bf, l, *tab_s, tm=rows_s)

        an_p = _attn_prompt(sinks, q_p, k_p, v_p, attn_g, seq)
        an_s, nk_s, nv_s = _attn_sample(sinks, q_s, k_s, v_s, ck_all, cv_all, attn_g, dec_seq, l)

        y_p, hl_p = _ssm_prompt(u_p, sp, l, nbatch, seq)
        y_s, hl_s = _ssm_sample(u_s, _pack_state(state_ssm_re[l], state_ssm_im[l]), sp, l, nseq, dec_seq)
        sn_p = _glu(y_p, w_glu_bf, l, ssm_g, tm_p)
        sn_s = _glu(y_s, w_glu_bf, l, ssm_g, rows_s)

        x1, lg_p = _outproj(an_p, sn_p, xp, w_out_bf, l, g1, b1, wr, br, tm_p, rows_all, 0)
        x1, lg_s = _outproj(an_s, sn_s, xs, w_out_bf, l, g1, b1, wr, br, rows_s, rows_all, rows_p // rows_s,
                            x1_buf=x1)
        nblocks = 2 * rows_all // MOE_BLOCK + N_EXPERTS
        route, meta = _route_plan(jnp.concatenate([lg_p, lg_s], axis=0))
        pos = route[:, R_P0:R_P1 + 1].astype(jnp.int32).reshape(2 * rows_all)
        bend = meta[0, :N_EXPERTS].astype(jnp.int32)

        cnt = meta[1, :N_EXPERTS].astype(jnp.int32)
        ys = _moe(bend, cnt, pos, x1, expert_w_gate, expert_w_up, expert_w_down, l, nblocks)
        xp, xs = _combine(pos, ys, x1, route, g2, b2, rows_p)

        wp = min(WINDOW, seq)
        tail = lambda a: a.reshape(nbatch, seq, KV_WIDTH)[:, seq - wp:].reshape(nbatch, wp, N_KV_HEADS, HEAD_DIM)
        outs["kp"].append(tail(k_p))
        outs["vp"].append(tail(v_p))
        hp = hl_p.reshape(nbatch, SSM_CHUNKS * 2 * CH_STATES)
        hr, hi = _unpack_state(hp, nbatch)
        outs["hrp"].append(hr)
        outs["hip"].append(hi)
        outs["ks"].append(nk_s.reshape(nseq, wbuf, N_KV_HEADS, HEAD_DIM))
        outs["vs"].append(nv_s.reshape(nseq, wbuf, N_KV_HEADS, HEAD_DIM))
        hr, hi = _unpack_state(hl_s, nseq)
        outs["hrs"].append(hr)
        outs["his"].append(hi)

    st = {k: jnp.stack(v) for k, v in outs.items()}
    return (xp.reshape(nbatch, seq, D_MODEL), xs.reshape(nseq, dec_seq, D_MODEL),
            st["kp"], st["vp"], st["hrp"], st["hip"], st["ks"], st["vs"], st["hrs"], st["his"])
```

```python
import functools
import math

import numpy as np
import jax
import jax.numpy as jnp
from jax import lax
from jax.experimental import pallas as pl
from jax.experimental.pallas import tpu as pltpu

F32 = jnp.float32
BF16 = jnp.bfloat16

D_MODEL = 2048
DEPTH = 2
PAST_LEN = 16384
ATTN_WIDTH = 1024
SSM_WIDTH = 1024
HEAD_DIM = 64
N_HEADS = 16
N_KV_HEADS = 4
Q_PER_KV = 4
KV_WIDTH = 256
WINDOW = 128
ROPE_THETA = 10000.0
SSM_GROUP = 16
SSM_GROUPS = 64
SSM_STATE = 64
N_EXPERT_GROUPS = 4
EXPERTS_PER_GROUP = 8
N_EXPERTS = 32
EXPERT_FF = 512
MOE_BLOCK = 128
DEEPNORM_ALPHA = (2.0 * DEPTH) ** 0.25
LN_EPS = 1e-5
RMS_EPS = 1e-6

LANES = 128
SUBLANES = 8
VMEM_LIMIT = 56 * 1024 * 1024

SSM_CHUNKS = 4
CH_GROUPS = SSM_GROUPS // SSM_CHUNKS
CH_STATES = CH_GROUPS * SSM_STATE
CH_IN = CH_GROUPS * SSM_GROUP
SCAN_T = 512
SCAN_SEG = SCAN_T // SUBLANES


def _cparams(sem=None, vmem=VMEM_LIMIT, bounds_checks=True):
    return pltpu.CompilerParams(dimension_semantics=sem, vmem_limit_bytes=vmem,
                                disable_bounds_checks=not bounds_checks)


def _full(shape):
    n = len(shape)
    return pl.BlockSpec(shape, lambda *a: (0,) * n)


def _inproj_body(x_ref, w_ref, cos_ref, sa_ref, sb_ref, q_ref, k_ref, v_ref, u_ref):
    x = x_ref[...].astype(BF16)
    cos = cos_ref[...]
    sa = sa_ref[...]
    sb = sb_ref[...]

    def rope(c):
        return c * cos + pltpu.roll(c, LANES - 32, 1) * sa + pltpu.roll(c, 32, 1) * sb

    pq = jnp.dot(x, w_ref[:, 0:ATTN_WIDTH], preferred_element_type=F32)
    for j in range(ATTN_WIDTH // LANES):
        q_ref[:, j * LANES:(j + 1) * LANES] = rope(pq[:, j * LANES:(j + 1) * LANES]).astype(BF16)
    pk = jnp.dot(x, w_ref[:, ATTN_WIDTH:ATTN_WIDTH + KV_WIDTH], preferred_element_type=F32)
    for j in range(KV_WIDTH // LANES):
        k_ref[:, j * LANES:(j + 1) * LANES] = rope(pk[:, j * LANES:(j + 1) * LANES])
    v_ref[...] = jnp.dot(x, w_ref[:, ATTN_WIDTH + KV_WIDTH:ATTN_WIDTH + 2 * KV_WIDTH],
                         preferred_element_type=F32)
    u_ref[...] = jnp.dot(x, w_ref[:, ATTN_WIDTH + 2 * KV_WIDTH:], preferred_element_type=F32)


def _layer_spec(shape, layer):
    return pl.BlockSpec((None,) + tuple(shape), lambda *a: (layer,) + (0,) * len(shape))


def _inproj(x, w_bf, layer, cos_t, sa_t, sb_t, tm):
    rows = x.shape[0]
    tab_blocks = cos_t.shape[0] // tm
    in_w = w_bf.shape[2]
    row_spec = lambda w: pl.BlockSpec((tm, w), lambda i: (i, 0))
    tab_spec = pl.BlockSpec((tm, LANES), lambda i: (i % tab_blocks, 0))
    return pl.pallas_call(
        _inproj_body,
        grid=(rows // tm,),
        in_specs=[row_spec(D_MODEL), _layer_spec((D_MODEL, in_w), layer), tab_spec, tab_spec, tab_spec],
        out_specs=[row_spec(ATTN_WIDTH), row_spec(KV_WIDTH), row_spec(KV_WIDTH), row_spec(SSM_WIDTH)],
        out_shape=[jax.ShapeDtypeStruct((rows, ATTN_WIDTH), BF16),
                   jax.ShapeDtypeStruct((rows, KV_WIDTH), F32),
                   jax.ShapeDtypeStruct((rows, KV_WIDTH), F32),
                   jax.ShapeDtypeStruct((rows, SSM_WIDTH), F32)],
        compiler_params=_cparams(("arbitrary",)),
        name="inproj_rope",
    )(x, w_bf, cos_t, sa_t, sb_t)


def _rope_tables(positions):
    half = HEAD_DIM // 2
    inv_freq = ROPE_THETA ** (-np.arange(half, dtype=np.float64) / half)
    ang = positions.astype(np.float64)[:, None] * inv_freq[None, :]
    lane = np.arange(LANES)
    cos = np.cos(ang)[:, lane % half]
    sin = np.sin(ang)[:, lane % half]
    first = (lane % HEAD_DIM) < half
    sa = np.where(first[None, :], -sin, 0.0)
    sb = np.where(first[None, :], 0.0, sin)
    return (jnp.asarray(cos, F32), jnp.asarray(sa, F32), jnp.asarray(sb, F32))


def _rms(x, g):
    return x * lax.rsqrt(jnp.mean(jnp.square(x), axis=-1, keepdims=True) + RMS_EPS) * g


def _attn_prompt_body(sink_ref, q_ref, kp_ref, kc_ref, vp_ref, vc_ref, g_ref, o_ref, acc_ref, *, blocks_per_seq):
    i = pl.program_id(0)
    has_prev = (i % blocks_per_seq) > 0
    kk = jnp.concatenate([kp_ref[...], kc_ref[...]], axis=0).astype(BF16)
    vt = jnp.concatenate([vp_ref[...], vc_ref[...]], axis=0).T.astype(BF16)
    ncol = Q_PER_KV * WINDOW
    key = lax.broadcasted_iota(jnp.int32, (2 * WINDOW, ncol), 0)
    col = lax.broadcasted_iota(jnp.int32, (2 * WINDOW, ncol), 1)
    qry = col % WINDOW
    visible = (key > qry) & (key <= qry + WINDOW) & ((key >= WINDOW) | has_prev)
    head = lax.broadcasted_iota(jnp.int32, (1, ncol), 1) // WINDOW
    q = q_ref[...] * (HEAD_DIM ** -0.5)
    scores = []
    for g in range(N_KV_HEADS):
        qg = jnp.concatenate([q[:, (g * Q_PER_KV + j) * HEAD_DIM:(g * Q_PER_KV + j + 1) * HEAD_DIM]
                              for j in range(Q_PER_KV)], axis=0)
        kh = kk[:, g * HEAD_DIM:(g + 1) * HEAD_DIM]
        scores.append(lax.dot_general(kh, qg, (((1,), (1,)), ((), ())), preferred_element_type=F32))
    for g in range(N_KV_HEADS):
        st = jnp.where(visible, scores[g], -jnp.inf)
        sink = jnp.zeros((1, ncol), F32)
        for j in range(Q_PER_KV):
            sink = jnp.where(head == j, sink_ref[g * Q_PER_KV + j], sink)
        m = jnp.maximum(jnp.max(st, axis=0, keepdims=True), sink)
        p = jnp.exp(st - m)
        denom = jnp.sum(p, axis=0, keepdims=True) + jnp.exp(sink - m)
        ot = jnp.dot(vt[g * HEAD_DIM:(g + 1) * HEAD_DIM, :], p.astype(BF16), preferred_element_type=F32)
        ot = ot / denom
        for j in range(Q_PER_KV):
            h = g * Q_PER_KV + j
            acc_ref[h * HEAD_DIM:(h + 1) * HEAD_DIM, :] = ot[:, j * WINDOW:(j + 1) * WINDOW]
    o_ref[...] = _rms(acc_ref[...].T, g_ref[...]).astype(BF16)


def _attn_prompt(sinks, q, k, v, g, seq):
    rows = q.shape[0]
    nblk = rows // WINDOW
    bps = seq // WINDOW
    cur = lambda w: pl.BlockSpec((WINDOW, w), lambda i: (i, 0))
    prev = lambda w: pl.BlockSpec((WINDOW, w), lambda i: (jnp.maximum(i - 1, 0), 0))
    return pl.pallas_call(
        functools.partial(_attn_prompt_body, blocks_per_seq=bps),
        grid=(nblk,),
        in_specs=[pl.BlockSpec(memory_space=pltpu.SMEM), cur(ATTN_WIDTH), prev(KV_WIDTH), cur(KV_WIDTH),
                  prev(KV_WIDTH), cur(KV_WIDTH), _full((1, ATTN_WIDTH))],
        out_specs=cur(ATTN_WIDTH),
        out_shape=jax.ShapeDtypeStruct((rows, ATTN_WIDTH), BF16),
        scratch_shapes=[pltpu.VMEM((ATTN_WIDTH, WINDOW), F32)],
        compiler_params=_cparams(("arbitrary",)),
        name="attn_prompt",
    )(sinks, q, k, k, v, v, g)


SEQ_PER_STEP = 8


def _attn_sample_body(sink_ref, q_ref, kn_ref, vn_ref, ck_ref, cv_ref, g_ref, o_ref, nk_ref, nv_ref, acc_ref,
                      *, dec_seq):
    nq = SEQ_PER_STEP * dec_seq
    wb = ck_ref.shape[1]
    nkc = SEQ_PER_STEP * wb
    ck = ck_ref[...].reshape(nkc, KV_WIDTH)
    cv = cv_ref[...].reshape(nkc, KV_WIDTH)
    kn = kn_ref[...]
    vn = vn_ref[...]
    ckb = ck.astype(BF16)
    cvb = cv.astype(BF16)
    knb = kn.astype(BF16)
    vnb = vn.astype(BF16)
    q = q_ref[...]

    rows = Q_PER_KV * nq
    r = lax.broadcasted_iota(jnp.int32, (rows, nkc), 0) % nq
    c = lax.broadcasted_iota(jnp.int32, (rows, nkc), 1)
    vis_c = ((r // dec_seq) == (c // wb)) & ((c % wb) >= (r % dec_seq) + 1 + (wb - WINDOW))
    rn = lax.broadcasted_iota(jnp.int32, (rows, nq), 0) % nq
    cn = lax.broadcasted_iota(jnp.int32, (rows, nq), 1)
    vis_n = ((rn // dec_seq) == (cn // dec_seq)) & ((cn % dec_seq) <= (rn % dec_seq))
    hrow = lax.broadcasted_iota(jnp.int32, (rows, 1), 0) // nq

    dn = (((1,), (1,)), ((), ()))
    for g in range(N_KV_HEADS):
        qs = jnp.concatenate([q[:, (g * Q_PER_KV + j) * HEAD_DIM:(g * Q_PER_KV + j + 1) * HEAD_DIM]
                              for j in range(Q_PER_KV)], axis=0)
        sl = slice(g * HEAD_DIM, (g + 1) * HEAD_DIM)
        sc = lax.dot_general(qs, ckb[:, sl], dn, preferred_element_type=F32) * (HEAD_DIM ** -0.5)
        sn = lax.dot_general(qs, knb[:, sl], dn, preferred_element_type=F32) * (HEAD_DIM ** -0.5)
        sc = jnp.where(vis_c, sc, -jnp.inf)
        sn = jnp.where(vis_n, sn, -jnp.inf)
        sink = jnp.zeros((rows, 1), F32)
        for j in range(Q_PER_KV):
            sink = jnp.where(hrow == j, sink_ref[g * Q_PER_KV + j], sink)
        m = jnp.maximum(jnp.maximum(jnp.max(sc, axis=-1, keepdims=True), jnp.max(sn, axis=-1, keepdims=True)), sink)
        pc = jnp.exp(sc - m)
        pn = jnp.exp(sn - m)
        denom = jnp.sum(pc, axis=-1, keepdims=True) + jnp.sum(pn, axis=-1, keepdims=True) + jnp.exp(sink - m)
        o = (jnp.dot(pc.astype(BF16), cvb[:, sl], preferred_element_type=F32)
             + jnp.dot(pn.astype(BF16), vnb[:, sl], preferred_element_type=F32)) / denom
        for j in range(Q_PER_KV):
            h = g * Q_PER_KV + j
            acc_ref[:, h * HEAD_DIM:(h + 1) * HEAD_DIM] = o[j * nq:(j + 1) * nq, :]
    o_ref[...] = _rms(acc_ref[...], g_ref[...]).astype(BF16)

    keep = wb - dec_seq
    for s in range(SEQ_PER_STEP):
        nk_ref[s, 0:keep, :] = ck_ref[s, dec_seq:wb, :]
        nv_ref[s, 0:keep, :] = cv_ref[s, dec_seq:wb, :]
        nk_ref[s, keep:wb, :] = kn[s * dec_seq:(s + 1) * dec_seq, :]
        nv_ref[s, keep:wb, :] = vn[s * dec_seq:(s + 1) * dec_seq, :]


def _attn_sample(sinks, q, kn, vn, ck, cv, g, dec_seq, layer):
    rows = q.shape[0]
    _, nseq, wb, _ = ck.shape
    nq = SEQ_PER_STEP * dec_seq
    rowb = lambda w: pl.BlockSpec((nq, w), lambda i: (i, 0))
    cin = pl.BlockSpec((None, SEQ_PER_STEP, wb, KV_WIDTH), lambda i: (layer, i, 0, 0))
    cb = pl.BlockSpec((SEQ_PER_STEP, wb, KV_WIDTH), lambda i: (i, 0, 0))
    return pl.pallas_call(
        functools.partial(_attn_sample_body, dec_seq=dec_seq),
        grid=(nseq // SEQ_PER_STEP,),
        in_specs=[pl.BlockSpec(memory_space=pltpu.SMEM), rowb(ATTN_WIDTH), rowb(KV_WIDTH), rowb(KV_WIDTH),
                  cin, cin, _full((1, ATTN_WIDTH))],
        out_specs=[rowb(ATTN_WIDTH), cb, cb],
        out_shape=[jax.ShapeDtypeStruct((rows, ATTN_WIDTH), BF16),
                   jax.ShapeDtypeStruct(ck.shape[1:], F32), jax.ShapeDtypeStruct(cv.shape[1:], F32)],
        scratch_shapes=[pltpu.VMEM((nq, ATTN_WIDTH), F32)],
        compiler_params=_cparams(("arbitrary",)),
        name="attn_sample",
    )(sinks, q, kn, vn, ck, cv, g)


def _ssm_prompt_body(ua_ref, uc_ref, b_ref, c_ref, lr_ref, li_ref, pr_ref, pi_ref, d_ref, y_ref, hl_ref,
                     uperm_ref, bu0, bu1, hb0, hb1, yp_ref, carry_ref, *, nchunks, ntc):
    k = pl.program_id(0)
    bus = (bu0, bu1)
    hbs = (hb0, hb1)
    lanes_per_chunk = CH_IN // LANES
    seg = lambda s: slice(s * SCAN_SEG, (s + 1) * SCAN_SEG)
    lane = lambda c: slice(c * LANES, (c + 1) * LANES)

    def stage_a(item, slot):
        cc = item % SSM_CHUNKS
        for c in range(lanes_per_chunk):
            for s in range(SUBLANES):
                uperm_ref[c, pl.ds(s, SCAN_SEG, stride=SUBLANES), :] = ua_ref[seg(s), lane(c)]
        up = jnp.concatenate([uperm_ref[c] for c in range(lanes_per_chunk)], axis=1)
        bus[slot][...] = jnp.dot(up.astype(BF16), b_ref[cc], preferred_element_type=F32)

    def stage_b(item, slot):
        cc = item % SSM_CHUNKS
        tc = (item // SSM_CHUNKS) % ntc
        bu_ref = bus[slot]
        hb_ref = hbs[slot]
        ar = jnp.broadcast_to(lr_ref[cc], (SUBLANES, CH_STATES))
        ai = jnp.broadcast_to(li_ref[cc], (SUBLANES, CH_STATES))

        def advance(h, j):
            hr, hi = h
            br = bu_ref[j * SUBLANES:(j + 1) * SUBLANES, 0:CH_STATES]
            bi = bu_ref[j * SUBLANES:(j + 1) * SUBLANES, CH_STATES:2 * CH_STATES]
            return ar * hr + (br - ai * hi), ar * hi + (bi + ai * hr)

        zero = jnp.zeros((SUBLANES, CH_STATES), F32)
        h = (zero, zero)
        for j in range(SCAN_SEG):
            h = advance(h, j)
        fr, fi = h

        cst = jnp.where(tc == 0, 0.0, carry_ref[cc])
        c_r = cst[0:1, :]
        c_i = cst[1:2, :]
        lsr = pr_ref[cc]
        lsi = pi_ref[cc]
        rows_r, rows_i = [], []
        for s in range(SUBLANES):
            rows_r.append(c_r)
            rows_i.append(c_i)
            n_r = fr[s:s + 1, :] + (lsr * c_r - lsi * c_i)
            n_i = fi[s:s + 1, :] + (lsr * c_i + lsi * c_r)
            c_r, c_i = n_r, n_i
        carry_ref[cc] = jnp.concatenate([c_r, c_i], axis=0)
        hl_ref[0, cc] = jnp.concatenate([c_r, c_i], axis=0)

        h = (jnp.concatenate(rows_r, axis=0), jnp.concatenate(rows_i, axis=0))
        for jj in range(SCAN_SEG // 2):
            h1 = advance(h, 2 * jj)
            h = advance(h1, 2 * jj + 1)
            rows = slice(2 * jj * SUBLANES, (2 * jj + 2) * SUBLANES)
            hb_ref[rows, 0:CH_STATES] = jnp.concatenate([h1[0], h[0]], axis=0).astype(BF16)
            hb_ref[rows, CH_STATES:2 * CH_STATES] = jnp.concatenate([h1[1], h[1]], axis=0).astype(BF16)

    def stage_c(item, slot):
        cc = item % SSM_CHUNKS
        yp = jnp.dot(hbs[slot][...], c_ref[cc], preferred_element_type=F32)
        d = d_ref[cc]
        for c in range(lanes_per_chunk):
            yp_ref[c] = yp[:, lane(c)]
        for s in range(SUBLANES):
            for c in range(lanes_per_chunk):
                y_ref[seg(s), lane(c)] = (yp_ref[c, pl.ds(s, SCAN_SEG, stride=SUBLANES), :]
                                          + d[:, lane(c)] * uc_ref[seg(s), lane(c)])

    def run(do_a, do_b, do_c, parity):
        if do_a:
            stage_a(k, parity)
        if do_b:
            stage_b(k - 1, 1 - parity)
        if do_c:
            stage_c(k - 2, parity)

    @pl.when(k == 0)
    def _():
        carry_ref[...] = jnp.zeros(carry_ref.shape, F32)
        run(True, False, False, 0)

    @pl.when(k == 1)
    def _():
        run(True, True, False, 1)

    for parity in range(2):
        @pl.when((k >= 2) & (k < nchunks) & (k % 2 == parity))
        def _():
            run(True, True, True, parity)

    @pl.when(k == nchunks)
    def _():
        run(False, True, True, nchunks % 2)

    @pl.when(k == nchunks + 1)
    def _():
        run(False, False, True, (nchunks + 1) % 2)


def _ssm_prompt(u, sp, layer, nbatch, seq):
    rows = u.shape[0]
    ntc = seq // SCAN_T
    nchunks = nbatch * ntc * SSM_CHUNKS
    assert nchunks >= 2
    item_a = lambda k: jnp.minimum(k, nchunks - 1)
    item_b = lambda k: jnp.clip(k - 1, 0, nchunks - 1)
    item_c = lambda k: jnp.clip(k - 2, 0, nchunks - 1)
    chunk_block = lambda item: (item // SSM_CHUNKS, item % SSM_CHUNKS)
    y, hl = pl.pallas_call(
        functools.partial(_ssm_prompt_body, nchunks=nchunks, ntc=ntc),
        grid=(nchunks + 2,),
        in_specs=[pl.BlockSpec((SCAN_T, CH_IN), lambda k: chunk_block(item_a(k))),
                  pl.BlockSpec((SCAN_T, CH_IN), lambda k: chunk_block(item_c(k))),
                  ] + [_layer_spec(sp[n].shape[1:], layer) for n in ("B", "C", "lr", "li", "pr", "pi", "d")],
        out_specs=[pl.BlockSpec((SCAN_T, CH_IN), lambda k: chunk_block(item_c(k))),
                   pl.BlockSpec((1, SSM_CHUNKS, 2, CH_STATES),
                                lambda k: (item_b(k) // (ntc * SSM_CHUNKS), 0, 0, 0))],
        out_shape=[jax.ShapeDtypeStruct((rows, SSM_WIDTH), F32),
                   jax.ShapeDtypeStruct((nbatch, SSM_CHUNKS, 2, CH_STATES), F32)],
        scratch_shapes=[pltpu.VMEM((CH_IN // LANES, SCAN_T, LANES), F32),
                        pltpu.VMEM((SCAN_T, 2 * CH_STATES), F32), pltpu.VMEM((SCAN_T, 2 * CH_STATES), F32),
                        pltpu.VMEM((SCAN_T, 2 * CH_STATES), BF16), pltpu.VMEM((SCAN_T, 2 * CH_STATES), BF16),
                        pltpu.VMEM((CH_IN // LANES, SCAN_T, LANES), F32),
                        pltpu.VMEM((SSM_CHUNKS, 2, CH_STATES), F32)],
        compiler_params=_cparams(("arbitrary",)),
        name="ssm_prompt",
    )(u, u, sp["B"], sp["C"], sp["lr"], sp["li"], sp["pr"], sp["pi"], sp["d"])
    return y, hl


def _ssm_sample_body(u_ref, h0_ref, b_ref, c_ref, lr_ref, li_ref, d_ref, y_ref, ho_ref, bu_ref, *, nseq, dec_seq):
    nl = CH_STATES // LANES
    for k in range(SSM_CHUNKS):
        uk = u_ref[:, k * CH_IN:(k + 1) * CH_IN]
        bu = jnp.dot(uk.astype(BF16), b_ref[k], preferred_element_type=F32)
        for c in range(2 * nl):
            bu_ref[c] = bu[:, c * LANES:(c + 1) * LANES]
        base = k * 2 * CH_STATES
        hr = h0_ref[:, base:base + CH_STATES]
        hi = h0_ref[:, base + CH_STATES:base + 2 * CH_STATES]
        ar = lr_ref[k]
        ai = li_ref[k]
        for t in range(dec_seq):
            br = jnp.concatenate([bu_ref[c, pl.ds(t, nseq, stride=dec_seq), :] for c in range(nl)], axis=1)
            bi = jnp.concatenate([bu_ref[nl + c, pl.ds(t, nseq, stride=dec_seq), :] for c in range(nl)], axis=1)
            hr, hi = ar * hr + (br - ai * hi), ar * hi + (bi + ai * hr)
            for c in range(nl):
                bu_ref[c, pl.ds(t, nseq, stride=dec_seq), :] = hr[:, c * LANES:(c + 1) * LANES]
                bu_ref[nl + c, pl.ds(t, nseq, stride=dec_seq), :] = hi[:, c * LANES:(c + 1) * LANES]
        ho_ref[:, base:base + CH_STATES] = hr
        ho_ref[:, base + CH_STATES:base + 2 * CH_STATES] = hi
        hall = jnp.concatenate([bu_ref[c] for c in range(2 * nl)], axis=1)
        y_ref[:, k * CH_IN:(k + 1) * CH_IN] = (
            jnp.dot(hall.astype(BF16), c_ref[k], preferred_element_type=F32) + d_ref[k] * uk)


def _ssm_sample(u, h0, sp, layer, nseq, dec_seq):
    rows = u.shape[0]
    return pl.pallas_call(
        functools.partial(_ssm_sample_body, nseq=nseq, dec_seq=dec_seq),
        grid=(1,),
        in_specs=[_full(u.shape), _full(h0.shape)] + [_layer_spec(sp[n].shape[1:], layer)
                                                      for n in ("B", "C", "lr", "li", "d")],
        out_specs=[_full((rows, SSM_WIDTH)), _full(h0.shape)],
        out_shape=[jax.ShapeDtypeStruct((rows, SSM_WIDTH), F32), jax.ShapeDtypeStruct(h0.shape, F32)],
        scratch_shapes=[pltpu.VMEM((2 * CH_STATES // LANES, rows, LANES), F32)],
        compiler_params=_cparams(("arbitrary",)),
        name="ssm_sample",
    )(u, h0, sp["B"], sp["C"], sp["lr"], sp["li"], sp["d"])


def _ssm_params(lambda_re, lambda_im, log_dt, b_re, b_im, c_re, c_im, d):
    lam = lax.complex(jnp.minimum(lambda_re.astype(F32), -1e-4), lambda_im.astype(F32))
    dt = jnp.exp(log_dt.astype(F32))[:, None]
    lam_dt = lam * dt
    lam_bar = jnp.exp(lam_dt)
    b = lax.complex(b_re.astype(F32), b_im.astype(F32))
    b_bar = ((lam_bar - 1.0) / lam)[..., None] * b
    def block_diag(rows, row_group, col_group):
        n, w = rows.shape
        tiled = jnp.tile(rows, (1, CH_GROUPS))
        own = (lax.broadcasted_iota(jnp.int32, tiled.shape, 0) // row_group) % CH_GROUPS
        col = lax.broadcasted_iota(jnp.int32, tiled.shape, 1) // col_group
        return jnp.where(own == col, tiled, 0.0)

    def pack_b(part):
        rows = part.transpose(0, 2, 1).reshape(SSM_GROUPS * SSM_GROUP, SSM_STATE)
        return block_diag(rows, SSM_GROUP, SSM_STATE).reshape(SSM_CHUNKS, CH_IN, CH_STATES)

    def pack_c(part):
        rows = part.transpose(0, 2, 1).reshape(SSM_GROUPS * SSM_STATE, SSM_GROUP)
        return block_diag(rows, SSM_STATE, SSM_GROUP).reshape(SSM_CHUNKS, CH_STATES, CH_IN)

    bmat = jnp.concatenate([pack_b(jnp.real(b_bar)), pack_b(jnp.imag(b_bar))], axis=2).astype(BF16)
    cmat = jnp.concatenate([pack_c(c_re.astype(F32)), pack_c(-c_im.astype(F32))], axis=1).astype(BF16)
    pw = jnp.exp(lam_dt * float(SCAN_SEG)).reshape(SSM_CHUNKS, 1, CH_STATES)
    lb = lam_bar.reshape(SSM_CHUNKS, 1, CH_STATES)
    return dict(B=bmat, C=cmat, lr=jnp.real(lb), li=jnp.imag(lb), pr=jnp.real(pw), pi=jnp.imag(pw),
                d=d.astype(F32).reshape(SSM_CHUNKS, 1, CH_IN))


def _pack_state(re, im):
    n = re.shape[0]
    st = jnp.stack([re.reshape(n, SSM_CHUNKS, CH_STATES), im.reshape(n, SSM_CHUNKS, CH_STATES)], axis=2)
    return st.reshape(n, SSM_CHUNKS * 2 * CH_STATES).astype(F32)


def _unpack_state(h, n):
    st = h.reshape(n, SSM_CHUNKS, 2, CH_GROUPS, SSM_STATE)
    return (st[:, :, 0].reshape(n, SSM_GROUPS, SSM_STATE), st[:, :, 1].reshape(n, SSM_GROUPS, SSM_STATE))


def _glu_body(y_ref, w_ref, g_ref, o_ref):
    a = jax.nn.gelu(y_ref[...])
    z = a * jax.nn.sigmoid(jnp.dot(a.astype(BF16), w_ref[...], preferred_element_type=F32))
    o_ref[...] = _rms(z, g_ref[...]).astype(BF16)


def _glu(y, w_bf, layer, g, tm):
    rows = y.shape[0]
    spec = pl.BlockSpec((tm, SSM_WIDTH), lambda i: (i, 0))
    return pl.pallas_call(
        _glu_body,
        grid=(rows // tm,),
        in_specs=[spec, _layer_spec((SSM_WIDTH, SSM_WIDTH), layer), _full((1, SSM_WIDTH))],
        out_specs=spec,
        out_shape=jax.ShapeDtypeStruct((rows, SSM_WIDTH), BF16),
        compiler_params=_cparams(("arbitrary",)),
        name="glu_rms",
    )(y, w_bf, g)


def _layer_norm(y, g, b):
    mu = jnp.mean(y, axis=-1, keepdims=True)
    yc = y - mu
    var = jnp.mean(jnp.square(yc), axis=-1, keepdims=True)
    return yc * lax.rsqrt(var + LN_EPS) * g + b


OUTPROJ_SUB = 256


def _outproj_body(*refs, aliased, nblk):
    if aliased:
        refs = refs[1:]
    a_ref, s_ref, x_ref, w_ref, g_ref, b_ref, wrc_ref, br_ref, o_ref, lg_ref = refs
    i = pl.program_id(0)

    @pl.when(i < nblk)
    def _():
        tm = a_ref.shape[0]
        sub = min(tm, OUTPROJ_SUB)
        for r0 in range(0, tm, sub):
            rows = slice(r0, r0 + sub)
            acc = jnp.dot(a_ref[rows, :], w_ref[0:ATTN_WIDTH, :], preferred_element_type=F32)
            acc = acc + jnp.dot(s_ref[rows, :], w_ref[ATTN_WIDTH:, :], preferred_element_type=F32)
            x1 = _layer_norm(DEEPNORM_ALPHA * x_ref[rows, :] + acc, g_ref[...], b_ref[...])
            o_ref[rows, :] = x1
            hh = jnp.dot(x1.astype(BF16), wrc_ref[...], preferred_element_type=F32)
            lg_ref[rows, :] = hh[:, 0:LANES] + hh[:, LANES:] + br_ref[...]

    @pl.when(i >= nblk)
    def _():
        o_ref[...] = jnp.zeros(o_ref.shape, F32)


def _outproj(an, sn, x, w_bf, layer, g, b, wr, br, tm, total_rows, row_block0, x1_buf=None):
    rows = an.shape[0]
    aliased = x1_buf is not None
    nblk = rows // tm
    tail = 0 if aliased or rows == total_rows else 1
    blk = lambda i: jnp.minimum(i, nblk - 1)
    half = lambda: pl.BlockSpec((tm, ATTN_WIDTH), lambda i: (blk(i), 0))
    in_specs = [half(), half(), pl.BlockSpec((tm, D_MODEL), lambda i: (blk(i), 0)),
                _layer_spec((D_MODEL, D_MODEL), layer), _full((1, D_MODEL)), _full((1, D_MODEL)),
                _full((D_MODEL, 2 * LANES)), _full((1, LANES))]
    wr_hi = wr.astype(BF16)
    wr_lo = (wr - wr_hi.astype(F32)).astype(BF16)
    args = [an, sn, x, w_bf, g, b, jnp.concatenate([wr_hi, wr_lo], axis=1), br]
    if aliased:
        in_specs = [pl.BlockSpec(memory_space=pl.ANY)] + in_specs
        args = [x1_buf] + args
    return pl.pallas_call(
        functools.partial(_outproj_body, aliased=aliased, nblk=nblk),
        grid=(nblk + tail,),
        in_specs=in_specs,
        out_specs=[pl.BlockSpec((tm, D_MODEL), lambda i: (i + row_block0, 0)),
                   pl.BlockSpec((tm, LANES), lambda i: (blk(i), 0))],
        out_shape=[jax.ShapeDtypeStruct((total_rows, D_MODEL), F32), jax.ShapeDtypeStruct((rows, LANES), F32)],
        input_output_aliases={0: 0} if aliased else {},
        compiler_params=_cparams(("arbitrary",)),
        name="outproj_ln1",
    )(*args)


ROUTE_T = 640
R_E0, R_E1, R_W0, R_W1, R_P0, R_P1 = range(6)


def _route_select(x):
    lane = lax.broadcasted_iota(jnp.int32, x.shape, 1)
    big = jnp.int32(1 << 20)
    neg = -jnp.inf
    gmask = lane < N_EXPERT_GROUPS
    gm = jnp.max(jnp.where(gmask, x, neg), axis=-1, keepdims=True)
    grp = jnp.min(jnp.where(gmask & (x == gm), lane, big), axis=-1, keepdims=True)
    p_grp = 1.0 / jnp.sum(jnp.where(gmask, jnp.exp(x - gm), 0.0), axis=-1, keepdims=True)
    lo = N_EXPERT_GROUPS + grp * EXPERTS_PER_GROUP
    emask = (lane >= lo) & (lane < lo + EXPERTS_PER_GROUP)
    v1 = jnp.max(jnp.where(emask, x, neg), axis=-1, keepdims=True)
    i1 = jnp.min(jnp.where(emask & (x == v1), lane, big), axis=-1, keepdims=True)
    emask2 = emask & (lane != i1)
    v2 = jnp.max(jnp.where(emask2, x, neg), axis=-1, keepdims=True)
    i2 = jnp.min(jnp.where(emask2 & (x == v2), lane, big), axis=-1, keepdims=True)
    e2 = jnp.exp(v2 - v1)
    w1 = p_grp / (1.0 + e2)
    w2 = p_grp * e2 / (1.0 + e2)
    return (i1 - N_EXPERT_GROUPS).astype(F32), (i2 - N_EXPERT_GROUPS).astype(F32), w1, w2


def _route_plan_body(lg_ref, ltri_ref, utri_ref, o_ref, meta_ref, rec_ref, run_ref):
    ph = pl.program_id(0)
    j = pl.program_id(1)
    tm = lg_ref.shape[0]
    lane = lax.broadcasted_iota(jnp.int32, (tm, LANES), 1)
    lanef = lane.astype(F32)
    r0 = pl.multiple_of(j * tm, tm)

    @pl.when((ph == 0) & (j == 0))
    def _():
        run_ref[...] = jnp.zeros(run_ref.shape, F32)

    @pl.when(ph == 0)
    def _():
        e0, e1, w0, w1 = _route_select(lg_ref[...])
        oh0 = (lanef == e0).astype(F32)
        oh1 = (lanef == e1).astype(F32)
        oh2 = oh0 + oh1
        before = jnp.dot(ltri_ref[...], oh2.astype(BF16), preferred_element_type=F32) + run_ref[0:1, :]
        rk0 = jnp.sum(before * oh0, axis=-1, keepdims=True)
        rk1 = jnp.sum(before * oh1, axis=-1, keepdims=True)
        run_ref[...] = run_ref[...] + jnp.sum(oh2, axis=0, keepdims=True)
        rec = jnp.where(lane == R_E0, e0, 0.0)
        rec = jnp.where(lane == R_E1, e1, rec)
        rec = jnp.where(lane == R_W0, w0, rec)
        rec = jnp.where(lane == R_W1, w1, rec)
        rec = jnp.where(lane == R_P0, rk0, rec)
        rec = jnp.where(lane == R_P1, rk1, rec)
        rec_ref[pl.ds(r0, tm), :] = rec

    @pl.when(ph == 1)
    def _():
        counts = run_ref[...]
        nblk = jnp.floor((counts + (MOE_BLOCK - 1)) * (1.0 / MOE_BLOCK))
        end_blk = jnp.dot(nblk.astype(BF16), utri_ref[...], preferred_element_type=F32)
        start_row = (end_blk - nblk) * MOE_BLOCK
        rec = rec_ref[pl.ds(r0, tm), :]
        oh0 = (lanef == rec[:, R_E0:R_E0 + 1]).astype(F32)
        oh1 = (lanef == rec[:, R_E1:R_E1 + 1]).astype(F32)
        p0 = jnp.sum(oh0 * start_row[0:1, :], axis=-1, keepdims=True) + rec[:, R_P0:R_P0 + 1]
        p1 = jnp.sum(oh1 * start_row[0:1, :], axis=-1, keepdims=True) + rec[:, R_P1:R_P1 + 1]
        out = jnp.where(lane == R_P0, p0, rec)
        out = jnp.where(lane == R_P1, p1, out)
        o_ref[...] = out

        @pl.when(j == 0)
        def _():
            mrow = lax.broadcasted_iota(jnp.int32, meta_ref.shape, 0)
            meta_ref[...] = jnp.where(mrow == 1, counts, end_blk)


def _route_plan(logits):
    rows = logits.shape[0]
    tm = ROUTE_T
    ltri = jnp.asarray(np.tril(np.ones((tm, tm), np.float32), -1), BF16)
    utri = jnp.asarray(np.triu(np.ones((LANES, LANES), np.float32)), BF16)
    mrows = SUBLANES
    return pl.pallas_call(
        _route_plan_body,
        grid=(2, rows // tm),
        in_specs=[pl.BlockSpec((tm, LANES), lambda p, j: (j * (1 - p), 0)),
                  pl.BlockSpec((tm, tm), lambda p, j: (0, 0)),
                  pl.BlockSpec((LANES, LANES), lambda p, j: (0, 0))],
        out_specs=[pl.BlockSpec((tm, LANES), lambda p, j: (j * p, 0)),
                   pl.BlockSpec((mrows, LANES), lambda p, j: (0, 0))],
        out_shape=[jax.ShapeDtypeStruct((rows, LANES), F32), jax.ShapeDtypeStruct((mrows, LANES), F32)],
        scratch_shapes=[pltpu.VMEM((rows, LANES), F32), pltpu.VMEM((SUBLANES, LANES), F32)],
        compiler_params=_cparams(("arbitrary", "arbitrary")),
        name="route_plan",
    )(logits, ltri, utri)


U32 = jnp.uint32
PACKED_W = D_MODEL // 2


def _pack_bf16_pairs(y):
    bits = lax.bitcast_convert_type(y.astype(BF16).astype(F32), U32)
    return bits[:, 0:PACKED_W] | (bits[:, PACKED_W:] >> 16)


def _unpack_bf16_pairs(p):
    hi = lax.bitcast_convert_type(p & jnp.uint32(0xFFFF0000), F32)
    lo = lax.bitcast_convert_type(p << 16, F32)
    return jnp.concatenate([hi, lo], axis=1)


def _build_rowmap(pos_ref, bend_ref, cnt_ref, o_ref):
    unroll = 8
    ntok = pos_ref.shape[0] // 2

    def zero_padding(e, _):
        first_blk = jnp.where(e == 0, 0, bend_ref[jnp.maximum(e - 1, 0)])

        def zero(r, _):
            o_ref[r] = 0
            return 0
        lax.fori_loop(first_blk * MOE_BLOCK + cnt_ref[e], bend_ref[e] * MOE_BLOCK, zero, 0)
        return 0
    lax.fori_loop(0, N_EXPERTS, zero_padding, 0)

    def fill(c, _):
        for u in range(unroll):
            t = c * unroll + u
            o_ref[pos_ref[2 * t]] = t
            o_ref[pos_ref[2 * t + 1]] = t
        return 0
    lax.fori_loop(0, ntok // unroll, fill, 0)


GATHER_SLOTS = 4


def _moe_body(bend_ref, cnt_ref, pos_ref, x_hbm, wg_ref, wu_ref, wd_ref, ys_hbm, *scratch, nblocks):
    xbufs = scratch[:GATHER_SLOTS]
    ybufs = scratch[GATHER_SLOTS:GATHER_SLOTS + 2]
    gsem, osem, wg_bf, wu_bf, wd_bf, row_tok_ref = scratch[GATHER_SLOTS + 2:]
    ahead = GATHER_SLOTS - 1
    e = pl.program_id(0)
    nused = bend_ref[N_EXPERTS - 1]
    b1 = bend_ref[e]
    b0 = jnp.where(e == 0, 0, bend_ref[jnp.maximum(e - 1, 0)])

    def gather_start(blk, s):
        base = blk * MOE_BLOCK
        for r in range(MOE_BLOCK):
            tok = row_tok_ref[base + r]
            pltpu.make_async_copy(x_hbm.at[pl.ds(tok, 1), :], xbufs[s].at[pl.ds(r, 1), :],
                                  gsem.at[s]).start(priority=r % 2)

    def gather_wait(s):
        pltpu.make_async_copy(x_hbm.at[pl.ds(0, MOE_BLOCK), :], xbufs[s], gsem.at[s]).wait()

    def out_copy(blk, s):
        row0 = pl.multiple_of(blk * MOE_BLOCK, MOE_BLOCK)
        return pltpu.make_async_copy(ybufs[s], ys_hbm.at[pl.ds(row0, MOE_BLOCK), :], osem.at[s])

    @pl.when(e == 0)
    def _():
        _build_rowmap(pos_ref, bend_ref, cnt_ref, row_tok_ref)
        for k in range(ahead):
            gather_start(k, k)

    @pl.when(b1 > b0)
    def _():
        wg_bf[...] = wg_ref[...].astype(BF16)
        wu_bf[...] = wu_ref[...].astype(BF16)
        wd_bf[...] = wd_ref[...].astype(BF16)

        def do_block(b, s):
            so = s % 2
            gather_wait(s)

            @pl.when(b >= 2)
            def _():
                out_copy(b - 2, so).wait()

            gather_start(jnp.minimum(b + ahead, nused - 1), (s + ahead) % GATHER_SLOTS)
            x = xbufs[s][...].astype(BF16)
            hg = jnp.dot(x, wg_bf[...], preferred_element_type=F32)
            hu = jnp.dot(x, wu_bf[...], preferred_element_type=F32)
            h = (jax.nn.silu(hg) * hu).astype(BF16)
            ybufs[so][...] = _pack_bf16_pairs(jnp.dot(h, wd_bf[...], preferred_element_type=F32))
            out_copy(b, so).start()

        def body(b, _):
            for s in range(GATHER_SLOTS):
                @pl.when(b % GATHER_SLOTS == s)
                def _():
                    do_block(b, s)
            return 0

        lax.fori_loop(b0, b1, body, 0)

    @pl.when(e == N_EXPERTS - 1)
    def _():
        last = nused - 1
        for s in range(GATHER_SLOTS):
            @pl.when(last % GATHER_SLOTS == s)
            def _():
                for k in range(1, GATHER_SLOTS):
                    gather_wait((s + k) % GATHER_SLOTS)
        out_copy(0, 0).wait()
        out_copy(0, 1).wait()
        yb0 = ybufs[0]
        yb0[...] = jnp.zeros(yb0.shape, U32)

        def zstart(b, _):
            out_copy(b, 0).start()
            return 0

        def zwait(b, _):
            out_copy(b, 0).wait()
            return 0

        lax.fori_loop(nused, nblocks, zstart, 0)
        lax.fori_loop(nused, nblocks, zwait, 0)


def _moe(bend, cnt, pos, x1, w_gate, w_up, w_down, layer, nblocks):
    assert 2 * x1.shape[0] > GATHER_SLOTS * MOE_BLOCK
    wspec = lambda a, b: pl.BlockSpec((None, None, a, b), lambda e, be, cn, ps: (layer, e, 0, 0))
    grid_spec = pltpu.PrefetchScalarGridSpec(
        num_scalar_prefetch=3,
        grid=(N_EXPERTS,),
        in_specs=[pl.BlockSpec(memory_space=pl.ANY), wspec(D_MODEL, EXPERT_FF), wspec(D_MODEL, EXPERT_FF),
                  wspec(EXPERT_FF, D_MODEL)],
        out_specs=pl.BlockSpec(memory_space=pl.ANY),
        scratch_shapes=([pltpu.VMEM((MOE_BLOCK, D_MODEL), F32) for _ in range(GATHER_SLOTS)]
                        + [pltpu.VMEM((MOE_BLOCK, PACKED_W), U32) for _ in range(2)]
                        + [pltpu.SemaphoreType.DMA((GATHER_SLOTS,)), pltpu.SemaphoreType.DMA((2,)),
                           pltpu.VMEM((D_MODEL, EXPERT_FF), BF16),
                           pltpu.VMEM((D_MODEL, EXPERT_FF), BF16),
                           pltpu.VMEM((EXPERT_FF, D_MODEL), BF16),
                           pltpu.SMEM((nblocks * MOE_BLOCK,), jnp.int32)]),
    )
    return pl.pallas_call(
        functools.partial(_moe_body, nblocks=nblocks),
        grid_spec=grid_spec,
        out_shape=jax.ShapeDtypeStruct((nblocks * MOE_BLOCK, PACKED_W), U32),
        compiler_params=_cparams(("arbitrary",), bounds_checks=False),
        name="moe_experts",
    )(bend, cnt, pos, x1, w_gate, w_up, w_down)


COMB_T = 128


def _combine_body(pos_ref, ys_hbm, x_ref, rt_ref, g_ref, b_ref, op_ref, os_ref, *scratch, n_prompt_blocks):
    i = pl.program_id(0)
    nsteps = pl.num_programs(0)
    bufs = tuple((scratch[2 * s], scratch[2 * s + 1]) for s in range(GATHER_SLOTS))
    sem = scratch[2 * GATHER_SLOTS]
    ahead = GATHER_SLOTS - 1

    def gather_start(step, s):
        base = step * (2 * COMB_T)
        for r in range(COMB_T):
            for k in range(2):
                p = pos_ref[base + 2 * r + k]
                pltpu.make_async_copy(ys_hbm.at[pl.ds(p, 1), :], bufs[s][k].at[pl.ds(r, 1), :],
                                      sem.at[s]).start(priority=k)

    def gather_wait(s):
        for k in range(2):
            pltpu.make_async_copy(ys_hbm.at[pl.ds(0, COMB_T), :], bufs[s][k], sem.at[s]).wait()

    @pl.when(i == 0)
    def _():
        for k in range(ahead):
            gather_start(k, k)

    def do_tile(s):
        gather_wait(s)
        gather_start(jnp.minimum(i + ahead, nsteps - 1), (s + ahead) % GATHER_SLOTS)
        rt = rt_ref[...]
        ff = (rt[:, R_W0:R_W0 + 1] * _unpack_bf16_pairs(bufs[s][0][...])
              + rt[:, R_W1:R_W1 + 1] * _unpack_bf16_pairs(bufs[s][1][...]))
        out = _layer_norm(DEEPNORM_ALPHA * x_ref[...] + ff, g_ref[...], b_ref[...])

        @pl.when(i < n_prompt_blocks)
        def _():
            op_ref[...] = out

        @pl.when(i >= n_prompt_blocks)
        def _():
            os_ref[...] = out

        @pl.when(i == nsteps - 1)
        def _():
            for k in range(1, GATHER_SLOTS):
                gather_wait((s + k) % GATHER_SLOTS)

    for s in range(GATHER_SLOTS):
        @pl.when(i % GATHER_SLOTS == s)
        def _():
            do_tile(s)


def _combine(pos, ys, x1, route, g, b, n_prompt_rows):
    rows = x1.shape[0]
    npb = n_prompt_rows // COMB_T
    assert rows // COMB_T >= GATHER_SLOTS
    grid_spec = pltpu.PrefetchScalarGridSpec(
        num_scalar_prefetch=1,
        grid=(rows // COMB_T,),
        in_specs=[pl.BlockSpec(memory_space=pl.ANY),
                  pl.BlockSpec((COMB_T, D_MODEL), lambda i, p: (i, 0)),
                  pl.BlockSpec((COMB_T, LANES), lambda i, p: (i, 0)),
                  pl.BlockSpec((1, D_MODEL), lambda i, p: (0, 0)),
                  pl.BlockSpec((1, D_MODEL), lambda i, p: (0, 0))],
        out_specs=[pl.BlockSpec((COMB_T, D_MODEL), lambda i, p: (jnp.minimum(i, npb - 1), 0)),
                   pl.BlockSpec((COMB_T, D_MODEL), lambda i, p: (jnp.maximum(i - npb, 0), 0))],
        scratch_shapes=([pltpu.VMEM((COMB_T, PACKED_W), U32) for _ in range(2 * GATHER_SLOTS)]
                        + [pltpu.SemaphoreType.DMA((GATHER_SLOTS,))]),
    )
    return pl.pallas_call(
        functools.partial(_combine_body, n_prompt_blocks=npb),
        grid_spec=grid_spec,
        out_shape=[jax.ShapeDtypeStruct((n_prompt_rows, D_MODEL), F32),
                   jax.ShapeDtypeStruct((rows - n_prompt_rows, D_MODEL), F32)],
        compiler_params=_cparams(("arbitrary",), bounds_checks=False),
        name="combine_ln2",
    )(pos, ys, x1, route, g, b)


def kernel(x_prompt, x_sample, cache_k, cache_v, state_ssm_re, state_ssm_im, w_in, ssm_lambda_re, ssm_lambda_im, ssm_log_dt, ssm_b_re, ssm_b_im, ssm_c_re, ssm_c_im, ssm_d, ssm_w_glu, attn_sinks, attn_norm_g, ssm_norm_g, w_out, ln1_g, ln1_b, router_group_w, router_group_b, router_expert_w, router_expert_b, expert_w_gate, expert_w_up, expert_w_down, ln2_g, ln2_b):
    nbatch, seq, _ = x_prompt.shape
    nseq, dec_seq, _ = x_sample.shape
    wbuf = cache_k.shape[2]
    rows_p = nbatch * seq
    rows_s = nseq * dec_seq
    rows_all = rows_p + rows_s
    tm_p = 512

    xp = x_prompt.reshape(rows_p, D_MODEL)
    xs = x_sample.reshape(rows_s, D_MODEL)
    tab_p = _rope_tables(np.arange(seq))
    tab_s = _rope_tables(PAST_LEN + (np.arange(rows_s) % dec_seq))
    ck_all = cache_k.astype(F32).reshape(DEPTH, nseq, wbuf, KV_WIDTH)
    cv_all = cache_v.astype(F32).reshape(DEPTH, nseq, wbuf, KV_WIDTH)

    sp = jax.vmap(_ssm_params)(ssm_lambda_re, ssm_lambda_im, ssm_log_dt, ssm_b_re, ssm_b_im,
                               ssm_c_re, ssm_c_im, ssm_d)
    w_in_bf = w_in.astype(BF16)
    w_glu_bf = ssm_w_glu.astype(BF16)
    w_out_bf = w_out.astype(BF16)

    outs = {k: [] for k in ("kp", "vp", "hrp", "hip", "ks", "vs", "hrs", "his")}
    for l in range(DEPTH):
        sinks = attn_sinks[l].astype(F32)
        attn_g = attn_norm_g[l].astype(F32).reshape(1, ATTN_WIDTH)
        ssm_g = ssm_norm_g[l].astype(F32).reshape(1, SSM_WIDTH)
        g1 = ln1_g[l].astype(F32).reshape(1, D_MODEL)
        b1 = ln1_b[l].astype(F32).reshape(1, D_MODEL)
        g2 = ln2_g[l].astype(F32).reshape(1, D_MODEL)
        b2 = ln2_b[l].astype(F32).reshape(1, D_MODEL)
        n_rt = N_EXPERT_GROUPS + N_EXPERTS
        wr = jnp.concatenate([router_group_w[l].astype(F32)]
                             + [router_expert_w[l, g].astype(F32) for g in range(N_EXPERT_GROUPS)]
                             + [jnp.zeros((D_MODEL, LANES - n_rt), F32)], axis=1)
        br = jnp.concatenate([router_group_b[l].astype(F32), router_expert_b[l].astype(F32).reshape(-1),
                              jnp.zeros((LANES - n_rt,), F32)]).reshape(1, LANES)

        q_p, k_p, v_p, u_p = _inproj(xp, w_in_bf, l, *tab_p, tm=tm_p)
        q_s, k_s, v_s, u_s = _inproj(xs, w_in_bf, l, *tab_s, tm=rows_s)

        an_p = _attn_prompt(sinks, q_p, k_p, v_p, attn_g, seq)
        an_s, nk_s, nv_s = _attn_sample(sinks, q_s, k_s, v_s, ck_all, cv_all, attn_g, dec_seq, l)

        y_p, hl_p = _ssm_prompt(u_p, sp, l, nbatch, seq)
        y_s, hl_s = _ssm_sample(u_s, _pack_state(state_ssm_re[l], state_ssm_im[l]), sp, l, nseq, dec_seq)
        sn_p = _glu(y_p, w_glu_bf, l, ssm_g, tm_p)
        sn_s = _glu(y_s, w_glu_bf, l, ssm_g, rows_s)

        x1, lg_p = _outproj(an_p, sn_p, xp, w_out_bf, l, g1, b1, wr, br, tm_p, rows_all, 0)
        x1, lg_s = _outproj(an_s, sn_s, xs, w_out_bf, l, g1, b1, wr, br, rows_s, rows_all, rows_p // rows_s,
                            x1_buf=x1)
        nblocks = 2 * rows_all // MOE_BLOCK + N_EXPERTS
        route, meta = _route_plan(jnp.concatenate([lg_p, lg_s], axis=0))
        pos = route[:, R_P0:R_P1 + 1].astype(jnp.int32).reshape(2 * rows_all)
        bend = meta[0, :N_EXPERTS].astype(jnp.int32)

        cnt = meta[1, :N_EXPERTS].astype(jnp.int32)
        ys = _moe(bend, cnt, pos, x1, expert_w_gate, expert_w_up, expert_w_down, l, nblocks)
        xp, xs = _combine(pos, ys, x1, route, g2, b2, rows_p)

        wp = min(WINDOW, seq)
        tail = lambda a: a.reshape(nbatch, seq, KV_WIDTH)[:, seq - wp:].reshape(nbatch, wp, N_KV_HEADS, HEAD_DIM)
        outs["kp"].append(tail(k_p))
        outs["vp"].append(tail(v_p))
        hp = hl_p.reshape(nbatch, SSM_CHUNKS * 2 * CH_STATES)
        hr, hi = _unpack_state(hp, nbatch)
        outs["hrp"].append(hr)
        outs["hip"].append(hi)
        outs["ks"].append(nk_s.reshape(nseq, wbuf, N_KV_HEADS, HEAD_DIM))
        outs["vs"].append(nv_s.reshape(nseq, wbuf, N_KV_HEADS, HEAD_DIM))
        hr, hi = _unpack_state(hl_s, nseq)
        outs["hrs"].append(hr)
        outs["his"].append(hi)

    st = {k: jnp.stack(v) for k, v in outs.items()}
    return (xp.reshape(nbatch, seq, D_MODEL), xs.reshape(nseq, dec_seq, D_MODEL),
            st["kp"], st["vp"], st["hrp"], st["hip"], st["ks"], st["vs"], st["hrs"], st["his"])
```

```python
import functools

import numpy as np
import jax
import jax.numpy as jnp
from jax import lax
from jax.experimental import pallas as pl
from jax.experimental.pallas import tpu as pltpu

F32 = jnp.float32
BF16 = jnp.bfloat16

D_MODEL = 2048
DEPTH = 2
PAST_LEN = 16384
ATTN_WIDTH = 1024
SSM_WIDTH = 1024
HEAD_DIM = 64
N_HEADS = 16
N_KV_HEADS = 4
Q_PER_KV = 4
KV_WIDTH = 256
WINDOW = 128
ROPE_THETA = 10000.0
SSM_GROUP = 16
SSM_GROUPS = 64
SSM_STATE = 64
N_EXPERT_GROUPS = 4
EXPERTS_PER_GROUP = 8
N_EXPERTS = 32
EXPERT_FF = 512
MOE_BLOCK = 128
DEEPNORM_ALPHA = (2.0 * DEPTH) ** 0.25
LN_EPS = 1e-5
RMS_EPS = 1e-6

LANES = 128
SUBLANES = 8
VMEM_LIMIT = 56 * 1024 * 1024

SSM_CHUNKS = 4
CH_GROUPS = SSM_GROUPS // SSM_CHUNKS
CH_STATES = CH_GROUPS * SSM_STATE
CH_IN = CH_GROUPS * SSM_GROUP
SCAN_T = 512
SCAN_SEG = SCAN_T // SUBLANES


def _cparams(sem=None, vmem=VMEM_LIMIT, bounds_checks=True):
    return pltpu.CompilerParams(dimension_semantics=sem, vmem_limit_bytes=vmem,
                                disable_bounds_checks=not bounds_checks)


def _full(shape):
    n = len(shape)
    return pl.BlockSpec(shape, lambda *a: (0,) * n)


def _dot_nt(a, b):
    return lax.dot_general(a, b, (((1,), (1,)), ((), ())), preferred_element_type=F32)


def _inproj_body(x_ref, w_ref, cos_ref, sa_ref, sb_ref, q_ref, k_ref, v_ref, u_ref):
    x = x_ref[...].astype(BF16)
    cos = cos_ref[...]
    sa = sa_ref[...]
    sb = sb_ref[...]

    half = HEAD_DIM // 2

    def rope(c):
        return c * cos + pltpu.roll(c, LANES - half, 1) * sa + pltpu.roll(c, half, 1) * sb

    pq = jnp.dot(x, w_ref[:, 0:ATTN_WIDTH], preferred_element_type=F32)
    for j in range(ATTN_WIDTH // LANES):
        q_ref[:, j * LANES:(j + 1) * LANES] = rope(pq[:, j * LANES:(j + 1) * LANES]).astype(BF16)
    pk = jnp.dot(x, w_ref[:, ATTN_WIDTH:ATTN_WIDTH + KV_WIDTH], preferred_element_type=F32)
    for j in range(KV_WIDTH // LANES):
        k_ref[:, j * LANES:(j + 1) * LANES] = rope(pk[:, j * LANES:(j + 1) * LANES])
    v_ref[...] = jnp.dot(x, w_ref[:, ATTN_WIDTH + KV_WIDTH:ATTN_WIDTH + 2 * KV_WIDTH],
                         preferred_element_type=F32)
    u_ref[...] = jnp.dot(x, w_ref[:, ATTN_WIDTH + 2 * KV_WIDTH:], preferred_element_type=F32)


def _layer_spec(shape, layer):
    return pl.BlockSpec((None,) + tuple(shape), lambda *a: (layer,) + (0,) * len(shape))


def _inproj(x, w_bf, layer, cos_t, sa_t, sb_t, tm):
    rows = x.shape[0]
    tab_blocks = cos_t.shape[0] // tm
    in_w = w_bf.shape[2]
    row_spec = lambda w: pl.BlockSpec((tm, w), lambda i: (i, 0))
    tab_spec = pl.BlockSpec((tm, LANES), lambda i: (i % tab_blocks, 0))
    return pl.pallas_call(
        _inproj_body,
        grid=(rows // tm,),
        in_specs=[row_spec(D_MODEL), _layer_spec((D_MODEL, in_w), layer), tab_spec, tab_spec, tab_spec],
        out_specs=[row_spec(ATTN_WIDTH), row_spec(KV_WIDTH), row_spec(KV_WIDTH), row_spec(SSM_WIDTH)],
        out_shape=[jax.ShapeDtypeStruct((rows, ATTN_WIDTH), BF16),
                   jax.ShapeDtypeStruct((rows, KV_WIDTH), F32),
                   jax.ShapeDtypeStruct((rows, KV_WIDTH), F32),
                   jax.ShapeDtypeStruct((rows, SSM_WIDTH), F32)],
        compiler_params=_cparams(("arbitrary",)),
        name="inproj_rope",
    )(x, w_bf, cos_t, sa_t, sb_t)


def _rope_tables(positions):
    half = HEAD_DIM // 2
    inv_freq = ROPE_THETA ** (-np.arange(half, dtype=np.float64) / half)
    ang = positions.astype(np.float64)[:, None] * inv_freq[None, :]
    lane = np.arange(LANES)
    cos = np.cos(ang)[:, lane % half]
    sin = np.sin(ang)[:, lane % half]
    first = (lane % HEAD_DIM) < half
    sa = np.where(first[None, :], -sin, 0.0)
    sb = np.where(first[None, :], 0.0, sin)
    return (jnp.asarray(cos, F32), jnp.asarray(sa, F32), jnp.asarray(sb, F32))


def _rms(x, g):
    return x * lax.rsqrt(jnp.mean(jnp.square(x), axis=-1, keepdims=True) + RMS_EPS) * g


def _attn_prompt_body(sink_ref, q_ref, kp_ref, kc_ref, vp_ref, vc_ref, g_ref, o_ref, acc_ref, *, blocks_per_seq):
    i = pl.program_id(0)
    has_prev = (i % blocks_per_seq) > 0
    kk = jnp.concatenate([kp_ref[...], kc_ref[...]], axis=0).astype(BF16)
    vt = jnp.concatenate([vp_ref[...], vc_ref[...]], axis=0).T.astype(BF16)
    ncol = Q_PER_KV * WINDOW
    key = lax.broadcasted_iota(jnp.int32, (2 * WINDOW, ncol), 0)
    col = lax.broadcasted_iota(jnp.int32, (2 * WINDOW, ncol), 1)
    qry = col % WINDOW
    visible = (key > qry) & (key <= qry + WINDOW) & ((key >= WINDOW) | has_prev)
    head = lax.broadcasted_iota(jnp.int32, (1, ncol), 1) // WINDOW
    q = q_ref[...] * (HEAD_DIM ** -0.5)
    scores = []
    for g in range(N_KV_HEADS):
        qg = jnp.concatenate([q[:, (g * Q_PER_KV + j) * HEAD_DIM:(g * Q_PER_KV + j + 1) * HEAD_DIM]
                              for j in range(Q_PER_KV)], axis=0)
        kh = kk[:, g * HEAD_DIM:(g + 1) * HEAD_DIM]
        scores.append(lax.dot_general(kh, qg, (((1,), (1,)), ((), ())), preferred_element_type=F32))
    for g in range(N_KV_HEADS):
        st = jnp.where(visible, scores[g], -jnp.inf)
        sink = jnp.zeros((1, ncol), F32)
        for j in range(Q_PER_KV):
            sink = jnp.where(head == j, sink_ref[g * Q_PER_KV + j], sink)
        m = jnp.maximum(jnp.max(st, axis=0, keepdims=True), sink)
        p = jnp.exp(st - m)
        denom = jnp.sum(p, axis=0, keepdims=True) + jnp.exp(sink - m)
        ot = jnp.dot(vt[g * HEAD_DIM:(g + 1) * HEAD_DIM, :], p.astype(BF16), preferred_element_type=F32)
        ot = ot / denom
        for j in range(Q_PER_KV):
            h = g * Q_PER_KV + j
            acc_ref[h * HEAD_DIM:(h + 1) * HEAD_DIM, :] = ot[:, j * WINDOW:(j + 1) * WINDOW]
    o_ref[...] = _rms(acc_ref[...].T, g_ref[...]).astype(BF16)


def _attn_prompt(sinks, q, k, v, g, seq):
    rows = q.shape[0]
    nblk = rows // WINDOW
    bps = seq // WINDOW
    cur = lambda w: pl.BlockSpec((WINDOW, w), lambda i: (i, 0))
    prev = lambda w: pl.BlockSpec((WINDOW, w), lambda i: (jnp.maximum(i - 1, 0), 0))
    return pl.pallas_call(
        functools.partial(_attn_prompt_body, blocks_per_seq=bps),
        grid=(nblk,),
        in_specs=[pl.BlockSpec(memory_space=pltpu.SMEM), cur(ATTN_WIDTH), prev(KV_WIDTH), cur(KV_WIDTH),
                  prev(KV_WIDTH), cur(KV_WIDTH), _full((1, ATTN_WIDTH))],
        out_specs=cur(ATTN_WIDTH),
        out_shape=jax.ShapeDtypeStruct((rows, ATTN_WIDTH), BF16),
        scratch_shapes=[pltpu.VMEM((ATTN_WIDTH, WINDOW), F32)],
        compiler_params=_cparams(("arbitrary",)),
        name="attn_prompt",
    )(sinks, q, k, k, v, v, g)


SEQ_PER_STEP = 8


def _attn_sample_body(sink_ref, q_ref, kn_ref, vn_ref, ck_ref, cv_ref, g_ref, o_ref, nk_ref, nv_ref, acc_ref,
                      *, dec_seq):
    nq = SEQ_PER_STEP * dec_seq
    wb = ck_ref.shape[1]
    nkc = SEQ_PER_STEP * wb
    ck = ck_ref[...].reshape(nkc, KV_WIDTH)
    cv = cv_ref[...].reshape(nkc, KV_WIDTH)
    kn = kn_ref[...]
    vn = vn_ref[...]
    ckb = ck.astype(BF16)
    cvb = cv.astype(BF16)
    knb = kn.astype(BF16)
    vnb = vn.astype(BF16)
    q = q_ref[...]

    rows = Q_PER_KV * nq
    r = lax.broadcasted_iota(jnp.int32, (rows, nkc), 0) % nq
    c = lax.broadcasted_iota(jnp.int32, (rows, nkc), 1)
    vis_c = ((r // dec_seq) == (c // wb)) & ((c % wb) >= (r % dec_seq) + 1 + (wb - WINDOW))
    rn = lax.broadcasted_iota(jnp.int32, (rows, nq), 0) % nq
    cn = lax.broadcasted_iota(jnp.int32, (rows, nq), 1)
    vis_n = ((rn // dec_seq) == (cn // dec_seq)) & ((cn % dec_seq) <= (rn % dec_seq))
    hrow = lax.broadcasted_iota(jnp.int32, (rows, 1), 0) // nq

    dn = (((1,), (1,)), ((), ()))
    for g in range(N_KV_HEADS):
        qs = jnp.concatenate([q[:, (g * Q_PER_KV + j) * HEAD_DIM:(g * Q_PER_KV + j + 1) * HEAD_DIM]
                              for j in range(Q_PER_KV)], axis=0)
        sl = slice(g * HEAD_DIM, (g + 1) * HEAD_DIM)
        sc = lax.dot_general(qs, ckb[:, sl], dn, preferred_element_type=F32) * (HEAD_DIM ** -0.5)
        sn = lax.dot_general(qs, knb[:, sl], dn, preferred_element_type=F32) * (HEAD_DIM ** -0.5)
        sc = jnp.where(vis_c, sc, -jnp.inf)
        sn = jnp.where(vis_n, sn, -jnp.inf)
        sink = jnp.zeros((rows, 1), F32)
        for j in range(Q_PER_KV):
            sink = jnp.where(hrow == j, sink_ref[g * Q_PER_KV + j], sink)
        m = jnp.maximum(jnp.maximum(jnp.max(sc, axis=-1, keepdims=True), jnp.max(sn, axis=-1, keepdims=True)), sink)
        pc = jnp.exp(sc - m)
        pn = jnp.exp(sn - m)
        denom = jnp.sum(pc, axis=-1, keepdims=True) + jnp.sum(pn, axis=-1, keepdims=True) + jnp.exp(sink - m)
        o = (jnp.dot(pc.astype(BF16), cvb[:, sl], preferred_element_type=F32)
             + jnp.dot(pn.astype(BF16), vnb[:, sl], preferred_element_type=F32)) / denom
        for j in range(Q_PER_KV):
            h = g * Q_PER_KV + j
            acc_ref[:, h * HEAD_DIM:(h + 1) * HEAD_DIM] = o[j * nq:(j + 1) * nq, :]
    o_ref[...] = _rms(acc_ref[...], g_ref[...]).astype(BF16)

    keep = wb - dec_seq
    for s in range(SEQ_PER_STEP):
        nk_ref[s, 0:keep, :] = ck_ref[s, dec_seq:wb, :]
        nv_ref[s, 0:keep, :] = cv_ref[s, dec_seq:wb, :]
        nk_ref[s, keep:wb, :] = kn[s * dec_seq:(s + 1) * dec_seq, :]
        nv_ref[s, keep:wb, :] = vn[s * dec_seq:(s + 1) * dec_seq, :]


def _attn_sample(sinks, q, kn, vn, ck, cv, g, dec_seq, layer):
    rows = q.shape[0]
    _, nseq, wb, _ = ck.shape
    nq = SEQ_PER_STEP * dec_seq
    rowb = lambda w: pl.BlockSpec((nq, w), lambda i: (i, 0))
    cin = pl.BlockSpec((None, SEQ_PER_STEP, wb, KV_WIDTH), lambda i: (layer, i, 0, 0))
    cb = pl.BlockSpec((SEQ_PER_STEP, wb, KV_WIDTH), lambda i: (i, 0, 0))
    return pl.pallas_call(
        functools.partial(_attn_sample_body, dec_seq=dec_seq),
        grid=(nseq // SEQ_PER_STEP,),
        in_specs=[pl.BlockSpec(memory_space=pltpu.SMEM), rowb(ATTN_WIDTH), rowb(KV_WIDTH), rowb(KV_WIDTH),
                  cin, cin, _full((1, ATTN_WIDTH))],
        out_specs=[rowb(ATTN_WIDTH), cb, cb],
        out_shape=[jax.ShapeDtypeStruct((rows, ATTN_WIDTH), BF16),
                   jax.ShapeDtypeStruct(ck.shape[1:], F32), jax.ShapeDtypeStruct(cv.shape[1:], F32)],
        scratch_shapes=[pltpu.VMEM((nq, ATTN_WIDTH), F32)],
        compiler_params=_cparams(("arbitrary",)),
        name="attn_sample",
    )(sinks, q, kn, vn, ck, cv, g)


def _ssm_prompt_body(ua_ref, uc_ref, b_ref, c_ref, lr_ref, li_ref, pr_ref, pi_ref, d_ref, y_ref, hl_ref,
                     uperm_ref, bu0, bu1, hb0, hb1, yp_ref, carry_ref, *, nchunks, ntc):
    k = pl.program_id(0)
    bus = (bu0, bu1)
    hbs = (hb0, hb1)
    lanes_per_chunk = CH_IN // LANES
    seg = lambda s: slice(s * SCAN_SEG, (s + 1) * SCAN_SEG)
    lane = lambda c: slice(c * LANES, (c + 1) * LANES)

    def stage_a(item, slot):
        cc = item % SSM_CHUNKS
        for c in range(lanes_per_chunk):
            for s in range(SUBLANES):
                uperm_ref[c, pl.ds(s, SCAN_SEG, stride=SUBLANES), :] = ua_ref[seg(s), lane(c)]
        up = jnp.concatenate([uperm_ref[c] for c in range(lanes_per_chunk)], axis=1)
        bus[slot][...] = _dot_nt(up.astype(BF16), b_ref[cc])

    def stage_b(item, slot):
        cc = item % SSM_CHUNKS
        tc = (item // SSM_CHUNKS) % ntc
        bu_ref = bus[slot]
        hb_ref = hbs[slot]
        ar = jnp.broadcast_to(lr_ref[cc], (SUBLANES, CH_STATES))
        ai = jnp.broadcast_to(li_ref[cc], (SUBLANES, CH_STATES))

        def advance(h, j):
            hr, hi = h
            br = bu_ref[j * SUBLANES:(j + 1) * SUBLANES, 0:CH_STATES]
            bi = bu_ref[j * SUBLANES:(j + 1) * SUBLANES, CH_STATES:2 * CH_STATES]
            return ar * hr + (br - ai * hi), ar * hi + (bi + ai * hr)

        zero = jnp.zeros((SUBLANES, CH_STATES), F32)
        h = (zero, zero)
        for j in range(SCAN_SEG):
            h = advance(h, j)
        fr, fi = h

        cst = jnp.where(tc == 0, 0.0, carry_ref[cc])
        c_r = cst[0:1, :]
        c_i = cst[1:2, :]
        lsr = pr_ref[cc]
        lsi = pi_ref[cc]
        rows_r, rows_i = [], []
        for s in range(SUBLANES):
            rows_r.append(c_r)
            rows_i.append(c_i)
            n_r = fr[s:s + 1, :] + (lsr * c_r - lsi * c_i)
            n_i = fi[s:s + 1, :] + (lsr * c_i + lsi * c_r)
            c_r, c_i = n_r, n_i
        carry_ref[cc] = jnp.concatenate([c_r, c_i], axis=0)
        hl_ref[0, cc] = jnp.concatenate([c_r, c_i], axis=0)

        h = (jnp.concatenate(rows_r, axis=0), jnp.concatenate(rows_i, axis=0))
        for jj in range(SCAN_SEG // 2):
            h1 = advance(h, 2 * jj)
            h = advance(h1, 2 * jj + 1)
            rows = slice(2 * jj * SUBLANES, (2 * jj + 2) * SUBLANES)
            hb_ref[rows, 0:CH_STATES] = jnp.concatenate([h1[0], h[0]], axis=0).astype(BF16)
            hb_ref[rows, CH_STATES:2 * CH_STATES] = jnp.concatenate([h1[1], h[1]], axis=0).astype(BF16)

    def stage_c(item, slot):
        cc = item % SSM_CHUNKS
        yp = _dot_nt(hbs[slot][...], c_ref[cc])
        d = d_ref[cc]
        for c in range(lanes_per_chunk):
            yp_ref[c] = yp[:, lane(c)]
        for s in range(SUBLANES):
            for c in range(lanes_per_chunk):
                y_ref[seg(s), lane(c)] = (yp_ref[c, pl.ds(s, SCAN_SEG, stride=SUBLANES), :]
                                          + d[:, lane(c)] * uc_ref[seg(s), lane(c)])

    def run(do_a, do_b, do_c, parity):
        if do_a:
            stage_a(k, parity)
        if do_b:
            stage_b(k - 1, 1 - parity)
        if do_c:
            stage_c(k - 2, parity)

    @pl.when(k == 0)
    def _():
        carry_ref[...] = jnp.zeros(carry_ref.shape, F32)
        run(True, False, False, 0)

    @pl.when(k == 1)
    def _():
        run(True, True, False, 1)

    for parity in range(2):
        @pl.when((k >= 2) & (k < nchunks) & (k % 2 == parity))
        def _():
            run(True, True, True, parity)

    @pl.when(k == nchunks)
    def _():
        run(False, True, True, nchunks % 2)

    @pl.when(k == nchunks + 1)
    def _():
        run(False, False, True, (nchunks + 1) % 2)


def _ssm_prompt(u, sp, layer, nbatch, seq):
    rows = u.shape[0]
    ntc = seq // SCAN_T
    nchunks = nbatch * ntc * SSM_CHUNKS
    assert nchunks >= 2
    item_a = lambda k: jnp.minimum(k, nchunks - 1)
    item_b = lambda k: jnp.clip(k - 1, 0, nchunks - 1)
    item_c = lambda k: jnp.clip(k - 2, 0, nchunks - 1)
    chunk_block = lambda item: (item // SSM_CHUNKS, item % SSM_CHUNKS)
    y, hl = pl.pallas_call(
        functools.partial(_ssm_prompt_body, nchunks=nchunks, ntc=ntc),
        grid=(nchunks + 2,),
        in_specs=[pl.BlockSpec((SCAN_T, CH_IN), lambda k: chunk_block(item_a(k))),
                  pl.BlockSpec((SCAN_T, CH_IN), lambda k: chunk_block(item_c(k))),
                  ] + [_layer_spec(sp[n].shape[1:], layer) for n in ("B", "C", "lr", "li", "pr", "pi", "d")],
        out_specs=[pl.BlockSpec((SCAN_T, CH_IN), lambda k: chunk_block(item_c(k))),
                   pl.BlockSpec((1, SSM_CHUNKS, 2, CH_STATES),
                                lambda k: (item_b(k) // (ntc * SSM_CHUNKS), 0, 0, 0))],
        out_shape=[jax.ShapeDtypeStruct((rows, SSM_WIDTH), F32),
                   jax.ShapeDtypeStruct((nbatch, SSM_CHUNKS, 2, CH_STATES), F32)],
        scratch_shapes=[pltpu.VMEM((CH_IN // LANES, SCAN_T, LANES), F32),
                        pltpu.VMEM((SCAN_T, 2 * CH_STATES), F32), pltpu.VMEM((SCAN_T, 2 * CH_STATES), F32),
                        pltpu.VMEM((SCAN_T, 2 * CH_STATES), BF16), pltpu.VMEM((SCAN_T, 2 * CH_STATES), BF16),
                        pltpu.VMEM((CH_IN // LANES, SCAN_T, LANES), F32),
                        pltpu.VMEM((SSM_CHUNKS, 2, CH_STATES), F32)],
        compiler_params=_cparams(("arbitrary",)),
        name="ssm_prompt",
    )(u, u, sp["B"], sp["C"], sp["lr"], sp["li"], sp["pr"], sp["pi"], sp["d"])
    return y, hl


def _ssm_sample_body(u_ref, h0_ref, b_ref, c_ref, lr_ref, li_ref, d_ref, y_ref, ho_ref, bu_ref, *, nseq, dec_seq):
    nl = CH_STATES // LANES
    for k in range(SSM_CHUNKS):
        uk = u_ref[:, k * CH_IN:(k + 1) * CH_IN]
        bu = _dot_nt(uk.astype(BF16), b_ref[k])
        for c in range(2 * nl):
            bu_ref[c] = bu[:, c * LANES:(c + 1) * LANES]
        base = k * 2 * CH_STATES
        hr = h0_ref[:, base:base + CH_STATES]
        hi = h0_ref[:, base + CH_STATES:base + 2 * CH_STATES]
        ar = lr_ref[k]
        ai = li_ref[k]
        for t in range(dec_seq):
            br = jnp.concatenate([bu_ref[c, pl.ds(t, nseq, stride=dec_seq), :] for c in range(nl)], axis=1)
            bi = jnp.concatenate([bu_ref[nl + c, pl.ds(t, nseq, stride=dec_seq), :] for c in range(nl)], axis=1)
            hr, hi = ar * hr + (br - ai * hi), ar * hi + (bi + ai * hr)
            for c in range(nl):
                bu_ref[c, pl.ds(t, nseq, stride=dec_seq), :] = hr[:, c * LANES:(c + 1) * LANES]
                bu_ref[nl + c, pl.ds(t, nseq, stride=dec_seq), :] = hi[:, c * LANES:(c + 1) * LANES]
        ho_ref[:, base:base + CH_STATES] = hr
        ho_ref[:, base + CH_STATES:base + 2 * CH_STATES] = hi
        hall = jnp.concatenate([bu_ref[c] for c in range(2 * nl)], axis=1)
        y_ref[:, k * CH_IN:(k + 1) * CH_IN] = (
            _dot_nt(hall.astype(BF16), c_ref[k]) + d_ref[k] * uk)


def _ssm_sample(u, h0, sp, layer, nseq, dec_seq):
    rows = u.shape[0]
    return pl.pallas_call(
        functools.partial(_ssm_sample_body, nseq=nseq, dec_seq=dec_seq),
        grid=(1,),
        in_specs=[_full(u.shape), _full(h0.shape)] + [_layer_spec(sp[n].shape[1:], layer)
                                                      for n in ("B", "C", "lr", "li", "d")],
        out_specs=[_full((rows, SSM_WIDTH)), _full(h0.shape)],
        out_shape=[jax.ShapeDtypeStruct((rows, SSM_WIDTH), F32), jax.ShapeDtypeStruct(h0.shape, F32)],
        scratch_shapes=[pltpu.VMEM((2 * CH_STATES // LANES, rows, LANES), F32)],
        compiler_params=_cparams(("arbitrary",)),
        name="ssm_sample",
    )(u, h0, sp["B"], sp["C"], sp["lr"], sp["li"], sp["d"])


def _ssm_params(lambda_re, lambda_im, log_dt, b_re, b_im, c_re, c_im, d):
    lam = lax.complex(jnp.minimum(lambda_re.astype(F32), -1e-4), lambda_im.astype(F32))
    dt = jnp.exp(log_dt.astype(F32))[:, None]
    lam_dt = lam * dt
    lam_bar = jnp.exp(lam_dt)
    b = lax.complex(b_re.astype(F32), b_im.astype(F32))
    b_bar = ((lam_bar - 1.0) / lam)[..., None] * b
    def block_diag(rows, row_group, col_group):
        n, w = rows.shape
        tiled = jnp.tile(rows, (1, CH_GROUPS))
        own = (lax.broadcasted_iota(jnp.int32, tiled.shape, 0) // row_group) % CH_GROUPS
        col = lax.broadcasted_iota(jnp.int32, tiled.shape, 1) // col_group
        return jnp.where(own == col, tiled, 0.0)

    def pack_b(part):
        rows = part.reshape(SSM_GROUPS * SSM_STATE, SSM_GROUP)
        return block_diag(rows, SSM_STATE, SSM_GROUP).reshape(SSM_CHUNKS, CH_STATES, CH_IN)

    def pack_c(part):
        rows = part.reshape(SSM_GROUPS * SSM_GROUP, SSM_STATE)
        return block_diag(rows, SSM_GROUP, SSM_STATE).reshape(SSM_CHUNKS, CH_IN, CH_STATES)

    bmat = jnp.concatenate([pack_b(jnp.real(b_bar)), pack_b(jnp.imag(b_bar))], axis=1).astype(BF16)
    cmat = jnp.concatenate([pack_c(c_re.astype(F32)), pack_c(-c_im.astype(F32))], axis=2).astype(BF16)
    pw = jnp.exp(lam_dt * float(SCAN_SEG)).reshape(SSM_CHUNKS, 1, CH_STATES)
    lb = lam_bar.reshape(SSM_CHUNKS, 1, CH_STATES)
    return dict(B=bmat, C=cmat, lr=jnp.real(lb), li=jnp.imag(lb), pr=jnp.real(pw), pi=jnp.imag(pw),
                d=d.astype(F32).reshape(SSM_CHUNKS, 1, CH_IN))


def _pack_state(re, im):
    n = re.shape[0]
    st = jnp.stack([re.reshape(n, SSM_CHUNKS, CH_STATES), im.reshape(n, SSM_CHUNKS, CH_STATES)], axis=2)
    return st.reshape(n, SSM_CHUNKS * 2 * CH_STATES).astype(F32)


def _unpack_state(h, n):
    st = h.reshape(n, SSM_CHUNKS, 2, CH_GROUPS, SSM_STATE)
    return (st[:, :, 0].reshape(n, SSM_GROUPS, SSM_STATE), st[:, :, 1].reshape(n, SSM_GROUPS, SSM_STATE))


def _glu_body(y_ref, w_ref, g_ref, o_ref):
    a = jax.nn.gelu(y_ref[...])
    z = a * jax.nn.sigmoid(jnp.dot(a.astype(BF16), w_ref[...], preferred_element_type=F32))
    o_ref[...] = _rms(z, g_ref[...]).astype(BF16)


def _glu(y, w_bf, layer, g, tm):
    rows = y.shape[0]
    spec = pl.BlockSpec((tm, SSM_WIDTH), lambda i: (i, 0))
    return pl.pallas_call(
        _glu_body,
        grid=(rows // tm,),
        in_specs=[spec, _layer_spec((SSM_WIDTH, SSM_WIDTH), layer), _full((1, SSM_WIDTH))],
        out_specs=spec,
        out_shape=jax.ShapeDtypeStruct((rows, SSM_WIDTH), BF16),
        compiler_params=_cparams(("arbitrary",)),
        name="glu_rms",
    )(y, w_bf, g)


def _layer_norm(y, g, b):
    mu = jnp.mean(y, axis=-1, keepdims=True)
    yc = y - mu
    var = jnp.mean(jnp.square(yc), axis=-1, keepdims=True)
    return yc * lax.rsqrt(var + LN_EPS) * g + b


OUTPROJ_SUB = 256


def _outproj_body(*refs, aliased, nblk):
    if aliased:
        refs = refs[1:]
    a_ref, s_ref, x_ref, w_ref, g_ref, b_ref, wrc_ref, br_ref, o_ref, lg_ref = refs
    i = pl.program_id(0)

    @pl.when(i < nblk)
    def _():
        tm = a_ref.shape[0]
        sub = min(tm, OUTPROJ_SUB)
        for r0 in range(0, tm, sub):
            rows = slice(r0, r0 + sub)
            acc = jnp.dot(a_ref[rows, :], w_ref[0:ATTN_WIDTH, :], preferred_element_type=F32)
            acc = acc + jnp.dot(s_ref[rows, :], w_ref[ATTN_WIDTH:, :], preferred_element_type=F32)
            x1 = _layer_norm(DEEPNORM_ALPHA * x_ref[rows, :] + acc, g_ref[...], b_ref[...])
            o_ref[rows, :] = x1
            hh = jnp.dot(x1.astype(BF16), wrc_ref[...], preferred_element_type=F32)
            lg_ref[rows, :] = hh[:, 0:LANES] + hh[:, LANES:] + br_ref[...]

    @pl.when(i >= nblk)
    def _():
        o_ref[...] = jnp.zeros(o_ref.shape, F32)


def _outproj(an, sn, x, w_bf, layer, g, b, wr, br, tm, total_rows, row_block0, x1_buf=None):
    rows = an.shape[0]
    aliased = x1_buf is not None
    nblk = rows // tm
    tail = 0 if aliased or rows == total_rows else 1
    blk = lambda i: jnp.minimum(i, nblk - 1)
    half = lambda: pl.BlockSpec((tm, ATTN_WIDTH), lambda i: (blk(i), 0))
    in_specs = [half(), half(), pl.BlockSpec((tm, D_MODEL), lambda i: (blk(i), 0)),
                _layer_spec((D_MODEL, D_MODEL), layer), _full((1, D_MODEL)), _full((1, D_MODEL)),
                _full((D_MODEL, 2 * LANES)), _full((1, LANES))]
    wr_hi = wr.astype(BF16)
    wr_lo = (wr - wr_hi.astype(F32)).astype(BF16)
    args = [an, sn, x, w_bf, g, b, jnp.concatenate([wr_hi, wr_lo], axis=1), br]
    if aliased:
        in_specs = [pl.BlockSpec(memory_space=pl.ANY)] + in_specs
        args = [x1_buf] + args
    return pl.pallas_call(
        functools.partial(_outproj_body, aliased=aliased, nblk=nblk),
        grid=(nblk + tail,),
        in_specs=in_specs,
        out_specs=[pl.BlockSpec((tm, D_MODEL), lambda i: (i + row_block0, 0)),
                   pl.BlockSpec((tm, LANES), lambda i: (blk(i), 0))],
        out_shape=[jax.ShapeDtypeStruct((total_rows, D_MODEL), F32), jax.ShapeDtypeStruct((rows, LANES), F32)],
        input_output_aliases={0: 0} if aliased else {},
        compiler_params=_cparams(("arbitrary",)),
        name="outproj_ln1",
    )(*args)


ROUTE_T = 640
R_E0, R_E1, R_W0, R_W1, R_P0, R_P1 = range(6)


def _route_select(x):
    lane = lax.broadcasted_iota(jnp.int32, x.shape, 1)
    big = jnp.int32(1 << 20)
    neg = -jnp.inf
    gmask = lane < N_EXPERT_GROUPS
    gm = jnp.max(jnp.where(gmask, x, neg), axis=-1, keepdims=True)
    grp = jnp.min(jnp.where(gmask & (x == gm), lane, big), axis=-1, keepdims=True)
    p_grp = 1.0 / jnp.sum(jnp.where(gmask, jnp.exp(x - gm), 0.0), axis=-1, keepdims=True)
    lo = N_EXPERT_GROUPS + grp * EXPERTS_PER_GROUP
    emask = (lane >= lo) & (lane < lo + EXPERTS_PER_GROUP)
    v1 = jnp.max(jnp.where(emask, x, neg), axis=-1, keepdims=True)
    i1 = jnp.min(jnp.where(emask & (x == v1), lane, big), axis=-1, keepdims=True)
    emask2 = emask & (lane != i1)
    v2 = jnp.max(jnp.where(emask2, x, neg), axis=-1, keepdims=True)
    i2 = jnp.min(jnp.where(emask2 & (x == v2), lane, big), axis=-1, keepdims=True)
    e2 = jnp.exp(v2 - v1)
    w1 = p_grp / (1.0 + e2)
    w2 = p_grp * e2 / (1.0 + e2)
    return (i1 - N_EXPERT_GROUPS).astype(F32), (i2 - N_EXPERT_GROUPS).astype(F32), w1, w2


def _route_plan_body(lg_ref, ltri_ref, utri_ref, o_ref, meta_ref, rec_ref, run_ref):
    ph = pl.program_id(0)
    j = pl.program_id(1)
    tm = lg_ref.shape[0]
    lane = lax.broadcasted_iota(jnp.int32, (tm, LANES), 1)
    lanef = lane.astype(F32)
    r0 = pl.multiple_of(j * tm, tm)

    @pl.when((ph == 0) & (j == 0))
    def _():
        run_ref[...] = jnp.zeros(run_ref.shape, F32)

    @pl.when(ph == 0)
    def _():
        e0, e1, w0, w1 = _route_select(lg_ref[...])
        oh0 = (lanef == e0).astype(F32)
        oh1 = (lanef == e1).astype(F32)
        oh2 = oh0 + oh1
        before = jnp.dot(ltri_ref[...], oh2.astype(BF16), preferred_element_type=F32) + run_ref[0:1, :]
        rk0 = jnp.sum(before * oh0, axis=-1, keepdims=True)
        rk1 = jnp.sum(before * oh1, axis=-1, keepdims=True)
        run_ref[...] = run_ref[...] + jnp.sum(oh2, axis=0, keepdims=True)
        rec = jnp.where(lane == R_E0, e0, 0.0)
        rec = jnp.where(lane == R_E1, e1, rec)
        rec = jnp.where(lane == R_W0, w0, rec)
        rec = jnp.where(lane == R_W1, w1, rec)
        rec = jnp.where(lane == R_P0, rk0, rec)
        rec = jnp.where(lane == R_P1, rk1, rec)
        rec_ref[pl.ds(r0, tm), :] = rec

    @pl.when(ph == 1)
    def _():
        counts = run_ref[...]
        nblk = jnp.floor((counts + (MOE_BLOCK - 1)) * (1.0 / MOE_BLOCK))
        end_blk = jnp.dot(nblk.astype(BF16), utri_ref[...], preferred_element_type=F32)
        start_row = (end_blk - nblk) * MOE_BLOCK
        rec = rec_ref[pl.ds(r0, tm), :]
        oh0 = (lanef == rec[:, R_E0:R_E0 + 1]).astype(F32)
        oh1 = (lanef == rec[:, R_E1:R_E1 + 1]).astype(F32)
        p0 = jnp.sum(oh0 * start_row[0:1, :], axis=-1, keepdims=True) + rec[:, R_P0:R_P0 + 1]
        p1 = jnp.sum(oh1 * start_row[0:1, :], axis=-1, keepdims=True) + rec[:, R_P1:R_P1 + 1]
        out = jnp.where(lane == R_P0, p0, rec)
        out = jnp.where(lane == R_P1, p1, out)
        o_ref[...] = out

        @pl.when(j == 0)
        def _():
            mrow = lax.broadcasted_iota(jnp.int32, meta_ref.shape, 0)
            meta_ref[...] = jnp.where(mrow == 1, counts, end_blk)


def _route_plan(logits):
    rows = logits.shape[0]
    tm = ROUTE_T
    ltri = jnp.asarray(np.tril(np.ones((tm, tm), np.float32), -1), BF16)
    utri = jnp.asarray(np.triu(np.ones((LANES, LANES), np.float32)), BF16)
    mrows = SUBLANES
    return pl.pallas_call(
        _route_plan_body,
        grid=(2, rows // tm),
        in_specs=[pl.BlockSpec((tm, LANES), lambda p, j: (j * (1 - p), 0)),
                  pl.BlockSpec((tm, tm), lambda p, j: (0, 0)),
                  pl.BlockSpec((LANES, LANES), lambda p, j: (0, 0))],
        out_specs=[pl.BlockSpec((tm, LANES), lambda p, j: (j * p, 0)),
                   pl.BlockSpec((mrows, LANES), lambda p, j: (0, 0))],
        out_shape=[jax.ShapeDtypeStruct((rows, LANES), F32), jax.ShapeDtypeStruct((mrows, LANES), F32)],
        scratch_shapes=[pltpu.VMEM((rows, LANES), F32), pltpu.VMEM((SUBLANES, LANES), F32)],
        compiler_params=_cparams(("arbitrary", "arbitrary")),
        name="route_plan",
    )(logits, ltri, utri)


U32 = jnp.uint32
PACKED_W = D_MODEL // 2


def _pack_bf16_pairs(y):
    bits = lax.bitcast_convert_type(y.astype(BF16).astype(F32), U32)
    return bits[:, 0:PACKED_W] | (bits[:, PACKED_W:] >> 16)


def _unpack_bf16_pairs(p):
    hi = lax.bitcast_convert_type(p & jnp.uint32(0xFFFF0000), F32)
    lo = lax.bitcast_convert_type(p << 16, F32)
    return jnp.concatenate([hi, lo], axis=1)


def _build_rowmap(pos_ref, bend_ref, cnt_ref, o_ref):
    unroll = 8
    ntok = pos_ref.shape[0] // 2

    def zero_padding(e, _):
        first_blk = jnp.where(e == 0, 0, bend_ref[jnp.maximum(e - 1, 0)])

        def zero(r, _):
            o_ref[r] = 0
            return 0
        lax.fori_loop(first_blk * MOE_BLOCK + cnt_ref[e], bend_ref[e] * MOE_BLOCK, zero, 0)
        return 0
    lax.fori_loop(0, N_EXPERTS, zero_padding, 0)

    def fill(c, _):
        for u in range(unroll):
            t = c * unroll + u
            o_ref[pos_ref[2 * t]] = t
            o_ref[pos_ref[2 * t + 1]] = t
        return 0
    lax.fori_loop(0, ntok // unroll, fill, 0)


GATHER_SLOTS = 4


def _moe_body(bend_ref, cnt_ref, pos_ref, x_hbm, wg_ref, wu_ref, wd_ref, ys_hbm, *scratch, nblocks):
    xbufs = scratch[:GATHER_SLOTS]
    ybufs = scratch[GATHER_SLOTS:GATHER_SLOTS + 2]
    gsem, osem, wg_bf, wu_bf, wd_bf, row_tok_ref = scratch[GATHER_SLOTS + 2:]
    ahead = GATHER_SLOTS - 1
    e = pl.program_id(0)
    nused = bend_ref[N_EXPERTS - 1]
    b1 = bend_ref[e]
    b0 = jnp.where(e == 0, 0, bend_ref[jnp.maximum(e - 1, 0)])

    def gather_start(blk, s):
        base = blk * MOE_BLOCK
        for r in range(MOE_BLOCK):
            tok = row_tok_ref[base + r]
            pltpu.make_async_copy(x_hbm.at[pl.ds(tok, 1), :], xbufs[s].at[pl.ds(r, 1), :],
                                  gsem.at[s]).start(priority=r % 2)

    def gather_wait(s):
        pltpu.make_async_copy(x_hbm.at[pl.ds(0, MOE_BLOCK), :], xbufs[s], gsem.at[s]).wait()

    def out_copy(blk, s):
        row0 = pl.multiple_of(blk * MOE_BLOCK, MOE_BLOCK)
        return pltpu.make_async_copy(ybufs[s], ys_hbm.at[pl.ds(row0, MOE_BLOCK), :], osem.at[s])

    @pl.when(e == 0)
    def _():
        _build_rowmap(pos_ref, bend_ref, cnt_ref, row_tok_ref)
        for k in range(ahead):
            gather_start(k, k)

    @pl.when(b1 > b0)
    def _():
        wg_bf[...] = wg_ref[...].astype(BF16)
        wu_bf[...] = wu_ref[...].astype(BF16)
        wd_bf[...] = wd_ref[...].astype(BF16)

        def do_block(b, s):
            so = s % 2
            gather_wait(s)

            @pl.when(b >= 2)
            def _():
                out_copy(b - 2, so).wait()

            gather_start(jnp.minimum(b + ahead, nused - 1), (s + ahead) % GATHER_SLOTS)
            x = xbufs[s][...].astype(BF16)
            hg = jnp.dot(x, wg_bf[...], preferred_element_type=F32)
            hu = jnp.dot(x, wu_bf[...], preferred_element_type=F32)
            h = (jax.nn.silu(hg) * hu).astype(BF16)
            ybufs[so][...] = _pack_bf16_pairs(jnp.dot(h, wd_bf[...], preferred_element_type=F32))
            out_copy(b, so).start()

        def body(b, _):
            for s in range(GATHER_SLOTS):
                @pl.when(b % GATHER_SLOTS == s)
                def _():
                    do_block(b, s)
            return 0

        lax.fori_loop(b0, b1, body, 0)

    @pl.when(e == N_EXPERTS - 1)
    def _():
        last = nused - 1
        for s in range(GATHER_SLOTS):
            @pl.when(last % GATHER_SLOTS == s)
            def _():
                for k in range(1, GATHER_SLOTS):
                    gather_wait((s + k) % GATHER_SLOTS)
        out_copy(0, 0).wait()
        out_copy(0, 1).wait()
        yb0 = ybufs[0]
        yb0[...] = jnp.zeros(yb0.shape, U32)

        def zstart(b, _):
            out_copy(b, 0).start()
            return 0

        def zwait(b, _):
            out_copy(b, 0).wait()
            return 0

        lax.fori_loop(nused, nblocks, zstart, 0)
        lax.fori_loop(nused, nblocks, zwait, 0)


def _moe(bend, cnt, pos, x1, w_gate, w_up, w_down, layer, nblocks):
    assert 2 * x1.shape[0] > GATHER_SLOTS * MOE_BLOCK
    wspec = lambda a, b: pl.BlockSpec((None, None, a, b), lambda e, be, cn, ps: (layer, e, 0, 0))
    grid_spec = pltpu.PrefetchScalarGridSpec(
        num_scalar_prefetch=3,
        grid=(N_EXPERTS,),
        in_specs=[pl.BlockSpec(memory_space=pl.ANY), wspec(D_MODEL, EXPERT_FF), wspec(D_MODEL, EXPERT_FF),
                  wspec(EXPERT_FF, D_MODEL)],
        out_specs=pl.BlockSpec(memory_space=pl.ANY),
        scratch_shapes=([pltpu.VMEM((MOE_BLOCK, D_MODEL), F32) for _ in range(GATHER_SLOTS)]
                        + [pltpu.VMEM((MOE_BLOCK, PACKED_W), U32) for _ in range(2)]
                        + [pltpu.SemaphoreType.DMA((GATHER_SLOTS,)), pltpu.SemaphoreType.DMA((2,)),
                           pltpu.VMEM((D_MODEL, EXPERT_FF), BF16),
                           pltpu.VMEM((D_MODEL, EXPERT_FF), BF16),
                           pltpu.VMEM((EXPERT_FF, D_MODEL), BF16),
                           pltpu.SMEM((nblocks * MOE_BLOCK,), jnp.int32)]),
    )
    return pl.pallas_call(
        functools.partial(_moe_body, nblocks=nblocks),
        grid_spec=grid_spec,
        out_shape=jax.ShapeDtypeStruct((nblocks * MOE_BLOCK, PACKED_W), U32),
        compiler_params=_cparams(("arbitrary",), bounds_checks=False),
        name="moe_experts",
    )(bend, cnt, pos, x1, w_gate, w_up, w_down)


COMB_T = 128


def _combine_body(pos_ref, ys_hbm, x_ref, rt_ref, g_ref, b_ref, op_ref, os_ref, *scratch, n_prompt_blocks):
    i = pl.program_id(0)
    nsteps = pl.num_programs(0)
    bufs = tuple((scratch[2 * s], scratch[2 * s + 1]) for s in range(GATHER_SLOTS))
    sem = scratch[2 * GATHER_SLOTS]
    ahead = GATHER_SLOTS - 1

    def gather_start(step, s):
        base = step * (2 * COMB_T)
        for r in range(COMB_T):
            for k in range(2):
                p = pos_ref[base + 2 * r + k]
                pltpu.make_async_copy(ys_hbm.at[pl.ds(p, 1), :], bufs[s][k].at[pl.ds(r, 1), :],
                                      sem.at[s]).start(priority=k)

    def gather_wait(s):
        for k in range(2):
            pltpu.make_async_copy(ys_hbm.at[pl.ds(0, COMB_T), :], bufs[s][k], sem.at[s]).wait()

    @pl.when(i == 0)
    def _():
        for k in range(ahead):
            gather_start(k, k)

    def do_tile(s):
        gather_wait(s)
        gather_start(jnp.minimum(i + ahead, nsteps - 1), (s + ahead) % GATHER_SLOTS)
        rt = rt_ref[...]
        ff = (rt[:, R_W0:R_W0 + 1] * _unpack_bf16_pairs(bufs[s][0][...])
              + rt[:, R_W1:R_W1 + 1] * _unpack_bf16_pairs(bufs[s][1][...]))
        out = _layer_norm(DEEPNORM_ALPHA * x_ref[...] + ff, g_ref[...], b_ref[...])

        @pl.when(i < n_prompt_blocks)
        def _():
            op_ref[...] = out

        @pl.when(i >= n_prompt_blocks)
        def _():
            os_ref[...] = out

        @pl.when(i == nsteps - 1)
        def _():
            for k in range(1, GATHER_SLOTS):
                gather_wait((s + k) % GATHER_SLOTS)

    for s in range(GATHER_SLOTS):
        @pl.when(i % GATHER_SLOTS == s)
        def _():
            do_tile(s)


def _combine(pos, ys, x1, route, g, b, n_prompt_rows):
    rows = x1.shape[0]
    npb = n_prompt_rows // COMB_T
    assert rows // COMB_T >= GATHER_SLOTS
    grid_spec = pltpu.PrefetchScalarGridSpec(
        num_scalar_prefetch=1,
        grid=(rows // COMB_T,),
        in_specs=[pl.BlockSpec(memory_space=pl.ANY),
                  pl.BlockSpec((COMB_T, D_MODEL), lambda i, p: (i, 0)),
                  pl.BlockSpec((COMB_T, LANES), lambda i, p: (i, 0)),
                  pl.BlockSpec((1, D_MODEL), lambda i, p: (0, 0)),
                  pl.BlockSpec((1, D_MODEL), lambda i, p: (0, 0))],
        out_specs=[pl.BlockSpec((COMB_T, D_MODEL), lambda i, p: (jnp.minimum(i, npb - 1), 0)),
                   pl.BlockSpec((COMB_T, D_MODEL), lambda i, p: (jnp.maximum(i - npb, 0), 0))],
        scratch_shapes=([pltpu.VMEM((COMB_T, PACKED_W), U32) for _ in range(2 * GATHER_SLOTS)]
                        + [pltpu.SemaphoreType.DMA((GATHER_SLOTS,))]),
    )
    return pl.pallas_call(
        functools.partial(_combine_body, n_prompt_blocks=npb),
        grid_spec=grid_spec,
        out_shape=[jax.ShapeDtypeStruct((n_prompt_rows, D_MODEL), F32),
                   jax.ShapeDtypeStruct((rows - n_prompt_rows, D_MODEL), F32)],
        compiler_params=_cparams(("arbitrary",), bounds_checks=False),
        name="combine_ln2",
    )(pos, ys, x1, route, g, b)


def kernel(x_prompt, x_sample, cache_k, cache_v, state_ssm_re, state_ssm_im, w_in, ssm_lambda_re, ssm_lambda_im, ssm_log_dt, ssm_b_re, ssm_b_im, ssm_c_re, ssm_c_im, ssm_d, ssm_w_glu, attn_sinks, attn_norm_g, ssm_norm_g, w_out, ln1_g, ln1_b, router_group_w, router_group_b, router_expert_w, router_expert_b, expert_w_gate, expert_w_up, expert_w_down, ln2_g, ln2_b):
    nbatch, seq, _ = x_prompt.shape
    nseq, dec_seq, _ = x_sample.shape
    wbuf = cache_k.shape[2]
    rows_p = nbatch * seq
    rows_s = nseq * dec_seq
    rows_all = rows_p + rows_s
    tm_p = 512

    xp = x_prompt.reshape(rows_p, D_MODEL)
    xs = x_sample.reshape(rows_s, D_MODEL)
    tab_p = _rope_tables(np.arange(seq))
    tab_s = _rope_tables(PAST_LEN + (np.arange(rows_s) % dec_seq))
    ck_all = cache_k.astype(F32).reshape(DEPTH, nseq, wbuf, KV_WIDTH)
    cv_all = cache_v.astype(F32).reshape(DEPTH, nseq, wbuf, KV_WIDTH)

    sp = jax.vmap(_ssm_params)(ssm_lambda_re, ssm_lambda_im, ssm_log_dt, ssm_b_re, ssm_b_im,
                               ssm_c_re, ssm_c_im, ssm_d)
    w_in_bf = w_in.astype(BF16)
    w_glu_bf = ssm_w_glu.astype(BF16)
    w_out_bf = w_out.astype(BF16)

    outs = {k: [] for k in ("kp", "vp", "hrp", "hip", "ks", "vs", "hrs", "his")}
    for l in range(DEPTH):
        sinks = attn_sinks[l].astype(F32)
        attn_g = attn_norm_g[l].astype(F32).reshape(1, ATTN_WIDTH)
        ssm_g = ssm_norm_g[l].astype(F32).reshape(1, SSM_WIDTH)
        g1 = ln1_g[l].astype(F32).reshape(1, D_MODEL)
        b1 = ln1_b[l].astype(F32).reshape(1, D_MODEL)
        g2 = ln2_g[l].astype(F32).reshape(1, D_MODEL)
        b2 = ln2_b[l].astype(F32).reshape(1, D_MODEL)
        n_rt = N_EXPERT_GROUPS + N_EXPERTS
        wr = jnp.concatenate([router_group_w[l].astype(F32)]
                             + [router_expert_w[l, g].astype(F32) for g in range(N_EXPERT_GROUPS)]
                             + [jnp.zeros((D_MODEL, LANES - n_rt), F32)], axis=1)
        br = jnp.concatenate([router_group_b[l].astype(F32), router_expert_b[l].astype(F32).reshape(-1),
                              jnp.zeros((LANES - n_rt,), F32)]).reshape(1, LANES)

        q_p, k_p, v_p, u_p = _inproj(xp, w_in_bf, l, *tab_p, tm=tm_p)
        q_s, k_s, v_s, u_s = _inproj(xs, w_in_bf, l, *tab_s, tm=rows_s)

        an_p = _attn_prompt(sinks, q_p, k_p, v_p, attn_g, seq)
        an_s, nk_s, nv_s = _attn_sample(sinks, q_s, k_s, v_s, ck_all, cv_all, attn_g, dec_seq, l)

        y_p, hl_p = _ssm_prompt(u_p, sp, l, nbatch, seq)
        y_s, hl_s = _ssm_sample(u_s, _pack_state(state_ssm_re[l], state_ssm_im[l]), sp, l, nseq, dec_seq)
        sn_p = _glu(y_p, w_glu_bf, l, ssm_g, tm_p)
        sn_s = _glu(y_s, w_glu_bf, l, ssm_g, rows_s)

        x1, lg_p = _outproj(an_p, sn_p, xp, w_out_bf, l, g1, b1, wr, br, tm_p, rows_all, 0)
        x1, lg_s = _outproj(an_s, sn_s, xs, w_out_bf, l, g1, b1, wr, br, rows_s, rows_all, rows_p // rows_s,
                            x1_buf=x1)
        nblocks = 2 * rows_all // MOE_BLOCK + N_EXPERTS
        route, meta = _route_plan(jnp.concatenate([lg_p, lg_s], axis=0))
        pos = route[:, R_P0:R_P1 + 1].astype(jnp.int32).reshape(2 * rows_all)
        bend = meta[0, :N_EXPERTS].astype(jnp.int32)

        cnt = meta[1, :N_EXPERTS].astype(jnp.int32)
        ys = _moe(bend, cnt, pos, x1, expert_w_gate, expert_w_up, expert_w_down, l, nblocks)
        xp, xs = _combine(pos, ys, x1, route, g2, b2, rows_p)

        wp = min(WINDOW, seq)
        tail = lambda a: a.reshape(nbatch, seq, KV_WIDTH)[:, seq - wp:].reshape(nbatch, wp, N_KV_HEADS, HEAD_DIM)
        outs["kp"].append(tail(k_p))
        outs["vp"].append(tail(v_p))
        hp = hl_p.reshape(nbatch, SSM_CHUNKS * 2 * CH_STATES)
        hr, hi = _unpack_state(hp, nbatch)
        outs["hrp"].append(hr)
        outs["hip"].append(hi)
        outs["ks"].append(nk_s.reshape(nseq, wbuf, N_KV_HEADS, HEAD_DIM))
        outs["vs"].append(nv_s.reshape(nseq, wbuf, N_KV_HEADS, HEAD_DIM))
        hr, hi = _unpack_state(hl_s, nseq)
        outs["hrs"].append(hr)
        outs["his"].append(hi)

    st = {k: jnp.stack(v) for k, v in outs.items()}
    return (xp.reshape(nbatch, seq, D_MODEL), xs.reshape(nseq, dec_seq, D_MODEL),
            st["kp"], st["vp"], st["hrp"], st["hip"], st["ks"], st["vs"], st["hrs"], st["his"])
```

```python
import functools

import numpy as np
import jax
import jax.numpy as jnp
from jax import lax
from jax.experimental import pallas as pl
from jax.experimental.pallas import tpu as pltpu

F32 = jnp.float32
BF16 = jnp.bfloat16

D_MODEL = 2048
DEPTH = 2
PAST_LEN = 16384
ATTN_WIDTH = 1024
SSM_WIDTH = 1024
HEAD_DIM = 64
N_HEADS = 16
N_KV_HEADS = 4
Q_PER_KV = 4
KV_WIDTH = 256
WINDOW = 128
ROPE_THETA = 10000.0
SSM_GROUP = 16
SSM_GROUPS = 64
SSM_STATE = 64
N_EXPERT_GROUPS = 4
EXPERTS_PER_GROUP = 8
N_EXPERTS = 32
EXPERT_FF = 512
MOE_BLOCK = 128
DEEPNORM_ALPHA = (2.0 * DEPTH) ** 0.25
LN_EPS = 1e-5
RMS_EPS = 1e-6

LANES = 128
SUBLANES = 8
VMEM_LIMIT = 56 * 1024 * 1024

SSM_CHUNKS = 4
CH_GROUPS = SSM_GROUPS // SSM_CHUNKS
CH_STATES = CH_GROUPS * SSM_STATE
CH_IN = CH_GROUPS * SSM_GROUP
SCAN_T = 512
SCAN_SEG = SCAN_T // SUBLANES


def _cparams(sem=None, vmem=VMEM_LIMIT, bounds_checks=True):
    return pltpu.CompilerParams(dimension_semantics=sem, vmem_limit_bytes=vmem,
                                disable_bounds_checks=not bounds_checks)


def _full(shape):
    n = len(shape)
    return pl.BlockSpec(shape, lambda *a: (0,) * n)


def _dot_nt(a, b):
    return lax.dot_general(a, b, (((1,), (1,)), ((), ())), preferred_element_type=F32)


def _inproj_body(x_ref, w_ref, cos_ref, sa_ref, sb_ref, q_ref, k_ref, v_ref, u_ref):
    x = x_ref[...].astype(BF16)
    cos = cos_ref[...]
    sa = sa_ref[...]
    sb = sb_ref[...]

    half = HEAD_DIM // 2

    def rope(c):
        return c * cos + pltpu.roll(c, LANES - half, 1) * sa + pltpu.roll(c, half, 1) * sb

    pq = jnp.dot(x, w_ref[:, 0:ATTN_WIDTH], preferred_element_type=F32)
    for j in range(ATTN_WIDTH // LANES):
        q_ref[:, j * LANES:(j + 1) * LANES] = rope(pq[:, j * LANES:(j + 1) * LANES]).astype(BF16)
    pk = jnp.dot(x, w_ref[:, ATTN_WIDTH:ATTN_WIDTH + KV_WIDTH], preferred_element_type=F32)
    for j in range(KV_WIDTH // LANES):
        k_ref[:, j * LANES:(j + 1) * LANES] = rope(pk[:, j * LANES:(j + 1) * LANES])
    v_ref[...] = jnp.dot(x, w_ref[:, ATTN_WIDTH + KV_WIDTH:ATTN_WIDTH + 2 * KV_WIDTH],
                         preferred_element_type=F32)
    u_ref[...] = jnp.dot(x, w_ref[:, ATTN_WIDTH + 2 * KV_WIDTH:], preferred_element_type=F32)


def _layer_spec(shape, layer):
    return pl.BlockSpec((None,) + tuple(shape), lambda *a: (layer,) + (0,) * len(shape))


def _inproj(x, w_bf, layer, cos_t, sa_t, sb_t, tm):
    rows = x.shape[0]
    tab_blocks = cos_t.shape[0] // tm
    in_w = w_bf.shape[2]
    row_spec = lambda w: pl.BlockSpec((tm, w), lambda i: (i, 0))
    tab_spec = pl.BlockSpec((tm, LANES), lambda i: (i % tab_blocks, 0))
    return pl.pallas_call(
        _inproj_body,
        grid=(rows // tm,),
        in_specs=[row_spec(D_MODEL), _layer_spec((D_MODEL, in_w), layer), tab_spec, tab_spec, tab_spec],
        out_specs=[row_spec(ATTN_WIDTH), row_spec(KV_WIDTH), row_spec(KV_WIDTH), row_spec(SSM_WIDTH)],
        out_shape=[jax.ShapeDtypeStruct((rows, ATTN_WIDTH), BF16),
                   jax.ShapeDtypeStruct((rows, KV_WIDTH), F32),
                   jax.ShapeDtypeStruct((rows, KV_WIDTH), F32),
                   jax.ShapeDtypeStruct((rows, SSM_WIDTH), F32)],
        compiler_params=_cparams(("arbitrary",)),
        name="inproj_rope",
    )(x, w_bf, cos_t, sa_t, sb_t)


def _rope_tables(positions):
    half = HEAD_DIM // 2
    inv_freq = ROPE_THETA ** (-np.arange(half, dtype=np.float64) / half)
    ang = positions.astype(np.float64)[:, None] * inv_freq[None, :]
    lane = np.arange(LANES)
    cos = np.cos(ang)[:, lane % half]
    sin = np.sin(ang)[:, lane % half]
    first = (lane % HEAD_DIM) < half
    sa = np.where(first[None, :], -sin, 0.0)
    sb = np.where(first[None, :], 0.0, sin)
    return (jnp.asarray(cos, F32), jnp.asarray(sa, F32), jnp.asarray(sb, F32))


def _rms(x, g):
    return x * lax.rsqrt(jnp.mean(jnp.square(x), axis=-1, keepdims=True) + RMS_EPS) * g


def _attn_prompt_body(sink_ref, q_ref, kp_ref, kc_ref, vp_ref, vc_ref, g_ref, o_ref, acc_ref, *, blocks_per_seq):
    i = pl.program_id(0)
    has_prev = (i % blocks_per_seq) > 0
    kk = jnp.concatenate([kp_ref[...], kc_ref[...]], axis=0).astype(BF16)
    vt = jnp.concatenate([vp_ref[...], vc_ref[...]], axis=0).T.astype(BF16)
    ncol = Q_PER_KV * WINDOW
    key = lax.broadcasted_iota(jnp.int32, (2 * WINDOW, ncol), 0)
    col = lax.broadcasted_iota(jnp.int32, (2 * WINDOW, ncol), 1)
    qry = col % WINDOW
    visible = (key > qry) & (key <= qry + WINDOW) & ((key >= WINDOW) | has_prev)
    head = lax.broadcasted_iota(jnp.int32, (1, ncol), 1) // WINDOW
    q = q_ref[...] * (HEAD_DIM ** -0.5)
    scores = []
    for g in range(N_KV_HEADS):
        qg = jnp.concatenate([q[:, (g * Q_PER_KV + j) * HEAD_DIM:(g * Q_PER_KV + j + 1) * HEAD_DIM]
                              for j in range(Q_PER_KV)], axis=0)
        kh = kk[:, g * HEAD_DIM:(g + 1) * HEAD_DIM]
        scores.append(lax.dot_general(kh, qg, (((1,), (1,)), ((), ())), preferred_element_type=F32))
    for g in range(N_KV_HEADS):
        st = jnp.where(visible, scores[g], -jnp.inf)
        sink = jnp.zeros((1, ncol), F32)
        for j in range(Q_PER_KV):
            sink = jnp.where(head == j, sink_ref[g * Q_PER_KV + j], sink)
        m = jnp.maximum(jnp.max(st, axis=0, keepdims=True), sink)
        p = jnp.exp(st - m)
        denom = jnp.sum(p, axis=0, keepdims=True) + jnp.exp(sink - m)
        ot = jnp.dot(vt[g * HEAD_DIM:(g + 1) * HEAD_DIM, :], p.astype(BF16), preferred_element_type=F32)
        ot = ot / denom
        for j in range(Q_PER_KV):
            h = g * Q_PER_KV + j
            acc_ref[h * HEAD_DIM:(h + 1) * HEAD_DIM, :] = ot[:, j * WINDOW:(j + 1) * WINDOW]
    o_ref[...] = _rms(acc_ref[...].T, g_ref[...]).astype(BF16)


def _attn_prompt(sinks, q, k, v, g, seq):
    rows = q.shape[0]
    nblk = rows // WINDOW
    bps = seq // WINDOW
    cur = lambda w: pl.BlockSpec((WINDOW, w), lambda i: (i, 0))
    prev = lambda w: pl.BlockSpec((WINDOW, w), lambda i: (jnp.maximum(i - 1, 0), 0))
    return pl.pallas_call(
        functools.partial(_attn_prompt_body, blocks_per_seq=bps),
        grid=(nblk,),
        in_specs=[pl.BlockSpec(memory_space=pltpu.SMEM), cur(ATTN_WIDTH), prev(KV_WIDTH), cur(KV_WIDTH),
                  prev(KV_WIDTH), cur(KV_WIDTH), _full((1, ATTN_WIDTH))],
        out_specs=cur(ATTN_WIDTH),
        out_shape=jax.ShapeDtypeStruct((rows, ATTN_WIDTH), BF16),
        scratch_shapes=[pltpu.VMEM((ATTN_WIDTH, WINDOW), F32)],
        compiler_params=_cparams(("arbitrary",)),
        name="attn_prompt",
    )(sinks, q, k, k, v, v, g)


SEQ_PER_STEP = 8


def _attn_sample_body(sink_ref, q_ref, kn_ref, vn_ref, ck_ref, cv_ref, g_ref, o_ref, nk_ref, nv_ref, acc_ref,
                      *, dec_seq):
    nq = SEQ_PER_STEP * dec_seq
    wb = ck_ref.shape[1]
    nkc = SEQ_PER_STEP * wb
    ck = ck_ref[...].reshape(nkc, KV_WIDTH)
    cv = cv_ref[...].reshape(nkc, KV_WIDTH)
    kn = kn_ref[...]
    vn = vn_ref[...]
    ckb = ck.astype(BF16)
    cvb = cv.astype(BF16)
    knb = kn.astype(BF16)
    vnb = vn.astype(BF16)
    q = q_ref[...]

    rows = Q_PER_KV * nq
    r = lax.broadcasted_iota(jnp.int32, (rows, nkc), 0) % nq
    c = lax.broadcasted_iota(jnp.int32, (rows, nkc), 1)
    vis_c = ((r // dec_seq) == (c // wb)) & ((c % wb) >= (r % dec_seq) + 1 + (wb - WINDOW))
    rn = lax.broadcasted_iota(jnp.int32, (rows, nq), 0) % nq
    cn = lax.broadcasted_iota(jnp.int32, (rows, nq), 1)
    vis_n = ((rn // dec_seq) == (cn // dec_seq)) & ((cn % dec_seq) <= (rn % dec_seq))
    hrow = lax.broadcasted_iota(jnp.int32, (rows, 1), 0) // nq

    dn = (((1,), (1,)), ((), ()))
    for g in range(N_KV_HEADS):
        qs = jnp.concatenate([q[:, (g * Q_PER_KV + j) * HEAD_DIM:(g * Q_PER_KV + j + 1) * HEAD_DIM]
                              for j in range(Q_PER_KV)], axis=0)
        sl = slice(g * HEAD_DIM, (g + 1) * HEAD_DIM)
        sc = lax.dot_general(qs, ckb[:, sl], dn, preferred_element_type=F32) * (HEAD_DIM ** -0.5)
        sn = lax.dot_general(qs, knb[:, sl], dn, preferred_element_type=F32) * (HEAD_DIM ** -0.5)
        sc = jnp.where(vis_c, sc, -jnp.inf)
        sn = jnp.where(vis_n, sn, -jnp.inf)
        sink = jnp.zeros((rows, 1), F32)
        for j in range(Q_PER_KV):
            sink = jnp.where(hrow == j, sink_ref[g * Q_PER_KV + j], sink)
        m = jnp.maximum(jnp.maximum(jnp.max(sc, axis=-1, keepdims=True), jnp.max(sn, axis=-1, keepdims=True)), sink)
        pc = jnp.exp(sc - m)
        pn = jnp.exp(sn - m)
        denom = jnp.sum(pc, axis=-1, keepdims=True) + jnp.sum(pn, axis=-1, keepdims=True) + jnp.exp(sink - m)
        o = (jnp.dot(pc.astype(BF16), cvb[:, sl], preferred_element_type=F32)
             + jnp.dot(pn.astype(BF16), vnb[:, sl], preferred_element_type=F32)) / denom
        for j in range(Q_PER_KV):
            h = g * Q_PER_KV + j
            acc_ref[:, h * HEAD_DIM:(h + 1) * HEAD_DIM] = o[j * nq:(j + 1) * nq, :]
    o_ref[...] = _rms(acc_ref[...], g_ref[...]).astype(BF16)

    keep = wb - dec_seq
    for s in range(SEQ_PER_STEP):
        nk_ref[s, 0:keep, :] = ck_ref[s, dec_seq:wb, :]
        nv_ref[s, 0:keep, :] = cv_ref[s, dec_seq:wb, :]
        nk_ref[s, keep:wb, :] = kn[s * dec_seq:(s + 1) * dec_seq, :]
        nv_ref[s, keep:wb, :] = vn[s * dec_seq:(s + 1) * dec_seq, :]


def _attn_sample(sinks, q, kn, vn, ck, cv, g, dec_seq, layer):
    rows = q.shape[0]
    _, nseq, wb, _ = ck.shape
    nq = SEQ_PER_STEP * dec_seq
    rowb = lambda w: pl.BlockSpec((nq, w), lambda i: (i, 0))
    cin = pl.BlockSpec((None, SEQ_PER_STEP, wb, KV_WIDTH), lambda i: (layer, i, 0, 0))
    cb = pl.BlockSpec((SEQ_PER_STEP, wb, KV_WIDTH), lambda i: (i, 0, 0))
    return pl.pallas_call(
        functools.partial(_attn_sample_body, dec_seq=dec_seq),
        grid=(nseq // SEQ_PER_STEP,),
        in_specs=[pl.BlockSpec(memory_space=pltpu.SMEM), rowb(ATTN_WIDTH), rowb(KV_WIDTH), rowb(KV_WIDTH),
                  cin, cin, _full((1, ATTN_WIDTH))],
        out_specs=[rowb(ATTN_WIDTH), cb, cb],
        out_shape=[jax.ShapeDtypeStruct((rows, ATTN_WIDTH), BF16),
                   jax.ShapeDtypeStruct(ck.shape[1:], F32), jax.ShapeDtypeStruct(cv.shape[1:], F32)],
        scratch_shapes=[pltpu.VMEM((nq, ATTN_WIDTH), F32)],
        compiler_params=_cparams(("arbitrary",)),
        name="attn_sample",
    )(sinks, q, kn, vn, ck, cv, g)


def _ssm_prompt_body(ua_ref, uc_ref, b_ref, c_ref, lr_ref, li_ref, pr_ref, pi_ref, d_ref, y_ref, hl_ref,
                     uperm_ref, bu0, bu1, hb0, hb1, yp_ref, carry_ref, *, nchunks, ntc):
    k = pl.program_id(0)
    bus = (bu0, bu1)
    hbs = (hb0, hb1)
    lanes_per_chunk = CH_IN // LANES
    seg = lambda s: slice(s * SCAN_SEG, (s + 1) * SCAN_SEG)
    lane = lambda c: slice(c * LANES, (c + 1) * LANES)

    def stage_a(item, slot):
        cc = item % SSM_CHUNKS
        for c in range(lanes_per_chunk):
            for s in range(SUBLANES):
                uperm_ref[c, pl.ds(s, SCAN_SEG, stride=SUBLANES), :] = ua_ref[seg(s), lane(c)]
        up = jnp.concatenate([uperm_ref[c] for c in range(lanes_per_chunk)], axis=1)
        bus[slot][...] = _dot_nt(up.astype(BF16), b_ref[cc])

    def stage_b(item, slot):
        cc = item % SSM_CHUNKS
        tc = (item // SSM_CHUNKS) % ntc
        bu_ref = bus[slot]
        hb_ref = hbs[slot]
        ar = jnp.broadcast_to(lr_ref[cc], (SUBLANES, CH_STATES))
        ai = jnp.broadcast_to(li_ref[cc], (SUBLANES, CH_STATES))

        def advance(h, j):
            hr, hi = h
            br = bu_ref[j * SUBLANES:(j + 1) * SUBLANES, 0:CH_STATES]
            bi = bu_ref[j * SUBLANES:(j + 1) * SUBLANES, CH_STATES:2 * CH_STATES]
            return ar * hr + (br - ai * hi), ar * hi + (bi + ai * hr)

        zero = jnp.zeros((SUBLANES, CH_STATES), F32)
        h = (zero, zero)
        for j in range(SCAN_SEG):
            h = advance(h, j)
        fr, fi = h

        cst = jnp.where(tc == 0, 0.0, carry_ref[cc])
        c_r = cst[0:1, :]
        c_i = cst[1:2, :]
        lsr = pr_ref[cc]
        lsi = pi_ref[cc]
        rows_r, rows_i = [], []
        for s in range(SUBLANES):
            rows_r.append(c_r)
            rows_i.append(c_i)
            n_r = fr[s:s + 1, :] + (lsr * c_r - lsi * c_i)
            n_i = fi[s:s + 1, :] + (lsr * c_i + lsi * c_r)
            c_r, c_i = n_r, n_i
        carry_ref[cc] = jnp.concatenate([c_r, c_i], axis=0)
        hl_ref[0, cc] = jnp.concatenate([c_r, c_i], axis=0)

        h = (jnp.concatenate(rows_r, axis=0), jnp.concatenate(rows_i, axis=0))
        for jj in range(SCAN_SEG // 2):
            h1 = advance(h, 2 * jj)
            h = advance(h1, 2 * jj + 1)
            rows = slice(2 * jj * SUBLANES, (2 * jj + 2) * SUBLANES)
            hb_ref[rows, 0:CH_STATES] = jnp.concatenate([h1[0], h[0]], axis=0).astype(BF16)
            hb_ref[rows, CH_STATES:2 * CH_STATES] = jnp.concatenate([h1[1], h[1]], axis=0).astype(BF16)

    def stage_c(item, slot):
        cc = item % SSM_CHUNKS
        yp = _dot_nt(hbs[slot][...], c_ref[cc])
        d = d_ref[cc]
        for c in range(lanes_per_chunk):
            yp_ref[c] = yp[:, lane(c)]
        for s in range(SUBLANES):
            for c in range(lanes_per_chunk):
                y_ref[seg(s), lane(c)] = (yp_ref[c, pl.ds(s, SCAN_SEG, stride=SUBLANES), :]
                                          + d[:, lane(c)] * uc_ref[seg(s), lane(c)])

    def run(do_a, do_b, do_c, parity):
        if do_a:
            stage_a(k, parity)
        if do_b:
            stage_b(k - 1, 1 - parity)
        if do_c:
            stage_c(k - 2, parity)

    @pl.when(k == 0)
    def _():
        carry_ref[...] = jnp.zeros(carry_ref.shape, F32)
        run(True, False, False, 0)

    @pl.when(k == 1)
    def _():
        run(True, True, False, 1)

    for parity in range(2):
        @pl.when((k >= 2) & (k < nchunks) & (k % 2 == parity))
        def _():
            run(True, True, True, parity)

    @pl.when(k == nchunks)
    def _():
        run(False, True, True, nchunks % 2)

    @pl.when(k == nchunks + 1)
    def _():
        run(False, False, True, (nchunks + 1) % 2)


def _ssm_prompt(u, sp, layer, nbatch, seq):
    rows = u.shape[0]
    ntc = seq // SCAN_T
    nchunks = nbatch * ntc * SSM_CHUNKS
    assert nchunks >= 2
    item_a = lambda k: jnp.minimum(k, nchunks - 1)
    item_b = lambda k: jnp.clip(k - 1, 0, nchunks - 1)
    item_c = lambda k: jnp.clip(k - 2, 0, nchunks - 1)
    chunk_block = lambda item: (item // SSM_CHUNKS, item % SSM_CHUNKS)
    y, hl = pl.pallas_call(
        functools.partial(_ssm_prompt_body, nchunks=nchunks, ntc=ntc),
        grid=(nchunks + 2,),
        in_specs=[pl.BlockSpec((SCAN_T, CH_IN), lambda k: chunk_block(item_a(k))),
                  pl.BlockSpec((SCAN_T, CH_IN), lambda k: chunk_block(item_c(k))),
                  ] + [_layer_spec(sp[n].shape[1:], layer) for n in ("B", "C", "lr", "li", "pr", "pi", "d")],
        out_specs=[pl.BlockSpec((SCAN_T, CH_IN), lambda k: chunk_block(item_c(k))),
                   pl.BlockSpec((1, SSM_CHUNKS, 2, CH_STATES),
                                lambda k: (item_b(k) // (ntc * SSM_CHUNKS), 0, 0, 0))],
        out_shape=[jax.ShapeDtypeStruct((rows, SSM_WIDTH), F32),
                   jax.ShapeDtypeStruct((nbatch, SSM_CHUNKS, 2, CH_STATES), F32)],
        scratch_shapes=[pltpu.VMEM((CH_IN // LANES, SCAN_T, LANES), F32),
                        pltpu.VMEM((SCAN_T, 2 * CH_STATES), F32), pltpu.VMEM((SCAN_T, 2 * CH_STATES), F32),
                        pltpu.VMEM((SCAN_T, 2 * CH_STATES), BF16), pltpu.VMEM((SCAN_T, 2 * CH_STATES), BF16),
                        pltpu.VMEM((CH_IN // LANES, SCAN_T, LANES), F32),
                        pltpu.VMEM((SSM_CHUNKS, 2, CH_STATES), F32)],
        compiler_params=_cparams(("arbitrary",)),
        name="ssm_prompt",
    )(u, u, sp["B"], sp["C"], sp["lr"], sp["li"], sp["pr"], sp["pi"], sp["d"])
    return y, hl


def _ssm_sample_body(u_ref, h0_ref, b_ref, c_ref, lr_ref, li_ref, d_ref, y_ref, ho_ref, bu_ref, *, nseq, dec_seq):
    nl = CH_STATES // LANES
    for k in range(SSM_CHUNKS):
        uk = u_ref[:, k * CH_IN:(k + 1) * CH_IN]
        bu = _dot_nt(uk.astype(BF16), b_ref[k])
        for c in range(2 * nl):
            bu_ref[c] = bu[:, c * LANES:(c + 1) * LANES]
        base = k * 2 * CH_STATES
        hr = h0_ref[:, base:base + CH_STATES]
        hi = h0_ref[:, base + CH_STATES:base + 2 * CH_STATES]
        ar = lr_ref[k]
        ai = li_ref[k]
        for t in range(dec_seq):
            br = jnp.concatenate([bu_ref[c, pl.ds(t, nseq, stride=dec_seq), :] for c in range(nl)], axis=1)
            bi = jnp.concatenate([bu_ref[nl + c, pl.ds(t, nseq, stride=dec_seq), :] for c in range(nl)], axis=1)
            hr, hi = ar * hr + (br - ai * hi), ar * hi + (bi + ai * hr)
            for c in range(nl):
                bu_ref[c, pl.ds(t, nseq, stride=dec_seq), :] = hr[:, c * LANES:(c + 1) * LANES]
                bu_ref[nl + c, pl.ds(t, nseq, stride=dec_seq), :] = hi[:, c * LANES:(c + 1) * LANES]
        ho_ref[:, base:base + CH_STATES] = hr
        ho_ref[:, base + CH_STATES:base + 2 * CH_STATES] = hi
        hall = jnp.concatenate([bu_ref[c] for c in range(2 * nl)], axis=1)
        y_ref[:, k * CH_IN:(k + 1) * CH_IN] = (
            _dot_nt(hall.astype(BF16), c_ref[k]) + d_ref[k] * uk)


def _ssm_sample(u, h0, sp, layer, nseq, dec_seq):
    rows = u.shape[0]
    return pl.pallas_call(
        functools.partial(_ssm_sample_body, nseq=nseq, dec_seq=dec_seq),
        grid=(1,),
        in_specs=[_full(u.shape), _full(h0.shape)] + [_layer_spec(sp[n].shape[1:], layer)
                                                      for n in ("B", "C", "lr", "li", "d")],
        out_specs=[_full((rows, SSM_WIDTH)), _full(h0.shape)],
        out_shape=[jax.ShapeDtypeStruct((rows, SSM_WIDTH), F32), jax.ShapeDtypeStruct(h0.shape, F32)],
        scratch_shapes=[pltpu.VMEM((2 * CH_STATES // LANES, rows, LANES), F32)],
        compiler_params=_cparams(("arbitrary",)),
        name="ssm_sample",
    )(u, h0, sp["B"], sp["C"], sp["lr"], sp["li"], sp["d"])


def _ssm_params(lambda_re, lambda_im, log_dt, b_re, b_im, c_re, c_im, d):
    lam = lax.complex(jnp.minimum(lambda_re.astype(F32), -1e-4), lambda_im.astype(F32))
    dt = jnp.exp(log_dt.astype(F32))[:, None]
    lam_dt = lam * dt
    lam_bar = jnp.exp(lam_dt)
    b = lax.complex(b_re.astype(F32), b_im.astype(F32))
    b_bar = ((lam_bar - 1.0) / lam)[..., None] * b
    rows_b = lambda part: part.reshape(SSM_CHUNKS, CH_STATES, SSM_GROUP)
    rows_c = lambda part: part.reshape(SSM_CHUNKS, CH_IN, SSM_STATE)
    pw = jnp.exp(lam_dt * float(SCAN_SEG)).reshape(SSM_CHUNKS, 1, CH_STATES)
    lb = lam_bar.reshape(SSM_CHUNKS, 1, CH_STATES)
    return dict(b_re=rows_b(jnp.real(b_bar)), b_im=rows_b(jnp.imag(b_bar)),
                c_re=rows_c(c_re.astype(F32)), c_im_neg=rows_c(-c_im.astype(F32)),
                lr=jnp.real(lb), li=jnp.imag(lb), pr=jnp.real(pw), pi=jnp.imag(pw),
                d=d.astype(F32).reshape(SSM_CHUNKS, 1, CH_IN))


def _ssm_pack_body(bre_ref, bim_ref, cre_ref, cimn_ref, eb_ref, ec_ref, b_out, c_out):
    def expand(rows_ref, onehot_ref, row_group, col_group):
        t = jnp.dot(rows_ref[...].astype(BF16), onehot_ref[...], preferred_element_type=F32)
        own = lax.broadcasted_iota(jnp.int32, t.shape, 0) // row_group
        col = lax.broadcasted_iota(jnp.int32, t.shape, 1) // col_group
        return jnp.where(own == col, t, 0.0).astype(BF16)

    b_out[0:CH_STATES, :] = expand(bre_ref, eb_ref, SSM_STATE, SSM_GROUP)
    b_out[CH_STATES:2 * CH_STATES, :] = expand(bim_ref, eb_ref, SSM_STATE, SSM_GROUP)
    c_out[:, 0:CH_STATES] = expand(cre_ref, ec_ref, SSM_GROUP, SSM_STATE)
    c_out[:, CH_STATES:2 * CH_STATES] = expand(cimn_ref, ec_ref, SSM_GROUP, SSM_STATE)


def _ssm_pack(sp):
    depth = sp["b_re"].shape[0]
    eb = jnp.asarray(np.tile(np.eye(SSM_GROUP, dtype=np.float32), (1, CH_GROUPS)), BF16)
    ec = jnp.asarray(np.tile(np.eye(SSM_STATE, dtype=np.float32), (1, CH_GROUPS)), BF16)
    item = lambda r, c: pl.BlockSpec((None, None, r, c), lambda l, k: (l, k, 0, 0))
    bmat, cmat = pl.pallas_call(
        _ssm_pack_body,
        grid=(depth, SSM_CHUNKS),
        in_specs=[item(CH_STATES, SSM_GROUP), item(CH_STATES, SSM_GROUP), item(CH_IN, SSM_STATE),
                  item(CH_IN, SSM_STATE), _full(eb.shape), _full(ec.shape)],
        out_specs=[item(2 * CH_STATES, CH_IN), item(CH_IN, 2 * CH_STATES)],
        out_shape=[jax.ShapeDtypeStruct((depth, SSM_CHUNKS, 2 * CH_STATES, CH_IN), BF16),
                   jax.ShapeDtypeStruct((depth, SSM_CHUNKS, CH_IN, 2 * CH_STATES), BF16)],
        compiler_params=_cparams(("arbitrary", "arbitrary")),
        name="ssm_pack",
    )(sp["b_re"], sp["b_im"], sp["c_re"], sp["c_im_neg"], eb, ec)
    return dict(sp, B=bmat, C=cmat)


def _pack_state(re, im):
    n = re.shape[0]
    st = jnp.stack([re.reshape(n, SSM_CHUNKS, CH_STATES), im.reshape(n, SSM_CHUNKS, CH_STATES)], axis=2)
    return st.reshape(n, SSM_CHUNKS * 2 * CH_STATES).astype(F32)


def _unpack_state(h, n):
    st = h.reshape(n, SSM_CHUNKS, 2, CH_GROUPS, SSM_STATE)
    return (st[:, :, 0].reshape(n, SSM_GROUPS, SSM_STATE), st[:, :, 1].reshape(n, SSM_GROUPS, SSM_STATE))


def _glu_body(y_ref, w_ref, g_ref, o_ref):
    a = jax.nn.gelu(y_ref[...])
    z = a * jax.nn.sigmoid(jnp.dot(a.astype(BF16), w_ref[...], preferred_element_type=F32))
    o_ref[...] = _rms(z, g_ref[...]).astype(BF16)


def _glu(y, w_bf, layer, g, tm):
    rows = y.shape[0]
    spec = pl.BlockSpec((tm, SSM_WIDTH), lambda i: (i, 0))
    return pl.pallas_call(
        _glu_body,
        grid=(rows // tm,),
        in_specs=[spec, _layer_spec((SSM_WIDTH, SSM_WIDTH), layer), _full((1, SSM_WIDTH))],
        out_specs=spec,
        out_shape=jax.ShapeDtypeStruct((rows, SSM_WIDTH), BF16),
        compiler_params=_cparams(("arbitrary",)),
        name="glu_rms",
    )(y, w_bf, g)


def _layer_norm(y, g, b):
    mu = jnp.mean(y, axis=-1, keepdims=True)
    yc = y - mu
    var = jnp.mean(jnp.square(yc), axis=-1, keepdims=True)
    return yc * lax.rsqrt(var + LN_EPS) * g + b


OUTPROJ_SUB = 256


def _outproj_body(*refs, aliased, nblk):
    if aliased:
        refs = refs[1:]
    a_ref, s_ref, x_ref, w_ref, g_ref, b_ref, wrc_ref, br_ref, o_ref, lg_ref = refs
    i = pl.program_id(0)

    @pl.when(i < nblk)
    def _():
        tm = a_ref.shape[0]
        sub = min(tm, OUTPROJ_SUB)
        for r0 in range(0, tm, sub):
            rows = slice(r0, r0 + sub)
            acc = jnp.dot(a_ref[rows, :], w_ref[0:ATTN_WIDTH, :], preferred_element_type=F32)
            acc = acc + jnp.dot(s_ref[rows, :], w_ref[ATTN_WIDTH:, :], preferred_element_type=F32)
            x1 = _layer_norm(DEEPNORM_ALPHA * x_ref[rows, :] + acc, g_ref[...], b_ref[...])
            o_ref[rows, :] = x1
            hh = jnp.dot(x1.astype(BF16), wrc_ref[...], preferred_element_type=F32)
            lg_ref[rows, :] = hh[:, 0:LANES] + hh[:, LANES:] + br_ref[...]

    @pl.when(i >= nblk)
    def _():
        o_ref[...] = jnp.zeros(o_ref.shape, F32)


def _outproj(an, sn, x, w_bf, layer, g, b, wr, br, tm, total_rows, row_block0, x1_buf=None):
    rows = an.shape[0]
    aliased = x1_buf is not None
    nblk = rows // tm
    tail = 0 if aliased or rows == total_rows else 1
    blk = lambda i: jnp.minimum(i, nblk - 1)
    half = lambda: pl.BlockSpec((tm, ATTN_WIDTH), lambda i: (blk(i), 0))
    in_specs = [half(), half(), pl.BlockSpec((tm, D_MODEL), lambda i: (blk(i), 0)),
                _layer_spec((D_MODEL, D_MODEL), layer), _full((1, D_MODEL)), _full((1, D_MODEL)),
                _full((D_MODEL, 2 * LANES)), _full((1, LANES))]
    wr_hi = wr.astype(BF16)
    wr_lo = (wr - wr_hi.astype(F32)).astype(BF16)
    args = [an, sn, x, w_bf, g, b, jnp.concatenate([wr_hi, wr_lo], axis=1), br]
    if aliased:
        in_specs = [pl.BlockSpec(memory_space=pl.ANY)] + in_specs
        args = [x1_buf] + args
    return pl.pallas_call(
        functools.partial(_outproj_body, aliased=aliased, nblk=nblk),
        grid=(nblk + tail,),
        in_specs=in_specs,
        out_specs=[pl.BlockSpec((tm, D_MODEL), lambda i: (i + row_block0, 0)),
                   pl.BlockSpec((tm, LANES), lambda i: (blk(i), 0))],
        out_shape=[jax.ShapeDtypeStruct((total_rows, D_MODEL), F32), jax.ShapeDtypeStruct((rows, LANES), F32)],
        input_output_aliases={0: 0} if aliased else {},
        compiler_params=_cparams(("arbitrary",)),
        name="outproj_ln1",
    )(*args)


ROUTE_T = 640
R_E0, R_E1, R_W0, R_W1, R_P0, R_P1 = range(6)


def _route_select(x):
    lane = lax.broadcasted_iota(jnp.int32, x.shape, 1)
    big = jnp.int32(1 << 20)
    neg = -jnp.inf
    gmask = lane < N_EXPERT_GROUPS
    gm = jnp.max(jnp.where(gmask, x, neg), axis=-1, keepdims=True)
    grp = jnp.min(jnp.where(gmask & (x == gm), lane, big), axis=-1, keepdims=True)
    p_grp = 1.0 / jnp.sum(jnp.where(gmask, jnp.exp(x - gm), 0.0), axis=-1, keepdims=True)
    lo = N_EXPERT_GROUPS + grp * EXPERTS_PER_GROUP
    emask = (lane >= lo) & (lane < lo + EXPERTS_PER_GROUP)
    v1 = jnp.max(jnp.where(emask, x, neg), axis=-1, keepdims=True)
    i1 = jnp.min(jnp.where(emask & (x == v1), lane, big), axis=-1, keepdims=True)
    emask2 = emask & (lane != i1)
    v2 = jnp.max(jnp.where(emask2, x, neg), axis=-1, keepdims=True)
    i2 = jnp.min(jnp.where(emask2 & (x == v2), lane, big), axis=-1, keepdims=True)
    e2 = jnp.exp(v2 - v1)
    w1 = p_grp / (1.0 + e2)
    w2 = p_grp * e2 / (1.0 + e2)
    return (i1 - N_EXPERT_GROUPS).astype(F32), (i2 - N_EXPERT_GROUPS).astype(F32), w1, w2


def _route_plan_body(lg_ref, ltri_ref, utri_ref, o_ref, meta_ref, rec_ref, run_ref):
    ph = pl.program_id(0)
    j = pl.program_id(1)
    tm = lg_ref.shape[0]
    lane = lax.broadcasted_iota(jnp.int32, (tm, LANES), 1)
    lanef = lane.astype(F32)
    r0 = pl.multiple_of(j * tm, tm)

    @pl.when((ph == 0) & (j == 0))
    def _():
        run_ref[...] = jnp.zeros(run_ref.shape, F32)

    @pl.when(ph == 0)
    def _():
        e0, e1, w0, w1 = _route_select(lg_ref[...])
        oh0 = (lanef == e0).astype(F32)
        oh1 = (lanef == e1).astype(F32)
        oh2 = oh0 + oh1
        before = jnp.dot(ltri_ref[...], oh2.astype(BF16), preferred_element_type=F32) + run_ref[0:1, :]
        rk0 = jnp.sum(before * oh0, axis=-1, keepdims=True)
        rk1 = jnp.sum(before * oh1, axis=-1, keepdims=True)
        run_ref[...] = run_ref[...] + jnp.sum(oh2, axis=0, keepdims=True)
        rec = jnp.where(lane == R_E0, e0, 0.0)
        rec = jnp.where(lane == R_E1, e1, rec)
        rec = jnp.where(lane == R_W0, w0, rec)
        rec = jnp.where(lane == R_W1, w1, rec)
        rec = jnp.where(lane == R_P0, rk0, rec)
        rec = jnp.where(lane == R_P1, rk1, rec)
        rec_ref[pl.ds(r0, tm), :] = rec

    @pl.when(ph == 1)
    def _():
        counts = run_ref[...]
        nblk = jnp.floor((counts + (MOE_BLOCK - 1)) * (1.0 / MOE_BLOCK))
        end_blk = jnp.dot(nblk.astype(BF16), utri_ref[...], preferred_element_type=F32)
        start_row = (end_blk - nblk) * MOE_BLOCK
        rec = rec_ref[pl.ds(r0, tm), :]
        oh0 = (lanef == rec[:, R_E0:R_E0 + 1]).astype(F32)
        oh1 = (lanef == rec[:, R_E1:R_E1 + 1]).astype(F32)
        p0 = jnp.sum(oh0 * start_row[0:1, :], axis=-1, keepdims=True) + rec[:, R_P0:R_P0 + 1]
        p1 = jnp.sum(oh1 * start_row[0:1, :], axis=-1, keepdims=True) + rec[:, R_P1:R_P1 + 1]
        out = jnp.where(lane == R_P0, p0, rec)
        out = jnp.where(lane == R_P1, p1, out)
        o_ref[...] = out

        @pl.when(j == 0)
        def _():
            mrow = lax.broadcasted_iota(jnp.int32, meta_ref.shape, 0)
            meta_ref[...] = jnp.where(mrow == 1, counts, end_blk)


def _route_plan(logits):
    rows = logits.shape[0]
    tm = ROUTE_T
    ltri = jnp.asarray(np.tril(np.ones((tm, tm), np.float32), -1), BF16)
    utri = jnp.asarray(np.triu(np.ones((LANES, LANES), np.float32)), BF16)
    mrows = SUBLANES
    return pl.pallas_call(
        _route_plan_body,
        grid=(2, rows // tm),
        in_specs=[pl.BlockSpec((tm, LANES), lambda p, j: (j * (1 - p), 0)),
                  pl.BlockSpec((tm, tm), lambda p, j: (0, 0)),
                  pl.BlockSpec((LANES, LANES), lambda p, j: (0, 0))],
        out_specs=[pl.BlockSpec((tm, LANES), lambda p, j: (j * p, 0)),
                   pl.BlockSpec((mrows, LANES), lambda p, j: (0, 0))],
        out_shape=[jax.ShapeDtypeStruct((rows, LANES), F32), jax.ShapeDtypeStruct((mrows, LANES), F32)],
        scratch_shapes=[pltpu.VMEM((rows, LANES), F32), pltpu.VMEM((SUBLANES, LANES), F32)],
        compiler_params=_cparams(("arbitrary", "arbitrary")),
        name="route_plan",
    )(logits, ltri, utri)


U32 = jnp.uint32
PACKED_W = D_MODEL // 2


def _pack_bf16_pairs(y):
    bits = lax.bitcast_convert_type(y.astype(BF16).astype(F32), U32)
    return bits[:, 0:PACKED_W] | (bits[:, PACKED_W:] >> 16)


def _unpack_bf16_pairs(p):
    hi = lax.bitcast_convert_type(p & jnp.uint32(0xFFFF0000), F32)
    lo = lax.bitcast_convert_type(p << 16, F32)
    return jnp.concatenate([hi, lo], axis=1)


def _build_rowmap(pos_ref, bend_ref, cnt_ref, o_ref):
    unroll = 8
    ntok = pos_ref.shape[0] // 2

    def zero_padding(e, _):
        first_blk = jnp.where(e == 0, 0, bend_ref[jnp.maximum(e - 1, 0)])

        def zero(r, _):
            o_ref[r] = 0
            return 0
        lax.fori_loop(first_blk * MOE_BLOCK + cnt_ref[e], bend_ref[e] * MOE_BLOCK, zero, 0)
        return 0
    lax.fori_loop(0, N_EXPERTS, zero_padding, 0)

    def fill(c, _):
        for u in range(unroll):
            t = c * unroll + u
            o_ref[pos_ref[2 * t]] = t
            o_ref[pos_ref[2 * t + 1]] = t
        return 0
    lax.fori_loop(0, ntok // unroll, fill, 0)


GATHER_SLOTS = 4


def _moe_body(bend_ref, cnt_ref, pos_ref, x_hbm, wg_ref, wu_ref, wd_ref, ys_hbm, *scratch, nblocks):
    xbufs = scratch[:GATHER_SLOTS]
    ybufs = scratch[GATHER_SLOTS:GATHER_SLOTS + 2]
    gsem, osem, wg_bf, wu_bf, wd_bf, row_tok_ref = scratch[GATHER_SLOTS + 2:]
    ahead = GATHER_SLOTS - 1
    e = pl.program_id(0)
    nused = bend_ref[N_EXPERTS - 1]
    b1 = bend_ref[e]
    b0 = jnp.where(e == 0, 0, bend_ref[jnp.maximum(e - 1, 0)])

    def gather_start(blk, s):
        base = blk * MOE_BLOCK
        for r in range(MOE_BLOCK):
            tok = row_tok_ref[base + r]
            pltpu.make_async_copy(x_hbm.at[pl.ds(tok, 1), :], xbufs[s].at[pl.ds(r, 1), :],
                                  gsem.at[s]).start(priority=r % 2)

    def gather_wait(s):
        pltpu.make_async_copy(x_hbm.at[pl.ds(0, MOE_BLOCK), :], xbufs[s], gsem.at[s]).wait()

    def out_copy(blk, s):
        row0 = pl.multiple_of(blk * MOE_BLOCK, MOE_BLOCK)
        return pltpu.make_async_copy(ybufs[s], ys_hbm.at[pl.ds(row0, MOE_BLOCK), :], osem.at[s])

    @pl.when(e == 0)
    def _():
        _build_rowmap(pos_ref, bend_ref, cnt_ref, row_tok_ref)
        for k in range(ahead):
            gather_start(k, k)

    @pl.when(b1 > b0)
    def _():
        wg_bf[...] = wg_ref[...].astype(BF16)
        wu_bf[...] = wu_ref[...].astype(BF16)
        wd_bf[...] = wd_ref[...].astype(BF16)

        def do_block(b, s):
            so = s % 2
            gather_wait(s)

            @pl.when(b >= 2)
            def _():
                out_copy(b - 2, so).wait()

            gather_start(jnp.minimum(b + ahead, nused - 1), (s + ahead) % GATHER_SLOTS)
            x = xbufs[s][...].astype(BF16)
            hg = jnp.dot(x, wg_bf[...], preferred_element_type=F32)
            hu = jnp.dot(x, wu_bf[...], preferred_element_type=F32)
            h = (jax.nn.silu(hg) * hu).astype(BF16)
            ybufs[so][...] = _pack_bf16_pairs(jnp.dot(h, wd_bf[...], preferred_element_type=F32))
            out_copy(b, so).start()

        def body(b, _):
            for s in range(GATHER_SLOTS):
                @pl.when(b % GATHER_SLOTS == s)
                def _():
                    do_block(b, s)
            return 0

        lax.fori_loop(b0, b1, body, 0)

    @pl.when(e == N_EXPERTS - 1)
    def _():
        last = nused - 1
        for s in range(GATHER_SLOTS):
            @pl.when(last % GATHER_SLOTS == s)
            def _():
                for k in range(1, GATHER_SLOTS):
                    gather_wait((s + k) % GATHER_SLOTS)
        out_copy(0, 0).wait()
        out_copy(0, 1).wait()
        yb0 = ybufs[0]
        yb0[...] = jnp.zeros(yb0.shape, U32)

        def zstart(b, _):
            out_copy(b, 0).start()
            return 0

        def zwait(b, _):
            out_copy(b, 0).wait()
            return 0

        lax.fori_loop(nused, nblocks, zstart, 0)
        lax.fori_loop(nused, nblocks, zwait, 0)


def _moe(bend, cnt, pos, x1, w_gate, w_up, w_down, layer, nblocks):
    assert 2 * x1.shape[0] > GATHER_SLOTS * MOE_BLOCK
    wspec = lambda a, b: pl.BlockSpec((None, None, a, b), lambda e, be, cn, ps: (layer, e, 0, 0))
    grid_spec = pltpu.PrefetchScalarGridSpec(
        num_scalar_prefetch=3,
        grid=(N_EXPERTS,),
        in_specs=[pl.BlockSpec(memory_space=pl.ANY), wspec(D_MODEL, EXPERT_FF), wspec(D_MODEL, EXPERT_FF),
                  wspec(EXPERT_FF, D_MODEL)],
        out_specs=pl.BlockSpec(memory_space=pl.ANY),
        scratch_shapes=([pltpu.VMEM((MOE_BLOCK, D_MODEL), F32) for _ in range(GATHER_SLOTS)]
                        + [pltpu.VMEM((MOE_BLOCK, PACKED_W), U32) for _ in range(2)]
                        + [pltpu.SemaphoreType.DMA((GATHER_SLOTS,)), pltpu.SemaphoreType.DMA((2,)),
                           pltpu.VMEM((D_MODEL, EXPERT_FF), BF16),
                           pltpu.VMEM((D_MODEL, EXPERT_FF), BF16),
                           pltpu.VMEM((EXPERT_FF, D_MODEL), BF16),
                           pltpu.SMEM((nblocks * MOE_BLOCK,), jnp.int32)]),
    )
    return pl.pallas_call(
        functools.partial(_moe_body, nblocks=nblocks),
        grid_spec=grid_spec,
        out_shape=jax.ShapeDtypeStruct((nblocks * MOE_BLOCK, PACKED_W), U32),
        compiler_params=_cparams(("arbitrary",), bounds_checks=False),
        name="moe_experts",
    )(bend, cnt, pos, x1, w_gate, w_up, w_down)


COMB_T = 128


def _combine_body(pos_ref, ys_hbm, x_ref, rt_ref, g_ref, b_ref, op_ref, os_ref, *scratch, n_prompt_blocks):
    i = pl.program_id(0)
    nsteps = pl.num_programs(0)
    bufs = tuple((scratch[2 * s], scratch[2 * s + 1]) for s in range(GATHER_SLOTS))
    sem = scratch[2 * GATHER_SLOTS]
    ahead = GATHER_SLOTS - 1

    def gather_start(step, s):
        base = step * (2 * COMB_T)
        for r in range(COMB_T):
            for k in range(2):
                p = pos_ref[base + 2 * r + k]
                pltpu.make_async_copy(ys_hbm.at[pl.ds(p, 1), :], bufs[s][k].at[pl.ds(r, 1), :],
                                      sem.at[s]).start(priority=k)

    def gather_wait(s):
        for k in range(2):
            pltpu.make_async_copy(ys_hbm.at[pl.ds(0, COMB_T), :], bufs[s][k], sem.at[s]).wait()

    @pl.when(i == 0)
    def _():
        for k in range(ahead):
            gather_start(k, k)

    def do_tile(s):
        gather_wait(s)
        gather_start(jnp.minimum(i + ahead, nsteps - 1), (s + ahead) % GATHER_SLOTS)
        rt = rt_ref[...]
        ff = (rt[:, R_W0:R_W0 + 1] * _unpack_bf16_pairs(bufs[s][0][...])
              + rt[:, R_W1:R_W1 + 1] * _unpack_bf16_pairs(bufs[s][1][...]))
        out = _layer_norm(DEEPNORM_ALPHA * x_ref[...] + ff, g_ref[...], b_ref[...])

        @pl.when(i < n_prompt_blocks)
        def _():
            op_ref[...] = out

        @pl.when(i >= n_prompt_blocks)
        def _():
            os_ref[...] = out

        @pl.when(i == nsteps - 1)
        def _():
            for k in range(1, GATHER_SLOTS):
                gather_wait((s + k) % GATHER_SLOTS)

    for s in range(GATHER_SLOTS):
        @pl.when(i % GATHER_SLOTS == s)
        def _():
            do_tile(s)


def _combine(pos, ys, x1, route, g, b, n_prompt_rows):
    rows = x1.shape[0]
    npb = n_prompt_rows // COMB_T
    assert rows // COMB_T >= GATHER_SLOTS
    grid_spec = pltpu.PrefetchScalarGridSpec(
        num_scalar_prefetch=1,
        grid=(rows // COMB_T,),
        in_specs=[pl.BlockSpec(memory_space=pl.ANY),
                  pl.BlockSpec((COMB_T, D_MODEL), lambda i, p: (i, 0)),
                  pl.BlockSpec((COMB_T, LANES), lambda i, p: (i, 0)),
                  pl.BlockSpec((1, D_MODEL), lambda i, p: (0, 0)),
                  pl.BlockSpec((1, D_MODEL), lambda i, p: (0, 0))],
        out_specs=[pl.BlockSpec((COMB_T, D_MODEL), lambda i, p: (jnp.minimum(i, npb - 1), 0)),
                   pl.BlockSpec((COMB_T, D_MODEL), lambda i, p: (jnp.maximum(i - npb, 0), 0))],
        scratch_shapes=([pltpu.VMEM((COMB_T, PACKED_W), U32) for _ in range(2 * GATHER_SLOTS)]
                        + [pltpu.SemaphoreType.DMA((GATHER_SLOTS,))]),
    )
    return pl.pallas_call(
        functools.partial(_combine_body, n_prompt_blocks=npb),
        grid_spec=grid_spec,
        out_shape=[jax.ShapeDtypeStruct((n_prompt_rows, D_MODEL), F32),
                   jax.ShapeDtypeStruct((rows - n_prompt_rows, D_MODEL), F32)],
        compiler_params=_cparams(("arbitrary",), bounds_checks=False),
        name="combine_ln2",
    )(pos, ys, x1, route, g, b)


def kernel(x_prompt, x_sample, cache_k, cache_v, state_ssm_re, state_ssm_im, w_in, ssm_lambda_re, ssm_lambda_im, ssm_log_dt, ssm_b_re, ssm_b_im, ssm_c_re, ssm_c_im, ssm_d, ssm_w_glu, attn_sinks, attn_norm_g, ssm_norm_g, w_out, ln1_g, ln1_b, router_group_w, router_group_b, router_expert_w, router_expert_b, expert_w_gate, expert_w_up, expert_w_down, ln2_g, ln2_b):
    nbatch, seq, _ = x_prompt.shape
    nseq, dec_seq, _ = x_sample.shape
    wbuf = cache_k.shape[2]
    rows_p = nbatch * seq
    rows_s = nseq * dec_seq
    rows_all = rows_p + rows_s
    tm_p = 512

    xp = x_prompt.reshape(rows_p, D_MODEL)
    xs = x_sample.reshape(rows_s, D_MODEL)
    tab_p = _rope_tables(np.arange(seq))
    tab_s = _rope_tables(PAST_LEN + (np.arange(rows_s) % dec_seq))
    ck_all = cache_k.astype(F32).reshape(DEPTH, nseq, wbuf, KV_WIDTH)
    cv_all = cache_v.astype(F32).reshape(DEPTH, nseq, wbuf, KV_WIDTH)

    sp = _ssm_pack(jax.vmap(_ssm_params)(ssm_lambda_re, ssm_lambda_im, ssm_log_dt, ssm_b_re, ssm_b_im,
                                         ssm_c_re, ssm_c_im, ssm_d))
    w_in_bf = w_in.astype(BF16)
    w_glu_bf = ssm_w_glu.astype(BF16)
    w_out_bf = w_out.astype(BF16)

    outs = {k: [] for k in ("kp", "vp", "hrp", "hip", "ks", "vs", "hrs", "his")}
    for l in range(DEPTH):
        sinks = attn_sinks[l].astype(F32)
        attn_g = attn_norm_g[l].astype(F32).reshape(1, ATTN_WIDTH)
        ssm_g = ssm_norm_g[l].astype(F32).reshape(1, SSM_WIDTH)
        g1 = ln1_g[l].astype(F32).reshape(1, D_MODEL)
        b1 = ln1_b[l].astype(F32).reshape(1, D_MODEL)
        g2 = ln2_g[l].astype(F32).reshape(1, D_MODEL)
        b2 = ln2_b[l].astype(F32).reshape(1, D_MODEL)
        n_rt = N_EXPERT_GROUPS + N_EXPERTS
        wr = jnp.concatenate([router_group_w[l].astype(F32)]
                             + [router_expert_w[l, g].astype(F32) for g in range(N_EXPERT_GROUPS)]
                             + [jnp.zeros((D_MODEL, LANES - n_rt), F32)], axis=1)
        br = jnp.concatenate([router_group_b[l].astype(F32), router_expert_b[l].astype(F32).reshape(-1),
                              jnp.zeros((LANES - n_rt,), F32)]).reshape(1, LANES)

        q_p, k_p, v_p, u_p = _inproj(xp, w_in_bf, l, *tab_p, tm=tm_p)
        q_s, k_s, v_s, u_s = _inproj(xs, w_in_bf, l, *tab_s, tm=rows_s)

        an_p = _attn_prompt(sinks, q_p, k_p, v_p, attn_g, seq)
        an_s, nk_s, nv_s = _attn_sample(sinks, q_s, k_s, v_s, ck_all, cv_all, attn_g, dec_seq, l)

        y_p, hl_p = _ssm_prompt(u_p, sp, l, nbatch, seq)
        y_s, hl_s = _ssm_sample(u_s, _pack_state(state_ssm_re[l], state_ssm_im[l]), sp, l, nseq, dec_seq)
        sn_p = _glu(y_p, w_glu_bf, l, ssm_g, tm_p)
        sn_s = _glu(y_s, w_glu_bf, l, ssm_g, rows_s)

        x1, lg_p = _outproj(an_p, sn_p, xp, w_out_bf, l, g1, b1, wr, br, tm_p, rows_all, 0)
        x1, lg_s = _outproj(an_s, sn_s, xs, w_out_bf, l, g1, b1, wr, br, rows_s, rows_all, rows_p // rows_s,
                            x1_buf=x1)
        nblocks = 2 * rows_all // MOE_BLOCK + N_EXPERTS
        route, meta = _route_plan(jnp.concatenate([lg_p, lg_s], axis=0))
        pos = route[:, R_P0:R_P1 + 1].astype(jnp.int32).reshape(2 * rows_all)
        bend = meta[0, :N_EXPERTS].astype(jnp.int32)

        cnt = meta[1, :N_EXPERTS].astype(jnp.int32)
        ys = _moe(bend, cnt, pos, x1, expert_w_gate, expert_w_up, expert_w_down, l, nblocks)
        xp, xs = _combine(pos, ys, x1, route, g2, b2, rows_p)

        wp = min(WINDOW, seq)
        tail = lambda a: a.reshape(nbatch, seq, KV_WIDTH)[:, seq - wp:].reshape(nbatch, wp, N_KV_HEADS, HEAD_DIM)
        outs["kp"].append(tail(k_p))
        outs["vp"].append(tail(v_p))
        hp = hl_p.reshape(nbatch, SSM_CHUNKS * 2 * CH_STATES)
        hr, hi = _unpack_state(hp, nbatch)
        outs["hrp"].append(hr)
        outs["hip"].append(hi)
        outs["ks"].append(nk_s.reshape(nseq, wbuf, N_KV_HEADS, HEAD_DIM))
        outs["vs"].append(nv_s.reshape(nseq, wbuf, N_KV_HEADS, HEAD_DIM))
        hr, hi = _unpack_state(hl_s, nseq)
        outs["hrs"].append(hr)
        outs["his"].append(hi)

    st = {k: jnp.stack(v) for k, v in outs.items()}
    return (xp.reshape(nbatch, seq, D_MODEL), xs.reshape(nseq, dec_seq, D_MODEL),
            st["kp"], st["vp"], st["hrp"], st["hip"], st["ks"], st["vs"], st["hrs"], st["his"])
```

```python
import functools

import numpy as np
import jax
import jax.numpy as jnp
from jax import lax
from jax.experimental import pallas as pl
from jax.experimental.pallas import tpu as pltpu

F32 = jnp.float32
BF16 = jnp.bfloat16

D_MODEL = 2048
DEPTH = 2
PAST_LEN = 16384
ATTN_WIDTH = 1024
SSM_WIDTH = 1024
HEAD_DIM = 64
N_HEADS = 16
N_KV_HEADS = 4
Q_PER_KV = 4
KV_WIDTH = 256
WINDOW = 128
ROPE_THETA = 10000.0
SSM_GROUP = 16
SSM_GROUPS = 64
SSM_STATE = 64
N_EXPERT_GROUPS = 4
EXPERTS_PER_GROUP = 8
N_EXPERTS = 32
EXPERT_FF = 512
MOE_BLOCK = 128
DEEPNORM_ALPHA = (2.0 * DEPTH) ** 0.25
LN_EPS = 1e-5
RMS_EPS = 1e-6

LANES = 128
SUBLANES = 8
VMEM_LIMIT = 56 * 1024 * 1024

SSM_CHUNKS = 4
CH_GROUPS = SSM_GROUPS // SSM_CHUNKS
CH_STATES = CH_GROUPS * SSM_STATE
CH_IN = CH_GROUPS * SSM_GROUP
SCAN_T = 512
SCAN_SEG = SCAN_T // SUBLANES


def _cparams(sem=None, vmem=VMEM_LIMIT, bounds_checks=True):
    return pltpu.CompilerParams(dimension_semantics=sem, vmem_limit_bytes=vmem,
                                disable_bounds_checks=not bounds_checks)


def _full(shape):
    n = len(shape)
    return pl.BlockSpec(shape, lambda *a: (0,) * n)


def _dot_nt(a, b):
    return lax.dot_general(a, b, (((1,), (1,)), ((), ())), preferred_element_type=F32)


def _inproj_body(x_ref, w_ref, cos_ref, sa_ref, sb_ref, q_ref, k_ref, v_ref, u_ref):
    x = x_ref[...].astype(BF16)
    cos = cos_ref[...]
    sa = sa_ref[...]
    sb = sb_ref[...]

    half = HEAD_DIM // 2

    def rope(c):
        return c * cos + pltpu.roll(c, LANES - half, 1) * sa + pltpu.roll(c, half, 1) * sb

    pq = jnp.dot(x, w_ref[:, 0:ATTN_WIDTH], preferred_element_type=F32)
    for j in range(ATTN_WIDTH // LANES):
        q_ref[:, j * LANES:(j + 1) * LANES] = rope(pq[:, j * LANES:(j + 1) * LANES]).astype(BF16)
    pk = jnp.dot(x, w_ref[:, ATTN_WIDTH:ATTN_WIDTH + KV_WIDTH], preferred_element_type=F32)
    for j in range(KV_WIDTH // LANES):
        k_ref[:, j * LANES:(j + 1) * LANES] = rope(pk[:, j * LANES:(j + 1) * LANES])
    v_ref[...] = jnp.dot(x, w_ref[:, ATTN_WIDTH + KV_WIDTH:ATTN_WIDTH + 2 * KV_WIDTH],
                         preferred_element_type=F32)
    u_ref[...] = jnp.dot(x, w_ref[:, ATTN_WIDTH + 2 * KV_WIDTH:], preferred_element_type=F32)


def _layer_spec(shape, layer):
    return pl.BlockSpec((None,) + tuple(shape), lambda *a: (layer,) + (0,) * len(shape))


def _inproj(x, w_bf, layer, cos_t, sa_t, sb_t, tm):
    rows = x.shape[0]
    tab_blocks = cos_t.shape[0] // tm
    in_w = w_bf.shape[2]
    row_spec = lambda w: pl.BlockSpec((tm, w), lambda i: (i, 0))
    tab_spec = pl.BlockSpec((tm, LANES), lambda i: (i % tab_blocks, 0))
    return pl.pallas_call(
        _inproj_body,
        grid=(rows // tm,),
        in_specs=[row_spec(D_MODEL), _layer_spec((D_MODEL, in_w), layer), tab_spec, tab_spec, tab_spec],
        out_specs=[row_spec(ATTN_WIDTH), row_spec(KV_WIDTH), row_spec(KV_WIDTH), row_spec(SSM_WIDTH)],
        out_shape=[jax.ShapeDtypeStruct((rows, ATTN_WIDTH), BF16),
                   jax.ShapeDtypeStruct((rows, KV_WIDTH), F32),
                   jax.ShapeDtypeStruct((rows, KV_WIDTH), F32),
                   jax.ShapeDtypeStruct((rows, SSM_WIDTH), F32)],
        compiler_params=_cparams(("arbitrary",)),
        name="inproj_rope",
    )(x, w_bf, cos_t, sa_t, sb_t)


def _rope_tables(positions):
    half = HEAD_DIM // 2
    inv_freq = ROPE_THETA ** (-np.arange(half, dtype=np.float64) / half)
    ang = positions.astype(np.float64)[:, None] * inv_freq[None, :]
    lane = np.arange(LANES)
    cos = np.cos(ang)[:, lane % half]
    sin = np.sin(ang)[:, lane % half]
    first = (lane % HEAD_DIM) < half
    sa = np.where(first[None, :], -sin, 0.0)
    sb = np.where(first[None, :], 0.0, sin)
    return (jnp.asarray(cos, F32), jnp.asarray(sa, F32), jnp.asarray(sb, F32))


def _rms(x, g):
    return x * lax.rsqrt(jnp.mean(jnp.square(x), axis=-1, keepdims=True) + RMS_EPS) * g


def _attn_prompt_body(sink_ref, q_ref, kp_ref, kc_ref, vp_ref, vc_ref, g_ref, o_ref, acc_ref, *, blocks_per_seq):
    i = pl.program_id(0)
    has_prev = (i % blocks_per_seq) > 0
    kk = jnp.concatenate([kp_ref[...], kc_ref[...]], axis=0).astype(BF16)
    vt = jnp.concatenate([vp_ref[...], vc_ref[...]], axis=0).T.astype(BF16)
    ncol = Q_PER_KV * WINDOW
    key = lax.broadcasted_iota(jnp.int32, (2 * WINDOW, ncol), 0)
    col = lax.broadcasted_iota(jnp.int32, (2 * WINDOW, ncol), 1)
    qry = col % WINDOW
    visible = (key > qry) & (key <= qry + WINDOW) & ((key >= WINDOW) | has_prev)
    head = lax.broadcasted_iota(jnp.int32, (1, ncol), 1) // WINDOW
    q = q_ref[...] * (HEAD_DIM ** -0.5)
    scores = []
    for g in range(N_KV_HEADS):
        qg = jnp.concatenate([q[:, (g * Q_PER_KV + j) * HEAD_DIM:(g * Q_PER_KV + j + 1) * HEAD_DIM]
                              for j in range(Q_PER_KV)], axis=0)
        kh = kk[:, g * HEAD_DIM:(g + 1) * HEAD_DIM]
        scores.append(lax.dot_general(kh, qg, (((1,), (1,)), ((), ())), preferred_element_type=F32))
    for g in range(N_KV_HEADS):
        st = jnp.where(visible, scores[g], -jnp.inf)
        sink = jnp.zeros((1, ncol), F32)
        for j in range(Q_PER_KV):
            sink = jnp.where(head == j, sink_ref[g * Q_PER_KV + j], sink)
        m = jnp.maximum(jnp.max(st, axis=0, keepdims=True), sink)
        p = jnp.exp(st - m)
        denom = jnp.sum(p, axis=0, keepdims=True) + jnp.exp(sink - m)
        ot = jnp.dot(vt[g * HEAD_DIM:(g + 1) * HEAD_DIM, :], p.astype(BF16), preferred_element_type=F32)
        ot = ot / denom
        for j in range(Q_PER_KV):
            h = g * Q_PER_KV + j
            acc_ref[h * HEAD_DIM:(h + 1) * HEAD_DIM, :] = ot[:, j * WINDOW:(j + 1) * WINDOW]
    o_ref[...] = _rms(acc_ref[...].T, g_ref[...]).astype(BF16)


def _attn_prompt(sinks, q, k, v, g, seq):
    rows = q.shape[0]
    nblk = rows // WINDOW
    bps = seq // WINDOW
    cur = lambda w: pl.BlockSpec((WINDOW, w), lambda i: (i, 0))
    prev = lambda w: pl.BlockSpec((WINDOW, w), lambda i: (jnp.maximum(i - 1, 0), 0))
    return pl.pallas_call(
        functools.partial(_attn_prompt_body, blocks_per_seq=bps),
        grid=(nblk,),
        in_specs=[pl.BlockSpec(memory_space=pltpu.SMEM), cur(ATTN_WIDTH), prev(KV_WIDTH), cur(KV_WIDTH),
                  prev(KV_WIDTH), cur(KV_WIDTH), _full((1, ATTN_WIDTH))],
        out_specs=cur(ATTN_WIDTH),
        out_shape=jax.ShapeDtypeStruct((rows, ATTN_WIDTH), BF16),
        scratch_shapes=[pltpu.VMEM((ATTN_WIDTH, WINDOW), F32)],
        compiler_params=_cparams(("arbitrary",)),
        name="attn_prompt",
    )(sinks, q, k, k, v, v, g)


SEQ_PER_STEP = 8


def _attn_sample_body(sink_ref, q_ref, kn_ref, vn_ref, ck_ref, cv_ref, g_ref, o_ref, nk_ref, nv_ref, acc_ref,
                      *, dec_seq):
    nq = SEQ_PER_STEP * dec_seq
    wb = ck_ref.shape[1]
    nkc = SEQ_PER_STEP * wb
    ck = ck_ref[...].reshape(nkc, KV_WIDTH)
    cv = cv_ref[...].reshape(nkc, KV_WIDTH)
    kn = kn_ref[...]
    vn = vn_ref[...]
    ckb = ck.astype(BF16)
    cvb = cv.astype(BF16)
    knb = kn.astype(BF16)
    vnb = vn.astype(BF16)
    q = q_ref[...]

    rows = Q_PER_KV * nq
    r = lax.broadcasted_iota(jnp.int32, (rows, nkc), 0) % nq
    c = lax.broadcasted_iota(jnp.int32, (rows, nkc), 1)
    vis_c = ((r // dec_seq) == (c // wb)) & ((c % wb) >= (r % dec_seq) + 1 + (wb - WINDOW))
    rn = lax.broadcasted_iota(jnp.int32, (rows, nq), 0) % nq
    cn = lax.broadcasted_iota(jnp.int32, (rows, nq), 1)
    vis_n = ((rn // dec_seq) == (cn // dec_seq)) & ((cn % dec_seq) <= (rn % dec_seq))
    hrow = lax.broadcasted_iota(jnp.int32, (rows, 1), 0) // nq

    dn = (((1,), (1,)), ((), ()))
    for g in range(N_KV_HEADS):
        qs = jnp.concatenate([q[:, (g * Q_PER_KV + j) * HEAD_DIM:(g * Q_PER_KV + j + 1) * HEAD_DIM]
                              for j in range(Q_PER_KV)], axis=0)
        sl = slice(g * HEAD_DIM, (g + 1) * HEAD_DIM)
        sc = lax.dot_general(qs, ckb[:, sl], dn, preferred_element_type=F32) * (HEAD_DIM ** -0.5)
        sn = lax.dot_general(qs, knb[:, sl], dn, preferred_element_type=F32) * (HEAD_DIM ** -0.5)
        sc = jnp.where(vis_c, sc, -jnp.inf)
        sn = jnp.where(vis_n, sn, -jnp.inf)
        sink = jnp.zeros((rows, 1), F32)
        for j in range(Q_PER_KV):
            sink = jnp.where(hrow == j, sink_ref[g * Q_PER_KV + j], sink)
        m = jnp.maximum(jnp.maximum(jnp.max(sc, axis=-1, keepdims=True), jnp.max(sn, axis=-1, keepdims=True)), sink)
        pc = jnp.exp(sc - m)
        pn = jnp.exp(sn - m)
        denom = jnp.sum(pc, axis=-1, keepdims=True) + jnp.sum(pn, axis=-1, keepdims=True) + jnp.exp(sink - m)
        o = (jnp.dot(pc.astype(BF16), cvb[:, sl], preferred_element_type=F32)
             + jnp.dot(pn.astype(BF16), vnb[:, sl], preferred_element_type=F32)) / denom
        for j in range(Q_PER_KV):
            h = g * Q_PER_KV + j
            acc_ref[:, h * HEAD_DIM:(h + 1) * HEAD_DIM] = o[j * nq:(j + 1) * nq, :]
    o_ref[...] = _rms(acc_ref[...], g_ref[...]).astype(BF16)

    keep = wb - dec_seq
    for s in range(SEQ_PER_STEP):
        nk_ref[s, 0:keep, :] = ck_ref[s, dec_seq:wb, :]
        nv_ref[s, 0:keep, :] = cv_ref[s, dec_seq:wb, :]
        nk_ref[s, keep:wb, :] = kn[s * dec_seq:(s + 1) * dec_seq, :]
        nv_ref[s, keep:wb, :] = vn[s * dec_seq:(s + 1) * dec_seq, :]


def _attn_sample(sinks, q, kn, vn, ck, cv, g, dec_seq, layer):
    rows = q.shape[0]
    _, nseq, wb, _ = ck.shape
    nq = SEQ_PER_STEP * dec_seq
    rowb = lambda w: pl.BlockSpec((nq, w), lambda i: (i, 0))
    cin = pl.BlockSpec((None, SEQ_PER_STEP, wb, KV_WIDTH), lambda i: (layer, i, 0, 0))
    cb = pl.BlockSpec((SEQ_PER_STEP, wb, KV_WIDTH), lambda i: (i, 0, 0))
    return pl.pallas_call(
        functools.partial(_attn_sample_body, dec_seq=dec_seq),
        grid=(nseq // SEQ_PER_STEP,),
        in_specs=[pl.BlockSpec(memory_space=pltpu.SMEM), rowb(ATTN_WIDTH), rowb(KV_WIDTH), rowb(KV_WIDTH),
                  cin, cin, _full((1, ATTN_WIDTH))],
        out_specs=[rowb(ATTN_WIDTH), cb, cb],
        out_shape=[jax.ShapeDtypeStruct((rows, ATTN_WIDTH), BF16),
                   jax.ShapeDtypeStruct(ck.shape[1:], F32), jax.ShapeDtypeStruct(cv.shape[1:], F32)],
        scratch_shapes=[pltpu.VMEM((nq, ATTN_WIDTH), F32)],
        compiler_params=_cparams(("arbitrary",)),
        name="attn_sample",
    )(sinks, q, kn, vn, ck, cv, g)


def _ssm_prompt_body(ua_ref, uc_ref, b_ref, c_ref, lr_ref, li_ref, pr_ref, pi_ref, d_ref, y_ref, hl_ref,
                     uperm_ref, bu0, bu1, hb0, hb1, yp_ref, carry_ref, *, nchunks, ntc):
    k = pl.program_id(0)
    bus = (bu0, bu1)
    hbs = (hb0, hb1)
    lanes_per_chunk = CH_IN // LANES
    seg = lambda s: slice(s * SCAN_SEG, (s + 1) * SCAN_SEG)
    lane = lambda c: slice(c * LANES, (c + 1) * LANES)

    def stage_a(item, slot):
        cc = item % SSM_CHUNKS
        for c in range(lanes_per_chunk):
            for s in range(SUBLANES):
                uperm_ref[c, pl.ds(s, SCAN_SEG, stride=SUBLANES), :] = ua_ref[seg(s), lane(c)]
        up = jnp.concatenate([uperm_ref[c] for c in range(lanes_per_chunk)], axis=1)
        bus[slot][...] = _dot_nt(up.astype(BF16), b_ref[cc])

    def stage_b(item, slot):
        cc = item % SSM_CHUNKS
        tc = (item // SSM_CHUNKS) % ntc
        bu_ref = bus[slot]
        hb_ref = hbs[slot]
        ar = jnp.broadcast_to(lr_ref[cc], (SUBLANES, CH_STATES))
        ai = jnp.broadcast_to(li_ref[cc], (SUBLANES, CH_STATES))

        def advance(h, j):
            hr, hi = h
            br = bu_ref[j * SUBLANES:(j + 1) * SUBLANES, 0:CH_STATES]
            bi = bu_ref[j * SUBLANES:(j + 1) * SUBLANES, CH_STATES:2 * CH_STATES]
            return ar * hr + (br - ai * hi), ar * hi + (bi + ai * hr)

        zero = jnp.zeros((SUBLANES, CH_STATES), F32)
        h = (zero, zero)
        for j in range(SCAN_SEG):
            h = advance(h, j)
        fr, fi = h

        cst = jnp.where(tc == 0, 0.0, carry_ref[cc])
        c_r = cst[0:1, :]
        c_i = cst[1:2, :]
        lsr = pr_ref[cc]
        lsi = pi_ref[cc]
        rows_r, rows_i = [], []
        for s in range(SUBLANES):
            rows_r.append(c_r)
            rows_i.append(c_i)
            n_r = fr[s:s + 1, :] + (lsr * c_r - lsi * c_i)
            n_i = fi[s:s + 1, :] + (lsr * c_i + lsi * c_r)
            c_r, c_i = n_r, n_i
        carry_ref[cc] = jnp.concatenate([c_r, c_i], axis=0)
        hl_ref[0, cc] = jnp.concatenate([c_r, c_i], axis=0)

        h = (jnp.concatenate(rows_r, axis=0), jnp.concatenate(rows_i, axis=0))
        for jj in range(SCAN_SEG // 2):
            h1 = advance(h, 2 * jj)
            h = advance(h1, 2 * jj + 1)
            rows = slice(2 * jj * SUBLANES, (2 * jj + 2) * SUBLANES)
            hb_ref[rows, 0:CH_STATES] = jnp.concatenate([h1[0], h[0]], axis=0).astype(BF16)
            hb_ref[rows, CH_STATES:2 * CH_STATES] = jnp.concatenate([h1[1], h[1]], axis=0).astype(BF16)

    def stage_c(item, slot):
        cc = item % SSM_CHUNKS
        yp = _dot_nt(hbs[slot][...], c_ref[cc])
        d = d_ref[cc]
        for c in range(lanes_per_chunk):
            yp_ref[c] = yp[:, lane(c)]
        for s in range(SUBLANES):
            for c in range(lanes_per_chunk):
                y_ref[seg(s), lane(c)] = (yp_ref[c, pl.ds(s, SCAN_SEG, stride=SUBLANES), :]
                                          + d[:, lane(c)] * uc_ref[seg(s), lane(c)])

    def run(do_a, do_b, do_c, parity):
        if do_a:
            stage_a(k, parity)
        if do_b:
            stage_b(k - 1, 1 - parity)
        if do_c:
            stage_c(k - 2, parity)

    @pl.when(k == 0)
    def _():
        carry_ref[...] = jnp.zeros(carry_ref.shape, F32)
        run(True, False, False, 0)

    @pl.when(k == 1)
    def _():
        run(True, True, False, 1)

    for parity in range(2):
        @pl.when((k >= 2) & (k < nchunks) & (k % 2 == parity))
        def _():
            run(True, True, True, parity)

    @pl.when(k == nchunks)
    def _():
        run(False, True, True, nchunks % 2)

    @pl.when(k == nchunks + 1)
    def _():
        run(False, False, True, (nchunks + 1) % 2)


def _ssm_prompt(u, sp, layer, nbatch, seq):
    rows = u.shape[0]
    ntc = seq // SCAN_T
    nchunks = nbatch * ntc * SSM_CHUNKS
    assert nchunks >= 2
    item_a = lambda k: jnp.minimum(k, nchunks - 1)
    item_b = lambda k: jnp.clip(k - 1, 0, nchunks - 1)
    item_c = lambda k: jnp.clip(k - 2, 0, nchunks - 1)
    chunk_block = lambda item: (item // SSM_CHUNKS, item % SSM_CHUNKS)
    y, hl = pl.pallas_call(
        functools.partial(_ssm_prompt_body, nchunks=nchunks, ntc=ntc),
        grid=(nchunks + 2,),
        in_specs=[pl.BlockSpec((SCAN_T, CH_IN), lambda k: chunk_block(item_a(k))),
                  pl.BlockSpec((SCAN_T, CH_IN), lambda k: chunk_block(item_c(k))),
                  ] + [_layer_spec(sp[n].shape[1:], layer) for n in ("B", "C", "lr", "li", "pr", "pi", "d")],
        out_specs=[pl.BlockSpec((SCAN_T, CH_IN), lambda k: chunk_block(item_c(k))),
                   pl.BlockSpec((1, SSM_CHUNKS, 2, CH_STATES),
                                lambda k: (item_b(k) // (ntc * SSM_CHUNKS), 0, 0, 0))],
        out_shape=[jax.ShapeDtypeStruct((rows, SSM_WIDTH), F32),
                   jax.ShapeDtypeStruct((nbatch, SSM_CHUNKS, 2, CH_STATES), F32)],
        scratch_shapes=[pltpu.VMEM((CH_IN // LANES, SCAN_T, LANES), F32),
                        pltpu.VMEM((SCAN_T, 2 * CH_STATES), F32), pltpu.VMEM((SCAN_T, 2 * CH_STATES), F32),
                        pltpu.VMEM((SCAN_T, 2 * CH_STATES), BF16), pltpu.VMEM((SCAN_T, 2 * CH_STATES), BF16),
                        pltpu.VMEM((CH_IN // LANES, SCAN_T, LANES), F32),
                        pltpu.VMEM((SSM_CHUNKS, 2, CH_STATES), F32)],
        compiler_params=_cparams(("arbitrary",)),
        name="ssm_prompt",
    )(u, u, sp["B"], sp["C"], sp["lr"], sp["li"], sp["pr"], sp["pi"], sp["d"])
    return y, hl


def _ssm_sample_body(u_ref, h0_ref, b_ref, c_ref, lr_ref, li_ref, d_ref, y_ref, ho_ref, bu_ref, *, nseq, dec_seq):
    nl = CH_STATES // LANES
    for k in range(SSM_CHUNKS):
        uk = u_ref[:, k * CH_IN:(k + 1) * CH_IN]
        bu = _dot_nt(uk.astype(BF16), b_ref[k])
        for c in range(2 * nl):
            bu_ref[c] = bu[:, c * LANES:(c + 1) * LANES]
        base = k * 2 * CH_STATES
        hr = h0_ref[:, base:base + CH_STATES]
        hi = h0_ref[:, base + CH_STATES:base + 2 * CH_STATES]
        ar = lr_ref[k]
        ai = li_ref[k]
        for t in range(dec_seq):
            br = jnp.concatenate([bu_ref[c, pl.ds(t, nseq, stride=dec_seq), :] for c in range(nl)], axis=1)
            bi = jnp.concatenate([bu_ref[nl + c, pl.ds(t, nseq, stride=dec_seq), :] for c in range(nl)], axis=1)
            hr, hi = ar * hr + (br - ai * hi), ar * hi + (bi + ai * hr)
            for c in range(nl):
                bu_ref[c, pl.ds(t, nseq, stride=dec_seq), :] = hr[:, c * LANES:(c + 1) * LANES]
                bu_ref[nl + c, pl.ds(t, nseq, stride=dec_seq), :] = hi[:, c * LANES:(c + 1) * LANES]
        ho_ref[:, base:base + CH_STATES] = hr
        ho_ref[:, base + CH_STATES:base + 2 * CH_STATES] = hi
        hall = jnp.concatenate([bu_ref[c] for c in range(2 * nl)], axis=1)
        y_ref[:, k * CH_IN:(k + 1) * CH_IN] = (
            _dot_nt(hall.astype(BF16), c_ref[k]) + d_ref[k] * uk)


def _ssm_sample(u, h0, sp, layer, nseq, dec_seq):
    rows = u.shape[0]
    return pl.pallas_call(
        functools.partial(_ssm_sample_body, nseq=nseq, dec_seq=dec_seq),
        grid=(1,),
        in_specs=[_full(u.shape), _full(h0.shape)] + [_layer_spec(sp[n].shape[1:], layer)
                                                      for n in ("B", "C", "lr", "li", "d")],
        out_specs=[_full((rows, SSM_WIDTH)), _full(h0.shape)],
        out_shape=[jax.ShapeDtypeStruct((rows, SSM_WIDTH), F32), jax.ShapeDtypeStruct(h0.shape, F32)],
        scratch_shapes=[pltpu.VMEM((2 * CH_STATES // LANES, rows, LANES), F32)],
        compiler_params=_cparams(("arbitrary",)),
        name="ssm_sample",
    )(u, h0, sp["B"], sp["C"], sp["lr"], sp["li"], sp["d"])


def _ssm_params(lambda_re, lambda_im, log_dt, b_re, b_im, c_re, c_im, d):
    lam = lax.complex(jnp.minimum(lambda_re.astype(F32), -1e-4), lambda_im.astype(F32))
    dt = jnp.exp(log_dt.astype(F32))[:, None]
    lam_dt = lam * dt
    lam_bar = jnp.exp(lam_dt)
    b = lax.complex(b_re.astype(F32), b_im.astype(F32))
    b_bar = ((lam_bar - 1.0) / lam)[..., None] * b
    rows_b = lambda part: part.reshape(SSM_CHUNKS, CH_STATES, SSM_GROUP)
    rows_c = lambda part: part.reshape(SSM_CHUNKS, CH_IN, SSM_STATE)
    pw = jnp.exp(lam_dt * float(SCAN_SEG)).reshape(SSM_CHUNKS, 1, CH_STATES)
    lb = lam_bar.reshape(SSM_CHUNKS, 1, CH_STATES)
    return dict(b_re=rows_b(jnp.real(b_bar)), b_im=rows_b(jnp.imag(b_bar)),
                c_re=rows_c(c_re.astype(F32)), c_im_neg=rows_c(-c_im.astype(F32)),
                lr=jnp.real(lb), li=jnp.imag(lb), pr=jnp.real(pw), pi=jnp.imag(pw),
                d=d.astype(F32).reshape(SSM_CHUNKS, 1, CH_IN))


def _ssm_pack_body(bre_ref, bim_ref, cre_ref, cimn_ref, eb_ref, ec_ref, b_out, c_out):
    def expand(rows_ref, onehot_ref, row_group, col_group):
        t = jnp.dot(rows_ref[...].astype(BF16), onehot_ref[...], preferred_element_type=F32)
        own = lax.broadcasted_iota(jnp.int32, t.shape, 0) // row_group
        col = lax.broadcasted_iota(jnp.int32, t.shape, 1) // col_group
        return jnp.where(own == col, t, 0.0).astype(BF16)

    b_out[0:CH_STATES, :] = expand(bre_ref, eb_ref, SSM_STATE, SSM_GROUP)
    b_out[CH_STATES:2 * CH_STATES, :] = expand(bim_ref, eb_ref, SSM_STATE, SSM_GROUP)
    c_out[:, 0:CH_STATES] = expand(cre_ref, ec_ref, SSM_GROUP, SSM_STATE)
    c_out[:, CH_STATES:2 * CH_STATES] = expand(cimn_ref, ec_ref, SSM_GROUP, SSM_STATE)


def _ssm_pack(sp):
    depth = sp["b_re"].shape[0]
    eb = jnp.asarray(np.tile(np.eye(SSM_GROUP, dtype=np.float32), (1, CH_GROUPS)), BF16)
    ec = jnp.asarray(np.tile(np.eye(SSM_STATE, dtype=np.float32), (1, CH_GROUPS)), BF16)
    item = lambda r, c: pl.BlockSpec((None, None, r, c), lambda l, k: (l, k, 0, 0))
    bmat, cmat = pl.pallas_call(
        _ssm_pack_body,
        grid=(depth, SSM_CHUNKS),
        in_specs=[item(CH_STATES, SSM_GROUP), item(CH_STATES, SSM_GROUP), item(CH_IN, SSM_STATE),
                  item(CH_IN, SSM_STATE), _full(eb.shape), _full(ec.shape)],
        out_specs=[item(2 * CH_STATES, CH_IN), item(CH_IN, 2 * CH_STATES)],
        out_shape=[jax.ShapeDtypeStruct((depth, SSM_CHUNKS, 2 * CH_STATES, CH_IN), BF16),
                   jax.ShapeDtypeStruct((depth, SSM_CHUNKS, CH_IN, 2 * CH_STATES), BF16)],
        compiler_params=_cparams(("arbitrary", "arbitrary")),
        name="ssm_pack",
    )(sp["b_re"], sp["b_im"], sp["c_re"], sp["c_im_neg"], eb, ec)
    return dict(sp, B=bmat, C=cmat)


def _pack_state(re, im):
    n = re.shape[0]
    st = jnp.stack([re.reshape(n, SSM_CHUNKS, CH_STATES), im.reshape(n, SSM_CHUNKS, CH_STATES)], axis=2)
    return st.reshape(n, SSM_CHUNKS * 2 * CH_STATES).astype(F32)


def _unpack_state(h, n):
    st = h.reshape(n, SSM_CHUNKS, 2, CH_GROUPS, SSM_STATE)
    return (st[:, :, 0].reshape(n, SSM_GROUPS, SSM_STATE), st[:, :, 1].reshape(n, SSM_GROUPS, SSM_STATE))


def _glu_body(y_ref, w_ref, g_ref, o_ref):
    a = jax.nn.gelu(y_ref[...])
    z = a * jax.nn.sigmoid(jnp.dot(a.astype(BF16), w_ref[...], preferred_element_type=F32))
    o_ref[...] = _rms(z, g_ref[...]).astype(BF16)


def _glu(y, w_bf, layer, g, tm):
    rows = y.shape[0]
    spec = pl.BlockSpec((tm, SSM_WIDTH), lambda i: (i, 0))
    return pl.pallas_call(
        _glu_body,
        grid=(rows // tm,),
        in_specs=[spec, _layer_spec((SSM_WIDTH, SSM_WIDTH), layer), _full((1, SSM_WIDTH))],
        out_specs=spec,
        out_shape=jax.ShapeDtypeStruct((rows, SSM_WIDTH), BF16),
        compiler_params=_cparams(("arbitrary",)),
        name="glu_rms",
    )(y, w_bf, g)


def _layer_norm(y, g, b):
    mu = jnp.mean(y, axis=-1, keepdims=True)
    yc = y - mu
    var = jnp.mean(jnp.square(yc), axis=-1, keepdims=True)
    return yc * lax.rsqrt(var + LN_EPS) * g + b


OUTPROJ_SUB = 256


def _outproj_body(*refs, aliased, nblk):
    if aliased:
        refs = refs[1:]
    a_ref, s_ref, x_ref, w_ref, g_ref, b_ref, wrc_ref, br_ref, o_ref, lg_ref = refs
    i = pl.program_id(0)

    @pl.when(i < nblk)
    def _():
        tm = a_ref.shape[0]
        sub = min(tm, OUTPROJ_SUB)
        for r0 in range(0, tm, sub):
            rows = slice(r0, r0 + sub)
            acc = jnp.dot(a_ref[rows, :], w_ref[0:ATTN_WIDTH, :], preferred_element_type=F32)
            acc = acc + jnp.dot(s_ref[rows, :], w_ref[ATTN_WIDTH:, :], preferred_element_type=F32)
            x1 = _layer_norm(DEEPNORM_ALPHA * x_ref[rows, :] + acc, g_ref[...], b_ref[...])
            o_ref[rows, :] = x1
            hh = jnp.dot(x1.astype(BF16), wrc_ref[...], preferred_element_type=F32)
            lg_ref[rows, :] = hh[:, 0:LANES] + hh[:, LANES:] + br_ref[...]

    @pl.when(i >= nblk)
    def _():
        o_ref[...] = jnp.zeros(o_ref.shape, F32)


def _outproj(an, sn, x, w_bf, layer, g, b, wr, br, tm, total_rows, row_block0, x1_buf=None):
    rows = an.shape[0]
    aliased = x1_buf is not None
    nblk = rows // tm
    tail = 0 if aliased or rows == total_rows else 1
    blk = lambda i: jnp.minimum(i, nblk - 1)
    half = lambda: pl.BlockSpec((tm, ATTN_WIDTH), lambda i: (blk(i), 0))
    in_specs = [half(), half(), pl.BlockSpec((tm, D_MODEL), lambda i: (blk(i), 0)),
                _layer_spec((D_MODEL, D_MODEL), layer), _full((1, D_MODEL)), _full((1, D_MODEL)),
                _full((D_MODEL, 2 * LANES)), _full((1, LANES))]
    wr_hi = wr.astype(BF16)
    wr_lo = (wr - wr_hi.astype(F32)).astype(BF16)
    args = [an, sn, x, w_bf, g, b, jnp.concatenate([wr_hi, wr_lo], axis=1), br]
    if aliased:
        in_specs = [pl.BlockSpec(memory_space=pl.ANY)] + in_specs
        args = [x1_buf] + args
    return pl.pallas_call(
        functools.partial(_outproj_body, aliased=aliased, nblk=nblk),
        grid=(nblk + tail,),
        in_specs=in_specs,
        out_specs=[pl.BlockSpec((tm, D_MODEL), lambda i: (i + row_block0, 0)),
                   pl.BlockSpec((tm, LANES), lambda i: (blk(i), 0))],
        out_shape=[jax.ShapeDtypeStruct((total_rows, D_MODEL), F32), jax.ShapeDtypeStruct((rows, LANES), F32)],
        input_output_aliases={0: 0} if aliased else {},
        compiler_params=_cparams(("arbitrary",)),
        name="outproj_ln1",
    )(*args)


ROUTE_T = 640
R_E0, R_E1, R_W0, R_W1, R_P0, R_P1 = range(6)


def _route_select(x):
    lane = lax.broadcasted_iota(jnp.int32, x.shape, 1)
    big = jnp.int32(1 << 20)
    neg = -jnp.inf
    gmask = lane < N_EXPERT_GROUPS
    gm = jnp.max(jnp.where(gmask, x, neg), axis=-1, keepdims=True)
    grp = jnp.min(jnp.where(gmask & (x == gm), lane, big), axis=-1, keepdims=True)
    p_grp = 1.0 / jnp.sum(jnp.where(gmask, jnp.exp(x - gm), 0.0), axis=-1, keepdims=True)
    lo = N_EXPERT_GROUPS + grp * EXPERTS_PER_GROUP
    emask = (lane >= lo) & (lane < lo + EXPERTS_PER_GROUP)
    v1 = jnp.max(jnp.where(emask, x, neg), axis=-1, keepdims=True)
    i1 = jnp.min(jnp.where(emask & (x == v1), lane, big), axis=-1, keepdims=True)
    emask2 = emask & (lane != i1)
    v2 = jnp.max(jnp.where(emask2, x, neg), axis=-1, keepdims=True)
    i2 = jnp.min(jnp.where(emask2 & (x == v2), lane, big), axis=-1, keepdims=True)
    e2 = jnp.exp(v2 - v1)
    w1 = p_grp / (1.0 + e2)
    w2 = p_grp * e2 / (1.0 + e2)
    return (i1 - N_EXPERT_GROUPS).astype(F32), (i2 - N_EXPERT_GROUPS).astype(F32), w1, w2


def _route_plan_body(lg_ref, ltri_ref, utri_ref, o_ref, meta_ref, rec_ref, run_ref):
    ph = pl.program_id(0)
    j = pl.program_id(1)
    tm = lg_ref.shape[0]
    lane = lax.broadcasted_iota(jnp.int32, (tm, LANES), 1)
    lanef = lane.astype(F32)
    r0 = pl.multiple_of(j * tm, tm)

    @pl.when((ph == 0) & (j == 0))
    def _():
        run_ref[...] = jnp.zeros(run_ref.shape, F32)

    @pl.when(ph == 0)
    def _():
        e0, e1, w0, w1 = _route_select(lg_ref[...])
        oh0 = (lanef == e0).astype(F32)
        oh1 = (lanef == e1).astype(F32)
        oh2 = oh0 + oh1
        before = jnp.dot(ltri_ref[...], oh2.astype(BF16), preferred_element_type=F32) + run_ref[0:1, :]
        rk0 = jnp.sum(before * oh0, axis=-1, keepdims=True)
        rk1 = jnp.sum(before * oh1, axis=-1, keepdims=True)
        run_ref[...] = run_ref[...] + jnp.sum(oh2, axis=0, keepdims=True)
        rec = jnp.where(lane == R_E0, e0, 0.0)
        rec = jnp.where(lane == R_E1, e1, rec)
        rec = jnp.where(lane == R_W0, w0, rec)
        rec = jnp.where(lane == R_W1, w1, rec)
        rec = jnp.where(lane == R_P0, rk0, rec)
        rec = jnp.where(lane == R_P1, rk1, rec)
        rec_ref[pl.ds(r0, tm), :] = rec

    @pl.when(ph == 1)
    def _():
        counts = run_ref[...]
        nblk = jnp.floor((counts + (MOE_BLOCK - 1)) * (1.0 / MOE_BLOCK))
        end_blk = jnp.dot(nblk.astype(BF16), utri_ref[...], preferred_element_type=F32)
        start_row = (end_blk - nblk) * MOE_BLOCK
        rec = rec_ref[pl.ds(r0, tm), :]
        oh0 = (lanef == rec[:, R_E0:R_E0 + 1]).astype(F32)
        oh1 = (lanef == rec[:, R_E1:R_E1 + 1]).astype(F32)
        p0 = jnp.sum(oh0 * start_row[0:1, :], axis=-1, keepdims=True) + rec[:, R_P0:R_P0 + 1]
        p1 = jnp.sum(oh1 * start_row[0:1, :], axis=-1, keepdims=True) + rec[:, R_P1:R_P1 + 1]
        out = jnp.where(lane == R_P0, p0, rec)
        out = jnp.where(lane == R_P1, p1, out)
        o_ref[...] = out

        @pl.when(j == 0)
        def _():
            mrow = lax.broadcasted_iota(jnp.int32, meta_ref.shape, 0)
            meta_ref[...] = jnp.where(mrow == 1, counts, end_blk)


def _route_plan(logits):
    rows = logits.shape[0]
    tm = ROUTE_T
    ltri = jnp.asarray(np.tril(np.ones((tm, tm), np.float32), -1), BF16)
    utri = jnp.asarray(np.triu(np.ones((LANES, LANES), np.float32)), BF16)
    mrows = SUBLANES
    return pl.pallas_call(
        _route_plan_body,
        grid=(2, rows // tm),
        in_specs=[pl.BlockSpec((tm, LANES), lambda p, j: (j * (1 - p), 0)),
                  pl.BlockSpec((tm, tm), lambda p, j: (0, 0)),
                  pl.BlockSpec((LANES, LANES), lambda p, j: (0, 0))],
        out_specs=[pl.BlockSpec((tm, LANES), lambda p, j: (j * p, 0)),
                   pl.BlockSpec((mrows, LANES), lambda p, j: (0, 0))],
        out_shape=[jax.ShapeDtypeStruct((rows, LANES), F32), jax.ShapeDtypeStruct((mrows, LANES), F32)],
        scratch_shapes=[pltpu.VMEM((rows, LANES), F32), pltpu.VMEM((SUBLANES, LANES), F32)],
        compiler_params=_cparams(("arbitrary", "arbitrary")),
        name="route_plan",
    )(logits, ltri, utri)


U32 = jnp.uint32
PACKED_W = D_MODEL // 2


def _pack_bf16_pairs(y):
    bits = lax.bitcast_convert_type(y.astype(BF16).astype(F32), U32)
    return bits[:, 0:PACKED_W] | (bits[:, PACKED_W:] >> 16)


def _unpack_bf16_pairs(p):
    hi = lax.bitcast_convert_type(p & jnp.uint32(0xFFFF0000), F32)
    lo = lax.bitcast_convert_type(p << 16, F32)
    return jnp.concatenate([hi, lo], axis=1)


ROWMAP_SHIFT = 16
DUMP_ROWS = 2 * MOE_BLOCK


def _build_rowmap(pos_ref, bend_ref, cnt_ref, o_ref):
    unroll = 8
    ntok = pos_ref.shape[0] // 2

    def mark_padding(e, _):
        first_blk = jnp.where(e == 0, 0, bend_ref[jnp.maximum(e - 1, 0)])

        def mark(r, _):
            o_ref[r] = (2 << ROWMAP_SHIFT) | (r % DUMP_ROWS)
            return 0
        lax.fori_loop(first_blk * MOE_BLOCK + cnt_ref[e], bend_ref[e] * MOE_BLOCK, mark, 0)
        return 0
    lax.fori_loop(0, N_EXPERTS, mark_padding, 0)

    def fill(c, _):
        for u in range(unroll):
            t = c * unroll + u
            o_ref[pos_ref[2 * t]] = t
            o_ref[pos_ref[2 * t + 1]] = (1 << ROWMAP_SHIFT) | t
        return 0
    lax.fori_loop(0, ntok // unroll, fill, 0)


GATHER_SLOTS = 4


def _moe_body(bend_ref, cnt_ref, pos_ref, x_hbm, wg_ref, wu_ref, wd_ref, ys_hbm, *scratch):
    xbufs = scratch[:GATHER_SLOTS]
    ybufs = scratch[GATHER_SLOTS:GATHER_SLOTS + 2]
    gsem, osem, wg_bf, wu_bf, wd_bf, rowmap_ref = scratch[GATHER_SLOTS + 2:]
    ahead = GATHER_SLOTS - 1
    ntok = pos_ref.shape[0] // 2
    index_mask = (1 << ROWMAP_SHIFT) - 1
    e = pl.program_id(0)
    nused = bend_ref[N_EXPERTS - 1]
    b1 = bend_ref[e]
    b0 = jnp.where(e == 0, 0, bend_ref[jnp.maximum(e - 1, 0)])

    def gather_start(blk, s):
        base = blk * MOE_BLOCK
        for r in range(MOE_BLOCK):
            tok = rowmap_ref[base + r] & index_mask
            pltpu.make_async_copy(x_hbm.at[pl.ds(tok, 1), :], xbufs[s].at[pl.ds(r, 1), :],
                                  gsem.at[s]).start(priority=r % 2)

    def gather_wait(s):
        pltpu.make_async_copy(x_hbm.at[pl.ds(0, MOE_BLOCK), :], xbufs[s], gsem.at[s]).wait()

    def scatter_start(blk, s):
        base = blk * MOE_BLOCK
        for r in range(MOE_BLOCK):
            v = rowmap_ref[base + r]
            dst = (v >> ROWMAP_SHIFT) * ntok + (v & index_mask)
            pltpu.make_async_copy(ybufs[s].at[pl.ds(r, 1), :], ys_hbm.at[pl.ds(dst, 1), :], osem.at[s]).start()

    def block_copy(row0, s):
        return pltpu.make_async_copy(ybufs[s], ys_hbm.at[pl.ds(row0, MOE_BLOCK), :], osem.at[s])

    def scatter_wait(s):
        block_copy(0, s).wait()

    @pl.when(e == 0)
    def _():
        for s in range(2):
            ybufs[s][...] = jnp.zeros(ybufs[s].shape, U32)
            block_copy(2 * ntok + s * MOE_BLOCK, s).start()
        _build_rowmap(pos_ref, bend_ref, cnt_ref, rowmap_ref)
        for s in range(2):
            scatter_wait(s)
        for k in range(ahead):
            gather_start(k, k)

    @pl.when(b1 > b0)
    def _():
        wg_bf[...] = wg_ref[...].astype(BF16)
        wu_bf[...] = wu_ref[...].astype(BF16)
        wd_bf[...] = wd_ref[...].astype(BF16)

        def do_block(b, s):
            so = s % 2
            gather_wait(s)

            @pl.when(b >= 2)
            def _():
                scatter_wait(so)

            gather_start(jnp.minimum(b + ahead, nused - 1), (s + ahead) % GATHER_SLOTS)
            x = xbufs[s][...].astype(BF16)
            hg = jnp.dot(x, wg_bf[...], preferred_element_type=F32)
            hu = jnp.dot(x, wu_bf[...], preferred_element_type=F32)
            h = (jax.nn.silu(hg) * hu).astype(BF16)
            ybufs[so][...] = _pack_bf16_pairs(jnp.dot(h, wd_bf[...], preferred_element_type=F32))
            scatter_start(b, so)

        def body(b, _):
            for s in range(GATHER_SLOTS):
                @pl.when(b % GATHER_SLOTS == s)
                def _():
                    do_block(b, s)
            return 0

        lax.fori_loop(b0, b1, body, 0)

    @pl.when(e == N_EXPERTS - 1)
    def _():
        last = nused - 1
        for s in range(GATHER_SLOTS):
            @pl.when(last % GATHER_SLOTS == s)
            def _():
                for k in range(1, GATHER_SLOTS):
                    gather_wait((s + k) % GATHER_SLOTS)
        scatter_wait(0)
        scatter_wait(1)


def _moe(bend, cnt, pos, x1, w_gate, w_up, w_down, layer, nblocks):
    ntok = x1.shape[0]
    assert 2 * ntok > GATHER_SLOTS * MOE_BLOCK
    assert ntok <= 1 << ROWMAP_SHIFT and DUMP_ROWS <= ntok and ntok % MOE_BLOCK == 0
    wspec = lambda a, b: pl.BlockSpec((None, None, a, b), lambda e, be, cn, ps: (layer, e, 0, 0))
    grid_spec = pltpu.PrefetchScalarGridSpec(
        num_scalar_prefetch=3,
        grid=(N_EXPERTS,),
        in_specs=[pl.BlockSpec(memory_space=pl.ANY), wspec(D_MODEL, EXPERT_FF), wspec(D_MODEL, EXPERT_FF),
                  wspec(EXPERT_FF, D_MODEL)],
        out_specs=pl.BlockSpec(memory_space=pl.ANY),
        scratch_shapes=([pltpu.VMEM((MOE_BLOCK, D_MODEL), F32) for _ in range(GATHER_SLOTS)]
                        + [pltpu.VMEM((MOE_BLOCK, PACKED_W), U32) for _ in range(2)]
                        + [pltpu.SemaphoreType.DMA((GATHER_SLOTS,)), pltpu.SemaphoreType.DMA((2,)),
                           pltpu.VMEM((D_MODEL, EXPERT_FF), BF16),
                           pltpu.VMEM((D_MODEL, EXPERT_FF), BF16),
                           pltpu.VMEM((EXPERT_FF, D_MODEL), BF16),
                           pltpu.SMEM((nblocks * MOE_BLOCK,), jnp.int32)]),
    )
    return pl.pallas_call(
        _moe_body,
        grid_spec=grid_spec,
        out_shape=jax.ShapeDtypeStruct((2 * ntok + DUMP_ROWS, PACKED_W), U32),
        compiler_params=_cparams(("arbitrary",), bounds_checks=False),
        name="moe_experts",
    )(bend, cnt, pos, x1, w_gate, w_up, w_down)


COMB_T = 128


def _combine_body(y0_ref, y1_ref, x_ref, rt_ref, g_ref, b_ref, op_ref, os_ref, *, n_prompt_blocks):
    i = pl.program_id(0)
    rt = rt_ref[...]
    ff = (rt[:, R_W0:R_W0 + 1] * _unpack_bf16_pairs(y0_ref[...])
          + rt[:, R_W1:R_W1 + 1] * _unpack_bf16_pairs(y1_ref[...]))
    out = _layer_norm(DEEPNORM_ALPHA * x_ref[...] + ff, g_ref[...], b_ref[...])

    @pl.when(i < n_prompt_blocks)
    def _():
        op_ref[...] = out

    @pl.when(i >= n_prompt_blocks)
    def _():
        os_ref[...] = out


def _combine(ys, x1, route, g, b, n_prompt_rows):
    rows = x1.shape[0]
    npb = n_prompt_rows // COMB_T
    nblk = rows // COMB_T
    return pl.pallas_call(
        functools.partial(_combine_body, n_prompt_blocks=npb),
        grid=(nblk,),
        in_specs=[pl.BlockSpec((COMB_T, PACKED_W), lambda i: (i, 0)),
                  pl.BlockSpec((COMB_T, PACKED_W), lambda i: (i + nblk, 0)),
                  pl.BlockSpec((COMB_T, D_MODEL), lambda i: (i, 0)),
                  pl.BlockSpec((COMB_T, LANES), lambda i: (i, 0)),
                  _full((1, D_MODEL)), _full((1, D_MODEL))],
        out_specs=[pl.BlockSpec((COMB_T, D_MODEL), lambda i: (jnp.minimum(i, npb - 1), 0)),
                   pl.BlockSpec((COMB_T, D_MODEL), lambda i: (jnp.maximum(i - npb, 0), 0))],
        out_shape=[jax.ShapeDtypeStruct((n_prompt_rows, D_MODEL), F32),
                   jax.ShapeDtypeStruct((rows - n_prompt_rows, D_MODEL), F32)],
        compiler_params=_cparams(("arbitrary",)),
        name="combine_ln2",
    )(ys, ys, x1, route, g, b)


def kernel(x_prompt, x_sample, cache_k, cache_v, state_ssm_re, state_ssm_im, w_in, ssm_lambda_re, ssm_lambda_im, ssm_log_dt, ssm_b_re, ssm_b_im, ssm_c_re, ssm_c_im, ssm_d, ssm_w_glu, attn_sinks, attn_norm_g, ssm_norm_g, w_out, ln1_g, ln1_b, router_group_w, router_group_b, router_expert_w, router_expert_b, expert_w_gate, expert_w_up, expert_w_down, ln2_g, ln2_b):
    nbatch, seq, _ = x_prompt.shape
    nseq, dec_seq, _ = x_sample.shape
    wbuf = cache_k.shape[2]
    rows_p = nbatch * seq
    rows_s = nseq * dec_seq
    rows_all = rows_p + rows_s
    tm_p = 512

    xp = x_prompt.reshape(rows_p, D_MODEL)
    xs = x_sample.reshape(rows_s, D_MODEL)
    tab_p = _rope_tables(np.arange(seq))
    tab_s = _rope_tables(PAST_LEN + (np.arange(rows_s) % dec_seq))
    ck_all = cache_k.astype(F32).reshape(DEPTH, nseq, wbuf, KV_WIDTH)
    cv_all = cache_v.astype(F32).reshape(DEPTH, nseq, wbuf, KV_WIDTH)

    sp = _ssm_pack(jax.vmap(_ssm_params)(ssm_lambda_re, ssm_lambda_im, ssm_log_dt, ssm_b_re, ssm_b_im,
                                         ssm_c_re, ssm_c_im, ssm_d))
    w_in_bf = w_in.astype(BF16)
    w_glu_bf = ssm_w_glu.astype(BF16)
    w_out_bf = w_out.astype(BF16)

    outs = {k: [] for k in ("kp", "vp", "hrp", "hip", "ks", "vs", "hrs", "his")}
    for l in range(DEPTH):
        sinks = attn_sinks[l].astype(F32)
        attn_g = attn_norm_g[l].astype(F32).reshape(1, ATTN_WIDTH)
        ssm_g = ssm_norm_g[l].astype(F32).reshape(1, SSM_WIDTH)
        g1 = ln1_g[l].astype(F32).reshape(1, D_MODEL)
        b1 = ln1_b[l].astype(F32).reshape(1, D_MODEL)
        g2 = ln2_g[l].astype(F32).reshape(1, D_MODEL)
        b2 = ln2_b[l].astype(F32).reshape(1, D_MODEL)
        n_rt = N_EXPERT_GROUPS + N_EXPERTS
        wr = jnp.concatenate([router_group_w[l].astype(F32)]
                             + [router_expert_w[l, g].astype(F32) for g in range(N_EXPERT_GROUPS)]
                             + [jnp.zeros((D_MODEL, LANES - n_rt), F32)], axis=1)
        br = jnp.concatenate([router_group_b[l].astype(F32), router_expert_b[l].astype(F32).reshape(-1),
                              jnp.zeros((LANES - n_rt,), F32)]).reshape(1, LANES)

        q_p, k_p, v_p, u_p = _inproj(xp, w_in_bf, l, *tab_p, tm=tm_p)
        q_s, k_s, v_s, u_s = _inproj(xs, w_in_bf, l, *tab_s, tm=rows_s)

        an_p = _attn_prompt(sinks, q_p, k_p, v_p, attn_g, seq)
        an_s, nk_s, nv_s = _attn_sample(sinks, q_s, k_s, v_s, ck_all, cv_all, attn_g, dec_seq, l)

        y_p, hl_p = _ssm_prompt(u_p, sp, l, nbatch, seq)
        y_s, hl_s = _ssm_sample(u_s, _pack_state(state_ssm_re[l], state_ssm_im[l]), sp, l, nseq, dec_seq)
        sn_p = _glu(y_p, w_glu_bf, l, ssm_g, tm_p)
        sn_s = _glu(y_s, w_glu_bf, l, ssm_g, rows_s)

        x1, lg_p = _outproj(an_p, sn_p, xp, w_out_bf, l, g1, b1, wr, br, tm_p, rows_all, 0)
        x1, lg_s = _outproj(an_s, sn_s, xs, w_out_bf, l, g1, b1, wr, br, rows_s, rows_all, rows_p // rows_s,
                            x1_buf=x1)
        nblocks = 2 * rows_all // MOE_BLOCK + N_EXPERTS
        route, meta = _route_plan(jnp.concatenate([lg_p, lg_s], axis=0))
        pos = route[:, R_P0:R_P1 + 1].astype(jnp.int32).reshape(2 * rows_all)
        bend = meta[0, :N_EXPERTS].astype(jnp.int32)

        cnt = meta[1, :N_EXPERTS].astype(jnp.int32)
        ys = _moe(bend, cnt, pos, x1, expert_w_gate, expert_w_up, expert_w_down, l, nblocks)
        xp, xs = _combine(ys, x1, route, g2, b2, rows_p)

        wp = min(WINDOW, seq)
        tail = lambda a: a.reshape(nbatch, seq, KV_WIDTH)[:, seq - wp:].reshape(nbatch, wp, N_KV_HEADS, HEAD_DIM)
        outs["kp"].append(tail(k_p))
        outs["vp"].append(tail(v_p))
        hp = hl_p.reshape(nbatch, SSM_CHUNKS * 2 * CH_STATES)
        hr, hi = _unpack_state(hp, nbatch)
        outs["hrp"].append(hr)
        outs["hip"].append(hi)
        outs["ks"].append(nk_s.reshape(nseq, wbuf, N_KV_HEADS, HEAD_DIM))
        outs["vs"].append(nv_s.reshape(nseq, wbuf, N_KV_HEADS, HEAD_DIM))
        hr, hi = _unpack_state(hl_s, nseq)
        outs["hrs"].append(hr)
        outs["his"].append(hi)

    st = {k: jnp.stack(v) for k, v in outs.items()}
    return (xp.reshape(nbatch, seq, D_MODEL), xs.reshape(nseq, dec_seq, D_MODEL),
            st["kp"], st["vp"], st["hrp"], st["hip"], st["ks"], st["vs"], st["hrs"], st["his"])
```

```python
import functools

import numpy as np
import jax
import jax.numpy as jnp
from jax import lax
from jax.experimental import pallas as pl
from jax.experimental.pallas import tpu as pltpu

F32 = jnp.float32
BF16 = jnp.bfloat16

D_MODEL = 2048
DEPTH = 2
PAST_LEN = 16384
ATTN_WIDTH = 1024
SSM_WIDTH = 1024
HEAD_DIM = 64
N_HEADS = 16
N_KV_HEADS = 4
Q_PER_KV = 4
KV_WIDTH = 256
WINDOW = 128
ROPE_THETA = 10000.0
SSM_GROUP = 16
SSM_GROUPS = 64
SSM_STATE = 64
N_EXPERT_GROUPS = 4
EXPERTS_PER_GROUP = 8
N_EXPERTS = 32
EXPERT_FF = 512
MOE_BLOCK = 128
DEEPNORM_ALPHA = (2.0 * DEPTH) ** 0.25
LN_EPS = 1e-5
RMS_EPS = 1e-6

LANES = 128
SUBLANES = 8
VMEM_LIMIT = 56 * 1024 * 1024

SSM_CHUNKS = 4
CH_GROUPS = SSM_GROUPS // SSM_CHUNKS
CH_STATES = CH_GROUPS * SSM_STATE
CH_IN = CH_GROUPS * SSM_GROUP
SCAN_T = 512
SCAN_SEG = SCAN_T // SUBLANES


def _cparams(sem=None, vmem=VMEM_LIMIT, bounds_checks=True):
    return pltpu.CompilerParams(dimension_semantics=sem, vmem_limit_bytes=vmem,
                                disable_bounds_checks=not bounds_checks)


def _full(shape):
    n = len(shape)
    return pl.BlockSpec(shape, lambda *a: (0,) * n)


def _dot_nt(a, b):
    return lax.dot_general(a, b, (((1,), (1,)), ((), ())), preferred_element_type=F32)


def _inproj_body(x_ref, w_ref, cos_ref, sa_ref, sb_ref, q_ref, k_ref, v_ref, u_ref):
    x = x_ref[...].astype(BF16)
    cos = cos_ref[...]
    sa = sa_ref[...]
    sb = sb_ref[...]

    half = HEAD_DIM // 2

    def rope(c):
        return c * cos + pltpu.roll(c, LANES - half, 1) * sa + pltpu.roll(c, half, 1) * sb

    pq = jnp.dot(x, w_ref[:, 0:ATTN_WIDTH], preferred_element_type=F32)
    for j in range(ATTN_WIDTH // LANES):
        q_ref[:, j * LANES:(j + 1) * LANES] = rope(pq[:, j * LANES:(j + 1) * LANES]).astype(BF16)
    pk = jnp.dot(x, w_ref[:, ATTN_WIDTH:ATTN_WIDTH + KV_WIDTH], preferred_element_type=F32)
    for j in range(KV_WIDTH // LANES):
        k_ref[:, j * LANES:(j + 1) * LANES] = rope(pk[:, j * LANES:(j + 1) * LANES])
    v_ref[...] = jnp.dot(x, w_ref[:, ATTN_WIDTH + KV_WIDTH:ATTN_WIDTH + 2 * KV_WIDTH],
                         preferred_element_type=F32)
    u_ref[...] = jnp.dot(x, w_ref[:, ATTN_WIDTH + 2 * KV_WIDTH:], preferred_element_type=F32)


def _layer_spec(shape, layer):
    return pl.BlockSpec((None,) + tuple(shape), lambda *a: (layer,) + (0,) * len(shape))


def _inproj(x, w_bf, layer, cos_t, sa_t, sb_t, tm):
    rows = x.shape[0]
    tab_blocks = cos_t.shape[0] // tm
    in_w = w_bf.shape[2]
    row_spec = lambda w: pl.BlockSpec((tm, w), lambda i: (i, 0))
    tab_spec = pl.BlockSpec((tm, LANES), lambda i: (i % tab_blocks, 0))
    return pl.pallas_call(
        _inproj_body,
        grid=(rows // tm,),
        in_specs=[row_spec(D_MODEL), _layer_spec((D_MODEL, in_w), layer), tab_spec, tab_spec, tab_spec],
        out_specs=[row_spec(ATTN_WIDTH), row_spec(KV_WIDTH), row_spec(KV_WIDTH), row_spec(SSM_WIDTH)],
        out_shape=[jax.ShapeDtypeStruct((rows, ATTN_WIDTH), BF16),
                   jax.ShapeDtypeStruct((rows, KV_WIDTH), F32),
                   jax.ShapeDtypeStruct((rows, KV_WIDTH), F32),
                   jax.ShapeDtypeStruct((rows, SSM_WIDTH), F32)],
        compiler_params=_cparams(("arbitrary",)),
        name="inproj_rope",
    )(x, w_bf, cos_t, sa_t, sb_t)


def _rope_tables(positions):
    half = HEAD_DIM // 2
    inv_freq = ROPE_THETA ** (-np.arange(half, dtype=np.float64) / half)
    ang = positions.astype(np.float64)[:, None] * inv_freq[None, :]
    lane = np.arange(LANES)
    cos = np.cos(ang)[:, lane % half]
    sin = np.sin(ang)[:, lane % half]
    first = (lane % HEAD_DIM) < half
    sa = np.where(first[None, :], -sin, 0.0)
    sb = np.where(first[None, :], 0.0, sin)
    return (jnp.asarray(cos, F32), jnp.asarray(sa, F32), jnp.asarray(sb, F32))


ATTN_BLOCKS = 2


def _rms(x, g):
    return x * lax.rsqrt(jnp.mean(jnp.square(x), axis=-1, keepdims=True) + RMS_EPS) * g


def _attn_prompt_body(sink_ref, q_ref, kp_ref, kc_ref, vp_ref, vc_ref, g_ref, o_ref, acc_ref, *, blocks_per_seq):
    i = pl.program_id(0)
    first_has_prev = ((i * ATTN_BLOCKS) % blocks_per_seq) > 0
    kk = jnp.concatenate([kp_ref[...], kc_ref[...]], axis=0).astype(BF16)
    vt = jnp.concatenate([vp_ref[...], vc_ref[...]], axis=0).T.astype(BF16)
    ncol = Q_PER_KV * WINDOW
    key = lax.broadcasted_iota(jnp.int32, (2 * WINDOW, ncol), 0)
    col = lax.broadcasted_iota(jnp.int32, (2 * WINDOW, ncol), 1)
    qry = col % WINDOW
    in_band = (key > qry) & (key <= qry + WINDOW)
    head = lax.broadcasted_iota(jnp.int32, (1, ncol), 1) // WINDOW
    q = q_ref[...] * (HEAD_DIM ** -0.5)
    scores = {}
    for b in range(ATTN_BLOCKS):
        for g in range(N_KV_HEADS):
            qg = jnp.concatenate([q[b * WINDOW:(b + 1) * WINDOW,
                                    (g * Q_PER_KV + j) * HEAD_DIM:(g * Q_PER_KV + j + 1) * HEAD_DIM]
                                  for j in range(Q_PER_KV)], axis=0)
            kh = kk[b * WINDOW:(b + 2) * WINDOW, g * HEAD_DIM:(g + 1) * HEAD_DIM]
            scores[b, g] = lax.dot_general(kh, qg, (((1,), (1,)), ((), ())), preferred_element_type=F32)
    for b in range(ATTN_BLOCKS):
        visible = in_band & ((key >= WINDOW) | first_has_prev) if b == 0 else in_band
        for g in range(N_KV_HEADS):
            st = jnp.where(visible, scores[b, g], -jnp.inf)
            sink = jnp.zeros((1, ncol), F32)
            for j in range(Q_PER_KV):
                sink = jnp.where(head == j, sink_ref[g * Q_PER_KV + j], sink)
            m = jnp.maximum(jnp.max(st, axis=0, keepdims=True), sink)
            p = jnp.exp(st - m)
            denom = jnp.sum(p, axis=0, keepdims=True) + jnp.exp(sink - m)
            ot = jnp.dot(vt[g * HEAD_DIM:(g + 1) * HEAD_DIM, b * WINDOW:(b + 2) * WINDOW], p.astype(BF16),
                         preferred_element_type=F32)
            ot = ot / denom
            for j in range(Q_PER_KV):
                h = g * Q_PER_KV + j
                acc_ref[h * HEAD_DIM:(h + 1) * HEAD_DIM, b * WINDOW:(b + 1) * WINDOW] = (
                    ot[:, j * WINDOW:(j + 1) * WINDOW])
    o_ref[...] = _rms(acc_ref[...].T, g_ref[...]).astype(BF16)


def _attn_prompt(sinks, q, k, v, g, seq):
    rows = q.shape[0]
    bps = seq // WINDOW
    assert bps % ATTN_BLOCKS == 0
    step_rows = ATTN_BLOCKS * WINDOW
    cur = lambda w: pl.BlockSpec((step_rows, w), lambda i: (i, 0))
    prev = lambda w: pl.BlockSpec((WINDOW, w), lambda i: (jnp.maximum(i * ATTN_BLOCKS - 1, 0), 0))
    return pl.pallas_call(
        functools.partial(_attn_prompt_body, blocks_per_seq=bps),
        grid=(rows // step_rows,),
        in_specs=[pl.BlockSpec(memory_space=pltpu.SMEM), cur(ATTN_WIDTH), prev(KV_WIDTH), cur(KV_WIDTH),
                  prev(KV_WIDTH), cur(KV_WIDTH), _full((1, ATTN_WIDTH))],
        out_specs=cur(ATTN_WIDTH),
        out_shape=jax.ShapeDtypeStruct((rows, ATTN_WIDTH), BF16),
        scratch_shapes=[pltpu.VMEM((ATTN_WIDTH, step_rows), F32)],
        compiler_params=_cparams(("arbitrary",)),
        name="attn_prompt",
    )(sinks, q, k, k, v, v, g)


SEQ_PER_STEP = 8


def _attn_sample_body(sink_ref, q_ref, kn_ref, vn_ref, ck_ref, cv_ref, g_ref, o_ref, nk_ref, nv_ref, acc_ref,
                      *, dec_seq):
    nq = SEQ_PER_STEP * dec_seq
    wb = ck_ref.shape[1]
    nkc = SEQ_PER_STEP * wb
    ck = ck_ref[...].reshape(nkc, KV_WIDTH)
    cv = cv_ref[...].reshape(nkc, KV_WIDTH)
    kn = kn_ref[...]
    vn = vn_ref[...]
    ckb = ck.astype(BF16)
    cvb = cv.astype(BF16)
    knb = kn.astype(BF16)
    vnb = vn.astype(BF16)
    q = q_ref[...]

    rows = Q_PER_KV * nq
    r = lax.broadcasted_iota(jnp.int32, (rows, nkc), 0) % nq
    c = lax.broadcasted_iota(jnp.int32, (rows, nkc), 1)
    vis_c = ((r // dec_seq) == (c // wb)) & ((c % wb) >= (r % dec_seq) + 1 + (wb - WINDOW))
    rn = lax.broadcasted_iota(jnp.int32, (rows, nq), 0) % nq
    cn = lax.broadcasted_iota(jnp.int32, (rows, nq), 1)
    vis_n = ((rn // dec_seq) == (cn // dec_seq)) & ((cn % dec_seq) <= (rn % dec_seq))
    hrow = lax.broadcasted_iota(jnp.int32, (rows, 1), 0) // nq

    dn = (((1,), (1,)), ((), ()))
    for g in range(N_KV_HEADS):
        qs = jnp.concatenate([q[:, (g * Q_PER_KV + j) * HEAD_DIM:(g * Q_PER_KV + j + 1) * HEAD_DIM]
                              for j in range(Q_PER_KV)], axis=0)
        sl = slice(g * HEAD_DIM, (g + 1) * HEAD_DIM)
        sc = lax.dot_general(qs, ckb[:, sl], dn, preferred_element_type=F32) * (HEAD_DIM ** -0.5)
        sn = lax.dot_general(qs, knb[:, sl], dn, preferred_element_type=F32) * (HEAD_DIM ** -0.5)
        sc = jnp.where(vis_c, sc, -jnp.inf)
        sn = jnp.where(vis_n, sn, -jnp.inf)
        sink = jnp.zeros((rows, 1), F32)
        for j in range(Q_PER_KV):
            sink = jnp.where(hrow == j, sink_ref[g * Q_PER_KV + j], sink)
        m = jnp.maximum(jnp.maximum(jnp.max(sc, axis=-1, keepdims=True), jnp.max(sn, axis=-1, keepdims=True)), sink)
        pc = jnp.exp(sc - m)
        pn = jnp.exp(sn - m)
        denom = jnp.sum(pc, axis=-1, keepdims=True) + jnp.sum(pn, axis=-1, keepdims=True) + jnp.exp(sink - m)
        o = (jnp.dot(pc.astype(BF16), cvb[:, sl], preferred_element_type=F32)
             + jnp.dot(pn.astype(BF16), vnb[:, sl], preferred_element_type=F32)) / denom
        for j in range(Q_PER_KV):
            h = g * Q_PER_KV + j
            acc_ref[:, h * HEAD_DIM:(h + 1) * HEAD_DIM] = o[j * nq:(j + 1) * nq, :]
    o_ref[...] = _rms(acc_ref[...], g_ref[...]).astype(BF16)

    keep = wb - dec_seq
    for s in range(SEQ_PER_STEP):
        nk_ref[s, 0:keep, :] = ck_ref[s, dec_seq:wb, :]
        nv_ref[s, 0:keep, :] = cv_ref[s, dec_seq:wb, :]
        nk_ref[s, keep:wb, :] = kn[s * dec_seq:(s + 1) * dec_seq, :]
        nv_ref[s, keep:wb, :] = vn[s * dec_seq:(s + 1) * dec_seq, :]


def _attn_sample(sinks, q, kn, vn, ck, cv, g, dec_seq, layer):
    rows = q.shape[0]
    _, nseq, wb, _ = ck.shape
    nq = SEQ_PER_STEP * dec_seq
    rowb = lambda w: pl.BlockSpec((nq, w), lambda i: (i, 0))
    cin = pl.BlockSpec((None, SEQ_PER_STEP, wb, KV_WIDTH), lambda i: (layer, i, 0, 0))
    cb = pl.BlockSpec((SEQ_PER_STEP, wb, KV_WIDTH), lambda i: (i, 0, 0))
    return pl.pallas_call(
        functools.partial(_attn_sample_body, dec_seq=dec_seq),
        grid=(nseq // SEQ_PER_STEP,),
        in_specs=[pl.BlockSpec(memory_space=pltpu.SMEM), rowb(ATTN_WIDTH), rowb(KV_WIDTH), rowb(KV_WIDTH),
                  cin, cin, _full((1, ATTN_WIDTH))],
        out_specs=[rowb(ATTN_WIDTH), cb, cb],
        out_shape=[jax.ShapeDtypeStruct((rows, ATTN_WIDTH), BF16),
                   jax.ShapeDtypeStruct(ck.shape[1:], F32), jax.ShapeDtypeStruct(cv.shape[1:], F32)],
        scratch_shapes=[pltpu.VMEM((nq, ATTN_WIDTH), F32)],
        compiler_params=_cparams(("arbitrary",)),
        name="attn_sample",
    )(sinks, q, kn, vn, ck, cv, g)


def _ssm_prompt_body(ua_ref, uc_ref, b_ref, c_ref, lr_ref, li_ref, pr_ref, pi_ref, d_ref, y_ref, hl_ref,
                     uperm_ref, bu0, bu1, hb0, hb1, yp_ref, carry_ref, *, nchunks, ntc):
    k = pl.program_id(0)
    bus = (bu0, bu1)
    hbs = (hb0, hb1)
    lanes_per_chunk = CH_IN // LANES
    seg = lambda s: slice(s * SCAN_SEG, (s + 1) * SCAN_SEG)
    lane = lambda c: slice(c * LANES, (c + 1) * LANES)

    def stage_a(item, slot):
        cc = item % SSM_CHUNKS
        for c in range(lanes_per_chunk):
            for s in range(SUBLANES):
                uperm_ref[c, pl.ds(s, SCAN_SEG, stride=SUBLANES), :] = ua_ref[seg(s), lane(c)]
        up = jnp.concatenate([uperm_ref[c] for c in range(lanes_per_chunk)], axis=1)
        bus[slot][...] = _dot_nt(up.astype(BF16), b_ref[cc])

    def stage_b(item, slot):
        cc = item % SSM_CHUNKS
        tc = (item // SSM_CHUNKS) % ntc
        bu_ref = bus[slot]
        hb_ref = hbs[slot]
        ar = jnp.broadcast_to(lr_ref[cc], (SUBLANES, CH_STATES))
        ai = jnp.broadcast_to(li_ref[cc], (SUBLANES, CH_STATES))

        def advance(h, j):
            hr, hi = h
            br = bu_ref[j * SUBLANES:(j + 1) * SUBLANES, 0:CH_STATES]
            bi = bu_ref[j * SUBLANES:(j + 1) * SUBLANES, CH_STATES:2 * CH_STATES]
            return ar * hr + (br - ai * hi), ar * hi + (bi + ai * hr)

        zero = jnp.zeros((SUBLANES, CH_STATES), F32)
        h = (zero, zero)
        for j in range(SCAN_SEG):
            h = advance(h, j)
        fr, fi = h

        cst = jnp.where(tc == 0, 0.0, carry_ref[cc])
        c_r = cst[0:1, :]
        c_i = cst[1:2, :]
        lsr = pr_ref[cc]
        lsi = pi_ref[cc]
        rows_r, rows_i = [], []
        for s in range(SUBLANES):
            rows_r.append(c_r)
            rows_i.append(c_i)
            n_r = fr[s:s + 1, :] + (lsr * c_r - lsi * c_i)
            n_i = fi[s:s + 1, :] + (lsr * c_i + lsi * c_r)
            c_r, c_i = n_r, n_i
        carry_ref[cc] = jnp.concatenate([c_r, c_i], axis=0)
        hl_ref[0, cc] = jnp.concatenate([c_r, c_i], axis=0)

        h = (jnp.concatenate(rows_r, axis=0), jnp.concatenate(rows_i, axis=0))
        for jj in range(SCAN_SEG // 2):
            h1 = advance(h, 2 * jj)
            h = advance(h1, 2 * jj + 1)
            rows = slice(2 * jj * SUBLANES, (2 * jj + 2) * SUBLANES)
            hb_ref[rows, 0:CH_STATES] = jnp.concatenate([h1[0], h[0]], axis=0).astype(BF16)
            hb_ref[rows, CH_STATES:2 * CH_STATES] = jnp.concatenate([h1[1], h[1]], axis=0).astype(BF16)

    def stage_c(item, slot):
        cc = item % SSM_CHUNKS
        yp = _dot_nt(hbs[slot][...], c_ref[cc])
        d = d_ref[cc]
        for c in range(lanes_per_chunk):
            yp_ref[c] = yp[:, lane(c)]
        for s in range(SUBLANES):
            for c in range(lanes_per_chunk):
                y_ref[seg(s), lane(c)] = (yp_ref[c, pl.ds(s, SCAN_SEG, stride=SUBLANES), :]
                                          + d[:, lane(c)] * uc_ref[seg(s), lane(c)])

    def run(do_a, do_b, do_c, parity):
        if do_a:
            stage_a(k, parity)
        if do_b:
            stage_b(k - 1, 1 - parity)
        if do_c:
            stage_c(k - 2, parity)

    @pl.when(k == 0)
    def _():
        carry_ref[...] = jnp.zeros(carry_ref.shape, F32)
        run(True, False, False, 0)

    @pl.when(k == 1)
    def _():
        run(True, True, False, 1)

    for parity in range(2):
        @pl.when((k >= 2) & (k < nchunks) & (k % 2 == parity))
        def _():
            run(True, True, True, parity)

    @pl.when(k == nchunks)
    def _():
        run(False, True, True, nchunks % 2)

    @pl.when(k == nchunks + 1)
    def _():
        run(False, False, True, (nchunks + 1) % 2)


def _ssm_prompt(u, sp, layer, nbatch, seq):
    rows = u.shape[0]
    ntc = seq // SCAN_T
    nchunks = nbatch * ntc * SSM_CHUNKS
    assert nchunks >= 2
    item_a = lambda k: jnp.minimum(k, nchunks - 1)
    item_b = lambda k: jnp.clip(k - 1, 0, nchunks - 1)
    item_c = lambda k: jnp.clip(k - 2, 0, nchunks - 1)
    chunk_block = lambda item: (item // SSM_CHUNKS, item % SSM_CHUNKS)
    y, hl = pl.pallas_call(
        functools.partial(_ssm_prompt_body, nchunks=nchunks, ntc=ntc),
        grid=(nchunks + 2,),
        in_specs=[pl.BlockSpec((SCAN_T, CH_IN), lambda k: chunk_block(item_a(k))),
                  pl.BlockSpec((SCAN_T, CH_IN), lambda k: chunk_block(item_c(k))),
                  ] + [_layer_spec(sp[n].shape[1:], layer) for n in ("B", "C", "lr", "li", "pr", "pi", "d")],
        out_specs=[pl.BlockSpec((SCAN_T, CH_IN), lambda k: chunk_block(item_c(k))),
                   pl.BlockSpec((1, SSM_CHUNKS, 2, CH_STATES),
                                lambda k: (item_b(k) // (ntc * SSM_CHUNKS), 0, 0, 0))],
        out_shape=[jax.ShapeDtypeStruct((rows, SSM_WIDTH), F32),
                   jax.ShapeDtypeStruct((nbatch, SSM_CHUNKS, 2, CH_STATES), F32)],
        scratch_shapes=[pltpu.VMEM((CH_IN // LANES, SCAN_T, LANES), F32),
                        pltpu.VMEM((SCAN_T, 2 * CH_STATES), F32), pltpu.VMEM((SCAN_T, 2 * CH_STATES), F32),
                        pltpu.VMEM((SCAN_T, 2 * CH_STATES), BF16), pltpu.VMEM((SCAN_T, 2 * CH_STATES), BF16),
                        pltpu.VMEM((CH_IN // LANES, SCAN_T, LANES), F32),
                        pltpu.VMEM((SSM_CHUNKS, 2, CH_STATES), F32)],
        compiler_params=_cparams(("arbitrary",)),
        name="ssm_prompt",
    )(u, u, sp["B"], sp["C"], sp["lr"], sp["li"], sp["pr"], sp["pi"], sp["d"])
    return y, hl


def _ssm_sample_body(u_ref, h0_ref, b_ref, c_ref, lr_ref, li_ref, d_ref, y_ref, ho_ref, bu_ref, *, nseq, dec_seq):
    nl = CH_STATES // LANES
    for k in range(SSM_CHUNKS):
        uk = u_ref[:, k * CH_IN:(k + 1) * CH_IN]
        bu = _dot_nt(uk.astype(BF16), b_ref[k])
        for c in range(2 * nl):
            bu_ref[c] = bu[:, c * LANES:(c + 1) * LANES]
        base = k * 2 * CH_STATES
        hr = h0_ref[:, base:base + CH_STATES]
        hi = h0_ref[:, base + CH_STATES:base + 2 * CH_STATES]
        ar = lr_ref[k]
        ai = li_ref[k]
        for t in range(dec_seq):
            br = jnp.concatenate([bu_ref[c, pl.ds(t, nseq, stride=dec_seq), :] for c in range(nl)], axis=1)
            bi = jnp.concatenate([bu_ref[nl + c, pl.ds(t, nseq, stride=dec_seq), :] for c in range(nl)], axis=1)
            hr, hi = ar * hr + (br - ai * hi), ar * hi + (bi + ai * hr)
            for c in range(nl):
                bu_ref[c, pl.ds(t, nseq, stride=dec_seq), :] = hr[:, c * LANES:(c + 1) * LANES]
                bu_ref[nl + c, pl.ds(t, nseq, stride=dec_seq), :] = hi[:, c * LANES:(c + 1) * LANES]
        ho_ref[:, base:base + CH_STATES] = hr
        ho_ref[:, base + CH_STATES:base + 2 * CH_STATES] = hi
        hall = jnp.concatenate([bu_ref[c] for c in range(2 * nl)], axis=1)
        y_ref[:, k * CH_IN:(k + 1) * CH_IN] = (
            _dot_nt(hall.astype(BF16), c_ref[k]) + d_ref[k] * uk)


def _ssm_sample(u, h0, sp, layer, nseq, dec_seq):
    rows = u.shape[0]
    return pl.pallas_call(
        functools.partial(_ssm_sample_body, nseq=nseq, dec_seq=dec_seq),
        grid=(1,),
        in_specs=[_full(u.shape), _full(h0.shape)] + [_layer_spec(sp[n].shape[1:], layer)
                                                      for n in ("B", "C", "lr", "li", "d")],
        out_specs=[_full((rows, SSM_WIDTH)), _full(h0.shape)],
        out_shape=[jax.ShapeDtypeStruct((rows, SSM_WIDTH), F32), jax.ShapeDtypeStruct(h0.shape, F32)],
        scratch_shapes=[pltpu.VMEM((2 * CH_STATES // LANES, rows, LANES), F32)],
        compiler_params=_cparams(("arbitrary",)),
        name="ssm_sample",
    )(u, h0, sp["B"], sp["C"], sp["lr"], sp["li"], sp["d"])


def _ssm_params(lambda_re, lambda_im, log_dt, b_re, b_im, c_re, c_im, d):
    lam = lax.complex(jnp.minimum(lambda_re.astype(F32), -1e-4), lambda_im.astype(F32))
    dt = jnp.exp(log_dt.astype(F32))[:, None]
    lam_dt = lam * dt
    lam_bar = jnp.exp(lam_dt)
    b = lax.complex(b_re.astype(F32), b_im.astype(F32))
    b_bar = ((lam_bar - 1.0) / lam)[..., None] * b
    rows_b = lambda part: part.reshape(SSM_CHUNKS, CH_STATES, SSM_GROUP)
    rows_c = lambda part: part.reshape(SSM_CHUNKS, CH_IN, SSM_STATE)
    pw = jnp.exp(lam_dt * float(SCAN_SEG)).reshape(SSM_CHUNKS, 1, CH_STATES)
    lb = lam_bar.reshape(SSM_CHUNKS, 1, CH_STATES)
    return dict(b_re=rows_b(jnp.real(b_bar)), b_im=rows_b(jnp.imag(b_bar)),
                c_re=rows_c(c_re.astype(F32)), c_im_neg=rows_c(-c_im.astype(F32)),
                lr=jnp.real(lb), li=jnp.imag(lb), pr=jnp.real(pw), pi=jnp.imag(pw),
                d=d.astype(F32).reshape(SSM_CHUNKS, 1, CH_IN))


def _ssm_pack_body(bre_ref, bim_ref, cre_ref, cimn_ref, eb_ref, ec_ref, b_out, c_out):
    def expand(rows_ref, onehot_ref, row_group, col_group):
        t = jnp.dot(rows_ref[...].astype(BF16), onehot_ref[...], preferred_element_type=F32)
        own = lax.broadcasted_iota(jnp.int32, t.shape, 0) // row_group
        col = lax.broadcasted_iota(jnp.int32, t.shape, 1) // col_group
        return jnp.where(own == col, t, 0.0).astype(BF16)

    b_out[0:CH_STATES, :] = expand(bre_ref, eb_ref, SSM_STATE, SSM_GROUP)
    b_out[CH_STATES:2 * CH_STATES, :] = expand(bim_ref, eb_ref, SSM_STATE, SSM_GROUP)
    c_out[:, 0:CH_STATES] = expand(cre_ref, ec_ref, SSM_GROUP, SSM_STATE)
    c_out[:, CH_STATES:2 * CH_STATES] = expand(cimn_ref, ec_ref, SSM_GROUP, SSM_STATE)


def _ssm_pack(sp):
    depth = sp["b_re"].shape[0]
    eb = jnp.asarray(np.tile(np.eye(SSM_GROUP, dtype=np.float32), (1, CH_GROUPS)), BF16)
    ec = jnp.asarray(np.tile(np.eye(SSM_STATE, dtype=np.float32), (1, CH_GROUPS)), BF16)
    item = lambda r, c: pl.BlockSpec((None, None, r, c), lambda l, k: (l, k, 0, 0))
    bmat, cmat = pl.pallas_call(
        _ssm_pack_body,
        grid=(depth, SSM_CHUNKS),
        in_specs=[item(CH_STATES, SSM_GROUP), item(CH_STATES, SSM_GROUP), item(CH_IN, SSM_STATE),
                  item(CH_IN, SSM_STATE), _full(eb.shape), _full(ec.shape)],
        out_specs=[item(2 * CH_STATES, CH_IN), item(CH_IN, 2 * CH_STATES)],
        out_shape=[jax.ShapeDtypeStruct((depth, SSM_CHUNKS, 2 * CH_STATES, CH_IN), BF16),
                   jax.ShapeDtypeStruct((depth, SSM_CHUNKS, CH_IN, 2 * CH_STATES), BF16)],
        compiler_params=_cparams(("arbitrary", "arbitrary")),
        name="ssm_pack",
    )(sp["b_re"], sp["b_im"], sp["c_re"], sp["c_im_neg"], eb, ec)
    return dict(sp, B=bmat, C=cmat)


def _pack_state(re, im):
    n = re.shape[0]
    st = jnp.stack([re.reshape(n, SSM_CHUNKS, CH_STATES), im.reshape(n, SSM_CHUNKS, CH_STATES)], axis=2)
    return st.reshape(n, SSM_CHUNKS * 2 * CH_STATES).astype(F32)


def _unpack_state(h, n):
    st = h.reshape(n, SSM_CHUNKS, 2, CH_GROUPS, SSM_STATE)
    return (st[:, :, 0].reshape(n, SSM_GROUPS, SSM_STATE), st[:, :, 1].reshape(n, SSM_GROUPS, SSM_STATE))


def _glu_body(y_ref, w_ref, g_ref, o_ref):
    a = jax.nn.gelu(y_ref[...])
    z = a * jax.nn.sigmoid(jnp.dot(a.astype(BF16), w_ref[...], preferred_element_type=F32))
    o_ref[...] = _rms(z, g_ref[...]).astype(BF16)


def _glu(y, w_bf, layer, g, tm):
    rows = y.shape[0]
    spec = pl.BlockSpec((tm, SSM_WIDTH), lambda i: (i, 0))
    return pl.pallas_call(
        _glu_body,
        grid=(rows // tm,),
        in_specs=[spec, _layer_spec((SSM_WIDTH, SSM_WIDTH), layer), _full((1, SSM_WIDTH))],
        out_specs=spec,
        out_shape=jax.ShapeDtypeStruct((rows, SSM_WIDTH), BF16),
        compiler_params=_cparams(("arbitrary",)),
        name="glu_rms",
    )(y, w_bf, g)


def _layer_norm(y, g, b):
    mu = jnp.mean(y, axis=-1, keepdims=True)
    yc = y - mu
    var = jnp.mean(jnp.square(yc), axis=-1, keepdims=True)
    return yc * lax.rsqrt(var + LN_EPS) * g + b


OUTPROJ_SUB = 256


def _outproj_body(*refs, aliased, nblk):
    if aliased:
        refs = refs[1:]
    a_ref, s_ref, x_ref, w_ref, g_ref, b_ref, wrc_ref, br_ref, o_ref, lg_ref = refs
    i = pl.program_id(0)

    @pl.when(i < nblk)
    def _():
        tm = a_ref.shape[0]
        sub = min(tm, OUTPROJ_SUB)
        for r0 in range(0, tm, sub):
            rows = slice(r0, r0 + sub)
            acc = jnp.dot(a_ref[rows, :], w_ref[0:ATTN_WIDTH, :], preferred_element_type=F32)
            acc = acc + jnp.dot(s_ref[rows, :], w_ref[ATTN_WIDTH:, :], preferred_element_type=F32)
            x1 = _layer_norm(DEEPNORM_ALPHA * x_ref[rows, :] + acc, g_ref[...], b_ref[...])
            o_ref[rows, :] = x1
            hh = jnp.dot(x1.astype(BF16), wrc_ref[...], preferred_element_type=F32)
            lg_ref[rows, :] = hh[:, 0:LANES] + hh[:, LANES:] + br_ref[...]

    @pl.when(i >= nblk)
    def _():
        o_ref[...] = jnp.zeros(o_ref.shape, F32)


def _outproj(an, sn, x, w_bf, layer, g, b, wr, br, tm, total_rows, row_block0, x1_buf=None):
    rows = an.shape[0]
    aliased = x1_buf is not None
    nblk = rows // tm
    tail = 0 if aliased or rows == total_rows else 1
    blk = lambda i: jnp.minimum(i, nblk - 1)
    half = lambda: pl.BlockSpec((tm, ATTN_WIDTH), lambda i: (blk(i), 0))
    in_specs = [half(), half(), pl.BlockSpec((tm, D_MODEL), lambda i: (blk(i), 0)),
                _layer_spec((D_MODEL, D_MODEL), layer), _full((1, D_MODEL)), _full((1, D_MODEL)),
                _full((D_MODEL, 2 * LANES)), _full((1, LANES))]
    wr_hi = wr.astype(BF16)
    wr_lo = (wr - wr_hi.astype(F32)).astype(BF16)
    args = [an, sn, x, w_bf, g, b, jnp.concatenate([wr_hi, wr_lo], axis=1), br]
    if aliased:
        in_specs = [pl.BlockSpec(memory_space=pl.ANY)] + in_specs
        args = [x1_buf] + args
    return pl.pallas_call(
        functools.partial(_outproj_body, aliased=aliased, nblk=nblk),
        grid=(nblk + tail,),
        in_specs=in_specs,
        out_specs=[pl.BlockSpec((tm, D_MODEL), lambda i: (i + row_block0, 0)),
                   pl.BlockSpec((tm, LANES), lambda i: (blk(i), 0))],
        out_shape=[jax.ShapeDtypeStruct((total_rows, D_MODEL), F32), jax.ShapeDtypeStruct((rows, LANES), F32)],
        input_output_aliases={0: 0} if aliased else {},
        compiler_params=_cparams(("arbitrary",)),
        name="outproj_ln1",
    )(*args)


ROUTE_T = 640
R_E0, R_E1, R_W0, R_W1, R_P0, R_P1 = range(6)


def _route_select(x):
    lane = lax.broadcasted_iota(jnp.int32, x.shape, 1)
    big = jnp.int32(1 << 20)
    neg = -jnp.inf
    gmask = lane < N_EXPERT_GROUPS
    gm = jnp.max(jnp.where(gmask, x, neg), axis=-1, keepdims=True)
    grp = jnp.min(jnp.where(gmask & (x == gm), lane, big), axis=-1, keepdims=True)
    p_grp = 1.0 / jnp.sum(jnp.where(gmask, jnp.exp(x - gm), 0.0), axis=-1, keepdims=True)
    lo = N_EXPERT_GROUPS + grp * EXPERTS_PER_GROUP
    emask = (lane >= lo) & (lane < lo + EXPERTS_PER_GROUP)
    v1 = jnp.max(jnp.where(emask, x, neg), axis=-1, keepdims=True)
    i1 = jnp.min(jnp.where(emask & (x == v1), lane, big), axis=-1, keepdims=True)
    emask2 = emask & (lane != i1)
    v2 = jnp.max(jnp.where(emask2, x, neg), axis=-1, keepdims=True)
    i2 = jnp.min(jnp.where(emask2 & (x == v2), lane, big), axis=-1, keepdims=True)
    e2 = jnp.exp(v2 - v1)
    w1 = p_grp / (1.0 + e2)
    w2 = p_grp * e2 / (1.0 + e2)
    return (i1 - N_EXPERT_GROUPS).astype(F32), (i2 - N_EXPERT_GROUPS).astype(F32), w1, w2


def _route_plan_body(lg_ref, ltri_ref, utri_ref, o_ref, meta_ref, rec_ref, run_ref):
    ph = pl.program_id(0)
    j = pl.program_id(1)
    tm = lg_ref.shape[0]
    lane = lax.broadcasted_iota(jnp.int32, (tm, LANES), 1)
    lanef = lane.astype(F32)
    r0 = pl.multiple_of(j * tm, tm)

    @pl.when((ph == 0) & (j == 0))
    def _():
        run_ref[...] = jnp.zeros(run_ref.shape, F32)

    @pl.when(ph == 0)
    def _():
        e0, e1, w0, w1 = _route_select(lg_ref[...])
        oh0 = (lanef == e0).astype(F32)
        oh1 = (lanef == e1).astype(F32)
        oh2 = oh0 + oh1
        before = jnp.dot(ltri_ref[...], oh2.astype(BF16), preferred_element_type=F32) + run_ref[0:1, :]
        rk0 = jnp.sum(before * oh0, axis=-1, keepdims=True)
        rk1 = jnp.sum(before * oh1, axis=-1, keepdims=True)
        run_ref[...] = run_ref[...] + jnp.sum(oh2, axis=0, keepdims=True)
        rec = jnp.where(lane == R_E0, e0, 0.0)
        rec = jnp.where(lane == R_E1, e1, rec)
        rec = jnp.where(lane == R_W0, w0, rec)
        rec = jnp.where(lane == R_W1, w1, rec)
        rec = jnp.where(lane == R_P0, rk0, rec)
        rec = jnp.where(lane == R_P1, rk1, rec)
        rec_ref[pl.ds(r0, tm), :] = rec

    @pl.when(ph == 1)
    def _():
        counts = run_ref[...]
        nblk = jnp.floor((counts + (MOE_BLOCK - 1)) * (1.0 / MOE_BLOCK))
        end_blk = jnp.dot(nblk.astype(BF16), utri_ref[...], preferred_element_type=F32)
        start_row = (end_blk - nblk) * MOE_BLOCK
        rec = rec_ref[pl.ds(r0, tm), :]
        oh0 = (lanef == rec[:, R_E0:R_E0 + 1]).astype(F32)
        oh1 = (lanef == rec[:, R_E1:R_E1 + 1]).astype(F32)
        p0 = jnp.sum(oh0 * start_row[0:1, :], axis=-1, keepdims=True) + rec[:, R_P0:R_P0 + 1]
        p1 = jnp.sum(oh1 * start_row[0:1, :], axis=-1, keepdims=True) + rec[:, R_P1:R_P1 + 1]
        out = jnp.where(lane == R_P0, p0, rec)
        out = jnp.where(lane == R_P1, p1, out)
        o_ref[...] = out

        @pl.when(j == 0)
        def _():
            mrow = lax.broadcasted_iota(jnp.int32, meta_ref.shape, 0)
            meta_ref[...] = jnp.where(mrow == 1, counts, end_blk)


def _route_plan(logits):
    rows = logits.shape[0]
    tm = ROUTE_T
    ltri = jnp.asarray(np.tril(np.ones((tm, tm), np.float32), -1), BF16)
    utri = jnp.asarray(np.triu(np.ones((LANES, LANES), np.float32)), BF16)
    mrows = SUBLANES
    return pl.pallas_call(
        _route_plan_body,
        grid=(2, rows // tm),
        in_specs=[pl.BlockSpec((tm, LANES), lambda p, j: (j * (1 - p), 0)),
                  pl.BlockSpec((tm, tm), lambda p, j: (0, 0)),
                  pl.BlockSpec((LANES, LANES), lambda p, j: (0, 0))],
        out_specs=[pl.BlockSpec((tm, LANES), lambda p, j: (j * p, 0)),
                   pl.BlockSpec((mrows, LANES), lambda p, j: (0, 0))],
        out_shape=[jax.ShapeDtypeStruct((rows, LANES), F32), jax.ShapeDtypeStruct((mrows, LANES), F32)],
        scratch_shapes=[pltpu.VMEM((rows, LANES), F32), pltpu.VMEM((SUBLANES, LANES), F32)],
        compiler_params=_cparams(("arbitrary", "arbitrary")),
        name="route_plan",
    )(logits, ltri, utri)


U32 = jnp.uint32
PACKED_W = D_MODEL // 2


def _pack_bf16_pairs(y):
    bits = lax.bitcast_convert_type(y.astype(BF16).astype(F32), U32)
    return bits[:, 0:PACKED_W] | (bits[:, PACKED_W:] >> 16)


def _unpack_bf16_pairs(p):
    hi = lax.bitcast_convert_type(p & jnp.uint32(0xFFFF0000), F32)
    lo = lax.bitcast_convert_type(p << 16, F32)
    return jnp.concatenate([hi, lo], axis=1)


def _build_rowmap(pos_ref, bend_ref, cnt_ref, o_ref):
    unroll = 8
    ntok = pos_ref.shape[0] // 2

    def zero_padding(e, _):
        first_blk = jnp.where(e == 0, 0, bend_ref[jnp.maximum(e - 1, 0)])

        def zero(r, _):
            o_ref[r] = 0
            return 0
        lax.fori_loop(first_blk * MOE_BLOCK + cnt_ref[e], bend_ref[e] * MOE_BLOCK, zero, 0)
        return 0
    lax.fori_loop(0, N_EXPERTS, zero_padding, 0)

    def fill(c, _):
        for u in range(unroll):
            t = c * unroll + u
            o_ref[pos_ref[2 * t]] = t
            o_ref[pos_ref[2 * t + 1]] = t
        return 0
    lax.fori_loop(0, ntok // unroll, fill, 0)


GATHER_SLOTS = 4


def _moe_body(bend_ref, cnt_ref, pos_ref, x_hbm, wg_ref, wu_ref, wd_ref, ys_hbm, *scratch, nblocks):
    xbufs = scratch[:GATHER_SLOTS]
    ybufs = scratch[GATHER_SLOTS:GATHER_SLOTS + 2]
    gsem, osem, wg_bf, wu_bf, wd_bf, row_tok_ref = scratch[GATHER_SLOTS + 2:]
    ahead = GATHER_SLOTS - 1
    e = pl.program_id(0)
    nused = bend_ref[N_EXPERTS - 1]
    b1 = bend_ref[e]
    b0 = jnp.where(e == 0, 0, bend_ref[jnp.maximum(e - 1, 0)])

    def gather_start(blk, s):
        base = blk * MOE_BLOCK
        for r in range(MOE_BLOCK):
            tok = row_tok_ref[base + r]
            pltpu.make_async_copy(x_hbm.at[pl.ds(tok, 1), :], xbufs[s].at[pl.ds(r, 1), :],
                                  gsem.at[s]).start(priority=r % 2)

    def gather_wait(s):
        pltpu.make_async_copy(x_hbm.at[pl.ds(0, MOE_BLOCK), :], xbufs[s], gsem.at[s]).wait()

    def out_copy(blk, s):
        row0 = pl.multiple_of(blk * MOE_BLOCK, MOE_BLOCK)
        return pltpu.make_async_copy(ybufs[s], ys_hbm.at[pl.ds(row0, MOE_BLOCK), :], osem.at[s])

    @pl.when(e == 0)
    def _():
        _build_rowmap(pos_ref, bend_ref, cnt_ref, row_tok_ref)
        for k in range(ahead):
            gather_start(k, k)

    @pl.when(b1 > b0)
    def _():
        wg_bf[...] = wg_ref[...].astype(BF16)
        wu_bf[...] = wu_ref[...].astype(BF16)
        wd_bf[...] = wd_ref[...].astype(BF16)

        def do_block(b, s):
            so = s % 2
            gather_wait(s)

            @pl.when(b >= 2)
            def _():
                out_copy(b - 2, so).wait()

            gather_start(jnp.minimum(b + ahead, nused - 1), (s + ahead) % GATHER_SLOTS)
            x = xbufs[s][...].astype(BF16)
            hg = jnp.dot(x, wg_bf[...], preferred_element_type=F32)
            hu = jnp.dot(x, wu_bf[...], preferred_element_type=F32)
            h = (jax.nn.silu(hg) * hu).astype(BF16)
            ybufs[so][...] = _pack_bf16_pairs(jnp.dot(h, wd_bf[...], preferred_element_type=F32))
            out_copy(b, so).start()

        def body(b, _):
            for s in range(GATHER_SLOTS):
                @pl.when(b % GATHER_SLOTS == s)
                def _():
                    do_block(b, s)
            return 0

        lax.fori_loop(b0, b1, body, 0)

    @pl.when(e == N_EXPERTS - 1)
    def _():
        last = nused - 1
        for s in range(GATHER_SLOTS):
            @pl.when(last % GATHER_SLOTS == s)
            def _():
                for k in range(1, GATHER_SLOTS):
                    gather_wait((s + k) % GATHER_SLOTS)
        out_copy(0, 0).wait()
        out_copy(0, 1).wait()
        yb0 = ybufs[0]
        yb0[...] = jnp.zeros(yb0.shape, U32)

        def zstart(b, _):
            out_copy(b, 0).start()
            return 0

        def zwait(b, _):
            out_copy(b, 0).wait()
            return 0

        lax.fori_loop(nused, nblocks, zstart, 0)
        lax.fori_loop(nused, nblocks, zwait, 0)


def _moe(bend, cnt, pos, x1, w_gate, w_up, w_down, layer, nblocks):
    assert 2 * x1.shape[0] > GATHER_SLOTS * MOE_BLOCK
    wspec = lambda a, b: pl.BlockSpec((None, None, a, b), lambda e, be, cn, ps: (layer, e, 0, 0))
    grid_spec = pltpu.PrefetchScalarGridSpec(
        num_scalar_prefetch=3,
        grid=(N_EXPERTS,),
        in_specs=[pl.BlockSpec(memory_space=pl.ANY), wspec(D_MODEL, EXPERT_FF), wspec(D_MODEL, EXPERT_FF),
                  wspec(EXPERT_FF, D_MODEL)],
        out_specs=pl.BlockSpec(memory_space=pl.ANY),
        scratch_shapes=([pltpu.VMEM((MOE_BLOCK, D_MODEL), F32) for _ in range(GATHER_SLOTS)]
                        + [pltpu.VMEM((MOE_BLOCK, PACKED_W), U32) for _ in range(2)]
                        + [pltpu.SemaphoreType.DMA((GATHER_SLOTS,)), pltpu.SemaphoreType.DMA((2,)),
                           pltpu.VMEM((D_MODEL, EXPERT_FF), BF16),
                           pltpu.VMEM((D_MODEL, EXPERT_FF), BF16),
                           pltpu.VMEM((EXPERT_FF, D_MODEL), BF16),
                           pltpu.SMEM((nblocks * MOE_BLOCK,), jnp.int32)]),
    )
    return pl.pallas_call(
        functools.partial(_moe_body, nblocks=nblocks),
        grid_spec=grid_spec,
        out_shape=jax.ShapeDtypeStruct((nblocks * MOE_BLOCK, PACKED_W), U32),
        compiler_params=_cparams(("arbitrary",), bounds_checks=False),
        name="moe_experts",
    )(bend, cnt, pos, x1, w_gate, w_up, w_down)


COMB_T = 128


def _combine_body(pos_ref, ys_hbm, x_ref, rt_ref, g_ref, b_ref, op_ref, os_ref, *scratch, n_prompt_blocks):
    i = pl.program_id(0)
    nsteps = pl.num_programs(0)
    bufs = tuple((scratch[2 * s], scratch[2 * s + 1]) for s in range(GATHER_SLOTS))
    sem = scratch[2 * GATHER_SLOTS]
    ahead = GATHER_SLOTS - 1

    def gather_start(step, s):
        base = step * (2 * COMB_T)
        for r in range(COMB_T):
            for k in range(2):
                p = pos_ref[base + 2 * r + k]
                pltpu.make_async_copy(ys_hbm.at[pl.ds(p, 1), :], bufs[s][k].at[pl.ds(r, 1), :],
                                      sem.at[s]).start(priority=k)

    def gather_wait(s):
        for k in range(2):
            pltpu.make_async_copy(ys_hbm.at[pl.ds(0, COMB_T), :], bufs[s][k], sem.at[s]).wait()

    @pl.when(i == 0)
    def _():
        for k in range(ahead):
            gather_start(k, k)

    def do_tile(s):
        gather_wait(s)
        gather_start(jnp.minimum(i + ahead, nsteps - 1), (s + ahead) % GATHER_SLOTS)
        rt = rt_ref[...]
        ff = (rt[:, R_W0:R_W0 + 1] * _unpack_bf16_pairs(bufs[s][0][...])
              + rt[:, R_W1:R_W1 + 1] * _unpack_bf16_pairs(bufs[s][1][...]))
        out = _layer_norm(DEEPNORM_ALPHA * x_ref[...] + ff, g_ref[...], b_ref[...])

        @pl.when(i < n_prompt_blocks)
        def _():
            op_ref[...] = out

        @pl.when(i >= n_prompt_blocks)
        def _():
            os_ref[...] = out

        @pl.when(i == nsteps - 1)
        def _():
            for k in range(1, GATHER_SLOTS):
                gather_wait((s + k) % GATHER_SLOTS)

    for s in range(GATHER_SLOTS):
        @pl.when(i % GATHER_SLOTS == s)
        def _():
            do_tile(s)


def _combine(pos, ys, x1, route, g, b, n_prompt_rows):
    rows = x1.shape[0]
    npb = n_prompt_rows // COMB_T
    assert rows // COMB_T >= GATHER_SLOTS
    grid_spec = pltpu.PrefetchScalarGridSpec(
        num_scalar_prefetch=1,
        grid=(rows // COMB_T,),
        in_specs=[pl.BlockSpec(memory_space=pl.ANY),
                  pl.BlockSpec((COMB_T, D_MODEL), lambda i, p: (i, 0)),
                  pl.BlockSpec((COMB_T, LANES), lambda i, p: (i, 0)),
                  pl.BlockSpec((1, D_MODEL), lambda i, p: (0, 0)),
                  pl.BlockSpec((1, D_MODEL), lambda i, p: (0, 0))],
        out_specs=[pl.BlockSpec((COMB_T, D_MODEL), lambda i, p: (jnp.minimum(i, npb - 1), 0)),
                   pl.BlockSpec((COMB_T, D_MODEL), lambda i, p: (jnp.maximum(i - npb, 0), 0))],
        scratch_shapes=([pltpu.VMEM((COMB_T, PACKED_W), U32) for _ in range(2 * GATHER_SLOTS)]
                        + [pltpu.SemaphoreType.DMA((GATHER_SLOTS,))]),
    )
    return pl.pallas_call(
        functools.partial(_combine_body, n_prompt_blocks=npb),
        grid_spec=grid_spec,
        out_shape=[jax.ShapeDtypeStruct((n_prompt_rows, D_MODEL), F32),
                   jax.ShapeDtypeStruct((rows - n_prompt_rows, D_MODEL), F32)],
        compiler_params=_cparams(("arbitrary",), bounds_checks=False),
        name="combine_ln2",
    )(pos, ys, x1, route, g, b)


def kernel(x_prompt, x_sample, cache_k, cache_v, state_ssm_re, state_ssm_im, w_in, ssm_lambda_re, ssm_lambda_im, ssm_log_dt, ssm_b_re, ssm_b_im, ssm_c_re, ssm_c_im, ssm_d, ssm_w_glu, attn_sinks, attn_norm_g, ssm_norm_g, w_out, ln1_g, ln1_b, router_group_w, router_group_b, router_expert_w, router_expert_b, expert_w_gate, expert_w_up, expert_w_down, ln2_g, ln2_b):
    nbatch, seq, _ = x_prompt.shape
    nseq, dec_seq, _ = x_sample.shape
    wbuf = cache_k.shape[2]
    rows_p = nbatch * seq
    rows_s = nseq * dec_seq
    rows_all = rows_p + rows_s
    tm_p = 512

    xp = x_prompt.reshape(rows_p, D_MODEL)
    xs = x_sample.reshape(rows_s, D_MODEL)
    tab_p = _rope_tables(np.arange(seq))
    tab_s = _rope_tables(PAST_LEN + (np.arange(rows_s) % dec_seq))
    ck_all = cache_k.astype(F32).reshape(DEPTH, nseq, wbuf, KV_WIDTH)
    cv_all = cache_v.astype(F32).reshape(DEPTH, nseq, wbuf, KV_WIDTH)

    sp = _ssm_pack(jax.vmap(_ssm_params)(ssm_lambda_re, ssm_lambda_im, ssm_log_dt, ssm_b_re, ssm_b_im,
                                         ssm_c_re, ssm_c_im, ssm_d))
    w_in_bf = w_in.astype(BF16)
    w_glu_bf = ssm_w_glu.astype(BF16)
    w_out_bf = w_out.astype(BF16)

    outs = {k: [] for k in ("kp", "vp", "hrp", "hip", "ks", "vs", "hrs", "his")}
    for l in range(DEPTH):
        sinks = attn_sinks[l].astype(F32)
        attn_g = attn_norm_g[l].astype(F32).reshape(1, ATTN_WIDTH)
        ssm_g = ssm_norm_g[l].astype(F32).reshape(1, SSM_WIDTH)
        g1 = ln1_g[l].astype(F32).reshape(1, D_MODEL)
        b1 = ln1_b[l].astype(F32).reshape(1, D_MODEL)
        g2 = ln2_g[l].astype(F32).reshape(1, D_MODEL)
        b2 = ln2_b[l].astype(F32).reshape(1, D_MODEL)
        n_rt = N_EXPERT_GROUPS + N_EXPERTS
        wr = jnp.concatenate([router_group_w[l].astype(F32)]
                             + [router_expert_w[l, g].astype(F32) for g in range(N_EXPERT_GROUPS)]
                             + [jnp.zeros((D_MODEL, LANES - n_rt), F32)], axis=1)
        br = jnp.concatenate([router_group_b[l].astype(F32), router_expert_b[l].astype(F32).reshape(-1),
                              jnp.zeros((LANES - n_rt,), F32)]).reshape(1, LANES)

        q_p, k_p, v_p, u_p = _inproj(xp, w_in_bf, l, *tab_p, tm=tm_p)
        q_s, k_s, v_s, u_s = _inproj(xs, w_in_bf, l, *tab_s, tm=rows_s)

        an_p = _attn_prompt(sinks, q_p, k_p, v_p, attn_g, seq)
        an_s, nk_s, nv_s = _attn_sample(sinks, q_s, k_s, v_s, ck_all, cv_all, attn_g, dec_seq, l)

        y_p, hl_p = _ssm_prompt(u_p, sp, l, nbatch, seq)
        y_s, hl_s = _ssm_sample(u_s, _pack_state(state_ssm_re[l], state_ssm_im[l]), sp, l, nseq, dec_seq)
        sn_p = _glu(y_p, w_glu_bf, l, ssm_g, 2 * tm_p)
        sn_s = _glu(y_s, w_glu_bf, l, ssm_g, rows_s)

        x1, lg_p = _outproj(an_p, sn_p, xp, w_out_bf, l, g1, b1, wr, br, tm_p, rows_all, 0)
        x1, lg_s = _outproj(an_s, sn_s, xs, w_out_bf, l, g1, b1, wr, br, rows_s, rows_all, rows_p // rows_s,
                            x1_buf=x1)
        nblocks = 2 * rows_all // MOE_BLOCK + N_EXPERTS
        route, meta = _route_plan(jnp.concatenate([lg_p, lg_s], axis=0))
        pos = route[:, R_P0:R_P1 + 1].astype(jnp.int32).reshape(2 * rows_all)
        bend = meta[0, :N_EXPERTS].astype(jnp.int32)

        cnt = meta[1, :N_EXPERTS].astype(jnp.int32)
        ys = _moe(bend, cnt, pos, x1, expert_w_gate, expert_w_up, expert_w_down, l, nblocks)
        xp, xs = _combine(pos, ys, x1, route, g2, b2, rows_p)

        wp = min(WINDOW, seq)
        tail = lambda a: a.reshape(nbatch, seq, KV_WIDTH)[:, seq - wp:].reshape(nbatch, wp, N_KV_HEADS, HEAD_DIM)
        outs["kp"].append(tail(k_p))
        outs["vp"].append(tail(v_p))
        hp = hl_p.reshape(nbatch, SSM_CHUNKS * 2 * CH_STATES)
        hr, hi = _unpack_state(hp, nbatch)
        outs["hrp"].append(hr)
        outs["hip"].append(hi)
        outs["ks"].append(nk_s.reshape(nseq, wbuf, N_KV_HEADS, HEAD_DIM))
        outs["vs"].append(nv_s.reshape(nseq, wbuf, N_KV_HEADS, HEAD_DIM))
        hr, hi = _unpack_state(hl_s, nseq)
        outs["hrs"].append(hr)
        outs["his"].append(hi)

    st = {k: jnp.stack(v) for k, v in outs.items()}
    return (xp.reshape(nbatch, seq, D_MODEL), xs.reshape(nseq, dec_seq, D_MODEL),
            st["kp"], st["vp"], st["hrp"], st["hip"], st["ks"], st["vs"], st["hrs"], st["his"])
```

```python
import functools

import numpy as np
import jax
import jax.numpy as jnp
from jax import lax
from jax.experimental import pallas as pl
from jax.experimental.pallas import tpu as pltpu

F32 = jnp.float32
BF16 = jnp.bfloat16

D_MODEL = 2048
DEPTH = 2
PAST_LEN = 16384
ATTN_WIDTH = 1024
SSM_WIDTH = 1024
HEAD_DIM = 64
N_HEADS = 16
N_KV_HEADS = 4
Q_PER_KV = 4
KV_WIDTH = 256
WINDOW = 128
ROPE_THETA = 10000.0
SSM_GROUP = 16
SSM_GROUPS = 64
SSM_STATE = 64
N_EXPERT_GROUPS = 4
EXPERTS_PER_GROUP = 8
N_EXPERTS = 32
EXPERT_FF = 512
MOE_BLOCK = 128
DEEPNORM_ALPHA = (2.0 * DEPTH) ** 0.25
LN_EPS = 1e-5
RMS_EPS = 1e-6

LANES = 128
SUBLANES = 8
VMEM_LIMIT = 56 * 1024 * 1024

SSM_CHUNKS = 4
CH_GROUPS = SSM_GROUPS // SSM_CHUNKS
CH_STATES = CH_GROUPS * SSM_STATE
CH_IN = CH_GROUPS * SSM_GROUP
SCAN_T = 512
SCAN_SEG = SCAN_T // SUBLANES


def _cparams(sem=None, vmem=VMEM_LIMIT, bounds_checks=True):
    return pltpu.CompilerParams(dimension_semantics=sem, vmem_limit_bytes=vmem,
                                disable_bounds_checks=not bounds_checks)


def _full(shape):
    n = len(shape)
    return pl.BlockSpec(shape, lambda *a: (0,) * n)


def _dot_nt(a, b):
    return lax.dot_general(a, b, (((1,), (1,)), ((), ())), preferred_element_type=F32)


def _inproj_body(x_ref, w_ref, cos_ref, sa_ref, sb_ref, q_ref, k_ref, v_ref, u_ref):
    x = x_ref[...].astype(BF16)
    cos = cos_ref[...]
    sa = sa_ref[...]
    sb = sb_ref[...]

    half = HEAD_DIM // 2

    def rope(c):
        return c * cos + pltpu.roll(c, LANES - half, 1) * sa + pltpu.roll(c, half, 1) * sb

    pq = jnp.dot(x, w_ref[:, 0:ATTN_WIDTH], preferred_element_type=F32)
    for j in range(ATTN_WIDTH // LANES):
        q_ref[:, j * LANES:(j + 1) * LANES] = rope(pq[:, j * LANES:(j + 1) * LANES]).astype(BF16)
    pk = jnp.dot(x, w_ref[:, ATTN_WIDTH:ATTN_WIDTH + KV_WIDTH], preferred_element_type=F32)
    for j in range(KV_WIDTH // LANES):
        k_ref[:, j * LANES:(j + 1) * LANES] = rope(pk[:, j * LANES:(j + 1) * LANES])
    v_ref[...] = jnp.dot(x, w_ref[:, ATTN_WIDTH + KV_WIDTH:ATTN_WIDTH + 2 * KV_WIDTH],
                         preferred_element_type=F32)
    u_ref[...] = jnp.dot(x, w_ref[:, ATTN_WIDTH + 2 * KV_WIDTH:], preferred_element_type=F32)


def _layer_spec(shape, layer):
    return pl.BlockSpec((None,) + tuple(shape), lambda *a: (layer,) + (0,) * len(shape))


def _inproj(x, w_bf, layer, cos_t, sa_t, sb_t, tm):
    rows = x.shape[0]
    tab_blocks = cos_t.shape[0] // tm
    in_w = w_bf.shape[2]
    row_spec = lambda w: pl.BlockSpec((tm, w), lambda i: (i, 0))
    tab_spec = pl.BlockSpec((tm, LANES), lambda i: (i % tab_blocks, 0))
    return pl.pallas_call(
        _inproj_body,
        grid=(rows // tm,),
        in_specs=[row_spec(D_MODEL), _layer_spec((D_MODEL, in_w), layer), tab_spec, tab_spec, tab_spec],
        out_specs=[row_spec(ATTN_WIDTH), row_spec(KV_WIDTH), row_spec(KV_WIDTH), row_spec(SSM_WIDTH)],
        out_shape=[jax.ShapeDtypeStruct((rows, ATTN_WIDTH), BF16),
                   jax.ShapeDtypeStruct((rows, KV_WIDTH), F32),
                   jax.ShapeDtypeStruct((rows, KV_WIDTH), F32),
                   jax.ShapeDtypeStruct((rows, SSM_WIDTH), F32)],
        compiler_params=_cparams(("arbitrary",)),
        name="inproj_rope",
    )(x, w_bf, cos_t, sa_t, sb_t)


def _rope_tables(positions):
    half = HEAD_DIM // 2
    inv_freq = ROPE_THETA ** (-np.arange(half, dtype=np.float64) / half)
    ang = positions.astype(np.float64)[:, None] * inv_freq[None, :]
    lane = np.arange(LANES)
    cos = np.cos(ang)[:, lane % half]
    sin = np.sin(ang)[:, lane % half]
    first = (lane % HEAD_DIM) < half
    sa = np.where(first[None, :], -sin, 0.0)
    sb = np.where(first[None, :], 0.0, sin)
    return (jnp.asarray(cos, F32), jnp.asarray(sa, F32), jnp.asarray(sb, F32))


ATTN_BLOCKS = 4


def _rms(x, g):
    return x * lax.rsqrt(jnp.mean(jnp.square(x), axis=-1, keepdims=True) + RMS_EPS) * g


def _attn_prompt_body(sink_ref, q_ref, kp_ref, kc_ref, vp_ref, vc_ref, g_ref, o_ref, acc_ref, *, blocks_per_seq):
    i = pl.program_id(0)
    first_has_prev = ((i * ATTN_BLOCKS) % blocks_per_seq) > 0
    kk = jnp.concatenate([kp_ref[...], kc_ref[...]], axis=0).astype(BF16)
    vt = jnp.concatenate([vp_ref[...], vc_ref[...]], axis=0).T.astype(BF16)
    ncol = Q_PER_KV * WINDOW
    key = lax.broadcasted_iota(jnp.int32, (2 * WINDOW, ncol), 0)
    col = lax.broadcasted_iota(jnp.int32, (2 * WINDOW, ncol), 1)
    qry = col % WINDOW
    in_band = (key > qry) & (key <= qry + WINDOW)
    head = lax.broadcasted_iota(jnp.int32, (1, ncol), 1) // WINDOW
    q = q_ref[...] * (HEAD_DIM ** -0.5)
    scores = {}
    for b in range(ATTN_BLOCKS):
        for g in range(N_KV_HEADS):
            qg = jnp.concatenate([q[b * WINDOW:(b + 1) * WINDOW,
                                    (g * Q_PER_KV + j) * HEAD_DIM:(g * Q_PER_KV + j + 1) * HEAD_DIM]
                                  for j in range(Q_PER_KV)], axis=0)
            kh = kk[b * WINDOW:(b + 2) * WINDOW, g * HEAD_DIM:(g + 1) * HEAD_DIM]
            scores[b, g] = lax.dot_general(kh, qg, (((1,), (1,)), ((), ())), preferred_element_type=F32)
    for b in range(ATTN_BLOCKS):
        visible = in_band & ((key >= WINDOW) | first_has_prev) if b == 0 else in_band
        for g in range(N_KV_HEADS):
            st = jnp.where(visible, scores[b, g], -jnp.inf)
            sink = jnp.zeros((1, ncol), F32)
            for j in range(Q_PER_KV):
                sink = jnp.where(head == j, sink_ref[g * Q_PER_KV + j], sink)
            m = jnp.maximum(jnp.max(st, axis=0, keepdims=True), sink)
            p = jnp.exp(st - m)
            denom = jnp.sum(p, axis=0, keepdims=True) + jnp.exp(sink - m)
            ot = jnp.dot(vt[g * HEAD_DIM:(g + 1) * HEAD_DIM, b * WINDOW:(b + 2) * WINDOW], p.astype(BF16),
                         preferred_element_type=F32)
            ot = ot / denom
            for j in range(Q_PER_KV):
                h = g * Q_PER_KV + j
                acc_ref[h * HEAD_DIM:(h + 1) * HEAD_DIM, b * WINDOW:(b + 1) * WINDOW] = (
                    ot[:, j * WINDOW:(j + 1) * WINDOW])
    o_ref[...] = _rms(acc_ref[...].T, g_ref[...]).astype(BF16)


def _attn_prompt(sinks, q, k, v, g, seq):
    rows = q.shape[0]
    bps = seq // WINDOW
    assert bps % ATTN_BLOCKS == 0
    step_rows = ATTN_BLOCKS * WINDOW
    cur = lambda w: pl.BlockSpec((step_rows, w), lambda i: (i, 0))
    prev = lambda w: pl.BlockSpec((WINDOW, w), lambda i: (jnp.maximum(i * ATTN_BLOCKS - 1, 0), 0))
    return pl.pallas_call(
        functools.partial(_attn_prompt_body, blocks_per_seq=bps),
        grid=(rows // step_rows,),
        in_specs=[pl.BlockSpec(memory_space=pltpu.SMEM), cur(ATTN_WIDTH), prev(KV_WIDTH), cur(KV_WIDTH),
                  prev(KV_WIDTH), cur(KV_WIDTH), _full((1, ATTN_WIDTH))],
        out_specs=cur(ATTN_WIDTH),
        out_shape=jax.ShapeDtypeStruct((rows, ATTN_WIDTH), BF16),
        scratch_shapes=[pltpu.VMEM((ATTN_WIDTH, step_rows), F32)],
        compiler_params=_cparams(("arbitrary",)),
        name="attn_prompt",
    )(sinks, q, k, k, v, v, g)


SEQ_PER_STEP = 8


def _attn_sample_body(sink_ref, q_ref, kn_ref, vn_ref, ck_ref, cv_ref, g_ref, o_ref, nk_ref, nv_ref, acc_ref,
                      *, dec_seq):
    nq = SEQ_PER_STEP * dec_seq
    wb = ck_ref.shape[1]
    nkc = SEQ_PER_STEP * wb
    ck = ck_ref[...].reshape(nkc, KV_WIDTH)
    cv = cv_ref[...].reshape(nkc, KV_WIDTH)
    kn = kn_ref[...]
    vn = vn_ref[...]
    ckb = ck.astype(BF16)
    cvb = cv.astype(BF16)
    knb = kn.astype(BF16)
    vnb = vn.astype(BF16)
    q = q_ref[...]

    rows = Q_PER_KV * nq
    r = lax.broadcasted_iota(jnp.int32, (rows, nkc), 0) % nq
    c = lax.broadcasted_iota(jnp.int32, (rows, nkc), 1)
    vis_c = ((r // dec_seq) == (c // wb)) & ((c % wb) >= (r % dec_seq) + 1 + (wb - WINDOW))
    rn = lax.broadcasted_iota(jnp.int32, (rows, nq), 0) % nq
    cn = lax.broadcasted_iota(jnp.int32, (rows, nq), 1)
    vis_n = ((rn // dec_seq) == (cn // dec_seq)) & ((cn % dec_seq) <= (rn % dec_seq))
    hrow = lax.broadcasted_iota(jnp.int32, (rows, 1), 0) // nq

    dn = (((1,), (1,)), ((), ()))
    for g in range(N_KV_HEADS):
        qs = jnp.concatenate([q[:, (g * Q_PER_KV + j) * HEAD_DIM:(g * Q_PER_KV + j + 1) * HEAD_DIM]
                              for j in range(Q_PER_KV)], axis=0)
        sl = slice(g * HEAD_DIM, (g + 1) * HEAD_DIM)
        sc = lax.dot_general(qs, ckb[:, sl], dn, preferred_element_type=F32) * (HEAD_DIM ** -0.5)
        sn = lax.dot_general(qs, knb[:, sl], dn, preferred_element_type=F32) * (HEAD_DIM ** -0.5)
        sc = jnp.where(vis_c, sc, -jnp.inf)
        sn = jnp.where(vis_n, sn, -jnp.inf)
        sink = jnp.zeros((rows, 1), F32)
        for j in range(Q_PER_KV):
            sink = jnp.where(hrow == j, sink_ref[g * Q_PER_KV + j], sink)
        m = jnp.maximum(jnp.maximum(jnp.max(sc, axis=-1, keepdims=True), jnp.max(sn, axis=-1, keepdims=True)), sink)
        pc = jnp.exp(sc - m)
        pn = jnp.exp(sn - m)
        denom = jnp.sum(pc, axis=-1, keepdims=True) + jnp.sum(pn, axis=-1, keepdims=True) + jnp.exp(sink - m)
        o = (jnp.dot(pc.astype(BF16), cvb[:, sl], preferred_element_type=F32)
             + jnp.dot(pn.astype(BF16), vnb[:, sl], preferred_element_type=F32)) / denom
        for j in range(Q_PER_KV):
            h = g * Q_PER_KV + j
            acc_ref[:, h * HEAD_DIM:(h + 1) * HEAD_DIM] = o[j * nq:(j + 1) * nq, :]
    o_ref[...] = _rms(acc_ref[...], g_ref[...]).astype(BF16)

    keep = wb - dec_seq
    for s in range(SEQ_PER_STEP):
        nk_ref[s, 0:keep, :] = ck_ref[s, dec_seq:wb, :]
        nv_ref[s, 0:keep, :] = cv_ref[s, dec_seq:wb, :]
        nk_ref[s, keep:wb, :] = kn[s * dec_seq:(s + 1) * dec_seq, :]
        nv_ref[s, keep:wb, :] = vn[s * dec_seq:(s + 1) * dec_seq, :]


def _attn_sample(sinks, q, kn, vn, ck, cv, g, dec_seq, layer):
    rows = q.shape[0]
    _, nseq, wb, _ = ck.shape
    nq = SEQ_PER_STEP * dec_seq
    rowb = lambda w: pl.BlockSpec((nq, w), lambda i: (i, 0))
    cin = pl.BlockSpec((None, SEQ_PER_STEP, wb, KV_WIDTH), lambda i: (layer, i, 0, 0))
    cb = pl.BlockSpec((SEQ_PER_STEP, wb, KV_WIDTH), lambda i: (i, 0, 0))
    return pl.pallas_call(
        functools.partial(_attn_sample_body, dec_seq=dec_seq),
        grid=(nseq // SEQ_PER_STEP,),
        in_specs=[pl.BlockSpec(memory_space=pltpu.SMEM), rowb(ATTN_WIDTH), rowb(KV_WIDTH), rowb(KV_WIDTH),
                  cin, cin, _full((1, ATTN_WIDTH))],
        out_specs=[rowb(ATTN_WIDTH), cb, cb],
        out_shape=[jax.ShapeDtypeStruct((rows, ATTN_WIDTH), BF16),
                   jax.ShapeDtypeStruct(ck.shape[1:], F32), jax.ShapeDtypeStruct(cv.shape[1:], F32)],
        scratch_shapes=[pltpu.VMEM((nq, ATTN_WIDTH), F32)],
        compiler_params=_cparams(("arbitrary",)),
        name="attn_sample",
    )(sinks, q, kn, vn, ck, cv, g)


def _ssm_prompt_body(ua_ref, uc_ref, b_ref, c_ref, lr_ref, li_ref, pr_ref, pi_ref, d_ref, y_ref, hl_ref,
                     uperm_ref, bu0, bu1, hb0, hb1, yp_ref, carry_ref, *, nchunks, ntc):
    k = pl.program_id(0)
    bus = (bu0, bu1)
    hbs = (hb0, hb1)
    lanes_per_chunk = CH_IN // LANES
    seg = lambda s: slice(s * SCAN_SEG, (s + 1) * SCAN_SEG)
    lane = lambda c: slice(c * LANES, (c + 1) * LANES)

    def stage_a(item, slot):
        cc = item % SSM_CHUNKS
        for c in range(lanes_per_chunk):
            for s in range(SUBLANES):
                uperm_ref[c, pl.ds(s, SCAN_SEG, stride=SUBLANES), :] = ua_ref[seg(s), lane(c)]
        up = jnp.concatenate([uperm_ref[c] for c in range(lanes_per_chunk)], axis=1)
        bus[slot][...] = _dot_nt(up.astype(BF16), b_ref[cc])

    def stage_b(item, slot):
        cc = item % SSM_CHUNKS
        tc = (item // SSM_CHUNKS) % ntc
        bu_ref = bus[slot]
        hb_ref = hbs[slot]
        ar = jnp.broadcast_to(lr_ref[cc], (SUBLANES, CH_STATES))
        ai = jnp.broadcast_to(li_ref[cc], (SUBLANES, CH_STATES))

        def advance(h, j):
            hr, hi = h
            br = bu_ref[j * SUBLANES:(j + 1) * SUBLANES, 0:CH_STATES]
            bi = bu_ref[j * SUBLANES:(j + 1) * SUBLANES, CH_STATES:2 * CH_STATES]
            return ar * hr + (br - ai * hi), ar * hi + (bi + ai * hr)

        zero = jnp.zeros((SUBLANES, CH_STATES), F32)
        h = (zero, zero)
        for j in range(SCAN_SEG):
            h = advance(h, j)
        fr, fi = h

        cst = jnp.where(tc == 0, 0.0, carry_ref[cc])
        c_r = cst[0:1, :]
        c_i = cst[1:2, :]
        lsr = pr_ref[cc]
        lsi = pi_ref[cc]
        rows_r, rows_i = [], []
        for s in range(SUBLANES):
            rows_r.append(c_r)
            rows_i.append(c_i)
            n_r = fr[s:s + 1, :] + (lsr * c_r - lsi * c_i)
            n_i = fi[s:s + 1, :] + (lsr * c_i + lsi * c_r)
            c_r, c_i = n_r, n_i
        carry_ref[cc] = jnp.concatenate([c_r, c_i], axis=0)
        hl_ref[0, cc] = jnp.concatenate([c_r, c_i], axis=0)

        h = (jnp.concatenate(rows_r, axis=0), jnp.concatenate(rows_i, axis=0))
        for jj in range(SCAN_SEG // 2):
            h1 = advance(h, 2 * jj)
            h = advance(h1, 2 * jj + 1)
            rows = slice(2 * jj * SUBLANES, (2 * jj + 2) * SUBLANES)
            hb_ref[rows, 0:CH_STATES] = jnp.concatenate([h1[0], h[0]], axis=0).astype(BF16)
            hb_ref[rows, CH_STATES:2 * CH_STATES] = jnp.concatenate([h1[1], h[1]], axis=0).astype(BF16)

    def stage_c(item, slot):
        cc = item % SSM_CHUNKS
        yp = _dot_nt(hbs[slot][...], c_ref[cc])
        d = d_ref[cc]
        for c in range(lanes_per_chunk):
            yp_ref[c] = yp[:, lane(c)]
        for s in range(SUBLANES):
            for c in range(lanes_per_chunk):
                y_ref[seg(s), lane(c)] = (yp_ref[c, pl.ds(s, SCAN_SEG, stride=SUBLANES), :]
                                          + d[:, lane(c)] * uc_ref[seg(s), lane(c)])

    def run(do_a, do_b, do_c, parity):
        if do_a:
            stage_a(k, parity)
        if do_b:
            stage_b(k - 1, 1 - parity)
        if do_c:
            stage_c(k - 2, parity)

    @pl.when(k == 0)
    def _():
        carry_ref[...] = jnp.zeros(carry_ref.shape, F32)
        run(True, False, False, 0)

    @pl.when(k == 1)
    def _():
        run(True, True, False, 1)

    for parity in range(2):
        @pl.when((k >= 2) & (k < nchunks) & (k % 2 == parity))
        def _():
            run(True, True, True, parity)

    @pl.when(k == nchunks)
    def _():
        run(False, True, True, nchunks % 2)

    @pl.when(k == nchunks + 1)
    def _():
        run(False, False, True, (nchunks + 1) % 2)


def _ssm_prompt(u, sp, layer, nbatch, seq):
    rows = u.shape[0]
    ntc = seq // SCAN_T
    nchunks = nbatch * ntc * SSM_CHUNKS
    assert nchunks >= 2
    item_a = lambda k: jnp.minimum(k, nchunks - 1)
    item_b = lambda k: jnp.clip(k - 1, 0, nchunks - 1)
    item_c = lambda k: jnp.clip(k - 2, 0, nchunks - 1)
    chunk_block = lambda item: (item // SSM_CHUNKS, item % SSM_CHUNKS)
    y, hl = pl.pallas_call(
        functools.partial(_ssm_prompt_body, nchunks=nchunks, ntc=ntc),
        grid=(nchunks + 2,),
        in_specs=[pl.BlockSpec((SCAN_T, CH_IN), lambda k: chunk_block(item_a(k))),
                  pl.BlockSpec((SCAN_T, CH_IN), lambda k: chunk_block(item_c(k))),
                  ] + [_layer_spec(sp[n].shape[1:], layer) for n in ("B", "C", "lr", "li", "pr", "pi", "d")],
        out_specs=[pl.BlockSpec((SCAN_T, CH_IN), lambda k: chunk_block(item_c(k))),
                   pl.BlockSpec((1, SSM_CHUNKS, 2, CH_STATES),
                                lambda k: (item_b(k) // (ntc * SSM_CHUNKS), 0, 0, 0))],
        out_shape=[jax.ShapeDtypeStruct((rows, SSM_WIDTH), F32),
                   jax.ShapeDtypeStruct((nbatch, SSM_CHUNKS, 2, CH_STATES), F32)],
        scratch_shapes=[pltpu.VMEM((CH_IN // LANES, SCAN_T, LANES), F32),
                        pltpu.VMEM((SCAN_T, 2 * CH_STATES), F32), pltpu.VMEM((SCAN_T, 2 * CH_STATES), F32),
                        pltpu.VMEM((SCAN_T, 2 * CH_STATES), BF16), pltpu.VMEM((SCAN_T, 2 * CH_STATES), BF16),
                        pltpu.VMEM((CH_IN // LANES, SCAN_T, LANES), F32),
                        pltpu.VMEM((SSM_CHUNKS, 2, CH_STATES), F32)],
        compiler_params=_cparams(("arbitrary",)),
        name="ssm_prompt",
    )(u, u, sp["B"], sp["C"], sp["lr"], sp["li"], sp["pr"], sp["pi"], sp["d"])
    return y, hl


def _ssm_sample_body(u_ref, h0_ref, b_ref, c_ref, lr_ref, li_ref, d_ref, y_ref, ho_ref, bu_ref, *, nseq, dec_seq):
    nl = CH_STATES // LANES
    for k in range(SSM_CHUNKS):
        uk = u_ref[:, k * CH_IN:(k + 1) * CH_IN]
        bu = _dot_nt(uk.astype(BF16), b_ref[k])
        for c in range(2 * nl):
            bu_ref[c] = bu[:, c * LANES:(c + 1) * LANES]
        base = k * 2 * CH_STATES
        hr = h0_ref[:, base:base + CH_STATES]
        hi = h0_ref[:, base + CH_STATES:base + 2 * CH_STATES]
        ar = lr_ref[k]
        ai = li_ref[k]
        for t in range(dec_seq):
            br = jnp.concatenate([bu_ref[c, pl.ds(t, nseq, stride=dec_seq), :] for c in range(nl)], axis=1)
            bi = jnp.concatenate([bu_ref[nl + c, pl.ds(t, nseq, stride=dec_seq), :] for c in range(nl)], axis=1)
            hr, hi = ar * hr + (br - ai * hi), ar * hi + (bi + ai * hr)
            for c in range(nl):
                bu_ref[c, pl.ds(t, nseq, stride=dec_seq), :] = hr[:, c * LANES:(c + 1) * LANES]
                bu_ref[nl + c, pl.ds(t, nseq, stride=dec_seq), :] = hi[:, c * LANES:(c + 1) * LANES]
        ho_ref[:, base:base + CH_STATES] = hr
        ho_ref[:, base + CH_STATES:base + 2 * CH_STATES] = hi
        hall = jnp.concatenate([bu_ref[c] for c in range(2 * nl)], axis=1)
        y_ref[:, k * CH_IN:(k + 1) * CH_IN] = (
            _dot_nt(hall.astype(BF16), c_ref[k]) + d_ref[k] * uk)


def _ssm_sample(u, h0, sp, layer, nseq, dec_seq):
    rows = u.shape[0]
    return pl.pallas_call(
        functools.partial(_ssm_sample_body, nseq=nseq, dec_seq=dec_seq),
        grid=(1,),
        in_specs=[_full(u.shape), _full(h0.shape)] + [_layer_spec(sp[n].shape[1:], layer)
                                                      for n in ("B", "C", "lr", "li", "d")],
        out_specs=[_full((rows, SSM_WIDTH)), _full(h0.shape)],
        out_shape=[jax.ShapeDtypeStruct((rows, SSM_WIDTH), F32), jax.ShapeDtypeStruct(h0.shape, F32)],
        scratch_shapes=[pltpu.VMEM((2 * CH_STATES // LANES, rows, LANES), F32)],
        compiler_params=_cparams(("arbitrary",)),
        name="ssm_sample",
    )(u, h0, sp["B"], sp["C"], sp["lr"], sp["li"], sp["d"])


def _ssm_params(lambda_re, lambda_im, log_dt, b_re, b_im, c_re, c_im, d):
    lam = lax.complex(jnp.minimum(lambda_re.astype(F32), -1e-4), lambda_im.astype(F32))
    dt = jnp.exp(log_dt.astype(F32))[:, None]
    lam_dt = lam * dt
    lam_bar = jnp.exp(lam_dt)
    b = lax.complex(b_re.astype(F32), b_im.astype(F32))
    b_bar = ((lam_bar - 1.0) / lam)[..., None] * b
    rows_b = lambda part: part.reshape(SSM_CHUNKS, CH_STATES, SSM_GROUP)
    rows_c = lambda part: part.reshape(SSM_CHUNKS, CH_IN, SSM_STATE)
    pw = jnp.exp(lam_dt * float(SCAN_SEG)).reshape(SSM_CHUNKS, 1, CH_STATES)
    lb = lam_bar.reshape(SSM_CHUNKS, 1, CH_STATES)
    return dict(b_re=rows_b(jnp.real(b_bar)), b_im=rows_b(jnp.imag(b_bar)),
                c_re=rows_c(c_re.astype(F32)), c_im_neg=rows_c(-c_im.astype(F32)),
                lr=jnp.real(lb), li=jnp.imag(lb), pr=jnp.real(pw), pi=jnp.imag(pw),
                d=d.astype(F32).reshape(SSM_CHUNKS, 1, CH_IN))


def _ssm_pack_body(bre_ref, bim_ref, cre_ref, cimn_ref, eb_ref, ec_ref, b_out, c_out):
    def expand(rows_ref, onehot_ref, row_group, col_group):
        t = jnp.dot(rows_ref[...].astype(BF16), onehot_ref[...], preferred_element_type=F32)
        own = lax.broadcasted_iota(jnp.int32, t.shape, 0) // row_group
        col = lax.broadcasted_iota(jnp.int32, t.shape, 1) // col_group
        return jnp.where(own == col, t, 0.0).astype(BF16)

    b_out[0:CH_STATES, :] = expand(bre_ref, eb_ref, SSM_STATE, SSM_GROUP)
    b_out[CH_STATES:2 * CH_STATES, :] = expand(bim_ref, eb_ref, SSM_STATE, SSM_GROUP)
    c_out[:, 0:CH_STATES] = expand(cre_ref, ec_ref, SSM_GROUP, SSM_STATE)
    c_out[:, CH_STATES:2 * CH_STATES] = expand(cimn_ref, ec_ref, SSM_GROUP, SSM_STATE)


def _ssm_pack(sp):
    depth = sp["b_re"].shape[0]
    eb = jnp.asarray(np.tile(np.eye(SSM_GROUP, dtype=np.float32), (1, CH_GROUPS)), BF16)
    ec = jnp.asarray(np.tile(np.eye(SSM_STATE, dtype=np.float32), (1, CH_GROUPS)), BF16)
    item = lambda r, c: pl.BlockSpec((None, None, r, c), lambda l, k: (l, k, 0, 0))
    bmat, cmat = pl.pallas_call(
        _ssm_pack_body,
        grid=(depth, SSM_CHUNKS),
        in_specs=[item(CH_STATES, SSM_GROUP), item(CH_STATES, SSM_GROUP), item(CH_IN, SSM_STATE),
                  item(CH_IN, SSM_STATE), _full(eb.shape), _full(ec.shape)],
        out_specs=[item(2 * CH_STATES, CH_IN), item(CH_IN, 2 * CH_STATES)],
        out_shape=[jax.ShapeDtypeStruct((depth, SSM_CHUNKS, 2 * CH_STATES, CH_IN), BF16),
                   jax.ShapeDtypeStruct((depth, SSM_CHUNKS, CH_IN, 2 * CH_STATES), BF16)],
        compiler_params=_cparams(("arbitrary", "arbitrary")),
        name="ssm_pack",
    )(sp["b_re"], sp["b_im"], sp["c_re"], sp["c_im_neg"], eb, ec)
    return dict(sp, B=bmat, C=cmat)


def _pack_state(re, im):
    n = re.shape[0]
    st = jnp.stack([re.reshape(n, SSM_CHUNKS, CH_STATES), im.reshape(n, SSM_CHUNKS, CH_STATES)], axis=2)
    return st.reshape(n, SSM_CHUNKS * 2 * CH_STATES).astype(F32)


def _unpack_state(h, n):
    st = h.reshape(n, SSM_CHUNKS, 2, CH_GROUPS, SSM_STATE)
    return (st[:, :, 0].reshape(n, SSM_GROUPS, SSM_STATE), st[:, :, 1].reshape(n, SSM_GROUPS, SSM_STATE))


def _glu_body(y_ref, w_ref, g_ref, o_ref):
    a = jax.nn.gelu(y_ref[...])
    z = a * jax.nn.sigmoid(jnp.dot(a.astype(BF16), w_ref[...], preferred_element_type=F32))
    o_ref[...] = _rms(z, g_ref[...]).astype(BF16)


def _glu(y, w_bf, layer, g, tm):
    rows = y.shape[0]
    spec = pl.BlockSpec((tm, SSM_WIDTH), lambda i: (i, 0))
    return pl.pallas_call(
        _glu_body,
        grid=(rows // tm,),
        in_specs=[spec, _layer_spec((SSM_WIDTH, SSM_WIDTH), layer), _full((1, SSM_WIDTH))],
        out_specs=spec,
        out_shape=jax.ShapeDtypeStruct((rows, SSM_WIDTH), BF16),
        compiler_params=_cparams(("arbitrary",)),
        name="glu_rms",
    )(y, w_bf, g)


def _layer_norm(y, g, b):
    mu = jnp.mean(y, axis=-1, keepdims=True)
    yc = y - mu
    var = jnp.mean(jnp.square(yc), axis=-1, keepdims=True)
    return yc * lax.rsqrt(var + LN_EPS) * g + b


OUTPROJ_SUB = 256


def _outproj_body(*refs, aliased, nblk):
    if aliased:
        refs = refs[1:]
    a_ref, s_ref, x_ref, w_ref, g_ref, b_ref, wrc_ref, br_ref, o_ref, lg_ref = refs
    i = pl.program_id(0)

    @pl.when(i < nblk)
    def _():
        tm = a_ref.shape[0]
        sub = min(tm, OUTPROJ_SUB)
        for r0 in range(0, tm, sub):
            rows = slice(r0, r0 + sub)
            acc = jnp.dot(a_ref[rows, :], w_ref[0:ATTN_WIDTH, :], preferred_element_type=F32)
            acc = acc + jnp.dot(s_ref[rows, :], w_ref[ATTN_WIDTH:, :], preferred_element_type=F32)
            x1 = _layer_norm(DEEPNORM_ALPHA * x_ref[rows, :] + acc, g_ref[...], b_ref[...])
            o_ref[rows, :] = x1
            hh = jnp.dot(x1.astype(BF16), wrc_ref[...], preferred_element_type=F32)
            lg_ref[rows, :] = hh[:, 0:LANES] + hh[:, LANES:] + br_ref[...]

    @pl.when(i >= nblk)
    def _():
        o_ref[...] = jnp.zeros(o_ref.shape, F32)


def _outproj(an, sn, x, w_bf, layer, g, b, wr, br, tm, total_rows, row_block0, x1_buf=None):
    rows = an.shape[0]
    aliased = x1_buf is not None
    nblk = rows // tm
    tail = 0 if aliased or rows == total_rows else 1
    blk = lambda i: jnp.minimum(i, nblk - 1)
    half = lambda: pl.BlockSpec((tm, ATTN_WIDTH), lambda i: (blk(i), 0))
    in_specs = [half(), half(), pl.BlockSpec((tm, D_MODEL), lambda i: (blk(i), 0)),
                _layer_spec((D_MODEL, D_MODEL), layer), _full((1, D_MODEL)), _full((1, D_MODEL)),
                _full((D_MODEL, 2 * LANES)), _full((1, LANES))]
    wr_hi = wr.astype(BF16)
    wr_lo = (wr - wr_hi.astype(F32)).astype(BF16)
    args = [an, sn, x, w_bf, g, b, jnp.concatenate([wr_hi, wr_lo], axis=1), br]
    if aliased:
        in_specs = [pl.BlockSpec(memory_space=pl.ANY)] + in_specs
        args = [x1_buf] + args
    return pl.pallas_call(
        functools.partial(_outproj_body, aliased=aliased, nblk=nblk),
        grid=(nblk + tail,),
        in_specs=in_specs,
        out_specs=[pl.BlockSpec((tm, D_MODEL), lambda i: (i + row_block0, 0)),
                   pl.BlockSpec((tm, LANES), lambda i: (blk(i), 0))],
        out_shape=[jax.ShapeDtypeStruct((total_rows, D_MODEL), F32), jax.ShapeDtypeStruct((rows, LANES), F32)],
        input_output_aliases={0: 0} if aliased else {},
        compiler_params=_cparams(("arbitrary",)),
        name="outproj_ln1",
    )(*args)


ROUTE_T = 640
R_E0, R_E1, R_W0, R_W1, R_P0, R_P1 = range(6)


def _route_select(x):
    lane = lax.broadcasted_iota(jnp.int32, x.shape, 1)
    big = jnp.int32(1 << 20)
    neg = -jnp.inf
    gmask = lane < N_EXPERT_GROUPS
    gm = jnp.max(jnp.where(gmask, x, neg), axis=-1, keepdims=True)
    grp = jnp.min(jnp.where(gmask & (x == gm), lane, big), axis=-1, keepdims=True)
    p_grp = 1.0 / jnp.sum(jnp.where(gmask, jnp.exp(x - gm), 0.0), axis=-1, keepdims=True)
    lo = N_EXPERT_GROUPS + grp * EXPERTS_PER_GROUP
    emask = (lane >= lo) & (lane < lo + EXPERTS_PER_GROUP)
    v1 = jnp.max(jnp.where(emask, x, neg), axis=-1, keepdims=True)
    i1 = jnp.min(jnp.where(emask & (x == v1), lane, big), axis=-1, keepdims=True)
    emask2 = emask & (lane != i1)
    v2 = jnp.max(jnp.where(emask2, x, neg), axis=-1, keepdims=True)
    i2 = jnp.min(jnp.where(emask2 & (x == v2), lane, big), axis=-1, keepdims=True)
    e2 = jnp.exp(v2 - v1)
    w1 = p_grp / (1.0 + e2)
    w2 = p_grp * e2 / (1.0 + e2)
    return (i1 - N_EXPERT_GROUPS).astype(F32), (i2 - N_EXPERT_GROUPS).astype(F32), w1, w2


def _route_plan_body(lg_ref, ltri_ref, utri_ref, o_ref, meta_ref, rec_ref, run_ref):
    ph = pl.program_id(0)
    j = pl.program_id(1)
    tm = lg_ref.shape[0]
    lane = lax.broadcasted_iota(jnp.int32, (tm, LANES), 1)
    lanef = lane.astype(F32)
    r0 = pl.multiple_of(j * tm, tm)

    @pl.when((ph == 0) & (j == 0))
    def _():
        run_ref[...] = jnp.zeros(run_ref.shape, F32)

    @pl.when(ph == 0)
    def _():
        e0, e1, w0, w1 = _route_select(lg_ref[...])
        oh0 = (lanef == e0).astype(F32)
        oh1 = (lanef == e1).astype(F32)
        oh2 = oh0 + oh1
        before = jnp.dot(ltri_ref[...], oh2.astype(BF16), preferred_element_type=F32) + run_ref[0:1, :]
        rk0 = jnp.sum(before * oh0, axis=-1, keepdims=True)
        rk1 = jnp.sum(before * oh1, axis=-1, keepdims=True)
        run_ref[...] = run_ref[...] + jnp.sum(oh2, axis=0, keepdims=True)
        rec = jnp.where(lane == R_E0, e0, 0.0)
        rec = jnp.where(lane == R_E1, e1, rec)
        rec = jnp.where(lane == R_W0, w0, rec)
        rec = jnp.where(lane == R_W1, w1, rec)
        rec = jnp.where(lane == R_P0, rk0, rec)
        rec = jnp.where(lane == R_P1, rk1, rec)
        rec_ref[pl.ds(r0, tm), :] = rec

    @pl.when(ph == 1)
    def _():
        counts = run_ref[...]
        nblk = jnp.floor((counts + (MOE_BLOCK - 1)) * (1.0 / MOE_BLOCK))
        end_blk = jnp.dot(nblk.astype(BF16), utri_ref[...], preferred_element_type=F32)
        start_row = (end_blk - nblk) * MOE_BLOCK
        rec = rec_ref[pl.ds(r0, tm), :]
        oh0 = (lanef == rec[:, R_E0:R_E0 + 1]).astype(F32)
        oh1 = (lanef == rec[:, R_E1:R_E1 + 1]).astype(F32)
        p0 = jnp.sum(oh0 * start_row[0:1, :], axis=-1, keepdims=True) + rec[:, R_P0:R_P0 + 1]
        p1 = jnp.sum(oh1 * start_row[0:1, :], axis=-1, keepdims=True) + rec[:, R_P1:R_P1 + 1]
        out = jnp.where(lane == R_P0, p0, rec)
        out = jnp.where(lane == R_P1, p1, out)
        o_ref[...] = out

        @pl.when(j == 0)
        def _():
            mrow = lax.broadcasted_iota(jnp.int32, meta_ref.shape, 0)
            meta_ref[...] = jnp.where(mrow == 1, counts, end_blk)


def _route_plan(logits):
    rows = logits.shape[0]
    tm = ROUTE_T
    ltri = jnp.asarray(np.tril(np.ones((tm, tm), np.float32), -1), BF16)
    utri = jnp.asarray(np.triu(np.ones((LANES, LANES), np.float32)), BF16)
    mrows = SUBLANES
    return pl.pallas_call(
        _route_plan_body,
        grid=(2, rows // tm),
        in_specs=[pl.BlockSpec((tm, LANES), lambda p, j: (j * (1 - p), 0)),
                  pl.BlockSpec((tm, tm), lambda p, j: (0, 0)),
                  pl.BlockSpec((LANES, LANES), lambda p, j: (0, 0))],
        out_specs=[pl.BlockSpec((tm, LANES), lambda p, j: (j * p, 0)),
                   pl.BlockSpec((mrows, LANES), lambda p, j: (0, 0))],
        out_shape=[jax.ShapeDtypeStruct((rows, LANES), F32), jax.ShapeDtypeStruct((mrows, LANES), F32)],
        scratch_shapes=[pltpu.VMEM((rows, LANES), F32), pltpu.VMEM((SUBLANES, LANES), F32)],
        compiler_params=_cparams(("arbitrary", "arbitrary")),
        name="route_plan",
    )(logits, ltri, utri)


U32 = jnp.uint32
PACKED_W = D_MODEL // 2


def _pack_bf16_pairs(y):
    bits = lax.bitcast_convert_type(y.astype(BF16).astype(F32), U32)
    return bits[:, 0:PACKED_W] | (bits[:, PACKED_W:] >> 16)


def _unpack_bf16_pairs(p):
    hi = lax.bitcast_convert_type(p & jnp.uint32(0xFFFF0000), F32)
    lo = lax.bitcast_convert_type(p << 16, F32)
    return jnp.concatenate([hi, lo], axis=1)


def _build_rowmap(pos_ref, bend_ref, cnt_ref, o_ref):
    unroll = 8
    ntok = pos_ref.shape[0] // 2

    def zero_padding(e, _):
        first_blk = jnp.where(e == 0, 0, bend_ref[jnp.maximum(e - 1, 0)])

        def zero(r, _):
            o_ref[r] = 0
            return 0
        lax.fori_loop(first_blk * MOE_BLOCK + cnt_ref[e], bend_ref[e] * MOE_BLOCK, zero, 0)
        return 0
    lax.fori_loop(0, N_EXPERTS, zero_padding, 0)

    def fill(c, _):
        for u in range(unroll):
            t = c * unroll + u
            o_ref[pos_ref[2 * t]] = t
            o_ref[pos_ref[2 * t + 1]] = t
        return 0
    lax.fori_loop(0, ntok // unroll, fill, 0)


GATHER_SLOTS = 4


def _moe_body(bend_ref, cnt_ref, pos_ref, x_hbm, wg_ref, wu_ref, wd_ref, ys_hbm, *scratch, nblocks):
    xbufs = scratch[:GATHER_SLOTS]
    ybufs = scratch[GATHER_SLOTS:GATHER_SLOTS + 2]
    gsem, osem, wg_bf, wu_bf, wd_bf, row_tok_ref = scratch[GATHER_SLOTS + 2:]
    ahead = GATHER_SLOTS - 1
    e = pl.program_id(0)
    nused = bend_ref[N_EXPERTS - 1]
    b1 = bend_ref[e]
    b0 = jnp.where(e == 0, 0, bend_ref[jnp.maximum(e - 1, 0)])

    def gather_start(blk, s):
        base = blk * MOE_BLOCK
        for r in range(MOE_BLOCK):
            tok = row_tok_ref[base + r]
            pltpu.make_async_copy(x_hbm.at[pl.ds(tok, 1), :], xbufs[s].at[pl.ds(r, 1), :],
                                  gsem.at[s]).start(priority=r % 2)

    def gather_wait(s):
        pltpu.make_async_copy(x_hbm.at[pl.ds(0, MOE_BLOCK), :], xbufs[s], gsem.at[s]).wait()

    def out_copy(blk, s):
        row0 = pl.multiple_of(blk * MOE_BLOCK, MOE_BLOCK)
        return pltpu.make_async_copy(ybufs[s], ys_hbm.at[pl.ds(row0, MOE_BLOCK), :], osem.at[s])

    @pl.when(e == 0)
    def _():
        _build_rowmap(pos_ref, bend_ref, cnt_ref, row_tok_ref)
        for k in range(ahead):
            gather_start(k, k)

    @pl.when(b1 > b0)
    def _():
        wg_bf[...] = wg_ref[...].astype(BF16)
        wu_bf[...] = wu_ref[...].astype(BF16)
        wd_bf[...] = wd_ref[...].astype(BF16)

        def do_block(b, s):
            so = s % 2
            gather_wait(s)

            @pl.when(b >= 2)
            def _():
                out_copy(b - 2, so).wait()

            gather_start(jnp.minimum(b + ahead, nused - 1), (s + ahead) % GATHER_SLOTS)
            x = xbufs[s][...].astype(BF16)
            hg = jnp.dot(x, wg_bf[...], preferred_element_type=F32)
            hu = jnp.dot(x, wu_bf[...], preferred_element_type=F32)
            h = (jax.nn.silu(hg) * hu).astype(BF16)
            ybufs[so][...] = _pack_bf16_pairs(jnp.dot(h, wd_bf[...], preferred_element_type=F32))
            out_copy(b, so).start()

        def body(b, _):
            for s in range(GATHER_SLOTS):
                @pl.when(b % GATHER_SLOTS == s)
                def _():
                    do_block(b, s)
            return 0

        lax.fori_loop(b0, b1, body, 0)

    @pl.when(e == N_EXPERTS - 1)
    def _():
        last = nused - 1
        for s in range(GATHER_SLOTS):
            @pl.when(last % GATHER_SLOTS == s)
            def _():
                for k in range(1, GATHER_SLOTS):
                    gather_wait((s + k) % GATHER_SLOTS)
        out_copy(0, 0).wait()
        out_copy(0, 1).wait()
        yb0 = ybufs[0]
        yb0[...] = jnp.zeros(yb0.shape, U32)

        def zstart(b, _):
            out_copy(b, 0).start()
            return 0

        def zwait(b, _):
            out_copy(b, 0).wait()
            return 0

        lax.fori_loop(nused, nblocks, zstart, 0)
        lax.fori_loop(nused, nblocks, zwait, 0)


def _moe(bend, cnt, pos, x1, w_gate, w_up, w_down, layer, nblocks):
    assert 2 * x1.shape[0] > GATHER_SLOTS * MOE_BLOCK
    wspec = lambda a, b: pl.BlockSpec((None, None, a, b), lambda e, be, cn, ps: (layer, e, 0, 0))
    grid_spec = pltpu.PrefetchScalarGridSpec(
        num_scalar_prefetch=3,
        grid=(N_EXPERTS,),
        in_specs=[pl.BlockSpec(memory_space=pl.ANY), wspec(D_MODEL, EXPERT_FF), wspec(D_MODEL, EXPERT_FF),
                  wspec(EXPERT_FF, D_MODEL)],
        out_specs=pl.BlockSpec(memory_space=pl.ANY),
        scratch_shapes=([pltpu.VMEM((MOE_BLOCK, D_MODEL), F32) for _ in range(GATHER_SLOTS)]
                        + [pltpu.VMEM((MOE_BLOCK, PACKED_W), U32) for _ in range(2)]
                        + [pltpu.SemaphoreType.DMA((GATHER_SLOTS,)), pltpu.SemaphoreType.DMA((2,)),
                           pltpu.VMEM((D_MODEL, EXPERT_FF), BF16),
                           pltpu.VMEM((D_MODEL, EXPERT_FF), BF16),
                           pltpu.VMEM((EXPERT_FF, D_MODEL), BF16),
                           pltpu.SMEM((nblocks * MOE_BLOCK,), jnp.int32)]),
    )
    return pl.pallas_call(
        functools.partial(_moe_body, nblocks=nblocks),
        grid_spec=grid_spec,
        out_shape=jax.ShapeDtypeStruct((nblocks * MOE_BLOCK, PACKED_W), U32),
        compiler_params=_cparams(("arbitrary",), bounds_checks=False),
        name="moe_experts",
    )(bend, cnt, pos, x1, w_gate, w_up, w_down)


COMB_T = 128


def _combine_body(pos_ref, ys_hbm, x_ref, rt_ref, g_ref, b_ref, op_ref, os_ref, *scratch, n_prompt_blocks):
    i = pl.program_id(0)
    nsteps = pl.num_programs(0)
    bufs = tuple((scratch[2 * s], scratch[2 * s + 1]) for s in range(GATHER_SLOTS))
    sem = scratch[2 * GATHER_SLOTS]
    ahead = GATHER_SLOTS - 1

    def gather_start(step, s):
        base = step * (2 * COMB_T)
        for r in range(COMB_T):
            for k in range(2):
                p = pos_ref[base + 2 * r + k]
                pltpu.make_async_copy(ys_hbm.at[pl.ds(p, 1), :], bufs[s][k].at[pl.ds(r, 1), :],
                                      sem.at[s]).start(priority=k)

    def gather_wait(s):
        for k in range(2):
            pltpu.make_async_copy(ys_hbm.at[pl.ds(0, COMB_T), :], bufs[s][k], sem.at[s]).wait()

    @pl.when(i == 0)
    def _():
        for k in range(ahead):
            gather_start(k, k)

    def do_tile(s):
        gather_wait(s)
        gather_start(jnp.minimum(i + ahead, nsteps - 1), (s + ahead) % GATHER_SLOTS)
        rt = rt_ref[...]
        ff = (rt[:, R_W0:R_W0 + 1] * _unpack_bf16_pairs(bufs[s][0][...])
              + rt[:, R_W1:R_W1 + 1] * _unpack_bf16_pairs(bufs[s][1][...]))
        out = _layer_norm(DEEPNORM_ALPHA * x_ref[...] + ff, g_ref[...], b_ref[...])

        @pl.when(i < n_prompt_blocks)
        def _():
            op_ref[...] = out

        @pl.when(i >= n_prompt_blocks)
        def _():
            os_ref[...] = out

        @pl.when(i == nsteps - 1)
        def _():
            for k in range(1, GATHER_SLOTS):
                gather_wait((s + k) % GATHER_SLOTS)

    for s in range(GATHER_SLOTS):
        @pl.when(i % GATHER_SLOTS == s)
        def _():
            do_tile(s)


def _combine(pos, ys, x1, route, g, b, n_prompt_rows):
    rows = x1.shape[0]
    npb = n_prompt_rows // COMB_T
    assert rows // COMB_T >= GATHER_SLOTS
    grid_spec = pltpu.PrefetchScalarGridSpec(
        num_scalar_prefetch=1,
        grid=(rows // COMB_T,),
        in_specs=[pl.BlockSpec(memory_space=pl.ANY),
                  pl.BlockSpec((COMB_T, D_MODEL), lambda i, p: (i, 0)),
                  pl.BlockSpec((COMB_T, LANES), lambda i, p: (i, 0)),
                  pl.BlockSpec((1, D_MODEL), lambda i, p: (0, 0)),
                  pl.BlockSpec((1, D_MODEL), lambda i, p: (0, 0))],
        out_specs=[pl.BlockSpec((COMB_T, D_MODEL), lambda i, p: (jnp.minimum(i, npb - 1), 0)),
                   pl.BlockSpec((COMB_T, D_MODEL), lambda i, p: (jnp.maximum(i - npb, 0), 0))],
        scratch_shapes=([pltpu.VMEM((COMB_T, PACKED_W), U32) for _ in range(2 * GATHER_SLOTS)]
                        + [pltpu.SemaphoreType.DMA((GATHER_SLOTS,))]),
    )
    return pl.pallas_call(
        functools.partial(_combine_body, n_prompt_blocks=npb),
        grid_spec=grid_spec,
        out_shape=[jax.ShapeDtypeStruct((n_prompt_rows, D_MODEL), F32),
                   jax.ShapeDtypeStruct((rows - n_prompt_rows, D_MODEL), F32)],
        compiler_params=_cparams(("arbitrary",), bounds_checks=False),
        name="combine_ln2",
    )(pos, ys, x1, route, g, b)


def kernel(x_prompt, x_sample, cache_k, cache_v, state_ssm_re, state_ssm_im, w_in, ssm_lambda_re, ssm_lambda_im, ssm_log_dt, ssm_b_re, ssm_b_im, ssm_c_re, ssm_c_im, ssm_d, ssm_w_glu, attn_sinks, attn_norm_g, ssm_norm_g, w_out, ln1_g, ln1_b, router_group_w, router_group_b, router_expert_w, router_expert_b, expert_w_gate, expert_w_up, expert_w_down, ln2_g, ln2_b):
    nbatch, seq, _ = x_prompt.shape
    nseq, dec_seq, _ = x_sample.shape
    wbuf = cache_k.shape[2]
    rows_p = nbatch * seq
    rows_s = nseq * dec_seq
    rows_all = rows_p + rows_s
    tm_p = 512

    xp = x_prompt.reshape(rows_p, D_MODEL)
    xs = x_sample.reshape(rows_s, D_MODEL)
    tab_p = _rope_tables(np.arange(seq))
    tab_s = _rope_tables(PAST_LEN + (np.arange(rows_s) % dec_seq))
    ck_all = cache_k.astype(F32).reshape(DEPTH, nseq, wbuf, KV_WIDTH)
    cv_all = cache_v.astype(F32).reshape(DEPTH, nseq, wbuf, KV_WIDTH)

    sp = _ssm_pack(jax.vmap(_ssm_params)(ssm_lambda_re, ssm_lambda_im, ssm_log_dt, ssm_b_re, ssm_b_im,
                                         ssm_c_re, ssm_c_im, ssm_d))
    w_in_bf = w_in.astype(BF16)
    w_glu_bf = ssm_w_glu.astype(BF16)
    w_out_bf = w_out.astype(BF16)

    outs = {k: [] for k in ("kp", "vp", "hrp", "hip", "ks", "vs", "hrs", "his")}
    for l in range(DEPTH):
        sinks = attn_sinks[l].astype(F32)
        attn_g = attn_norm_g[l].astype(F32).reshape(1, ATTN_WIDTH)
        ssm_g = ssm_norm_g[l].astype(F32).reshape(1, SSM_WIDTH)
        g1 = ln1_g[l].astype(F32).reshape(1, D_MODEL)
        b1 = ln1_b[l].astype(F32).reshape(1, D_MODEL)
        g2 = ln2_g[l].astype(F32).reshape(1, D_MODEL)
        b2 = ln2_b[l].astype(F32).reshape(1, D_MODEL)
        n_rt = N_EXPERT_GROUPS + N_EXPERTS
        wr = jnp.concatenate([router_group_w[l].astype(F32)]
                             + [router_expert_w[l, g].astype(F32) for g in range(N_EXPERT_GROUPS)]
                             + [jnp.zeros((D_MODEL, LANES - n_rt), F32)], axis=1)
        br = jnp.concatenate([router_group_b[l].astype(F32), router_expert_b[l].astype(F32).reshape(-1),
                              jnp.zeros((LANES - n_rt,), F32)]).reshape(1, LANES)

        q_p, k_p, v_p, u_p = _inproj(xp, w_in_bf, l, *tab_p, tm=tm_p)
        q_s, k_s, v_s, u_s = _inproj(xs, w_in_bf, l, *tab_s, tm=rows_s)

        an_p = _attn_prompt(sinks, q_p, k_p, v_p, attn_g, seq)
        an_s, nk_s, nv_s = _attn_sample(sinks, q_s, k_s, v_s, ck_all, cv_all, attn_g, dec_seq, l)

        y_p, hl_p = _ssm_prompt(u_p, sp, l, nbatch, seq)
        y_s, hl_s = _ssm_sample(u_s, _pack_state(state_ssm_re[l], state_ssm_im[l]), sp, l, nseq, dec_seq)
        sn_p = _glu(y_p, w_glu_bf, l, ssm_g, 2 * tm_p)
        sn_s = _glu(y_s, w_glu_bf, l, ssm_g, rows_s)

        x1, lg_p = _outproj(an_p, sn_p, xp, w_out_bf, l, g1, b1, wr, br, tm_p, rows_all, 0)
        x1, lg_s = _outproj(an_s, sn_s, xs, w_out_bf, l, g1, b1, wr, br, rows_s, rows_all, rows_p // rows_s,
                            x1_buf=x1)
        nblocks = 2 * rows_all // MOE_BLOCK + N_EXPERTS
        route, meta = _route_plan(jnp.concatenate([lg_p, lg_s], axis=0))
        pos = route[:, R_P0:R_P1 + 1].astype(jnp.int32).reshape(2 * rows_all)
        bend = meta[0, :N_EXPERTS].astype(jnp.int32)

        cnt = meta[1, :N_EXPERTS].astype(jnp.int32)
        ys = _moe(bend, cnt, pos, x1, expert_w_gate, expert_w_up, expert_w_down, l, nblocks)
        xp, xs = _combine(pos, ys, x1, route, g2, b2, rows_p)

        wp = min(WINDOW, seq)
        tail = lambda a: a.reshape(nbatch, seq, KV_WIDTH)[:, seq - wp:].reshape(nbatch, wp, N_KV_HEADS, HEAD_DIM)
        outs["kp"].append(tail(k_p))
        outs["vp"].append(tail(v_p))
        hp = hl_p.reshape(nbatch, SSM_CHUNKS * 2 * CH_STATES)
        hr, hi = _unpack_state(hp, nbatch)
        outs["hrp"].append(hr)
        outs["hip"].append(hi)
        outs["ks"].append(nk_s.reshape(nseq, wbuf, N_KV_HEADS, HEAD_DIM))
        outs["vs"].append(nv_s.reshape(nseq, wbuf, N_KV_HEADS, HEAD_DIM))
        hr, hi = _unpack_state(hl_s, nseq)
        outs["hrs"].append(hr)
        outs["his"].append(hi)

    st = {k: jnp.stack(v) for k, v in outs.items()}
    return (xp.reshape(nbatch, seq, D_MODEL), xs.reshape(nseq, dec_seq, D_MODEL),
            st["kp"], st["vp"], st["hrp"], st["hip"], st["ks"], st["vs"], st["hrs"], st["his"])
```

```python
import functools

import numpy as np
import jax
import jax.numpy as jnp
from jax import lax
from jax.experimental import pallas as pl
from jax.experimental.pallas import tpu as pltpu

F32 = jnp.float32
BF16 = jnp.bfloat16

D_MODEL = 2048
DEPTH = 2
PAST_LEN = 16384
ATTN_WIDTH = 1024
SSM_WIDTH = 1024
HEAD_DIM = 64
N_HEADS = 16
N_KV_HEADS = 4
Q_PER_KV = 4
KV_WIDTH = 256
WINDOW = 128
ROPE_THETA = 10000.0
SSM_GROUP = 16
SSM_GROUPS = 64
SSM_STATE = 64
N_EXPERT_GROUPS = 4
EXPERTS_PER_GROUP = 8
N_EXPERTS = 32
EXPERT_FF = 512
MOE_BLOCK = 128
DEEPNORM_ALPHA = (2.0 * DEPTH) ** 0.25
LN_EPS = 1e-5
RMS_EPS = 1e-6

LANES = 128
SUBLANES = 8
VMEM_LIMIT = 56 * 1024 * 1024

SSM_CHUNKS = 4
CH_GROUPS = SSM_GROUPS // SSM_CHUNKS
CH_STATES = CH_GROUPS * SSM_STATE
CH_IN = CH_GROUPS * SSM_GROUP
SCAN_T = 512
SCAN_SEG = SCAN_T // SUBLANES


def _cparams(sem=None, vmem=VMEM_LIMIT, bounds_checks=True):
    return pltpu.CompilerParams(dimension_semantics=sem, vmem_limit_bytes=vmem,
                                disable_bounds_checks=not bounds_checks)


def _full(shape):
    n = len(shape)
    return pl.BlockSpec(shape, lambda *a: (0,) * n)


def _dot_nt(a, b):
    return lax.dot_general(a, b, (((1,), (1,)), ((), ())), preferred_element_type=F32)


def _inproj_body(x_ref, w_ref, cos_ref, sa_ref, sb_ref, q_ref, k_ref, v_ref, u_ref):
    x = x_ref[...].astype(BF16)
    cos = cos_ref[...]
    sa = sa_ref[...]
    sb = sb_ref[...]

    half = HEAD_DIM // 2

    def rope(c):
        return c * cos + pltpu.roll(c, LANES - half, 1) * sa + pltpu.roll(c, half, 1) * sb

    pq = jnp.dot(x, w_ref[:, 0:ATTN_WIDTH], preferred_element_type=F32)
    for j in range(ATTN_WIDTH // LANES):
        q_ref[:, j * LANES:(j + 1) * LANES] = rope(pq[:, j * LANES:(j + 1) * LANES]).astype(BF16)
    pk = jnp.dot(x, w_ref[:, ATTN_WIDTH:ATTN_WIDTH + KV_WIDTH], preferred_element_type=F32)
    for j in range(KV_WIDTH // LANES):
        k_ref[:, j * LANES:(j + 1) * LANES] = rope(pk[:, j * LANES:(j + 1) * LANES])
    v_ref[...] = jnp.dot(x, w_ref[:, ATTN_WIDTH + KV_WIDTH:ATTN_WIDTH + 2 * KV_WIDTH],
                         preferred_element_type=F32)
    u_ref[...] = jnp.dot(x, w_ref[:, ATTN_WIDTH + 2 * KV_WIDTH:], preferred_element_type=F32)


def _layer_spec(shape, layer):
    return pl.BlockSpec((None,) + tuple(shape), lambda *a: (layer,) + (0,) * len(shape))


def _inproj(x, w_bf, layer, cos_t, sa_t, sb_t, tm):
    rows = x.shape[0]
    tab_blocks = cos_t.shape[0] // tm
    in_w = w_bf.shape[2]
    row_spec = lambda w: pl.BlockSpec((tm, w), lambda i: (i, 0))
    tab_spec = pl.BlockSpec((tm, LANES), lambda i: (i % tab_blocks, 0))
    return pl.pallas_call(
        _inproj_body,
        grid=(rows // tm,),
        in_specs=[row_spec(D_MODEL), _layer_spec((D_MODEL, in_w), layer), tab_spec, tab_spec, tab_spec],
        out_specs=[row_spec(ATTN_WIDTH), row_spec(KV_WIDTH), row_spec(KV_WIDTH), row_spec(SSM_WIDTH)],
        out_shape=[jax.ShapeDtypeStruct((rows, ATTN_WIDTH), BF16),
                   jax.ShapeDtypeStruct((rows, KV_WIDTH), F32),
                   jax.ShapeDtypeStruct((rows, KV_WIDTH), F32),
                   jax.ShapeDtypeStruct((rows, SSM_WIDTH), F32)],
        compiler_params=_cparams(("arbitrary",)),
        name="inproj_rope",
    )(x, w_bf, cos_t, sa_t, sb_t)


def _rope_tables(positions):
    half = HEAD_DIM // 2
    inv_freq = ROPE_THETA ** (-np.arange(half, dtype=np.float64) / half)
    ang = positions.astype(np.float64)[:, None] * inv_freq[None, :]
    lane = np.arange(LANES)
    cos = np.cos(ang)[:, lane % half]
    sin = np.sin(ang)[:, lane % half]
    first = (lane % HEAD_DIM) < half
    sa = np.where(first[None, :], -sin, 0.0)
    sb = np.where(first[None, :], 0.0, sin)
    return (jnp.asarray(cos, F32), jnp.asarray(sa, F32), jnp.asarray(sb, F32))


ATTN_BLOCKS = 8


def _rms(x, g):
    return x * lax.rsqrt(jnp.mean(jnp.square(x), axis=-1, keepdims=True) + RMS_EPS) * g


def _attn_prompt_body(sink_ref, q_ref, kp_ref, kc_ref, vp_ref, vc_ref, g_ref, o_ref, acc_ref, *, blocks_per_seq):
    i = pl.program_id(0)
    first_has_prev = ((i * ATTN_BLOCKS) % blocks_per_seq) > 0
    kk = jnp.concatenate([kp_ref[...], kc_ref[...]], axis=0).astype(BF16)
    vt = jnp.concatenate([vp_ref[...], vc_ref[...]], axis=0).T.astype(BF16)
    ncol = Q_PER_KV * WINDOW
    key = lax.broadcasted_iota(jnp.int32, (2 * WINDOW, ncol), 0)
    col = lax.broadcasted_iota(jnp.int32, (2 * WINDOW, ncol), 1)
    qry = col % WINDOW
    in_band = (key > qry) & (key <= qry + WINDOW)
    head = lax.broadcasted_iota(jnp.int32, (1, ncol), 1) // WINDOW
    q = q_ref[...] * (HEAD_DIM ** -0.5)
    scores = {}
    for b in range(ATTN_BLOCKS):
        for g in range(N_KV_HEADS):
            qg = jnp.concatenate([q[b * WINDOW:(b + 1) * WINDOW,
                                    (g * Q_PER_KV + j) * HEAD_DIM:(g * Q_PER_KV + j + 1) * HEAD_DIM]
                                  for j in range(Q_PER_KV)], axis=0)
            kh = kk[b * WINDOW:(b + 2) * WINDOW, g * HEAD_DIM:(g + 1) * HEAD_DIM]
            scores[b, g] = lax.dot_general(kh, qg, (((1,), (1,)), ((), ())), preferred_element_type=F32)
    for b in range(ATTN_BLOCKS):
        visible = in_band & ((key >= WINDOW) | first_has_prev) if b == 0 else in_band
        for g in range(N_KV_HEADS):
            st = jnp.where(visible, scores[b, g], -jnp.inf)
            sink = jnp.zeros((1, ncol), F32)
            for j in range(Q_PER_KV):
                sink = jnp.where(head == j, sink_ref[g * Q_PER_KV + j], sink)
            m = jnp.maximum(jnp.max(st, axis=0, keepdims=True), sink)
            p = jnp.exp(st - m)
            denom = jnp.sum(p, axis=0, keepdims=True) + jnp.exp(sink - m)
            ot = jnp.dot(vt[g * HEAD_DIM:(g + 1) * HEAD_DIM, b * WINDOW:(b + 2) * WINDOW], p.astype(BF16),
                         preferred_element_type=F32)
            ot = ot / denom
            for j in range(Q_PER_KV):
                h = g * Q_PER_KV + j
                acc_ref[h * HEAD_DIM:(h + 1) * HEAD_DIM, b * WINDOW:(b + 1) * WINDOW] = (
                    ot[:, j * WINDOW:(j + 1) * WINDOW])
    o_ref[...] = _rms(acc_ref[...].T, g_ref[...]).astype(BF16)


def _attn_prompt(sinks, q, k, v, g, seq):
    rows = q.shape[0]
    bps = seq // WINDOW
    assert bps % ATTN_BLOCKS == 0
    step_rows = ATTN_BLOCKS * WINDOW
    cur = lambda w: pl.BlockSpec((step_rows, w), lambda i: (i, 0))
    prev = lambda w: pl.BlockSpec((WINDOW, w), lambda i: (jnp.maximum(i * ATTN_BLOCKS - 1, 0), 0))
    return pl.pallas_call(
        functools.partial(_attn_prompt_body, blocks_per_seq=bps),
        grid=(rows // step_rows,),
        in_specs=[pl.BlockSpec(memory_space=pltpu.SMEM), cur(ATTN_WIDTH), prev(KV_WIDTH), cur(KV_WIDTH),
                  prev(KV_WIDTH), cur(KV_WIDTH), _full((1, ATTN_WIDTH))],
        out_specs=cur(ATTN_WIDTH),
        out_shape=jax.ShapeDtypeStruct((rows, ATTN_WIDTH), BF16),
        scratch_shapes=[pltpu.VMEM((ATTN_WIDTH, step_rows), F32)],
        compiler_params=_cparams(("arbitrary",)),
        name="attn_prompt",
    )(sinks, q, k, k, v, v, g)


SEQ_PER_STEP = 8


def _attn_sample_body(sink_ref, q_ref, kn_ref, vn_ref, ck_ref, cv_ref, g_ref, o_ref, nk_ref, nv_ref, acc_ref,
                      *, dec_seq):
    nq = SEQ_PER_STEP * dec_seq
    wb = ck_ref.shape[1]
    nkc = SEQ_PER_STEP * wb
    ck = ck_ref[...].reshape(nkc, KV_WIDTH)
    cv = cv_ref[...].reshape(nkc, KV_WIDTH)
    kn = kn_ref[...]
    vn = vn_ref[...]
    ckb = ck.astype(BF16)
    cvb = cv.astype(BF16)
    knb = kn.astype(BF16)
    vnb = vn.astype(BF16)
    q = q_ref[...]

    rows = Q_PER_KV * nq
    r = lax.broadcasted_iota(jnp.int32, (rows, nkc), 0) % nq
    c = lax.broadcasted_iota(jnp.int32, (rows, nkc), 1)
    vis_c = ((r // dec_seq) == (c // wb)) & ((c % wb) >= (r % dec_seq) + 1 + (wb - WINDOW))
    rn = lax.broadcasted_iota(jnp.int32, (rows, nq), 0) % nq
    cn = lax.broadcasted_iota(jnp.int32, (rows, nq), 1)
    vis_n = ((rn // dec_seq) == (cn // dec_seq)) & ((cn % dec_seq) <= (rn % dec_seq))
    hrow = lax.broadcasted_iota(jnp.int32, (rows, 1), 0) // nq

    dn = (((1,), (1,)), ((), ()))
    for g in range(N_KV_HEADS):
        qs = jnp.concatenate([q[:, (g * Q_PER_KV + j) * HEAD_DIM:(g * Q_PER_KV + j + 1) * HEAD_DIM]
                              for j in range(Q_PER_KV)], axis=0)
        sl = slice(g * HEAD_DIM, (g + 1) * HEAD_DIM)
        sc = lax.dot_general(qs, ckb[:, sl], dn, preferred_element_type=F32) * (HEAD_DIM ** -0.5)
        sn = lax.dot_general(qs, knb[:, sl], dn, preferred_element_type=F32) * (HEAD_DIM ** -0.5)
        sc = jnp.where(vis_c, sc, -jnp.inf)
        sn = jnp.where(vis_n, sn, -jnp.inf)
        sink = jnp.zeros((rows, 1), F32)
        for j in range(Q_PER_KV):
            sink = jnp.where(hrow == j, sink_ref[g * Q_PER_KV + j], sink)
        m = jnp.maximum(jnp.maximum(jnp.max(sc, axis=-1, keepdims=True), jnp.max(sn, axis=-1, keepdims=True)), sink)
        pc = jnp.exp(sc - m)
        pn = jnp.exp(sn - m)
        denom = jnp.sum(pc, axis=-1, keepdims=True) + jnp.sum(pn, axis=-1, keepdims=True) + jnp.exp(sink - m)
        o = (jnp.dot(pc.astype(BF16), cvb[:, sl], preferred_element_type=F32)
             + jnp.dot(pn.astype(BF16), vnb[:, sl], preferred_element_type=F32)) / denom
        for j in range(Q_PER_KV):
            h = g * Q_PER_KV + j
            acc_ref[:, h * HEAD_DIM:(h + 1) * HEAD_DIM] = o[j * nq:(j + 1) * nq, :]
    o_ref[...] = _rms(acc_ref[...], g_ref[...]).astype(BF16)

    keep = wb - dec_seq
    for s in range(SEQ_PER_STEP):
        nk_ref[s, 0:keep, :] = ck_ref[s, dec_seq:wb, :]
        nv_ref[s, 0:keep, :] = cv_ref[s, dec_seq:wb, :]
        nk_ref[s, keep:wb, :] = kn[s * dec_seq:(s + 1) * dec_seq, :]
        nv_ref[s, keep:wb, :] = vn[s * dec_seq:(s + 1) * dec_seq, :]


def _attn_sample(sinks, q, kn, vn, ck, cv, g, dec_seq, layer):
    rows = q.shape[0]
    _, nseq, wb, _ = ck.shape
    nq = SEQ_PER_STEP * dec_seq
    rowb = lambda w: pl.BlockSpec((nq, w), lambda i: (i, 0))
    cin = pl.BlockSpec((None, SEQ_PER_STEP, wb, KV_WIDTH), lambda i: (layer, i, 0, 0))
    cb = pl.BlockSpec((SEQ_PER_STEP, wb, KV_WIDTH), lambda i: (i, 0, 0))
    return pl.pallas_call(
        functools.partial(_attn_sample_body, dec_seq=dec_seq),
        grid=(nseq // SEQ_PER_STEP,),
        in_specs=[pl.BlockSpec(memory_space=pltpu.SMEM), rowb(ATTN_WIDTH), rowb(KV_WIDTH), rowb(KV_WIDTH),
                  cin, cin, _full((1, ATTN_WIDTH))],
        out_specs=[rowb(ATTN_WIDTH), cb, cb],
        out_shape=[jax.ShapeDtypeStruct((rows, ATTN_WIDTH), BF16),
                   jax.ShapeDtypeStruct(ck.shape[1:], F32), jax.ShapeDtypeStruct(cv.shape[1:], F32)],
        scratch_shapes=[pltpu.VMEM((nq, ATTN_WIDTH), F32)],
        compiler_params=_cparams(("arbitrary",)),
        name="attn_sample",
    )(sinks, q, kn, vn, ck, cv, g)


def _ssm_prompt_body(ua_ref, uc_ref, b_ref, c_ref, lr_ref, li_ref, pr_ref, pi_ref, d_ref, y_ref, hl_ref,
                     uperm_ref, bu0, bu1, hb0, hb1, yp_ref, carry_ref, *, nchunks, ntc):
    k = pl.program_id(0)
    bus = (bu0, bu1)
    hbs = (hb0, hb1)
    lanes_per_chunk = CH_IN // LANES
    seg = lambda s: slice(s * SCAN_SEG, (s + 1) * SCAN_SEG)
    lane = lambda c: slice(c * LANES, (c + 1) * LANES)

    def stage_a(item, slot):
        cc = item % SSM_CHUNKS
        for c in range(lanes_per_chunk):
            for s in range(SUBLANES):
                uperm_ref[c, pl.ds(s, SCAN_SEG, stride=SUBLANES), :] = ua_ref[seg(s), lane(c)]
        up = jnp.concatenate([uperm_ref[c] for c in range(lanes_per_chunk)], axis=1)
        bus[slot][...] = _dot_nt(up.astype(BF16), b_ref[cc])

    def stage_b(item, slot):
        cc = item % SSM_CHUNKS
        tc = (item // SSM_CHUNKS) % ntc
        bu_ref = bus[slot]
        hb_ref = hbs[slot]
        ar = jnp.broadcast_to(lr_ref[cc], (SUBLANES, CH_STATES))
        ai = jnp.broadcast_to(li_ref[cc], (SUBLANES, CH_STATES))

        def advance(h, j):
            hr, hi = h
            br = bu_ref[j * SUBLANES:(j + 1) * SUBLANES, 0:CH_STATES]
            bi = bu_ref[j * SUBLANES:(j + 1) * SUBLANES, CH_STATES:2 * CH_STATES]
            return ar * hr + (br - ai * hi), ar * hi + (bi + ai * hr)

        zero = jnp.zeros((SUBLANES, CH_STATES), F32)
        h = (zero, zero)
        for j in range(SCAN_SEG):
            h = advance(h, j)
        fr, fi = h

        cst = jnp.where(tc == 0, 0.0, carry_ref[cc])
        c_r = cst[0:1, :]
        c_i = cst[1:2, :]
        lsr = pr_ref[cc]
        lsi = pi_ref[cc]
        rows_r, rows_i = [], []
        for s in range(SUBLANES):
            rows_r.append(c_r)
            rows_i.append(c_i)
            n_r = fr[s:s + 1, :] + (lsr * c_r - lsi * c_i)
            n_i = fi[s:s + 1, :] + (lsr * c_i + lsi * c_r)
            c_r, c_i = n_r, n_i
        carry_ref[cc] = jnp.concatenate([c_r, c_i], axis=0)
        hl_ref[0, cc] = jnp.concatenate([c_r, c_i], axis=0)

        h = (jnp.concatenate(rows_r, axis=0), jnp.concatenate(rows_i, axis=0))
        for jj in range(SCAN_SEG // 2):
            h1 = advance(h, 2 * jj)
            h = advance(h1, 2 * jj + 1)
            rows = slice(2 * jj * SUBLANES, (2 * jj + 2) * SUBLANES)
            hb_ref[rows, 0:CH_STATES] = jnp.concatenate([h1[0], h[0]], axis=0).astype(BF16)
            hb_ref[rows, CH_STATES:2 * CH_STATES] = jnp.concatenate([h1[1], h[1]], axis=0).astype(BF16)

    def stage_c(item, slot):
        cc = item % SSM_CHUNKS
        yp = _dot_nt(hbs[slot][...], c_ref[cc])
        d = d_ref[cc]
        for c in range(lanes_per_chunk):
            yp_ref[c] = yp[:, lane(c)]
        for s in range(SUBLANES):
            for c in range(lanes_per_chunk):
                y_ref[seg(s), lane(c)] = (yp_ref[c, pl.ds(s, SCAN_SEG, stride=SUBLANES), :]
                                          + d[:, lane(c)] * uc_ref[seg(s), lane(c)])

    def run(do_a, do_b, do_c, parity):
        if do_a:
            stage_a(k, parity)
        if do_b:
            stage_b(k - 1, 1 - parity)
        if do_c:
            stage_c(k - 2, parity)

    @pl.when(k == 0)
    def _():
        carry_ref[...] = jnp.zeros(carry_ref.shape, F32)
        run(True, False, False, 0)

    @pl.when(k == 1)
    def _():
        run(True, True, False, 1)

    for parity in range(2):
        @pl.when((k >= 2) & (k < nchunks) & (k % 2 == parity))
        def _():
            run(True, True, True, parity)

    @pl.when(k == nchunks)
    def _():
        run(False, True, True, nchunks % 2)

    @pl.when(k == nchunks + 1)
    def _():
        run(False, False, True, (nchunks + 1) % 2)


def _ssm_prompt(u, sp, layer, nbatch, seq):
    rows = u.shape[0]
    ntc = seq // SCAN_T
    nchunks = nbatch * ntc * SSM_CHUNKS
    assert nchunks >= 2
    item_a = lambda k: jnp.minimum(k, nchunks - 1)
    item_b = lambda k: jnp.clip(k - 1, 0, nchunks - 1)
    item_c = lambda k: jnp.clip(k - 2, 0, nchunks - 1)
    chunk_block = lambda item: (item // SSM_CHUNKS, item % SSM_CHUNKS)
    y, hl = pl.pallas_call(
        functools.partial(_ssm_prompt_body, nchunks=nchunks, ntc=ntc),
        grid=(nchunks + 2,),
        in_specs=[pl.BlockSpec((SCAN_T, CH_IN), lambda k: chunk_block(item_a(k))),
                  pl.BlockSpec((SCAN_T, CH_IN), lambda k: chunk_block(item_c(k))),
                  ] + [_layer_spec(sp[n].shape[1:], layer) for n in ("B", "C", "lr", "li", "pr", "pi", "d")],
        out_specs=[pl.BlockSpec((SCAN_T, CH_IN), lambda k: chunk_block(item_c(k))),
                   pl.BlockSpec((1, SSM_CHUNKS, 2, CH_STATES),
                                lambda k: (item_b(k) // (ntc * SSM_CHUNKS), 0, 0, 0))],
        out_shape=[jax.ShapeDtypeStruct((rows, SSM_WIDTH), F32),
                   jax.ShapeDtypeStruct((nbatch, SSM_CHUNKS, 2, CH_STATES), F32)],
        scratch_shapes=[pltpu.VMEM((CH_IN // LANES, SCAN_T, LANES), F32),
                        pltpu.VMEM((SCAN_T, 2 * CH_STATES), F32), pltpu.VMEM((SCAN_T, 2 * CH_STATES), F32),
                        pltpu.VMEM((SCAN_T, 2 * CH_STATES), BF16), pltpu.VMEM((SCAN_T, 2 * CH_STATES), BF16),
                        pltpu.VMEM((CH_IN // LANES, SCAN_T, LANES), F32),
                        pltpu.VMEM((SSM_CHUNKS, 2, CH_STATES), F32)],
        compiler_params=_cparams(("arbitrary",)),
        name="ssm_prompt",
    )(u, u, sp["B"], sp["C"], sp["lr"], sp["li"], sp["pr"], sp["pi"], sp["d"])
    return y, hl


def _ssm_sample_body(u_ref, h0_ref, b_ref, c_ref, lr_ref, li_ref, d_ref, y_ref, ho_ref, bu_ref, *, nseq, dec_seq):
    nl = CH_STATES // LANES
    for k in range(SSM_CHUNKS):
        uk = u_ref[:, k * CH_IN:(k + 1) * CH_IN]
        bu = _dot_nt(uk.astype(BF16), b_ref[k])
        for c in range(2 * nl):
            bu_ref[c] = bu[:, c * LANES:(c + 1) * LANES]
        base = k * 2 * CH_STATES
        hr = h0_ref[:, base:base + CH_STATES]
        hi = h0_ref[:, base + CH_STATES:base + 2 * CH_STATES]
        ar = lr_ref[k]
        ai = li_ref[k]
        for t in range(dec_seq):
            br = jnp.concatenate([bu_ref[c, pl.ds(t, nseq, stride=dec_seq), :] for c in range(nl)], axis=1)
            bi = jnp.concatenate([bu_ref[nl + c, pl.ds(t, nseq, stride=dec_seq), :] for c in range(nl)], axis=1)
            hr, hi = ar * hr + (br - ai * hi), ar * hi + (bi + ai * hr)
            for c in range(nl):
                bu_ref[c, pl.ds(t, nseq, stride=dec_seq), :] = hr[:, c * LANES:(c + 1) * LANES]
                bu_ref[nl + c, pl.ds(t, nseq, stride=dec_seq), :] = hi[:, c * LANES:(c + 1) * LANES]
        ho_ref[:, base:base + CH_STATES] = hr
        ho_ref[:, base + CH_STATES:base + 2 * CH_STATES] = hi
        hall = jnp.concatenate([bu_ref[c] for c in range(2 * nl)], axis=1)
        y_ref[:, k * CH_IN:(k + 1) * CH_IN] = (
            _dot_nt(hall.astype(BF16), c_ref[k]) + d_ref[k] * uk)


def _ssm_sample(u, h0, sp, layer, nseq, dec_seq):
    rows = u.shape[0]
    return pl.pallas_call(
        functools.partial(_ssm_sample_body, nseq=nseq, dec_seq=dec_seq),
        grid=(1,),
        in_specs=[_full(u.shape), _full(h0.shape)] + [_layer_spec(sp[n].shape[1:], layer)
                                                      for n in ("B", "C", "lr", "li", "d")],
        out_specs=[_full((rows, SSM_WIDTH)), _full(h0.shape)],
        out_shape=[jax.ShapeDtypeStruct((rows, SSM_WIDTH), F32), jax.ShapeDtypeStruct(h0.shape, F32)],
        scratch_shapes=[pltpu.VMEM((2 * CH_STATES // LANES, rows, LANES), F32)],
        compiler_params=_cparams(("arbitrary",)),
        name="ssm_sample",
    )(u, h0, sp["B"], sp["C"], sp["lr"], sp["li"], sp["d"])


def _ssm_params(lambda_re, lambda_im, log_dt, b_re, b_im, c_re, c_im, d):
    lam = lax.complex(jnp.minimum(lambda_re.astype(F32), -1e-4), lambda_im.astype(F32))
    dt = jnp.exp(log_dt.astype(F32))[:, None]
    lam_dt = lam * dt
    lam_bar = jnp.exp(lam_dt)
    b = lax.complex(b_re.astype(F32), b_im.astype(F32))
    b_bar = ((lam_bar - 1.0) / lam)[..., None] * b
    rows_b = lambda part: part.reshape(SSM_CHUNKS, CH_STATES, SSM_GROUP)
    rows_c = lambda part: part.reshape(SSM_CHUNKS, CH_IN, SSM_STATE)
    pw = jnp.exp(lam_dt * float(SCAN_SEG)).reshape(SSM_CHUNKS, 1, CH_STATES)
    lb = lam_bar.reshape(SSM_CHUNKS, 1, CH_STATES)
    return dict(b_re=rows_b(jnp.real(b_bar)), b_im=rows_b(jnp.imag(b_bar)),
                c_re=rows_c(c_re.astype(F32)), c_im_neg=rows_c(-c_im.astype(F32)),
                lr=jnp.real(lb), li=jnp.imag(lb), pr=jnp.real(pw), pi=jnp.imag(pw),
                d=d.astype(F32).reshape(SSM_CHUNKS, 1, CH_IN))


def _ssm_pack_body(bre_ref, bim_ref, cre_ref, cimn_ref, eb_ref, ec_ref, b_out, c_out):
    def expand(rows_ref, onehot_ref, row_group, col_group):
        t = jnp.dot(rows_ref[...].astype(BF16), onehot_ref[...], preferred_element_type=F32)
        own = lax.broadcasted_iota(jnp.int32, t.shape, 0) // row_group
        col = lax.broadcasted_iota(jnp.int32, t.shape, 1) // col_group
        return jnp.where(own == col, t, 0.0).astype(BF16)

    b_out[0:CH_STATES, :] = expand(bre_ref, eb_ref, SSM_STATE, SSM_GROUP)
    b_out[CH_STATES:2 * CH_STATES, :] = expand(bim_ref, eb_ref, SSM_STATE, SSM_GROUP)
    c_out[:, 0:CH_STATES] = expand(cre_ref, ec_ref, SSM_GROUP, SSM_STATE)
    c_out[:, CH_STATES:2 * CH_STATES] = expand(cimn_ref, ec_ref, SSM_GROUP, SSM_STATE)


def _ssm_pack(sp):
    depth = sp["b_re"].shape[0]
    eb = jnp.asarray(np.tile(np.eye(SSM_GROUP, dtype=np.float32), (1, CH_GROUPS)), BF16)
    ec = jnp.asarray(np.tile(np.eye(SSM_STATE, dtype=np.float32), (1, CH_GROUPS)), BF16)
    item = lambda r, c: pl.BlockSpec((None, None, r, c), lambda l, k: (l, k, 0, 0))
    bmat, cmat = pl.pallas_call(
        _ssm_pack_body,
        grid=(depth, SSM_CHUNKS),
        in_specs=[item(CH_STATES, SSM_GROUP), item(CH_STATES, SSM_GROUP), item(CH_IN, SSM_STATE),
                  item(CH_IN, SSM_STATE), _full(eb.shape), _full(ec.shape)],
        out_specs=[item(2 * CH_STATES, CH_IN), item(CH_IN, 2 * CH_STATES)],
        out_shape=[jax.ShapeDtypeStruct((depth, SSM_CHUNKS, 2 * CH_STATES, CH_IN), BF16),
                   jax.ShapeDtypeStruct((depth, SSM_CHUNKS, CH_IN, 2 * CH_STATES), BF16)],
        compiler_params=_cparams(("arbitrary", "arbitrary")),
        name="ssm_pack",
    )(sp["b_re"], sp["b_im"], sp["c_re"], sp["c_im_neg"], eb, ec)
    return dict(sp, B=bmat, C=cmat)


def _pack_state(re, im):
    n = re.shape[0]
    st = jnp.stack([re.reshape(n, SSM_CHUNKS, CH_STATES), im.reshape(n, SSM_CHUNKS, CH_STATES)], axis=2)
    return st.reshape(n, SSM_CHUNKS * 2 * CH_STATES).astype(F32)


def _unpack_state(h, n):
    st = h.reshape(n, SSM_CHUNKS, 2, CH_GROUPS, SSM_STATE)
    return (st[:, :, 0].reshape(n, SSM_GROUPS, SSM_STATE), st[:, :, 1].reshape(n, SSM_GROUPS, SSM_STATE))


def _glu_body(y_ref, w_ref, g_ref, o_ref):
    a = jax.nn.gelu(y_ref[...])
    z = a * jax.nn.sigmoid(jnp.dot(a.astype(BF16), w_ref[...], preferred_element_type=F32))
    o_ref[...] = _rms(z, g_ref[...]).astype(BF16)


def _glu(y, w_bf, layer, g, tm):
    rows = y.shape[0]
    spec = pl.BlockSpec((tm, SSM_WIDTH), lambda i: (i, 0))
    return pl.pallas_call(
        _glu_body,
        grid=(rows // tm,),
        in_specs=[spec, _layer_spec((SSM_WIDTH, SSM_WIDTH), layer), _full((1, SSM_WIDTH))],
        out_specs=spec,
        out_shape=jax.ShapeDtypeStruct((rows, SSM_WIDTH), BF16),
        compiler_params=_cparams(("arbitrary",)),
        name="glu_rms",
    )(y, w_bf, g)


def _layer_norm(y, g, b):
    mu = jnp.mean(y, axis=-1, keepdims=True)
    yc = y - mu
    var = jnp.mean(jnp.square(yc), axis=-1, keepdims=True)
    return yc * lax.rsqrt(var + LN_EPS) * g + b


OUTPROJ_SUB = 256


def _outproj_body(*refs, aliased, nblk):
    if aliased:
        refs = refs[1:]
    a_ref, s_ref, x_ref, w_ref, g_ref, b_ref, wrc_ref, br_ref, o_ref, lg_ref = refs
    i = pl.program_id(0)

    @pl.when(i < nblk)
    def _():
        tm = a_ref.shape[0]
        sub = min(tm, OUTPROJ_SUB)
        for r0 in range(0, tm, sub):
            rows = slice(r0, r0 + sub)
            acc = jnp.dot(a_ref[rows, :], w_ref[0:ATTN_WIDTH, :], preferred_element_type=F32)
            acc = acc + jnp.dot(s_ref[rows, :], w_ref[ATTN_WIDTH:, :], preferred_element_type=F32)
            x1 = _layer_norm(DEEPNORM_ALPHA * x_ref[rows, :] + acc, g_ref[...], b_ref[...])
            o_ref[rows, :] = x1
            hh = jnp.dot(x1.astype(BF16), wrc_ref[...], preferred_element_type=F32)
            lg_ref[rows, :] = hh[:, 0:LANES] + hh[:, LANES:] + br_ref[...]

    @pl.when(i >= nblk)
    def _():
        o_ref[...] = jnp.zeros(o_ref.shape, F32)


def _outproj(an, sn, x, w_bf, layer, g, b, wr, br, tm, total_rows, row_block0, x1_buf=None):
    rows = an.shape[0]
    aliased = x1_buf is not None
    nblk = rows // tm
    tail = 0 if aliased or rows == total_rows else 1
    blk = lambda i: jnp.minimum(i, nblk - 1)
    half = lambda: pl.BlockSpec((tm, ATTN_WIDTH), lambda i: (blk(i), 0))
    in_specs = [half(), half(), pl.BlockSpec((tm, D_MODEL), lambda i: (blk(i), 0)),
                _layer_spec((D_MODEL, D_MODEL), layer), _full((1, D_MODEL)), _full((1, D_MODEL)),
                _full((D_MODEL, 2 * LANES)), _full((1, LANES))]
    wr_hi = wr.astype(BF16)
    wr_lo = (wr - wr_hi.astype(F32)).astype(BF16)
    args = [an, sn, x, w_bf, g, b, jnp.concatenate([wr_hi, wr_lo], axis=1), br]
    if aliased:
        in_specs = [pl.BlockSpec(memory_space=pl.ANY)] + in_specs
        args = [x1_buf] + args
    return pl.pallas_call(
        functools.partial(_outproj_body, aliased=aliased, nblk=nblk),
        grid=(nblk + tail,),
        in_specs=in_specs,
        out_specs=[pl.BlockSpec((tm, D_MODEL), lambda i: (i + row_block0, 0)),
                   pl.BlockSpec((tm, LANES), lambda i: (blk(i), 0))],
        out_shape=[jax.ShapeDtypeStruct((total_rows, D_MODEL), F32), jax.ShapeDtypeStruct((rows, LANES), F32)],
        input_output_aliases={0: 0} if aliased else {},
        compiler_params=_cparams(("arbitrary",)),
        name="outproj_ln1",
    )(*args)


ROUTE_T = 640
R_E0, R_E1, R_W0, R_W1, R_P0, R_P1 = range(6)


def _route_select(x):
    lane = lax.broadcasted_iota(jnp.int32, x.shape, 1)
    big = jnp.int32(1 << 20)
    neg = -jnp.inf
    gmask = lane < N_EXPERT_GROUPS
    gm = jnp.max(jnp.where(gmask, x, neg), axis=-1, keepdims=True)
    grp = jnp.min(jnp.where(gmask & (x == gm), lane, big), axis=-1, keepdims=True)
    p_grp = 1.0 / jnp.sum(jnp.where(gmask, jnp.exp(x - gm), 0.0), axis=-1, keepdims=True)
    lo = N_EXPERT_GROUPS + grp * EXPERTS_PER_GROUP
    emask = (lane >= lo) & (lane < lo + EXPERTS_PER_GROUP)
    v1 = jnp.max(jnp.where(emask, x, neg), axis=-1, keepdims=True)
    i1 = jnp.min(jnp.where(emask & (x == v1), lane, big), axis=-1, keepdims=True)
    emask2 = emask & (lane != i1)
    v2 = jnp.max(jnp.where(emask2, x, neg), axis=-1, keepdims=True)
    i2 = jnp.min(jnp.where(emask2 & (x == v2), lane, big), axis=-1, keepdims=True)
    e2 = jnp.exp(v2 - v1)
    w1 = p_grp / (1.0 + e2)
    w2 = p_grp * e2 / (1.0 + e2)
    return (i1 - N_EXPERT_GROUPS).astype(F32), (i2 - N_EXPERT_GROUPS).astype(F32), w1, w2


def _route_plan_body(lg_ref, ltri_ref, utri_ref, o_ref, meta_ref, rec_ref, run_ref):
    ph = pl.program_id(0)
    j = pl.program_id(1)
    tm = lg_ref.shape[0]
    lane = lax.broadcasted_iota(jnp.int32, (tm, LANES), 1)
    lanef = lane.astype(F32)
    r0 = pl.multiple_of(j * tm, tm)

    @pl.when((ph == 0) & (j == 0))
    def _():
        run_ref[...] = jnp.zeros(run_ref.shape, F32)

    @pl.when(ph == 0)
    def _():
        e0, e1, w0, w1 = _route_select(lg_ref[...])
        oh0 = (lanef == e0).astype(F32)
        oh1 = (lanef == e1).astype(F32)
        oh2 = oh0 + oh1
        before = jnp.dot(ltri_ref[...], oh2.astype(BF16), preferred_element_type=F32) + run_ref[0:1, :]
        rk0 = jnp.sum(before * oh0, axis=-1, keepdims=True)
        rk1 = jnp.sum(before * oh1, axis=-1, keepdims=True)
        run_ref[...] = run_ref[...] + jnp.sum(oh2, axis=0, keepdims=True)
        rec = jnp.where(lane == R_E0, e0, 0.0)
        rec = jnp.where(lane == R_E1, e1, rec)
        rec = jnp.where(lane == R_W0, w0, rec)
        rec = jnp.where(lane == R_W1, w1, rec)
        rec = jnp.where(lane == R_P0, rk0, rec)
        rec = jnp.where(lane == R_P1, rk1, rec)
        rec_ref[pl.ds(r0, tm), :] = rec

    @pl.when(ph == 1)
    def _():
        counts = run_ref[...]
        nblk = jnp.floor((counts + (MOE_BLOCK - 1)) * (1.0 / MOE_BLOCK))
        end_blk = jnp.dot(nblk.astype(BF16), utri_ref[...], preferred_element_type=F32)
        start_row = (end_blk - nblk) * MOE_BLOCK
        rec = rec_ref[pl.ds(r0, tm), :]
        oh0 = (lanef == rec[:, R_E0:R_E0 + 1]).astype(F32)
        oh1 = (lanef == rec[:, R_E1:R_E1 + 1]).astype(F32)
        p0 = jnp.sum(oh0 * start_row[0:1, :], axis=-1, keepdims=True) + rec[:, R_P0:R_P0 + 1]
        p1 = jnp.sum(oh1 * start_row[0:1, :], axis=-1, keepdims=True) + rec[:, R_P1:R_P1 + 1]
        out = jnp.where(lane == R_P0, p0, rec)
        out = jnp.where(lane == R_P1, p1, out)
        o_ref[...] = out

        @pl.when(j == 0)
        def _():
            mrow = lax.broadcasted_iota(jnp.int32, meta_ref.shape, 0)
            meta_ref[...] = jnp.where(mrow == 1, counts, end_blk)


def _route_plan(logits):
    rows = logits.shape[0]
    tm = ROUTE_T
    ltri = jnp.asarray(np.tril(np.ones((tm, tm), np.float32), -1), BF16)
    utri = jnp.asarray(np.triu(np.ones((LANES, LANES), np.float32)), BF16)
    mrows = SUBLANES
    return pl.pallas_call(
        _route_plan_body,
        grid=(2, rows // tm),
        in_specs=[pl.BlockSpec((tm, LANES), lambda p, j: (j * (1 - p), 0)),
                  pl.BlockSpec((tm, tm), lambda p, j: (0, 0)),
                  pl.BlockSpec((LANES, LANES), lambda p, j: (0, 0))],
        out_specs=[pl.BlockSpec((tm, LANES), lambda p, j: (j * p, 0)),
                   pl.BlockSpec((mrows, LANES), lambda p, j: (0, 0))],
        out_shape=[jax.ShapeDtypeStruct((rows, LANES), F32), jax.ShapeDtypeStruct((mrows, LANES), F32)],
        scratch_shapes=[pltpu.VMEM((rows, LANES), F32), pltpu.VMEM((SUBLANES, LANES), F32)],
        compiler_params=_cparams(("arbitrary", "arbitrary")),
        name="route_plan",
    )(logits, ltri, utri)


U32 = jnp.uint32
PACKED_W = D_MODEL // 2


def _pack_bf16_pairs(y):
    bits = lax.bitcast_convert_type(y.astype(BF16).astype(F32), U32)
    return bits[:, 0:PACKED_W] | (bits[:, PACKED_W:] >> 16)


def _unpack_bf16_pairs(p):
    hi = lax.bitcast_convert_type(p & jnp.uint32(0xFFFF0000), F32)
    lo = lax.bitcast_convert_type(p << 16, F32)
    return jnp.concatenate([hi, lo], axis=1)


def _build_rowmap(pos_ref, bend_ref, cnt_ref, o_ref):
    unroll = 8
    ntok = pos_ref.shape[0] // 2

    def zero_padding(e, _):
        first_blk = jnp.where(e == 0, 0, bend_ref[jnp.maximum(e - 1, 0)])

        def zero(r, _):
            o_ref[r] = 0
            return 0
        lax.fori_loop(first_blk * MOE_BLOCK + cnt_ref[e], bend_ref[e] * MOE_BLOCK, zero, 0)
        return 0
    lax.fori_loop(0, N_EXPERTS, zero_padding, 0)

    def fill(c, _):
        for u in range(unroll):
            t = c * unroll + u
            o_ref[pos_ref[2 * t]] = t
            o_ref[pos_ref[2 * t + 1]] = t
        return 0
    lax.fori_loop(0, ntok // unroll, fill, 0)


GATHER_SLOTS = 4


def _moe_body(bend_ref, cnt_ref, pos_ref, x_hbm, wg_ref, wu_ref, wd_ref, ys_hbm, *scratch, nblocks):
    xbufs = scratch[:GATHER_SLOTS]
    ybufs = scratch[GATHER_SLOTS:GATHER_SLOTS + 2]
    gsem, osem, wg_bf, wu_bf, wd_bf, row_tok_ref = scratch[GATHER_SLOTS + 2:]
    ahead = GATHER_SLOTS - 1
    e = pl.program_id(0)
    nused = bend_ref[N_EXPERTS - 1]
    b1 = bend_ref[e]
    b0 = jnp.where(e == 0, 0, bend_ref[jnp.maximum(e - 1, 0)])

    def gather_start(blk, s):
        base = blk * MOE_BLOCK
        for r in range(MOE_BLOCK):
            tok = row_tok_ref[base + r]
            pltpu.make_async_copy(x_hbm.at[pl.ds(tok, 1), :], xbufs[s].at[pl.ds(r, 1), :],
                                  gsem.at[s]).start(priority=r % 2)

    def gather_wait(s):
        pltpu.make_async_copy(x_hbm.at[pl.ds(0, MOE_BLOCK), :], xbufs[s], gsem.at[s]).wait()

    def out_copy(blk, s):
        row0 = pl.multiple_of(blk * MOE_BLOCK, MOE_BLOCK)
        return pltpu.make_async_copy(ybufs[s], ys_hbm.at[pl.ds(row0, MOE_BLOCK), :], osem.at[s])

    @pl.when(e == 0)
    def _():
        _build_rowmap(pos_ref, bend_ref, cnt_ref, row_tok_ref)
        for k in range(ahead):
            gather_start(k, k)

    @pl.when(b1 > b0)
    def _():
        wg_bf[...] = wg_ref[...].astype(BF16)
        wu_bf[...] = wu_ref[...].astype(BF16)
        wd_bf[...] = wd_ref[...].astype(BF16)

        def do_block(b, s):
            so = s % 2
            gather_wait(s)

            @pl.when(b >= 2)
            def _():
                out_copy(b - 2, so).wait()

            gather_start(jnp.minimum(b + ahead, nused - 1), (s + ahead) % GATHER_SLOTS)
            x = xbufs[s][...].astype(BF16)
            hg = jnp.dot(x, wg_bf[...], preferred_element_type=F32)
            hu = jnp.dot(x, wu_bf[...], preferred_element_type=F32)
            h = (jax.nn.silu(hg) * hu).astype(BF16)
            ybufs[so][...] = _pack_bf16_pairs(jnp.dot(h, wd_bf[...], preferred_element_type=F32))
            out_copy(b, so).start()

        def body(b, _):
            for s in range(GATHER_SLOTS):
                @pl.when(b % GATHER_SLOTS == s)
                def _():
                    do_block(b, s)
            return 0

        lax.fori_loop(b0, b1, body, 0)

    @pl.when(e == N_EXPERTS - 1)
    def _():
        last = nused - 1
        for s in range(GATHER_SLOTS):
            @pl.when(last % GATHER_SLOTS == s)
            def _():
                for k in range(1, GATHER_SLOTS):
                    gather_wait((s + k) % GATHER_SLOTS)
        out_copy(0, 0).wait()
        out_copy(0, 1).wait()
        yb0 = ybufs[0]
        yb0[...] = jnp.zeros(yb0.shape, U32)

        def zstart(b, _):
            out_copy(b, 0).start()
            return 0

        def zwait(b, _):
            out_copy(b, 0).wait()
            return 0

        lax.fori_loop(nused, nblocks, zstart, 0)
        lax.fori_loop(nused, nblocks, zwait, 0)


def _moe(bend, cnt, pos, x1, w_gate, w_up, w_down, layer, nblocks):
    assert 2 * x1.shape[0] > GATHER_SLOTS * MOE_BLOCK
    wspec = lambda a, b: pl.BlockSpec((None, None, a, b), lambda e, be, cn, ps: (layer, e, 0, 0))
    grid_spec = pltpu.PrefetchScalarGridSpec(
        num_scalar_prefetch=3,
        grid=(N_EXPERTS,),
        in_specs=[pl.BlockSpec(memory_space=pl.ANY), wspec(D_MODEL, EXPERT_FF), wspec(D_MODEL, EXPERT_FF),
                  wspec(EXPERT_FF, D_MODEL)],
        out_specs=pl.BlockSpec(memory_space=pl.ANY),
        scratch_shapes=([pltpu.VMEM((MOE_BLOCK, D_MODEL), F32) for _ in range(GATHER_SLOTS)]
                        + [pltpu.VMEM((MOE_BLOCK, PACKED_W), U32) for _ in range(2)]
                        + [pltpu.SemaphoreType.DMA((GATHER_SLOTS,)), pltpu.SemaphoreType.DMA((2,)),
                           pltpu.VMEM((D_MODEL, EXPERT_FF), BF16),
                           pltpu.VMEM((D_MODEL, EXPERT_FF), BF16),
                           pltpu.VMEM((EXPERT_FF, D_MODEL), BF16),
                           pltpu.SMEM((nblocks * MOE_BLOCK,), jnp.int32)]),
    )
    return pl.pallas_call(
        functools.partial(_moe_body, nblocks=nblocks),
        grid_spec=grid_spec,
        out_shape=jax.ShapeDtypeStruct((nblocks * MOE_BLOCK, PACKED_W), U32),
        compiler_params=_cparams(("arbitrary",), bounds_checks=False),
        name="moe_experts",
    )(bend, cnt, pos, x1, w_gate, w_up, w_down)


COMB_T = 128


def _combine_body(pos_ref, ys_hbm, x_ref, rt_ref, g_ref, b_ref, op_ref, os_ref, *scratch, n_prompt_blocks):
    i = pl.program_id(0)
    nsteps = pl.num_programs(0)
    bufs = tuple((scratch[2 * s], scratch[2 * s + 1]) for s in range(GATHER_SLOTS))
    sem = scratch[2 * GATHER_SLOTS]
    ahead = GATHER_SLOTS - 1

    def gather_start(step, s):
        base = step * (2 * COMB_T)
        for r in range(COMB_T):
            for k in range(2):
                p = pos_ref[base + 2 * r + k]
                pltpu.make_async_copy(ys_hbm.at[pl.ds(p, 1), :], bufs[s][k].at[pl.ds(r, 1), :],
                                      sem.at[s]).start(priority=k)

    def gather_wait(s):
        for k in range(2):
            pltpu.make_async_copy(ys_hbm.at[pl.ds(0, COMB_T), :], bufs[s][k], sem.at[s]).wait()

    @pl.when(i == 0)
    def _():
        for k in range(ahead):
            gather_start(k, k)

    def do_tile(s):
        gather_wait(s)
        gather_start(jnp.minimum(i + ahead, nsteps - 1), (s + ahead) % GATHER_SLOTS)
        rt = rt_ref[...]
        ff = (rt[:, R_W0:R_W0 + 1] * _unpack_bf16_pairs(bufs[s][0][...])
              + rt[:, R_W1:R_W1 + 1] * _unpack_bf16_pairs(bufs[s][1][...]))
        out = _layer_norm(DEEPNORM_ALPHA * x_ref[...] + ff, g_ref[...], b_ref[...])

        @pl.when(i < n_prompt_blocks)
        def _():
            op_ref[...] = out

        @pl.when(i >= n_prompt_blocks)
        def _():
            os_ref[...] = out

        @pl.when(i == nsteps - 1)
        def _():
            for k in range(1, GATHER_SLOTS):
                gather_wait((s + k) % GATHER_SLOTS)

    for s in range(GATHER_SLOTS):
        @pl.when(i % GATHER_SLOTS == s)
        def _():
            do_tile(s)


def _combine(pos, ys, x1, route, g, b, n_prompt_rows):
    rows = x1.shape[0]
    npb = n_prompt_rows // COMB_T
    assert rows // COMB_T >= GATHER_SLOTS
    grid_spec = pltpu.PrefetchScalarGridSpec(
        num_scalar_prefetch=1,
        grid=(rows // COMB_T,),
        in_specs=[pl.BlockSpec(memory_space=pl.ANY),
                  pl.BlockSpec((COMB_T, D_MODEL), lambda i, p: (i, 0)),
                  pl.BlockSpec((COMB_T, LANES), lambda i, p: (i, 0)),
                  pl.BlockSpec((1, D_MODEL), lambda i, p: (0, 0)),
                  pl.BlockSpec((1, D_MODEL), lambda i, p: (0, 0))],
        out_specs=[pl.BlockSpec((COMB_T, D_MODEL), lambda i, p: (jnp.minimum(i, npb - 1), 0)),
                   pl.BlockSpec((COMB_T, D_MODEL), lambda i, p: (jnp.maximum(i - npb, 0), 0))],
        scratch_shapes=([pltpu.VMEM((COMB_T, PACKED_W), U32) for _ in range(2 * GATHER_SLOTS)]
                        + [pltpu.SemaphoreType.DMA((GATHER_SLOTS,))]),
    )
    return pl.pallas_call(
        functools.partial(_combine_body, n_prompt_blocks=npb),
        grid_spec=grid_spec,
        out_shape=[jax.ShapeDtypeStruct((n_prompt_rows, D_MODEL), F32),
                   jax.ShapeDtypeStruct((rows - n_prompt_rows, D_MODEL), F32)],
        compiler_params=_cparams(("arbitrary",), bounds_checks=False),
        name="combine_ln2",
    )(pos, ys, x1, route, g, b)


def kernel(x_prompt, x_sample, cache_k, cache_v, state_ssm_re, state_ssm_im, w_in, ssm_lambda_re, ssm_lambda_im, ssm_log_dt, ssm_b_re, ssm_b_im, ssm_c_re, ssm_c_im, ssm_d, ssm_w_glu, attn_sinks, attn_norm_g, ssm_norm_g, w_out, ln1_g, ln1_b, router_group_w, router_group_b, router_expert_w, router_expert_b, expert_w_gate, expert_w_up, expert_w_down, ln2_g, ln2_b):
    nbatch, seq, _ = x_prompt.shape
    nseq, dec_seq, _ = x_sample.shape
    wbuf = cache_k.shape[2]
    rows_p = nbatch * seq
    rows_s = nseq * dec_seq
    rows_all = rows_p + rows_s
    tm_p = 512

    xp = x_prompt.reshape(rows_p, D_MODEL)
    xs = x_sample.reshape(rows_s, D_MODEL)
    tab_p = _rope_tables(np.arange(seq))
    tab_s = _rope_tables(PAST_LEN + (np.arange(rows_s) % dec_seq))
    ck_all = cache_k.astype(F32).reshape(DEPTH, nseq, wbuf, KV_WIDTH)
    cv_all = cache_v.astype(F32).reshape(DEPTH, nseq, wbuf, KV_WIDTH)

    sp = _ssm_pack(jax.vmap(_ssm_params)(ssm_lambda_re, ssm_lambda_im, ssm_log_dt, ssm_b_re, ssm_b_im,
                                         ssm_c_re, ssm_c_im, ssm_d))
    w_in_bf = w_in.astype(BF16)
    w_glu_bf = ssm_w_glu.astype(BF16)
    w_out_bf = w_out.astype(BF16)

    outs = {k: [] for k in ("kp", "vp", "hrp", "hip", "ks", "vs", "hrs", "his")}
    for l in range(DEPTH):
        sinks = attn_sinks[l].astype(F32)
        attn_g = attn_norm_g[l].astype(F32).reshape(1, ATTN_WIDTH)
        ssm_g = ssm_norm_g[l].astype(F32).reshape(1, SSM_WIDTH)
        g1 = ln1_g[l].astype(F32).reshape(1, D_MODEL)
        b1 = ln1_b[l].astype(F32).reshape(1, D_MODEL)
        g2 = ln2_g[l].astype(F32).reshape(1, D_MODEL)
        b2 = ln2_b[l].astype(F32).reshape(1, D_MODEL)
        n_rt = N_EXPERT_GROUPS + N_EXPERTS
        wr = jnp.concatenate([router_group_w[l].astype(F32)]
                             + [router_expert_w[l, g].astype(F32) for g in range(N_EXPERT_GROUPS)]
                             + [jnp.zeros((D_MODEL, LANES - n_rt), F32)], axis=1)
        br = jnp.concatenate([router_group_b[l].astype(F32), router_expert_b[l].astype(F32).reshape(-1),
                              jnp.zeros((LANES - n_rt,), F32)]).reshape(1, LANES)

        q_p, k_p, v_p, u_p = _inproj(xp, w_in_bf, l, *tab_p, tm=tm_p)
        q_s, k_s, v_s, u_s = _inproj(xs, w_in_bf, l, *tab_s, tm=rows_s)

        an_p = _attn_prompt(sinks, q_p, k_p, v_p, attn_g, seq)
        an_s, nk_s, nv_s = _attn_sample(sinks, q_s, k_s, v_s, ck_all, cv_all, attn_g, dec_seq, l)

        y_p, hl_p = _ssm_prompt(u_p, sp, l, nbatch, seq)
        y_s, hl_s = _ssm_sample(u_s, _pack_state(state_ssm_re[l], state_ssm_im[l]), sp, l, nseq, dec_seq)
        sn_p = _glu(y_p, w_glu_bf, l, ssm_g, 2 * tm_p)
        sn_s = _glu(y_s, w_glu_bf, l, ssm_g, rows_s)

        x1, lg_p = _outproj(an_p, sn_p, xp, w_out_bf, l, g1, b1, wr, br, tm_p, rows_all, 0)
        x1, lg_s = _outproj(an_s, sn_s, xs, w_out_bf, l, g1, b1, wr, br, rows_s, rows_all, rows_p // rows_s,
                            x1_buf=x1)
        nblocks = 2 * rows_all // MOE_BLOCK + N_EXPERTS
        route, meta = _route_plan(jnp.concatenate([lg_p, lg_s], axis=0))
        pos = route[:, R_P0:R_P1 + 1].astype(jnp.int32).reshape(2 * rows_all)
        bend = meta[0, :N_EXPERTS].astype(jnp.int32)

        cnt = meta[1, :N_EXPERTS].astype(jnp.int32)
        ys = _moe(bend, cnt, pos, x1, expert_w_gate, expert_w_up, expert_w_down, l, nblocks)
        xp, xs = _combine(pos, ys, x1, route, g2, b2, rows_p)

        wp = min(WINDOW, seq)
        tail = lambda a: a.reshape(nbatch, seq, KV_WIDTH)[:, seq - wp:].reshape(nbatch, wp, N_KV_HEADS, HEAD_DIM)
        outs["kp"].append(tail(k_p))
        outs["vp"].append(tail(v_p))
        hp = hl_p.reshape(nbatch, SSM_CHUNKS * 2 * CH_STATES)
        hr, hi = _unpack_state(hp, nbatch)
        outs["hrp"].append(hr)
        outs["hip"].append(hi)
        outs["ks"].append(nk_s.reshape(nseq, wbuf, N_KV_HEADS, HEAD_DIM))
        outs["vs"].append(nv_s.reshape(nseq, wbuf, N_KV_HEADS, HEAD_DIM))
        hr, hi = _unpack_state(hl_s, nseq)
        outs["hrs"].append(hr)
        outs["his"].append(hi)

    st = {k: jnp.stack(v) for k, v in outs.items()}
    return (xp.reshape(nbatch, seq, D_MODEL), xs.reshape(nseq, dec_seq, D_MODEL),
            st["kp"], st["vp"], st["hrp"], st["hip"], st["ks"], st["vs"], st["hrs"], st["his"])
```

```python
import functools

import numpy as np
import jax
import jax.numpy as jnp
from jax import lax
from jax.experimental import pallas as pl
from jax.experimental.pallas import tpu as pltpu

F32 = jnp.float32
BF16 = jnp.bfloat16

D_MODEL = 2048
DEPTH = 2
PAST_LEN = 16384
ATTN_WIDTH = 1024
SSM_WIDTH = 1024
HEAD_DIM = 64
N_HEADS = 16
N_KV_HEADS = 4
Q_PER_KV = 4
KV_WIDTH = 256
WINDOW = 128
ROPE_THETA = 10000.0
SSM_GROUP = 16
SSM_GROUPS = 64
SSM_STATE = 64
N_EXPERT_GROUPS = 4
EXPERTS_PER_GROUP = 8
N_EXPERTS = 32
EXPERT_FF = 512
MOE_BLOCK = 128
DEEPNORM_ALPHA = (2.0 * DEPTH) ** 0.25
LN_EPS = 1e-5
RMS_EPS = 1e-6

LANES = 128
SUBLANES = 8
VMEM_LIMIT = 56 * 1024 * 1024

SSM_CHUNKS = 4
CH_GROUPS = SSM_GROUPS // SSM_CHUNKS
CH_STATES = CH_GROUPS * SSM_STATE
CH_IN = CH_GROUPS * SSM_GROUP
SCAN_T = 512
SCAN_SEG = SCAN_T // SUBLANES


def _cparams(sem=None, vmem=VMEM_LIMIT, bounds_checks=True):
    return pltpu.CompilerParams(dimension_semantics=sem, vmem_limit_bytes=vmem,
                                disable_bounds_checks=not bounds_checks)


def _full(shape):
    n = len(shape)
    return pl.BlockSpec(shape, lambda *a: (0,) * n)


def _dot_nt(a, b):
    return lax.dot_general(a, b, (((1,), (1,)), ((), ())), preferred_element_type=F32)


def _inproj_body(x_ref, w_ref, cos_ref, sa_ref, sb_ref, q_ref, k_ref, v_ref, u_ref):
    x = x_ref[...].astype(BF16)
    cos = cos_ref[...]
    sa = sa_ref[...]
    sb = sb_ref[...]

    half = HEAD_DIM // 2

    def rope(c):
        return c * cos + pltpu.roll(c, LANES - half, 1) * sa + pltpu.roll(c, half, 1) * sb

    pq = jnp.dot(x, w_ref[:, 0:ATTN_WIDTH], preferred_element_type=F32)
    for j in range(ATTN_WIDTH // LANES):
        q_ref[:, j * LANES:(j + 1) * LANES] = rope(pq[:, j * LANES:(j + 1) * LANES]).astype(BF16)
    pk = jnp.dot(x, w_ref[:, ATTN_WIDTH:ATTN_WIDTH + KV_WIDTH], preferred_element_type=F32)
    for j in range(KV_WIDTH // LANES):
        k_ref[:, j * LANES:(j + 1) * LANES] = rope(pk[:, j * LANES:(j + 1) * LANES])
    v_ref[...] = jnp.dot(x, w_ref[:, ATTN_WIDTH + KV_WIDTH:ATTN_WIDTH + 2 * KV_WIDTH],
                         preferred_element_type=F32)
    u_ref[...] = jnp.dot(x, w_ref[:, ATTN_WIDTH + 2 * KV_WIDTH:], preferred_element_type=F32)


def _layer_spec(shape, layer):
    return pl.BlockSpec((None,) + tuple(shape), lambda *a: (layer,) + (0,) * len(shape))


def _inproj(x, w_bf, layer, cos_t, sa_t, sb_t, tm):
    rows = x.shape[0]
    tab_blocks = cos_t.shape[0] // tm
    in_w = w_bf.shape[2]
    row_spec = lambda w: pl.BlockSpec((tm, w), lambda i: (i, 0))
    tab_spec = pl.BlockSpec((tm, LANES), lambda i: (i % tab_blocks, 0))
    return pl.pallas_call(
        _inproj_body,
        grid=(rows // tm,),
        in_specs=[row_spec(D_MODEL), _layer_spec((D_MODEL, in_w), layer), tab_spec, tab_spec, tab_spec],
        out_specs=[row_spec(ATTN_WIDTH), row_spec(KV_WIDTH), row_spec(KV_WIDTH), row_spec(SSM_WIDTH)],
        out_shape=[jax.ShapeDtypeStruct((rows, ATTN_WIDTH), BF16),
                   jax.ShapeDtypeStruct((rows, KV_WIDTH), F32),
                   jax.ShapeDtypeStruct((rows, KV_WIDTH), F32),
                   jax.ShapeDtypeStruct((rows, SSM_WIDTH), F32)],
        compiler_params=_cparams(("arbitrary",)),
        name="inproj_rope",
    )(x, w_bf, cos_t, sa_t, sb_t)


def _rope_tables(positions):
    half = HEAD_DIM // 2
    inv_freq = ROPE_THETA ** (-np.arange(half, dtype=np.float64) / half)
    ang = positions.astype(np.float64)[:, None] * inv_freq[None, :]
    lane = np.arange(LANES)
    cos = np.cos(ang)[:, lane % half]
    sin = np.sin(ang)[:, lane % half]
    first = (lane % HEAD_DIM) < half
    sa = np.where(first[None, :], -sin, 0.0)
    sb = np.where(first[None, :], 0.0, sin)
    return (jnp.asarray(cos, F32), jnp.asarray(sa, F32), jnp.asarray(sb, F32))


ATTN_BLOCKS = 8


def _rms(x, g):
    return x * lax.rsqrt(jnp.mean(jnp.square(x), axis=-1, keepdims=True) + RMS_EPS) * g


def _attn_prompt_body(sink_ref, q_ref, kp_ref, kc_ref, vp_ref, vc_ref, g_ref, o_ref, acc_ref, *, blocks_per_seq):
    i = pl.program_id(0)
    first_has_prev = ((i * ATTN_BLOCKS) % blocks_per_seq) > 0
    kk = jnp.concatenate([kp_ref[...], kc_ref[...]], axis=0).astype(BF16)
    vt = jnp.concatenate([vp_ref[...], vc_ref[...]], axis=0).T.astype(BF16)
    ncol = Q_PER_KV * WINDOW
    key = lax.broadcasted_iota(jnp.int32, (2 * WINDOW, ncol), 0)
    col = lax.broadcasted_iota(jnp.int32, (2 * WINDOW, ncol), 1)
    qry = col % WINDOW
    in_band = (key > qry) & (key <= qry + WINDOW)
    head = lax.broadcasted_iota(jnp.int32, (1, ncol), 1) // WINDOW
    q = q_ref[...] * (HEAD_DIM ** -0.5)
    scores = {}
    for b in range(ATTN_BLOCKS):
        for g in range(N_KV_HEADS):
            qg = jnp.concatenate([q[b * WINDOW:(b + 1) * WINDOW,
                                    (g * Q_PER_KV + j) * HEAD_DIM:(g * Q_PER_KV + j + 1) * HEAD_DIM]
                                  for j in range(Q_PER_KV)], axis=0)
            kh = kk[b * WINDOW:(b + 2) * WINDOW, g * HEAD_DIM:(g + 1) * HEAD_DIM]
            scores[b, g] = lax.dot_general(kh, qg, (((1,), (1,)), ((), ())), preferred_element_type=F32)
    for b in range(ATTN_BLOCKS):
        visible = in_band & ((key >= WINDOW) | first_has_prev) if b == 0 else in_band
        for g in range(N_KV_HEADS):
            st = jnp.where(visible, scores[b, g], -jnp.inf)
            sink = jnp.zeros((1, ncol), F32)
            for j in range(Q_PER_KV):
                sink = jnp.where(head == j, sink_ref[g * Q_PER_KV + j], sink)
            m = jnp.maximum(jnp.max(st, axis=0, keepdims=True), sink)
            p = jnp.exp(st - m)
            denom = jnp.sum(p, axis=0, keepdims=True) + jnp.exp(sink - m)
            ot = jnp.dot(vt[g * HEAD_DIM:(g + 1) * HEAD_DIM, b * WINDOW:(b + 2) * WINDOW], p.astype(BF16),
                         preferred_element_type=F32)
            ot = ot / denom
            for j in range(Q_PER_KV):
                h = g * Q_PER_KV + j
                acc_ref[h * HEAD_DIM:(h + 1) * HEAD_DIM, b * WINDOW:(b + 1) * WINDOW] = (
                    ot[:, j * WINDOW:(j + 1) * WINDOW])
    o_ref[...] = _rms(acc_ref[...].T, g_ref[...]).astype(BF16)


def _attn_prompt(sinks, q, k, v, g, seq):
    rows = q.shape[0]
    bps = seq // WINDOW
    assert bps % ATTN_BLOCKS == 0
    step_rows = ATTN_BLOCKS * WINDOW
    cur = lambda w: pl.BlockSpec((step_rows, w), lambda i: (i, 0))
    prev = lambda w: pl.BlockSpec((WINDOW, w), lambda i: (jnp.maximum(i * ATTN_BLOCKS - 1, 0), 0))
    return pl.pallas_call(
        functools.partial(_attn_prompt_body, blocks_per_seq=bps),
        grid=(rows // step_rows,),
        in_specs=[pl.BlockSpec(memory_space=pltpu.SMEM), cur(ATTN_WIDTH), prev(KV_WIDTH), cur(KV_WIDTH),
                  prev(KV_WIDTH), cur(KV_WIDTH), _full((1, ATTN_WIDTH))],
        out_specs=cur(ATTN_WIDTH),
        out_shape=jax.ShapeDtypeStruct((rows, ATTN_WIDTH), BF16),
        scratch_shapes=[pltpu.VMEM((ATTN_WIDTH, step_rows), F32)],
        compiler_params=_cparams(("arbitrary",)),
        name="attn_prompt",
    )(sinks, q, k, k, v, v, g)


SEQ_PER_STEP = 8


def _attn_sample_body(sink_ref, q_ref, kn_ref, vn_ref, ck_ref, cv_ref, g_ref, o_ref, nk_ref, nv_ref, acc_ref,
                      *, dec_seq):
    nq = SEQ_PER_STEP * dec_seq
    wb = ck_ref.shape[1]
    nkc = SEQ_PER_STEP * wb
    ck = ck_ref[...].reshape(nkc, KV_WIDTH)
    cv = cv_ref[...].reshape(nkc, KV_WIDTH)
    kn = kn_ref[...]
    vn = vn_ref[...]
    ckb = ck.astype(BF16)
    cvb = cv.astype(BF16)
    knb = kn.astype(BF16)
    vnb = vn.astype(BF16)
    q = q_ref[...]

    rows = Q_PER_KV * nq
    r = lax.broadcasted_iota(jnp.int32, (rows, nkc), 0) % nq
    c = lax.broadcasted_iota(jnp.int32, (rows, nkc), 1)
    vis_c = ((r // dec_seq) == (c // wb)) & ((c % wb) >= (r % dec_seq) + 1 + (wb - WINDOW))
    rn = lax.broadcasted_iota(jnp.int32, (rows, nq), 0) % nq
    cn = lax.broadcasted_iota(jnp.int32, (rows, nq), 1)
    vis_n = ((rn // dec_seq) == (cn // dec_seq)) & ((cn % dec_seq) <= (rn % dec_seq))
    hrow = lax.broadcasted_iota(jnp.int32, (rows, 1), 0) // nq

    dn = (((1,), (1,)), ((), ()))
    for g in range(N_KV_HEADS):
        qs = jnp.concatenate([q[:, (g * Q_PER_KV + j) * HEAD_DIM:(g * Q_PER_KV + j + 1) * HEAD_DIM]
                              for j in range(Q_PER_KV)], axis=0)
        sl = slice(g * HEAD_DIM, (g + 1) * HEAD_DIM)
        sc = lax.dot_general(qs, ckb[:, sl], dn, preferred_element_type=F32) * (HEAD_DIM ** -0.5)
        sn = lax.dot_general(qs, knb[:, sl], dn, preferred_element_type=F32) * (HEAD_DIM ** -0.5)
        sc = jnp.where(vis_c, sc, -jnp.inf)
        sn = jnp.where(vis_n, sn, -jnp.inf)
        sink = jnp.zeros((rows, 1), F32)
        for j in range(Q_PER_KV):
            sink = jnp.where(hrow == j, sink_ref[g * Q_PER_KV + j], sink)
        m = jnp.maximum(jnp.maximum(jnp.max(sc, axis=-1, keepdims=True), jnp.max(sn, axis=-1, keepdims=True)), sink)
        pc = jnp.exp(sc - m)
        pn = jnp.exp(sn - m)
        denom = jnp.sum(pc, axis=-1, keepdims=True) + jnp.sum(pn, axis=-1, keepdims=True) + jnp.exp(sink - m)
        o = (jnp.dot(pc.astype(BF16), cvb[:, sl], preferred_element_type=F32)
             + jnp.dot(pn.astype(BF16), vnb[:, sl], preferred_element_type=F32)) / denom
        for j in range(Q_PER_KV):
            h = g * Q_PER_KV + j
            acc_ref[:, h * HEAD_DIM:(h + 1) * HEAD_DIM] = o[j * nq:(j + 1) * nq, :]
    o_ref[...] = _rms(acc_ref[...], g_ref[...]).astype(BF16)

    keep = wb - dec_seq
    for s in range(SEQ_PER_STEP):
        nk_ref[s, 0:keep, :] = ck_ref[s, dec_seq:wb, :]
        nv_ref[s, 0:keep, :] = cv_ref[s, dec_seq:wb, :]
        nk_ref[s, keep:wb, :] = kn[s * dec_seq:(s + 1) * dec_seq, :]
        nv_ref[s, keep:wb, :] = vn[s * dec_seq:(s + 1) * dec_seq, :]


def _attn_sample(sinks, q, kn, vn, ck, cv, g, dec_seq, layer):
    rows = q.shape[0]
    _, nseq, wb, _ = ck.shape
    nq = SEQ_PER_STEP * dec_seq
    rowb = lambda w: pl.BlockSpec((nq, w), lambda i: (i, 0))
    cin = pl.BlockSpec((None, SEQ_PER_STEP, wb, KV_WIDTH), lambda i: (layer, i, 0, 0))
    cb = pl.BlockSpec((SEQ_PER_STEP, wb, KV_WIDTH), lambda i: (i, 0, 0))
    return pl.pallas_call(
        functools.partial(_attn_sample_body, dec_seq=dec_seq),
        grid=(nseq // SEQ_PER_STEP,),
        in_specs=[pl.BlockSpec(memory_space=pltpu.SMEM), rowb(ATTN_WIDTH), rowb(KV_WIDTH), rowb(KV_WIDTH),
                  cin, cin, _full((1, ATTN_WIDTH))],
        out_specs=[rowb(ATTN_WIDTH), cb, cb],
        out_shape=[jax.ShapeDtypeStruct((rows, ATTN_WIDTH), BF16),
                   jax.ShapeDtypeStruct(ck.shape[1:], F32), jax.ShapeDtypeStruct(cv.shape[1:], F32)],
        scratch_shapes=[pltpu.VMEM((nq, ATTN_WIDTH), F32)],
        compiler_params=_cparams(("arbitrary",)),
        name="attn_sample",
    )(sinks, q, kn, vn, ck, cv, g)


def _ssm_prompt_body(ua_ref, uc_ref, b_ref, c_ref, lr_ref, li_ref, pr_ref, pi_ref, d_ref, y_ref, hl_ref,
                     uperm_ref, bu0, bu1, hb0, hb1, yp_ref, carry_ref, *, nchunks, ntc):
    k = pl.program_id(0)
    bus = (bu0, bu1)
    hbs = (hb0, hb1)
    lanes_per_chunk = CH_IN // LANES
    seg = lambda s: slice(s * SCAN_SEG, (s + 1) * SCAN_SEG)
    lane = lambda c: slice(c * LANES, (c + 1) * LANES)

    def stage_a(item, slot):
        cc = item % SSM_CHUNKS
        for c in range(lanes_per_chunk):
            for s in range(SUBLANES):
                uperm_ref[c, pl.ds(s, SCAN_SEG, stride=SUBLANES), :] = ua_ref[seg(s), lane(c)]
        up = jnp.concatenate([uperm_ref[c] for c in range(lanes_per_chunk)], axis=1)
        bus[slot][...] = _dot_nt(up.astype(BF16), b_ref[cc])

    def stage_b(item, slot):
        cc = item % SSM_CHUNKS
        tc = (item // SSM_CHUNKS) % ntc
        bu_ref = bus[slot]
        hb_ref = hbs[slot]
        ar = jnp.broadcast_to(lr_ref[cc], (SUBLANES, CH_STATES))
        ai = jnp.broadcast_to(li_ref[cc], (SUBLANES, CH_STATES))

        def advance(h, j):
            hr, hi = h
            br = bu_ref[j * SUBLANES:(j + 1) * SUBLANES, 0:CH_STATES]
            bi = bu_ref[j * SUBLANES:(j + 1) * SUBLANES, CH_STATES:2 * CH_STATES]
            return ar * hr + (br - ai * hi), ar * hi + (bi + ai * hr)

        zero = jnp.zeros((SUBLANES, CH_STATES), F32)
        h = (zero, zero)
        for j in range(SCAN_SEG):
            h = advance(h, j)
        fr, fi = h

        cst = jnp.where(tc == 0, 0.0, carry_ref[cc])
        c_r = cst[0:1, :]
        c_i = cst[1:2, :]
        lsr = pr_ref[cc]
        lsi = pi_ref[cc]
        rows_r, rows_i = [], []
        for s in range(SUBLANES):
            rows_r.append(c_r)
            rows_i.append(c_i)
            n_r = fr[s:s + 1, :] + (lsr * c_r - lsi * c_i)
            n_i = fi[s:s + 1, :] + (lsr * c_i + lsi * c_r)
            c_r, c_i = n_r, n_i
        carry_ref[cc] = jnp.concatenate([c_r, c_i], axis=0)
        hl_ref[0, cc] = jnp.concatenate([c_r, c_i], axis=0)

        h = (jnp.concatenate(rows_r, axis=0), jnp.concatenate(rows_i, axis=0))
        for jj in range(SCAN_SEG // 2):
            h1 = advance(h, 2 * jj)
            h = advance(h1, 2 * jj + 1)
            rows = slice(2 * jj * SUBLANES, (2 * jj + 2) * SUBLANES)
            hb_ref[rows, 0:CH_STATES] = jnp.concatenate([h1[0], h[0]], axis=0).astype(BF16)
            hb_ref[rows, CH_STATES:2 * CH_STATES] = jnp.concatenate([h1[1], h[1]], axis=0).astype(BF16)

    def stage_c(item, slot):
        cc = item % SSM_CHUNKS
        yp = _dot_nt(hbs[slot][...], c_ref[cc])
        d = d_ref[cc]
        for c in range(lanes_per_chunk):
            yp_ref[c] = yp[:, lane(c)]
        for s in range(SUBLANES):
            for c in range(lanes_per_chunk):
                y_ref[seg(s), lane(c)] = (yp_ref[c, pl.ds(s, SCAN_SEG, stride=SUBLANES), :]
                                          + d[:, lane(c)] * uc_ref[seg(s), lane(c)])

    def run(do_a, do_b, do_c, parity):
        if do_a:
            stage_a(k, parity)
        if do_b:
            stage_b(k - 1, 1 - parity)
        if do_c:
            stage_c(k - 2, parity)

    @pl.when(k == 0)
    def _():
        carry_ref[...] = jnp.zeros(carry_ref.shape, F32)
        run(True, False, False, 0)

    @pl.when(k == 1)
    def _():
        run(True, True, False, 1)

    for parity in range(2):
        @pl.when((k >= 2) & (k < nchunks) & (k % 2 == parity))
        def _():
            run(True, True, True, parity)

    @pl.when(k == nchunks)
    def _():
        run(False, True, True, nchunks % 2)

    @pl.when(k == nchunks + 1)
    def _():
        run(False, False, True, (nchunks + 1) % 2)


def _ssm_prompt(u, sp, layer, nbatch, seq):
    rows = u.shape[0]
    ntc = seq // SCAN_T
    nchunks = nbatch * ntc * SSM_CHUNKS
    assert nchunks >= 2
    item_a = lambda k: jnp.minimum(k, nchunks - 1)
    item_b = lambda k: jnp.clip(k - 1, 0, nchunks - 1)
    item_c = lambda k: jnp.clip(k - 2, 0, nchunks - 1)
    chunk_block = lambda item: (item // SSM_CHUNKS, item % SSM_CHUNKS)
    y, hl = pl.pallas_call(
        functools.partial(_ssm_prompt_body, nchunks=nchunks, ntc=ntc),
        grid=(nchunks + 2,),
        in_specs=[pl.BlockSpec((SCAN_T, CH_IN), lambda k: chunk_block(item_a(k))),
                  pl.BlockSpec((SCAN_T, CH_IN), lambda k: chunk_block(item_c(k))),
                  ] + [_layer_spec(sp[n].shape[1:], layer) for n in ("B", "C", "lr", "li", "pr", "pi", "d")],
        out_specs=[pl.BlockSpec((SCAN_T, CH_IN), lambda k: chunk_block(item_c(k))),
                   pl.BlockSpec((1, SSM_CHUNKS, 2, CH_STATES),
                                lambda k: (item_b(k) // (ntc * SSM_CHUNKS), 0, 0, 0))],
        out_shape=[jax.ShapeDtypeStruct((rows, SSM_WIDTH), F32),
                   jax.ShapeDtypeStruct((nbatch, SSM_CHUNKS, 2, CH_STATES), F32)],
        scratch_shapes=[pltpu.VMEM((CH_IN // LANES, SCAN_T, LANES), F32),
                        pltpu.VMEM((SCAN_T, 2 * CH_STATES), F32), pltpu.VMEM((SCAN_T, 2 * CH_STATES), F32),
                        pltpu.VMEM((SCAN_T, 2 * CH_STATES), BF16), pltpu.VMEM((SCAN_T, 2 * CH_STATES), BF16),
                        pltpu.VMEM((CH_IN // LANES, SCAN_T, LANES), F32),
                        pltpu.VMEM((SSM_CHUNKS, 2, CH_STATES), F32)],
        compiler_params=_cparams(("arbitrary",)),
        name="ssm_prompt",
    )(u, u, sp["B"], sp["C"], sp["lr"], sp["li"], sp["pr"], sp["pi"], sp["d"])
    return y, hl


def _ssm_sample_body(u_ref, h0_ref, b_ref, c_ref, lr_ref, li_ref, d_ref, y_ref, ho_ref, bu_ref, *, nseq, dec_seq):
    nl = CH_STATES // LANES
    for k in range(SSM_CHUNKS):
        uk = u_ref[:, k * CH_IN:(k + 1) * CH_IN]
        bu = _dot_nt(uk.astype(BF16), b_ref[k])
        for c in range(2 * nl):
            bu_ref[c] = bu[:, c * LANES:(c + 1) * LANES]
        base = k * 2 * CH_STATES
        hr = h0_ref[:, base:base + CH_STATES]
        hi = h0_ref[:, base + CH_STATES:base + 2 * CH_STATES]
        ar = lr_ref[k]
        ai = li_ref[k]
        for t in range(dec_seq):
            br = jnp.concatenate([bu_ref[c, pl.ds(t, nseq, stride=dec_seq), :] for c in range(nl)], axis=1)
            bi = jnp.concatenate([bu_ref[nl + c, pl.ds(t, nseq, stride=dec_seq), :] for c in range(nl)], axis=1)
            hr, hi = ar * hr + (br - ai * hi), ar * hi + (bi + ai * hr)
            for c in range(nl):
                bu_ref[c, pl.ds(t, nseq, stride=dec_seq), :] = hr[:, c * LANES:(c + 1) * LANES]
                bu_ref[nl + c, pl.ds(t, nseq, stride=dec_seq), :] = hi[:, c * LANES:(c + 1) * LANES]
        ho_ref[:, base:base + CH_STATES] = hr
        ho_ref[:, base + CH_STATES:base + 2 * CH_STATES] = hi
        hall = jnp.concatenate([bu_ref[c] for c in range(2 * nl)], axis=1)
        y_ref[:, k * CH_IN:(k + 1) * CH_IN] = (
            _dot_nt(hall.astype(BF16), c_ref[k]) + d_ref[k] * uk)


def _ssm_sample(u, h0, sp, layer, nseq, dec_seq):
    rows = u.shape[0]
    return pl.pallas_call(
        functools.partial(_ssm_sample_body, nseq=nseq, dec_seq=dec_seq),
        grid=(1,),
        in_specs=[_full(u.shape), _full(h0.shape)] + [_layer_spec(sp[n].shape[1:], layer)
                                                      for n in ("B", "C", "lr", "li", "d")],
        out_specs=[_full((rows, SSM_WIDTH)), _full(h0.shape)],
        out_shape=[jax.ShapeDtypeStruct((rows, SSM_WIDTH), F32), jax.ShapeDtypeStruct(h0.shape, F32)],
        scratch_shapes=[pltpu.VMEM((2 * CH_STATES // LANES, rows, LANES), F32)],
        compiler_params=_cparams(("arbitrary",)),
        name="ssm_sample",
    )(u, h0, sp["B"], sp["C"], sp["lr"], sp["li"], sp["d"])


def _ssm_params(lambda_re, lambda_im, log_dt, b_re, b_im, c_re, c_im, d):
    lam = lax.complex(jnp.minimum(lambda_re.astype(F32), -1e-4), lambda_im.astype(F32))
    dt = jnp.exp(log_dt.astype(F32))[:, None]
    lam_dt = lam * dt
    lam_bar = jnp.exp(lam_dt)
    b = lax.complex(b_re.astype(F32), b_im.astype(F32))
    b_bar = ((lam_bar - 1.0) / lam)[..., None] * b
    rows_b = lambda part: part.reshape(SSM_CHUNKS, CH_STATES, SSM_GROUP)
    rows_c = lambda part: part.reshape(SSM_CHUNKS, CH_IN, SSM_STATE)
    pw = jnp.exp(lam_dt * float(SCAN_SEG)).reshape(SSM_CHUNKS, 1, CH_STATES)
    lb = lam_bar.reshape(SSM_CHUNKS, 1, CH_STATES)
    return dict(b_re=rows_b(jnp.real(b_bar)), b_im=rows_b(jnp.imag(b_bar)),
                c_re=rows_c(c_re.astype(F32)), c_im_neg=rows_c(-c_im.astype(F32)),
                lr=jnp.real(lb), li=jnp.imag(lb), pr=jnp.real(pw), pi=jnp.imag(pw),
                d=d.astype(F32).reshape(SSM_CHUNKS, 1, CH_IN))


def _ssm_pack_body(bre_ref, bim_ref, cre_ref, cimn_ref, eb_ref, ec_ref, b_out, c_out):
    def expand(rows_ref, onehot_ref, row_group, col_group):
        t = jnp.dot(rows_ref[...].astype(BF16), onehot_ref[...], preferred_element_type=F32)
        own = lax.broadcasted_iota(jnp.int32, t.shape, 0) // row_group
        col = lax.broadcasted_iota(jnp.int32, t.shape, 1) // col_group
        return jnp.where(own == col, t, 0.0).astype(BF16)

    b_out[0:CH_STATES, :] = expand(bre_ref, eb_ref, SSM_STATE, SSM_GROUP)
    b_out[CH_STATES:2 * CH_STATES, :] = expand(bim_ref, eb_ref, SSM_STATE, SSM_GROUP)
    c_out[:, 0:CH_STATES] = expand(cre_ref, ec_ref, SSM_GROUP, SSM_STATE)
    c_out[:, CH_STATES:2 * CH_STATES] = expand(cimn_ref, ec_ref, SSM_GROUP, SSM_STATE)


def _ssm_pack(sp):
    depth = sp["b_re"].shape[0]
    eb = jnp.asarray(np.tile(np.eye(SSM_GROUP, dtype=np.float32), (1, CH_GROUPS)), BF16)
    ec = jnp.asarray(np.tile(np.eye(SSM_STATE, dtype=np.float32), (1, CH_GROUPS)), BF16)
    item = lambda r, c: pl.BlockSpec((None, None, r, c), lambda l, k: (l, k, 0, 0))
    bmat, cmat = pl.pallas_call(
        _ssm_pack_body,
        grid=(depth, SSM_CHUNKS),
        in_specs=[item(CH_STATES, SSM_GROUP), item(CH_STATES, SSM_GROUP), item(CH_IN, SSM_STATE),
                  item(CH_IN, SSM_STATE), _full(eb.shape), _full(ec.shape)],
        out_specs=[item(2 * CH_STATES, CH_IN), item(CH_IN, 2 * CH_STATES)],
        out_shape=[jax.ShapeDtypeStruct((depth, SSM_CHUNKS, 2 * CH_STATES, CH_IN), BF16),
                   jax.ShapeDtypeStruct((depth, SSM_CHUNKS, CH_IN, 2 * CH_STATES), BF16)],
        compiler_params=_cparams(("arbitrary", "arbitrary")),
        name="ssm_pack",
    )(sp["b_re"], sp["b_im"], sp["c_re"], sp["c_im_neg"], eb, ec)
    return dict(sp, B=bmat, C=cmat)


def _pack_state(re, im):
    n = re.shape[0]
    st = jnp.stack([re.reshape(n, SSM_CHUNKS, CH_STATES), im.reshape(n, SSM_CHUNKS, CH_STATES)], axis=2)
    return st.reshape(n, SSM_CHUNKS * 2 * CH_STATES).astype(F32)


def _unpack_state(h, n):
    st = h.reshape(n, SSM_CHUNKS, 2, CH_GROUPS, SSM_STATE)
    return (st[:, :, 0].reshape(n, SSM_GROUPS, SSM_STATE), st[:, :, 1].reshape(n, SSM_GROUPS, SSM_STATE))


def _glu_rms(y, w_glu, g):
    a = jax.nn.gelu(y)
    z = a * jax.nn.sigmoid(jnp.dot(a.astype(BF16), w_glu, preferred_element_type=F32))
    return _rms(z, g).astype(BF16)


def _layer_norm(y, g, b):
    mu = jnp.mean(y, axis=-1, keepdims=True)
    yc = y - mu
    var = jnp.mean(jnp.square(yc), axis=-1, keepdims=True)
    return yc * lax.rsqrt(var + LN_EPS) * g + b


OUTPROJ_SUB = 256


def _outproj_body(*refs, aliased, nblk):
    if aliased:
        refs = refs[1:]
    a_ref, y_ref, x_ref, w_ref, wglu_ref, sg_ref, g_ref, b_ref, wrc_ref, br_ref, o_ref, lg_ref = refs
    i = pl.program_id(0)

    @pl.when(i < nblk)
    def _():
        tm = a_ref.shape[0]
        sub = min(tm, OUTPROJ_SUB)
        for r0 in range(0, tm, sub):
            rows = slice(r0, r0 + sub)
            sn = _glu_rms(y_ref[rows, :], wglu_ref[...], sg_ref[...])
            acc = jnp.dot(a_ref[rows, :], w_ref[0:ATTN_WIDTH, :], preferred_element_type=F32)
            acc = acc + jnp.dot(sn, w_ref[ATTN_WIDTH:, :], preferred_element_type=F32)
            x1 = _layer_norm(DEEPNORM_ALPHA * x_ref[rows, :] + acc, g_ref[...], b_ref[...])
            o_ref[rows, :] = x1
            hh = jnp.dot(x1.astype(BF16), wrc_ref[...], preferred_element_type=F32)
            lg_ref[rows, :] = hh[:, 0:LANES] + hh[:, LANES:] + br_ref[...]

    @pl.when(i >= nblk)
    def _():
        o_ref[...] = jnp.zeros(o_ref.shape, F32)


def _outproj(an, y, x, w_bf, w_glu_bf, layer, ssm_g, g, b, wr, br, tm, total_rows, row_block0, x1_buf=None):
    rows = an.shape[0]
    aliased = x1_buf is not None
    nblk = rows // tm
    tail = 0 if aliased or rows == total_rows else 1
    blk = lambda i: jnp.minimum(i, nblk - 1)
    half = lambda: pl.BlockSpec((tm, ATTN_WIDTH), lambda i: (blk(i), 0))
    in_specs = [half(), half(), pl.BlockSpec((tm, D_MODEL), lambda i: (blk(i), 0)),
                _layer_spec((D_MODEL, D_MODEL), layer), _layer_spec((SSM_WIDTH, SSM_WIDTH), layer),
                _full((1, SSM_WIDTH)), _full((1, D_MODEL)), _full((1, D_MODEL)),
                _full((D_MODEL, 2 * LANES)), _full((1, LANES))]
    wr_hi = wr.astype(BF16)
    wr_lo = (wr - wr_hi.astype(F32)).astype(BF16)
    args = [an, y, x, w_bf, w_glu_bf, ssm_g, g, b, jnp.concatenate([wr_hi, wr_lo], axis=1), br]
    if aliased:
        in_specs = [pl.BlockSpec(memory_space=pl.ANY)] + in_specs
        args = [x1_buf] + args
    return pl.pallas_call(
        functools.partial(_outproj_body, aliased=aliased, nblk=nblk),
        grid=(nblk + tail,),
        in_specs=in_specs,
        out_specs=[pl.BlockSpec((tm, D_MODEL), lambda i: (i + row_block0, 0)),
                   pl.BlockSpec((tm, LANES), lambda i: (blk(i), 0))],
        out_shape=[jax.ShapeDtypeStruct((total_rows, D_MODEL), F32), jax.ShapeDtypeStruct((rows, LANES), F32)],
        input_output_aliases={0: 0} if aliased else {},
        compiler_params=_cparams(("arbitrary",)),
        name="outproj_ln1",
    )(*args)


ROUTE_T = 640
R_E0, R_E1, R_W0, R_W1, R_P0, R_P1 = range(6)


def _route_select(x):
    lane = lax.broadcasted_iota(jnp.int32, x.shape, 1)
    big = jnp.int32(1 << 20)
    neg = -jnp.inf
    gmask = lane < N_EXPERT_GROUPS
    gm = jnp.max(jnp.where(gmask, x, neg), axis=-1, keepdims=True)
    grp = jnp.min(jnp.where(gmask & (x == gm), lane, big), axis=-1, keepdims=True)
    p_grp = 1.0 / jnp.sum(jnp.where(gmask, jnp.exp(x - gm), 0.0), axis=-1, keepdims=True)
    lo = N_EXPERT_GROUPS + grp * EXPERTS_PER_GROUP
    emask = (lane >= lo) & (lane < lo + EXPERTS_PER_GROUP)
    v1 = jnp.max(jnp.where(emask, x, neg), axis=-1, keepdims=True)
    i1 = jnp.min(jnp.where(emask & (x == v1), lane, big), axis=-1, keepdims=True)
    emask2 = emask & (lane != i1)
    v2 = jnp.max(jnp.where(emask2, x, neg), axis=-1, keepdims=True)
    i2 = jnp.min(jnp.where(emask2 & (x == v2), lane, big), axis=-1, keepdims=True)
    e2 = jnp.exp(v2 - v1)
    w1 = p_grp / (1.0 + e2)
    w2 = p_grp * e2 / (1.0 + e2)
    return (i1 - N_EXPERT_GROUPS).astype(F32), (i2 - N_EXPERT_GROUPS).astype(F32), w1, w2


def _route_plan_body(lg_ref, ltri_ref, utri_ref, o_ref, meta_ref, rec_ref, run_ref):
    ph = pl.program_id(0)
    j = pl.program_id(1)
    tm = lg_ref.shape[0]
    lane = lax.broadcasted_iota(jnp.int32, (tm, LANES), 1)
    lanef = lane.astype(F32)
    r0 = pl.multiple_of(j * tm, tm)

    @pl.when((ph == 0) & (j == 0))
    def _():
        run_ref[...] = jnp.zeros(run_ref.shape, F32)

    @pl.when(ph == 0)
    def _():
        e0, e1, w0, w1 = _route_select(lg_ref[...])
        oh0 = (lanef == e0).astype(F32)
        oh1 = (lanef == e1).astype(F32)
        oh2 = oh0 + oh1
        before = jnp.dot(ltri_ref[...], oh2.astype(BF16), preferred_element_type=F32) + run_ref[0:1, :]
        rk0 = jnp.sum(before * oh0, axis=-1, keepdims=True)
        rk1 = jnp.sum(before * oh1, axis=-1, keepdims=True)
        run_ref[...] = run_ref[...] + jnp.sum(oh2, axis=0, keepdims=True)
        rec = jnp.where(lane == R_E0, e0, 0.0)
        rec = jnp.where(lane == R_E1, e1, rec)
        rec = jnp.where(lane == R_W0, w0, rec)
        rec = jnp.where(lane == R_W1, w1, rec)
        rec = jnp.where(lane == R_P0, rk0, rec)
        rec = jnp.where(lane == R_P1, rk1, rec)
        rec_ref[pl.ds(r0, tm), :] = rec

    @pl.when(ph == 1)
    def _():
        counts = run_ref[...]
        nblk = jnp.floor((counts + (MOE_BLOCK - 1)) * (1.0 / MOE_BLOCK))
        end_blk = jnp.dot(nblk.astype(BF16), utri_ref[...], preferred_element_type=F32)
        start_row = (end_blk - nblk) * MOE_BLOCK
        rec = rec_ref[pl.ds(r0, tm), :]
        oh0 = (lanef == rec[:, R_E0:R_E0 + 1]).astype(F32)
        oh1 = (lanef == rec[:, R_E1:R_E1 + 1]).astype(F32)
        p0 = jnp.sum(oh0 * start_row[0:1, :], axis=-1, keepdims=True) + rec[:, R_P0:R_P0 + 1]
        p1 = jnp.sum(oh1 * start_row[0:1, :], axis=-1, keepdims=True) + rec[:, R_P1:R_P1 + 1]
        out = jnp.where(lane == R_P0, p0, rec)
        out = jnp.where(lane == R_P1, p1, out)
        o_ref[...] = out

        @pl.when(j == 0)
        def _():
            mrow = lax.broadcasted_iota(jnp.int32, meta_ref.shape, 0)
            meta_ref[...] = jnp.where(mrow == 1, counts, end_blk)


def _route_plan(logits):
    rows = logits.shape[0]
    tm = ROUTE_T
    ltri = jnp.asarray(np.tril(np.ones((tm, tm), np.float32), -1), BF16)
    utri = jnp.asarray(np.triu(np.ones((LANES, LANES), np.float32)), BF16)
    mrows = SUBLANES
    return pl.pallas_call(
        _route_plan_body,
        grid=(2, rows // tm),
        in_specs=[pl.BlockSpec((tm, LANES), lambda p, j: (j * (1 - p), 0)),
                  pl.BlockSpec((tm, tm), lambda p, j: (0, 0)),
                  pl.BlockSpec((LANES, LANES), lambda p, j: (0, 0))],
        out_specs=[pl.BlockSpec((tm, LANES), lambda p, j: (j * p, 0)),
                   pl.BlockSpec((mrows, LANES), lambda p, j: (0, 0))],
        out_shape=[jax.ShapeDtypeStruct((rows, LANES), F32), jax.ShapeDtypeStruct((mrows, LANES), F32)],
        scratch_shapes=[pltpu.VMEM((rows, LANES), F32), pltpu.VMEM((SUBLANES, LANES), F32)],
        compiler_params=_cparams(("arbitrary", "arbitrary")),
        name="route_plan",
    )(logits, ltri, utri)


U32 = jnp.uint32
PACKED_W = D_MODEL // 2


def _pack_bf16_pairs(y):
    bits = lax.bitcast_convert_type(y.astype(BF16).astype(F32), U32)
    return bits[:, 0:PACKED_W] | (bits[:, PACKED_W:] >> 16)


def _unpack_bf16_pairs(p):
    hi = lax.bitcast_convert_type(p & jnp.uint32(0xFFFF0000), F32)
    lo = lax.bitcast_convert_type(p << 16, F32)
    return jnp.concatenate([hi, lo], axis=1)


def _build_rowmap(pos_ref, bend_ref, cnt_ref, o_ref):
    unroll = 8
    ntok = pos_ref.shape[0] // 2

    def zero_padding(e, _):
        first_blk = jnp.where(e == 0, 0, bend_ref[jnp.maximum(e - 1, 0)])

        def zero(r, _):
            o_ref[r] = 0
            return 0
        lax.fori_loop(first_blk * MOE_BLOCK + cnt_ref[e], bend_ref[e] * MOE_BLOCK, zero, 0)
        return 0
    lax.fori_loop(0, N_EXPERTS, zero_padding, 0)

    def fill(c, _):
        for u in range(unroll):
            t = c * unroll + u
            o_ref[pos_ref[2 * t]] = t
            o_ref[pos_ref[2 * t + 1]] = t
        return 0
    lax.fori_loop(0, ntok // unroll, fill, 0)


GATHER_SLOTS = 4


def _moe_body(bend_ref, cnt_ref, pos_ref, x_hbm, wg_ref, wu_ref, wd_ref, ys_hbm, *scratch, nblocks):
    xbufs = scratch[:GATHER_SLOTS]
    ybufs = scratch[GATHER_SLOTS:GATHER_SLOTS + 2]
    gsem, osem, wg_bf, wu_bf, wd_bf, row_tok_ref = scratch[GATHER_SLOTS + 2:]
    ahead = GATHER_SLOTS - 1
    e = pl.program_id(0)
    nused = bend_ref[N_EXPERTS - 1]
    b1 = bend_ref[e]
    b0 = jnp.where(e == 0, 0, bend_ref[jnp.maximum(e - 1, 0)])

    def gather_start(blk, s):
        base = blk * MOE_BLOCK
        for r in range(MOE_BLOCK):
            tok = row_tok_ref[base + r]
            pltpu.make_async_copy(x_hbm.at[pl.ds(tok, 1), :], xbufs[s].at[pl.ds(r, 1), :],
                                  gsem.at[s]).start(priority=r % 2)

    def gather_wait(s):
        pltpu.make_async_copy(x_hbm.at[pl.ds(0, MOE_BLOCK), :], xbufs[s], gsem.at[s]).wait()

    def out_copy(blk, s):
        row0 = pl.multiple_of(blk * MOE_BLOCK, MOE_BLOCK)
        return pltpu.make_async_copy(ybufs[s], ys_hbm.at[pl.ds(row0, MOE_BLOCK), :], osem.at[s])

    @pl.when(e == 0)
    def _():
        _build_rowmap(pos_ref, bend_ref, cnt_ref, row_tok_ref)
        for k in range(ahead):
            gather_start(k, k)

    @pl.when(b1 > b0)
    def _():
        wg_bf[...] = wg_ref[...].astype(BF16)
        wu_bf[...] = wu_ref[...].astype(BF16)
        wd_bf[...] = wd_ref[...].astype(BF16)

        def do_block(b, s):
            so = s % 2
            gather_wait(s)

            @pl.when(b >= 2)
            def _():
                out_copy(b - 2, so).wait()

            gather_start(jnp.minimum(b + ahead, nused - 1), (s + ahead) % GATHER_SLOTS)
            x = xbufs[s][...].astype(BF16)
            hg = jnp.dot(x, wg_bf[...], preferred_element_type=F32)
            hu = jnp.dot(x, wu_bf[...], preferred_element_type=F32)
            h = (jax.nn.silu(hg) * hu).astype(BF16)
            ybufs[so][...] = _pack_bf16_pairs(jnp.dot(h, wd_bf[...], preferred_element_type=F32))
            out_copy(b, so).start()

        def body(b, _):
            for s in range(GATHER_SLOTS):
                @pl.when(b % GATHER_SLOTS == s)
                def _():
                    do_block(b, s)
            return 0

        lax.fori_loop(b0, b1, body, 0)

    @pl.when(e == N_EXPERTS - 1)
    def _():
        last = nused - 1
        for s in range(GATHER_SLOTS):
            @pl.when(last % GATHER_SLOTS == s)
            def _():
                for k in range(1, GATHER_SLOTS):
                    gather_wait((s + k) % GATHER_SLOTS)
        out_copy(0, 0).wait()
        out_copy(0, 1).wait()
        yb0 = ybufs[0]
        yb0[...] = jnp.zeros(yb0.shape, U32)

        def zstart(b, _):
            out_copy(b, 0).start()
            return 0

        def zwait(b, _):
            out_copy(b, 0).wait()
            return 0

        lax.fori_loop(nused, nblocks, zstart, 0)
        lax.fori_loop(nused, nblocks, zwait, 0)


def _moe(bend, cnt, pos, x1, w_gate, w_up, w_down, layer, nblocks):
    assert 2 * x1.shape[0] > GATHER_SLOTS * MOE_BLOCK
    wspec = lambda a, b: pl.BlockSpec((None, None, a, b), lambda e, be, cn, ps: (layer, e, 0, 0))
    grid_spec = pltpu.PrefetchScalarGridSpec(
        num_scalar_prefetch=3,
        grid=(N_EXPERTS,),
        in_specs=[pl.BlockSpec(memory_space=pl.ANY), wspec(D_MODEL, EXPERT_FF), wspec(D_MODEL, EXPERT_FF),
                  wspec(EXPERT_FF, D_MODEL)],
        out_specs=pl.BlockSpec(memory_space=pl.ANY),
        scratch_shapes=([pltpu.VMEM((MOE_BLOCK, D_MODEL), F32) for _ in range(GATHER_SLOTS)]
                        + [pltpu.VMEM((MOE_BLOCK, PACKED_W), U32) for _ in range(2)]
                        + [pltpu.SemaphoreType.DMA((GATHER_SLOTS,)), pltpu.SemaphoreType.DMA((2,)),
                           pltpu.VMEM((D_MODEL, EXPERT_FF), BF16),
                           pltpu.VMEM((D_MODEL, EXPERT_FF), BF16),
                           pltpu.VMEM((EXPERT_FF, D_MODEL), BF16),
                           pltpu.SMEM((nblocks * MOE_BLOCK,), jnp.int32)]),
    )
    return pl.pallas_call(
        functools.partial(_moe_body, nblocks=nblocks),
        grid_spec=grid_spec,
        out_shape=jax.ShapeDtypeStruct((nblocks * MOE_BLOCK, PACKED_W), U32),
        compiler_params=_cparams(("arbitrary",), bounds_checks=False),
        name="moe_experts",
    )(bend, cnt, pos, x1, w_gate, w_up, w_down)


COMB_T = 128


def _combine_body(pos_ref, ys_hbm, x_ref, rt_ref, g_ref, b_ref, op_ref, os_ref, *scratch, n_prompt_blocks):
    i = pl.program_id(0)
    nsteps = pl.num_programs(0)
    bufs = tuple((scratch[2 * s], scratch[2 * s + 1]) for s in range(GATHER_SLOTS))
    sem = scratch[2 * GATHER_SLOTS]
    ahead = GATHER_SLOTS - 1

    def gather_start(step, s):
        base = step * (2 * COMB_T)
        for r in range(COMB_T):
            for k in range(2):
                p = pos_ref[base + 2 * r + k]
                pltpu.make_async_copy(ys_hbm.at[pl.ds(p, 1), :], bufs[s][k].at[pl.ds(r, 1), :],
                                      sem.at[s]).start(priority=k)

    def gather_wait(s):
        for k in range(2):
            pltpu.make_async_copy(ys_hbm.at[pl.ds(0, COMB_T), :], bufs[s][k], sem.at[s]).wait()

    @pl.when(i == 0)
    def _():
        for k in range(ahead):
            gather_start(k, k)

    def do_tile(s):
        gather_wait(s)
        gather_start(jnp.minimum(i + ahead, nsteps - 1), (s + ahead) % GATHER_SLOTS)
        rt = rt_ref[...]
        ff = (rt[:, R_W0:R_W0 + 1] * _unpack_bf16_pairs(bufs[s][0][...])
              + rt[:, R_W1:R_W1 + 1] * _unpack_bf16_pairs(bufs[s][1][...]))
        out = _layer_norm(DEEPNORM_ALPHA * x_ref[...] + ff, g_ref[...], b_ref[...])

        @pl.when(i < n_prompt_blocks)
        def _():
            op_ref[...] = out

        @pl.when(i >= n_prompt_blocks)
        def _():
            os_ref[...] = out

        @pl.when(i == nsteps - 1)
        def _():
            for k in range(1, GATHER_SLOTS):
                gather_wait((s + k) % GATHER_SLOTS)

    for s in range(GATHER_SLOTS):
        @pl.when(i % GATHER_SLOTS == s)
        def _():
            do_tile(s)


def _combine(pos, ys, x1, route, g, b, n_prompt_rows):
    rows = x1.shape[0]
    npb = n_prompt_rows // COMB_T
    assert rows // COMB_T >= GATHER_SLOTS
    grid_spec = pltpu.PrefetchScalarGridSpec(
        num_scalar_prefetch=1,
        grid=(rows // COMB_T,),
        in_specs=[pl.BlockSpec(memory_space=pl.ANY),
                  pl.BlockSpec((COMB_T, D_MODEL), lambda i, p: (i, 0)),
                  pl.BlockSpec((COMB_T, LANES), lambda i, p: (i, 0)),
                  pl.BlockSpec((1, D_MODEL), lambda i, p: (0, 0)),
                  pl.BlockSpec((1, D_MODEL), lambda i, p: (0, 0))],
        out_specs=[pl.BlockSpec((COMB_T, D_MODEL), lambda i, p: (jnp.minimum(i, npb - 1), 0)),
                   pl.BlockSpec((COMB_T, D_MODEL), lambda i, p: (jnp.maximum(i - npb, 0), 0))],
        scratch_shapes=([pltpu.VMEM((COMB_T, PACKED_W), U32) for _ in range(2 * GATHER_SLOTS)]
                        + [pltpu.SemaphoreType.DMA((GATHER_SLOTS,))]),
    )
    return pl.pallas_call(
        functools.partial(_combine_body, n_prompt_blocks=npb),
        grid_spec=grid_spec,
        out_shape=[jax.ShapeDtypeStruct((n_prompt_rows, D_MODEL), F32),
                   jax.ShapeDtypeStruct((rows - n_prompt_rows, D_MODEL), F32)],
        compiler_params=_cparams(("arbitrary",), bounds_checks=False),
        name="combine_ln2",
    )(pos, ys, x1, route, g, b)


def kernel(x_prompt, x_sample, cache_k, cache_v, state_ssm_re, state_ssm_im, w_in, ssm_lambda_re, ssm_lambda_im, ssm_log_dt, ssm_b_re, ssm_b_im, ssm_c_re, ssm_c_im, ssm_d, ssm_w_glu, attn_sinks, attn_norm_g, ssm_norm_g, w_out, ln1_g, ln1_b, router_group_w, router_group_b, router_expert_w, router_expert_b, expert_w_gate, expert_w_up, expert_w_down, ln2_g, ln2_b):
    nbatch, seq, _ = x_prompt.shape
    nseq, dec_seq, _ = x_sample.shape
    wbuf = cache_k.shape[2]
    rows_p = nbatch * seq
    rows_s = nseq * dec_seq
    rows_all = rows_p + rows_s
    tm_p = 512

    xp = x_prompt.reshape(rows_p, D_MODEL)
    xs = x_sample.reshape(rows_s, D_MODEL)
    tab_p = _rope_tables(np.arange(seq))
    tab_s = _rope_tables(PAST_LEN + (np.arange(rows_s) % dec_seq))
    ck_all = cache_k.astype(F32).reshape(DEPTH, nseq, wbuf, KV_WIDTH)
    cv_all = cache_v.astype(F32).reshape(DEPTH, nseq, wbuf, KV_WIDTH)

    sp = _ssm_pack(jax.vmap(_ssm_params)(ssm_lambda_re, ssm_lambda_im, ssm_log_dt, ssm_b_re, ssm_b_im,
                                         ssm_c_re, ssm_c_im, ssm_d))
    w_in_bf = w_in.astype(BF16)
    w_glu_bf = ssm_w_glu.astype(BF16)
    w_out_bf = w_out.astype(BF16)

    outs = {k: [] for k in ("kp", "vp", "hrp", "hip", "ks", "vs", "hrs", "his")}
    for l in range(DEPTH):
        sinks = attn_sinks[l].astype(F32)
        attn_g = attn_norm_g[l].astype(F32).reshape(1, ATTN_WIDTH)
        ssm_g = ssm_norm_g[l].astype(F32).reshape(1, SSM_WIDTH)
        g1 = ln1_g[l].astype(F32).reshape(1, D_MODEL)
        b1 = ln1_b[l].astype(F32).reshape(1, D_MODEL)
        g2 = ln2_g[l].astype(F32).reshape(1, D_MODEL)
        b2 = ln2_b[l].astype(F32).reshape(1, D_MODEL)
        n_rt = N_EXPERT_GROUPS + N_EXPERTS
        wr = jnp.concatenate([router_group_w[l].astype(F32)]
                             + [router_expert_w[l, g].astype(F32) for g in range(N_EXPERT_GROUPS)]
                             + [jnp.zeros((D_MODEL, LANES - n_rt), F32)], axis=1)
        br = jnp.concatenate([router_group_b[l].astype(F32), router_expert_b[l].astype(F32).reshape(-1),
                              jnp.zeros((LANES - n_rt,), F32)]).reshape(1, LANES)

        q_p, k_p, v_p, u_p = _inproj(xp, w_in_bf, l, *tab_p, tm=tm_p)
        q_s, k_s, v_s, u_s = _inproj(xs, w_in_bf, l, *tab_s, tm=rows_s)

        an_p = _attn_prompt(sinks, q_p, k_p, v_p, attn_g, seq)
        an_s, nk_s, nv_s = _attn_sample(sinks, q_s, k_s, v_s, ck_all, cv_all, attn_g, dec_seq, l)

        y_p, hl_p = _ssm_prompt(u_p, sp, l, nbatch, seq)
        y_s, hl_s = _ssm_sample(u_s, _pack_state(state_ssm_re[l], state_ssm_im[l]), sp, l, nseq, dec_seq)

        x1, lg_p = _outproj(an_p, y_p, xp, w_out_bf, w_glu_bf, l, ssm_g, g1, b1, wr, br, tm_p, rows_all, 0)
        x1, lg_s = _outproj(an_s, y_s, xs, w_out_bf, w_glu_bf, l, ssm_g, g1, b1, wr, br, rows_s, rows_all,
                            rows_p // rows_s, x1_buf=x1)
        nblocks = 2 * rows_all // MOE_BLOCK + N_EXPERTS
        route, meta = _route_plan(jnp.concatenate([lg_p, lg_s], axis=0))
        pos = route[:, R_P0:R_P1 + 1].astype(jnp.int32).reshape(2 * rows_all)
        bend = meta[0, :N_EXPERTS].astype(jnp.int32)

        cnt = meta[1, :N_EXPERTS].astype(jnp.int32)
        ys = _moe(bend, cnt, pos, x1, expert_w_gate, expert_w_up, expert_w_down, l, nblocks)
        xp, xs = _combine(pos, ys, x1, route, g2, b2, rows_p)

        wp = min(WINDOW, seq)
        tail = lambda a: a.reshape(nbatch, seq, KV_WIDTH)[:, seq - wp:].reshape(nbatch, wp, N_KV_HEADS, HEAD_DIM)
        outs["kp"].append(tail(k_p))
        outs["vp"].append(tail(v_p))
        hp = hl_p.reshape(nbatch, SSM_CHUNKS * 2 * CH_STATES)
        hr, hi = _unpack_state(hp, nbatch)
        outs["hrp"].append(hr)
        outs["hip"].append(hi)
        outs["ks"].append(nk_s.reshape(nseq, wbuf, N_KV_HEADS, HEAD_DIM))
        outs["vs"].append(nv_s.reshape(nseq, wbuf, N_KV_HEADS, HEAD_DIM))
        hr, hi = _unpack_state(hl_s, nseq)
        outs["hrs"].append(hr)
        outs["his"].append(hi)

    st = {k: jnp.stack(v) for k, v in outs.items()}
    return (xp.reshape(nbatch, seq, D_MODEL), xs.reshape(nseq, dec_seq, D_MODEL),
            st["kp"], st["vp"], st["hrp"], st["hip"], st["ks"], st["vs"], st["hrs"], st["his"])
```
